```python
import jax, jax.numpy as jnp
from jax import lax
import numpy as np

D_MODEL = 1024
BATCH = 2
SEQ = 8192
DEPTH = 1
DEC_BATCH = 8
DEC_SEQ = 64
PAST_LEN = 2048

CHUNK = 64
SGU_CHUNK = 128
SGU_GROUPS = 8
SGU_WIDTH = 1024
SGU_GROUP_CH = SGU_WIDTH // SGU_GROUPS
N_HEADS = 16
N_KV_HEADS = 4
HEAD_DIM = 64
Q_PER_KV = N_HEADS // N_KV_HEADS
WINDOW = 128
WINDOW_CHUNKS = WINDOW // CHUNK
ROT_DIM = HEAD_DIM // 4
ROPE_THETA = 500000.0
ATT_W = N_HEADS * HEAD_DIM
KV_W = N_KV_HEADS * HEAD_DIM
N_EXPERTS = 32
TOP_K = 4
D_FF = 1024
SWIGLU_ALPHA = 1.702
SWIGLU_LIMIT = 7.0
MOE_BLOCK = 256
NORM_EPS = 1e-5
NEG_INF = -1e30
IN_SPLITS = (SGU_WIDTH, SGU_WIDTH, ATT_W, KV_W, KV_W, D_MODEL, D_MODEL)
N_IN = SGU_WIDTH * 2 + ATT_W + KV_W * 2 + D_MODEL * 2

kernel_name = 'streaming_sgu_swa_moe_hybrid_step'


def rms_norm(x, g):
    xf = x.astype(jnp.float32)
    y = xf * lax.rsqrt(jnp.mean(xf * xf, axis=-1, keepdims=True) + NORM_EPS)
    return (y * g.astype(jnp.float32)).astype(x.dtype)


def layer_norm(x, g, b):
    xf = x.astype(jnp.float32)
    mu = jnp.mean(xf, axis=-1, keepdims=True)
    xc = xf - mu
    y = xc * lax.rsqrt(jnp.mean(xc * xc, axis=-1, keepdims=True) + NORM_EPS)
    return (y * g.astype(jnp.float32) + b.astype(jnp.float32)).astype(x.dtype)


def rope_partial(x, pos):
    half = ROT_DIM // 2
    inv = ROPE_THETA ** (-jnp.arange(half, dtype=jnp.float32) * 2.0 / ROT_DIM)
    ang = pos.astype(jnp.float32)[:, None] * inv[None, :]
    cos = jnp.cos(ang)[:, None, :]
    sin = jnp.sin(ang)[:, None, :]
    xf = x.astype(jnp.float32)
    x1 = xf[..., :half]
    x2 = xf[..., half:ROT_DIM]
    out = jnp.concatenate([x1 * cos - x2 * sin, x2 * cos + x1 * sin, xf[..., ROT_DIM:]], axis=-1)
    return out.astype(x.dtype)


def mixer_inputs(x, g_mix, w_in, b_in, pos):
    b, l = x.shape[0], x.shape[1]
    h = rms_norm(x, g_mix)
    z = h @ w_in + b_in
    idx = np.cumsum(IN_SPLITS)[:-1].tolist()
    zu, zv, q, k, v, ga, gb = jnp.split(z, idx, axis=-1)
    q = rope_partial(q.reshape(b, l, N_HEADS, HEAD_DIM), pos)
    k = rope_partial(k.reshape(b, l, N_KV_HEADS, HEAD_DIM), pos)
    v = v.reshape(b, l, N_KV_HEADS, HEAD_DIM)
    return zu, zv, q, k, v, ga, gb


def sgu_matrix(w_sp):
    i = jnp.arange(SGU_CHUNK)
    mask = (i[None, :] // CHUNK) <= (i[:, None] // CHUNK)
    return jnp.where(mask[None], w_sp, 0)


def sgu_prompt(zu, zv, ln_g, ln_b, w_sp, b_sp):
    b, s, _ = zu.shape
    u = jax.nn.gelu(zu, approximate=False)
    v = layer_norm(jax.nn.gelu(zv, approximate=False), ln_g, ln_b)
    vb = v.reshape(b, s // SGU_CHUNK, SGU_CHUNK, SGU_GROUPS, SGU_GROUP_CH)
    sp = jnp.einsum('gij,bnjgc->bnigc', sgu_matrix(w_sp), vb) + b_sp.T[None, None, :, :, None]
    return u * sp.reshape(b, s, SGU_WIDTH)


def sgu_sample(zu, zv, ln_g, ln_b, w_sp, b_sp):
    b, l, _ = zu.shape
    u = jax.nn.gelu(zu, approximate=False)
    v = layer_norm(jax.nn.gelu(zv, approximate=False), ln_g, ln_b)
    wm = sgu_matrix(w_sp)[:, :l, :l]
    sp = jnp.einsum('gij,bjgc->bigc', wm, v.reshape(b, l, SGU_GROUPS, SGU_GROUP_CH)) + b_sp[:, :l].T[None, :, :, None]
    return u * sp.reshape(b, l, SGU_WIDTH), v


def sink_softmax(s, sinks):
    sk = sinks.astype(jnp.float32).reshape(N_KV_HEADS, Q_PER_KV, 1)
    m = jnp.maximum(jnp.max(s, axis=-1), sk)
    p = jnp.exp(s - m[..., None])
    return p / (jnp.sum(p, axis=-1, keepdims=True) + jnp.exp(sk - m)[..., None])


def swa_prompt(q, k, v, sinks):
    b, s = q.shape[0], q.shape[1]
    nc = s // CHUNK
    nb = WINDOW_CHUNKS
    qb = q.reshape(b, nc, CHUNK, N_KV_HEADS, Q_PER_KV, HEAD_DIM)
    pad = jnp.zeros((b, WINDOW, N_KV_HEADS, HEAD_DIM), k.dtype)
    kc = jnp.concatenate([pad, k], axis=1).reshape(b, nc + nb, CHUNK, N_KV_HEADS, HEAD_DIM)
    vc = jnp.concatenate([pad.astype(v.dtype), v], axis=1).reshape(b, nc + nb, CHUNK, N_KV_HEADS, HEAD_DIM)
    kb = jnp.concatenate([kc[:, j:j + nc] for j in range(nb + 1)], axis=2)
    vb = jnp.concatenate([vc[:, j:j + nc] for j in range(nb + 1)], axis=2)
    valid = (jnp.arange(nc)[:, None] + jnp.arange(nb + 1)[None, :]) >= nb
    valid = jnp.repeat(valid, CHUNK, axis=1)
    sc = jnp.einsum('bnqkgd,bnskd->bnkgqs', qb, kb).astype(jnp.float32) * (HEAD_DIM ** -0.5)
    sc = jnp.where(valid[None, :, None, None, None, :], sc, NEG_INF)
    p = sink_softmax(sc, sinks)
    o = jnp.einsum('bnkgqs,bnskd->bnqkgd', p.astype(vb.dtype), vb)
    return o.reshape(b, s, ATT_W)


def swa_sample(q, k_new, v_new, cache_k, cache_v, sinks):
    b, l = q.shape[0], q.shape[1]
    kk = jnp.concatenate([cache_k, k_new], axis=1)
    vv = jnp.concatenate([cache_v, v_new], axis=1)
    qg = q.reshape(b, l, N_KV_HEADS, Q_PER_KV, HEAD_DIM)
    sc = jnp.einsum('bqkgd,bskd->bkgqs', qg, kk).astype(jnp.float32) * (HEAD_DIM ** -0.5)
    p = sink_softmax(sc, sinks)
    o = jnp.einsum('bkgqs,bskd->bqkgd', p.astype(vv.dtype), vv)
    return o.reshape(b, l, ATT_W)


def merge_branches(a, o, ga, gb, w_pa, w_pb, w_o):
    m = jax.nn.sigmoid(ga) * (a @ w_pa) + jax.nn.sigmoid(gb) * (o @ w_pb)
    return m @ w_o


def expert_block(xb, e, w_gu, b_gu, w_dn, b_dn):
    gu = xb @ w_gu[e] + b_gu[e]
    gate = jnp.minimum(gu[:, :D_FF], SWIGLU_LIMIT)
    lin = jnp.clip(gu[:, D_FF:], -SWIGLU_LIMIT, SWIGLU_LIMIT)
    act = gate * jax.nn.sigmoid(SWIGLU_ALPHA * gate) * (lin + 1)
    return act @ w_dn[e] + b_dn[e]


def moe(h, w_router, b_router, w_gu, b_gu, w_dn, b_dn):
    t, d = h.shape
    n_assign = t * TOP_K
    logits = (h @ w_router + b_router).astype(jnp.float32)
    top_val, top_idx = lax.top_k(logits, TOP_K)
    gates = jax.nn.softmax(top_val, axis=-1)
    flat_e = top_idx.reshape(-1)
    order = jnp.argsort(flat_e)
    sorted_e = flat_e[order]
    tok = order // TOP_K
    counts = jnp.bincount(flat_e, length=N_EXPERTS)
    padded = (counts + MOE_BLOCK - 1) // MOE_BLOCK * MOE_BLOCK
    end_pad = jnp.cumsum(padded)
    start_pad = end_pad - padded
    start = jnp.cumsum(counts) - counts
    dest = start_pad[sorted_e] + jnp.arange(n_assign) - start[sorted_e]
    n_blocks = -(-(n_assign + N_EXPERTS * (MOE_BLOCK - 1)) // MOE_BLOCK)
    slot_tok = jnp.full((n_blocks * MOE_BLOCK,), t, jnp.int32).at[dest].set(tok.astype(jnp.int32))
    h_pad = jnp.concatenate([h, jnp.zeros((1, d), h.dtype)], axis=0)
    xb = h_pad[slot_tok].reshape(n_blocks, MOE_BLOCK, d)
    block_e = jnp.minimum(jnp.searchsorted(end_pad, jnp.arange(n_blocks) * MOE_BLOCK, side='right'), N_EXPERTS - 1)
    yb = lax.map(lambda xe: expert_block(xe[0], xe[1], w_gu, b_gu, w_dn, b_dn), (xb, block_e))
    y_slot = yb.reshape(-1, d)[dest]
    wgt = gates.reshape(-1)[order].astype(h.dtype)
    return jnp.zeros((t, d), h.dtype).at[tok].add(y_slot * wgt[:, None])


def channel_mix(x, g_ffn, w_router, b_router, w_gu, b_gu, w_dn, b_dn):
    b, l, d = x.shape
    h = rms_norm(x, g_ffn).reshape(b * l, d)
    return x + moe(h, w_router, b_router, w_gu, b_gu, w_dn, b_dn).reshape(b, l, d)


def setup_inputs(seed: int = 0) -> dict:
    key = jax.random.key(seed)
    ks = jax.random.split(key, 24)
    f32 = jnp.float32

    def nrm(k, shape, scale):
        return jax.random.normal(k, shape, f32) * scale

    d = D_MODEL
    kv_rows = min(WINDOW, PAST_LEN)
    return {
        'x_prompt': nrm(ks[0], (BATCH, SEQ, d), 1.0),
        'x_sample': nrm(ks[1], (DEC_BATCH, DEC_SEQ, d), 1.0),
        'cache_k': nrm(ks[2], (DEPTH, DEC_BATCH, kv_rows, N_KV_HEADS, HEAD_DIM), 1.0),
        'cache_v': nrm(ks[3], (DEPTH, DEC_BATCH, kv_rows, N_KV_HEADS, HEAD_DIM), 1.0),
        'g_mix': 1.0 + nrm(ks[4], (DEPTH, d), 0.02),
        'w_in': nrm(ks[5], (DEPTH, d, N_IN), d ** -0.5),
        'b_in': nrm(ks[6], (DEPTH, N_IN), 0.02),
        'ln_v_g': 1.0 + nrm(ks[7], (DEPTH, SGU_WIDTH), 0.02),
        'ln_v_b': nrm(ks[8], (DEPTH, SGU_WIDTH), 0.02),
        'w_sp': nrm(ks[9], (DEPTH, SGU_GROUPS, SGU_CHUNK, SGU_CHUNK), SGU_CHUNK ** -0.5),
        'b_sp': 1.0 + nrm(ks[10], (DEPTH, SGU_GROUPS, SGU_CHUNK), 0.02),
        'attn_sinks': nrm(ks[11], (DEPTH, N_HEADS), 1.0),
        'w_pa': nrm(ks[12], (DEPTH, SGU_WIDTH, d), SGU_WIDTH ** -0.5),
        'w_pb': nrm(ks[13], (DEPTH, ATT_W, d), ATT_W ** -0.5),
        'w_o': nrm(ks[14], (DEPTH, d, d), d ** -0.5),
        'g_ffn': 1.0 + nrm(ks[15], (DEPTH, d), 0.02),
        'w_router': nrm(ks[16], (DEPTH, d, N_EXPERTS), d ** -0.5),
        'b_router': nrm(ks[17], (DEPTH, N_EXPERTS), 0.01),
        'w_gu': nrm(ks[18], (DEPTH, N_EXPERTS, d, 2 * D_FF), d ** -0.5),
        'b_gu': nrm(ks[19], (DEPTH, N_EXPERTS, 2 * D_FF), 0.02),
        'w_dn': nrm(ks[20], (DEPTH, N_EXPERTS, D_FF, d), D_FF ** -0.5),
        'b_dn': nrm(ks[21], (DEPTH, N_EXPERTS, d), 0.02),
        'g_final': 1.0 + nrm(ks[22], (d,), 0.02),
    }


def reference(x_prompt, x_sample, cache_k, cache_v, g_mix, w_in, b_in, ln_v_g, ln_v_b, w_sp, b_sp,
              attn_sinks, w_pa, w_pb, w_o, g_ffn, w_router, b_router, w_gu, b_gu, w_dn, b_dn, g_final):
    seq = x_prompt.shape[1]
    n_new = x_sample.shape[1]
    pos_p = jnp.arange(seq, dtype=jnp.int32)
    pos_s = PAST_LEN + jnp.arange(n_new, dtype=jnp.int32)
    keep = min(WINDOW, seq)
    xp, xs = x_prompt, x_sample
    kp_l, vp_l, ks_l, vs_l, us_l = [], [], [], [], []
    for l in range(DEPTH):
        zu, zv, q, k, v, ga, gb = mixer_inputs(xp, g_mix[l], w_in[l], b_in[l], pos_p)
        a = sgu_prompt(zu, zv, ln_v_g[l], ln_v_b[l], w_sp[l], b_sp[l])
        o = swa_prompt(q, k, v, attn_sinks[l])
        xp = xp + merge_branches(a, o, ga, gb, w_pa[l], w_pb[l], w_o[l])
        xp = channel_mix(xp, g_ffn[l], w_router[l], b_router[l], w_gu[l], b_gu[l], w_dn[l], b_dn[l])
        kp_l.append(k[:, seq - keep:])
        vp_l.append(v[:, seq - keep:])
        zu, zv, q, k, v, ga, gb = mixer_inputs(xs, g_mix[l], w_in[l], b_in[l], pos_s)
        a, v_sgu = sgu_sample(zu, zv, ln_v_g[l], ln_v_b[l], w_sp[l], b_sp[l])
        o = swa_sample(q, k, v, cache_k[l], cache_v[l], attn_sinks[l])
        xs = xs + merge_branches(a, o, ga, gb, w_pa[l], w_pb[l], w_o[l])
        xs = channel_mix(xs, g_ffn[l], w_router[l], b_router[l], w_gu[l], b_gu[l], w_dn[l], b_dn[l])
        ks_l.append(k)
        vs_l.append(v)
        us_l.append(v_sgu)
    y_prompt = rms_norm(xp, g_final)
    y_sample = rms_norm(xs, g_final)
    return (y_prompt, y_sample, jnp.stack(kp_l), jnp.stack(vp_l), jnp.stack(ks_l), jnp.stack(vs_l), jnp.stack(us_l))
```

```python
import functools

import numpy as np
import jax
import jax.numpy as jnp
from jax import lax
from jax.experimental import pallas as pl
from jax.experimental.pallas import tpu as pltpu

D_MODEL = 1024
PAST_LEN = 2048
CHUNK = 64
SGU_CHUNK = 128
SGU_GROUPS = 8
SGU_WIDTH = 1024
N_HEADS = 16
N_KV_HEADS = 4
HEAD_DIM = 64
Q_PER_KV = N_HEADS // N_KV_HEADS
WINDOW = 128
ROT_DIM = HEAD_DIM // 4
ROPE_THETA = 500000.0
ATT_W = N_HEADS * HEAD_DIM
KV_W = N_KV_HEADS * HEAD_DIM
N_EXPERTS = 32
TOP_K = 4
D_FF = 1024
SWIGLU_ALPHA = 1.702
SWIGLU_LIMIT = 7.0
NORM_EPS = 1e-5
NEG_INF = -1e30
N_IN = SGU_WIDTH * 2 + ATT_W + KV_W * 2 + D_MODEL * 2

LANES = 128
ROW_TILE = 256
MOE_TILE = 256
KV_DUP_W = N_KV_HEADS * LANES
KEY_SPAN = WINDOW + CHUNK
VMEM_LIMIT = 56 * 1024 * 1024

_SQRT_HALF = 0.7071067811865476


def _erf(x):
    x = jnp.clip(x, -4.0, 4.0)
    x2 = x * x
    a = x2 * (-2.72614225801306e-10) + 2.77068142495902e-08
    a = a * x2 + (-2.10102402082508e-06)
    a = a * x2 + (-5.69250639462346e-05)
    a = a * x2 + (-7.34990630326855e-04)
    a = a * x2 + (-2.95459980854025e-03)
    a = a * x2 + (-1.60960333262415e-02)
    a = a * x
    b = x2 * (-1.45660718464996e-05) + (-2.13374055278905e-04)
    b = b * x2 + (-1.68282697438203e-03)
    b = b * x2 + (-7.37332916720468e-03)
    b = b * x2 + (-1.42647390514189e-02)
    return a / b


def _gelu(x):
    return 0.5 * x * (1.0 + _erf(x * _SQRT_HALF))


def _sigmoid(x):
    return 1.0 / (1.0 + jnp.exp(-x))


def _bf16(x):
    return x.astype(jnp.bfloat16)


def _dot(a, b):
    return jnp.dot(a, b, preferred_element_type=jnp.float32)


def _dot_nt(a, b):
    return lax.dot_general(a, b, (((1,), (1,)), ((), ())), preferred_element_type=jnp.float32)


def _rms(x, g):
    return x * lax.rsqrt(jnp.mean(x * x, axis=-1, keepdims=True) + NORM_EPS) * g


def _lane_lo(rows):
    return lax.broadcasted_iota(jnp.int32, (rows, LANES), 1) < HEAD_DIM


def _dup_heads(kv):
    rows = kv.shape[0]
    lo = _lane_lo(rows)
    out = []
    for j in range(KV_W // LANES):
        blk = kv[:, j * LANES:(j + 1) * LANES]
        swp = pltpu.roll(blk, HEAD_DIM, axis=1)
        out.append(jnp.where(lo, blk, swp))
        out.append(jnp.where(lo, swp, blk))
    return _bf16(jnp.concatenate(out, axis=1))


def _rope_block(zb, cos_b, sin_lo, sin_hi):
    up = pltpu.roll(zb, LANES - ROT_DIM // 2, axis=1)
    dn = pltpu.roll(zb, ROT_DIM // 2, axis=1)
    return zb * cos_b + up * sin_lo + dn * sin_hi


def _proj_kernel(n_prompt_tiles, xp_ref, xs_ref, gmix_ref, w_ref, b_ref, lng_ref, lnb_ref,
                 cos_ref, slo_ref, shi_ref,
                 u_ref, vln_ref, vs_ref, q_ref, k_ref, v_ref, kd_ref, vd_ref, ga_ref, gb_ref):
    i = pl.program_id(0)
    x = jnp.where(i < n_prompt_tiles, xp_ref[...], xs_ref[...])
    h = _bf16(_rms(x, gmix_ref[...]))

    def seg(lo, width):
        return _dot(h, w_ref[:, lo:lo + width]) + b_ref[:, lo:lo + width]

    o = 0
    u_ref[...] = _bf16(_gelu(seg(o, SGU_WIDTH)))
    o += SGU_WIDTH
    gv = _gelu(seg(o, SGU_WIDTH))
    gc = gv - jnp.mean(gv, axis=-1, keepdims=True)
    var = jnp.mean(gc * gc, axis=-1, keepdims=True)
    vln = gc * lax.rsqrt(var + NORM_EPS) * lng_ref[...] + lnb_ref[...]
    vln_ref[...] = _bf16(vln)

    @pl.when(i >= n_prompt_tiles)
    def _():
        vs_ref[...] = vln

    o += SGU_WIDTH
    cos_b, sin_lo, sin_hi = cos_ref[...], slo_ref[...], shi_ref[...]
    zq = seg(o, ATT_W)
    for j in range(ATT_W // LANES):
        blk = _rope_block(zq[:, j * LANES:(j + 1) * LANES], cos_b, sin_lo, sin_hi)
        q_ref[:, j * LANES:(j + 1) * LANES] = _bf16(blk * (HEAD_DIM ** -0.5))
    o += ATT_W
    zk = seg(o, KV_W)
    kr = jnp.concatenate(
        [_rope_block(zk[:, j * LANES:(j + 1) * LANES], cos_b, sin_lo, sin_hi)
         for j in range(KV_W // LANES)], axis=1)
    k_ref[...] = kr
    kd_ref[...] = _dup_heads(kr)
    o += KV_W
    zv = seg(o, KV_W)
    v_ref[...] = zv
    vd_ref[...] = _dup_heads(zv)
    o += KV_W
    ga_ref[...] = _bf16(_sigmoid(seg(o, D_MODEL)))
    o += D_MODEL
    gb_ref[...] = _bf16(_sigmoid(seg(o, D_MODEL)))


def _rope_tables(pos):
    half = ROT_DIM // 2
    inv = ROPE_THETA ** (-jnp.arange(half, dtype=jnp.float32) * 2.0 / ROT_DIM)
    ang = pos.astype(jnp.float32)[:, None] * inv[None, :]
    cos, sin = jnp.cos(ang), jnp.sin(ang)
    n = pos.shape[0]
    ones = jnp.ones((n, HEAD_DIM - ROT_DIM), jnp.float32)
    zeros = jnp.zeros((n, HEAD_DIM - ROT_DIM), jnp.float32)
    zh = jnp.zeros((n, half), jnp.float32)
    cos_h = jnp.concatenate([cos, cos, ones], axis=1)
    slo_h = jnp.concatenate([-sin, zh, zeros], axis=1)
    shi_h = jnp.concatenate([zh, sin, zeros], axis=1)
    rep = LANES // HEAD_DIM
    return (jnp.tile(cos_h, (1, rep)), jnp.tile(slo_h, (1, rep)), jnp.tile(shi_h, (1, rep)))


def _row_spec(width):
    return pl.BlockSpec((ROW_TILE, width), lambda i: (i, 0))


def _const_spec(shape):
    return pl.BlockSpec(shape, lambda i: (0,) * len(shape))


def _prompt_spec(width, n_prompt_tiles):
    return pl.BlockSpec((ROW_TILE, width), lambda i: (jnp.minimum(i, n_prompt_tiles - 1), 0))


def _sample_spec(width, n_prompt_tiles):
    return pl.BlockSpec((ROW_TILE, width), lambda i: (jnp.maximum(i - n_prompt_tiles, 0), 0))


def _params():
    return pltpu.CompilerParams(dimension_semantics=("arbitrary",), vmem_limit_bytes=VMEM_LIMIT)


def _project(xp, xs, g_mix, w_in, b_in, ln_g, ln_b, tables):
    tp, ts = xp.shape[0], xs.shape[0]
    t = tp + ts
    npt = tp // ROW_TILE
    f32, bf16 = jnp.float32, jnp.bfloat16
    out_shape = (
        jax.ShapeDtypeStruct((t, SGU_WIDTH), bf16),
        jax.ShapeDtypeStruct((t, SGU_WIDTH), bf16),
        jax.ShapeDtypeStruct((ts, SGU_WIDTH), f32),
        jax.ShapeDtypeStruct((t, ATT_W), bf16),
        jax.ShapeDtypeStruct((t, KV_W), f32),
        jax.ShapeDtypeStruct((t, KV_W), f32),
        jax.ShapeDtypeStruct((t, KV_DUP_W), bf16),
        jax.ShapeDtypeStruct((t, KV_DUP_W), bf16),
        jax.ShapeDtypeStruct((t, D_MODEL), bf16),
        jax.ShapeDtypeStruct((t, D_MODEL), bf16),
    )
    return pl.pallas_call(
        functools.partial(_proj_kernel, npt),
        out_shape=out_shape,
        grid=(t // ROW_TILE,),
        in_specs=[
            _prompt_spec(D_MODEL, npt), _sample_spec(D_MODEL, npt),
            _const_spec((1, D_MODEL)), _const_spec((D_MODEL, N_IN)),
            _const_spec((1, N_IN)), _const_spec((1, SGU_WIDTH)), _const_spec((1, SGU_WIDTH)),
            _row_spec(LANES), _row_spec(LANES), _row_spec(LANES),
        ],
        out_specs=(
            _row_spec(SGU_WIDTH), _row_spec(SGU_WIDTH), _sample_spec(SGU_WIDTH, npt),
            _row_spec(ATT_W), _row_spec(KV_W), _row_spec(KV_W), _row_spec(KV_DUP_W),
            _row_spec(KV_DUP_W), _row_spec(D_MODEL), _row_spec(D_MODEL),
        ),
        compiler_params=_params(),
        name="proj",
    )(xp, xs, g_mix.reshape(1, -1), w_in.astype(bf16), b_in.reshape(1, -1),
      ln_g.reshape(1, -1), ln_b.reshape(1, -1), *tables)


def _attend(qa, qb, kwin, vwin, sink, valid):
    lo = _lane_lo(CHUNK)
    zero = jnp.zeros_like(qa)
    lhs = jnp.concatenate([jnp.where(lo, qa, zero), jnp.where(lo, zero, qa),
                           jnp.where(lo, qb, zero), jnp.where(lo, zero, qb)], axis=0)
    s = _dot_nt(lhs, kwin)
    if valid is not None:
        s = jnp.where(valid, s, NEG_INF)
    m = jnp.maximum(jnp.max(s, axis=-1, keepdims=True), sink)
    p = jnp.exp(s - m)
    denom = jnp.sum(p, axis=-1, keepdims=True) + jnp.exp(sink - m)
    r = _dot(_bf16(p / denom), vwin)
    oa = jnp.where(lo, r[0:CHUNK], r[CHUNK:2 * CHUNK])
    ob = jnp.where(lo, r[2 * CHUNK:3 * CHUNK], r[3 * CHUNK:4 * CHUNK])
    return oa, ob


def _top4(logits):
    rows = logits.shape[0]
    lane = lax.broadcasted_iota(jnp.int32, (rows, N_EXPERTS), 1)
    work = logits
    vals, idxs = [], []
    for _ in range(TOP_K):
        m = jnp.max(work, axis=-1, keepdims=True)
        idx = jnp.min(jnp.where(work == m, lane, N_EXPERTS), axis=-1, keepdims=True)
        vals.append(m)
        idxs.append(idx)
        work = jnp.where(lane == idx, -jnp.inf, work)
    exps = [jnp.exp(v - vals[0]) for v in vals]
    denom = exps[0] + exps[1] + exps[2] + exps[3]
    wide = lax.broadcasted_iota(jnp.int32, (rows, LANES), 1)
    topi = jnp.zeros((rows, LANES), jnp.int32)
    topg = jnp.zeros((rows, LANES), jnp.float32)
    for k in range(TOP_K):
        topi = jnp.where(wide == k, idxs[k], topi)
        topg = jnp.where(wide == k, exps[k] / denom, topg)
    return topi, topg


def _mix_kernel(tiles_per_seq, n_prompt_tiles,
                xp_ref, xs_ref, u_ref, vln_ref, q_ref, kd_ref, vd_ref, kdp_ref, vdp_ref,
                ck_ref, cv_ref, ga_ref, gb_ref, wsp_ref, bsp_ref, sink_ref,
                wpa_ref, wpb_ref, wo_ref, gffn_ref, wrh_ref, wrl_ref, br_ref,
                x1_ref, h2_ref, topi_ref, topg_ref,
                a_s, o_s, kwin_s, vwin_s):
    i = pl.program_id(0)
    n_streams = ROW_TILE // CHUNK

    def sgu_rows(r0, rows):
        ri = lax.broadcasted_iota(jnp.int32, (rows, rows), 0) // CHUNK
        ci = lax.broadcasted_iota(jnp.int32, (rows, rows), 1) // CHUNK
        for g in range(SGU_GROUPS):
            cols = slice(g * LANES, (g + 1) * LANES)
            w = _bf16(jnp.where(ci <= ri, wsp_ref[g, :rows, :rows], 0.0))
            sp = _dot(w, vln_ref[r0:r0 + rows, cols]) + bsp_ref[:rows, g:g + 1]
            a_s[r0:r0 + rows, cols] = _bf16(u_ref[r0:r0 + rows, cols].astype(jnp.float32) * sp)

    def attend_rows(r0, kwin_of, valid):
        for g in range(N_KV_HEADS):
            c0 = g * Q_PER_KV * HEAD_DIM
            kwin, vwin = kwin_of(g)
            oa, ob = _attend(q_ref[r0:r0 + CHUNK, c0:c0 + LANES],
                             q_ref[r0:r0 + CHUNK, c0 + LANES:c0 + 2 * LANES],
                             kwin, vwin, sink_ref[g][:, 0:1], valid)
            o_s[r0:r0 + CHUNK, c0:c0 + LANES] = _bf16(oa)
            o_s[r0:r0 + CHUNK, c0 + LANES:c0 + 2 * LANES] = _bf16(ob)

    @pl.when(i < n_prompt_tiles)
    def _prompt():
        for c in range(ROW_TILE // SGU_CHUNK):
            sgu_rows(c * SGU_CHUNK, SGU_CHUNK)
        kwin_s[0:WINDOW] = kdp_ref[...]
        kwin_s[WINDOW:WINDOW + ROW_TILE] = kd_ref[...]
        vwin_s[0:WINDOW] = vdp_ref[...]
        vwin_s[WINDOW:WINDOW + ROW_TILE] = vd_ref[...]
        first = (i % tiles_per_seq) == 0
        col = lax.broadcasted_iota(jnp.int32, (1, KEY_SPAN), 1)
        for j in range(ROW_TILE // CHUNK):
            r0 = j * CHUNK
            valid = jnp.logical_or(jnp.logical_not(first), col + r0 >= WINDOW) if r0 < WINDOW else None

            def kwin_of(g, r0=r0):
                cols = slice(g * LANES, (g + 1) * LANES)
                return kwin_s[r0:r0 + KEY_SPAN, cols], vwin_s[r0:r0 + KEY_SPAN, cols]

            attend_rows(r0, kwin_of, valid)

    @pl.when(i >= n_prompt_tiles)
    def _sample():
        for s in range(n_streams):
            r0 = s * CHUNK
            sgu_rows(r0, CHUNK)
            kwin_s[0:WINDOW] = _dup_heads(ck_ref[s])
            kwin_s[WINDOW:KEY_SPAN] = kd_ref[r0:r0 + CHUNK]
            vwin_s[0:WINDOW] = _dup_heads(cv_ref[s])
            vwin_s[WINDOW:KEY_SPAN] = vd_ref[r0:r0 + CHUNK]

            def kwin_of(g):
                cols = slice(g * LANES, (g + 1) * LANES)
                return kwin_s[0:KEY_SPAN, cols], vwin_s[0:KEY_SPAN, cols]

            attend_rows(r0, kwin_of, None)

    x = jnp.where(i < n_prompt_tiles, xp_ref[...], xs_ref[...])
    m = (ga_ref[...].astype(jnp.float32) * _dot(a_s[...], wpa_ref[...])
         + gb_ref[...].astype(jnp.float32) * _dot(o_s[...], wpb_ref[...]))
    x1 = x + _dot(_bf16(m), wo_ref[...])
    x1_ref[...] = x1
    h2 = _rms(x1, gffn_ref[...])
    h2_ref[...] = h2
    hh = _bf16(h2)
    hl = _bf16(h2 - hh.astype(jnp.float32))
    logits = (_dot(hh, wrh_ref[...]) + _dot(hh, wrl_ref[...]) + _dot(hl, wrh_ref[...])) + br_ref[...]
    topi, topg = _top4(logits)
    topi_ref[...] = topi
    topg_ref[...] = topg


def _mix(xp, xs, u, vln, q, kd, vd, cache_k, cache_v, ga, gb, w_sp, b_sp, sinks,
         w_pa, w_pb, w_o, g_ffn, w_router, b_router, seq):
    tp, ts = xp.shape[0], xs.shape[0]
    t = tp + ts
    npt = tp // ROW_TILE
    tiles_per_seq = seq // ROW_TILE
    f32, bf16 = jnp.float32, jnp.bfloat16
    n_streams = ROW_TILE // CHUNK
    win_per_tile = ROW_TILE // WINDOW

    prev_spec = pl.BlockSpec(
        (WINDOW, KV_DUP_W), lambda i: (jnp.maximum(jnp.minimum(i, npt - 1) * win_per_tile - 1, 0), 0))
    cache_spec = pl.BlockSpec(
        (n_streams, WINDOW, KV_W), lambda i: (jnp.maximum(i - npt, 0), 0, 0))
    sink_cols = jnp.broadcast_to(
        jnp.repeat(sinks.astype(f32).reshape(N_KV_HEADS, Q_PER_KV), CHUNK, axis=1)[:, :, None],
        (N_KV_HEADS, Q_PER_KV * CHUNK, LANES))
    wr_hi = w_router.astype(bf16)
    wr_lo = (w_router - wr_hi.astype(f32)).astype(bf16)
    out_shape = (
        jax.ShapeDtypeStruct((t, D_MODEL), f32),
        jax.ShapeDtypeStruct((t, D_MODEL), f32),
        jax.ShapeDtypeStruct((t, LANES), jnp.int32),
        jax.ShapeDtypeStruct((t, LANES), f32),
    )
    return pl.pallas_call(
        functools.partial(_mix_kernel, tiles_per_seq, npt),
        out_shape=out_shape,
        grid=(t // ROW_TILE,),
        in_specs=[
            _prompt_spec(D_MODEL, npt), _sample_spec(D_MODEL, npt),
            _row_spec(SGU_WIDTH), _row_spec(SGU_WIDTH), _row_spec(ATT_W),
            _row_spec(KV_DUP_W), _row_spec(KV_DUP_W), prev_spec, prev_spec,
            cache_spec, cache_spec, _row_spec(D_MODEL), _row_spec(D_MODEL),
            _const_spec((SGU_GROUPS, SGU_CHUNK, SGU_CHUNK)), _const_spec((SGU_CHUNK, SGU_GROUPS)),
            _const_spec((N_KV_HEADS, Q_PER_KV * CHUNK, LANES)),
            _const_spec((SGU_WIDTH, D_MODEL)), _const_spec((ATT_W, D_MODEL)),
            _const_spec((D_MODEL, D_MODEL)), _const_spec((1, D_MODEL)),
            _const_spec((D_MODEL, N_EXPERTS)), _const_spec((D_MODEL, N_EXPERTS)),
            _const_spec((1, N_EXPERTS)),
        ],
        out_specs=(_row_spec(D_MODEL), _row_spec(D_MODEL), _row_spec(LANES), _row_spec(LANES)),
        scratch_shapes=[
            pltpu.VMEM((ROW_TILE, SGU_WIDTH), bf16), pltpu.VMEM((ROW_TILE, ATT_W), bf16),
            pltpu.VMEM((WINDOW + ROW_TILE, KV_DUP_W), bf16),
            pltpu.VMEM((WINDOW + ROW_TILE, KV_DUP_W), bf16),
        ],
        compiler_params=_params(),
        name="mix",
    )(xp, xs, u, vln, q, kd, vd, kd, vd,
      cache_k.reshape(-1, WINDOW, KV_W), cache_v.reshape(-1, WINDOW, KV_W), ga, gb,
      w_sp, b_sp.T, sink_cols, w_pa.astype(bf16), w_pb.astype(bf16), w_o.astype(bf16),
      g_ffn.reshape(1, -1), wr_hi, wr_lo, b_router.reshape(1, -1))


def _rank_kernel(topi_ref, rank_ref, count_ref, carry_s):
    i = pl.program_id(0)

    @pl.when(i == 0)
    def _():
        carry_s[...] = jnp.zeros_like(carry_s)

    topi = topi_ref[...]
    lane = lax.broadcasted_iota(jnp.int32, (ROW_TILE, LANES), 1)
    onehot = jnp.zeros((ROW_TILE, LANES), jnp.float32)
    for k in range(TOP_K):
        onehot = jnp.where(lane == topi[:, k:k + 1], 1.0, onehot)
    r = lax.broadcasted_iota(jnp.int32, (ROW_TILE, ROW_TILE), 0)
    c = lax.broadcasted_iota(jnp.int32, (ROW_TILE, ROW_TILE), 1)
    below = _bf16(jnp.where(c < r, 1.0, 0.0))
    rank = _dot(below, _bf16(onehot)) + carry_s[0:1, :]
    out = jnp.zeros((ROW_TILE, LANES), jnp.float32)
    for k in range(TOP_K):
        sel = jnp.sum(jnp.where(lane == topi[:, k:k + 1], rank, 0.0), axis=-1, keepdims=True)
        out = jnp.where(lane == k, sel, out)
    rank_ref[...] = out.astype(jnp.int32)
    carry_s[...] = carry_s[...] + jnp.sum(onehot, axis=0, keepdims=True)
    count_ref[...] = carry_s[...].astype(jnp.int32)


def _rank(topi):
    t = topi.shape[0]
    return pl.pallas_call(
        _rank_kernel,
        out_shape=(jax.ShapeDtypeStruct((t, LANES), jnp.int32),
                   jax.ShapeDtypeStruct((8, LANES), jnp.int32)),
        grid=(t // ROW_TILE,),
        in_specs=[_row_spec(LANES)],
        out_specs=(_row_spec(LANES), _const_spec((8, LANES))),
        scratch_shapes=[pltpu.VMEM((8, LANES), jnp.float32)],
        compiler_params=_params(),
        name="rank",
    )(topi)


def _expert_kernel(te_ref, nu_ref, nv_ref,
                   src_ref, src_next_ref, dst_ref, h2_hbm, wgu_ref, bgu_ref, wdn_ref, bdn_ref,
                   y_hbm,
                   xbuf, ybuf, wgu_s, wdn_s, gsem, ssem):
    i = pl.program_id(0)
    n_used = nu_ref[0]
    slot = i % 2

    def gather_copy(idx_ref, r, buf_slot):
        return pltpu.make_async_copy(h2_hbm.at[pl.ds(idx_ref[0, 0, r], 1)],
                                     xbuf.at[buf_slot, pl.ds(r, 1)], gsem.at[buf_slot])

    def scatter_copy(r, buf_slot):
        return pltpu.make_async_copy(ybuf.at[buf_slot, pl.ds(r, 1)],
                                     y_hbm.at[pl.ds(dst_ref[0, 0, r], 1)], ssem.at[buf_slot])

    def start_gather(idx_ref, buf_slot):
        def body(r, carry):
            gather_copy(idx_ref, r, buf_slot).start()
            return carry
        lax.fori_loop(0, MOE_TILE, body, 0)

    def wait_rows(copy_of, n_rows):
        def body(r, carry):
            copy_of(r).wait()
            return carry
        lax.fori_loop(0, n_rows, body, 0)

    @pl.when(jnp.logical_and(i == 0, n_used > 0))
    def _():
        start_gather(src_ref, 0)

    @pl.when(i + 1 < n_used)
    def _():
        start_gather(src_next_ref, 1 - slot)

    expert_changed = jnp.logical_or(i == 0, te_ref[i] != te_ref[jnp.maximum(i - 1, 0)])

    @pl.when(jnp.logical_and(i < n_used, expert_changed))
    def _():
        wgu_s[...] = _bf16(wgu_ref[0])
        wdn_s[...] = _bf16(wdn_ref[0])

    @pl.when(i < n_used)
    def _():
        wait_rows(lambda r: gather_copy(src_ref, r, slot), MOE_TILE)
        x = _bf16(xbuf[slot])
        gu = _dot(x, wgu_s[...]) + bgu_ref[0]
        gate = jnp.minimum(gu[:, :D_FF], SWIGLU_LIMIT)
        lin = jnp.clip(gu[:, D_FF:], -SWIGLU_LIMIT, SWIGLU_LIMIT)
        act = gate * _sigmoid(SWIGLU_ALPHA * gate) * (lin + 1.0)
        ybuf[slot] = _dot(_bf16(act), wdn_s[...]) + bdn_ref[0]

        def body(r, carry):
            scatter_copy(r, slot).start()
            return carry
        lax.fori_loop(0, nv_ref[i], body, 0)

    @pl.when(jnp.logical_and(i >= 1, i <= n_used))
    def _():
        wait_rows(lambda r: scatter_copy(r, 1 - slot), nv_ref[jnp.maximum(i - 1, 0)])


def _experts(tile_expert, n_used, n_valid, slot_src, slot_dst, h2, w_gu, b_gu, w_dn, b_dn, n_out_rows):
    n_tiles = slot_src.shape[0]
    t = h2.shape[0]
    f32, bf16 = jnp.float32, jnp.bfloat16
    idx_spec = pl.BlockSpec((1, 1, MOE_TILE), lambda i, te, nu, nv: (i, 0, 0), memory_space=pltpu.SMEM)
    idx_next_spec = pl.BlockSpec(
        (1, 1, MOE_TILE), lambda i, te, nu, nv: (jnp.minimum(i + 1, n_tiles - 1), 0, 0),
        memory_space=pltpu.SMEM)
    grid_spec = pltpu.PrefetchScalarGridSpec(
        num_scalar_prefetch=3,
        grid=(n_tiles,),
        in_specs=[
            idx_spec, idx_next_spec, idx_spec,
            pl.BlockSpec(memory_space=pl.ANY),
            pl.BlockSpec((1, D_MODEL, 2 * D_FF), lambda i, te, nu, nv: (te[i], 0, 0)),
            pl.BlockSpec((1, 1, 2 * D_FF), lambda i, te, nu, nv: (te[i], 0, 0)),
            pl.BlockSpec((1, D_FF, D_MODEL), lambda i, te, nu, nv: (te[i], 0, 0)),
            pl.BlockSpec((1, 1, D_MODEL), lambda i, te, nu, nv: (te[i], 0, 0)),
        ],
        out_specs=pl.BlockSpec(memory_space=pl.ANY),
        scratch_shapes=[
            pltpu.VMEM((2, MOE_TILE, D_MODEL), f32), pltpu.VMEM((2, MOE_TILE, D_MODEL), f32),
            pltpu.VMEM((D_MODEL, 2 * D_FF), bf16), pltpu.VMEM((D_FF, D_MODEL), bf16),
            pltpu.SemaphoreType.DMA((2,)), pltpu.SemaphoreType.DMA((2,)),
        ],
    )
    return pl.pallas_call(
        _expert_kernel,
        out_shape=jax.ShapeDtypeStruct((n_out_rows, D_MODEL), f32),
        grid_spec=grid_spec,
        compiler_params=_params(),
        name="experts",
    )(tile_expert, n_used, n_valid, slot_src, slot_src, slot_dst, h2, w_gu,
      b_gu.reshape(N_EXPERTS, 1, -1), w_dn, b_dn.reshape(N_EXPERTS, 1, -1))


def _combine_kernel(n_prompt_tiles, x1_ref, y0_ref, y1_ref, y2_ref, y3_ref, g_ref, gfin_ref,
                    yp_ref, ys_ref):
    i = pl.program_id(0)
    g = g_ref[...]
    acc = x1_ref[...]
    for k, y_ref in enumerate((y0_ref, y1_ref, y2_ref, y3_ref)):
        acc = acc + g[:, k:k + 1] * y_ref[...]
    out = _rms(acc, gfin_ref[...])

    @pl.when(i < n_prompt_tiles)
    def _():
        yp_ref[...] = out

    @pl.when(i >= n_prompt_tiles)
    def _():
        ys_ref[...] = out


def _combine(x1, y_tk, topg, g_final, tp):
    t = x1.shape[0]
    npt = tp // ROW_TILE
    nt = t // ROW_TILE
    f32 = jnp.float32
    y_specs = [pl.BlockSpec((ROW_TILE, D_MODEL), lambda i, k=k: (k * nt + i, 0)) for k in range(TOP_K)]
    return pl.pallas_call(
        functools.partial(_combine_kernel, npt),
        out_shape=(jax.ShapeDtypeStruct((tp, D_MODEL), f32),
                   jax.ShapeDtypeStruct((t - tp, D_MODEL), f32)),
        grid=(nt,),
        in_specs=[_row_spec(D_MODEL)] + y_specs + [_row_spec(LANES), _const_spec((1, D_MODEL))],
        out_specs=(_prompt_spec(D_MODEL, npt), _sample_spec(D_MODEL, npt)),
        compiler_params=_params(),
        name="combine",
    )(x1, y_tk, y_tk, y_tk, y_tk, topg, g_final.reshape(1, -1))


def _plan(topi, rank, counts, t):
    n_assign = t * TOP_K
    n_tiles = (n_assign + N_EXPERTS * (MOE_TILE - 1)) // MOE_TILE + 1
    counts = counts[0, :N_EXPERTS]
    tiles_e = (counts + MOE_TILE - 1) // MOE_TILE
    tile_end = jnp.cumsum(tiles_e)
    start_pad = (tile_end - tiles_e) * MOE_TILE
    n_used = tile_end[-1]
    idx = topi[:, :TOP_K]
    dest = start_pad[idx] + rank[:, :TOP_K]
    tok = jnp.broadcast_to(jnp.arange(t, dtype=jnp.int32)[:, None], (t, TOP_K))
    out_row = jnp.arange(TOP_K, dtype=jnp.int32)[None, :] * t + tok
    n_slots = n_tiles * MOE_TILE
    slot_src = jnp.zeros((n_slots,), jnp.int32).at[dest.reshape(-1)].set(tok.reshape(-1))
    slot_dst = jnp.zeros((n_slots,), jnp.int32).at[dest.reshape(-1)].set(out_row.reshape(-1))
    tile_ids = jnp.arange(n_tiles, dtype=jnp.int32)
    tile_expert = jnp.searchsorted(tile_end, jnp.minimum(tile_ids, n_used - 1), side='right')
    tile_expert = jnp.minimum(tile_expert, N_EXPERTS - 1).astype(jnp.int32)
    first_tile = (tile_end - tiles_e)[tile_expert]
    n_valid = jnp.clip(counts[tile_expert] - (tile_ids - first_tile) * MOE_TILE, 0, MOE_TILE)
    n_valid = jnp.where(tile_ids < n_used, n_valid, 0).astype(jnp.int32)
    return (tile_expert, n_used.reshape(1).astype(jnp.int32), n_valid,
            slot_src.reshape(n_tiles, 1, MOE_TILE), slot_dst.reshape(n_tiles, 1, MOE_TILE))


def kernel(x_prompt, x_sample, cache_k, cache_v, g_mix, w_in, b_in, ln_v_g, ln_v_b, w_sp, b_sp,
           attn_sinks, w_pa, w_pb, w_o, g_ffn, w_router, b_router, w_gu, b_gu, w_dn, b_dn, g_final):
    nb, seq, d = x_prompt.shape
    nsb, nnew, _ = x_sample.shape
    tp, ts = nb * seq, nsb * nnew
    t = tp + ts
    xp = x_prompt.reshape(tp, d)
    xs = x_sample.reshape(ts, d)
    pos = jnp.concatenate([jnp.tile(jnp.arange(seq, dtype=jnp.int32), nb),
                           jnp.tile(PAST_LEN + jnp.arange(nnew, dtype=jnp.int32), nsb)])
    tables = _rope_tables(pos)
    u, vln, v_sgu, q, k, v, kd, vd, ga, gb = _project(
        xp, xs, g_mix[0], w_in[0], b_in[0], ln_v_g[0], ln_v_b[0], tables)
    x1, h2, topi, topg = _mix(
        xp, xs, u, vln, q, kd, vd, cache_k[0], cache_v[0], ga, gb, w_sp[0], b_sp[0], attn_sinks[0],
        w_pa[0], w_pb[0], w_o[0], g_ffn[0], w_router[0], b_router[0], seq)
    rank, counts = _rank(topi)
    tile_expert, n_used, n_valid, slot_src, slot_dst = _plan(topi, rank, counts, t)
    y_tk = _experts(tile_expert, n_used, n_valid, slot_src, slot_dst, h2, w_gu[0], b_gu[0], w_dn[0],
                    b_dn[0], t * TOP_K)
    y_p, y_s = _combine(x1, y_tk, topg, g_final, tp)

    keep = min(WINDOW, seq)
    kp = k[:tp].reshape(nb, seq, N_KV_HEADS, HEAD_DIM)[:, seq - keep:]
    vp = v[:tp].reshape(nb, seq, N_KV_HEADS, HEAD_DIM)[:, seq - keep:]
    ks = k[tp:].reshape(nsb, nnew, N_KV_HEADS, HEAD_DIM)
    vs = v[tp:].reshape(nsb, nnew, N_KV_HEADS, HEAD_DIM)
    return (y_p.reshape(nb, seq, d), y_s.reshape(nsb, nnew, d), kp[None], vp[None], ks[None],
            vs[None], v_sgu.reshape(1, nsb, nnew, SGU_WIDTH))
```

```python
import functools

import numpy as np
import jax
import jax.numpy as jnp
from jax import lax
from jax.experimental import pallas as pl
from jax.experimental.pallas import tpu as pltpu

D_MODEL = 1024
PAST_LEN = 2048
CHUNK = 64
SGU_CHUNK = 128
SGU_GROUPS = 8
SGU_WIDTH = 1024
N_HEADS = 16
N_KV_HEADS = 4
HEAD_DIM = 64
Q_PER_KV = N_HEADS // N_KV_HEADS
WINDOW = 128
ROT_DIM = HEAD_DIM // 4
ROPE_THETA = 500000.0
ATT_W = N_HEADS * HEAD_DIM
KV_W = N_KV_HEADS * HEAD_DIM
N_EXPERTS = 32
TOP_K = 4
D_FF = 1024
SWIGLU_ALPHA = 1.702
SWIGLU_LIMIT = 7.0
NORM_EPS = 1e-5
NEG_INF = -1e30
N_IN = SGU_WIDTH * 2 + ATT_W + KV_W * 2 + D_MODEL * 2

LANES = 128
ROW_TILE = 256
MOE_TILE = 256
ROW_UNROLL = 8
SLOT_CHUNK = 2048
KV_DUP_W = N_KV_HEADS * LANES
KEY_SPAN = WINDOW + CHUNK
VMEM_LIMIT = 56 * 1024 * 1024

_SQRT_HALF = 0.7071067811865476


def _erf(x):
    x = jnp.clip(x, -4.0, 4.0)
    x2 = x * x
    a = x2 * (-2.72614225801306e-10) + 2.77068142495902e-08
    a = a * x2 + (-2.10102402082508e-06)
    a = a * x2 + (-5.69250639462346e-05)
    a = a * x2 + (-7.34990630326855e-04)
    a = a * x2 + (-2.95459980854025e-03)
    a = a * x2 + (-1.60960333262415e-02)
    a = a * x
    b = x2 * (-1.45660718464996e-05) + (-2.13374055278905e-04)
    b = b * x2 + (-1.68282697438203e-03)
    b = b * x2 + (-7.37332916720468e-03)
    b = b * x2 + (-1.42647390514189e-02)
    return a / b


def _gelu(x):
    return 0.5 * x * (1.0 + _erf(x * _SQRT_HALF))


def _sigmoid(x):
    return 1.0 / (1.0 + jnp.exp(-x))


def _bf16(x):
    return x.astype(jnp.bfloat16)


def _dot(a, b):
    return jnp.dot(a, b, preferred_element_type=jnp.float32)


ROW_SUBTILES = D_MODEL // LANES


def _store_row_tiled(ref, lead, x):
    rows = x.shape[0]
    for s in range(ROW_SUBTILES):
        ref[(*lead, pl.ds(s, rows, stride=ROW_SUBTILES), slice(None))] = x[:, s * LANES:(s + 1) * LANES]


def _load_row_tiled(ref, lead, rows):
    return jnp.concatenate(
        [ref[(*lead, pl.ds(s, rows, stride=ROW_SUBTILES), slice(None))] for s in range(ROW_SUBTILES)],
        axis=1)


def _dot_nt(a, b):
    return lax.dot_general(a, b, (((1,), (1,)), ((), ())), preferred_element_type=jnp.float32)


def _rms(x, g):
    return x * lax.rsqrt(jnp.mean(x * x, axis=-1, keepdims=True) + NORM_EPS) * g


def _lane_lo(rows):
    return lax.broadcasted_iota(jnp.int32, (rows, LANES), 1) < HEAD_DIM


def _dup_heads(kv):
    rows = kv.shape[0]
    lo = _lane_lo(rows)
    out = []
    for j in range(KV_W // LANES):
        blk = kv[:, j * LANES:(j + 1) * LANES]
        swp = pltpu.roll(blk, HEAD_DIM, axis=1)
        out.append(jnp.where(lo, blk, swp))
        out.append(jnp.where(lo, swp, blk))
    return _bf16(jnp.concatenate(out, axis=1))


def _rope_block(zb, cos_b, sin_lo, sin_hi):
    up = pltpu.roll(zb, LANES - ROT_DIM // 2, axis=1)
    dn = pltpu.roll(zb, ROT_DIM // 2, axis=1)
    return zb * cos_b + up * sin_lo + dn * sin_hi


def _proj_kernel(n_prompt_tiles, xp_ref, xs_ref, gmix_ref, w_ref, b_ref, lng_ref, lnb_ref,
                 cos_ref, slo_ref, shi_ref,
                 u_ref, vln_ref, vs_ref, q_ref, k_ref, v_ref, kd_ref, vd_ref, ga_ref, gb_ref):
    i = pl.program_id(0)
    x = jnp.where(i < n_prompt_tiles, xp_ref[...], xs_ref[...])
    h = _bf16(_rms(x, gmix_ref[...]))

    def seg(lo, width):
        return _dot(h, w_ref[:, lo:lo + width]) + b_ref[:, lo:lo + width]

    o = 0
    u_ref[...] = _bf16(_gelu(seg(o, SGU_WIDTH)))
    o += SGU_WIDTH
    gv = _gelu(seg(o, SGU_WIDTH))
    gc = gv - jnp.mean(gv, axis=-1, keepdims=True)
    var = jnp.mean(gc * gc, axis=-1, keepdims=True)
    vln = gc * lax.rsqrt(var + NORM_EPS) * lng_ref[...] + lnb_ref[...]
    vln_ref[...] = _bf16(vln)

    @pl.when(i >= n_prompt_tiles)
    def _():
        vs_ref[...] = vln

    o += SGU_WIDTH
    cos_b, sin_lo, sin_hi = cos_ref[...], slo_ref[...], shi_ref[...]
    zq = seg(o, ATT_W)
    for j in range(ATT_W // LANES):
        blk = _rope_block(zq[:, j * LANES:(j + 1) * LANES], cos_b, sin_lo, sin_hi)
        q_ref[:, j * LANES:(j + 1) * LANES] = _bf16(blk * (HEAD_DIM ** -0.5))
    o += ATT_W
    zk = seg(o, KV_W)
    kr = jnp.concatenate(
        [_rope_block(zk[:, j * LANES:(j + 1) * LANES], cos_b, sin_lo, sin_hi)
         for j in range(KV_W // LANES)], axis=1)
    k_ref[...] = kr
    kd_ref[...] = _dup_heads(kr)
    o += KV_W
    zv = seg(o, KV_W)
    v_ref[...] = zv
    vd_ref[...] = _dup_heads(zv)
    o += KV_W
    ga_ref[...] = _bf16(_sigmoid(seg(o, D_MODEL)))
    o += D_MODEL
    gb_ref[...] = _bf16(_sigmoid(seg(o, D_MODEL)))


def _rope_tables(pos):
    half = ROT_DIM // 2
    inv = np.float32(ROPE_THETA) ** (-np.arange(half, dtype=np.float32) * np.float32(2.0) / ROT_DIM)
    ang = pos.astype(np.float32)[:, None] * inv.astype(np.float32)[None, :]
    cos = jnp.asarray(np.cos(ang.astype(np.float64)).astype(np.float32))
    sin = jnp.asarray(np.sin(ang.astype(np.float64)).astype(np.float32))
    n = pos.shape[0]
    ones = jnp.ones((n, HEAD_DIM - ROT_DIM), jnp.float32)
    zeros = jnp.zeros((n, HEAD_DIM - ROT_DIM), jnp.float32)
    zh = jnp.zeros((n, half), jnp.float32)
    cos_h = jnp.concatenate([cos, cos, ones], axis=1)
    slo_h = jnp.concatenate([-sin, zh, zeros], axis=1)
    shi_h = jnp.concatenate([zh, sin, zeros], axis=1)
    rep = LANES // HEAD_DIM
    return (jnp.tile(cos_h, (1, rep)), jnp.tile(slo_h, (1, rep)), jnp.tile(shi_h, (1, rep)))


def _row_spec(width):
    return pl.BlockSpec((ROW_TILE, width), lambda i: (i, 0))


def _const_spec(shape):
    return pl.BlockSpec(shape, lambda i: (0,) * len(shape))


def _prompt_spec(width, n_prompt_tiles):
    return pl.BlockSpec((ROW_TILE, width), lambda i: (jnp.minimum(i, n_prompt_tiles - 1), 0))


def _sample_spec(width, n_prompt_tiles):
    return pl.BlockSpec((ROW_TILE, width), lambda i: (jnp.maximum(i - n_prompt_tiles, 0), 0))


def _params():
    return pltpu.CompilerParams(dimension_semantics=("arbitrary",), vmem_limit_bytes=VMEM_LIMIT)


def _project(xp, xs, g_mix, w_in, b_in, ln_g, ln_b, tables):
    tp, ts = xp.shape[0], xs.shape[0]
    t = tp + ts
    npt = tp // ROW_TILE
    f32, bf16 = jnp.float32, jnp.bfloat16
    out_shape = (
        jax.ShapeDtypeStruct((t, SGU_WIDTH), bf16),
        jax.ShapeDtypeStruct((t, SGU_WIDTH), bf16),
        jax.ShapeDtypeStruct((ts, SGU_WIDTH), f32),
        jax.ShapeDtypeStruct((t, ATT_W), bf16),
        jax.ShapeDtypeStruct((t, KV_W), f32),
        jax.ShapeDtypeStruct((t, KV_W), f32),
        jax.ShapeDtypeStruct((t, KV_DUP_W), bf16),
        jax.ShapeDtypeStruct((t, KV_DUP_W), bf16),
        jax.ShapeDtypeStruct((t, D_MODEL), bf16),
        jax.ShapeDtypeStruct((t, D_MODEL), bf16),
    )
    return pl.pallas_call(
        functools.partial(_proj_kernel, npt),
        out_shape=out_shape,
        grid=(t // ROW_TILE,),
        in_specs=[
            _prompt_spec(D_MODEL, npt), _sample_spec(D_MODEL, npt),
            _const_spec((1, D_MODEL)), _const_spec((D_MODEL, N_IN)),
            _const_spec((1, N_IN)), _const_spec((1, SGU_WIDTH)), _const_spec((1, SGU_WIDTH)),
            _row_spec(LANES), _row_spec(LANES), _row_spec(LANES),
        ],
        out_specs=(
            _row_spec(SGU_WIDTH), _row_spec(SGU_WIDTH), _sample_spec(SGU_WIDTH, npt),
            _row_spec(ATT_W), _row_spec(KV_W), _row_spec(KV_W), _row_spec(KV_DUP_W),
            _row_spec(KV_DUP_W), _row_spec(D_MODEL), _row_spec(D_MODEL),
        ),
        compiler_params=_params(),
        name="proj",
    )(xp, xs, g_mix.reshape(1, -1), w_in.astype(bf16), b_in.reshape(1, -1),
      ln_g.reshape(1, -1), ln_b.reshape(1, -1), *tables)


def _attend(qa, qb, kwin, vwin, sink, valid):
    lo = _lane_lo(CHUNK)
    zero = jnp.zeros_like(qa)
    lhs = jnp.concatenate([jnp.where(lo, qa, zero), jnp.where(lo, zero, qa),
                           jnp.where(lo, qb, zero), jnp.where(lo, zero, qb)], axis=0)
    s = _dot_nt(lhs, kwin)
    if valid is not None:
        s = jnp.where(valid, s, NEG_INF)
    m = jnp.maximum(jnp.max(s, axis=-1, keepdims=True), sink)
    p = jnp.exp(s - m)
    denom = jnp.sum(p, axis=-1, keepdims=True) + jnp.exp(sink - m)
    r = _dot(_bf16(p / denom), vwin)
    oa = jnp.where(lo, r[0:CHUNK], r[CHUNK:2 * CHUNK])
    ob = jnp.where(lo, r[2 * CHUNK:3 * CHUNK], r[3 * CHUNK:4 * CHUNK])
    return oa, ob


def _top4(logits):
    rows = logits.shape[0]
    lane = lax.broadcasted_iota(jnp.int32, (rows, N_EXPERTS), 1)
    work = logits
    vals, idxs = [], []
    for _ in range(TOP_K):
        m = jnp.max(work, axis=-1, keepdims=True)
        idx = jnp.min(jnp.where(work == m, lane, N_EXPERTS), axis=-1, keepdims=True)
        vals.append(m)
        idxs.append(idx)
        work = jnp.where(lane == idx, -jnp.inf, work)
    exps = [jnp.exp(v - vals[0]) for v in vals]
    denom = exps[0] + exps[1] + exps[2] + exps[3]
    wide = lax.broadcasted_iota(jnp.int32, (rows, LANES), 1)
    topi = jnp.zeros((rows, LANES), jnp.int32)
    topg = jnp.zeros((rows, LANES), jnp.float32)
    for k in range(TOP_K):
        topi = jnp.where(wide == k, idxs[k], topi)
        topg = jnp.where(wide == k, exps[k] / denom, topg)
    return topi, topg


def _mix_kernel(tiles_per_seq, n_prompt_tiles,
                xp_ref, xs_ref, u_ref, vln_ref, q_ref, kd_ref, vd_ref, kdp_ref, vdp_ref,
                ck_ref, cv_ref, ga_ref, gb_ref, wsp_ref, bsp_ref, sink_ref,
                wpa_ref, wpb_ref, wo_ref, gffn_ref, wrh_ref, wrl_ref, br_ref,
                x1_ref, h2_ref, topi_ref, topg_ref,
                a_s, o_s, kwin_s, vwin_s):
    i = pl.program_id(0)
    n_streams = ROW_TILE // CHUNK

    def sgu_rows(r0, rows):
        ri = lax.broadcasted_iota(jnp.int32, (rows, rows), 0) // CHUNK
        ci = lax.broadcasted_iota(jnp.int32, (rows, rows), 1) // CHUNK
        for g in range(SGU_GROUPS):
            cols = slice(g * LANES, (g + 1) * LANES)
            w = _bf16(jnp.where(ci <= ri, wsp_ref[g, :rows, :rows], 0.0))
            sp = _dot(w, vln_ref[r0:r0 + rows, cols]) + bsp_ref[:rows, g:g + 1]
            a_s[r0:r0 + rows, cols] = _bf16(u_ref[r0:r0 + rows, cols].astype(jnp.float32) * sp)

    def attend_rows(r0, kwin_of, valid):
        for g in range(N_KV_HEADS):
            c0 = g * Q_PER_KV * HEAD_DIM
            kwin, vwin = kwin_of(g)
            oa, ob = _attend(q_ref[r0:r0 + CHUNK, c0:c0 + LANES],
                             q_ref[r0:r0 + CHUNK, c0 + LANES:c0 + 2 * LANES],
                             kwin, vwin, sink_ref[g][:, 0:1], valid)
            o_s[r0:r0 + CHUNK, c0:c0 + LANES] = _bf16(oa)
            o_s[r0:r0 + CHUNK, c0 + LANES:c0 + 2 * LANES] = _bf16(ob)

    @pl.when(i < n_prompt_tiles)
    def _prompt():
        for c in range(ROW_TILE // SGU_CHUNK):
            sgu_rows(c * SGU_CHUNK, SGU_CHUNK)
        kwin_s[0:WINDOW] = kdp_ref[...]
        kwin_s[WINDOW:WINDOW + ROW_TILE] = kd_ref[...]
        vwin_s[0:WINDOW] = vdp_ref[...]
        vwin_s[WINDOW:WINDOW + ROW_TILE] = vd_ref[...]
        first = (i % tiles_per_seq) == 0
        col = lax.broadcasted_iota(jnp.int32, (1, KEY_SPAN), 1)
        for j in range(ROW_TILE // CHUNK):
            r0 = j * CHUNK
            valid = jnp.logical_or(jnp.logical_not(first), col + r0 >= WINDOW) if r0 < WINDOW else None

            def kwin_of(g, r0=r0):
                cols = slice(g * LANES, (g + 1) * LANES)
                return kwin_s[r0:r0 + KEY_SPAN, cols], vwin_s[r0:r0 + KEY_SPAN, cols]

            attend_rows(r0, kwin_of, valid)

    @pl.when(i >= n_prompt_tiles)
    def _sample():
        for s in range(n_streams):
            r0 = s * CHUNK
            sgu_rows(r0, CHUNK)
            kwin_s[0:WINDOW] = _dup_heads(ck_ref[s])
            kwin_s[WINDOW:KEY_SPAN] = kd_ref[r0:r0 + CHUNK]
            vwin_s[0:WINDOW] = _dup_heads(cv_ref[s])
            vwin_s[WINDOW:KEY_SPAN] = vd_ref[r0:r0 + CHUNK]

            def kwin_of(g):
                cols = slice(g * LANES, (g + 1) * LANES)
                return kwin_s[0:KEY_SPAN, cols], vwin_s[0:KEY_SPAN, cols]

            attend_rows(r0, kwin_of, None)

    x = jnp.where(i < n_prompt_tiles, xp_ref[...], xs_ref[...])
    m = (ga_ref[...].astype(jnp.float32) * _dot(a_s[...], wpa_ref[...])
         + gb_ref[...].astype(jnp.float32) * _dot(o_s[...], wpb_ref[...]))
    x1 = x + _dot(_bf16(m), wo_ref[...])
    x1_ref[...] = x1
    h2 = _rms(x1, gffn_ref[...])
    _store_row_tiled(h2_ref, (), h2)
    hh = _bf16(h2)
    hl = _bf16(h2 - hh.astype(jnp.float32))
    logits = (_dot(hh, wrh_ref[...]) + _dot(hh, wrl_ref[...]) + _dot(hl, wrh_ref[...])) + br_ref[...]
    topi, topg = _top4(logits)
    topi_ref[...] = topi
    topg_ref[...] = topg


def _mix(xp, xs, u, vln, q, kd, vd, cache_k, cache_v, ga, gb, w_sp, b_sp, sinks,
         w_pa, w_pb, w_o, g_ffn, w_router, b_router, seq):
    tp, ts = xp.shape[0], xs.shape[0]
    t = tp + ts
    npt = tp // ROW_TILE
    tiles_per_seq = seq // ROW_TILE
    f32, bf16 = jnp.float32, jnp.bfloat16
    n_streams = ROW_TILE // CHUNK
    win_per_tile = ROW_TILE // WINDOW

    prev_spec = pl.BlockSpec(
        (WINDOW, KV_DUP_W), lambda i: (jnp.maximum(jnp.minimum(i, npt - 1) * win_per_tile - 1, 0), 0))
    cache_spec = pl.BlockSpec(
        (n_streams, WINDOW, KV_W), lambda i: (jnp.maximum(i - npt, 0), 0, 0))
    sink_cols = jnp.broadcast_to(
        jnp.repeat(sinks.astype(f32).reshape(N_KV_HEADS, Q_PER_KV), CHUNK, axis=1)[:, :, None],
        (N_KV_HEADS, Q_PER_KV * CHUNK, LANES))
    wr_hi = w_router.astype(bf16)
    wr_lo = (w_router - wr_hi.astype(f32)).astype(bf16)
    out_shape = (
        jax.ShapeDtypeStruct((t, D_MODEL), f32),
        jax.ShapeDtypeStruct((t * ROW_SUBTILES, LANES), f32),
        jax.ShapeDtypeStruct((t, LANES), jnp.int32),
        jax.ShapeDtypeStruct((t, LANES), f32),
    )
    return pl.pallas_call(
        functools.partial(_mix_kernel, tiles_per_seq, npt),
        out_shape=out_shape,
        grid=(t // ROW_TILE,),
        in_specs=[
            _prompt_spec(D_MODEL, npt), _sample_spec(D_MODEL, npt),
            _row_spec(SGU_WIDTH), _row_spec(SGU_WIDTH), _row_spec(ATT_W),
            _row_spec(KV_DUP_W), _row_spec(KV_DUP_W), prev_spec, prev_spec,
            cache_spec, cache_spec, _row_spec(D_MODEL), _row_spec(D_MODEL),
            _const_spec((SGU_GROUPS, SGU_CHUNK, SGU_CHUNK)), _const_spec((SGU_CHUNK, SGU_GROUPS)),
            _const_spec((N_KV_HEADS, Q_PER_KV * CHUNK, LANES)),
            _const_spec((SGU_WIDTH, D_MODEL)), _const_spec((ATT_W, D_MODEL)),
            _const_spec((D_MODEL, D_MODEL)), _const_spec((1, D_MODEL)),
            _const_spec((D_MODEL, N_EXPERTS)), _const_spec((D_MODEL, N_EXPERTS)),
            _const_spec((1, N_EXPERTS)),
        ],
        out_specs=(_row_spec(D_MODEL),
                   pl.BlockSpec((ROW_TILE * ROW_SUBTILES, LANES), lambda i: (i, 0)),
                   _row_spec(LANES), _row_spec(LANES)),
        scratch_shapes=[
            pltpu.VMEM((ROW_TILE, SGU_WIDTH), bf16), pltpu.VMEM((ROW_TILE, ATT_W), bf16),
            pltpu.VMEM((WINDOW + ROW_TILE, KV_DUP_W), bf16),
            pltpu.VMEM((WINDOW + ROW_TILE, KV_DUP_W), bf16),
        ],
        compiler_params=_params(),
        name="mix",
    )(xp, xs, u, vln, q, kd, vd, kd, vd,
      cache_k.reshape(-1, WINDOW, KV_W), cache_v.reshape(-1, WINDOW, KV_W), ga, gb,
      w_sp, b_sp.T, sink_cols, w_pa.astype(bf16), w_pb.astype(bf16), w_o.astype(bf16),
      g_ffn.reshape(1, -1), wr_hi, wr_lo, b_router.reshape(1, -1))


def _rank_kernel(topi_ref, rank_ref, count_ref, carry_s):
    i = pl.program_id(0)

    @pl.when(i == 0)
    def _():
        carry_s[...] = jnp.zeros_like(carry_s)

    topi = topi_ref[...]
    lane = lax.broadcasted_iota(jnp.int32, (ROW_TILE, LANES), 1)
    onehot = jnp.zeros((ROW_TILE, LANES), jnp.float32)
    for k in range(TOP_K):
        onehot = jnp.where(lane == topi[:, k:k + 1], 1.0, onehot)
    r = lax.broadcasted_iota(jnp.int32, (ROW_TILE, ROW_TILE), 0)
    c = lax.broadcasted_iota(jnp.int32, (ROW_TILE, ROW_TILE), 1)
    below = _bf16(jnp.where(c < r, 1.0, 0.0))
    rank = _dot(below, _bf16(onehot)) + carry_s[0:1, :]
    out = jnp.zeros((ROW_TILE, LANES), jnp.float32)
    for k in range(TOP_K):
        sel = jnp.sum(jnp.where(lane == topi[:, k:k + 1], rank, 0.0), axis=-1, keepdims=True)
        out = jnp.where(lane == k, sel, out)
    rank_ref[...] = out.astype(jnp.int32)
    carry_s[...] = carry_s[...] + jnp.sum(onehot, axis=0, keepdims=True)
    count_ref[...] = carry_s[...].astype(jnp.int32)


def _rank(topi):
    t = topi.shape[0]
    return pl.pallas_call(
        _rank_kernel,
        out_shape=(jax.ShapeDtypeStruct((t, LANES), jnp.int32),
                   jax.ShapeDtypeStruct((8, LANES), jnp.int32)),
        grid=(t // ROW_TILE,),
        in_specs=[_row_spec(LANES)],
        out_specs=(_row_spec(LANES), _const_spec((8, LANES))),
        scratch_shapes=[pltpu.VMEM((8, LANES), jnp.float32)],
        compiler_params=_params(),
        name="rank",
    )(topi)


def _unrolled_rows(n_rows, fn):
    if isinstance(n_rows, int):
        groups, tail_start = n_rows // ROW_UNROLL, n_rows - n_rows % ROW_UNROLL
    else:
        groups = lax.shift_right_logical(n_rows, ROW_UNROLL.bit_length() - 1)
        tail_start = groups * ROW_UNROLL

    def group(gi, carry):
        for lane in range(ROW_UNROLL):
            fn(gi * ROW_UNROLL + lane, lane)
        return carry

    def tail(r, carry):
        fn(r, 0)
        return carry

    lax.fori_loop(0, groups, group, 0)
    lax.fori_loop(tail_start, n_rows, tail, 0)


def _expert_kernel(n_tokens, te_ref, nu_ref, nv_ref,
                   slot_ref, slot_next_ref, h2_hbm, wgu_ref, bgu_ref, wdn_ref, bdn_ref,
                   y_hbm,
                   xbuf, ybuf, wgu_s, wdn_s, gsem, ssem):
    i = pl.program_id(0)
    n_used = nu_ref[0]
    slot = i % 2

    def tile_of(row):
        return pl.ds(pl.multiple_of(row * ROW_SUBTILES, ROW_SUBTILES), ROW_SUBTILES)

    def start_gather(idx_ref, buf_slot):
        def one(r, lane):
            tok = lax.shift_right_logical(idx_ref[0, 0, r], TOP_K.bit_length() - 1)
            pltpu.make_async_copy(h2_hbm.at[tile_of(tok)], xbuf.at[buf_slot, tile_of(r)],
                                  gsem.at[buf_slot]).start(priority=lane % 2)
        _unrolled_rows(MOE_TILE, one)

    @pl.when(jnp.logical_and(i == 0, n_used > 0))
    def _():
        start_gather(slot_ref, 0)

    @pl.when(i + 1 < n_used)
    def _():
        start_gather(slot_next_ref, 1 - slot)

    expert_changed = jnp.logical_or(i == 0, te_ref[i] != te_ref[jnp.maximum(i - 1, 0)])

    @pl.when(jnp.logical_and(i < n_used, expert_changed))
    def _():
        wgu_s[...] = _bf16(wgu_ref[0])
        wdn_s[...] = _bf16(wdn_ref[0])

    @pl.when(i < n_used)
    def _():
        pltpu.make_async_copy(h2_hbm.at[pl.ds(0, MOE_TILE * ROW_SUBTILES)], xbuf.at[slot],
                              gsem.at[slot]).wait()
        x = _bf16(_load_row_tiled(xbuf, (slot,), MOE_TILE))
        gu = _dot(x, wgu_s[...]) + bgu_ref[0]
        gate = jnp.minimum(gu[:, :D_FF], SWIGLU_LIMIT)
        lin = jnp.clip(gu[:, D_FF:], -SWIGLU_LIMIT, SWIGLU_LIMIT)
        act = gate * _sigmoid(SWIGLU_ALPHA * gate) * (lin + 1.0)
        _store_row_tiled(ybuf, (slot,), _dot(_bf16(act), wdn_s[...]) + bdn_ref[0])

        def one(r, lane):
            packed = slot_ref[0, 0, r]
            tok = lax.shift_right_logical(packed, TOP_K.bit_length() - 1)
            row = (packed & (TOP_K - 1)) * n_tokens + tok
            pltpu.make_async_copy(ybuf.at[slot, tile_of(r)], y_hbm.at[tile_of(row)],
                                  ssem.at[slot]).start(priority=lane % 2)
        _unrolled_rows(nv_ref[i], one)

    n_prev = nv_ref[jnp.maximum(i - 1, 0)]

    @pl.when(jnp.logical_and(jnp.logical_and(i >= 1, i <= n_used), n_prev > 0))
    def _():
        span = pl.ds(0, pl.multiple_of(n_prev * ROW_SUBTILES, ROW_SUBTILES))
        pltpu.make_async_copy(ybuf.at[1 - slot, span], y_hbm.at[span], ssem.at[1 - slot]).wait()


def _experts(tile_expert, n_used, n_valid, slots, h2, w_gu, b_gu, w_dn, b_dn):
    n_tiles = slots.shape[0]
    t = h2.shape[0] // ROW_SUBTILES
    f32, bf16 = jnp.float32, jnp.bfloat16
    idx_spec = pl.BlockSpec((1, 1, MOE_TILE), lambda i, te, nu, nv: (i, 0, 0), memory_space=pltpu.SMEM)
    idx_next_spec = pl.BlockSpec(
        (1, 1, MOE_TILE), lambda i, te, nu, nv: (jnp.minimum(i + 1, n_tiles - 1), 0, 0),
        memory_space=pltpu.SMEM)
    grid_spec = pltpu.PrefetchScalarGridSpec(
        num_scalar_prefetch=3,
        grid=(n_tiles,),
        in_specs=[
            idx_spec, idx_next_spec,
            pl.BlockSpec(memory_space=pl.ANY),
            pl.BlockSpec((1, D_MODEL, 2 * D_FF), lambda i, te, nu, nv: (te[i], 0, 0)),
            pl.BlockSpec((1, 1, 2 * D_FF), lambda i, te, nu, nv: (te[i], 0, 0)),
            pl.BlockSpec((1, D_FF, D_MODEL), lambda i, te, nu, nv: (te[i], 0, 0)),
            pl.BlockSpec((1, 1, D_MODEL), lambda i, te, nu, nv: (te[i], 0, 0)),
        ],
        out_specs=pl.BlockSpec(memory_space=pl.ANY),
        scratch_shapes=[
            pltpu.VMEM((2, MOE_TILE * ROW_SUBTILES, LANES), f32),
            pltpu.VMEM((2, MOE_TILE * ROW_SUBTILES, LANES), f32),
            pltpu.VMEM((D_MODEL, 2 * D_FF), bf16), pltpu.VMEM((D_FF, D_MODEL), bf16),
            pltpu.SemaphoreType.DMA((2,)), pltpu.SemaphoreType.DMA((2,)),
        ],
    )
    return pl.pallas_call(
        functools.partial(_expert_kernel, t),
        out_shape=jax.ShapeDtypeStruct((t * TOP_K * ROW_SUBTILES, LANES), f32),
        grid_spec=grid_spec,
        compiler_params=_params(),
        name="experts",
    )(tile_expert, n_used, n_valid, slots, slots, h2, w_gu,
      b_gu.reshape(N_EXPERTS, 1, -1), w_dn, b_dn.reshape(N_EXPERTS, 1, -1))


def _combine_kernel(n_prompt_tiles, x1_ref, y0_ref, y1_ref, y2_ref, y3_ref, g_ref, gfin_ref,
                    yp_ref, ys_ref):
    i = pl.program_id(0)
    g = g_ref[...]
    acc = x1_ref[...]
    for k, y_ref in enumerate((y0_ref, y1_ref, y2_ref, y3_ref)):
        acc = acc + g[:, k:k + 1] * _load_row_tiled(y_ref, (), ROW_TILE)
    out = _rms(acc, gfin_ref[...])

    @pl.when(i < n_prompt_tiles)
    def _():
        yp_ref[...] = out

    @pl.when(i >= n_prompt_tiles)
    def _():
        ys_ref[...] = out


def _combine(x1, y_tk, topg, g_final, tp):
    t = x1.shape[0]
    npt = tp // ROW_TILE
    nt = t // ROW_TILE
    f32 = jnp.float32
    y_specs = [pl.BlockSpec((ROW_TILE * ROW_SUBTILES, LANES), lambda i, k=k: (k * nt + i, 0))
               for k in range(TOP_K)]
    return pl.pallas_call(
        functools.partial(_combine_kernel, npt),
        out_shape=(jax.ShapeDtypeStruct((tp, D_MODEL), f32),
                   jax.ShapeDtypeStruct((t - tp, D_MODEL), f32)),
        grid=(nt,),
        in_specs=[_row_spec(D_MODEL)] + y_specs + [_row_spec(LANES), _const_spec((1, D_MODEL))],
        out_specs=(_prompt_spec(D_MODEL, npt), _sample_spec(D_MODEL, npt)),
        compiler_params=_params(),
        name="combine",
    )(x1, y_tk, y_tk, y_tk, y_tk, topg, g_final.reshape(1, -1))


def _slot_kernel(dest_ref, slots_ref):
    i = pl.program_id(0)

    @pl.when(i == 0)
    def _():
        def zero(r, lane):
            slots_ref[r] = 0
        _unrolled_rows(slots_ref.shape[0], zero)

    base = i * SLOT_CHUNK

    def put(r, lane):
        slots_ref[dest_ref[0, 0, r]] = base + r
    _unrolled_rows(SLOT_CHUNK, put)


def _slot_map(dest, n_slots):
    n_assign = dest.shape[0]
    n_chunks = n_assign // SLOT_CHUNK
    return pl.pallas_call(
        _slot_kernel,
        out_shape=jax.ShapeDtypeStruct((n_slots,), jnp.int32),
        grid=(n_chunks,),
        in_specs=[pl.BlockSpec((1, 1, SLOT_CHUNK), lambda i: (i, 0, 0), memory_space=pltpu.SMEM)],
        out_specs=pl.BlockSpec(memory_space=pltpu.SMEM),
        compiler_params=_params(),
        name="slot_map",
    )(dest.reshape(n_chunks, 1, SLOT_CHUNK))


def _plan(topi, rank, counts, t):
    n_assign = t * TOP_K
    n_tiles = (n_assign + N_EXPERTS * (MOE_TILE - 1)) // MOE_TILE + 1
    counts = counts[0, :N_EXPERTS]
    tiles_e = (counts + MOE_TILE - 1) // MOE_TILE
    tile_end = jnp.cumsum(tiles_e)
    tile_start = tile_end - tiles_e
    n_used = tile_end[-1]
    dest = (tile_start * MOE_TILE)[topi[:, :TOP_K]] + rank[:, :TOP_K]
    slots = _slot_map(dest.reshape(-1), n_tiles * MOE_TILE)
    tile_ids = jnp.arange(n_tiles, dtype=jnp.int32)
    live = jnp.minimum(tile_ids, n_used - 1)
    tile_expert = jnp.sum(tile_end[None, :] <= live[:, None], axis=1).astype(jnp.int32)
    n_valid = jnp.clip(counts[tile_expert] - (tile_ids - tile_start[tile_expert]) * MOE_TILE,
                       0, MOE_TILE)
    n_valid = jnp.where(tile_ids < n_used, n_valid, 0).astype(jnp.int32)
    return (tile_expert, n_used.reshape(1).astype(jnp.int32), n_valid,
            slots.reshape(n_tiles, 1, MOE_TILE))


def kernel(x_prompt, x_sample, cache_k, cache_v, g_mix, w_in, b_in, ln_v_g, ln_v_b, w_sp, b_sp,
           attn_sinks, w_pa, w_pb, w_o, g_ffn, w_router, b_router, w_gu, b_gu, w_dn, b_dn, g_final):
    nb, seq, d = x_prompt.shape
    nsb, nnew, _ = x_sample.shape
    tp, ts = nb * seq, nsb * nnew
    t = tp + ts
    xp = x_prompt.reshape(tp, d)
    xs = x_sample.reshape(ts, d)
    pos = np.concatenate([np.tile(np.arange(seq), nb), np.tile(PAST_LEN + np.arange(nnew), nsb)])
    tables = _rope_tables(pos)
    u, vln, v_sgu, q, k, v, kd, vd, ga, gb = _project(
        xp, xs, g_mix[0], w_in[0], b_in[0], ln_v_g[0], ln_v_b[0], tables)
    x1, h2, topi, topg = _mix(
        xp, xs, u, vln, q, kd, vd, cache_k[0], cache_v[0], ga, gb, w_sp[0], b_sp[0], attn_sinks[0],
        w_pa[0], w_pb[0], w_o[0], g_ffn[0], w_router[0], b_router[0], seq)
    rank, counts = _rank(topi)
    tile_expert, n_used, n_valid, slots = _plan(topi, rank, counts, t)
    y_tk = _experts(tile_expert, n_used, n_valid, slots, h2, w_gu[0], b_gu[0], w_dn[0], b_dn[0])
    y_p, y_s = _combine(x1, y_tk, topg, g_final, tp)

    keep = min(WINDOW, seq)
    kp = k[:tp].reshape(nb, seq, N_KV_HEADS, HEAD_DIM)[:, seq - keep:]
    vp = v[:tp].reshape(nb, seq, N_KV_HEADS, HEAD_DIM)[:, seq - keep:]
    ks = k[tp:].reshape(nsb, nnew, N_KV_HEADS, HEAD_DIM)
    vs = v[tp:].reshape(nsb, nnew, N_KV_HEADS, HEAD_DIM)
    return (y_p.reshape(nb, seq, d), y_s.reshape(nsb, nnew, d), kp[None], vp[None], ks[None],
            vs[None], v_sgu.reshape(1, nsb, nnew, SGU_WIDTH))
```

```python
import functools

import numpy as np
import jax
import jax.numpy as jnp
from jax import lax
from jax.experimental import pallas as pl
from jax.experimental.pallas import tpu as pltpu

D_MODEL = 1024
PAST_LEN = 2048
CHUNK = 64
SGU_CHUNK = 128
SGU_GROUPS = 8
SGU_WIDTH = 1024
N_HEADS = 16
N_KV_HEADS = 4
HEAD_DIM = 64
Q_PER_KV = N_HEADS // N_KV_HEADS
WINDOW = 128
ROT_DIM = HEAD_DIM // 4
ROPE_THETA = 500000.0
ATT_W = N_HEADS * HEAD_DIM
KV_W = N_KV_HEADS * HEAD_DIM
N_EXPERTS = 32
TOP_K = 4
D_FF = 1024
SWIGLU_ALPHA = 1.702
SWIGLU_LIMIT = 7.0
NORM_EPS = 1e-5
NEG_INF = -1e30
N_IN = SGU_WIDTH * 2 + ATT_W + KV_W * 2 + D_MODEL * 2

LANES = 128
ROW_TILE = 256
MOE_TILE = 256
ROW_UNROLL = 8
SLOT_CHUNK = 2048
KV_DUP_W = N_KV_HEADS * LANES
KEY_SPAN = WINDOW + CHUNK
VMEM_LIMIT = 56 * 1024 * 1024

_SQRT_HALF = 0.7071067811865476


def _gelu(x):
    z = jnp.abs(x) * _SQRT_HALF
    t = 1.0 / (1.0 + 0.3275911 * z)
    poly = t * (0.254829592 + t * (-0.284496736 + t * (1.421413741
                + t * (-1.453152027 + t * 1.061405429))))
    half_tail = (0.5 * x) * (poly * jnp.exp(-z * z))
    return jnp.where(x >= 0.0, x - half_tail, half_tail)


def _sigmoid(x):
    return 1.0 / (1.0 + jnp.exp(-x))


def _bf16(x):
    return x.astype(jnp.bfloat16)


def _dot(a, b):
    return jnp.dot(a, b, preferred_element_type=jnp.float32)


ROW_SUBTILES = D_MODEL // LANES


def _store_row_tiled(ref, lead, x):
    rows = x.shape[0]
    for s in range(ROW_SUBTILES):
        ref[(*lead, pl.ds(s, rows, stride=ROW_SUBTILES), slice(None))] = x[:, s * LANES:(s + 1) * LANES]


def _load_row_tiled(ref, lead, rows):
    return jnp.concatenate(
        [ref[(*lead, pl.ds(s, rows, stride=ROW_SUBTILES), slice(None))] for s in range(ROW_SUBTILES)],
        axis=1)


def _dot_nt(a, b):
    return lax.dot_general(a, b, (((1,), (1,)), ((), ())), preferred_element_type=jnp.float32)


def _rms(x, g):
    return x * lax.rsqrt(jnp.mean(x * x, axis=-1, keepdims=True) + NORM_EPS) * g


def _lane_lo(rows):
    return lax.broadcasted_iota(jnp.int32, (rows, LANES), 1) < HEAD_DIM


def _dup_heads(kv):
    rows = kv.shape[0]
    lo = _lane_lo(rows)
    out = []
    for j in range(KV_W // LANES):
        blk = kv[:, j * LANES:(j + 1) * LANES]
        swp = pltpu.roll(blk, HEAD_DIM, axis=1)
        out.append(jnp.where(lo, blk, swp))
        out.append(jnp.where(lo, swp, blk))
    return _bf16(jnp.concatenate(out, axis=1))


def _rope_block(zb, cos_b, sin_lo, sin_hi):
    up = pltpu.roll(zb, LANES - ROT_DIM // 2, axis=1)
    dn = pltpu.roll(zb, ROT_DIM // 2, axis=1)
    return zb * cos_b + up * sin_lo + dn * sin_hi


def _proj_kernel(n_prompt_tiles, xp_ref, xs_ref, gmix_ref, w_ref, b_ref, lng_ref, lnb_ref,
                 cos_ref, slo_ref, shi_ref,
                 u_ref, vln_ref, vs_ref, q_ref, k_ref, v_ref, kd_ref, vd_ref, ga_ref, gb_ref):
    i = pl.program_id(0)
    x = jnp.where(i < n_prompt_tiles, xp_ref[...], xs_ref[...])
    h = _bf16(_rms(x, gmix_ref[...]))

    def seg(lo, width):
        return _dot(h, w_ref[:, lo:lo + width]) + b_ref[:, lo:lo + width]

    o = 0
    u_ref[...] = _bf16(_gelu(seg(o, SGU_WIDTH)))
    o += SGU_WIDTH
    gv = _gelu(seg(o, SGU_WIDTH))
    gc = gv - jnp.mean(gv, axis=-1, keepdims=True)
    var = jnp.mean(gc * gc, axis=-1, keepdims=True)
    vln = gc * lax.rsqrt(var + NORM_EPS) * lng_ref[...] + lnb_ref[...]
    vln_ref[...] = _bf16(vln)

    @pl.when(i >= n_prompt_tiles)
    def _():
        vs_ref[...] = vln

    o += SGU_WIDTH
    cos_b, sin_lo, sin_hi = cos_ref[...], slo_ref[...], shi_ref[...]
    zq = seg(o, ATT_W)
    for j in range(ATT_W // LANES):
        blk = _rope_block(zq[:, j * LANES:(j + 1) * LANES], cos_b, sin_lo, sin_hi)
        q_ref[:, j * LANES:(j + 1) * LANES] = _bf16(blk * (HEAD_DIM ** -0.5))
    o += ATT_W
    zk = seg(o, KV_W)
    kr = jnp.concatenate(
        [_rope_block(zk[:, j * LANES:(j + 1) * LANES], cos_b, sin_lo, sin_hi)
         for j in range(KV_W // LANES)], axis=1)
    k_ref[...] = kr
    kd_ref[...] = _dup_heads(kr)
    o += KV_W
    zv = seg(o, KV_W)
    v_ref[...] = zv
    vd_ref[...] = _dup_heads(zv)
    o += KV_W
    ga_ref[...] = _bf16(_sigmoid(seg(o, D_MODEL)))
    o += D_MODEL
    gb_ref[...] = _bf16(_sigmoid(seg(o, D_MODEL)))


def _rope_tables(pos):
    half = ROT_DIM // 2
    inv = np.float32(ROPE_THETA) ** (-np.arange(half, dtype=np.float32) * np.float32(2.0) / ROT_DIM)
    ang = pos.astype(np.float32)[:, None] * inv.astype(np.float32)[None, :]
    cos = jnp.asarray(np.cos(ang.astype(np.float64)).astype(np.float32))
    sin = jnp.asarray(np.sin(ang.astype(np.float64)).astype(np.float32))
    n = pos.shape[0]
    ones = jnp.ones((n, HEAD_DIM - ROT_DIM), jnp.float32)
    zeros = jnp.zeros((n, HEAD_DIM - ROT_DIM), jnp.float32)
    zh = jnp.zeros((n, half), jnp.float32)
    cos_h = jnp.concatenate([cos, cos, ones], axis=1)
    slo_h = jnp.concatenate([-sin, zh, zeros], axis=1)
    shi_h = jnp.concatenate([zh, sin, zeros], axis=1)
    rep = LANES // HEAD_DIM
    return (jnp.tile(cos_h, (1, rep)), jnp.tile(slo_h, (1, rep)), jnp.tile(shi_h, (1, rep)))


def _row_spec(width):
    return pl.BlockSpec((ROW_TILE, width), lambda i: (i, 0))


def _const_spec(shape):
    return pl.BlockSpec(shape, lambda i: (0,) * len(shape))


def _prompt_spec(width, n_prompt_tiles):
    return pl.BlockSpec((ROW_TILE, width), lambda i: (jnp.minimum(i, n_prompt_tiles - 1), 0))


def _sample_spec(width, n_prompt_tiles):
    return pl.BlockSpec((ROW_TILE, width), lambda i: (jnp.maximum(i - n_prompt_tiles, 0), 0))


def _params():
    return pltpu.CompilerParams(dimension_semantics=("arbitrary",), vmem_limit_bytes=VMEM_LIMIT)


def _project(xp, xs, g_mix, w_in, b_in, ln_g, ln_b, tables):
    tp, ts = xp.shape[0], xs.shape[0]
    t = tp + ts
    npt = tp // ROW_TILE
    f32, bf16 = jnp.float32, jnp.bfloat16
    out_shape = (
        jax.ShapeDtypeStruct((t, SGU_WIDTH), bf16),
        jax.ShapeDtypeStruct((t, SGU_WIDTH), bf16),
        jax.ShapeDtypeStruct((ts, SGU_WIDTH), f32),
        jax.ShapeDtypeStruct((t, ATT_W), bf16),
        jax.ShapeDtypeStruct((t, KV_W), f32),
        jax.ShapeDtypeStruct((t, KV_W), f32),
        jax.ShapeDtypeStruct((t, KV_DUP_W), bf16),
        jax.ShapeDtypeStruct((t, KV_DUP_W), bf16),
        jax.ShapeDtypeStruct((t, D_MODEL), bf16),
        jax.ShapeDtypeStruct((t, D_MODEL), bf16),
    )
    return pl.pallas_call(
        functools.partial(_proj_kernel, npt),
        out_shape=out_shape,
        grid=(t // ROW_TILE,),
        in_specs=[
            _prompt_spec(D_MODEL, npt), _sample_spec(D_MODEL, npt),
            _const_spec((1, D_MODEL)), _const_spec((D_MODEL, N_IN)),
            _const_spec((1, N_IN)), _const_spec((1, SGU_WIDTH)), _const_spec((1, SGU_WIDTH)),
            _row_spec(LANES), _row_spec(LANES), _row_spec(LANES),
        ],
        out_specs=(
            _row_spec(SGU_WIDTH), _row_spec(SGU_WIDTH), _sample_spec(SGU_WIDTH, npt),
            _row_spec(ATT_W), _row_spec(KV_W), _row_spec(KV_W), _row_spec(KV_DUP_W),
            _row_spec(KV_DUP_W), _row_spec(D_MODEL), _row_spec(D_MODEL),
        ),
        compiler_params=_params(),
        name="proj",
    )(xp, xs, g_mix.reshape(1, -1), w_in.astype(bf16), b_in.reshape(1, -1),
      ln_g.reshape(1, -1), ln_b.reshape(1, -1), *tables)


def _attend(qa, qb, kwin, vwin, sink, valid):
    lo = _lane_lo(CHUNK)
    zero = jnp.zeros_like(qa)
    lhs = jnp.concatenate([jnp.where(lo, qa, zero), jnp.where(lo, zero, qa),
                           jnp.where(lo, qb, zero), jnp.where(lo, zero, qb)], axis=0)
    s = _dot_nt(lhs, kwin)
    if valid is not None:
        s = jnp.where(valid, s, NEG_INF)
    s_a, s_b = s[:, :LANES], s[:, LANES:]
    tail = s_b.shape[1]
    m = jnp.maximum(jnp.max(s, axis=-1, keepdims=True), sink)
    p_a = jnp.exp(s_a - m)
    p_b = jnp.exp(s_b - m[:, :tail])
    denom = (jnp.sum(jnp.concatenate([p_a, p_b], axis=1), axis=-1, keepdims=True)
             + jnp.exp(sink - m))
    inv = 1.0 / denom
    pn = jnp.concatenate([p_a * inv, p_b * inv[:, :tail]], axis=1)
    r = _dot(_bf16(pn), vwin)
    oa = jnp.where(lo, r[0:CHUNK], r[CHUNK:2 * CHUNK])
    ob = jnp.where(lo, r[2 * CHUNK:3 * CHUNK], r[3 * CHUNK:4 * CHUNK])
    return oa, ob


def _top4(logits):
    rows = logits.shape[0]
    lane = lax.broadcasted_iota(jnp.int32, (rows, N_EXPERTS), 1)
    work = logits
    vals, idxs = [], []
    for _ in range(TOP_K):
        m = jnp.max(work, axis=-1, keepdims=True)
        idx = jnp.min(jnp.where(work == m, lane, N_EXPERTS), axis=-1, keepdims=True)
        vals.append(m)
        idxs.append(idx)
        work = jnp.where(lane == idx, -jnp.inf, work)
    exps = [jnp.exp(v - vals[0]) for v in vals]
    denom = exps[0] + exps[1] + exps[2] + exps[3]
    wide = lax.broadcasted_iota(jnp.int32, (rows, LANES), 1)
    topi = jnp.zeros((rows, LANES), jnp.int32)
    topg = jnp.zeros((rows, LANES), jnp.float32)
    for k in range(TOP_K):
        topi = jnp.where(wide == k, idxs[k], topi)
        topg = jnp.where(wide == k, exps[k] / denom, topg)
    return topi, topg


def _mix_kernel(tiles_per_seq, n_prompt_tiles,
                xp_ref, xs_ref, u_ref, vln_ref, q_ref, kd_ref, vd_ref, kdp_ref, vdp_ref,
                ck_ref, cv_ref, ga_ref, gb_ref, wsp_ref, bsp_ref, sink_ref,
                wpa_ref, wpb_ref, wo_ref, gffn_ref, wrh_ref, wrl_ref, br_ref,
                x1_ref, h2_ref, topi_ref, topg_ref,
                a_s, o_s, kwin_s, vwin_s):
    i = pl.program_id(0)
    n_streams = ROW_TILE // CHUNK

    def sgu_rows(r0, rows):
        ri = lax.broadcasted_iota(jnp.int32, (rows, rows), 0) // CHUNK
        ci = lax.broadcasted_iota(jnp.int32, (rows, rows), 1) // CHUNK
        for g in range(SGU_GROUPS):
            cols = slice(g * LANES, (g + 1) * LANES)
            w = _bf16(jnp.where(ci <= ri, wsp_ref[g, :rows, :rows], 0.0))
            sp = _dot(w, vln_ref[r0:r0 + rows, cols]) + bsp_ref[g, :rows, :]
            a_s[r0:r0 + rows, cols] = _bf16(u_ref[r0:r0 + rows, cols].astype(jnp.float32) * sp)

    def attend_rows(r0, kwin_of, valid):
        for g in range(N_KV_HEADS):
            c0 = g * Q_PER_KV * HEAD_DIM
            kwin, vwin = kwin_of(g)
            oa, ob = _attend(q_ref[r0:r0 + CHUNK, c0:c0 + LANES],
                             q_ref[r0:r0 + CHUNK, c0 + LANES:c0 + 2 * LANES],
                             kwin, vwin, sink_ref[g], valid)
            o_s[r0:r0 + CHUNK, c0:c0 + LANES] = _bf16(oa)
            o_s[r0:r0 + CHUNK, c0 + LANES:c0 + 2 * LANES] = _bf16(ob)

    @pl.when(i < n_prompt_tiles)
    def _prompt():
        for c in range(ROW_TILE // SGU_CHUNK):
            sgu_rows(c * SGU_CHUNK, SGU_CHUNK)
        kwin_s[0:WINDOW] = kdp_ref[...]
        kwin_s[WINDOW:WINDOW + ROW_TILE] = kd_ref[...]
        vwin_s[0:WINDOW] = vdp_ref[...]
        vwin_s[WINDOW:WINDOW + ROW_TILE] = vd_ref[...]
        first = (i % tiles_per_seq) == 0
        col = lax.broadcasted_iota(jnp.int32, (1, KEY_SPAN), 1)
        for j in range(ROW_TILE // CHUNK):
            r0 = j * CHUNK
            valid = jnp.logical_or(jnp.logical_not(first), col + r0 >= WINDOW) if r0 < WINDOW else None

            def kwin_of(g, r0=r0):
                cols = slice(g * LANES, (g + 1) * LANES)
                return kwin_s[r0:r0 + KEY_SPAN, cols], vwin_s[r0:r0 + KEY_SPAN, cols]

            attend_rows(r0, kwin_of, valid)

    @pl.when(i >= n_prompt_tiles)
    def _sample():
        for s in range(n_streams):
            r0 = s * CHUNK
            sgu_rows(r0, CHUNK)
            kwin_s[0:WINDOW] = _dup_heads(ck_ref[s])
            kwin_s[WINDOW:KEY_SPAN] = kd_ref[r0:r0 + CHUNK]
            vwin_s[0:WINDOW] = _dup_heads(cv_ref[s])
            vwin_s[WINDOW:KEY_SPAN] = vd_ref[r0:r0 + CHUNK]

            def kwin_of(g):
                cols = slice(g * LANES, (g + 1) * LANES)
                return kwin_s[0:KEY_SPAN, cols], vwin_s[0:KEY_SPAN, cols]

            attend_rows(r0, kwin_of, None)

    x = jnp.where(i < n_prompt_tiles, xp_ref[...], xs_ref[...])
    m = (ga_ref[...].astype(jnp.float32) * _dot(a_s[...], wpa_ref[...])
         + gb_ref[...].astype(jnp.float32) * _dot(o_s[...], wpb_ref[...]))
    x1 = x + _dot(_bf16(m), wo_ref[...])
    x1_ref[...] = x1
    h2 = _rms(x1, gffn_ref[...])
    _store_row_tiled(h2_ref, (), h2)
    hh = _bf16(h2)
    hl = _bf16(h2 - hh.astype(jnp.float32))
    logits = (_dot(hh, wrh_ref[...]) + _dot(hh, wrl_ref[...]) + _dot(hl, wrh_ref[...])) + br_ref[...]
    topi, topg = _top4(logits)
    topi_ref[...] = topi
    topg_ref[...] = topg


def _mix(xp, xs, u, vln, q, kd, vd, cache_k, cache_v, ga, gb, w_sp, b_sp, sinks,
         w_pa, w_pb, w_o, g_ffn, w_router, b_router, seq):
    tp, ts = xp.shape[0], xs.shape[0]
    t = tp + ts
    npt = tp // ROW_TILE
    tiles_per_seq = seq // ROW_TILE
    f32, bf16 = jnp.float32, jnp.bfloat16
    n_streams = ROW_TILE // CHUNK
    win_per_tile = ROW_TILE // WINDOW

    prev_spec = pl.BlockSpec(
        (WINDOW, KV_DUP_W), lambda i: (jnp.maximum(jnp.minimum(i, npt - 1) * win_per_tile - 1, 0), 0))
    cache_spec = pl.BlockSpec(
        (n_streams, WINDOW, KV_W), lambda i: (jnp.maximum(i - npt, 0), 0, 0))
    sink_cols = jnp.broadcast_to(
        jnp.repeat(sinks.astype(f32).reshape(N_KV_HEADS, Q_PER_KV), CHUNK, axis=1)[:, :, None],
        (N_KV_HEADS, Q_PER_KV * CHUNK, LANES))
    wr_hi = w_router.astype(bf16)
    wr_lo = (w_router - wr_hi.astype(f32)).astype(bf16)
    out_shape = (
        jax.ShapeDtypeStruct((t, D_MODEL), f32),
        jax.ShapeDtypeStruct((t * ROW_SUBTILES, LANES), f32),
        jax.ShapeDtypeStruct((t, LANES), jnp.int32),
        jax.ShapeDtypeStruct((t, LANES), f32),
    )
    return pl.pallas_call(
        functools.partial(_mix_kernel, tiles_per_seq, npt),
        out_shape=out_shape,
        grid=(t // ROW_TILE,),
        in_specs=[
            _prompt_spec(D_MODEL, npt), _sample_spec(D_MODEL, npt),
            _row_spec(SGU_WIDTH), _row_spec(SGU_WIDTH), _row_spec(ATT_W),
            _row_spec(KV_DUP_W), _row_spec(KV_DUP_W), prev_spec, prev_spec,
            cache_spec, cache_spec, _row_spec(D_MODEL), _row_spec(D_MODEL),
            _const_spec((SGU_GROUPS, SGU_CHUNK, SGU_CHUNK)), _const_spec((SGU_GROUPS, SGU_CHUNK, LANES)),
            _const_spec((N_KV_HEADS, Q_PER_KV * CHUNK, LANES)),
            _const_spec((SGU_WIDTH, D_MODEL)), _const_spec((ATT_W, D_MODEL)),
            _const_spec((D_MODEL, D_MODEL)), _const_spec((1, D_MODEL)),
            _const_spec((D_MODEL, N_EXPERTS)), _const_spec((D_MODEL, N_EXPERTS)),
            _const_spec((1, N_EXPERTS)),
        ],
        out_specs=(_row_spec(D_MODEL),
                   pl.BlockSpec((ROW_TILE * ROW_SUBTILES, LANES), lambda i: (i, 0)),
                   _row_spec(LANES), _row_spec(LANES)),
        scratch_shapes=[
            pltpu.VMEM((ROW_TILE, SGU_WIDTH), bf16), pltpu.VMEM((ROW_TILE, ATT_W), bf16),
            pltpu.VMEM((WINDOW + ROW_TILE, KV_DUP_W), bf16),
            pltpu.VMEM((WINDOW + ROW_TILE, KV_DUP_W), bf16),
        ],
        compiler_params=_params(),
        name="mix",
    )(xp, xs, u, vln, q, kd, vd, kd, vd,
      cache_k.reshape(-1, WINDOW, KV_W), cache_v.reshape(-1, WINDOW, KV_W), ga, gb,
      w_sp, jnp.broadcast_to(b_sp[:, :, None], (SGU_GROUPS, SGU_CHUNK, LANES)), sink_cols,
      w_pa.astype(bf16), w_pb.astype(bf16), w_o.astype(bf16),
      g_ffn.reshape(1, -1), wr_hi, wr_lo, b_router.reshape(1, -1))


def _rank_kernel(topi_ref, rank_ref, count_ref, carry_s):
    i = pl.program_id(0)

    @pl.when(i == 0)
    def _():
        carry_s[...] = jnp.zeros_like(carry_s)

    topi = topi_ref[...]
    lane = lax.broadcasted_iota(jnp.int32, (ROW_TILE, LANES), 1)
    onehot = jnp.zeros((ROW_TILE, LANES), jnp.float32)
    for k in range(TOP_K):
        onehot = jnp.where(lane == topi[:, k:k + 1], 1.0, onehot)
    r = lax.broadcasted_iota(jnp.int32, (ROW_TILE, ROW_TILE), 0)
    c = lax.broadcasted_iota(jnp.int32, (ROW_TILE, ROW_TILE), 1)
    below = _bf16(jnp.where(c < r, 1.0, 0.0))
    rank = _dot(below, _bf16(onehot)) + carry_s[0:1, :]
    out = jnp.zeros((ROW_TILE, LANES), jnp.float32)
    for k in range(TOP_K):
        sel = jnp.sum(jnp.where(lane == topi[:, k:k + 1], rank, 0.0), axis=-1, keepdims=True)
        out = jnp.where(lane == k, sel, out)
    rank_ref[...] = out.astype(jnp.int32)
    carry_s[...] = carry_s[...] + jnp.sum(onehot, axis=0, keepdims=True)
    count_ref[...] = carry_s[...].astype(jnp.int32)


def _rank(topi):
    t = topi.shape[0]
    return pl.pallas_call(
        _rank_kernel,
        out_shape=(jax.ShapeDtypeStruct((t, LANES), jnp.int32),
                   jax.ShapeDtypeStruct((8, LANES), jnp.int32)),
        grid=(t // ROW_TILE,),
        in_specs=[_row_spec(LANES)],
        out_specs=(_row_spec(LANES), _const_spec((8, LANES))),
        scratch_shapes=[pltpu.VMEM((8, LANES), jnp.float32)],
        compiler_params=_params(),
        name="rank",
    )(topi)


def _unrolled_rows(n_rows, fn):
    if isinstance(n_rows, int):
        groups, tail_start = n_rows // ROW_UNROLL, n_rows - n_rows % ROW_UNROLL
    else:
        groups = lax.shift_right_logical(n_rows, ROW_UNROLL.bit_length() - 1)
        tail_start = groups * ROW_UNROLL

    def group(gi, carry):
        for lane in range(ROW_UNROLL):
            fn(gi * ROW_UNROLL + lane, lane)
        return carry

    def tail(r, carry):
        fn(r, 0)
        return carry

    lax.fori_loop(0, groups, group, 0)
    lax.fori_loop(tail_start, n_rows, tail, 0)


def _expert_kernel(n_tokens, te_ref, nu_ref, nv_ref,
                   slot_ref, slot_next_ref, h2_hbm, wgu_ref, bgu_ref, wdn_ref, bdn_ref,
                   y_hbm,
                   xbuf, ybuf, wgu_s, wdn_s, gsem, ssem):
    i = pl.program_id(0)
    n_used = nu_ref[0]
    slot = i % 2

    def tile_of(row):
        return pl.ds(pl.multiple_of(row * ROW_SUBTILES, ROW_SUBTILES), ROW_SUBTILES)

    def start_gather(idx_ref, buf_slot):
        def one(r, lane):
            tok = lax.shift_right_logical(idx_ref[0, 0, r], TOP_K.bit_length() - 1)
            pltpu.make_async_copy(h2_hbm.at[tile_of(tok)], xbuf.at[buf_slot, tile_of(r)],
                                  gsem.at[buf_slot]).start(priority=lane % 2)
        _unrolled_rows(MOE_TILE, one)

    @pl.when(jnp.logical_and(i == 0, n_used > 0))
    def _():
        start_gather(slot_ref, 0)

    @pl.when(i + 1 < n_used)
    def _():
        start_gather(slot_next_ref, 1 - slot)

    expert_changed = jnp.logical_or(i == 0, te_ref[i] != te_ref[jnp.maximum(i - 1, 0)])

    @pl.when(jnp.logical_and(i < n_used, expert_changed))
    def _():
        wgu_s[...] = _bf16(wgu_ref[0])
        wdn_s[...] = _bf16(wdn_ref[0])

    @pl.when(i < n_used)
    def _():
        pltpu.make_async_copy(h2_hbm.at[pl.ds(0, MOE_TILE * ROW_SUBTILES)], xbuf.at[slot],
                              gsem.at[slot]).wait()
        x = _bf16(_load_row_tiled(xbuf, (slot,), MOE_TILE))
        gu = _dot(x, wgu_s[...]) + bgu_ref[0]
        gate = jnp.minimum(gu[:, :D_FF], SWIGLU_LIMIT)
        lin = jnp.clip(gu[:, D_FF:], -SWIGLU_LIMIT, SWIGLU_LIMIT)
        act = gate * _sigmoid(SWIGLU_ALPHA * gate) * (lin + 1.0)
        _store_row_tiled(ybuf, (slot,), _dot(_bf16(act), wdn_s[...]) + bdn_ref[0])

        def one(r, lane):
            packed = slot_ref[0, 0, r]
            tok = lax.shift_right_logical(packed, TOP_K.bit_length() - 1)
            row = (packed & (TOP_K - 1)) * n_tokens + tok
            pltpu.make_async_copy(ybuf.at[slot, tile_of(r)], y_hbm.at[tile_of(row)],
                                  ssem.at[slot]).start(priority=lane % 2)
        _unrolled_rows(nv_ref[i], one)

    n_prev = nv_ref[jnp.maximum(i - 1, 0)]

    @pl.when(jnp.logical_and(jnp.logical_and(i >= 1, i <= n_used), n_prev > 0))
    def _():
        span = pl.ds(0, pl.multiple_of(n_prev * ROW_SUBTILES, ROW_SUBTILES))
        pltpu.make_async_copy(ybuf.at[1 - slot, span], y_hbm.at[span], ssem.at[1 - slot]).wait()


def _experts(tile_expert, n_used, n_valid, slots, h2, w_gu, b_gu, w_dn, b_dn):
    n_tiles = slots.shape[0]
    t = h2.shape[0] // ROW_SUBTILES
    f32, bf16 = jnp.float32, jnp.bfloat16
    idx_spec = pl.BlockSpec((1, 1, MOE_TILE), lambda i, te, nu, nv: (i, 0, 0), memory_space=pltpu.SMEM)
    idx_next_spec = pl.BlockSpec(
        (1, 1, MOE_TILE), lambda i, te, nu, nv: (jnp.minimum(i + 1, n_tiles - 1), 0, 0),
        memory_space=pltpu.SMEM)
    grid_spec = pltpu.PrefetchScalarGridSpec(
        num_scalar_prefetch=3,
        grid=(n_tiles,),
        in_specs=[
            idx_spec, idx_next_spec,
            pl.BlockSpec(memory_space=pl.ANY),
            pl.BlockSpec((1, D_MODEL, 2 * D_FF), lambda i, te, nu, nv: (te[i], 0, 0)),
            pl.BlockSpec((1, 1, 2 * D_FF), lambda i, te, nu, nv: (te[i], 0, 0)),
            pl.BlockSpec((1, D_FF, D_MODEL), lambda i, te, nu, nv: (te[i], 0, 0)),
            pl.BlockSpec((1, 1, D_MODEL), lambda i, te, nu, nv: (te[i], 0, 0)),
        ],
        out_specs=pl.BlockSpec(memory_space=pl.ANY),
        scratch_shapes=[
            pltpu.VMEM((2, MOE_TILE * ROW_SUBTILES, LANES), f32),
            pltpu.VMEM((2, MOE_TILE * ROW_SUBTILES, LANES), f32),
            pltpu.VMEM((D_MODEL, 2 * D_FF), bf16), pltpu.VMEM((D_FF, D_MODEL), bf16),
            pltpu.SemaphoreType.DMA((2,)), pltpu.SemaphoreType.DMA((2,)),
        ],
    )
    return pl.pallas_call(
        functools.partial(_expert_kernel, t),
        out_shape=jax.ShapeDtypeStruct((t * TOP_K * ROW_SUBTILES, LANES), f32),
        grid_spec=grid_spec,
        compiler_params=_params(),
        name="experts",
    )(tile_expert, n_used, n_valid, slots, slots, h2, w_gu,
      b_gu.reshape(N_EXPERTS, 1, -1), w_dn, b_dn.reshape(N_EXPERTS, 1, -1))


def _combine_kernel(n_prompt_tiles, x1_ref, y0_ref, y1_ref, y2_ref, y3_ref, g_ref, gfin_ref,
                    yp_ref, ys_ref):
    i = pl.program_id(0)
    g = g_ref[...]
    acc = x1_ref[...]
    for k, y_ref in enumerate((y0_ref, y1_ref, y2_ref, y3_ref)):
        acc = acc + g[:, k:k + 1] * _load_row_tiled(y_ref, (), ROW_TILE)
    out = _rms(acc, gfin_ref[...])

    @pl.when(i < n_prompt_tiles)
    def _():
        yp_ref[...] = out

    @pl.when(i >= n_prompt_tiles)
    def _():
        ys_ref[...] = out


def _combine(x1, y_tk, topg, g_final, tp):
    t = x1.shape[0]
    npt = tp // ROW_TILE
    nt = t // ROW_TILE
    f32 = jnp.float32
    y_specs = [pl.BlockSpec((ROW_TILE * ROW_SUBTILES, LANES), lambda i, k=k: (k * nt + i, 0))
               for k in range(TOP_K)]
    return pl.pallas_call(
        functools.partial(_combine_kernel, npt),
        out_shape=(jax.ShapeDtypeStruct((tp, D_MODEL), f32),
                   jax.ShapeDtypeStruct((t - tp, D_MODEL), f32)),
        grid=(nt,),
        in_specs=[_row_spec(D_MODEL)] + y_specs + [_row_spec(LANES), _const_spec((1, D_MODEL))],
        out_specs=(_prompt_spec(D_MODEL, npt), _sample_spec(D_MODEL, npt)),
        compiler_params=_params(),
        name="combine",
    )(x1, y_tk, y_tk, y_tk, y_tk, topg, g_final.reshape(1, -1))


def _slot_kernel(dest_ref, slots_ref):
    i = pl.program_id(0)

    @pl.when(i == 0)
    def _():
        def zero(r, lane):
            slots_ref[r] = 0
        _unrolled_rows(slots_ref.shape[0], zero)

    base = i * SLOT_CHUNK

    def put(r, lane):
        slots_ref[dest_ref[0, 0, r]] = base + r
    _unrolled_rows(SLOT_CHUNK, put)


def _slot_map(dest, n_slots):
    n_assign = dest.shape[0]
    n_chunks = n_assign // SLOT_CHUNK
    return pl.pallas_call(
        _slot_kernel,
        out_shape=jax.ShapeDtypeStruct((n_slots,), jnp.int32),
        grid=(n_chunks,),
        in_specs=[pl.BlockSpec((1, 1, SLOT_CHUNK), lambda i: (i, 0, 0), memory_space=pltpu.SMEM)],
        out_specs=pl.BlockSpec(memory_space=pltpu.SMEM),
        compiler_params=_params(),
        name="slot_map",
    )(dest.reshape(n_chunks, 1, SLOT_CHUNK))


def _plan(topi, rank, counts, t):
    n_assign = t * TOP_K
    n_tiles = (n_assign + N_EXPERTS * (MOE_TILE - 1)) // MOE_TILE + 1
    counts = counts[0, :N_EXPERTS]
    tiles_e = (counts + MOE_TILE - 1) // MOE_TILE
    tile_end = jnp.cumsum(tiles_e)
    tile_start = tile_end - tiles_e
    n_used = tile_end[-1]
    dest = (tile_start * MOE_TILE)[topi[:, :TOP_K]] + rank[:, :TOP_K]
    slots = _slot_map(dest.reshape(-1), n_tiles * MOE_TILE)
    tile_ids = jnp.arange(n_tiles, dtype=jnp.int32)
    live = jnp.minimum(tile_ids, n_used - 1)
    tile_expert = jnp.sum(tile_end[None, :] <= live[:, None], axis=1).astype(jnp.int32)
    n_valid = jnp.clip(counts[tile_expert] - (tile_ids - tile_start[tile_expert]) * MOE_TILE,
                       0, MOE_TILE)
    n_valid = jnp.where(tile_ids < n_used, n_valid, 0).astype(jnp.int32)
    return (tile_expert, n_used.reshape(1).astype(jnp.int32), n_valid,
            slots.reshape(n_tiles, 1, MOE_TILE))


def kernel(x_prompt, x_sample, cache_k, cache_v, g_mix, w_in, b_in, ln_v_g, ln_v_b, w_sp, b_sp,
           attn_sinks, w_pa, w_pb, w_o, g_ffn, w_router, b_router, w_gu, b_gu, w_dn, b_dn, g_final):
    nb, seq, d = x_prompt.shape
    nsb, nnew, _ = x_sample.shape
    tp, ts = nb * seq, nsb * nnew
    t = tp + ts
    xp = x_prompt.reshape(tp, d)
    xs = x_sample.reshape(ts, d)
    pos = np.concatenate([np.tile(np.arange(seq), nb), np.tile(PAST_LEN + np.arange(nnew), nsb)])
    tables = _rope_tables(pos)
    u, vln, v_sgu, q, k, v, kd, vd, ga, gb = _project(
        xp, xs, g_mix[0], w_in[0], b_in[0], ln_v_g[0], ln_v_b[0], tables)
    x1, h2, topi, topg = _mix(
        xp, xs, u, vln, q, kd, vd, cache_k[0], cache_v[0], ga, gb, w_sp[0], b_sp[0], attn_sinks[0],
        w_pa[0], w_pb[0], w_o[0], g_ffn[0], w_router[0], b_router[0], seq)
    rank, counts = _rank(topi)
    tile_expert, n_used, n_valid, slots = _plan(topi, rank, counts, t)
    y_tk = _experts(tile_expert, n_used, n_valid, slots, h2, w_gu[0], b_gu[0], w_dn[0], b_dn[0])
    y_p, y_s = _combine(x1, y_tk, topg, g_final, tp)

    keep = min(WINDOW, seq)
    kp = k[:tp].reshape(nb, seq, N_KV_HEADS, HEAD_DIM)[:, seq - keep:]
    vp = v[:tp].reshape(nb, seq, N_KV_HEADS, HEAD_DIM)[:, seq - keep:]
    ks = k[tp:].reshape(nsb, nnew, N_KV_HEADS, HEAD_DIM)
    vs = v[tp:].reshape(nsb, nnew, N_KV_HEADS, HEAD_DIM)
    return (y_p.reshape(nb, seq, d), y_s.reshape(nsb, nnew, d), kp[None], vp[None], ks[None],
            vs[None], v_sgu.reshape(1, nsb, nnew, SGU_WIDTH))
```

```python
import functools

import numpy as np
import jax
import jax.numpy as jnp
from jax import lax
from jax.experimental import pallas as pl
from jax.experimental.pallas import tpu as pltpu

D_MODEL = 1024
PAST_LEN = 2048
CHUNK = 64
SGU_CHUNK = 128
SGU_GROUPS = 8
SGU_WIDTH = 1024
N_HEADS = 16
N_KV_HEADS = 4
HEAD_DIM = 64
Q_PER_KV = N_HEADS // N_KV_HEADS
WINDOW = 128
ROT_DIM = HEAD_DIM // 4
ROPE_THETA = 500000.0
ATT_W = N_HEADS * HEAD_DIM
KV_W = N_KV_HEADS * HEAD_DIM
N_EXPERTS = 32
TOP_K = 4
D_FF = 1024
SWIGLU_ALPHA = 1.702
SWIGLU_LIMIT = 7.0
NORM_EPS = 1e-5
NEG_INF = -1e30
N_IN = SGU_WIDTH * 2 + ATT_W + KV_W * 2 + D_MODEL * 2

LANES = 128
ROW_TILE = 256
MOE_TILE = 256
ROW_UNROLL = 8
KV_DUP_W = N_KV_HEADS * LANES
KEY_SPAN = WINDOW + CHUNK
VMEM_LIMIT = 56 * 1024 * 1024

_SQRT_HALF = 0.7071067811865476


def _gelu(x):
    z = jnp.abs(x) * _SQRT_HALF
    t = 1.0 / (1.0 + 0.3275911 * z)
    poly = t * (0.254829592 + t * (-0.284496736 + t * (1.421413741
                + t * (-1.453152027 + t * 1.061405429))))
    half_tail = (0.5 * x) * (poly * jnp.exp(-z * z))
    return jnp.where(x >= 0.0, x - half_tail, half_tail)


def _sigmoid(x):
    return 1.0 / (1.0 + jnp.exp(-x))


def _bf16(x):
    return x.astype(jnp.bfloat16)


def _dot(a, b):
    return jnp.dot(a, b, preferred_element_type=jnp.float32)


ROW_SUBTILES = D_MODEL // LANES


def _store_row_tiled(ref, lead, x):
    rows = x.shape[0]
    for s in range(ROW_SUBTILES):
        ref[(*lead, pl.ds(s, rows, stride=ROW_SUBTILES), slice(None))] = x[:, s * LANES:(s + 1) * LANES]


def _load_row_tiled(ref, lead, rows):
    return jnp.concatenate(
        [ref[(*lead, pl.ds(s, rows, stride=ROW_SUBTILES), slice(None))] for s in range(ROW_SUBTILES)],
        axis=1)


def _dot_nt(a, b):
    return lax.dot_general(a, b, (((1,), (1,)), ((), ())), preferred_element_type=jnp.float32)


def _rms(x, g):
    return x * lax.rsqrt(jnp.mean(x * x, axis=-1, keepdims=True) + NORM_EPS) * g


def _lane_lo(rows):
    return lax.broadcasted_iota(jnp.int32, (rows, LANES), 1) < HEAD_DIM


def _dup_heads(kv):
    rows = kv.shape[0]
    lo = _lane_lo(rows)
    out = []
    for j in range(KV_W // LANES):
        blk = kv[:, j * LANES:(j + 1) * LANES]
        swp = pltpu.roll(blk, HEAD_DIM, axis=1)
        out.append(jnp.where(lo, blk, swp))
        out.append(jnp.where(lo, swp, blk))
    return _bf16(jnp.concatenate(out, axis=1))


def _rope_block(zb, cos_b, sin_lo, sin_hi):
    up = pltpu.roll(zb, LANES - ROT_DIM // 2, axis=1)
    dn = pltpu.roll(zb, ROT_DIM // 2, axis=1)
    return zb * cos_b + up * sin_lo + dn * sin_hi


def _proj_kernel(n_prompt_tiles, xp_ref, xs_ref, gmix_ref, w_ref, b_ref, lng_ref, lnb_ref,
                 cos_ref, slo_ref, shi_ref,
                 u_ref, vln_ref, vs_ref, q_ref, k_ref, v_ref, kd_ref, vd_ref, ga_ref, gb_ref):
    i = pl.program_id(0)
    x = jnp.where(i < n_prompt_tiles, xp_ref[...], xs_ref[...])
    h = _bf16(_rms(x, gmix_ref[...]))

    def seg(lo, width):
        return _dot(h, w_ref[:, lo:lo + width]) + b_ref[:, lo:lo + width]

    o = 0
    u_ref[...] = _bf16(_gelu(seg(o, SGU_WIDTH)))
    o += SGU_WIDTH
    gv = _gelu(seg(o, SGU_WIDTH))
    gc = gv - jnp.mean(gv, axis=-1, keepdims=True)
    var = jnp.mean(gc * gc, axis=-1, keepdims=True)
    vln = gc * lax.rsqrt(var + NORM_EPS) * lng_ref[...] + lnb_ref[...]
    vln_ref[...] = _bf16(vln)

    @pl.when(i >= n_prompt_tiles)
    def _():
        vs_ref[...] = vln

    o += SGU_WIDTH
    cos_b, sin_lo, sin_hi = cos_ref[...], slo_ref[...], shi_ref[...]
    zq = seg(o, ATT_W)
    for j in range(ATT_W // LANES):
        blk = _rope_block(zq[:, j * LANES:(j + 1) * LANES], cos_b, sin_lo, sin_hi)
        q_ref[:, j * LANES:(j + 1) * LANES] = _bf16(blk * (HEAD_DIM ** -0.5))
    o += ATT_W
    zk = seg(o, KV_W)
    kr = jnp.concatenate(
        [_rope_block(zk[:, j * LANES:(j + 1) * LANES], cos_b, sin_lo, sin_hi)
         for j in range(KV_W // LANES)], axis=1)
    k_ref[...] = kr
    kd_ref[...] = _dup_heads(kr)
    o += KV_W
    zv = seg(o, KV_W)
    v_ref[...] = zv
    vd_ref[...] = _dup_heads(zv)
    o += KV_W
    ga_ref[...] = _bf16(_sigmoid(seg(o, D_MODEL)))
    o += D_MODEL
    gb_ref[...] = _bf16(_sigmoid(seg(o, D_MODEL)))


def _rope_tables(pos):
    half = ROT_DIM // 2
    inv = np.float32(ROPE_THETA) ** (-np.arange(half, dtype=np.float32) * np.float32(2.0) / ROT_DIM)
    ang = pos.astype(np.float32)[:, None] * inv.astype(np.float32)[None, :]
    cos = jnp.asarray(np.cos(ang.astype(np.float64)).astype(np.float32))
    sin = jnp.asarray(np.sin(ang.astype(np.float64)).astype(np.float32))
    n = pos.shape[0]
    ones = jnp.ones((n, HEAD_DIM - ROT_DIM), jnp.float32)
    zeros = jnp.zeros((n, HEAD_DIM - ROT_DIM), jnp.float32)
    zh = jnp.zeros((n, half), jnp.float32)
    cos_h = jnp.concatenate([cos, cos, ones], axis=1)
    slo_h = jnp.concatenate([-sin, zh, zeros], axis=1)
    shi_h = jnp.concatenate([zh, sin, zeros], axis=1)
    rep = LANES // HEAD_DIM
    return (jnp.tile(cos_h, (1, rep)), jnp.tile(slo_h, (1, rep)), jnp.tile(shi_h, (1, rep)))


def _row_spec(width):
    return pl.BlockSpec((ROW_TILE, width), lambda i: (i, 0))


def _const_spec(shape):
    return pl.BlockSpec(shape, lambda i: (0,) * len(shape))


def _prompt_spec(width, n_prompt_tiles):
    return pl.BlockSpec((ROW_TILE, width), lambda i: (jnp.minimum(i, n_prompt_tiles - 1), 0))


def _sample_spec(width, n_prompt_tiles):
    return pl.BlockSpec((ROW_TILE, width), lambda i: (jnp.maximum(i - n_prompt_tiles, 0), 0))


def _params():
    return pltpu.CompilerParams(dimension_semantics=("arbitrary",), vmem_limit_bytes=VMEM_LIMIT)


def _project(xp, xs, g_mix, w_in, b_in, ln_g, ln_b, tables):
    tp, ts = xp.shape[0], xs.shape[0]
    t = tp + ts
    npt = tp // ROW_TILE
    f32, bf16 = jnp.float32, jnp.bfloat16
    out_shape = (
        jax.ShapeDtypeStruct((t, SGU_WIDTH), bf16),
        jax.ShapeDtypeStruct((t, SGU_WIDTH), bf16),
        jax.ShapeDtypeStruct((ts, SGU_WIDTH), f32),
        jax.ShapeDtypeStruct((t, ATT_W), bf16),
        jax.ShapeDtypeStruct((t, KV_W), f32),
        jax.ShapeDtypeStruct((t, KV_W), f32),
        jax.ShapeDtypeStruct((t, KV_DUP_W), bf16),
        jax.ShapeDtypeStruct((t, KV_DUP_W), bf16),
        jax.ShapeDtypeStruct((t, D_MODEL), bf16),
        jax.ShapeDtypeStruct((t, D_MODEL), bf16),
    )
    return pl.pallas_call(
        functools.partial(_proj_kernel, npt),
        out_shape=out_shape,
        grid=(t // ROW_TILE,),
        in_specs=[
            _prompt_spec(D_MODEL, npt), _sample_spec(D_MODEL, npt),
            _const_spec((1, D_MODEL)), _const_spec((D_MODEL, N_IN)),
            _const_spec((1, N_IN)), _const_spec((1, SGU_WIDTH)), _const_spec((1, SGU_WIDTH)),
            _row_spec(LANES), _row_spec(LANES), _row_spec(LANES),
        ],
        out_specs=(
            _row_spec(SGU_WIDTH), _row_spec(SGU_WIDTH), _sample_spec(SGU_WIDTH, npt),
            _row_spec(ATT_W), _row_spec(KV_W), _row_spec(KV_W), _row_spec(KV_DUP_W),
            _row_spec(KV_DUP_W), _row_spec(D_MODEL), _row_spec(D_MODEL),
        ),
        compiler_params=_params(),
        name="proj",
    )(xp, xs, g_mix.reshape(1, -1), w_in.astype(bf16), b_in.reshape(1, -1),
      ln_g.reshape(1, -1), ln_b.reshape(1, -1), *tables)


def _attend(qa, qb, kwin, vwin, sink, valid):
    lo = _lane_lo(CHUNK)
    zero = jnp.zeros_like(qa)
    lhs = jnp.concatenate([jnp.where(lo, qa, zero), jnp.where(lo, zero, qa),
                           jnp.where(lo, qb, zero), jnp.where(lo, zero, qb)], axis=0)
    s = _dot_nt(lhs, kwin)
    if valid is not None:
        s = jnp.where(valid, s, NEG_INF)
    s_a, s_b = s[:, :LANES], s[:, LANES:]
    tail = s_b.shape[1]
    m = jnp.maximum(jnp.max(s, axis=-1, keepdims=True), sink)
    p_a = jnp.exp(s_a - m)
    p_b = jnp.exp(s_b - m[:, :tail])
    denom = (jnp.sum(jnp.concatenate([p_a, p_b], axis=1), axis=-1, keepdims=True)
             + jnp.exp(sink - m))
    inv = 1.0 / denom
    pn = jnp.concatenate([p_a * inv, p_b * inv[:, :tail]], axis=1)
    r = _dot(_bf16(pn), vwin)
    oa = jnp.where(lo, r[0:CHUNK], r[CHUNK:2 * CHUNK])
    ob = jnp.where(lo, r[2 * CHUNK:3 * CHUNK], r[3 * CHUNK:4 * CHUNK])
    return oa, ob


def _top4(logits):
    rows = logits.shape[0]
    lane = lax.broadcasted_iota(jnp.int32, (rows, N_EXPERTS), 1)
    work = logits
    vals, idxs = [], []
    for _ in range(TOP_K):
        m = jnp.max(work, axis=-1, keepdims=True)
        idx = jnp.min(jnp.where(work == m, lane, N_EXPERTS), axis=-1, keepdims=True)
        vals.append(m)
        idxs.append(idx)
        work = jnp.where(lane == idx, -jnp.inf, work)
    exps = [jnp.exp(v - vals[0]) for v in vals]
    denom = exps[0] + exps[1] + exps[2] + exps[3]
    wide = lax.broadcasted_iota(jnp.int32, (rows, LANES), 1)
    topi = jnp.zeros((rows, LANES), jnp.int32)
    topg = jnp.zeros((rows, LANES), jnp.float32)
    for k in range(TOP_K):
        topi = jnp.where(wide == k, idxs[k], topi)
        topg = jnp.where(wide == k, exps[k] / denom, topg)
    return topi, topg


def _mix_kernel(tiles_per_seq, n_prompt_tiles,
                xp_ref, xs_ref, u_ref, vln_ref, q_ref, kd_ref, vd_ref, kdp_ref, vdp_ref,
                ck_ref, cv_ref, ga_ref, gb_ref, wsp_ref, bsp_ref, sink_ref,
                wpa_ref, wpb_ref, wo_ref, gffn_ref, wrh_ref, wrl_ref, br_ref,
                x1_ref, h2_ref, topi_ref, topg_ref,
                a_s, o_s, kwin_s, vwin_s):
    i = pl.program_id(0)
    n_streams = ROW_TILE // CHUNK

    def sgu_rows(r0, rows):
        ri = lax.broadcasted_iota(jnp.int32, (rows, rows), 0) // CHUNK
        ci = lax.broadcasted_iota(jnp.int32, (rows, rows), 1) // CHUNK
        for g in range(SGU_GROUPS):
            cols = slice(g * LANES, (g + 1) * LANES)
            w = _bf16(jnp.where(ci <= ri, wsp_ref[g, :rows, :rows], 0.0))
            sp = _dot(w, vln_ref[r0:r0 + rows, cols]) + bsp_ref[g, :rows, :]
            a_s[r0:r0 + rows, cols] = _bf16(u_ref[r0:r0 + rows, cols].astype(jnp.float32) * sp)

    def attend_rows(r0, kwin_of, valid):
        for g in range(N_KV_HEADS):
            c0 = g * Q_PER_KV * HEAD_DIM
            kwin, vwin = kwin_of(g)
            oa, ob = _attend(q_ref[r0:r0 + CHUNK, c0:c0 + LANES],
                             q_ref[r0:r0 + CHUNK, c0 + LANES:c0 + 2 * LANES],
                             kwin, vwin, sink_ref[g], valid)
            o_s[r0:r0 + CHUNK, c0:c0 + LANES] = _bf16(oa)
            o_s[r0:r0 + CHUNK, c0 + LANES:c0 + 2 * LANES] = _bf16(ob)

    @pl.when(i < n_prompt_tiles)
    def _prompt():
        for c in range(ROW_TILE // SGU_CHUNK):
            sgu_rows(c * SGU_CHUNK, SGU_CHUNK)
        kwin_s[0:WINDOW] = kdp_ref[...]
        kwin_s[WINDOW:WINDOW + ROW_TILE] = kd_ref[...]
        vwin_s[0:WINDOW] = vdp_ref[...]
        vwin_s[WINDOW:WINDOW + ROW_TILE] = vd_ref[...]
        first = (i % tiles_per_seq) == 0
        col = lax.broadcasted_iota(jnp.int32, (1, KEY_SPAN), 1)
        for j in range(ROW_TILE // CHUNK):
            r0 = j * CHUNK
            valid = jnp.logical_or(jnp.logical_not(first), col + r0 >= WINDOW) if r0 < WINDOW else None

            def kwin_of(g, r0=r0):
                cols = slice(g * LANES, (g + 1) * LANES)
                return kwin_s[r0:r0 + KEY_SPAN, cols], vwin_s[r0:r0 + KEY_SPAN, cols]

            attend_rows(r0, kwin_of, valid)

    @pl.when(i >= n_prompt_tiles)
    def _sample():
        for s in range(n_streams):
            r0 = s * CHUNK
            sgu_rows(r0, CHUNK)
            kwin_s[0:WINDOW] = _dup_heads(ck_ref[s])
            kwin_s[WINDOW:KEY_SPAN] = kd_ref[r0:r0 + CHUNK]
            vwin_s[0:WINDOW] = _dup_heads(cv_ref[s])
            vwin_s[WINDOW:KEY_SPAN] = vd_ref[r0:r0 + CHUNK]

            def kwin_of(g):
                cols = slice(g * LANES, (g + 1) * LANES)
                return kwin_s[0:KEY_SPAN, cols], vwin_s[0:KEY_SPAN, cols]

            attend_rows(r0, kwin_of, None)

    x = jnp.where(i < n_prompt_tiles, xp_ref[...], xs_ref[...])
    m = (ga_ref[...].astype(jnp.float32) * _dot(a_s[...], wpa_ref[...])
         + gb_ref[...].astype(jnp.float32) * _dot(o_s[...], wpb_ref[...]))
    x1 = x + _dot(_bf16(m), wo_ref[...])
    x1_ref[...] = x1
    h2 = _rms(x1, gffn_ref[...])
    _store_row_tiled(h2_ref, (), h2)
    hh = _bf16(h2)
    hl = _bf16(h2 - hh.astype(jnp.float32))
    logits = (_dot(hh, wrh_ref[...]) + _dot(hh, wrl_ref[...]) + _dot(hl, wrh_ref[...])) + br_ref[...]
    topi, topg = _top4(logits)
    topi_ref[...] = topi
    topg_ref[...] = topg


def _mix(xp, xs, u, vln, q, kd, vd, cache_k, cache_v, ga, gb, w_sp, b_sp, sinks,
         w_pa, w_pb, w_o, g_ffn, w_router, b_router, seq):
    tp, ts = xp.shape[0], xs.shape[0]
    t = tp + ts
    npt = tp // ROW_TILE
    tiles_per_seq = seq // ROW_TILE
    f32, bf16 = jnp.float32, jnp.bfloat16
    n_streams = ROW_TILE // CHUNK
    win_per_tile = ROW_TILE // WINDOW

    prev_spec = pl.BlockSpec(
        (WINDOW, KV_DUP_W), lambda i: (jnp.maximum(jnp.minimum(i, npt - 1) * win_per_tile - 1, 0), 0))
    cache_spec = pl.BlockSpec(
        (n_streams, WINDOW, KV_W), lambda i: (jnp.maximum(i - npt, 0), 0, 0))
    sink_cols = jnp.broadcast_to(
        jnp.repeat(sinks.astype(f32).reshape(N_KV_HEADS, Q_PER_KV), CHUNK, axis=1)[:, :, None],
        (N_KV_HEADS, Q_PER_KV * CHUNK, LANES))
    wr_hi = w_router.astype(bf16)
    wr_lo = (w_router - wr_hi.astype(f32)).astype(bf16)
    out_shape = (
        jax.ShapeDtypeStruct((t, D_MODEL), f32),
        jax.ShapeDtypeStruct((t * ROW_SUBTILES, LANES), f32),
        jax.ShapeDtypeStruct((t, LANES), jnp.int32),
        jax.ShapeDtypeStruct((t, LANES), f32),
    )
    return pl.pallas_call(
        functools.partial(_mix_kernel, tiles_per_seq, npt),
        out_shape=out_shape,
        grid=(t // ROW_TILE,),
        in_specs=[
            _prompt_spec(D_MODEL, npt), _sample_spec(D_MODEL, npt),
            _row_spec(SGU_WIDTH), _row_spec(SGU_WIDTH), _row_spec(ATT_W),
            _row_spec(KV_DUP_W), _row_spec(KV_DUP_W), prev_spec, prev_spec,
            cache_spec, cache_spec, _row_spec(D_MODEL), _row_spec(D_MODEL),
            _const_spec((SGU_GROUPS, SGU_CHUNK, SGU_CHUNK)), _const_spec((SGU_GROUPS, SGU_CHUNK, LANES)),
            _const_spec((N_KV_HEADS, Q_PER_KV * CHUNK, LANES)),
            _const_spec((SGU_WIDTH, D_MODEL)), _const_spec((ATT_W, D_MODEL)),
            _const_spec((D_MODEL, D_MODEL)), _const_spec((1, D_MODEL)),
            _const_spec((D_MODEL, N_EXPERTS)), _const_spec((D_MODEL, N_EXPERTS)),
            _const_spec((1, N_EXPERTS)),
        ],
        out_specs=(_row_spec(D_MODEL),
                   pl.BlockSpec((ROW_TILE * ROW_SUBTILES, LANES), lambda i: (i, 0)),
                   _row_spec(LANES), _row_spec(LANES)),
        scratch_shapes=[
            pltpu.VMEM((ROW_TILE, SGU_WIDTH), bf16), pltpu.VMEM((ROW_TILE, ATT_W), bf16),
            pltpu.VMEM((WINDOW + ROW_TILE, KV_DUP_W), bf16),
            pltpu.VMEM((WINDOW + ROW_TILE, KV_DUP_W), bf16),
        ],
        compiler_params=_params(),
        name="mix",
    )(xp, xs, u, vln, q, kd, vd, kd, vd,
      cache_k.reshape(-1, WINDOW, KV_W), cache_v.reshape(-1, WINDOW, KV_W), ga, gb,
      w_sp, jnp.broadcast_to(b_sp[:, :, None], (SGU_GROUPS, SGU_CHUNK, LANES)), sink_cols,
      w_pa.astype(bf16), w_pb.astype(bf16), w_o.astype(bf16),
      g_ffn.reshape(1, -1), wr_hi, wr_lo, b_router.reshape(1, -1))


def _rank_kernel(topi_ref, topg_ref, lst_ref, gt_ref, meta_ref, count_ref, carry_s):
    i = pl.program_id(0)

    @pl.when(i == 0)
    def _():
        carry_s[...] = jnp.zeros_like(carry_s)

    topi = topi_ref[...]
    lane = lax.broadcasted_iota(jnp.int32, (ROW_TILE, LANES), 1)
    onehot = jnp.zeros((ROW_TILE, LANES), jnp.float32)
    for k in range(TOP_K):
        onehot = jnp.where(lane == topi[:, k:k + 1], 1.0, onehot)
    r = lax.broadcasted_iota(jnp.int32, (ROW_TILE, ROW_TILE), 0)
    c = lax.broadcasted_iota(jnp.int32, (ROW_TILE, ROW_TILE), 1)
    below = _bf16(jnp.where(c < r, 1.0, 0.0))
    in_tile = _dot(below, _bf16(onehot))
    count = jnp.sum(onehot, axis=0, keepdims=True)
    er = lax.broadcasted_iota(jnp.int32, (LANES, LANES), 0)
    ec = lax.broadcasted_iota(jnp.int32, (LANES, LANES), 1)
    before = _bf16(jnp.where(er < ec, 1.0, 0.0))
    start = _dot(_bf16(jnp.broadcast_to(count, (8, LANES))), before)[0:1, :]
    local = in_tile + start
    slot = jnp.full((ROW_TILE, LANES), -1.0, jnp.float32)
    for k in range(TOP_K):
        sel = jnp.sum(jnp.where(lane == topi[:, k:k + 1], local, 0.0), axis=-1, keepdims=True)
        slot = jnp.where(lane == k, sel, slot)
    lst_ref[...] = slot.T[0:8, :].astype(jnp.int32)
    gt_ref[...] = topg_ref[...].T[0:8, :]
    row = lax.broadcasted_iota(jnp.int32, (8, LANES), 0)
    meta = jnp.where(row == 0, carry_s[...], jnp.where(row == 1, count, jnp.where(row == 2, start, 0.0)))
    meta_ref[...] = meta.astype(jnp.int32)
    carry_s[...] = carry_s[...] + count
    count_ref[...] = carry_s[...].astype(jnp.int32)


def _rank(topi, topg):
    t = topi.shape[0]
    nt = t // ROW_TILE
    tile8 = lambda width: pl.BlockSpec((8, width), lambda i: (i, 0))
    return pl.pallas_call(
        _rank_kernel,
        out_shape=(jax.ShapeDtypeStruct((nt * 8, ROW_TILE), jnp.int32),
                   jax.ShapeDtypeStruct((nt * 8, ROW_TILE), jnp.float32),
                   jax.ShapeDtypeStruct((nt * 8, LANES), jnp.int32),
                   jax.ShapeDtypeStruct((8, LANES), jnp.int32)),
        grid=(nt,),
        in_specs=[_row_spec(LANES), _row_spec(LANES)],
        out_specs=(tile8(ROW_TILE), tile8(ROW_TILE), tile8(LANES), _const_spec((8, LANES))),
        scratch_shapes=[pltpu.VMEM((8, LANES), jnp.float32)],
        compiler_params=_params(),
        name="rank",
    )(topi, topg)


def _unrolled_rows(n_rows, fn):
    if isinstance(n_rows, int):
        groups, tail_start = n_rows // ROW_UNROLL, n_rows - n_rows % ROW_UNROLL
    else:
        groups = lax.shift_right_logical(n_rows, ROW_UNROLL.bit_length() - 1)
        tail_start = groups * ROW_UNROLL

    def group(gi, carry):
        for lane in range(ROW_UNROLL):
            fn(gi * ROW_UNROLL + lane, lane)
        return carry

    def tail(r, carry):
        fn(r, 0)
        return carry

    lax.fori_loop(0, groups, group, 0)
    lax.fori_loop(tail_start, n_rows, tail, 0)


def _row_span(first_row, n_rows):
    return pl.ds(pl.multiple_of(first_row * ROW_SUBTILES, ROW_SUBTILES),
                 pl.multiple_of(n_rows * ROW_SUBTILES, ROW_SUBTILES))


def _dispatch_kernel(pad_dst_ref, pad_n_ref, nu_ref,
                     src_ref, n_ref, dst_ref, slot_ref, h2_ref,
                     xs_hbm,
                     local, zeros_s, run_sem, pad_sem):
    i = pl.program_id(0)
    last = pl.num_programs(0) - 1
    buf = i % 2
    n_row_tiles = xs_hbm.shape[0] // zeros_s.shape[0]

    def pad_copy(e):
        n = pad_n_ref[e]
        return n, pltpu.make_async_copy(zeros_s.at[_row_span(0, n)], xs_hbm.at[_row_span(pad_dst_ref[e], n)],
                                        pad_sem)

    def unused_tiles(act):
        def body(j, carry):
            rows = zeros_s.shape[0]
            act(pltpu.make_async_copy(
                zeros_s, xs_hbm.at[pl.ds(pl.multiple_of(j * rows, rows), rows)], pad_sem))
            return carry
        lax.fori_loop(nu_ref[0], n_row_tiles, body, 0)

    def run_copy(e):
        n = n_ref[0, 0, e]
        return n, pltpu.make_async_copy(local.at[buf, _row_span(src_ref[0, 0, e], n)],
                                        xs_hbm.at[_row_span(dst_ref[0, 0, e], n)], run_sem.at[buf])

    def wait_runs(b):
        pltpu.make_async_copy(local.at[b], xs_hbm.at[pl.ds(0, local.shape[1])], run_sem.at[b]).wait()

    @pl.when(i == 0)
    def _():
        zeros_s[...] = jnp.zeros_like(zeros_s)
        for e in range(N_EXPERTS):
            n, copy = pad_copy(e)
            pl.when(n > 0)(copy.start)
        unused_tiles(lambda copy: copy.start())

    @pl.when(i >= 2)
    def _():
        wait_runs(buf)

    def place(t, lane):
        row = h2_ref[pl.ds(pl.multiple_of(t * ROW_SUBTILES, ROW_SUBTILES), ROW_SUBTILES), :]
        for k in range(TOP_K):
            at = pl.multiple_of(slot_ref[0, 0, k * ROW_TILE + t] * ROW_SUBTILES, ROW_SUBTILES)
            local[buf, pl.ds(at, ROW_SUBTILES), :] = row
    _unrolled_rows(ROW_TILE, place)

    for e in range(N_EXPERTS):
        n, copy = run_copy(e)
        pl.when(n > 0)(copy.start)

    @pl.when(i == last)
    def _():
        wait_runs(buf)

        @pl.when(last >= 1)
        def _():
            wait_runs(1 - buf)

        for e in range(N_EXPERTS):
            n, copy = pad_copy(e)
            pl.when(n > 0)(copy.wait)
        unused_tiles(lambda copy: copy.wait())


def _run_spec(index_of):
    return pl.BlockSpec((1, 1, LANES), lambda i, *_: (index_of(i), 0, 0), memory_space=pltpu.SMEM)


def _dispatch(pad_dst, pad_n, n_used, run_src, run_n, run_dst, slots, h2, n_slots):
    nt = run_n.shape[0]
    picks = ROW_TILE * TOP_K
    grid_spec = pltpu.PrefetchScalarGridSpec(
        num_scalar_prefetch=3,
        grid=(nt,),
        in_specs=[
            _run_spec(lambda i: i), _run_spec(lambda i: i), _run_spec(lambda i: i),
            pl.BlockSpec((1, 1, picks), lambda i, *_: (i, 0, 0), memory_space=pltpu.SMEM),
            pl.BlockSpec((ROW_TILE * ROW_SUBTILES, LANES), lambda i, *_: (i, 0)),
        ],
        out_specs=pl.BlockSpec(memory_space=pl.ANY),
        scratch_shapes=[
            pltpu.VMEM((2, picks * ROW_SUBTILES, LANES), jnp.float32),
            pltpu.VMEM((MOE_TILE * ROW_SUBTILES, LANES), jnp.float32),
            pltpu.SemaphoreType.DMA((2,)), pltpu.SemaphoreType.DMA,
        ],
    )
    return pl.pallas_call(
        _dispatch_kernel,
        out_shape=jax.ShapeDtypeStruct((n_slots * ROW_SUBTILES, LANES), jnp.float32),
        grid_spec=grid_spec,
        compiler_params=_params(),
        name="dispatch",
    )(pad_dst, pad_n, n_used, run_src, run_n, run_dst, slots, h2)


def _expert_kernel(te_ref, nu_ref, xs_ref, wgu_ref, bgu_ref, wdn_ref, bdn_ref, ys_ref, wgu_s, wdn_s):
    i = pl.program_id(0)
    n_used = nu_ref[0]
    expert_changed = jnp.logical_or(i == 0, te_ref[i] != te_ref[jnp.maximum(i - 1, 0)])

    @pl.when(jnp.logical_and(i < n_used, expert_changed))
    def _():
        wgu_s[...] = _bf16(wgu_ref[0])
        wdn_s[...] = _bf16(wdn_ref[0])

    @pl.when(i < n_used)
    def _():
        x = _bf16(_load_row_tiled(xs_ref, (), MOE_TILE))
        gu = _dot(x, wgu_s[...]) + bgu_ref[0]
        gate = jnp.minimum(gu[:, :D_FF], SWIGLU_LIMIT)
        lin = jnp.clip(gu[:, D_FF:], -SWIGLU_LIMIT, SWIGLU_LIMIT)
        act = gate * _sigmoid(SWIGLU_ALPHA * gate) * (lin + 1.0)
        _store_row_tiled(ys_ref, (), _dot(_bf16(act), wdn_s[...]) + bdn_ref[0])

    @pl.when(i >= n_used)
    def _():
        ys_ref[...] = jnp.zeros_like(ys_ref)


def _experts(tile_expert, n_used, xs, w_gu, b_gu, w_dn, b_dn):
    n_tiles = tile_expert.shape[0]
    f32, bf16 = jnp.float32, jnp.bfloat16
    tile_rows = MOE_TILE * ROW_SUBTILES
    grid_spec = pltpu.PrefetchScalarGridSpec(
        num_scalar_prefetch=2,
        grid=(n_tiles,),
        in_specs=[
            pl.BlockSpec((tile_rows, LANES), lambda i, te, nu: (jnp.minimum(i, nu[0] - 1), 0)),
            pl.BlockSpec((1, D_MODEL, 2 * D_FF), lambda i, te, nu: (te[i], 0, 0)),
            pl.BlockSpec((1, 1, 2 * D_FF), lambda i, te, nu: (te[i], 0, 0)),
            pl.BlockSpec((1, D_FF, D_MODEL), lambda i, te, nu: (te[i], 0, 0)),
            pl.BlockSpec((1, 1, D_MODEL), lambda i, te, nu: (te[i], 0, 0)),
        ],
        out_specs=pl.BlockSpec((tile_rows, LANES), lambda i, te, nu: (i, 0)),
        scratch_shapes=[pltpu.VMEM((D_MODEL, 2 * D_FF), bf16), pltpu.VMEM((D_FF, D_MODEL), bf16)],
    )
    return pl.pallas_call(
        _expert_kernel,
        out_shape=jax.ShapeDtypeStruct(xs.shape, f32),
        grid_spec=grid_spec,
        compiler_params=_params(),
        name="experts",
    )(tile_expert, n_used, xs, w_gu, b_gu.reshape(N_EXPERTS, 1, -1), w_dn,
      b_dn.reshape(N_EXPERTS, 1, -1))


def _combine_kernel(n_prompt_tiles,
                    src_ref, n_ref, dst_ref, src_nx_ref, n_nx_ref, dst_nx_ref, slot_ref, gate_ref,
                    x1_ref, gfin_ref, ys_hbm,
                    yp_ref, yo_ref,
                    local, mixed, run_sem):
    i = pl.program_id(0)
    last = pl.num_programs(0) - 1
    buf = i % 2

    def fetch_runs(s_ref, c_ref, d_ref, b):
        for e in range(N_EXPERTS):
            n = c_ref[0, 0, e]
            copy = pltpu.make_async_copy(ys_hbm.at[_row_span(d_ref[0, 0, e], n)],
                                         local.at[b, _row_span(s_ref[0, 0, e], n)], run_sem.at[b])
            pl.when(n > 0)(copy.start)

    @pl.when(i == 0)
    def _():
        fetch_runs(src_ref, n_ref, dst_ref, 0)

    @pl.when(i < last)
    def _():
        fetch_runs(src_nx_ref, n_nx_ref, dst_nx_ref, 1 - buf)

    pltpu.make_async_copy(ys_hbm.at[pl.ds(0, local.shape[1])], local.at[buf], run_sem.at[buf]).wait()

    def blend(t, lane):
        acc = None
        for k in range(TOP_K):
            at = pl.multiple_of(slot_ref[0, 0, k * ROW_TILE + t] * ROW_SUBTILES, ROW_SUBTILES)
            term = gate_ref[0, 0, k * ROW_TILE + t] * local[buf, pl.ds(at, ROW_SUBTILES), :]
            acc = term if acc is None else acc + term
        mixed[pl.ds(pl.multiple_of(t * ROW_SUBTILES, ROW_SUBTILES), ROW_SUBTILES), :] = acc
    _unrolled_rows(ROW_TILE, blend)

    out = _rms(x1_ref[...] + _load_row_tiled(mixed, (), ROW_TILE), gfin_ref[...])

    @pl.when(i < n_prompt_tiles)
    def _():
        yp_ref[...] = out

    @pl.when(i >= n_prompt_tiles)
    def _():
        yo_ref[...] = out


def _combine(run_src, run_n, run_dst, slots, gates, x1, ys, g_final, tp):
    t = x1.shape[0]
    npt = tp // ROW_TILE
    nt = t // ROW_TILE
    f32 = jnp.float32
    picks = ROW_TILE * TOP_K
    nxt = lambda i: jnp.minimum(i + 1, nt - 1)
    pick_spec = pl.BlockSpec((1, 1, picks), lambda i: (i, 0, 0), memory_space=pltpu.SMEM)
    return pl.pallas_call(
        functools.partial(_combine_kernel, npt),
        out_shape=(jax.ShapeDtypeStruct((tp, D_MODEL), f32),
                   jax.ShapeDtypeStruct((t - tp, D_MODEL), f32)),
        grid=(nt,),
        in_specs=[_run_spec(lambda i: i), _run_spec(lambda i: i), _run_spec(lambda i: i),
                  _run_spec(nxt), _run_spec(nxt), _run_spec(nxt), pick_spec, pick_spec,
                  _row_spec(D_MODEL), _const_spec((1, D_MODEL)), pl.BlockSpec(memory_space=pl.ANY)],
        out_specs=(_prompt_spec(D_MODEL, npt), _sample_spec(D_MODEL, npt)),
        scratch_shapes=[pltpu.VMEM((2, picks * ROW_SUBTILES, LANES), f32),
                        pltpu.VMEM((ROW_TILE * ROW_SUBTILES, LANES), f32),
                        pltpu.SemaphoreType.DMA((2,))],
        compiler_params=_params(),
        name="combine",
    )(run_src, run_n, run_dst, run_src, run_n, run_dst, slots, gates, x1, g_final.reshape(1, -1), ys)


def _plan(meta, counts, t):
    nt = t // ROW_TILE
    n_tiles = (t * TOP_K + N_EXPERTS * (MOE_TILE - 1)) // MOE_TILE
    counts = counts[0, :N_EXPERTS]
    tiles_e = (counts + MOE_TILE - 1) // MOE_TILE
    tile_end = jnp.cumsum(tiles_e)
    tile_start = tile_end - tiles_e
    n_used = tile_end[-1]
    first_row = jnp.pad(tile_start * MOE_TILE, (0, LANES - N_EXPERTS))
    meta = meta.reshape(nt, 8, LANES)
    run_dst = meta[:, 0:1, :] + first_row[None, None, :]
    run_n = meta[:, 1:2, :]
    run_src = meta[:, 2:3, :]
    pad_dst = tile_start * MOE_TILE + counts
    pad_n = tiles_e * MOE_TILE - counts
    tile_ids = jnp.arange(n_tiles, dtype=jnp.int32)
    live = jnp.minimum(tile_ids, n_used - 1)
    tile_expert = jnp.sum(tile_end[None, :] <= live[:, None], axis=1).astype(jnp.int32)
    return (tile_expert, n_used.reshape(1).astype(jnp.int32), run_src, run_n, run_dst,
            pad_dst.astype(jnp.int32), pad_n.astype(jnp.int32), n_tiles * MOE_TILE)


def kernel(x_prompt, x_sample, cache_k, cache_v, g_mix, w_in, b_in, ln_v_g, ln_v_b, w_sp, b_sp,
           attn_sinks, w_pa, w_pb, w_o, g_ffn, w_router, b_router, w_gu, b_gu, w_dn, b_dn, g_final):
    nb, seq, d = x_prompt.shape
    nsb, nnew, _ = x_sample.shape
    tp, ts = nb * seq, nsb * nnew
    t = tp + ts
    xp = x_prompt.reshape(tp, d)
    xs = x_sample.reshape(ts, d)
    pos = np.concatenate([np.tile(np.arange(seq), nb), np.tile(PAST_LEN + np.arange(nnew), nsb)])
    tables = _rope_tables(pos)
    u, vln, v_sgu, q, k, v, kd, vd, ga, gb = _project(
        xp, xs, g_mix[0], w_in[0], b_in[0], ln_v_g[0], ln_v_b[0], tables)
    x1, h2, topi, topg = _mix(
        xp, xs, u, vln, q, kd, vd, cache_k[0], cache_v[0], ga, gb, w_sp[0], b_sp[0], attn_sinks[0],
        w_pa[0], w_pb[0], w_o[0], g_ffn[0], w_router[0], b_router[0], seq)
    slot_t, gate_t, meta, counts = _rank(topi, topg)
    nt = t // ROW_TILE
    picks = lambda a: a.reshape(nt, 8, ROW_TILE)[:, :TOP_K, :].reshape(nt, 1, TOP_K * ROW_TILE)
    slots, gates = picks(slot_t), picks(gate_t)
    tile_expert, n_used, run_src, run_n, run_dst, pad_dst, pad_n, n_slots = _plan(meta, counts, t)
    x_sorted = _dispatch(pad_dst, pad_n, n_used, run_src, run_n, run_dst, slots, h2, n_slots)
    y_sorted = _experts(tile_expert, n_used, x_sorted, w_gu[0], b_gu[0], w_dn[0], b_dn[0])
    y_p, y_s = _combine(run_src, run_n, run_dst, slots, gates, x1, y_sorted, g_final, tp)

    keep = min(WINDOW, seq)
    tails = lambda a: jnp.stack([a[(b + 1) * seq - keep:(b + 1) * seq] for b in range(nb)]).reshape(
        nb, keep, N_KV_HEADS, HEAD_DIM)
    kp, vp = tails(k), tails(v)
    ks = k[tp:].reshape(nsb, nnew, N_KV_HEADS, HEAD_DIM)
    vs = v[tp:].reshape(nsb, nnew, N_KV_HEADS, HEAD_DIM)
    return (y_p.reshape(nb, seq, d), y_s.reshape(nsb, nnew, d), kp[None], vp[None], ks[None],
            vs[None], v_sgu.reshape(1, nsb, nnew, SGU_WIDTH))
```

```python
import functools

import numpy as np
import jax
import jax.numpy as jnp
from jax import lax
from jax.experimental import pallas as pl
from jax.experimental.pallas import tpu as pltpu

D_MODEL = 1024
PAST_LEN = 2048
CHUNK = 64
SGU_CHUNK = 128
SGU_GROUPS = 8
SGU_WIDTH = 1024
N_HEADS = 16
N_KV_HEADS = 4
HEAD_DIM = 64
Q_PER_KV = N_HEADS // N_KV_HEADS
WINDOW = 128
ROT_DIM = HEAD_DIM // 4
ROPE_THETA = 500000.0
ATT_W = N_HEADS * HEAD_DIM
KV_W = N_KV_HEADS * HEAD_DIM
N_EXPERTS = 32
TOP_K = 4
D_FF = 1024
SWIGLU_ALPHA = 1.702
SWIGLU_LIMIT = 7.0
NORM_EPS = 1e-5
NEG_INF = -1e30
N_IN = SGU_WIDTH * 2 + ATT_W + KV_W * 2 + D_MODEL * 2

LANES = 128
ROW_TILE = 256
MOE_TILE = 256
ROW_UNROLL = 8
KV_DUP_W = N_KV_HEADS * LANES
KEY_SPAN = WINDOW + CHUNK
VMEM_LIMIT = 56 * 1024 * 1024

_SQRT_HALF = 0.7071067811865476


def _gelu(x):
    z = jnp.abs(x) * _SQRT_HALF
    t = 1.0 / (1.0 + 0.3275911 * z)
    poly = t * (0.254829592 + t * (-0.284496736 + t * (1.421413741
                + t * (-1.453152027 + t * 1.061405429))))
    half_tail = (0.5 * x) * (poly * jnp.exp(-z * z))
    return jnp.where(x >= 0.0, x - half_tail, half_tail)


def _sigmoid(x):
    return 1.0 / (1.0 + jnp.exp(-x))


def _bf16(x):
    return x.astype(jnp.bfloat16)


def _dot(a, b):
    return jnp.dot(a, b, preferred_element_type=jnp.float32)


ROW_SUBTILES = D_MODEL // LANES


def _store_row_tiled(ref, lead, x):
    rows = x.shape[0]
    for s in range(ROW_SUBTILES):
        ref[(*lead, pl.ds(s, rows, stride=ROW_SUBTILES), slice(None))] = x[:, s * LANES:(s + 1) * LANES]


def _load_row_tiled(ref, lead, rows):
    return jnp.concatenate(
        [ref[(*lead, pl.ds(s, rows, stride=ROW_SUBTILES), slice(None))] for s in range(ROW_SUBTILES)],
        axis=1)


def _dot_nt(a, b):
    return lax.dot_general(a, b, (((1,), (1,)), ((), ())), preferred_element_type=jnp.float32)


def _rms(x, g):
    return x * lax.rsqrt(jnp.mean(x * x, axis=-1, keepdims=True) + NORM_EPS) * g


def _lane_lo(rows):
    return lax.broadcasted_iota(jnp.int32, (rows, LANES), 1) < HEAD_DIM


def _dup_heads(kv):
    rows = kv.shape[0]
    lo = _lane_lo(rows)
    out = []
    for j in range(KV_W // LANES):
        blk = kv[:, j * LANES:(j + 1) * LANES]
        swp = pltpu.roll(blk, HEAD_DIM, axis=1)
        out.append(jnp.where(lo, blk, swp))
        out.append(jnp.where(lo, swp, blk))
    return _bf16(jnp.concatenate(out, axis=1))


def _rope_block(zb, cos_b, sin_lo, sin_hi):
    up = pltpu.roll(zb, LANES - ROT_DIM // 2, axis=1)
    dn = pltpu.roll(zb, ROT_DIM // 2, axis=1)
    return zb * cos_b + up * sin_lo + dn * sin_hi


def _proj_kernel(n_prompt_tiles, xp_ref, xs_ref, gmix_ref, w_ref, b_ref, lng_ref, lnb_ref,
                 cos_ref, slo_ref, shi_ref,
                 u_ref, vln_ref, vs_ref, q_ref, k_ref, v_ref, kd_ref, vd_ref, ga_ref, gb_ref):
    i = pl.program_id(0)
    x = jnp.where(i < n_prompt_tiles, xp_ref[...], xs_ref[...])
    h = _bf16(_rms(x, gmix_ref[...]))

    def seg(lo, width):
        return _dot(h, w_ref[:, lo:lo + width]) + b_ref[:, lo:lo + width]

    o = 0
    u_ref[...] = _bf16(_gelu(seg(o, SGU_WIDTH)))
    o += SGU_WIDTH
    gv = _gelu(seg(o, SGU_WIDTH))
    gc = gv - jnp.mean(gv, axis=-1, keepdims=True)
    var = jnp.mean(gc * gc, axis=-1, keepdims=True)
    vln = gc * lax.rsqrt(var + NORM_EPS) * lng_ref[...] + lnb_ref[...]
    vln_ref[...] = _bf16(vln)

    @pl.when(i >= n_prompt_tiles)
    def _():
        vs_ref[...] = vln

    o += SGU_WIDTH
    cos_b, sin_lo, sin_hi = cos_ref[...], slo_ref[...], shi_ref[...]
    zq = seg(o, ATT_W)
    for j in range(ATT_W // LANES):
        blk = _rope_block(zq[:, j * LANES:(j + 1) * LANES], cos_b, sin_lo, sin_hi)
        q_ref[:, j * LANES:(j + 1) * LANES] = _bf16(blk * (HEAD_DIM ** -0.5))
    o += ATT_W
    zk = seg(o, KV_W)
    kr = jnp.concatenate(
        [_rope_block(zk[:, j * LANES:(j + 1) * LANES], cos_b, sin_lo, sin_hi)
         for j in range(KV_W // LANES)], axis=1)
    k_ref[...] = kr
    kd_ref[...] = _dup_heads(kr)
    o += KV_W
    zv = seg(o, KV_W)
    v_ref[...] = zv
    vd_ref[...] = _dup_heads(zv)
    o += KV_W
    ga_ref[...] = _bf16(_sigmoid(seg(o, D_MODEL)))
    o += D_MODEL
    gb_ref[...] = _bf16(_sigmoid(seg(o, D_MODEL)))


def _rope_tables(pos):
    half = ROT_DIM // 2
    inv = np.float32(ROPE_THETA) ** (-np.arange(half, dtype=np.float32) * np.float32(2.0) / ROT_DIM)
    ang = pos.astype(np.float32)[:, None] * inv.astype(np.float32)[None, :]
    cos = jnp.asarray(np.cos(ang.astype(np.float64)).astype(np.float32))
    sin = jnp.asarray(np.sin(ang.astype(np.float64)).astype(np.float32))
    n = pos.shape[0]
    ones = jnp.ones((n, HEAD_DIM - ROT_DIM), jnp.float32)
    zeros = jnp.zeros((n, HEAD_DIM - ROT_DIM), jnp.float32)
    zh = jnp.zeros((n, half), jnp.float32)
    cos_h = jnp.concatenate([cos, cos, ones], axis=1)
    slo_h = jnp.concatenate([-sin, zh, zeros], axis=1)
    shi_h = jnp.concatenate([zh, sin, zeros], axis=1)
    rep = LANES // HEAD_DIM
    return (jnp.tile(cos_h, (1, rep)), jnp.tile(slo_h, (1, rep)), jnp.tile(shi_h, (1, rep)))


def _row_spec(width):
    return pl.BlockSpec((ROW_TILE, width), lambda i: (i, 0))


def _const_spec(shape):
    return pl.BlockSpec(shape, lambda i: (0,) * len(shape))


def _prompt_spec(width, n_prompt_tiles):
    return pl.BlockSpec((ROW_TILE, width), lambda i: (jnp.minimum(i, n_prompt_tiles - 1), 0))


def _sample_spec(width, n_prompt_tiles):
    return pl.BlockSpec((ROW_TILE, width), lambda i: (jnp.maximum(i - n_prompt_tiles, 0), 0))


def _params():
    return pltpu.CompilerParams(dimension_semantics=("arbitrary",), vmem_limit_bytes=VMEM_LIMIT)


def _project(xp, xs, g_mix, w_in, b_in, ln_g, ln_b, tables):
    tp, ts = xp.shape[0], xs.shape[0]
    t = tp + ts
    npt = tp // ROW_TILE
    f32, bf16 = jnp.float32, jnp.bfloat16
    out_shape = (
        jax.ShapeDtypeStruct((t, SGU_WIDTH), bf16),
        jax.ShapeDtypeStruct((t, SGU_WIDTH), bf16),
        jax.ShapeDtypeStruct((ts, SGU_WIDTH), f32),
        jax.ShapeDtypeStruct((t, ATT_W), bf16),
        jax.ShapeDtypeStruct((t, KV_W), f32),
        jax.ShapeDtypeStruct((t, KV_W), f32),
        jax.ShapeDtypeStruct((t, KV_DUP_W), bf16),
        jax.ShapeDtypeStruct((t, KV_DUP_W), bf16),
        jax.ShapeDtypeStruct((t, D_MODEL), bf16),
        jax.ShapeDtypeStruct((t, D_MODEL), bf16),
    )
    return pl.pallas_call(
        functools.partial(_proj_kernel, npt),
        out_shape=out_shape,
        grid=(t // ROW_TILE,),
        in_specs=[
            _prompt_spec(D_MODEL, npt), _sample_spec(D_MODEL, npt),
            _const_spec((1, D_MODEL)), _const_spec((D_MODEL, N_IN)),
            _const_spec((1, N_IN)), _const_spec((1, SGU_WIDTH)), _const_spec((1, SGU_WIDTH)),
            _row_spec(LANES), _row_spec(LANES), _row_spec(LANES),
        ],
        out_specs=(
            _row_spec(SGU_WIDTH), _row_spec(SGU_WIDTH), _sample_spec(SGU_WIDTH, npt),
            _row_spec(ATT_W), _row_spec(KV_W), _row_spec(KV_W), _row_spec(KV_DUP_W),
            _row_spec(KV_DUP_W), _row_spec(D_MODEL), _row_spec(D_MODEL),
        ),
        compiler_params=_params(),
        name="proj",
    )(xp, xs, g_mix.reshape(1, -1), w_in.astype(bf16), b_in.reshape(1, -1),
      ln_g.reshape(1, -1), ln_b.reshape(1, -1), *tables)


def _attend(qa, qb, kwin, vwin, sink, valid):
    lo = _lane_lo(CHUNK)
    zero = jnp.zeros_like(qa)
    lhs = jnp.concatenate([jnp.where(lo, qa, zero), jnp.where(lo, zero, qa),
                           jnp.where(lo, qb, zero), jnp.where(lo, zero, qb)], axis=0)
    s = _dot_nt(lhs, kwin)
    if valid is not None:
        s = jnp.where(valid, s, NEG_INF)
    s_a, s_b = s[:, :LANES], s[:, LANES:]
    tail = s_b.shape[1]
    m = jnp.maximum(jnp.max(s, axis=-1, keepdims=True), sink)
    p_a = jnp.exp(s_a - m)
    p_b = jnp.exp(s_b - m[:, :tail])
    denom = (jnp.sum(jnp.concatenate([p_a, p_b], axis=1), axis=-1, keepdims=True)
             + jnp.exp(sink - m))
    inv = 1.0 / denom
    pn = jnp.concatenate([p_a * inv, p_b * inv[:, :tail]], axis=1)
    r = _dot(_bf16(pn), vwin)
    oa = jnp.where(lo, r[0:CHUNK], r[CHUNK:2 * CHUNK])
    ob = jnp.where(lo, r[2 * CHUNK:3 * CHUNK], r[3 * CHUNK:4 * CHUNK])
    return oa, ob


def _top4(logits):
    rows = logits.shape[0]
    lane = lax.broadcasted_iota(jnp.int32, (rows, N_EXPERTS), 1)
    work = logits
    vals, idxs = [], []
    for _ in range(TOP_K):
        m = jnp.max(work, axis=-1, keepdims=True)
        idx = jnp.min(jnp.where(work == m, lane, N_EXPERTS), axis=-1, keepdims=True)
        vals.append(m)
        idxs.append(idx)
        work = jnp.where(lane == idx, -jnp.inf, work)
    exps = [jnp.exp(v - vals[0]) for v in vals]
    denom = exps[0] + exps[1] + exps[2] + exps[3]
    wide = lax.broadcasted_iota(jnp.int32, (rows, LANES), 1)
    topi = jnp.zeros((rows, LANES), jnp.int32)
    topg = jnp.zeros((rows, LANES), jnp.float32)
    for k in range(TOP_K):
        topi = jnp.where(wide == k, idxs[k], topi)
        topg = jnp.where(wide == k, exps[k] / denom, topg)
    return topi, topg


def _mix_kernel(tiles_per_seq, n_prompt_tiles,
                xp_ref, xs_ref, u_ref, vln_ref, q_ref, kd_ref, vd_ref, kdp_ref, vdp_ref,
                ck_ref, cv_ref, ga_ref, gb_ref, wsp_ref, bsp_ref, sink_ref,
                wpa_ref, wpb_ref, wo_ref, gffn_ref, wrh_ref, wrl_ref, br_ref,
                x1_ref, h2_ref, topi_ref, topg_ref,
                a_s, o_s, kwin_s, vwin_s):
    i = pl.program_id(0)
    n_streams = ROW_TILE // CHUNK

    def sgu_rows(r0, rows):
        ri = lax.broadcasted_iota(jnp.int32, (rows, rows), 0) // CHUNK
        ci = lax.broadcasted_iota(jnp.int32, (rows, rows), 1) // CHUNK
        for g in range(SGU_GROUPS):
            cols = slice(g * LANES, (g + 1) * LANES)
            w = _bf16(jnp.where(ci <= ri, wsp_ref[g, :rows, :rows], 0.0))
            sp = _dot(w, vln_ref[r0:r0 + rows, cols]) + bsp_ref[g, :rows, :]
            a_s[r0:r0 + rows, cols] = _bf16(u_ref[r0:r0 + rows, cols].astype(jnp.float32) * sp)

    def attend_rows(r0, kwin_of, valid):
        for g in range(N_KV_HEADS):
            c0 = g * Q_PER_KV * HEAD_DIM
            kwin, vwin = kwin_of(g)
            oa, ob = _attend(q_ref[r0:r0 + CHUNK, c0:c0 + LANES],
                             q_ref[r0:r0 + CHUNK, c0 + LANES:c0 + 2 * LANES],
                             kwin, vwin, sink_ref[g], valid)
            o_s[r0:r0 + CHUNK, c0:c0 + LANES] = _bf16(oa)
            o_s[r0:r0 + CHUNK, c0 + LANES:c0 + 2 * LANES] = _bf16(ob)

    @pl.when(i < n_prompt_tiles)
    def _prompt():
        for c in range(ROW_TILE // SGU_CHUNK):
            sgu_rows(c * SGU_CHUNK, SGU_CHUNK)
        kwin_s[0:WINDOW] = kdp_ref[...]
        kwin_s[WINDOW:WINDOW + ROW_TILE] = kd_ref[...]
        vwin_s[0:WINDOW] = vdp_ref[...]
        vwin_s[WINDOW:WINDOW + ROW_TILE] = vd_ref[...]
        first = (i % tiles_per_seq) == 0
        col = lax.broadcasted_iota(jnp.int32, (1, KEY_SPAN), 1)
        for j in range(ROW_TILE // CHUNK):
            r0 = j * CHUNK
            valid = jnp.logical_or(jnp.logical_not(first), col + r0 >= WINDOW) if r0 < WINDOW else None

            def kwin_of(g, r0=r0):
                cols = slice(g * LANES, (g + 1) * LANES)
                return kwin_s[r0:r0 + KEY_SPAN, cols], vwin_s[r0:r0 + KEY_SPAN, cols]

            attend_rows(r0, kwin_of, valid)

    @pl.when(i >= n_prompt_tiles)
    def _sample():
        for s in range(n_streams):
            r0 = s * CHUNK
            sgu_rows(r0, CHUNK)
            kwin_s[0:WINDOW] = _dup_heads(ck_ref[s])
            kwin_s[WINDOW:KEY_SPAN] = kd_ref[r0:r0 + CHUNK]
            vwin_s[0:WINDOW] = _dup_heads(cv_ref[s])
            vwin_s[WINDOW:KEY_SPAN] = vd_ref[r0:r0 + CHUNK]

            def kwin_of(g):
                cols = slice(g * LANES, (g + 1) * LANES)
                return kwin_s[0:KEY_SPAN, cols], vwin_s[0:KEY_SPAN, cols]

            attend_rows(r0, kwin_of, None)

    x = jnp.where(i < n_prompt_tiles, xp_ref[...], xs_ref[...])
    m = (ga_ref[...].astype(jnp.float32) * _dot(a_s[...], wpa_ref[...])
         + gb_ref[...].astype(jnp.float32) * _dot(o_s[...], wpb_ref[...]))
    x1 = x + _dot(_bf16(m), wo_ref[...])
    x1_ref[...] = x1
    h2 = _rms(x1, gffn_ref[...])
    _store_row_tiled(h2_ref, (), h2)
    hh = _bf16(h2)
    hl = _bf16(h2 - hh.astype(jnp.float32))
    logits = (_dot(hh, wrh_ref[...]) + _dot(hh, wrl_ref[...]) + _dot(hl, wrh_ref[...])) + br_ref[...]
    topi, topg = _top4(logits)
    topi_ref[...] = topi
    topg_ref[...] = topg


def _mix(xp, xs, u, vln, q, kd, vd, cache_k, cache_v, ga, gb, w_sp, b_sp, sinks,
         w_pa, w_pb, w_o, g_ffn, w_router, b_router, seq):
    tp, ts = xp.shape[0], xs.shape[0]
    t = tp + ts
    npt = tp // ROW_TILE
    tiles_per_seq = seq // ROW_TILE
    f32, bf16 = jnp.float32, jnp.bfloat16
    n_streams = ROW_TILE // CHUNK
    win_per_tile = ROW_TILE // WINDOW

    prev_spec = pl.BlockSpec(
        (WINDOW, KV_DUP_W), lambda i: (jnp.maximum(jnp.minimum(i, npt - 1) * win_per_tile - 1, 0), 0))
    cache_spec = pl.BlockSpec(
        (n_streams, WINDOW, KV_W), lambda i: (jnp.maximum(i - npt, 0), 0, 0))
    sink_cols = jnp.broadcast_to(
        jnp.repeat(sinks.astype(f32).reshape(N_KV_HEADS, Q_PER_KV), CHUNK, axis=1)[:, :, None],
        (N_KV_HEADS, Q_PER_KV * CHUNK, LANES))
    wr_hi = w_router.astype(bf16)
    wr_lo = (w_router - wr_hi.astype(f32)).astype(bf16)
    out_shape = (
        jax.ShapeDtypeStruct((t, D_MODEL), f32),
        jax.ShapeDtypeStruct((t * ROW_SUBTILES, LANES), f32),
        jax.ShapeDtypeStruct((t, LANES), jnp.int32),
        jax.ShapeDtypeStruct((t, LANES), f32),
    )
    return pl.pallas_call(
        functools.partial(_mix_kernel, tiles_per_seq, npt),
        out_shape=out_shape,
        grid=(t // ROW_TILE,),
        in_specs=[
            _prompt_spec(D_MODEL, npt), _sample_spec(D_MODEL, npt),
            _row_spec(SGU_WIDTH), _row_spec(SGU_WIDTH), _row_spec(ATT_W),
            _row_spec(KV_DUP_W), _row_spec(KV_DUP_W), prev_spec, prev_spec,
            cache_spec, cache_spec, _row_spec(D_MODEL), _row_spec(D_MODEL),
            _const_spec((SGU_GROUPS, SGU_CHUNK, SGU_CHUNK)), _const_spec((SGU_GROUPS, SGU_CHUNK, LANES)),
            _const_spec((N_KV_HEADS, Q_PER_KV * CHUNK, LANES)),
            _const_spec((SGU_WIDTH, D_MODEL)), _const_spec((ATT_W, D_MODEL)),
            _const_spec((D_MODEL, D_MODEL)), _const_spec((1, D_MODEL)),
            _const_spec((D_MODEL, N_EXPERTS)), _const_spec((D_MODEL, N_EXPERTS)),
            _const_spec((1, N_EXPERTS)),
        ],
        out_specs=(_row_spec(D_MODEL),
                   pl.BlockSpec((ROW_TILE * ROW_SUBTILES, LANES), lambda i: (i, 0)),
                   _row_spec(LANES), _row_spec(LANES)),
        scratch_shapes=[
            pltpu.VMEM((ROW_TILE, SGU_WIDTH), bf16), pltpu.VMEM((ROW_TILE, ATT_W), bf16),
            pltpu.VMEM((WINDOW + ROW_TILE, KV_DUP_W), bf16),
            pltpu.VMEM((WINDOW + ROW_TILE, KV_DUP_W), bf16),
        ],
        compiler_params=_params(),
        name="mix",
    )(xp, xs, u, vln, q, kd, vd, kd, vd,
      cache_k.reshape(-1, WINDOW, KV_W), cache_v.reshape(-1, WINDOW, KV_W), ga, gb,
      w_sp, jnp.broadcast_to(b_sp[:, :, None], (SGU_GROUPS, SGU_CHUNK, LANES)), sink_cols,
      w_pa.astype(bf16), w_pb.astype(bf16), w_o.astype(bf16),
      g_ffn.reshape(1, -1), wr_hi, wr_lo, b_router.reshape(1, -1))


def _rank_kernel(topi_ref, topg_ref, lst_ref, gt_ref, meta_ref, count_ref, carry_s):
    i = pl.program_id(0)

    @pl.when(i == 0)
    def _():
        carry_s[...] = jnp.zeros_like(carry_s)

    topi = topi_ref[...]
    lane = lax.broadcasted_iota(jnp.int32, (ROW_TILE, LANES), 1)
    onehot = jnp.zeros((ROW_TILE, LANES), jnp.float32)
    for k in range(TOP_K):
        onehot = jnp.where(lane == topi[:, k:k + 1], 1.0, onehot)
    r = lax.broadcasted_iota(jnp.int32, (ROW_TILE, ROW_TILE), 0)
    c = lax.broadcasted_iota(jnp.int32, (ROW_TILE, ROW_TILE), 1)
    below = _bf16(jnp.where(c < r, 1.0, 0.0))
    in_tile = _dot(below, _bf16(onehot))
    count = jnp.sum(onehot, axis=0, keepdims=True)
    er = lax.broadcasted_iota(jnp.int32, (LANES, LANES), 0)
    ec = lax.broadcasted_iota(jnp.int32, (LANES, LANES), 1)
    before = _bf16(jnp.where(er < ec, 1.0, 0.0))
    start = _dot(_bf16(jnp.broadcast_to(count, (8, LANES))), before)[0:1, :]
    local = in_tile + start
    slot = jnp.full((ROW_TILE, LANES), -1.0, jnp.float32)
    for k in range(TOP_K):
        sel = jnp.sum(jnp.where(lane == topi[:, k:k + 1], local, 0.0), axis=-1, keepdims=True)
        slot = jnp.where(lane == k, sel, slot)
    lst_ref[...] = slot.T[0:8, :].astype(jnp.int32)
    gt_ref[...] = topg_ref[...].T[0:8, :]
    row = lax.broadcasted_iota(jnp.int32, (8, LANES), 0)
    meta = jnp.where(row == 0, carry_s[...], jnp.where(row == 1, count, jnp.where(row == 2, start, 0.0)))
    meta_ref[...] = meta.astype(jnp.int32)
    carry_s[...] = carry_s[...] + count
    count_ref[...] = carry_s[...].astype(jnp.int32)


def _rank(topi, topg):
    t = topi.shape[0]
    nt = t // ROW_TILE
    tile8 = lambda width: pl.BlockSpec((8, width), lambda i: (i, 0))
    return pl.pallas_call(
        _rank_kernel,
        out_shape=(jax.ShapeDtypeStruct((nt * 8, ROW_TILE), jnp.int32),
                   jax.ShapeDtypeStruct((nt * 8, ROW_TILE), jnp.float32),
                   jax.ShapeDtypeStruct((nt * 8, LANES), jnp.int32),
                   jax.ShapeDtypeStruct((8, LANES), jnp.int32)),
        grid=(nt,),
        in_specs=[_row_spec(LANES), _row_spec(LANES)],
        out_specs=(tile8(ROW_TILE), tile8(ROW_TILE), tile8(LANES), _const_spec((8, LANES))),
        scratch_shapes=[pltpu.VMEM((8, LANES), jnp.float32)],
        compiler_params=_params(),
        name="rank",
    )(topi, topg)


def _unrolled_rows(n_rows, fn):
    if isinstance(n_rows, int):
        groups, tail_start = n_rows // ROW_UNROLL, n_rows - n_rows % ROW_UNROLL
    else:
        groups = lax.shift_right_logical(n_rows, ROW_UNROLL.bit_length() - 1)
        tail_start = groups * ROW_UNROLL

    def group(gi, carry):
        for lane in range(ROW_UNROLL):
            fn(gi * ROW_UNROLL + lane, lane)
        return carry

    def tail(r, carry):
        fn(r, 0)
        return carry

    lax.fori_loop(0, groups, group, 0)
    lax.fori_loop(tail_start, n_rows, tail, 0)


def _row_span(first_row, n_rows):
    return pl.ds(pl.multiple_of(first_row * ROW_SUBTILES, ROW_SUBTILES),
                 pl.multiple_of(n_rows * ROW_SUBTILES, ROW_SUBTILES))


def _dispatch_kernel(pad_dst_ref, pad_n_ref, nu_ref,
                     src_ref, n_ref, dst_ref, slot_ref, h2_ref,
                     xs_hbm,
                     local, zeros_s, run_sem, pad_sem):
    i = pl.program_id(0)
    last = pl.num_programs(0) - 1
    buf = i % 2
    n_row_tiles = xs_hbm.shape[0] // zeros_s.shape[0]

    def pad_copy(e):
        n = pad_n_ref[e]
        return n, pltpu.make_async_copy(zeros_s.at[_row_span(0, n)], xs_hbm.at[_row_span(pad_dst_ref[e], n)],
                                        pad_sem)

    def unused_tiles(act):
        def body(j, carry):
            rows = zeros_s.shape[0]
            act(pltpu.make_async_copy(
                zeros_s, xs_hbm.at[pl.ds(pl.multiple_of(j * rows, rows), rows)], pad_sem))
            return carry
        lax.fori_loop(nu_ref[0], n_row_tiles, body, 0)

    def run_copy(e):
        n = n_ref[0, 0, e]
        return n, pltpu.make_async_copy(local.at[buf, _row_span(src_ref[0, 0, e], n)],
                                        xs_hbm.at[_row_span(dst_ref[0, 0, e], n)], run_sem.at[buf])

    def wait_runs(b):
        pltpu.make_async_copy(local.at[b], xs_hbm.at[pl.ds(0, local.shape[1])], run_sem.at[b]).wait()

    @pl.when(i == 0)
    def _():
        zeros_s[...] = jnp.zeros_like(zeros_s)
        for e in range(N_EXPERTS):
            n, copy = pad_copy(e)
            pl.when(n > 0)(copy.start)
        unused_tiles(lambda copy: copy.start())

    @pl.when(i >= 2)
    def _():
        wait_runs(buf)

    def place(t, lane):
        row = h2_ref[pl.ds(pl.multiple_of(t * ROW_SUBTILES, ROW_SUBTILES), ROW_SUBTILES), :]
        for k in range(TOP_K):
            at = pl.multiple_of(slot_ref[0, 0, k * ROW_TILE + t] * ROW_SUBTILES, ROW_SUBTILES)
            local[buf, pl.ds(at, ROW_SUBTILES), :] = row
    _unrolled_rows(ROW_TILE, place)

    for e in range(N_EXPERTS):
        n, copy = run_copy(e)
        pl.when(n > 0)(copy.start)

    @pl.when(i == last)
    def _():
        wait_runs(buf)

        @pl.when(last >= 1)
        def _():
            wait_runs(1 - buf)

        for e in range(N_EXPERTS):
            n, copy = pad_copy(e)
            pl.when(n > 0)(copy.wait)
        unused_tiles(lambda copy: copy.wait())


def _run_spec(index_of):
    return pl.BlockSpec((1, 1, LANES), lambda i, *_: (index_of(i), 0, 0), memory_space=pltpu.SMEM)


def _dispatch(pad_dst, pad_n, n_used, run_src, run_n, run_dst, slots, h2, n_slots):
    nt = run_n.shape[0]
    picks = ROW_TILE * TOP_K
    grid_spec = pltpu.PrefetchScalarGridSpec(
        num_scalar_prefetch=3,
        grid=(nt,),
        in_specs=[
            _run_spec(lambda i: i), _run_spec(lambda i: i), _run_spec(lambda i: i),
            pl.BlockSpec((1, 1, picks), lambda i, *_: (i, 0, 0), memory_space=pltpu.SMEM),
            pl.BlockSpec((ROW_TILE * ROW_SUBTILES, LANES), lambda i, *_: (i, 0)),
        ],
        out_specs=pl.BlockSpec(memory_space=pl.ANY),
        scratch_shapes=[
            pltpu.VMEM((2, picks * ROW_SUBTILES, LANES), jnp.float32),
            pltpu.VMEM((MOE_TILE * ROW_SUBTILES, LANES), jnp.float32),
            pltpu.SemaphoreType.DMA((2,)), pltpu.SemaphoreType.DMA,
        ],
    )
    return pl.pallas_call(
        _dispatch_kernel,
        out_shape=jax.ShapeDtypeStruct((n_slots * ROW_SUBTILES, LANES), jnp.float32),
        grid_spec=grid_spec,
        compiler_params=_params(),
        name="dispatch",
    )(pad_dst, pad_n, n_used, run_src, run_n, run_dst, slots, h2)


def _expert_kernel(te_ref, nu_ref, nx_ref, par_ref,
                   xs_ref, bgu_ref, bdn_ref, wgu_hbm, wdn_hbm,
                   ys_ref,
                   wgu_f, wdn_f, wgu_s, wdn_s, wsem):
    i = pl.program_id(0)
    n_used = nu_ref[0]
    expert = te_ref[i]
    buf = par_ref[i]
    expert_changed = jnp.logical_or(i == 0, expert != te_ref[jnp.maximum(i - 1, 0)])

    def weight_copies(e, b):
        return (pltpu.make_async_copy(wgu_hbm.at[e], wgu_f.at[b], wsem.at[0, b]),
                pltpu.make_async_copy(wdn_hbm.at[e], wdn_f.at[b], wsem.at[1, b]))

    @pl.when(jnp.logical_and(i < n_used, expert_changed))
    def _():
        @pl.when(i == 0)
        def _():
            for copy in weight_copies(expert, buf):
                copy.start()

        for copy in weight_copies(expert, buf):
            copy.wait()
        following = nx_ref[i]

        @pl.when(following != expert)
        def _():
            for copy in weight_copies(following, 1 - buf):
                copy.start()

        wgu_s[...] = _bf16(wgu_f[buf])
        wdn_s[...] = _bf16(wdn_f[buf])

    @pl.when(i < n_used)
    def _():
        x = _bf16(_load_row_tiled(xs_ref, (), MOE_TILE))
        gu = _dot(x, wgu_s[...]) + bgu_ref[0]
        gate = jnp.minimum(gu[:, :D_FF], SWIGLU_LIMIT)
        lin = jnp.clip(gu[:, D_FF:], -SWIGLU_LIMIT, SWIGLU_LIMIT)
        act = gate * _sigmoid(SWIGLU_ALPHA * gate) * (lin + 1.0)
        _store_row_tiled(ys_ref, (), _dot(_bf16(act), wdn_s[...]) + bdn_ref[0])

    @pl.when(i >= n_used)
    def _():
        ys_ref[...] = jnp.zeros_like(ys_ref)


def _experts(tile_expert, n_used, next_expert, weight_buf, xs, w_gu, b_gu, w_dn, b_dn):
    n_tiles = tile_expert.shape[0]
    f32, bf16 = jnp.float32, jnp.bfloat16
    tile_rows = MOE_TILE * ROW_SUBTILES
    grid_spec = pltpu.PrefetchScalarGridSpec(
        num_scalar_prefetch=4,
        grid=(n_tiles,),
        in_specs=[
            pl.BlockSpec((tile_rows, LANES), lambda i, te, nu, nx, par: (jnp.minimum(i, nu[0] - 1), 0)),
            pl.BlockSpec((1, 1, 2 * D_FF), lambda i, te, nu, nx, par: (te[i], 0, 0)),
            pl.BlockSpec((1, 1, D_MODEL), lambda i, te, nu, nx, par: (te[i], 0, 0)),
            pl.BlockSpec(memory_space=pl.ANY), pl.BlockSpec(memory_space=pl.ANY),
        ],
        out_specs=pl.BlockSpec((tile_rows, LANES), lambda i, te, nu, nx, par: (i, 0)),
        scratch_shapes=[
            pltpu.VMEM((2, D_MODEL, 2 * D_FF), f32), pltpu.VMEM((2, D_FF, D_MODEL), f32),
            pltpu.VMEM((D_MODEL, 2 * D_FF), bf16), pltpu.VMEM((D_FF, D_MODEL), bf16),
            pltpu.SemaphoreType.DMA((2, 2)),
        ],
    )
    return pl.pallas_call(
        _expert_kernel,
        out_shape=jax.ShapeDtypeStruct(xs.shape, f32),
        grid_spec=grid_spec,
        compiler_params=_params(),
        name="experts",
    )(tile_expert, n_used, next_expert, weight_buf, xs, b_gu.reshape(N_EXPERTS, 1, -1),
      b_dn.reshape(N_EXPERTS, 1, -1), w_gu, w_dn)


def _combine_kernel(n_prompt_tiles,
                    src_ref, n_ref, dst_ref, src_nx_ref, n_nx_ref, dst_nx_ref, slot_ref, gate_ref,
                    x1_ref, gfin_ref, ys_hbm,
                    yp_ref, yo_ref,
                    local, mixed, run_sem):
    i = pl.program_id(0)
    last = pl.num_programs(0) - 1
    buf = i % 2

    def fetch_runs(s_ref, c_ref, d_ref, b):
        for e in range(N_EXPERTS):
            n = c_ref[0, 0, e]
            copy = pltpu.make_async_copy(ys_hbm.at[_row_span(d_ref[0, 0, e], n)],
                                         local.at[b, _row_span(s_ref[0, 0, e], n)], run_sem.at[b])
            pl.when(n > 0)(copy.start)

    @pl.when(i == 0)
    def _():
        fetch_runs(src_ref, n_ref, dst_ref, 0)

    @pl.when(i < last)
    def _():
        fetch_runs(src_nx_ref, n_nx_ref, dst_nx_ref, 1 - buf)

    pltpu.make_async_copy(ys_hbm.at[pl.ds(0, local.shape[1])], local.at[buf], run_sem.at[buf]).wait()

    def blend(t, lane):
        acc = None
        for k in range(TOP_K):
            at = pl.multiple_of(slot_ref[0, 0, k * ROW_TILE + t] * ROW_SUBTILES, ROW_SUBTILES)
            term = gate_ref[0, 0, k * ROW_TILE + t] * local[buf, pl.ds(at, ROW_SUBTILES), :]
            acc = term if acc is None else acc + term
        mixed[pl.ds(pl.multiple_of(t * ROW_SUBTILES, ROW_SUBTILES), ROW_SUBTILES), :] = acc
    _unrolled_rows(ROW_TILE, blend)

    out = _rms(x1_ref[...] + _load_row_tiled(mixed, (), ROW_TILE), gfin_ref[...])

    @pl.when(i < n_prompt_tiles)
    def _():
        yp_ref[...] = out

    @pl.when(i >= n_prompt_tiles)
    def _():
        yo_ref[...] = out


def _combine(run_src, run_n, run_dst, slots, gates, x1, ys, g_final, tp):
    t = x1.shape[0]
    npt = tp // ROW_TILE
    nt = t // ROW_TILE
    f32 = jnp.float32
    picks = ROW_TILE * TOP_K
    nxt = lambda i: jnp.minimum(i + 1, nt - 1)
    pick_spec = pl.BlockSpec((1, 1, picks), lambda i: (i, 0, 0), memory_space=pltpu.SMEM)
    return pl.pallas_call(
        functools.partial(_combine_kernel, npt),
        out_shape=(jax.ShapeDtypeStruct((tp, D_MODEL), f32),
                   jax.ShapeDtypeStruct((t - tp, D_MODEL), f32)),
        grid=(nt,),
        in_specs=[_run_spec(lambda i: i), _run_spec(lambda i: i), _run_spec(lambda i: i),
                  _run_spec(nxt), _run_spec(nxt), _run_spec(nxt), pick_spec, pick_spec,
                  _row_spec(D_MODEL), _const_spec((1, D_MODEL)), pl.BlockSpec(memory_space=pl.ANY)],
        out_specs=(_prompt_spec(D_MODEL, npt), _sample_spec(D_MODEL, npt)),
        scratch_shapes=[pltpu.VMEM((2, picks * ROW_SUBTILES, LANES), f32),
                        pltpu.VMEM((ROW_TILE * ROW_SUBTILES, LANES), f32),
                        pltpu.SemaphoreType.DMA((2,))],
        compiler_params=_params(),
        name="combine",
    )(run_src, run_n, run_dst, run_src, run_n, run_dst, slots, gates, x1, g_final.reshape(1, -1), ys)


def _plan(meta, counts, t):
    nt = t // ROW_TILE
    n_tiles = (t * TOP_K + N_EXPERTS * (MOE_TILE - 1)) // MOE_TILE
    counts = counts[0, :N_EXPERTS]
    tiles_e = (counts + MOE_TILE - 1) // MOE_TILE
    tile_end = jnp.cumsum(tiles_e)
    tile_start = tile_end - tiles_e
    n_used = tile_end[-1]
    first_row = jnp.pad(tile_start * MOE_TILE, (0, LANES - N_EXPERTS))
    meta = meta.reshape(nt, 8, LANES)
    run_dst = meta[:, 0:1, :] + first_row[None, None, :]
    run_n = meta[:, 1:2, :]
    run_src = meta[:, 2:3, :]
    pad_dst = tile_start * MOE_TILE + counts
    pad_n = tiles_e * MOE_TILE - counts
    tile_ids = jnp.arange(n_tiles, dtype=jnp.int32)
    live = jnp.minimum(tile_ids, n_used - 1)
    tile_expert = jnp.sum(tile_end[None, :] <= live[:, None], axis=1).astype(jnp.int32)
    used = tiles_e > 0
    ids = jnp.arange(N_EXPERTS, dtype=jnp.int32)
    later_used = jnp.where(jnp.logical_and(used[None, :], ids[None, :] > ids[:, None]), ids[None, :],
                           N_EXPERTS)
    following = jnp.min(later_used, axis=1)
    following = jnp.where(following < N_EXPERTS, following, ids)
    buf_of = (jnp.cumsum(used.astype(jnp.int32)) - 1) % 2
    return (tile_expert, n_used.reshape(1).astype(jnp.int32), following[tile_expert].astype(jnp.int32),
            buf_of[tile_expert].astype(jnp.int32), run_src, run_n, run_dst,
            pad_dst.astype(jnp.int32), pad_n.astype(jnp.int32), n_tiles * MOE_TILE)


def kernel(x_prompt, x_sample, cache_k, cache_v, g_mix, w_in, b_in, ln_v_g, ln_v_b, w_sp, b_sp,
           attn_sinks, w_pa, w_pb, w_o, g_ffn, w_router, b_router, w_gu, b_gu, w_dn, b_dn, g_final):
    nb, seq, d = x_prompt.shape
    nsb, nnew, _ = x_sample.shape
    tp, ts = nb * seq, nsb * nnew
    t = tp + ts
    xp = x_prompt.reshape(tp, d)
    xs = x_sample.reshape(ts, d)
    pos = np.concatenate([np.tile(np.arange(seq), nb), np.tile(PAST_LEN + np.arange(nnew), nsb)])
    tables = _rope_tables(pos)
    u, vln, v_sgu, q, k, v, kd, vd, ga, gb = _project(
        xp, xs, g_mix[0], w_in[0], b_in[0], ln_v_g[0], ln_v_b[0], tables)
    x1, h2, topi, topg = _mix(
        xp, xs, u, vln, q, kd, vd, cache_k[0], cache_v[0], ga, gb, w_sp[0], b_sp[0], attn_sinks[0],
        w_pa[0], w_pb[0], w_o[0], g_ffn[0], w_router[0], b_router[0], seq)
    slot_t, gate_t, meta, counts = _rank(topi, topg)
    nt = t // ROW_TILE
    picks = lambda a: a.reshape(nt, 8, ROW_TILE)[:, :TOP_K, :].reshape(nt, 1, TOP_K * ROW_TILE)
    slots, gates = picks(slot_t), picks(gate_t)
    (tile_expert, n_used, next_expert, weight_buf, run_src, run_n, run_dst, pad_dst, pad_n,
     n_slots) = _plan(meta, counts, t)
    x_sorted = _dispatch(pad_dst, pad_n, n_used, run_src, run_n, run_dst, slots, h2, n_slots)
    y_sorted = _experts(tile_expert, n_used, next_expert, weight_buf, x_sorted, w_gu[0], b_gu[0],
                        w_dn[0], b_dn[0])
    y_p, y_s = _combine(run_src, run_n, run_dst, slots, gates, x1, y_sorted, g_final, tp)

    keep = min(WINDOW, seq)
    tails = lambda a: jnp.stack([a[(b + 1) * seq - keep:(b + 1) * seq] for b in range(nb)]).reshape(
        nb, keep, N_KV_HEADS, HEAD_DIM)
    kp, vp = tails(k), tails(v)
    ks = k[tp:].reshape(nsb, nnew, N_KV_HEADS, HEAD_DIM)
    vs = v[tp:].reshape(nsb, nnew, N_KV_HEADS, HEAD_DIM)
    return (y_p.reshape(nb, seq, d), y_s.reshape(nsb, nnew, d), kp[None], vp[None], ks[None],
            vs[None], v_sgu.reshape(1, nsb, nnew, SGU_WIDTH))
```

```python
import functools

import numpy as np
import jax
import jax.numpy as jnp
from jax import lax
from jax.experimental import pallas as pl
from jax.experimental.pallas import tpu as pltpu

D_MODEL = 1024
PAST_LEN = 2048
CHUNK = 64
SGU_CHUNK = 128
SGU_GROUPS = 8
SGU_WIDTH = 1024
N_HEADS = 16
N_KV_HEADS = 4
HEAD_DIM = 64
Q_PER_KV = N_HEADS // N_KV_HEADS
WINDOW = 128
ROT_DIM = HEAD_DIM // 4
ROPE_THETA = 500000.0
ATT_W = N_HEADS * HEAD_DIM
KV_W = N_KV_HEADS * HEAD_DIM
N_EXPERTS = 32
TOP_K = 4
D_FF = 1024
SWIGLU_ALPHA = 1.702
SWIGLU_LIMIT = 7.0
NORM_EPS = 1e-5
NEG_INF = -1e30
N_IN = SGU_WIDTH * 2 + ATT_W + KV_W * 2 + D_MODEL * 2

LANES = 128
ROW_TILE = 256
MOE_TILE = 512
ROW_UNROLL = 8
COL_BLOCK = 256
KV_DUP_W = N_KV_HEADS * LANES
KEY_SPAN = WINDOW + CHUNK
VMEM_LIMIT = 56 * 1024 * 1024

_SQRT_HALF = 0.7071067811865476


def _gelu(x):
    z = jnp.abs(x) * _SQRT_HALF
    t = 1.0 / (1.0 + 0.3275911 * z)
    poly = t * (0.254829592 + t * (-0.284496736 + t * (1.421413741
                + t * (-1.453152027 + t * 1.061405429))))
    half_tail = (0.5 * x) * (poly * jnp.exp(-z * z))
    return jnp.where(x >= 0.0, x - half_tail, half_tail)


def _sigmoid(x):
    return 1.0 / (1.0 + jnp.exp(-x))


def _bf16(x):
    return x.astype(jnp.bfloat16)


def _dot(a, b):
    return jnp.dot(a, b, preferred_element_type=jnp.float32)


ROW_SUBTILES = D_MODEL // LANES


def _store_row_tiled(ref, lead, x):
    rows = x.shape[0]
    for s in range(ROW_SUBTILES):
        ref[(*lead, pl.ds(s, rows, stride=ROW_SUBTILES), slice(None))] = x[:, s * LANES:(s + 1) * LANES]


def _load_row_tiled(ref, lead, rows):
    return jnp.concatenate(
        [ref[(*lead, pl.ds(s, rows, stride=ROW_SUBTILES), slice(None))] for s in range(ROW_SUBTILES)],
        axis=1)


def _dot_nt(a, b):
    return lax.dot_general(a, b, (((1,), (1,)), ((), ())), preferred_element_type=jnp.float32)


def _rms(x, g):
    return x * lax.rsqrt(jnp.mean(x * x, axis=-1, keepdims=True) + NORM_EPS) * g


def _lane_lo(rows):
    return lax.broadcasted_iota(jnp.int32, (rows, LANES), 1) < HEAD_DIM


def _dup_heads(kv):
    rows = kv.shape[0]
    lo = _lane_lo(rows)
    out = []
    for j in range(KV_W // LANES):
        blk = kv[:, j * LANES:(j + 1) * LANES]
        swp = pltpu.roll(blk, HEAD_DIM, axis=1)
        out.append(jnp.where(lo, blk, swp))
        out.append(jnp.where(lo, swp, blk))
    return _bf16(jnp.concatenate(out, axis=1))


def _rope_block(zb, cos_b, sin_lo, sin_hi):
    up = pltpu.roll(zb, LANES - ROT_DIM // 2, axis=1)
    dn = pltpu.roll(zb, ROT_DIM // 2, axis=1)
    return zb * cos_b + up * sin_lo + dn * sin_hi


def _proj_kernel(n_prompt_tiles, xp_ref, xs_ref, gmix_ref, w_ref, b_ref, lng_ref, lnb_ref,
                 cos_ref, slo_ref, shi_ref,
                 u_ref, vln_ref, vs_ref, q_ref, k_ref, v_ref, kd_ref, vd_ref, ga_ref, gb_ref):
    i = pl.program_id(0)
    x = jnp.where(i < n_prompt_tiles, xp_ref[...], xs_ref[...])
    h = _bf16(_rms(x, gmix_ref[...]))

    cos_b, sin_lo, sin_hi = cos_ref[...], slo_ref[...], shi_ref[...]
    off_u, off_v, off_q, off_k, off_vv, off_ga, off_gb = (
        int(o) for o in np.cumsum((0, SGU_WIDTH, SGU_WIDTH, ATT_W, KV_W, KV_W, D_MODEL)))

    def z(lo):
        return _dot(h, w_ref[:, lo:lo + COL_BLOCK]) + b_ref[:, lo:lo + COL_BLOCK]

    def cols(j):
        return slice(j * COL_BLOCK, (j + 1) * COL_BLOCK)

    def rope(zb):
        return jnp.concatenate(
            [_rope_block(zb[:, t * LANES:(t + 1) * LANES], cos_b, sin_lo, sin_hi)
             for t in range(COL_BLOCK // LANES)], axis=1)

    gelu_v = []

    def do_u(j):
        u_ref[:, cols(j)] = _bf16(_gelu(z(off_u + j * COL_BLOCK)))

    def do_v(j):
        gelu_v.append(_gelu(z(off_v + j * COL_BLOCK)))

    def do_q(j):
        q_ref[:, cols(j)] = _bf16(rope(z(off_q + j * COL_BLOCK)) * (HEAD_DIM ** -0.5))

    def do_k(j):
        kr = rope(z(off_k))
        k_ref[...] = kr
        kd_ref[...] = _dup_heads(kr)

    def do_vv(j):
        zv = z(off_vv)
        v_ref[...] = zv
        vd_ref[...] = _dup_heads(zv)

    def do_ga(j):
        ga_ref[:, cols(j)] = _bf16(_sigmoid(z(off_ga + j * COL_BLOCK)))

    def do_gb(j):
        gb_ref[:, cols(j)] = _bf16(_sigmoid(z(off_gb + j * COL_BLOCK)))

    order = ((do_v, 0), (do_q, 0), (do_v, 1), (do_q, 1), (do_v, 2), (do_q, 2), (do_v, 3), (do_q, 3),
             (do_u, 0), (do_k, 0), (do_u, 1), (do_vv, 0), (do_u, 2), (do_ga, 0), (do_u, 3), (do_ga, 1),
             (do_ga, 2), (do_ga, 3), (do_gb, 0), (do_gb, 1), (do_gb, 2), (do_gb, 3))
    for fn, j in order:
        fn(j)
    gv = jnp.concatenate(gelu_v, axis=1)
    gc = gv - jnp.mean(gv, axis=-1, keepdims=True)
    var = jnp.mean(gc * gc, axis=-1, keepdims=True)
    vln = gc * lax.rsqrt(var + NORM_EPS) * lng_ref[...] + lnb_ref[...]
    vln_ref[...] = _bf16(vln)
    vs_ref[...] = vln


def _rope_tables(pos):
    half = ROT_DIM // 2
    inv = np.float32(ROPE_THETA) ** (-np.arange(half, dtype=np.float32) * np.float32(2.0) / ROT_DIM)
    ang = pos.astype(np.float32)[:, None] * inv.astype(np.float32)[None, :]
    cos = jnp.asarray(np.cos(ang.astype(np.float64)).astype(np.float32))
    sin = jnp.asarray(np.sin(ang.astype(np.float64)).astype(np.float32))
    n = pos.shape[0]
    ones = jnp.ones((n, HEAD_DIM - ROT_DIM), jnp.float32)
    zeros = jnp.zeros((n, HEAD_DIM - ROT_DIM), jnp.float32)
    zh = jnp.zeros((n, half), jnp.float32)
    cos_h = jnp.concatenate([cos, cos, ones], axis=1)
    slo_h = jnp.concatenate([-sin, zh, zeros], axis=1)
    shi_h = jnp.concatenate([zh, sin, zeros], axis=1)
    rep = LANES // HEAD_DIM
    return (jnp.tile(cos_h, (1, rep)), jnp.tile(slo_h, (1, rep)), jnp.tile(shi_h, (1, rep)))


def _row_spec(width):
    return pl.BlockSpec((ROW_TILE, width), lambda i: (i, 0))


def _const_spec(shape):
    return pl.BlockSpec(shape, lambda i: (0,) * len(shape))


def _prompt_spec(width, n_prompt_tiles):
    return pl.BlockSpec((ROW_TILE, width), lambda i: (jnp.minimum(i, n_prompt_tiles - 1), 0))


def _sample_spec(width, n_prompt_tiles):
    return pl.BlockSpec((ROW_TILE, width), lambda i: (jnp.maximum(i - n_prompt_tiles, 0), 0))


def _params():
    return pltpu.CompilerParams(dimension_semantics=("arbitrary",), vmem_limit_bytes=VMEM_LIMIT)


def _project(xp, xs, g_mix, w_in, b_in, ln_g, ln_b, tables):
    tp, ts = xp.shape[0], xs.shape[0]
    t = tp + ts
    npt = tp // ROW_TILE
    f32, bf16 = jnp.float32, jnp.bfloat16
    out_shape = (
        jax.ShapeDtypeStruct((t, SGU_WIDTH), bf16),
        jax.ShapeDtypeStruct((t, SGU_WIDTH), bf16),
        jax.ShapeDtypeStruct((ts, SGU_WIDTH), f32),
        jax.ShapeDtypeStruct((t, ATT_W), bf16),
        jax.ShapeDtypeStruct((t, KV_W), f32),
        jax.ShapeDtypeStruct((t, KV_W), f32),
        jax.ShapeDtypeStruct((t, KV_DUP_W), bf16),
        jax.ShapeDtypeStruct((t, KV_DUP_W), bf16),
        jax.ShapeDtypeStruct((t, D_MODEL), bf16),
        jax.ShapeDtypeStruct((t, D_MODEL), bf16),
    )
    return pl.pallas_call(
        functools.partial(_proj_kernel, npt),
        out_shape=out_shape,
        grid=(t // ROW_TILE,),
        in_specs=[
            _prompt_spec(D_MODEL, npt), _sample_spec(D_MODEL, npt),
            _const_spec((1, D_MODEL)), _const_spec((D_MODEL, N_IN)),
            _const_spec((1, N_IN)), _const_spec((1, SGU_WIDTH)), _const_spec((1, SGU_WIDTH)),
            _row_spec(LANES), _row_spec(LANES), _row_spec(LANES),
        ],
        out_specs=(
            _row_spec(SGU_WIDTH), _row_spec(SGU_WIDTH), _sample_spec(SGU_WIDTH, npt),
            _row_spec(ATT_W), _row_spec(KV_W), _row_spec(KV_W), _row_spec(KV_DUP_W),
            _row_spec(KV_DUP_W), _row_spec(D_MODEL), _row_spec(D_MODEL),
        ),
        compiler_params=_params(),
        name="proj",
    )(xp, xs, g_mix.reshape(1, -1), w_in.astype(bf16), b_in.reshape(1, -1),
      ln_g.reshape(1, -1), ln_b.reshape(1, -1), *tables)


def _attend(qa, qb, kwin, vwin, sink, valid):
    lo = _lane_lo(CHUNK)
    zero = jnp.zeros_like(qa)
    lhs = jnp.concatenate([jnp.where(lo, qa, zero), jnp.where(lo, zero, qa),
                           jnp.where(lo, qb, zero), jnp.where(lo, zero, qb)], axis=0)
    s = _dot_nt(lhs, kwin)
    if valid is not None:
        s = jnp.where(valid, s, NEG_INF)
    s_a, s_b = s[:, :LANES], s[:, LANES:]
    tail = s_b.shape[1]
    m = jnp.maximum(jnp.max(s, axis=-1, keepdims=True), sink)
    p_a = jnp.exp(s_a - m)
    p_b = jnp.exp(s_b - m[:, :tail])
    denom = (jnp.sum(jnp.concatenate([p_a, p_b], axis=1), axis=-1, keepdims=True)
             + jnp.exp(sink - m))
    inv = 1.0 / denom
    pn = jnp.concatenate([p_a * inv, p_b * inv[:, :tail]], axis=1)
    r = _dot(_bf16(pn), vwin)
    oa = jnp.where(lo, r[0:CHUNK], r[CHUNK:2 * CHUNK])
    ob = jnp.where(lo, r[2 * CHUNK:3 * CHUNK], r[3 * CHUNK:4 * CHUNK])
    return oa, ob


def _top4(logits):
    rows = logits.shape[0]
    lane = lax.broadcasted_iota(jnp.int32, (rows, N_EXPERTS), 1)
    work = logits
    vals, idxs = [], []
    for _ in range(TOP_K):
        m = jnp.max(work, axis=-1, keepdims=True)
        idx = jnp.min(jnp.where(work == m, lane, N_EXPERTS), axis=-1, keepdims=True)
        vals.append(m)
        idxs.append(idx)
        work = jnp.where(lane == idx, -jnp.inf, work)
    exps = [jnp.exp(v - vals[0]) for v in vals]
    denom = exps[0] + exps[1] + exps[2] + exps[3]
    wide = lax.broadcasted_iota(jnp.int32, (rows, LANES), 1)
    topi = jnp.zeros((rows, LANES), jnp.int32)
    topg = jnp.zeros((rows, LANES), jnp.float32)
    for k in range(TOP_K):
        topi = jnp.where(wide == k, idxs[k], topi)
        topg = jnp.where(wide == k, exps[k] / denom, topg)
    return topi, topg


def _mix_kernel(tiles_per_seq, n_prompt_tiles,
                xp_ref, xs_ref, u_ref, vln_ref, q_ref, kd_ref, vd_ref, kdp_ref, vdp_ref,
                ck_ref, cv_ref, ga_ref, gb_ref, wsp_ref, bsp_ref, sink_ref,
                wpa_ref, wpb_ref, wo_ref, gffn_ref, wrh_ref, wrl_ref, br_ref,
                x1_ref, h2_ref, topi_ref, topg_ref,
                a_s, o_s, kwin_s, vwin_s):
    i = pl.program_id(0)
    n_streams = ROW_TILE // CHUNK

    def sgu_rows(r0, rows):
        ri = lax.broadcasted_iota(jnp.int32, (rows, rows), 0) // CHUNK
        ci = lax.broadcasted_iota(jnp.int32, (rows, rows), 1) // CHUNK
        for g in range(SGU_GROUPS):
            cols = slice(g * LANES, (g + 1) * LANES)
            w = _bf16(jnp.where(ci <= ri, wsp_ref[g, :rows, :rows], 0.0))
            sp = _dot(w, vln_ref[r0:r0 + rows, cols]) + bsp_ref[g, :rows, :]
            a_s[r0:r0 + rows, cols] = _bf16(u_ref[r0:r0 + rows, cols].astype(jnp.float32) * sp)

    def attend_rows(r0, kwin_of, valid):
        for g in range(N_KV_HEADS):
            c0 = g * Q_PER_KV * HEAD_DIM
            kwin, vwin = kwin_of(g)
            oa, ob = _attend(q_ref[r0:r0 + CHUNK, c0:c0 + LANES],
                             q_ref[r0:r0 + CHUNK, c0 + LANES:c0 + 2 * LANES],
                             kwin, vwin, sink_ref[g], valid)
            o_s[r0:r0 + CHUNK, c0:c0 + LANES] = _bf16(oa)
            o_s[r0:r0 + CHUNK, c0 + LANES:c0 + 2 * LANES] = _bf16(ob)

    @pl.when(i < n_prompt_tiles)
    def _prompt():
        for c in range(ROW_TILE // SGU_CHUNK):
            sgu_rows(c * SGU_CHUNK, SGU_CHUNK)
        kwin_s[0:WINDOW] = kdp_ref[...]
        kwin_s[WINDOW:WINDOW + ROW_TILE] = kd_ref[...]
        vwin_s[0:WINDOW] = vdp_ref[...]
        vwin_s[WINDOW:WINDOW + ROW_TILE] = vd_ref[...]
        first = (i % tiles_per_seq) == 0
        col = lax.broadcasted_iota(jnp.int32, (1, KEY_SPAN), 1)
        for j in range(ROW_TILE // CHUNK):
            r0 = j * CHUNK
            valid = jnp.logical_or(jnp.logical_not(first), col + r0 >= WINDOW) if r0 < WINDOW else None

            def kwin_of(g, r0=r0):
                cols = slice(g * LANES, (g + 1) * LANES)
                return kwin_s[r0:r0 + KEY_SPAN, cols], vwin_s[r0:r0 + KEY_SPAN, cols]

            attend_rows(r0, kwin_of, valid)

    @pl.when(i >= n_prompt_tiles)
    def _sample():
        for s in range(n_streams):
            r0 = s * CHUNK
            sgu_rows(r0, CHUNK)
            kwin_s[0:WINDOW] = _dup_heads(ck_ref[s])
            kwin_s[WINDOW:KEY_SPAN] = kd_ref[r0:r0 + CHUNK]
            vwin_s[0:WINDOW] = _dup_heads(cv_ref[s])
            vwin_s[WINDOW:KEY_SPAN] = vd_ref[r0:r0 + CHUNK]

            def kwin_of(g):
                cols = slice(g * LANES, (g + 1) * LANES)
                return kwin_s[0:KEY_SPAN, cols], vwin_s[0:KEY_SPAN, cols]

            attend_rows(r0, kwin_of, None)

    x = jnp.where(i < n_prompt_tiles, xp_ref[...], xs_ref[...])
    m = (ga_ref[...].astype(jnp.float32) * _dot(a_s[...], wpa_ref[...])
         + gb_ref[...].astype(jnp.float32) * _dot(o_s[...], wpb_ref[...]))
    x1 = x + _dot(_bf16(m), wo_ref[...])
    x1_ref[...] = x1
    h2 = _rms(x1, gffn_ref[...])
    _store_row_tiled(h2_ref, (), h2)
    hh = _bf16(h2)
    hl = _bf16(h2 - hh.astype(jnp.float32))
    logits = (_dot(hh, wrh_ref[...]) + _dot(hh, wrl_ref[...]) + _dot(hl, wrh_ref[...])) + br_ref[...]
    topi, topg = _top4(logits)
    topi_ref[...] = topi
    topg_ref[...] = topg


def _mix(xp, xs, u, vln, q, kd, vd, cache_k, cache_v, ga, gb, w_sp, b_sp, sinks,
         w_pa, w_pb, w_o, g_ffn, w_router, b_router, seq):
    tp, ts = xp.shape[0], xs.shape[0]
    t = tp + ts
    npt = tp // ROW_TILE
    tiles_per_seq = seq // ROW_TILE
    f32, bf16 = jnp.float32, jnp.bfloat16
    n_streams = ROW_TILE // CHUNK
    win_per_tile = ROW_TILE // WINDOW

    prev_spec = pl.BlockSpec(
        (WINDOW, KV_DUP_W), lambda i: (jnp.maximum(jnp.minimum(i, npt - 1) * win_per_tile - 1, 0), 0))
    cache_spec = pl.BlockSpec(
        (n_streams, WINDOW, KV_W), lambda i: (jnp.maximum(i - npt, 0), 0, 0))
    sink_cols = jnp.broadcast_to(
        jnp.repeat(sinks.astype(f32).reshape(N_KV_HEADS, Q_PER_KV), CHUNK, axis=1)[:, :, None],
        (N_KV_HEADS, Q_PER_KV * CHUNK, LANES))
    wr_hi = w_router.astype(bf16)
    wr_lo = (w_router - wr_hi.astype(f32)).astype(bf16)
    out_shape = (
        jax.ShapeDtypeStruct((t, D_MODEL), f32),
        jax.ShapeDtypeStruct((t * ROW_SUBTILES, LANES), f32),
        jax.ShapeDtypeStruct((t, LANES), jnp.int32),
        jax.ShapeDtypeStruct((t, LANES), f32),
    )
    return pl.pallas_call(
        functools.partial(_mix_kernel, tiles_per_seq, npt),
        out_shape=out_shape,
        grid=(t // ROW_TILE,),
        in_specs=[
            _prompt_spec(D_MODEL, npt), _sample_spec(D_MODEL, npt),
            _row_spec(SGU_WIDTH), _row_spec(SGU_WIDTH), _row_spec(ATT_W),
            _row_spec(KV_DUP_W), _row_spec(KV_DUP_W), prev_spec, prev_spec,
            cache_spec, cache_spec, _row_spec(D_MODEL), _row_spec(D_MODEL),
            _const_spec((SGU_GROUPS, SGU_CHUNK, SGU_CHUNK)), _const_spec((SGU_GROUPS, SGU_CHUNK, LANES)),
            _const_spec((N_KV_HEADS, Q_PER_KV * CHUNK, LANES)),
            _const_spec((SGU_WIDTH, D_MODEL)), _const_spec((ATT_W, D_MODEL)),
            _const_spec((D_MODEL, D_MODEL)), _const_spec((1, D_MODEL)),
            _const_spec((D_MODEL, N_EXPERTS)), _const_spec((D_MODEL, N_EXPERTS)),
            _const_spec((1, N_EXPERTS)),
        ],
        out_specs=(_row_spec(D_MODEL),
                   pl.BlockSpec((ROW_TILE * ROW_SUBTILES, LANES), lambda i: (i, 0)),
                   _row_spec(LANES), _row_spec(LANES)),
        scratch_shapes=[
            pltpu.VMEM((ROW_TILE, SGU_WIDTH), bf16), pltpu.VMEM((ROW_TILE, ATT_W), bf16),
            pltpu.VMEM((WINDOW + ROW_TILE, KV_DUP_W), bf16),
            pltpu.VMEM((WINDOW + ROW_TILE, KV_DUP_W), bf16),
        ],
        compiler_params=_params(),
        name="mix",
    )(xp, xs, u, vln, q, kd, vd, kd, vd,
      cache_k.reshape(-1, WINDOW, KV_W), cache_v.reshape(-1, WINDOW, KV_W), ga, gb,
      w_sp, jnp.broadcast_to(b_sp[:, :, None], (SGU_GROUPS, SGU_CHUNK, LANES)), sink_cols,
      w_pa.astype(bf16), w_pb.astype(bf16), w_o.astype(bf16),
      g_ffn.reshape(1, -1), wr_hi, wr_lo, b_router.reshape(1, -1))


def _rank_kernel(topi_ref, topg_ref, lst_ref, gt_ref, meta_ref, count_ref, carry_s):
    i = pl.program_id(0)

    @pl.when(i == 0)
    def _():
        carry_s[...] = jnp.zeros_like(carry_s)

    topi = topi_ref[...]
    lane = lax.broadcasted_iota(jnp.int32, (ROW_TILE, LANES), 1)
    onehot = jnp.zeros((ROW_TILE, LANES), jnp.float32)
    for k in range(TOP_K):
        onehot = jnp.where(lane == topi[:, k:k + 1], 1.0, onehot)
    r = lax.broadcasted_iota(jnp.int32, (ROW_TILE, ROW_TILE), 0)
    c = lax.broadcasted_iota(jnp.int32, (ROW_TILE, ROW_TILE), 1)
    below = _bf16(jnp.where(c < r, 1.0, 0.0))
    in_tile = _dot(below, _bf16(onehot))
    count = jnp.sum(onehot, axis=0, keepdims=True)
    er = lax.broadcasted_iota(jnp.int32, (LANES, LANES), 0)
    ec = lax.broadcasted_iota(jnp.int32, (LANES, LANES), 1)
    before = _bf16(jnp.where(er < ec, 1.0, 0.0))
    start = _dot(_bf16(jnp.broadcast_to(count, (8, LANES))), before)[0:1, :]
    local = in_tile + start
    slot = jnp.full((ROW_TILE, LANES), -1.0, jnp.float32)
    for k in range(TOP_K):
        sel = jnp.sum(jnp.where(lane == topi[:, k:k + 1], local, 0.0), axis=-1, keepdims=True)
        slot = jnp.where(lane == k, sel, slot)
    lst_ref[...] = slot.T[0:8, :].astype(jnp.int32)
    gt_ref[...] = topg_ref[...].T[0:8, :]
    row = lax.broadcasted_iota(jnp.int32, (8, LANES), 0)
    meta = jnp.where(row == 0, carry_s[...], jnp.where(row == 1, count, jnp.where(row == 2, start, 0.0)))
    meta_ref[...] = meta.astype(jnp.int32)
    carry_s[...] = carry_s[...] + count
    count_ref[...] = carry_s[...].astype(jnp.int32)


def _rank(topi, topg):
    t = topi.shape[0]
    nt = t // ROW_TILE
    tile8 = lambda width: pl.BlockSpec((8, width), lambda i: (i, 0))
    return pl.pallas_call(
        _rank_kernel,
        out_shape=(jax.ShapeDtypeStruct((nt * 8, ROW_TILE), jnp.int32),
                   jax.ShapeDtypeStruct((nt * 8, ROW_TILE), jnp.float32),
                   jax.ShapeDtypeStruct((nt * 8, LANES), jnp.int32),
                   jax.ShapeDtypeStruct((8, LANES), jnp.int32)),
        grid=(nt,),
        in_specs=[_row_spec(LANES), _row_spec(LANES)],
        out_specs=(tile8(ROW_TILE), tile8(ROW_TILE), tile8(LANES), _const_spec((8, LANES))),
        scratch_shapes=[pltpu.VMEM((8, LANES), jnp.float32)],
        compiler_params=_params(),
        name="rank",
    )(topi, topg)


def _unrolled_rows(n_rows, fn):
    if isinstance(n_rows, int):
        groups, tail_start = n_rows // ROW_UNROLL, n_rows - n_rows % ROW_UNROLL
    else:
        groups = lax.shift_right_logical(n_rows, ROW_UNROLL.bit_length() - 1)
        tail_start = groups * ROW_UNROLL

    def group(gi, carry):
        for lane in range(ROW_UNROLL):
            fn(gi * ROW_UNROLL + lane, lane)
        return carry

    def tail(r, carry):
        fn(r, 0)
        return carry

    lax.fori_loop(0, groups, group, 0)
    lax.fori_loop(tail_start, n_rows, tail, 0)


def _row_span(first_row, n_rows):
    return pl.ds(pl.multiple_of(first_row * ROW_SUBTILES, ROW_SUBTILES),
                 pl.multiple_of(n_rows * ROW_SUBTILES, ROW_SUBTILES))


def _dispatch_kernel(pad_dst_ref, pad_n_ref, nu_ref,
                     src_ref, n_ref, dst_ref, slot_ref, h2_ref,
                     xs_hbm,
                     local, zeros_s, run_sem, pad_sem):
    i = pl.program_id(0)
    last = pl.num_programs(0) - 1
    buf = i % 2
    n_row_tiles = xs_hbm.shape[0] // zeros_s.shape[0]

    def pad_copy(e):
        n = pad_n_ref[e]
        return n, pltpu.make_async_copy(zeros_s.at[_row_span(0, n)], xs_hbm.at[_row_span(pad_dst_ref[e], n)],
                                        pad_sem)

    def unused_tiles(act):
        def body(j, carry):
            rows = zeros_s.shape[0]
            act(pltpu.make_async_copy(
                zeros_s, xs_hbm.at[pl.ds(pl.multiple_of(j * rows, rows), rows)], pad_sem))
            return carry
        lax.fori_loop(nu_ref[0], n_row_tiles, body, 0)

    def run_copy(e):
        n = n_ref[0, 0, e]
        return n, pltpu.make_async_copy(local.at[buf, _row_span(src_ref[0, 0, e], n)],
                                        xs_hbm.at[_row_span(dst_ref[0, 0, e], n)], run_sem.at[buf])

    def wait_runs(b):
        pltpu.make_async_copy(local.at[b], xs_hbm.at[pl.ds(0, local.shape[1])], run_sem.at[b]).wait()

    @pl.when(i == 0)
    def _():
        zeros_s[...] = jnp.zeros_like(zeros_s)
        for e in range(N_EXPERTS):
            n, copy = pad_copy(e)
            pl.when(n > 0)(copy.start)
        unused_tiles(lambda copy: copy.start())

    @pl.when(i >= 2)
    def _():
        wait_runs(buf)

    def place(t, lane):
        row = h2_ref[pl.ds(pl.multiple_of(t * ROW_SUBTILES, ROW_SUBTILES), ROW_SUBTILES), :]
        for k in range(TOP_K):
            at = pl.multiple_of(slot_ref[0, 0, k * ROW_TILE + t] * ROW_SUBTILES, ROW_SUBTILES)
            local[buf, pl.ds(at, ROW_SUBTILES), :] = row
    _unrolled_rows(ROW_TILE, place)

    for e in range(N_EXPERTS):
        n, copy = run_copy(e)
        pl.when(n > 0)(copy.start)

    @pl.when(i == last)
    def _():
        wait_runs(buf)

        @pl.when(last >= 1)
        def _():
            wait_runs(1 - buf)

        for e in range(N_EXPERTS):
            n, copy = pad_copy(e)
            pl.when(n > 0)(copy.wait)
        unused_tiles(lambda copy: copy.wait())


def _run_spec(index_of):
    return pl.BlockSpec((1, 1, LANES), lambda i, *_: (index_of(i), 0, 0), memory_space=pltpu.SMEM)


def _dispatch(pad_dst, pad_n, n_used, run_src, run_n, run_dst, slots, h2, n_slots):
    nt = run_n.shape[0]
    picks = ROW_TILE * TOP_K
    grid_spec = pltpu.PrefetchScalarGridSpec(
        num_scalar_prefetch=3,
        grid=(nt,),
        in_specs=[
            _run_spec(lambda i: i), _run_spec(lambda i: i), _run_spec(lambda i: i),
            pl.BlockSpec((1, 1, picks), lambda i, *_: (i, 0, 0), memory_space=pltpu.SMEM),
            pl.BlockSpec((ROW_TILE * ROW_SUBTILES, LANES), lambda i, *_: (i, 0)),
        ],
        out_specs=pl.BlockSpec(memory_space=pl.ANY),
        scratch_shapes=[
            pltpu.VMEM((2, picks * ROW_SUBTILES, LANES), jnp.float32),
            pltpu.VMEM((MOE_TILE * ROW_SUBTILES, LANES), jnp.float32),
            pltpu.SemaphoreType.DMA((2,)), pltpu.SemaphoreType.DMA,
        ],
    )
    return pl.pallas_call(
        _dispatch_kernel,
        out_shape=jax.ShapeDtypeStruct((n_slots * ROW_SUBTILES, LANES), jnp.float32),
        grid_spec=grid_spec,
        compiler_params=_params(),
        name="dispatch",
    )(pad_dst, pad_n, n_used, run_src, run_n, run_dst, slots, h2)


def _expert_kernel(te_ref, nu_ref, nx_ref, par_ref,
                   xs_ref, bgu_ref, bdn_ref, wgu_hbm, wdn_hbm,
                   ys_ref,
                   wgu_f, wdn_f, wgu_s, wdn_s, wsem):
    i = pl.program_id(0)
    n_used = nu_ref[0]
    expert = te_ref[i]
    buf = par_ref[i]
    expert_changed = jnp.logical_or(i == 0, expert != te_ref[jnp.maximum(i - 1, 0)])

    def weight_copies(e, b):
        return (pltpu.make_async_copy(wgu_hbm.at[e], wgu_f.at[b], wsem.at[0, b]),
                pltpu.make_async_copy(wdn_hbm.at[e], wdn_f.at[b], wsem.at[1, b]))

    @pl.when(jnp.logical_and(i < n_used, expert_changed))
    def _():
        @pl.when(i == 0)
        def _():
            for copy in weight_copies(expert, buf):
                copy.start()

        for copy in weight_copies(expert, buf):
            copy.wait()
        following = nx_ref[i]

        @pl.when(following != expert)
        def _():
            for copy in weight_copies(following, 1 - buf):
                copy.start()

        wgu_s[...] = _bf16(wgu_f[buf])
        wdn_s[...] = _bf16(wdn_f[buf])

    @pl.when(i < n_used)
    def _():
        x = _bf16(_load_row_tiled(xs_ref, (), MOE_TILE))
        gu = _dot(x, wgu_s[...]) + bgu_ref[0]
        gate = jnp.minimum(gu[:, :D_FF], SWIGLU_LIMIT)
        lin = jnp.clip(gu[:, D_FF:], -SWIGLU_LIMIT, SWIGLU_LIMIT)
        act = gate * _sigmoid(SWIGLU_ALPHA * gate) * (lin + 1.0)
        _store_row_tiled(ys_ref, (), _dot(_bf16(act), wdn_s[...]) + bdn_ref[0])

    @pl.when(i >= n_used)
    def _():
        ys_ref[...] = jnp.zeros_like(ys_ref)


def _experts(tile_expert, n_used, next_expert, weight_buf, xs, w_gu, b_gu, w_dn, b_dn):
    n_tiles = tile_expert.shape[0]
    f32, bf16 = jnp.float32, jnp.bfloat16
    tile_rows = MOE_TILE * ROW_SUBTILES
    grid_spec = pltpu.PrefetchScalarGridSpec(
        num_scalar_prefetch=4,
        grid=(n_tiles,),
        in_specs=[
            pl.BlockSpec((tile_rows, LANES), lambda i, te, nu, nx, par: (jnp.minimum(i, nu[0] - 1), 0)),
            pl.BlockSpec((1, 1, 2 * D_FF), lambda i, te, nu, nx, par: (te[i], 0, 0)),
            pl.BlockSpec((1, 1, D_MODEL), lambda i, te, nu, nx, par: (te[i], 0, 0)),
            pl.BlockSpec(memory_space=pl.ANY), pl.BlockSpec(memory_space=pl.ANY),
        ],
        out_specs=pl.BlockSpec((tile_rows, LANES), lambda i, te, nu, nx, par: (i, 0)),
        scratch_shapes=[
            pltpu.VMEM((2, D_MODEL, 2 * D_FF), f32), pltpu.VMEM((2, D_FF, D_MODEL), f32),
            pltpu.VMEM((D_MODEL, 2 * D_FF), bf16), pltpu.VMEM((D_FF, D_MODEL), bf16),
            pltpu.SemaphoreType.DMA((2, 2)),
        ],
    )
    return pl.pallas_call(
        _expert_kernel,
        out_shape=jax.ShapeDtypeStruct(xs.shape, f32),
        grid_spec=grid_spec,
        compiler_params=_params(),
        name="experts",
    )(tile_expert, n_used, next_expert, weight_buf, xs, b_gu.reshape(N_EXPERTS, 1, -1),
      b_dn.reshape(N_EXPERTS, 1, -1), w_gu, w_dn)


def _combine_kernel(n_prompt_tiles,
                    src_ref, n_ref, dst_ref, src_nx_ref, n_nx_ref, dst_nx_ref, slot_ref, gate_ref,
                    x1_ref, gfin_ref, ys_hbm,
                    yp_ref, yo_ref,
                    local, mixed, run_sem):
    i = pl.program_id(0)
    last = pl.num_programs(0) - 1
    buf = i % 2

    def fetch_runs(s_ref, c_ref, d_ref, b):
        for e in range(N_EXPERTS):
            n = c_ref[0, 0, e]
            copy = pltpu.make_async_copy(ys_hbm.at[_row_span(d_ref[0, 0, e], n)],
                                         local.at[b, _row_span(s_ref[0, 0, e], n)], run_sem.at[b])
            pl.when(n > 0)(copy.start)

    @pl.when(i == 0)
    def _():
        fetch_runs(src_ref, n_ref, dst_ref, 0)

    @pl.when(i < last)
    def _():
        fetch_runs(src_nx_ref, n_nx_ref, dst_nx_ref, 1 - buf)

    pltpu.make_async_copy(ys_hbm.at[pl.ds(0, local.shape[1])], local.at[buf], run_sem.at[buf]).wait()

    def blend(t, lane):
        acc = None
        for k in range(TOP_K):
            at = pl.multiple_of(slot_ref[0, 0, k * ROW_TILE + t] * ROW_SUBTILES, ROW_SUBTILES)
            term = gate_ref[0, 0, k * ROW_TILE + t] * local[buf, pl.ds(at, ROW_SUBTILES), :]
            acc = term if acc is None else acc + term
        mixed[pl.ds(pl.multiple_of(t * ROW_SUBTILES, ROW_SUBTILES), ROW_SUBTILES), :] = acc
    _unrolled_rows(ROW_TILE, blend)

    out = _rms(x1_ref[...] + _load_row_tiled(mixed, (), ROW_TILE), gfin_ref[...])

    @pl.when(i < n_prompt_tiles)
    def _():
        yp_ref[...] = out

    @pl.when(i >= n_prompt_tiles)
    def _():
        yo_ref[...] = out


def _combine(run_src, run_n, run_dst, slots, gates, x1, ys, g_final, tp):
    t = x1.shape[0]
    npt = tp // ROW_TILE
    nt = t // ROW_TILE
    f32 = jnp.float32
    picks = ROW_TILE * TOP_K
    nxt = lambda i: jnp.minimum(i + 1, nt - 1)
    pick_spec = pl.BlockSpec((1, 1, picks), lambda i: (i, 0, 0), memory_space=pltpu.SMEM)
    return pl.pallas_call(
        functools.partial(_combine_kernel, npt),
        out_shape=(jax.ShapeDtypeStruct((tp, D_MODEL), f32),
                   jax.ShapeDtypeStruct((t - tp, D_MODEL), f32)),
        grid=(nt,),
        in_specs=[_run_spec(lambda i: i), _run_spec(lambda i: i), _run_spec(lambda i: i),
                  _run_spec(nxt), _run_spec(nxt), _run_spec(nxt), pick_spec, pick_spec,
                  _row_spec(D_MODEL), _const_spec((1, D_MODEL)), pl.BlockSpec(memory_space=pl.ANY)],
        out_specs=(_prompt_spec(D_MODEL, npt), _sample_spec(D_MODEL, npt)),
        scratch_shapes=[pltpu.VMEM((2, picks * ROW_SUBTILES, LANES), f32),
                        pltpu.VMEM((ROW_TILE * ROW_SUBTILES, LANES), f32),
                        pltpu.SemaphoreType.DMA((2,))],
        compiler_params=_params(),
        name="combine",
    )(run_src, run_n, run_dst, run_src, run_n, run_dst, slots, gates, x1, g_final.reshape(1, -1), ys)


def _plan(meta, counts, t):
    nt = t // ROW_TILE
    n_tiles = (t * TOP_K + N_EXPERTS * (MOE_TILE - 1)) // MOE_TILE
    counts = counts[0, :N_EXPERTS]
    tiles_e = (counts + MOE_TILE - 1) // MOE_TILE
    tile_end = jnp.cumsum(tiles_e)
    tile_start = tile_end - tiles_e
    n_used = tile_end[-1]
    first_row = jnp.pad(tile_start * MOE_TILE, (0, LANES - N_EXPERTS))
    meta = meta.reshape(nt, 8, LANES)
    run_dst = meta[:, 0:1, :] + first_row[None, None, :]
    run_n = meta[:, 1:2, :]
    run_src = meta[:, 2:3, :]
    pad_dst = tile_start * MOE_TILE + counts
    pad_n = tiles_e * MOE_TILE - counts
    tile_ids = jnp.arange(n_tiles, dtype=jnp.int32)
    live = jnp.minimum(tile_ids, n_used - 1)
    tile_expert = jnp.sum(tile_end[None, :] <= live[:, None], axis=1).astype(jnp.int32)
    used = tiles_e > 0
    ids = jnp.arange(N_EXPERTS, dtype=jnp.int32)
    later_used = jnp.where(jnp.logical_and(used[None, :], ids[None, :] > ids[:, None]), ids[None, :],
                           N_EXPERTS)
    following = jnp.min(later_used, axis=1)
    following = jnp.where(following < N_EXPERTS, following, ids)
    buf_of = (jnp.cumsum(used.astype(jnp.int32)) - 1) % 2
    return (tile_expert, n_used.reshape(1).astype(jnp.int32), following[tile_expert].astype(jnp.int32),
            buf_of[tile_expert].astype(jnp.int32), run_src, run_n, run_dst,
            pad_dst.astype(jnp.int32), pad_n.astype(jnp.int32), n_tiles * MOE_TILE)


def kernel(x_prompt, x_sample, cache_k, cache_v, g_mix, w_in, b_in, ln_v_g, ln_v_b, w_sp, b_sp,
           attn_sinks, w_pa, w_pb, w_o, g_ffn, w_router, b_router, w_gu, b_gu, w_dn, b_dn, g_final):
    nb, seq, d = x_prompt.shape
    nsb, nnew, _ = x_sample.shape
    tp, ts = nb * seq, nsb * nnew
    t = tp + ts
    xp = x_prompt.reshape(tp, d)
    xs = x_sample.reshape(ts, d)
    pos = np.concatenate([np.tile(np.arange(seq), nb), np.tile(PAST_LEN + np.arange(nnew), nsb)])
    tables = _rope_tables(pos)
    u, vln, v_sgu, q, k, v, kd, vd, ga, gb = _project(
        xp, xs, g_mix[0], w_in[0], b_in[0], ln_v_g[0], ln_v_b[0], tables)
    x1, h2, topi, topg = _mix(
        xp, xs, u, vln, q, kd, vd, cache_k[0], cache_v[0], ga, gb, w_sp[0], b_sp[0], attn_sinks[0],
        w_pa[0], w_pb[0], w_o[0], g_ffn[0], w_router[0], b_router[0], seq)
    slot_t, gate_t, meta, counts = _rank(topi, topg)
    nt = t // ROW_TILE
    picks = lambda a: a.reshape(nt, 8, ROW_TILE)[:, :TOP_K, :].reshape(nt, 1, TOP_K * ROW_TILE)
    slots, gates = picks(slot_t), picks(gate_t)
    (tile_expert, n_used, next_expert, weight_buf, run_src, run_n, run_dst, pad_dst, pad_n,
     n_slots) = _plan(meta, counts, t)
    x_sorted = _dispatch(pad_dst, pad_n, n_used, run_src, run_n, run_dst, slots, h2, n_slots)
    y_sorted = _experts(tile_expert, n_used, next_expert, weight_buf, x_sorted, w_gu[0], b_gu[0],
                        w_dn[0], b_dn[0])
    y_p, y_s = _combine(run_src, run_n, run_dst, slots, gates, x1, y_sorted, g_final, tp)

    keep = min(WINDOW, seq)
    tails = lambda a: jnp.stack([a[(b + 1) * seq - keep:(b + 1) * seq] for b in range(nb)]).reshape(
        nb, keep, N_KV_HEADS, HEAD_DIM)
    kp, vp = tails(k), tails(v)
    ks = k[tp:].reshape(nsb, nnew, N_KV_HEADS, HEAD_DIM)
    vs = v[tp:].reshape(nsb, nnew, N_KV_HEADS, HEAD_DIM)
    return (y_p.reshape(nb, seq, d), y_s.reshape(nsb, nnew, d), kp[None], vp[None], ks[None],
            vs[None], v_sgu.reshape(1, nsb, nnew, SGU_WIDTH))
```

```python
import functools

import numpy as np
import jax
import jax.numpy as jnp
from jax import lax
from jax.experimental import pallas as pl
from jax.experimental.pallas import tpu as pltpu

D_MODEL = 1024
PAST_LEN = 2048
CHUNK = 64
SGU_CHUNK = 128
SGU_GROUPS = 8
SGU_WIDTH = 1024
N_HEADS = 16
N_KV_HEADS = 4
HEAD_DIM = 64
Q_PER_KV = N_HEADS // N_KV_HEADS
WINDOW = 128
ROT_DIM = HEAD_DIM // 4
ROPE_THETA = 500000.0
ATT_W = N_HEADS * HEAD_DIM
KV_W = N_KV_HEADS * HEAD_DIM
N_EXPERTS = 32
TOP_K = 4
D_FF = 1024
SWIGLU_ALPHA = 1.702
SWIGLU_LIMIT = 7.0
NORM_EPS = 1e-5
NEG_INF = -1e30
N_IN = SGU_WIDTH * 2 + ATT_W + KV_W * 2 + D_MODEL * 2

LANES = 128
ROW_TILE = 256
MOE_TILE = 512
ROW_UNROLL = 8
COL_BLOCK = 256
KV_DUP_W = N_KV_HEADS * LANES
KEY_SPAN = WINDOW + CHUNK
VMEM_LIMIT = 56 * 1024 * 1024

_SQRT_HALF = 0.7071067811865476


def _gelu(x):
    z = jnp.abs(x) * _SQRT_HALF
    t = 1.0 / (1.0 + 0.3275911 * z)
    poly = t * (0.254829592 + t * (-0.284496736 + t * (1.421413741
                + t * (-1.453152027 + t * 1.061405429))))
    half_tail = (0.5 * x) * (poly * jnp.exp(-z * z))
    return jnp.where(x >= 0.0, x - half_tail, half_tail)


def _sigmoid(x):
    return 1.0 / (1.0 + jnp.exp(-x))


def _bf16(x):
    return x.astype(jnp.bfloat16)


def _dot(a, b):
    return jnp.dot(a, b, preferred_element_type=jnp.float32)


ROW_SUBTILES = D_MODEL // LANES


def _store_row_tiled(ref, lead, x):
    rows = x.shape[0]
    for s in range(ROW_SUBTILES):
        ref[(*lead, pl.ds(s, rows, stride=ROW_SUBTILES), slice(None))] = x[:, s * LANES:(s + 1) * LANES]


def _load_row_tiled(ref, lead, rows):
    return jnp.concatenate(
        [ref[(*lead, pl.ds(s, rows, stride=ROW_SUBTILES), slice(None))] for s in range(ROW_SUBTILES)],
        axis=1)


def _dot_nt(a, b):
    return lax.dot_general(a, b, (((1,), (1,)), ((), ())), preferred_element_type=jnp.float32)


def _rms(x, g):
    return x * lax.rsqrt(jnp.mean(x * x, axis=-1, keepdims=True) + NORM_EPS) * g


def _lane_lo(rows):
    return lax.broadcasted_iota(jnp.int32, (rows, LANES), 1) < HEAD_DIM


def _dup_heads(kv):
    rows = kv.shape[0]
    lo = _lane_lo(rows)
    out = []
    for j in range(KV_W // LANES):
        blk = kv[:, j * LANES:(j + 1) * LANES]
        swp = pltpu.roll(blk, HEAD_DIM, axis=1)
        out.append(jnp.where(lo, blk, swp))
        out.append(jnp.where(lo, swp, blk))
    return _bf16(jnp.concatenate(out, axis=1))


def _rope_block(zb, cos_b, sin_lo, sin_hi):
    up = pltpu.roll(zb, LANES - ROT_DIM // 2, axis=1)
    dn = pltpu.roll(zb, ROT_DIM // 2, axis=1)
    return zb * cos_b + up * sin_lo + dn * sin_hi


def _proj_kernel(n_prompt_tiles, xp_ref, xs_ref, gmix_ref, w_ref, b_ref, lng_ref, lnb_ref,
                 cos_ref, slo_ref, shi_ref,
                 u_ref, vln_ref, vs_ref, q_ref, k_ref, v_ref, kd_ref, vd_ref, ga_ref, gb_ref):
    i = pl.program_id(0)
    x = jnp.where(i < n_prompt_tiles, xp_ref[...], xs_ref[...])
    h = _bf16(_rms(x, gmix_ref[...]))

    cos_b, sin_lo, sin_hi = cos_ref[...], slo_ref[...], shi_ref[...]
    off_u, off_v, off_q, off_k, off_vv, off_ga, off_gb = (
        int(o) for o in np.cumsum((0, SGU_WIDTH, SGU_WIDTH, ATT_W, KV_W, KV_W, D_MODEL)))

    def z(lo):
        return _dot(h, w_ref[:, lo:lo + COL_BLOCK]) + b_ref[:, lo:lo + COL_BLOCK]

    def cols(j):
        return slice(j * COL_BLOCK, (j + 1) * COL_BLOCK)

    def rope(zb):
        return jnp.concatenate(
            [_rope_block(zb[:, t * LANES:(t + 1) * LANES], cos_b, sin_lo, sin_hi)
             for t in range(COL_BLOCK // LANES)], axis=1)

    gelu_v = []

    def do_u(j):
        u_ref[:, cols(j)] = _bf16(_gelu(z(off_u + j * COL_BLOCK)))

    def do_v(j):
        gelu_v.append(_gelu(z(off_v + j * COL_BLOCK)))

    def do_q(j):
        q_ref[:, cols(j)] = _bf16(rope(z(off_q + j * COL_BLOCK)) * (HEAD_DIM ** -0.5))

    def do_k(j):
        kr = rope(z(off_k))
        k_ref[...] = kr
        kd_ref[...] = _dup_heads(kr)

    def do_vv(j):
        zv = z(off_vv)
        v_ref[...] = zv
        vd_ref[...] = _dup_heads(zv)

    def do_ga(j):
        ga_ref[:, cols(j)] = _bf16(_sigmoid(z(off_ga + j * COL_BLOCK)))

    def do_gb(j):
        gb_ref[:, cols(j)] = _bf16(_sigmoid(z(off_gb + j * COL_BLOCK)))

    order = ((do_v, 0), (do_q, 0), (do_v, 1), (do_q, 1), (do_v, 2), (do_q, 2), (do_v, 3), (do_q, 3),
             (do_u, 0), (do_k, 0), (do_u, 1), (do_vv, 0), (do_u, 2), (do_ga, 0), (do_u, 3), (do_ga, 1),
             (do_ga, 2), (do_ga, 3), (do_gb, 0), (do_gb, 1), (do_gb, 2), (do_gb, 3))
    for fn, j in order:
        fn(j)
    gv = jnp.concatenate(gelu_v, axis=1)
    gc = gv - jnp.mean(gv, axis=-1, keepdims=True)
    var = jnp.mean(gc * gc, axis=-1, keepdims=True)
    vln = gc * lax.rsqrt(var + NORM_EPS) * lng_ref[...] + lnb_ref[...]
    vln_ref[...] = _bf16(vln)
    vs_ref[...] = vln


def _rope_tables(pos):
    half = ROT_DIM // 2
    inv = np.float32(ROPE_THETA) ** (-np.arange(half, dtype=np.float32) * np.float32(2.0) / ROT_DIM)
    ang = pos.astype(np.float32)[:, None] * inv.astype(np.float32)[None, :]
    cos = jnp.asarray(np.cos(ang.astype(np.float64)).astype(np.float32))
    sin = jnp.asarray(np.sin(ang.astype(np.float64)).astype(np.float32))
    n = pos.shape[0]
    ones = jnp.ones((n, HEAD_DIM - ROT_DIM), jnp.float32)
    zeros = jnp.zeros((n, HEAD_DIM - ROT_DIM), jnp.float32)
    zh = jnp.zeros((n, half), jnp.float32)
    cos_h = jnp.concatenate([cos, cos, ones], axis=1)
    slo_h = jnp.concatenate([-sin, zh, zeros], axis=1)
    shi_h = jnp.concatenate([zh, sin, zeros], axis=1)
    rep = LANES // HEAD_DIM
    return (jnp.tile(cos_h, (1, rep)), jnp.tile(slo_h, (1, rep)), jnp.tile(shi_h, (1, rep)))


def _row_spec(width):
    return pl.BlockSpec((ROW_TILE, width), lambda i: (i, 0))


def _const_spec(shape):
    return pl.BlockSpec(shape, lambda i: (0,) * len(shape))


def _prompt_spec(width, n_prompt_tiles):
    return pl.BlockSpec((ROW_TILE, width), lambda i: (jnp.minimum(i, n_prompt_tiles - 1), 0))


def _sample_spec(width, n_prompt_tiles):
    return pl.BlockSpec((ROW_TILE, width), lambda i: (jnp.maximum(i - n_prompt_tiles, 0), 0))


def _params():
    return pltpu.CompilerParams(dimension_semantics=("arbitrary",), vmem_limit_bytes=VMEM_LIMIT)


def _project(xp, xs, g_mix, w_in, b_in, ln_g, ln_b, tables):
    tp, ts = xp.shape[0], xs.shape[0]
    t = tp + ts
    npt = tp // ROW_TILE
    f32, bf16 = jnp.float32, jnp.bfloat16
    out_shape = (
        jax.ShapeDtypeStruct((t, SGU_WIDTH), bf16),
        jax.ShapeDtypeStruct((t, SGU_WIDTH), bf16),
        jax.ShapeDtypeStruct((ts, SGU_WIDTH), f32),
        jax.ShapeDtypeStruct((t, ATT_W), bf16),
        jax.ShapeDtypeStruct((t, KV_W), f32),
        jax.ShapeDtypeStruct((t, KV_W), f32),
        jax.ShapeDtypeStruct((t, KV_DUP_W), bf16),
        jax.ShapeDtypeStruct((t, KV_DUP_W), bf16),
        jax.ShapeDtypeStruct((t, D_MODEL), bf16),
        jax.ShapeDtypeStruct((t, D_MODEL), bf16),
    )
    return pl.pallas_call(
        functools.partial(_proj_kernel, npt),
        out_shape=out_shape,
        grid=(t // ROW_TILE,),
        in_specs=[
            _prompt_spec(D_MODEL, npt), _sample_spec(D_MODEL, npt),
            _const_spec((1, D_MODEL)), _const_spec((D_MODEL, N_IN)),
            _const_spec((1, N_IN)), _const_spec((1, SGU_WIDTH)), _const_spec((1, SGU_WIDTH)),
            _row_spec(LANES), _row_spec(LANES), _row_spec(LANES),
        ],
        out_specs=(
            _row_spec(SGU_WIDTH), _row_spec(SGU_WIDTH), _sample_spec(SGU_WIDTH, npt),
            _row_spec(ATT_W), _row_spec(KV_W), _row_spec(KV_W), _row_spec(KV_DUP_W),
            _row_spec(KV_DUP_W), _row_spec(D_MODEL), _row_spec(D_MODEL),
        ),
        compiler_params=_params(),
        name="proj",
    )(xp, xs, g_mix.reshape(1, -1), w_in.astype(bf16), b_in.reshape(1, -1),
      ln_g.reshape(1, -1), ln_b.reshape(1, -1), *tables)


def _attend(qa, qb, kwin, vwin, sink, valid):
    lo = _lane_lo(CHUNK)
    zero = jnp.zeros_like(qa)
    lhs = jnp.concatenate([jnp.where(lo, qa, zero), jnp.where(lo, zero, qa),
                           jnp.where(lo, qb, zero), jnp.where(lo, zero, qb)], axis=0)
    s = _dot_nt(lhs, kwin)
    if valid is not None:
        s = jnp.where(valid, s, NEG_INF)
    s_a, s_b = s[:, :LANES], s[:, LANES:]
    tail = s_b.shape[1]
    m = jnp.maximum(jnp.max(s, axis=-1, keepdims=True), sink)
    p_a = jnp.exp(s_a - m)
    p_b = jnp.exp(s_b - m[:, :tail])
    denom = (jnp.sum(jnp.concatenate([p_a, p_b], axis=1), axis=-1, keepdims=True)
             + jnp.exp(sink - m))
    inv = 1.0 / denom
    pn = jnp.concatenate([p_a * inv, p_b * inv[:, :tail]], axis=1)
    r = _dot(_bf16(pn), vwin)
    oa = jnp.where(lo, r[0:CHUNK], r[CHUNK:2 * CHUNK])
    ob = jnp.where(lo, r[2 * CHUNK:3 * CHUNK], r[3 * CHUNK:4 * CHUNK])
    return oa, ob


def _stack_rows(rows):
    ri = lax.broadcasted_iota(jnp.int32, (8, rows[0].shape[1]), 0)
    out = jnp.zeros((8, rows[0].shape[1]), rows[0].dtype)
    for k, row in enumerate(rows):
        out = jnp.where(ri == k, row, out)
    return out


def _route_pick(logits_t):
    rows = logits_t.shape[1]
    eid = lax.broadcasted_iota(jnp.int32, (N_EXPERTS, rows), 0)
    work = logits_t
    vals, idxs = [], []
    for _ in range(TOP_K):
        m = jnp.max(work, axis=0, keepdims=True)
        idx = jnp.min(jnp.where(work == m, eid, N_EXPERTS), axis=0, keepdims=True)
        vals.append(m)
        idxs.append(idx)
        work = jnp.where(eid == idx, -jnp.inf, work)
    exps = [jnp.exp(v - vals[0]) for v in vals]
    inv = 1.0 / (exps[0] + exps[1] + exps[2] + exps[3])
    gates = _stack_rows([e * inv for e in exps])

    picked = jnp.zeros((N_EXPERTS, rows), jnp.float32)
    for idx in idxs:
        picked = jnp.where(eid == idx, 1.0, picked)
    tr = lax.broadcasted_iota(jnp.int32, (rows, rows), 0)
    tc = lax.broadcasted_iota(jnp.int32, (rows, rows), 1)
    earlier = _bf16(jnp.where(tr < tc, 1.0, 0.0))
    in_tile = _dot(_bf16(picked), earlier)
    count_col = jnp.broadcast_to(jnp.sum(picked, axis=1, keepdims=True), (N_EXPERTS, LANES))
    eid_wide = lax.broadcasted_iota(jnp.int32, (LANES, rows), 0)
    picked_wide = jnp.zeros((LANES, rows), jnp.float32)
    for idx in idxs:
        picked_wide = jnp.where(eid_wide == idx, 1.0, picked_wide)
    count_row = _dot_nt(jnp.ones((8, rows), jnp.bfloat16), _bf16(picked_wide))
    return idxs, gates, in_tile, count_col, count_row


def _route_place(idxs, in_tile, count_col, count_row, carry_s, live):
    rows = in_tile.shape[1]
    eid = lax.broadcasted_iota(jnp.int32, (N_EXPERTS, rows), 0)
    er = lax.broadcasted_iota(jnp.int32, (N_EXPERTS, N_EXPERTS), 0)
    ec = lax.broadcasted_iota(jnp.int32, (N_EXPERTS, N_EXPERTS), 1)
    start_col = _dot(_bf16(jnp.where(ec < er, 1.0, 0.0)), _bf16(count_col))
    local = in_tile + jnp.concatenate([start_col] * (rows // LANES), axis=1)
    slots = _stack_rows([jnp.sum(jnp.where(eid == idx, local, 0.0), axis=0, keepdims=True)
                         for idx in idxs]).astype(jnp.int32)
    lr = lax.broadcasted_iota(jnp.int32, (LANES, LANES), 0)
    lc = lax.broadcasted_iota(jnp.int32, (LANES, LANES), 1)
    start_row = _dot(_bf16(count_row), _bf16(jnp.where(lr < lc, 1.0, 0.0)))
    ri = lax.broadcasted_iota(jnp.int32, (8, LANES), 0)
    meta = jnp.where(ri == 0, carry_s[...], jnp.where(ri == 1, count_row, jnp.where(ri == 2, start_row, 0.0)))
    carry_s[...] = carry_s[...] + count_row * live
    return slots, meta.astype(jnp.int32)


def _mix_kernel(tiles_per_seq, n_prompt_tiles,
                xp_ref, xs_ref, u_ref, vln_ref, q_ref, kd_ref, vd_ref, kdp_ref, vdp_ref,
                ck_ref, cv_ref, ga_ref, gb_ref, wsp_ref, bsp_ref, sink_ref,
                wpa_ref, wpb_ref, wo_ref, gffn_ref, wrh_ref, wrl_ref, br_ref,
                x1_ref, h2_ref, slot_ref, gate_ref, meta_ref, count_ref,
                a_s, o_s, kwin_s, vwin_s, carry_s, hhi_s, hlo_s):
    i = pl.program_id(0)
    n_streams = ROW_TILE // CHUNK

    @pl.when(i == 0)
    def _():
        carry_s[...] = jnp.zeros_like(carry_s)
        hhi_s[...] = jnp.zeros_like(hhi_s)
        hlo_s[...] = jnp.zeros_like(hlo_s)

    def sgu_rows(r0, rows):
        ri = lax.broadcasted_iota(jnp.int32, (rows, rows), 0) // CHUNK
        ci = lax.broadcasted_iota(jnp.int32, (rows, rows), 1) // CHUNK
        for g in range(SGU_GROUPS):
            cols = slice(g * LANES, (g + 1) * LANES)
            w = _bf16(jnp.where(ci <= ri, wsp_ref[g, :rows, :rows], 0.0))
            sp = _dot(w, vln_ref[r0:r0 + rows, cols]) + bsp_ref[g, :rows, :]
            a_s[r0:r0 + rows, cols] = _bf16(u_ref[r0:r0 + rows, cols].astype(jnp.float32) * sp)

    def attend_rows(r0, kwin_of, valid):
        for g in range(N_KV_HEADS):
            c0 = g * Q_PER_KV * HEAD_DIM
            kwin, vwin = kwin_of(g)
            oa, ob = _attend(q_ref[r0:r0 + CHUNK, c0:c0 + LANES],
                             q_ref[r0:r0 + CHUNK, c0 + LANES:c0 + 2 * LANES],
                             kwin, vwin, sink_ref[g], valid)
            o_s[r0:r0 + CHUNK, c0:c0 + LANES] = _bf16(oa)
            o_s[r0:r0 + CHUNK, c0 + LANES:c0 + 2 * LANES] = _bf16(ob)

    @pl.when(i < n_prompt_tiles)
    def _prompt():
        for c in range(ROW_TILE // SGU_CHUNK):
            sgu_rows(c * SGU_CHUNK, SGU_CHUNK)
        kwin_s[0:WINDOW] = kdp_ref[...]
        kwin_s[WINDOW:WINDOW + ROW_TILE] = kd_ref[...]
        vwin_s[0:WINDOW] = vdp_ref[...]
        vwin_s[WINDOW:WINDOW + ROW_TILE] = vd_ref[...]
        first = (i % tiles_per_seq) == 0
        col = lax.broadcasted_iota(jnp.int32, (1, KEY_SPAN), 1)
        for j in range(ROW_TILE // CHUNK):
            r0 = j * CHUNK
            valid = jnp.logical_or(jnp.logical_not(first), col + r0 >= WINDOW) if r0 < WINDOW else None

            def kwin_of(g, r0=r0):
                cols = slice(g * LANES, (g + 1) * LANES)
                return kwin_s[r0:r0 + KEY_SPAN, cols], vwin_s[r0:r0 + KEY_SPAN, cols]

            attend_rows(r0, kwin_of, valid)

    @pl.when(i >= n_prompt_tiles)
    def _sample():
        for s in range(n_streams):
            r0 = s * CHUNK
            sgu_rows(r0, CHUNK)
            kwin_s[0:WINDOW] = _dup_heads(ck_ref[s])
            kwin_s[WINDOW:KEY_SPAN] = kd_ref[r0:r0 + CHUNK]
            vwin_s[0:WINDOW] = _dup_heads(cv_ref[s])
            vwin_s[WINDOW:KEY_SPAN] = vd_ref[r0:r0 + CHUNK]

            def kwin_of(g):
                cols = slice(g * LANES, (g + 1) * LANES)
                return kwin_s[0:KEY_SPAN, cols], vwin_s[0:KEY_SPAN, cols]

            attend_rows(r0, kwin_of, None)

    hh, hl = hhi_s[...], hlo_s[...]
    logits_t = (_dot_nt(wrh_ref[...], hh) + _dot_nt(wrl_ref[...], hh) + _dot_nt(wrh_ref[...], hl)
                + jnp.concatenate([br_ref[...]] * (ROW_TILE // LANES), axis=1))
    m_a = ga_ref[...].astype(jnp.float32) * _dot(a_s[...], wpa_ref[...])
    idxs, gates, in_tile, count_col, count_row = _route_pick(logits_t)
    gate_ref[...] = gates
    m = m_a + gb_ref[...].astype(jnp.float32) * _dot(o_s[...], wpb_ref[...])
    slots, meta = _route_place(idxs, in_tile, count_col, count_row, carry_s, jnp.where(i > 0, 1.0, 0.0))
    slot_ref[...] = slots
    meta_ref[...] = meta
    count_ref[...] = carry_s[...].astype(jnp.int32)

    x = jnp.where(i < n_prompt_tiles, xp_ref[...], xs_ref[...])
    x1 = x + _dot(_bf16(m), wo_ref[...])
    x1_ref[...] = x1
    h2 = _rms(x1, gffn_ref[...])
    _store_row_tiled(h2_ref, (), h2)
    h2_hi = _bf16(h2)
    hhi_s[...] = h2_hi
    hlo_s[...] = _bf16(h2 - h2_hi.astype(jnp.float32))


def _mix(xp, xs, u, vln, q, kd, vd, cache_k, cache_v, ga, gb, w_sp, b_sp, sinks,
         w_pa, w_pb, w_o, g_ffn, w_router, b_router, seq):
    tp, ts = xp.shape[0], xs.shape[0]
    t = tp + ts
    npt = tp // ROW_TILE
    tiles_per_seq = seq // ROW_TILE
    f32, bf16 = jnp.float32, jnp.bfloat16
    n_streams = ROW_TILE // CHUNK
    win_per_tile = ROW_TILE // WINDOW

    nt = t // ROW_TILE
    cur = lambda i: jnp.minimum(i, nt - 1)
    smp = lambda i: jnp.maximum(cur(i) - npt, 0)
    row = lambda width: pl.BlockSpec((ROW_TILE, width), lambda i: (cur(i), 0))
    prev_spec = pl.BlockSpec(
        (WINDOW, KV_DUP_W), lambda i: (jnp.maximum(jnp.minimum(i, npt - 1) * win_per_tile - 1, 0), 0))
    cache_spec = pl.BlockSpec((n_streams, WINDOW, KV_W), lambda i: (smp(i), 0, 0))
    xs_spec = pl.BlockSpec((ROW_TILE, D_MODEL), lambda i: (smp(i), 0))
    sink_cols = jnp.broadcast_to(
        jnp.repeat(sinks.astype(f32).reshape(N_KV_HEADS, Q_PER_KV), CHUNK, axis=1)[:, :, None],
        (N_KV_HEADS, Q_PER_KV * CHUNK, LANES))
    wr_t = w_router.T
    wr_hi = wr_t.astype(bf16)
    wr_lo = (wr_t - wr_hi.astype(f32)).astype(bf16)
    routed8 = lambda width: pl.BlockSpec((8, width), lambda i: (jnp.maximum(i - 1, 0), 0))
    out_shape = (
        jax.ShapeDtypeStruct((t, D_MODEL), f32),
        jax.ShapeDtypeStruct((t * ROW_SUBTILES, LANES), f32),
        jax.ShapeDtypeStruct((nt * 8, ROW_TILE), jnp.int32),
        jax.ShapeDtypeStruct((nt * 8, ROW_TILE), f32),
        jax.ShapeDtypeStruct((nt * 8, LANES), jnp.int32),
        jax.ShapeDtypeStruct((8, LANES), jnp.int32),
    )
    return pl.pallas_call(
        functools.partial(_mix_kernel, tiles_per_seq, npt),
        out_shape=out_shape,
        grid=(nt + 1,),
        in_specs=[
            _prompt_spec(D_MODEL, npt), xs_spec,
            row(SGU_WIDTH), row(SGU_WIDTH), row(ATT_W),
            row(KV_DUP_W), row(KV_DUP_W), prev_spec, prev_spec,
            cache_spec, cache_spec, row(D_MODEL), row(D_MODEL),
            _const_spec((SGU_GROUPS, SGU_CHUNK, SGU_CHUNK)), _const_spec((SGU_GROUPS, SGU_CHUNK, LANES)),
            _const_spec((N_KV_HEADS, Q_PER_KV * CHUNK, LANES)),
            _const_spec((SGU_WIDTH, D_MODEL)), _const_spec((ATT_W, D_MODEL)),
            _const_spec((D_MODEL, D_MODEL)), _const_spec((1, D_MODEL)),
            _const_spec((N_EXPERTS, D_MODEL)), _const_spec((N_EXPERTS, D_MODEL)),
            _const_spec((N_EXPERTS, LANES)),
        ],
        out_specs=(row(D_MODEL),
                   pl.BlockSpec((ROW_TILE * ROW_SUBTILES, LANES), lambda i: (cur(i), 0)),
                   routed8(ROW_TILE), routed8(ROW_TILE), routed8(LANES), _const_spec((8, LANES))),
        scratch_shapes=[
            pltpu.VMEM((ROW_TILE, SGU_WIDTH), bf16), pltpu.VMEM((ROW_TILE, ATT_W), bf16),
            pltpu.VMEM((WINDOW + ROW_TILE, KV_DUP_W), bf16),
            pltpu.VMEM((WINDOW + ROW_TILE, KV_DUP_W), bf16),
            pltpu.VMEM((8, LANES), f32),
            pltpu.VMEM((ROW_TILE, D_MODEL), bf16), pltpu.VMEM((ROW_TILE, D_MODEL), bf16),
        ],
        compiler_params=_params(),
        name="mix",
    )(xp, xs, u, vln, q, kd, vd, kd, vd,
      cache_k.reshape(-1, WINDOW, KV_W), cache_v.reshape(-1, WINDOW, KV_W), ga, gb,
      w_sp, jnp.broadcast_to(b_sp[:, :, None], (SGU_GROUPS, SGU_CHUNK, LANES)), sink_cols,
      w_pa.astype(bf16), w_pb.astype(bf16), w_o.astype(bf16),
      g_ffn.reshape(1, -1), wr_hi, wr_lo,
      jnp.broadcast_to(b_router.astype(f32)[:, None], (N_EXPERTS, LANES)))


def _unrolled_rows(n_rows, fn):
    if isinstance(n_rows, int):
        groups, tail_start = n_rows // ROW_UNROLL, n_rows - n_rows % ROW_UNROLL
    else:
        groups = lax.shift_right_logical(n_rows, ROW_UNROLL.bit_length() - 1)
        tail_start = groups * ROW_UNROLL

    def group(gi, carry):
        for lane in range(ROW_UNROLL):
            fn(gi * ROW_UNROLL + lane, lane)
        return carry

    def tail(r, carry):
        fn(r, 0)
        return carry

    lax.fori_loop(0, groups, group, 0)
    lax.fori_loop(tail_start, n_rows, tail, 0)


def _row_span(first_row, n_rows):
    return pl.ds(pl.multiple_of(first_row * ROW_SUBTILES, ROW_SUBTILES),
                 pl.multiple_of(n_rows * ROW_SUBTILES, ROW_SUBTILES))


def _dispatch_kernel(pad_dst_ref, pad_n_ref, nu_ref,
                     src_ref, n_ref, dst_ref, slot_ref, h2_ref,
                     xs_hbm,
                     local, zeros_s, run_sem, pad_sem):
    i = pl.program_id(0)
    last = pl.num_programs(0) - 1
    buf = i % 2
    n_row_tiles = xs_hbm.shape[0] // zeros_s.shape[0]

    def pad_copy(e):
        n = pad_n_ref[e]
        return n, pltpu.make_async_copy(zeros_s.at[_row_span(0, n)], xs_hbm.at[_row_span(pad_dst_ref[e], n)],
                                        pad_sem)

    def unused_tiles(act):
        def body(j, carry):
            rows = zeros_s.shape[0]
            act(pltpu.make_async_copy(
                zeros_s, xs_hbm.at[pl.ds(pl.multiple_of(j * rows, rows), rows)], pad_sem))
            return carry
        lax.fori_loop(nu_ref[0], n_row_tiles, body, 0)

    def run_copy(e):
        n = n_ref[0, 0, e]
        return n, pltpu.make_async_copy(local.at[buf, _row_span(src_ref[0, 0, e], n)],
                                        xs_hbm.at[_row_span(dst_ref[0, 0, e], n)], run_sem.at[buf])

    def wait_runs(b):
        pltpu.make_async_copy(local.at[b], xs_hbm.at[pl.ds(0, local.shape[1])], run_sem.at[b]).wait()

    @pl.when(i == 0)
    def _():
        zeros_s[...] = jnp.zeros_like(zeros_s)
        for e in range(N_EXPERTS):
            n, copy = pad_copy(e)
            pl.when(n > 0)(copy.start)
        unused_tiles(lambda copy: copy.start())

    @pl.when(i >= 2)
    def _():
        wait_runs(buf)

    def place(t, lane):
        row = h2_ref[pl.ds(pl.multiple_of(t * ROW_SUBTILES, ROW_SUBTILES), ROW_SUBTILES), :]
        for k in range(TOP_K):
            at = pl.multiple_of(slot_ref[0, 0, k * ROW_TILE + t] * ROW_SUBTILES, ROW_SUBTILES)
            local[buf, pl.ds(at, ROW_SUBTILES), :] = row
    _unrolled_rows(ROW_TILE, place)

    for e in range(N_EXPERTS):
        n, copy = run_copy(e)
        pl.when(n > 0)(copy.start)

    @pl.when(i == last)
    def _():
        wait_runs(buf)

        @pl.when(last >= 1)
        def _():
            wait_runs(1 - buf)

        for e in range(N_EXPERTS):
            n, copy = pad_copy(e)
            pl.when(n > 0)(copy.wait)
        unused_tiles(lambda copy: copy.wait())


def _run_spec(index_of):
    return pl.BlockSpec((1, 1, LANES), lambda i, *_: (index_of(i), 0, 0), memory_space=pltpu.SMEM)


def _dispatch(pad_dst, pad_n, n_used, run_src, run_n, run_dst, slots, h2, n_slots):
    nt = run_n.shape[0]
    picks = ROW_TILE * TOP_K
    grid_spec = pltpu.PrefetchScalarGridSpec(
        num_scalar_prefetch=3,
        grid=(nt,),
        in_specs=[
            _run_spec(lambda i: i), _run_spec(lambda i: i), _run_spec(lambda i: i),
            pl.BlockSpec((1, 1, picks), lambda i, *_: (i, 0, 0), memory_space=pltpu.SMEM),
            pl.BlockSpec((ROW_TILE * ROW_SUBTILES, LANES), lambda i, *_: (i, 0)),
        ],
        out_specs=pl.BlockSpec(memory_space=pl.ANY),
        scratch_shapes=[
            pltpu.VMEM((2, picks * ROW_SUBTILES, LANES), jnp.float32),
            pltpu.VMEM((MOE_TILE * ROW_SUBTILES, LANES), jnp.float32),
            pltpu.SemaphoreType.DMA((2,)), pltpu.SemaphoreType.DMA,
        ],
    )
    return pl.pallas_call(
        _dispatch_kernel,
        out_shape=jax.ShapeDtypeStruct((n_slots * ROW_SUBTILES, LANES), jnp.float32),
        grid_spec=grid_spec,
        compiler_params=_params(),
        name="dispatch",
    )(pad_dst, pad_n, n_used, run_src, run_n, run_dst, slots, h2)


def _expert_kernel(te_ref, nu_ref, nx_ref, par_ref,
                   xs_ref, bgu_ref, bdn_ref, wgu_hbm, wdn_hbm,
                   ys_ref,
                   wgu_f, wdn_f, wgu_s, wdn_s, wsem):
    i = pl.program_id(0)
    n_used = nu_ref[0]
    expert = te_ref[i]
    buf = par_ref[i]
    expert_changed = jnp.logical_or(i == 0, expert != te_ref[jnp.maximum(i - 1, 0)])

    def weight_copies(e, b):
        return (pltpu.make_async_copy(wgu_hbm.at[e], wgu_f.at[b], wsem.at[0, b]),
                pltpu.make_async_copy(wdn_hbm.at[e], wdn_f.at[b], wsem.at[1, b]))

    @pl.when(jnp.logical_and(i < n_used, expert_changed))
    def _():
        @pl.when(i == 0)
        def _():
            for copy in weight_copies(expert, buf):
                copy.start()

        for copy in weight_copies(expert, buf):
            copy.wait()
        following = nx_ref[i]

        @pl.when(following != expert)
        def _():
            for copy in weight_copies(following, 1 - buf):
                copy.start()

        wgu_s[...] = _bf16(wgu_f[buf])
        wdn_s[...] = _bf16(wdn_f[buf])

    @pl.when(i < n_used)
    def _():
        x = _bf16(_load_row_tiled(xs_ref, (), MOE_TILE))
        gu = _dot(x, wgu_s[...]) + bgu_ref[0]
        gate = jnp.minimum(gu[:, :D_FF], SWIGLU_LIMIT)
        lin = jnp.clip(gu[:, D_FF:], -SWIGLU_LIMIT, SWIGLU_LIMIT)
        act = gate * _sigmoid(SWIGLU_ALPHA * gate) * (lin + 1.0)
        _store_row_tiled(ys_ref, (), _dot(_bf16(act), wdn_s[...]) + bdn_ref[0])

    @pl.when(i >= n_used)
    def _():
        ys_ref[...] = jnp.zeros_like(ys_ref)


def _experts(tile_expert, n_used, next_expert, weight_buf, xs, w_gu, b_gu, w_dn, b_dn):
    n_tiles = tile_expert.shape[0]
    f32, bf16 = jnp.float32, jnp.bfloat16
    tile_rows = MOE_TILE * ROW_SUBTILES
    grid_spec = pltpu.PrefetchScalarGridSpec(
        num_scalar_prefetch=4,
        grid=(n_tiles,),
        in_specs=[
            pl.BlockSpec((tile_rows, LANES), lambda i, te, nu, nx, par: (jnp.minimum(i, nu[0] - 1), 0)),
            pl.BlockSpec((1, 1, 2 * D_FF), lambda i, te, nu, nx, par: (te[i], 0, 0)),
            pl.BlockSpec((1, 1, D_MODEL), lambda i, te, nu, nx, par: (te[i], 0, 0)),
            pl.BlockSpec(memory_space=pl.ANY), pl.BlockSpec(memory_space=pl.ANY),
        ],
        out_specs=pl.BlockSpec((tile_rows, LANES), lambda i, te, nu, nx, par: (i, 0)),
        scratch_shapes=[
            pltpu.VMEM((2, D_MODEL, 2 * D_FF), f32), pltpu.VMEM((2, D_FF, D_MODEL), f32),
            pltpu.VMEM((D_MODEL, 2 * D_FF), bf16), pltpu.VMEM((D_FF, D_MODEL), bf16),
            pltpu.SemaphoreType.DMA((2, 2)),
        ],
    )
    return pl.pallas_call(
        _expert_kernel,
        out_shape=jax.ShapeDtypeStruct(xs.shape, f32),
        grid_spec=grid_spec,
        compiler_params=_params(),
        name="experts",
    )(tile_expert, n_used, next_expert, weight_buf, xs, b_gu.reshape(N_EXPERTS, 1, -1),
      b_dn.reshape(N_EXPERTS, 1, -1), w_gu, w_dn)


def _combine_kernel(n_prompt_tiles,
                    src_ref, n_ref, dst_ref, src_nx_ref, n_nx_ref, dst_nx_ref, slot_ref, gate_ref,
                    x1_ref, gfin_ref, ys_hbm,
                    yp_ref, yo_ref,
                    local, mixed, run_sem):
    i = pl.program_id(0)
    last = pl.num_programs(0) - 1
    buf = i % 2

    def fetch_runs(s_ref, c_ref, d_ref, b):
        for e in range(N_EXPERTS):
            n = c_ref[0, 0, e]
            copy = pltpu.make_async_copy(ys_hbm.at[_row_span(d_ref[0, 0, e], n)],
                                         local.at[b, _row_span(s_ref[0, 0, e], n)], run_sem.at[b])
            pl.when(n > 0)(copy.start)

    @pl.when(i == 0)
    def _():
        fetch_runs(src_ref, n_ref, dst_ref, 0)

    @pl.when(i < last)
    def _():
        fetch_runs(src_nx_ref, n_nx_ref, dst_nx_ref, 1 - buf)

    pltpu.make_async_copy(ys_hbm.at[pl.ds(0, local.shape[1])], local.at[buf], run_sem.at[buf]).wait()

    def blend(t, lane):
        acc = None
        for k in range(TOP_K):
            at = pl.multiple_of(slot_ref[0, 0, k * ROW_TILE + t] * ROW_SUBTILES, ROW_SUBTILES)
            term = gate_ref[0, 0, k * ROW_TILE + t] * local[buf, pl.ds(at, ROW_SUBTILES), :]
            acc = term if acc is None else acc + term
        mixed[pl.ds(pl.multiple_of(t * ROW_SUBTILES, ROW_SUBTILES), ROW_SUBTILES), :] = acc
    _unrolled_rows(ROW_TILE, blend)

    out = _rms(x1_ref[...] + _load_row_tiled(mixed, (), ROW_TILE), gfin_ref[...])

    @pl.when(i < n_prompt_tiles)
    def _():
        yp_ref[...] = out

    @pl.when(i >= n_prompt_tiles)
    def _():
        yo_ref[...] = out


def _combine(run_src, run_n, run_dst, slots, gates, x1, ys, g_final, tp):
    t = x1.shape[0]
    npt = tp // ROW_TILE
    nt = t // ROW_TILE
    f32 = jnp.float32
    picks = ROW_TILE * TOP_K
    nxt = lambda i: jnp.minimum(i + 1, nt - 1)
    pick_spec = pl.BlockSpec((1, 1, picks), lambda i: (i, 0, 0), memory_space=pltpu.SMEM)
    return pl.pallas_call(
        functools.partial(_combine_kernel, npt),
        out_shape=(jax.ShapeDtypeStruct((tp, D_MODEL), f32),
                   jax.ShapeDtypeStruct((t - tp, D_MODEL), f32)),
        grid=(nt,),
        in_specs=[_run_spec(lambda i: i), _run_spec(lambda i: i), _run_spec(lambda i: i),
                  _run_spec(nxt), _run_spec(nxt), _run_spec(nxt), pick_spec, pick_spec,
                  _row_spec(D_MODEL), _const_spec((1, D_MODEL)), pl.BlockSpec(memory_space=pl.ANY)],
        out_specs=(_prompt_spec(D_MODEL, npt), _sample_spec(D_MODEL, npt)),
        scratch_shapes=[pltpu.VMEM((2, picks * ROW_SUBTILES, LANES), f32),
                        pltpu.VMEM((ROW_TILE * ROW_SUBTILES, LANES), f32),
                        pltpu.SemaphoreType.DMA((2,))],
        compiler_params=_params(),
        name="combine",
    )(run_src, run_n, run_dst, run_src, run_n, run_dst, slots, gates, x1, g_final.reshape(1, -1), ys)


def _plan(meta, counts, t):
    nt = t // ROW_TILE
    n_tiles = (t * TOP_K + N_EXPERTS * (MOE_TILE - 1)) // MOE_TILE
    counts = counts[0, :N_EXPERTS]
    tiles_e = (counts + MOE_TILE - 1) // MOE_TILE
    tile_end = jnp.cumsum(tiles_e)
    tile_start = tile_end - tiles_e
    n_used = tile_end[-1]
    first_row = jnp.pad(tile_start * MOE_TILE, (0, LANES - N_EXPERTS))
    meta = meta.reshape(nt, 8, LANES)
    run_dst = meta[:, 0:1, :] + first_row[None, None, :]
    run_n = meta[:, 1:2, :]
    run_src = meta[:, 2:3, :]
    pad_dst = tile_start * MOE_TILE + counts
    pad_n = tiles_e * MOE_TILE - counts
    tile_ids = jnp.arange(n_tiles, dtype=jnp.int32)
    live = jnp.minimum(tile_ids, n_used - 1)
    tile_expert = jnp.sum(tile_end[None, :] <= live[:, None], axis=1).astype(jnp.int32)
    used = tiles_e > 0
    ids = jnp.arange(N_EXPERTS, dtype=jnp.int32)
    later_used = jnp.where(jnp.logical_and(used[None, :], ids[None, :] > ids[:, None]), ids[None, :],
                           N_EXPERTS)
    following = jnp.min(later_used, axis=1)
    following = jnp.where(following < N_EXPERTS, following, ids)
    buf_of = (jnp.cumsum(used.astype(jnp.int32)) - 1) % 2
    return (tile_expert, n_used.reshape(1).astype(jnp.int32), following[tile_expert].astype(jnp.int32),
            buf_of[tile_expert].astype(jnp.int32), run_src, run_n, run_dst,
            pad_dst.astype(jnp.int32), pad_n.astype(jnp.int32), n_tiles * MOE_TILE)


def kernel(x_prompt, x_sample, cache_k, cache_v, g_mix, w_in, b_in, ln_v_g, ln_v_b, w_sp, b_sp,
           attn_sinks, w_pa, w_pb, w_o, g_ffn, w_router, b_router, w_gu, b_gu, w_dn, b_dn, g_final):
    nb, seq, d = x_prompt.shape
    nsb, nnew, _ = x_sample.shape
    tp, ts = nb * seq, nsb * nnew
    t = tp + ts
    xp = x_prompt.reshape(tp, d)
    xs = x_sample.reshape(ts, d)
    pos = np.concatenate([np.tile(np.arange(seq), nb), np.tile(PAST_LEN + np.arange(nnew), nsb)])
    tables = _rope_tables(pos)
    u, vln, v_sgu, q, k, v, kd, vd, ga, gb = _project(
        xp, xs, g_mix[0], w_in[0], b_in[0], ln_v_g[0], ln_v_b[0], tables)
    x1, h2, slot_t, gate_t, meta, counts = _mix(
        xp, xs, u, vln, q, kd, vd, cache_k[0], cache_v[0], ga, gb, w_sp[0], b_sp[0], attn_sinks[0],
        w_pa[0], w_pb[0], w_o[0], g_ffn[0], w_router[0], b_router[0], seq)
    nt = t // ROW_TILE
    picks = lambda a: a.reshape(nt, 8, ROW_TILE)[:, :TOP_K, :].reshape(nt, 1, TOP_K * ROW_TILE)
    slots, gates = picks(slot_t), picks(gate_t)
    (tile_expert, n_used, next_expert, weight_buf, run_src, run_n, run_dst, pad_dst, pad_n,
     n_slots) = _plan(meta, counts, t)
    x_sorted = _dispatch(pad_dst, pad_n, n_used, run_src, run_n, run_dst, slots, h2, n_slots)
    y_sorted = _experts(tile_expert, n_used, next_expert, weight_buf, x_sorted, w_gu[0], b_gu[0],
                        w_dn[0], b_dn[0])
    y_p, y_s = _combine(run_src, run_n, run_dst, slots, gates, x1, y_sorted, g_final, tp)

    keep = min(WINDOW, seq)
    tails = lambda a: jnp.stack([a[(b + 1) * seq - keep:(b + 1) * seq] for b in range(nb)]).reshape(
        nb, keep, N_KV_HEADS, HEAD_DIM)
    kp, vp = tails(k), tails(v)
    ks = k[tp:].reshape(nsb, nnew, N_KV_HEADS, HEAD_DIM)
    vs = v[tp:].reshape(nsb, nnew, N_KV_HEADS, HEAD_DIM)
    return (y_p.reshape(nb, seq, d), y_s.reshape(nsb, nnew, d), kp[None], vp[None], ks[None],
            vs[None], v_sgu.reshape(1, nsb, nnew, SGU_WIDTH))
```

```python
import functools

import numpy as np
import jax
import jax.numpy as jnp
from jax import lax
from jax.experimental import pallas as pl
from jax.experimental.pallas import tpu as pltpu

D_MODEL = 1024
PAST_LEN = 2048
CHUNK = 64
SGU_CHUNK = 128
SGU_GROUPS = 8
SGU_WIDTH = 1024
N_HEADS = 16
N_KV_HEADS = 4
HEAD_DIM = 64
Q_PER_KV = N_HEADS // N_KV_HEADS
WINDOW = 128
ROT_DIM = HEAD_DIM // 4
ROPE_THETA = 500000.0
ATT_W = N_HEADS * HEAD_DIM
KV_W = N_KV_HEADS * HEAD_DIM
N_EXPERTS = 32
TOP_K = 4
D_FF = 1024
SWIGLU_ALPHA = 1.702
SWIGLU_LIMIT = 7.0
NORM_EPS = 1e-5
NEG_INF = -1e30
N_IN = SGU_WIDTH * 2 + ATT_W + KV_W * 2 + D_MODEL * 2

LANES = 128
ROW_TILE = 256
MOE_TILE = 512
ROW_UNROLL = 8
COL_BLOCK = 256
KV_DUP_W = N_KV_HEADS * LANES
KEY_SPAN = WINDOW + CHUNK
VMEM_LIMIT = 56 * 1024 * 1024

_SQRT_HALF = 0.7071067811865476


def _gelu(x):
    z = jnp.abs(x) * _SQRT_HALF
    t = 1.0 / (1.0 + 0.3275911 * z)
    poly = t * (0.254829592 + t * (-0.284496736 + t * (1.421413741
                + t * (-1.453152027 + t * 1.061405429))))
    half_tail = (0.5 * x) * (poly * jnp.exp(-z * z))
    return jnp.where(x >= 0.0, x - half_tail, half_tail)


def _sigmoid(x):
    return 1.0 / (1.0 + jnp.exp(-x))


def _bf16(x):
    return x.astype(jnp.bfloat16)


def _dot(a, b):
    return jnp.dot(a, b, preferred_element_type=jnp.float32)


ROW_SUBTILES = D_MODEL // LANES


def _store_row_tiled(ref, lead, x):
    rows = x.shape[0]
    for s in range(ROW_SUBTILES):
        ref[(*lead, pl.ds(s, rows, stride=ROW_SUBTILES), slice(None))] = x[:, s * LANES:(s + 1) * LANES]


def _load_row_tiled(ref, lead, rows):
    return jnp.concatenate(
        [ref[(*lead, pl.ds(s, rows, stride=ROW_SUBTILES), slice(None))] for s in range(ROW_SUBTILES)],
        axis=1)


def _dot_nt(a, b):
    return lax.dot_general(a, b, (((1,), (1,)), ((), ())), preferred_element_type=jnp.float32)


def _rms(x, g):
    return x * lax.rsqrt(jnp.mean(x * x, axis=-1, keepdims=True) + NORM_EPS) * g


def _lane_lo(rows):
    return lax.broadcasted_iota(jnp.int32, (rows, LANES), 1) < HEAD_DIM


def _dup_heads(kv):
    rows = kv.shape[0]
    lo = _lane_lo(rows)
    out = []
    for j in range(KV_W // LANES):
        blk = kv[:, j * LANES:(j + 1) * LANES]
        swp = pltpu.roll(blk, HEAD_DIM, axis=1)
        out.append(jnp.where(lo, blk, swp))
        out.append(jnp.where(lo, swp, blk))
    return _bf16(jnp.concatenate(out, axis=1))


def _rope_block(zb, cos_b, sin_lo, sin_hi):
    up = pltpu.roll(zb, LANES - ROT_DIM // 2, axis=1)
    dn = pltpu.roll(zb, ROT_DIM // 2, axis=1)
    return zb * cos_b + up * sin_lo + dn * sin_hi


def _proj_kernel(n_prompt_tiles, xp_ref, xs_ref, gmix_ref, w_ref, b_ref, lng_ref, lnb_ref,
                 cos_ref, slo_ref, shi_ref,
                 u_ref, vln_ref, vs_ref, q_ref, k_ref, v_ref, kd_ref, vd_ref, ga_ref, gb_ref):
    i = pl.program_id(0)
    x = jnp.where(i < n_prompt_tiles, xp_ref[...], xs_ref[...])
    h = _bf16(_rms(x, gmix_ref[...]))

    cos_b, sin_lo, sin_hi = cos_ref[...], slo_ref[...], shi_ref[...]
    off_u, off_v, off_q, off_k, off_vv, off_ga, off_gb = (
        int(o) for o in np.cumsum((0, SGU_WIDTH, SGU_WIDTH, ATT_W, KV_W, KV_W, D_MODEL)))

    def z(lo):
        return _dot(h, w_ref[:, lo:lo + COL_BLOCK]) + b_ref[:, lo:lo + COL_BLOCK]

    def cols(j):
        return slice(j * COL_BLOCK, (j + 1) * COL_BLOCK)

    def rope(zb):
        return jnp.concatenate(
            [_rope_block(zb[:, t * LANES:(t + 1) * LANES], cos_b, sin_lo, sin_hi)
             for t in range(COL_BLOCK // LANES)], axis=1)

    gelu_v = []

    def do_u(j):
        u_ref[:, cols(j)] = _bf16(_gelu(z(off_u + j * COL_BLOCK)))

    def do_v(j):
        gelu_v.append(_gelu(z(off_v + j * COL_BLOCK)))

    def do_q(j):
        q_ref[:, cols(j)] = _bf16(rope(z(off_q + j * COL_BLOCK)) * (HEAD_DIM ** -0.5))

    def do_k(j):
        kr = rope(z(off_k))
        k_ref[...] = kr
        kd_ref[...] = _dup_heads(kr)

    def do_vv(j):
        zv = z(off_vv)
        v_ref[...] = zv
        vd_ref[...] = _dup_heads(zv)

    def do_ga(j):
        ga_ref[:, cols(j)] = _bf16(_sigmoid(z(off_ga + j * COL_BLOCK)))

    def do_gb(j):
        gb_ref[:, cols(j)] = _bf16(_sigmoid(z(off_gb + j * COL_BLOCK)))

    order = ((do_v, 0), (do_q, 0), (do_v, 1), (do_q, 1), (do_v, 2), (do_q, 2), (do_v, 3), (do_q, 3),
             (do_u, 0), (do_k, 0), (do_u, 1), (do_vv, 0), (do_u, 2), (do_ga, 0), (do_u, 3), (do_ga, 1),
             (do_ga, 2), (do_ga, 3), (do_gb, 0), (do_gb, 1), (do_gb, 2), (do_gb, 3))
    for fn, j in order:
        fn(j)
    gv = jnp.concatenate(gelu_v, axis=1)
    gc = gv - jnp.mean(gv, axis=-1, keepdims=True)
    var = jnp.mean(gc * gc, axis=-1, keepdims=True)
    vln = gc * lax.rsqrt(var + NORM_EPS) * lng_ref[...] + lnb_ref[...]
    vln_ref[...] = _bf16(vln)
    vs_ref[...] = vln


def _rope_tables(pos):
    half = ROT_DIM // 2
    inv = np.float32(ROPE_THETA) ** (-np.arange(half, dtype=np.float32) * np.float32(2.0) / ROT_DIM)
    ang = pos.astype(np.float32)[:, None] * inv.astype(np.float32)[None, :]
    cos = jnp.asarray(np.cos(ang.astype(np.float64)).astype(np.float32))
    sin = jnp.asarray(np.sin(ang.astype(np.float64)).astype(np.float32))
    n = pos.shape[0]
    ones = jnp.ones((n, HEAD_DIM - ROT_DIM), jnp.float32)
    zeros = jnp.zeros((n, HEAD_DIM - ROT_DIM), jnp.float32)
    zh = jnp.zeros((n, half), jnp.float32)
    cos_h = jnp.concatenate([cos, cos, ones], axis=1)
    slo_h = jnp.concatenate([-sin, zh, zeros], axis=1)
    shi_h = jnp.concatenate([zh, sin, zeros], axis=1)
    rep = LANES // HEAD_DIM
    return (jnp.tile(cos_h, (1, rep)), jnp.tile(slo_h, (1, rep)), jnp.tile(shi_h, (1, rep)))


def _row_spec(width):
    return pl.BlockSpec((ROW_TILE, width), lambda i: (i, 0))


def _const_spec(shape):
    return pl.BlockSpec(shape, lambda i: (0,) * len(shape))


def _prompt_spec(width, n_prompt_tiles):
    return pl.BlockSpec((ROW_TILE, width), lambda i: (jnp.minimum(i, n_prompt_tiles - 1), 0))


def _sample_spec(width, n_prompt_tiles):
    return pl.BlockSpec((ROW_TILE, width), lambda i: (jnp.maximum(i - n_prompt_tiles, 0), 0))


def _params():
    return pltpu.CompilerParams(dimension_semantics=("arbitrary",), vmem_limit_bytes=VMEM_LIMIT)


def _project(xp, xs, g_mix, w_in, b_in, ln_g, ln_b, tables):
    tp, ts = xp.shape[0], xs.shape[0]
    t = tp + ts
    npt = tp // ROW_TILE
    f32, bf16 = jnp.float32, jnp.bfloat16
    out_shape = (
        jax.ShapeDtypeStruct((t, SGU_WIDTH), bf16),
        jax.ShapeDtypeStruct((t, SGU_WIDTH), bf16),
        jax.ShapeDtypeStruct((ts, SGU_WIDTH), f32),
        jax.ShapeDtypeStruct((t, ATT_W), bf16),
        jax.ShapeDtypeStruct((t, KV_W), f32),
        jax.ShapeDtypeStruct((t, KV_W), f32),
        jax.ShapeDtypeStruct((t, KV_DUP_W), bf16),
        jax.ShapeDtypeStruct((t, KV_DUP_W), bf16),
        jax.ShapeDtypeStruct((t, D_MODEL), bf16),
        jax.ShapeDtypeStruct((t, D_MODEL), bf16),
    )
    return pl.pallas_call(
        functools.partial(_proj_kernel, npt),
        out_shape=out_shape,
        grid=(t // ROW_TILE,),
        in_specs=[
            _prompt_spec(D_MODEL, npt), _sample_spec(D_MODEL, npt),
            _const_spec((1, D_MODEL)), _const_spec((D_MODEL, N_IN)),
            _const_spec((1, N_IN)), _const_spec((1, SGU_WIDTH)), _const_spec((1, SGU_WIDTH)),
            _row_spec(LANES), _row_spec(LANES), _row_spec(LANES),
        ],
        out_specs=(
            _row_spec(SGU_WIDTH), _row_spec(SGU_WIDTH), _sample_spec(SGU_WIDTH, npt),
            _row_spec(ATT_W), _row_spec(KV_W), _row_spec(KV_W), _row_spec(KV_DUP_W),
            _row_spec(KV_DUP_W), _row_spec(D_MODEL), _row_spec(D_MODEL),
        ),
        compiler_params=_params(),
        name="proj",
    )(xp, xs, g_mix.reshape(1, -1), w_in.astype(bf16), b_in.reshape(1, -1),
      ln_g.reshape(1, -1), ln_b.reshape(1, -1), *tables)


def _attend(qa, qb, kwin, vwin, sink, valid):
    lo = _lane_lo(CHUNK)
    zero = jnp.zeros_like(qa)
    lhs = jnp.concatenate([jnp.where(lo, qa, zero), jnp.where(lo, zero, qa),
                           jnp.where(lo, qb, zero), jnp.where(lo, zero, qb)], axis=0)
    s = _dot_nt(lhs, kwin)
    if valid is not None:
        s = jnp.where(valid, s, NEG_INF)
    s_a, s_b = s[:, :LANES], s[:, LANES:]
    tail = s_b.shape[1]
    m = jnp.maximum(jnp.max(s, axis=-1, keepdims=True), sink)
    p_a = jnp.exp(s_a - m)
    p_b = jnp.exp(s_b - m[:, :tail])
    denom = (jnp.sum(jnp.concatenate([p_a, p_b], axis=1), axis=-1, keepdims=True)
             + jnp.exp(sink - m))
    inv = 1.0 / denom
    pn = jnp.concatenate([p_a * inv, p_b * inv[:, :tail]], axis=1)
    r = _dot(_bf16(pn), vwin)
    oa = jnp.where(lo, r[0:CHUNK], r[CHUNK:2 * CHUNK])
    ob = jnp.where(lo, r[2 * CHUNK:3 * CHUNK], r[3 * CHUNK:4 * CHUNK])
    return oa, ob


def _stack_rows(rows):
    ri = lax.broadcasted_iota(jnp.int32, (8, rows[0].shape[1]), 0)
    out = jnp.zeros((8, rows[0].shape[1]), rows[0].dtype)
    for k, row in enumerate(rows):
        out = jnp.where(ri == k, row, out)
    return out


def _route_pick(logits_t):
    rows = logits_t.shape[1]
    eid = lax.broadcasted_iota(jnp.int32, (N_EXPERTS, rows), 0)
    work = logits_t
    vals, idxs = [], []
    for _ in range(TOP_K):
        m = jnp.max(work, axis=0, keepdims=True)
        idx = jnp.min(jnp.where(work == m, eid, N_EXPERTS), axis=0, keepdims=True)
        vals.append(m)
        idxs.append(idx)
        work = jnp.where(eid == idx, -jnp.inf, work)
    exps = [jnp.exp(v - vals[0]) for v in vals]
    inv = 1.0 / (exps[0] + exps[1] + exps[2] + exps[3])
    gates = _stack_rows([e * inv for e in exps])

    picked = jnp.zeros((N_EXPERTS, rows), jnp.float32)
    for idx in idxs:
        picked = jnp.where(eid == idx, 1.0, picked)
    tr = lax.broadcasted_iota(jnp.int32, (rows, rows), 0)
    tc = lax.broadcasted_iota(jnp.int32, (rows, rows), 1)
    earlier = _bf16(jnp.where(tr < tc, 1.0, 0.0))
    in_tile = _dot(_bf16(picked), earlier)
    count_col = jnp.broadcast_to(jnp.sum(picked, axis=1, keepdims=True), (N_EXPERTS, LANES))
    eid_wide = lax.broadcasted_iota(jnp.int32, (LANES, rows), 0)
    picked_wide = jnp.zeros((LANES, rows), jnp.float32)
    for idx in idxs:
        picked_wide = jnp.where(eid_wide == idx, 1.0, picked_wide)
    count_row = _dot_nt(jnp.ones((8, rows), jnp.bfloat16), _bf16(picked_wide))
    return idxs, gates, in_tile, count_col, count_row


def _route_place(idxs, in_tile, count_col, count_row, carry_s, live):
    rows = in_tile.shape[1]
    eid = lax.broadcasted_iota(jnp.int32, (N_EXPERTS, rows), 0)
    er = lax.broadcasted_iota(jnp.int32, (N_EXPERTS, N_EXPERTS), 0)
    ec = lax.broadcasted_iota(jnp.int32, (N_EXPERTS, N_EXPERTS), 1)
    start_col = _dot(_bf16(jnp.where(ec < er, 1.0, 0.0)), _bf16(count_col))
    local = in_tile + jnp.concatenate([start_col] * (rows // LANES), axis=1)
    slots = _stack_rows([jnp.sum(jnp.where(eid == idx, local, 0.0), axis=0, keepdims=True)
                         for idx in idxs]).astype(jnp.int32)
    lr = lax.broadcasted_iota(jnp.int32, (LANES, LANES), 0)
    lc = lax.broadcasted_iota(jnp.int32, (LANES, LANES), 1)
    start_row = _dot(_bf16(count_row), _bf16(jnp.where(lr < lc, 1.0, 0.0)))
    ri = lax.broadcasted_iota(jnp.int32, (8, LANES), 0)
    meta = jnp.where(ri == 0, carry_s[...], jnp.where(ri == 1, count_row, jnp.where(ri == 2, start_row, 0.0)))
    carry_s[...] = carry_s[...] + count_row * live
    return slots, meta.astype(jnp.int32)


def _mix_kernel(tiles_per_seq, n_prompt_tiles,
                xp_ref, xs_ref, u_ref, vln_ref, q_ref, kd_ref, vd_ref, kdp_ref, vdp_ref,
                ck_ref, cv_ref, ga_ref, gb_ref, wsp_ref, bsp_ref, sink_ref,
                wpa_ref, wpb_ref, wo_ref, gffn_ref, wrh_ref, wrl_ref, br_ref,
                x1_ref, hloc_ref, slot_ref, gate_ref, meta_ref, count_ref,
                a_s, o_s, kwin_s, vwin_s, carry_s, hhi_s, hlo_s):
    i = pl.program_id(0)
    n_streams = ROW_TILE // CHUNK

    @pl.when(i == 0)
    def _():
        carry_s[...] = jnp.zeros_like(carry_s)
        hhi_s[...] = jnp.zeros_like(hhi_s)
        hlo_s[...] = jnp.zeros_like(hlo_s)

    def sgu_rows(r0, rows):
        ri = lax.broadcasted_iota(jnp.int32, (rows, rows), 0) // CHUNK
        ci = lax.broadcasted_iota(jnp.int32, (rows, rows), 1) // CHUNK
        for g in range(SGU_GROUPS):
            cols = slice(g * LANES, (g + 1) * LANES)
            w = _bf16(jnp.where(ci <= ri, wsp_ref[g, :rows, :rows], 0.0))
            sp = _dot(w, vln_ref[r0:r0 + rows, cols]) + bsp_ref[g, :rows, :]
            a_s[r0:r0 + rows, cols] = _bf16(u_ref[r0:r0 + rows, cols].astype(jnp.float32) * sp)

    def attend_rows(r0, kwin_of, valid):
        for g in range(N_KV_HEADS):
            c0 = g * Q_PER_KV * HEAD_DIM
            kwin, vwin = kwin_of(g)
            oa, ob = _attend(q_ref[r0:r0 + CHUNK, c0:c0 + LANES],
                             q_ref[r0:r0 + CHUNK, c0 + LANES:c0 + 2 * LANES],
                             kwin, vwin, sink_ref[g], valid)
            o_s[r0:r0 + CHUNK, c0:c0 + LANES] = _bf16(oa)
            o_s[r0:r0 + CHUNK, c0 + LANES:c0 + 2 * LANES] = _bf16(ob)

    @pl.when(i < n_prompt_tiles)
    def _prompt():
        for c in range(ROW_TILE // SGU_CHUNK):
            sgu_rows(c * SGU_CHUNK, SGU_CHUNK)
        kwin_s[0:WINDOW] = kdp_ref[...]
        kwin_s[WINDOW:WINDOW + ROW_TILE] = kd_ref[...]
        vwin_s[0:WINDOW] = vdp_ref[...]
        vwin_s[WINDOW:WINDOW + ROW_TILE] = vd_ref[...]
        first = (i % tiles_per_seq) == 0
        col = lax.broadcasted_iota(jnp.int32, (1, KEY_SPAN), 1)
        for j in range(ROW_TILE // CHUNK):
            r0 = j * CHUNK
            valid = jnp.logical_or(jnp.logical_not(first), col + r0 >= WINDOW) if r0 < WINDOW else None

            def kwin_of(g, r0=r0):
                cols = slice(g * LANES, (g + 1) * LANES)
                return kwin_s[r0:r0 + KEY_SPAN, cols], vwin_s[r0:r0 + KEY_SPAN, cols]

            attend_rows(r0, kwin_of, valid)

    @pl.when(i >= n_prompt_tiles)
    def _sample():
        for s in range(n_streams):
            r0 = s * CHUNK
            sgu_rows(r0, CHUNK)
            kwin_s[0:WINDOW] = _dup_heads(ck_ref[s])
            kwin_s[WINDOW:KEY_SPAN] = kd_ref[r0:r0 + CHUNK]
            vwin_s[0:WINDOW] = _dup_heads(cv_ref[s])
            vwin_s[WINDOW:KEY_SPAN] = vd_ref[r0:r0 + CHUNK]

            def kwin_of(g):
                cols = slice(g * LANES, (g + 1) * LANES)
                return kwin_s[0:KEY_SPAN, cols], vwin_s[0:KEY_SPAN, cols]

            attend_rows(r0, kwin_of, None)

    hh, hl = hhi_s[...], hlo_s[...]
    logits_t = (_dot_nt(wrh_ref[...], hh) + _dot_nt(wrl_ref[...], hh) + _dot_nt(wrh_ref[...], hl)
                + jnp.concatenate([br_ref[...]] * (ROW_TILE // LANES), axis=1))
    m_a = ga_ref[...].astype(jnp.float32) * _dot(a_s[...], wpa_ref[...])
    idxs, gates, in_tile, count_col, count_row = _route_pick(logits_t)
    gate_ref[...] = gates
    m = m_a + gb_ref[...].astype(jnp.float32) * _dot(o_s[...], wpb_ref[...])
    slots, meta = _route_place(idxs, in_tile, count_col, count_row, carry_s, jnp.where(i > 0, 1.0, 0.0))
    slot_ref[...] = slots
    meta_ref[...] = meta
    count_ref[...] = carry_s[...].astype(jnp.int32)
    n_slots = ROW_TILE * TOP_K
    sid = lax.broadcasted_iota(jnp.int32, (n_slots, ROW_TILE), 0)
    place = jnp.zeros((n_slots, ROW_TILE), jnp.float32)
    for k in range(TOP_K):
        place = jnp.where(sid == slots[k:k + 1, :], 1.0, place)
    _store_row_tiled(hloc_ref, (), _dot(_bf16(place), hh))

    x = jnp.where(i < n_prompt_tiles, xp_ref[...], xs_ref[...])
    x1 = x + _dot(_bf16(m), wo_ref[...])
    x1_ref[...] = x1
    h2 = _rms(x1, gffn_ref[...])
    h2_hi = _bf16(h2)
    hhi_s[...] = h2_hi
    hlo_s[...] = _bf16(h2 - h2_hi.astype(jnp.float32))


def _mix(xp, xs, u, vln, q, kd, vd, cache_k, cache_v, ga, gb, w_sp, b_sp, sinks,
         w_pa, w_pb, w_o, g_ffn, w_router, b_router, seq):
    tp, ts = xp.shape[0], xs.shape[0]
    t = tp + ts
    npt = tp // ROW_TILE
    tiles_per_seq = seq // ROW_TILE
    f32, bf16 = jnp.float32, jnp.bfloat16
    n_streams = ROW_TILE // CHUNK
    win_per_tile = ROW_TILE // WINDOW

    nt = t // ROW_TILE
    cur = lambda i: jnp.minimum(i, nt - 1)
    smp = lambda i: jnp.maximum(cur(i) - npt, 0)
    row = lambda width: pl.BlockSpec((ROW_TILE, width), lambda i: (cur(i), 0))
    prev_spec = pl.BlockSpec(
        (WINDOW, KV_DUP_W), lambda i: (jnp.maximum(jnp.minimum(i, npt - 1) * win_per_tile - 1, 0), 0))
    cache_spec = pl.BlockSpec((n_streams, WINDOW, KV_W), lambda i: (smp(i), 0, 0))
    xs_spec = pl.BlockSpec((ROW_TILE, D_MODEL), lambda i: (smp(i), 0))
    sink_cols = jnp.broadcast_to(
        jnp.repeat(sinks.astype(f32).reshape(N_KV_HEADS, Q_PER_KV), CHUNK, axis=1)[:, :, None],
        (N_KV_HEADS, Q_PER_KV * CHUNK, LANES))
    wr_t = w_router.T
    wr_hi = wr_t.astype(bf16)
    wr_lo = (wr_t - wr_hi.astype(f32)).astype(bf16)
    routed8 = lambda width: pl.BlockSpec((8, width), lambda i: (jnp.maximum(i - 1, 0), 0))
    out_shape = (
        jax.ShapeDtypeStruct((t, D_MODEL), f32),
        jax.ShapeDtypeStruct((t * TOP_K * ROW_SUBTILES, LANES), f32),
        jax.ShapeDtypeStruct((nt * 8, ROW_TILE), jnp.int32),
        jax.ShapeDtypeStruct((nt * 8, ROW_TILE), f32),
        jax.ShapeDtypeStruct((nt * 8, LANES), jnp.int32),
        jax.ShapeDtypeStruct((8, LANES), jnp.int32),
    )
    return pl.pallas_call(
        functools.partial(_mix_kernel, tiles_per_seq, npt),
        out_shape=out_shape,
        grid=(nt + 1,),
        in_specs=[
            _prompt_spec(D_MODEL, npt), xs_spec,
            row(SGU_WIDTH), row(SGU_WIDTH), row(ATT_W),
            row(KV_DUP_W), row(KV_DUP_W), prev_spec, prev_spec,
            cache_spec, cache_spec, row(D_MODEL), row(D_MODEL),
            _const_spec((SGU_GROUPS, SGU_CHUNK, SGU_CHUNK)), _const_spec((SGU_GROUPS, SGU_CHUNK, LANES)),
            _const_spec((N_KV_HEADS, Q_PER_KV * CHUNK, LANES)),
            _const_spec((SGU_WIDTH, D_MODEL)), _const_spec((ATT_W, D_MODEL)),
            _const_spec((D_MODEL, D_MODEL)), _const_spec((1, D_MODEL)),
            _const_spec((N_EXPERTS, D_MODEL)), _const_spec((N_EXPERTS, D_MODEL)),
            _const_spec((N_EXPERTS, LANES)),
        ],
        out_specs=(row(D_MODEL),
                   pl.BlockSpec((ROW_TILE * TOP_K * ROW_SUBTILES, LANES),
                                lambda i: (jnp.maximum(i - 1, 0), 0)),
                   routed8(ROW_TILE), routed8(ROW_TILE), routed8(LANES), _const_spec((8, LANES))),
        scratch_shapes=[
            pltpu.VMEM((ROW_TILE, SGU_WIDTH), bf16), pltpu.VMEM((ROW_TILE, ATT_W), bf16),
            pltpu.VMEM((WINDOW + ROW_TILE, KV_DUP_W), bf16),
            pltpu.VMEM((WINDOW + ROW_TILE, KV_DUP_W), bf16),
            pltpu.VMEM((8, LANES), f32),
            pltpu.VMEM((ROW_TILE, D_MODEL), bf16), pltpu.VMEM((ROW_TILE, D_MODEL), bf16),
        ],
        compiler_params=_params(),
        name="mix",
    )(xp, xs, u, vln, q, kd, vd, kd, vd,
      cache_k.reshape(-1, WINDOW, KV_W), cache_v.reshape(-1, WINDOW, KV_W), ga, gb,
      w_sp, jnp.broadcast_to(b_sp[:, :, None], (SGU_GROUPS, SGU_CHUNK, LANES)), sink_cols,
      w_pa.astype(bf16), w_pb.astype(bf16), w_o.astype(bf16),
      g_ffn.reshape(1, -1), wr_hi, wr_lo,
      jnp.broadcast_to(b_router.astype(f32)[:, None], (N_EXPERTS, LANES)))


def _unrolled_rows(n_rows, fn):
    if isinstance(n_rows, int):
        groups, tail_start = n_rows // ROW_UNROLL, n_rows - n_rows % ROW_UNROLL
    else:
        groups = lax.shift_right_logical(n_rows, ROW_UNROLL.bit_length() - 1)
        tail_start = groups * ROW_UNROLL

    def group(gi, carry):
        for lane in range(ROW_UNROLL):
            fn(gi * ROW_UNROLL + lane, lane)
        return carry

    def tail(r, carry):
        fn(r, 0)
        return carry

    lax.fori_loop(0, groups, group, 0)
    lax.fori_loop(tail_start, n_rows, tail, 0)


def _row_span(first_row, n_rows):
    return pl.ds(pl.multiple_of(first_row * ROW_SUBTILES, ROW_SUBTILES),
                 pl.multiple_of(n_rows * ROW_SUBTILES, ROW_SUBTILES))


def _run_spec(index_of):
    return pl.BlockSpec((1, 1, LANES), lambda i, *_: (index_of(i), 0, 0), memory_space=pltpu.SMEM)


def _expert_kernel(n_token_tiles,
                   te_ref, nu_ref, nx_ref, par_ref, nv_ref, tf_ref, cnt_ref, loc_ref,
                   bgu_ref, bdn_ref, hloc_hbm, wgu_hbm, wdn_hbm,
                   ys_ref,
                   xbuf, wgu_f, wdn_f, wgu_s, wdn_s, walk, xsem, wsem):
    i = pl.program_id(0)
    n_used = nu_ref[0]
    expert = te_ref[i]
    buf = par_ref[i]
    slot = i % 2
    expert_changed = jnp.logical_or(i == 0, expert != te_ref[jnp.maximum(i - 1, 0)])

    def fetch_rows(j, b):
        e = te_ref[j]
        need = nv_ref[j]

        @pl.when(tf_ref[j] == 1)
        def _():
            walk[0] = 0
            walk[1] = 0

        @pl.when(need < MOE_TILE)
        def _():
            xbuf[b] = jnp.zeros(xbuf.shape[1:], xbuf.dtype)

        def unfinished(state):
            filled, tile, _ = state
            return jnp.logical_and(filled < need, tile < n_token_tiles)

        def take_run(state):
            filled, tile, off = state
            run = cnt_ref[tile * N_EXPERTS + e]
            take = jnp.minimum(run - off, need - filled)

            @pl.when(take > 0)
            def _():
                src = tile * (ROW_TILE * TOP_K) + loc_ref[tile * N_EXPERTS + e] + off
                pltpu.make_async_copy(hloc_hbm.at[_row_span(src, take)],
                                      xbuf.at[b, _row_span(filled, take)], xsem.at[b]).start()

            run_done = off + take == run
            return (filled + take, jnp.where(run_done, tile + 1, tile), jnp.where(run_done, 0, off + take))

        _, tile, off = lax.while_loop(unfinished, take_run, (jnp.int32(0), walk[0], walk[1]))
        walk[0] = tile
        walk[1] = off

    @pl.when(jnp.logical_and(i == 0, n_used > 0))
    def _():
        fetch_rows(0, 0)

    @pl.when(i + 1 < n_used)
    def _():
        fetch_rows(i + 1, 1 - slot)

    def weight_copies(e, b):
        return (pltpu.make_async_copy(wgu_hbm.at[e], wgu_f.at[b], wsem.at[0, b]),
                pltpu.make_async_copy(wdn_hbm.at[e], wdn_f.at[b], wsem.at[1, b]))

    @pl.when(jnp.logical_and(i < n_used, expert_changed))
    def _():
        @pl.when(i == 0)
        def _():
            for copy in weight_copies(expert, buf):
                copy.start()

        for copy in weight_copies(expert, buf):
            copy.wait()
        following = nx_ref[i]

        @pl.when(following != expert)
        def _():
            for copy in weight_copies(following, 1 - buf):
                copy.start()

        wgu_s[...] = _bf16(wgu_f[buf])
        wdn_s[...] = _bf16(wdn_f[buf])

    @pl.when(i < n_used)
    def _():
        rows = nv_ref[i]
        pltpu.make_async_copy(hloc_hbm.at[_row_span(0, rows)], xbuf.at[slot, _row_span(0, rows)],
                              xsem.at[slot]).wait()
        x = _bf16(_load_row_tiled(xbuf, (slot,), MOE_TILE))
        gu = _dot(x, wgu_s[...]) + bgu_ref[0]
        gate = jnp.minimum(gu[:, :D_FF], SWIGLU_LIMIT)
        lin = jnp.clip(gu[:, D_FF:], -SWIGLU_LIMIT, SWIGLU_LIMIT)
        act = gate * _sigmoid(SWIGLU_ALPHA * gate) * (lin + 1.0)
        _store_row_tiled(ys_ref, (), _dot(_bf16(act), wdn_s[...]) + bdn_ref[0])

    @pl.when(i >= n_used)
    def _():
        ys_ref[...] = jnp.zeros_like(ys_ref)


def _experts(tile_expert, n_used, next_expert, weight_buf, n_valid, tile_first, run_n, run_loc,
             h_local, w_gu, b_gu, w_dn, b_dn):
    n_tiles = tile_expert.shape[0]
    n_token_tiles = h_local.shape[0] // (ROW_TILE * TOP_K * ROW_SUBTILES)
    f32, bf16 = jnp.float32, jnp.bfloat16
    tile_rows = MOE_TILE * ROW_SUBTILES
    grid_spec = pltpu.PrefetchScalarGridSpec(
        num_scalar_prefetch=8,
        grid=(n_tiles,),
        in_specs=[
            pl.BlockSpec((1, 1, 2 * D_FF), lambda i, te, *_: (te[i], 0, 0)),
            pl.BlockSpec((1, 1, D_MODEL), lambda i, te, *_: (te[i], 0, 0)),
            pl.BlockSpec(memory_space=pl.ANY), pl.BlockSpec(memory_space=pl.ANY),
            pl.BlockSpec(memory_space=pl.ANY),
        ],
        out_specs=pl.BlockSpec((tile_rows, LANES), lambda i, *_: (i, 0)),
        scratch_shapes=[
            pltpu.VMEM((2, tile_rows, LANES), f32),
            pltpu.VMEM((2, D_MODEL, 2 * D_FF), f32), pltpu.VMEM((2, D_FF, D_MODEL), f32),
            pltpu.VMEM((D_MODEL, 2 * D_FF), bf16), pltpu.VMEM((D_FF, D_MODEL), bf16),
            pltpu.SMEM((2,), jnp.int32),
            pltpu.SemaphoreType.DMA((2,)), pltpu.SemaphoreType.DMA((2, 2)),
        ],
    )
    return pl.pallas_call(
        functools.partial(_expert_kernel, n_token_tiles),
        out_shape=jax.ShapeDtypeStruct((n_tiles * tile_rows, LANES), f32),
        grid_spec=grid_spec,
        compiler_params=_params(),
        name="experts",
    )(tile_expert, n_used, next_expert, weight_buf, n_valid, tile_first, run_n, run_loc,
      b_gu.reshape(N_EXPERTS, 1, -1), b_dn.reshape(N_EXPERTS, 1, -1), h_local, w_gu, w_dn)


def _combine_kernel(n_prompt_tiles,
                    src_ref, n_ref, dst_ref, src_nx_ref, n_nx_ref, dst_nx_ref, slot_ref, gate_ref,
                    x1_ref, gfin_ref, ys_hbm,
                    yp_ref, yo_ref,
                    local, mixed, run_sem):
    i = pl.program_id(0)
    last = pl.num_programs(0) - 1
    buf = i % 2

    def fetch_runs(s_ref, c_ref, d_ref, b):
        for e in range(N_EXPERTS):
            n = c_ref[0, 0, e]
            copy = pltpu.make_async_copy(ys_hbm.at[_row_span(d_ref[0, 0, e], n)],
                                         local.at[b, _row_span(s_ref[0, 0, e], n)], run_sem.at[b])
            pl.when(n > 0)(copy.start)

    @pl.when(i == 0)
    def _():
        fetch_runs(src_ref, n_ref, dst_ref, 0)

    @pl.when(i < last)
    def _():
        fetch_runs(src_nx_ref, n_nx_ref, dst_nx_ref, 1 - buf)

    pltpu.make_async_copy(ys_hbm.at[pl.ds(0, local.shape[1])], local.at[buf], run_sem.at[buf]).wait()

    def blend(t, lane):
        acc = None
        for k in range(TOP_K):
            at = pl.multiple_of(slot_ref[0, 0, k * ROW_TILE + t] * ROW_SUBTILES, ROW_SUBTILES)
            term = gate_ref[0, 0, k * ROW_TILE + t] * local[buf, pl.ds(at, ROW_SUBTILES), :]
            acc = term if acc is None else acc + term
        mixed[pl.ds(pl.multiple_of(t * ROW_SUBTILES, ROW_SUBTILES), ROW_SUBTILES), :] = acc
    _unrolled_rows(ROW_TILE, blend)

    out = _rms(x1_ref[...] + _load_row_tiled(mixed, (), ROW_TILE), gfin_ref[...])

    @pl.when(i < n_prompt_tiles)
    def _():
        yp_ref[...] = out

    @pl.when(i >= n_prompt_tiles)
    def _():
        yo_ref[...] = out


def _combine(run_src, run_n, run_dst, slots, gates, x1, ys, g_final, tp):
    t = x1.shape[0]
    npt = tp // ROW_TILE
    nt = t // ROW_TILE
    f32 = jnp.float32
    picks = ROW_TILE * TOP_K
    nxt = lambda i: jnp.minimum(i + 1, nt - 1)
    pick_spec = pl.BlockSpec((1, 1, picks), lambda i: (i, 0, 0), memory_space=pltpu.SMEM)
    return pl.pallas_call(
        functools.partial(_combine_kernel, npt),
        out_shape=(jax.ShapeDtypeStruct((tp, D_MODEL), f32),
                   jax.ShapeDtypeStruct((t - tp, D_MODEL), f32)),
        grid=(nt,),
        in_specs=[_run_spec(lambda i: i), _run_spec(lambda i: i), _run_spec(lambda i: i),
                  _run_spec(nxt), _run_spec(nxt), _run_spec(nxt), pick_spec, pick_spec,
                  _row_spec(D_MODEL), _const_spec((1, D_MODEL)), pl.BlockSpec(memory_space=pl.ANY)],
        out_specs=(_prompt_spec(D_MODEL, npt), _sample_spec(D_MODEL, npt)),
        scratch_shapes=[pltpu.VMEM((2, picks * ROW_SUBTILES, LANES), f32),
                        pltpu.VMEM((ROW_TILE * ROW_SUBTILES, LANES), f32),
                        pltpu.SemaphoreType.DMA((2,))],
        compiler_params=_params(),
        name="combine",
    )(run_src, run_n, run_dst, run_src, run_n, run_dst, slots, gates, x1, g_final.reshape(1, -1), ys)


def _plan(meta, counts, t):
    nt = t // ROW_TILE
    n_tiles = (t * TOP_K + N_EXPERTS * (MOE_TILE - 1)) // MOE_TILE
    counts = counts[0, :N_EXPERTS]
    tiles_e = (counts + MOE_TILE - 1) // MOE_TILE
    tile_end = jnp.cumsum(tiles_e)
    tile_start = tile_end - tiles_e
    n_used = tile_end[-1]
    first_row = jnp.pad(tile_start * MOE_TILE, (0, LANES - N_EXPERTS))
    meta = meta.reshape(nt, 8, LANES)
    run_dst = meta[:, 0:1, :] + first_row[None, None, :]
    run_n = meta[:, 1:2, :]
    run_src = meta[:, 2:3, :]
    tile_ids = jnp.arange(n_tiles, dtype=jnp.int32)
    live = jnp.minimum(tile_ids, n_used - 1)
    tile_expert = jnp.sum(tile_end[None, :] <= live[:, None], axis=1).astype(jnp.int32)
    in_expert = tile_ids - tile_start[tile_expert]
    n_valid = jnp.clip(counts[tile_expert] - in_expert * MOE_TILE, 0, MOE_TILE)
    n_valid = jnp.where(tile_ids < n_used, n_valid, 0).astype(jnp.int32)
    tile_first = jnp.logical_and(in_expert == 0, tile_ids < n_used).astype(jnp.int32)
    run_n_flat = run_n[:, 0, :N_EXPERTS].reshape(-1)
    run_loc_flat = run_src[:, 0, :N_EXPERTS].reshape(-1)
    used = tiles_e > 0
    ids = jnp.arange(N_EXPERTS, dtype=jnp.int32)
    later_used = jnp.where(jnp.logical_and(used[None, :], ids[None, :] > ids[:, None]), ids[None, :],
                           N_EXPERTS)
    following = jnp.min(later_used, axis=1)
    following = jnp.where(following < N_EXPERTS, following, ids)
    buf_of = (jnp.cumsum(used.astype(jnp.int32)) - 1) % 2
    return (tile_expert, n_used.reshape(1).astype(jnp.int32), following[tile_expert].astype(jnp.int32),
            buf_of[tile_expert].astype(jnp.int32), n_valid, tile_first, run_n_flat, run_loc_flat,
            run_src, run_n, run_dst)


def kernel(x_prompt, x_sample, cache_k, cache_v, g_mix, w_in, b_in, ln_v_g, ln_v_b, w_sp, b_sp,
           attn_sinks, w_pa, w_pb, w_o, g_ffn, w_router, b_router, w_gu, b_gu, w_dn, b_dn, g_final):
    nb, seq, d = x_prompt.shape
    nsb, nnew, _ = x_sample.shape
    tp, ts = nb * seq, nsb * nnew
    t = tp + ts
    xp = x_prompt.reshape(tp, d)
    xs = x_sample.reshape(ts, d)
    pos = np.concatenate([np.tile(np.arange(seq), nb), np.tile(PAST_LEN + np.arange(nnew), nsb)])
    tables = _rope_tables(pos)
    u, vln, v_sgu, q, k, v, kd, vd, ga, gb = _project(
        xp, xs, g_mix[0], w_in[0], b_in[0], ln_v_g[0], ln_v_b[0], tables)
    x1, h_local, slot_t, gate_t, meta, counts = _mix(
        xp, xs, u, vln, q, kd, vd, cache_k[0], cache_v[0], ga, gb, w_sp[0], b_sp[0], attn_sinks[0],
        w_pa[0], w_pb[0], w_o[0], g_ffn[0], w_router[0], b_router[0], seq)
    nt = t // ROW_TILE
    picks = lambda a: a.reshape(nt, 8, ROW_TILE)[:, :TOP_K, :].reshape(nt, 1, TOP_K * ROW_TILE)
    slots, gates = picks(slot_t), picks(gate_t)
    (tile_expert, n_used, next_expert, weight_buf, n_valid, tile_first, run_n_flat, run_loc_flat,
     run_src, run_n, run_dst) = _plan(meta, counts, t)
    y_sorted = _experts(tile_expert, n_used, next_expert, weight_buf, n_valid, tile_first, run_n_flat,
                        run_loc_flat, h_local, w_gu[0], b_gu[0], w_dn[0], b_dn[0])
    y_p, y_s = _combine(run_src, run_n, run_dst, slots, gates, x1, y_sorted, g_final, tp)

    keep = min(WINDOW, seq)
    tails = lambda a: jnp.stack([a[(b + 1) * seq - keep:(b + 1) * seq] for b in range(nb)]).reshape(
        nb, keep, N_KV_HEADS, HEAD_DIM)
    kp, vp = tails(k), tails(v)
    ks = k[tp:].reshape(nsb, nnew, N_KV_HEADS, HEAD_DIM)
    vs = v[tp:].reshape(nsb, nnew, N_KV_HEADS, HEAD_DIM)
    return (y_p.reshape(nb, seq, d), y_s.reshape(nsb, nnew, d), kp[None], vp[None], ks[None],
            vs[None], v_sgu.reshape(1, nsb, nnew, SGU_WIDTH))
```

```python
import functools

import numpy as np
import jax
import jax.numpy as jnp
from jax import lax
from jax.experimental import pallas as pl
from jax.experimental.pallas import tpu as pltpu

D_MODEL = 1024
PAST_LEN = 2048
CHUNK = 64
SGU_CHUNK = 128
SGU_GROUPS = 8
SGU_WIDTH = 1024
N_HEADS = 16
N_KV_HEADS = 4
HEAD_DIM = 64
Q_PER_KV = N_HEADS // N_KV_HEADS
WINDOW = 128
ROT_DIM = HEAD_DIM // 4
ROPE_THETA = 500000.0
ATT_W = N_HEADS * HEAD_DIM
KV_W = N_KV_HEADS * HEAD_DIM
N_EXPERTS = 32
TOP_K = 4
D_FF = 1024
SWIGLU_ALPHA = 1.702
SWIGLU_LIMIT = 7.0
NORM_EPS = 1e-5
NEG_INF = -1e30
N_IN = SGU_WIDTH * 2 + ATT_W + KV_W * 2 + D_MODEL * 2

LANES = 128
ROW_TILE = 256
MOE_TILE = 512
ROW_UNROLL = 8
COL_BLOCK = 256
KV_DUP_W = N_KV_HEADS * LANES
KEY_SPAN = WINDOW + CHUNK
VMEM_LIMIT = 56 * 1024 * 1024

_SQRT_HALF = 0.7071067811865476


def _gelu(x):
    z = jnp.abs(x) * _SQRT_HALF
    t = 1.0 / (1.0 + 0.3275911 * z)
    poly = t * (0.254829592 + t * (-0.284496736 + t * (1.421413741
                + t * (-1.453152027 + t * 1.061405429))))
    half_tail = (0.5 * x) * (poly * jnp.exp(-z * z))
    return jnp.where(x >= 0.0, x - half_tail, half_tail)


def _sigmoid(x):
    return 1.0 / (1.0 + jnp.exp(-x))


def _bf16(x):
    return x.astype(jnp.bfloat16)


def _dot(a, b):
    return jnp.dot(a, b, preferred_element_type=jnp.float32)


ROW_SUBTILES = D_MODEL // LANES


def _store_row_tiled(ref, lead, x):
    rows = x.shape[0]
    for s in range(ROW_SUBTILES):
        ref[(*lead, pl.ds(s, rows, stride=ROW_SUBTILES), slice(None))] = x[:, s * LANES:(s + 1) * LANES]


def _load_row_tiled(ref, lead, rows):
    return jnp.concatenate(
        [ref[(*lead, pl.ds(s, rows, stride=ROW_SUBTILES), slice(None))] for s in range(ROW_SUBTILES)],
        axis=1)


def _dot_nt(a, b):
    return lax.dot_general(a, b, (((1,), (1,)), ((), ())), preferred_element_type=jnp.float32)


def _rms(x, g):
    return x * lax.rsqrt(jnp.mean(x * x, axis=-1, keepdims=True) + NORM_EPS) * g


def _lane_lo(rows):
    return lax.broadcasted_iota(jnp.int32, (rows, LANES), 1) < HEAD_DIM


def _dup_heads(kv):
    rows = kv.shape[0]
    lo = _lane_lo(rows)
    out = []
    for j in range(KV_W // LANES):
        blk = kv[:, j * LANES:(j + 1) * LANES]
        swp = pltpu.roll(blk, HEAD_DIM, axis=1)
        out.append(jnp.where(lo, blk, swp))
        out.append(jnp.where(lo, swp, blk))
    return _bf16(jnp.concatenate(out, axis=1))


def _rope_block(zb, cos_b, sin_lo, sin_hi):
    up = pltpu.roll(zb, LANES - ROT_DIM // 2, axis=1)
    dn = pltpu.roll(zb, ROT_DIM // 2, axis=1)
    return zb * cos_b + up * sin_lo + dn * sin_hi


def _proj_kernel(n_prompt_tiles, xp_ref, xs_ref, gmix_ref, w_ref, b_ref, lng_ref, lnb_ref,
                 cos_ref, slo_ref, shi_ref,
                 u_ref, vln_ref, vs_ref, q_ref, k_ref, v_ref, kd_ref, vd_ref, ga_ref, gb_ref):
    i = pl.program_id(0)
    x = jnp.where(i < n_prompt_tiles, xp_ref[...], xs_ref[...])
    h = _bf16(_rms(x, gmix_ref[...]))

    cos_b, sin_lo, sin_hi = cos_ref[...], slo_ref[...], shi_ref[...]
    off_u, off_v, off_q, off_k, off_vv, off_ga, off_gb = (
        int(o) for o in np.cumsum((0, SGU_WIDTH, SGU_WIDTH, ATT_W, KV_W, KV_W, D_MODEL)))

    def z(lo):
        return _dot(h, w_ref[:, lo:lo + COL_BLOCK]) + b_ref[:, lo:lo + COL_BLOCK]

    def cols(j):
        return slice(j * COL_BLOCK, (j + 1) * COL_BLOCK)

    def rope(zb):
        return jnp.concatenate(
            [_rope_block(zb[:, t * LANES:(t + 1) * LANES], cos_b, sin_lo, sin_hi)
             for t in range(COL_BLOCK // LANES)], axis=1)

    gelu_v = []

    def do_u(j):
        u_ref[:, cols(j)] = _bf16(_gelu(z(off_u + j * COL_BLOCK)))

    def do_v(j):
        gelu_v.append(_gelu(z(off_v + j * COL_BLOCK)))

    def do_q(j):
        q_ref[:, cols(j)] = _bf16(rope(z(off_q + j * COL_BLOCK)) * (HEAD_DIM ** -0.5))

    def do_k(j):
        kr = rope(z(off_k))
        k_ref[...] = kr
        kd_ref[...] = _dup_heads(kr)

    def do_vv(j):
        zv = z(off_vv)
        v_ref[...] = zv
        vd_ref[...] = _dup_heads(zv)

    def do_ga(j):
        ga_ref[:, cols(j)] = _bf16(_sigmoid(z(off_ga + j * COL_BLOCK)))

    def do_gb(j):
        gb_ref[:, cols(j)] = _bf16(_sigmoid(z(off_gb + j * COL_BLOCK)))

    order = ((do_v, 0), (do_q, 0), (do_v, 1), (do_q, 1), (do_v, 2), (do_q, 2), (do_v, 3), (do_q, 3),
             (do_u, 0), (do_k, 0), (do_u, 1), (do_vv, 0), (do_u, 2), (do_ga, 0), (do_u, 3), (do_ga, 1),
             (do_ga, 2), (do_ga, 3), (do_gb, 0), (do_gb, 1), (do_gb, 2), (do_gb, 3))
    for fn, j in order:
        fn(j)
    gv = jnp.concatenate(gelu_v, axis=1)
    gc = gv - jnp.mean(gv, axis=-1, keepdims=True)
    var = jnp.mean(gc * gc, axis=-1, keepdims=True)
    vln = gc * lax.rsqrt(var + NORM_EPS) * lng_ref[...] + lnb_ref[...]
    vln_ref[...] = _bf16(vln)
    vs_ref[...] = vln


def _rope_tables(pos):
    half = ROT_DIM // 2
    inv = np.float32(ROPE_THETA) ** (-np.arange(half, dtype=np.float32) * np.float32(2.0) / ROT_DIM)
    ang = pos.astype(np.float32)[:, None] * inv.astype(np.float32)[None, :]
    cos = np.cos(ang.astype(np.float64)).astype(np.float32)
    sin = np.sin(ang.astype(np.float64)).astype(np.float32)
    n = pos.shape[0]
    ones = np.ones((n, HEAD_DIM - ROT_DIM), np.float32)
    zeros = np.zeros((n, HEAD_DIM - ROT_DIM), np.float32)
    zh = np.zeros((n, half), np.float32)
    cos_h = np.concatenate([cos, cos, ones], axis=1)
    slo_h = np.concatenate([-sin, zh, zeros], axis=1)
    shi_h = np.concatenate([zh, sin, zeros], axis=1)
    rep = LANES // HEAD_DIM
    return tuple(jnp.asarray(np.tile(a, (1, rep))) for a in (cos_h, slo_h, shi_h))


def _row_spec(width):
    return pl.BlockSpec((ROW_TILE, width), lambda i: (i, 0))


def _const_spec(shape):
    return pl.BlockSpec(shape, lambda i: (0,) * len(shape))


def _prompt_spec(width, n_prompt_tiles):
    return pl.BlockSpec((ROW_TILE, width), lambda i: (jnp.minimum(i, n_prompt_tiles - 1), 0))


def _sample_spec(width, n_prompt_tiles):
    return pl.BlockSpec((ROW_TILE, width), lambda i: (jnp.maximum(i - n_prompt_tiles, 0), 0))


def _params():
    return pltpu.CompilerParams(dimension_semantics=("arbitrary",), vmem_limit_bytes=VMEM_LIMIT)


def _project(xp, xs, g_mix, w_in, b_in, ln_g, ln_b, tables, seq):
    tp, ts = xp.shape[0], xs.shape[0]
    t = tp + ts
    npt = tp // ROW_TILE
    tiles_per_seq = seq // ROW_TILE
    f32, bf16 = jnp.float32, jnp.bfloat16
    table_spec = pl.BlockSpec(
        (ROW_TILE, LANES), lambda i: (jnp.where(i < npt, i % tiles_per_seq, tiles_per_seq), 0))
    out_shape = (
        jax.ShapeDtypeStruct((t, SGU_WIDTH), bf16),
        jax.ShapeDtypeStruct((t, SGU_WIDTH), bf16),
        jax.ShapeDtypeStruct((ts, SGU_WIDTH), f32),
        jax.ShapeDtypeStruct((t, ATT_W), bf16),
        jax.ShapeDtypeStruct((t, KV_W), f32),
        jax.ShapeDtypeStruct((t, KV_W), f32),
        jax.ShapeDtypeStruct((t, KV_DUP_W), bf16),
        jax.ShapeDtypeStruct((t, KV_DUP_W), bf16),
        jax.ShapeDtypeStruct((t, D_MODEL), bf16),
        jax.ShapeDtypeStruct((t, D_MODEL), bf16),
    )
    return pl.pallas_call(
        functools.partial(_proj_kernel, npt),
        out_shape=out_shape,
        grid=(t // ROW_TILE,),
        in_specs=[
            _prompt_spec(D_MODEL, npt), _sample_spec(D_MODEL, npt),
            _const_spec((1, D_MODEL)), _const_spec((D_MODEL, N_IN)),
            _const_spec((1, N_IN)), _const_spec((1, SGU_WIDTH)), _const_spec((1, SGU_WIDTH)),
            table_spec, table_spec, table_spec,
        ],
        out_specs=(
            _row_spec(SGU_WIDTH), _row_spec(SGU_WIDTH), _sample_spec(SGU_WIDTH, npt),
            _row_spec(ATT_W), _row_spec(KV_W), _row_spec(KV_W), _row_spec(KV_DUP_W),
            _row_spec(KV_DUP_W), _row_spec(D_MODEL), _row_spec(D_MODEL),
        ),
        compiler_params=_params(),
        name="proj",
    )(xp, xs, g_mix.reshape(1, -1), w_in.astype(bf16), b_in.reshape(1, -1),
      ln_g.reshape(1, -1), ln_b.reshape(1, -1), *tables)


def _attend(qa, qb, kwin, vwin, sink, valid):
    lo = _lane_lo(CHUNK)
    zero = jnp.zeros_like(qa)
    lhs = jnp.concatenate([jnp.where(lo, qa, zero), jnp.where(lo, zero, qa),
                           jnp.where(lo, qb, zero), jnp.where(lo, zero, qb)], axis=0)
    s = _dot_nt(lhs, kwin)
    if valid is not None:
        s = jnp.where(valid, s, NEG_INF)
    s_a, s_b = s[:, :LANES], s[:, LANES:]
    tail = s_b.shape[1]
    m = jnp.maximum(jnp.max(s, axis=-1, keepdims=True), sink)
    p_a = jnp.exp(s_a - m)
    p_b = jnp.exp(s_b - m[:, :tail])
    denom = (jnp.sum(jnp.concatenate([p_a, p_b], axis=1), axis=-1, keepdims=True)
             + jnp.exp(sink - m))
    inv = 1.0 / denom
    pn = jnp.concatenate([p_a * inv, p_b * inv[:, :tail]], axis=1)
    r = _dot(_bf16(pn), vwin)
    oa = jnp.where(lo, r[0:CHUNK], r[CHUNK:2 * CHUNK])
    ob = jnp.where(lo, r[2 * CHUNK:3 * CHUNK], r[3 * CHUNK:4 * CHUNK])
    return oa, ob


def _stack_rows(rows):
    ri = lax.broadcasted_iota(jnp.int32, (8, rows[0].shape[1]), 0)
    out = jnp.zeros((8, rows[0].shape[1]), rows[0].dtype)
    for k, row in enumerate(rows):
        out = jnp.where(ri == k, row, out)
    return out


def _route_pick(logits_t):
    rows = logits_t.shape[1]
    eid = lax.broadcasted_iota(jnp.int32, (N_EXPERTS, rows), 0)
    work = logits_t
    vals, idxs = [], []
    for _ in range(TOP_K):
        m = jnp.max(work, axis=0, keepdims=True)
        idx = jnp.min(jnp.where(work == m, eid, N_EXPERTS), axis=0, keepdims=True)
        vals.append(m)
        idxs.append(idx)
        work = jnp.where(eid == idx, -jnp.inf, work)
    exps = [jnp.exp(v - vals[0]) for v in vals]
    inv = 1.0 / (exps[0] + exps[1] + exps[2] + exps[3])
    gates = _stack_rows([e * inv for e in exps])

    picked = jnp.zeros((N_EXPERTS, rows), jnp.float32)
    for idx in idxs:
        picked = jnp.where(eid == idx, 1.0, picked)
    tr = lax.broadcasted_iota(jnp.int32, (rows, rows), 0)
    tc = lax.broadcasted_iota(jnp.int32, (rows, rows), 1)
    earlier = _bf16(jnp.where(tr < tc, 1.0, 0.0))
    in_tile = _dot(_bf16(picked), earlier)
    count_col = jnp.broadcast_to(jnp.sum(picked, axis=1, keepdims=True), (N_EXPERTS, LANES))
    eid_wide = lax.broadcasted_iota(jnp.int32, (LANES, rows), 0)
    picked_wide = jnp.zeros((LANES, rows), jnp.float32)
    for idx in idxs:
        picked_wide = jnp.where(eid_wide == idx, 1.0, picked_wide)
    count_row = _dot_nt(jnp.ones((8, rows), jnp.bfloat16), _bf16(picked_wide))
    return idxs, gates, in_tile, count_col, count_row


def _route_place(idxs, in_tile, count_col, count_row, carry_s, live):
    rows = in_tile.shape[1]
    eid = lax.broadcasted_iota(jnp.int32, (N_EXPERTS, rows), 0)
    er = lax.broadcasted_iota(jnp.int32, (N_EXPERTS, N_EXPERTS), 0)
    ec = lax.broadcasted_iota(jnp.int32, (N_EXPERTS, N_EXPERTS), 1)
    start_col = _dot(_bf16(jnp.where(ec < er, 1.0, 0.0)), _bf16(count_col))
    local = in_tile + jnp.concatenate([start_col] * (rows // LANES), axis=1)
    slots = _stack_rows([jnp.sum(jnp.where(eid == idx, local, 0.0), axis=0, keepdims=True)
                         for idx in idxs]).astype(jnp.int32)
    lr = lax.broadcasted_iota(jnp.int32, (LANES, LANES), 0)
    lc = lax.broadcasted_iota(jnp.int32, (LANES, LANES), 1)
    start_row = _dot(_bf16(count_row), _bf16(jnp.where(lr < lc, 1.0, 0.0)))
    ri = lax.broadcasted_iota(jnp.int32, (8, LANES), 0)
    meta = jnp.where(ri == 0, carry_s[...], jnp.where(ri == 1, count_row, jnp.where(ri == 2, start_row, 0.0)))
    carry_s[...] = carry_s[...] + count_row * live
    return slots, meta.astype(jnp.int32)


def _mix_kernel(tiles_per_seq, n_prompt_tiles,
                xp_ref, xs_ref, u_ref, vln_ref, q_ref, kd_ref, vd_ref, kdp_ref, vdp_ref,
                ck_ref, cv_ref, ga_ref, gb_ref, wsp_ref, bsp_ref, sink_ref,
                wpa_ref, wpb_ref, wo_ref, gffn_ref, wrh_ref, wrl_ref, br_ref,
                x1_ref, hloc_ref, slot_ref, gate_ref, meta_ref, count_ref,
                a_s, o_s, kwin_s, vwin_s, carry_s, hhi_s, hlo_s):
    i = pl.program_id(0)
    n_streams = ROW_TILE // CHUNK

    @pl.when(i == 0)
    def _():
        carry_s[...] = jnp.zeros_like(carry_s)
        hhi_s[...] = jnp.zeros_like(hhi_s)
        hlo_s[...] = jnp.zeros_like(hlo_s)

    def sgu_rows(r0, rows):
        ri = lax.broadcasted_iota(jnp.int32, (rows, rows), 0) // CHUNK
        ci = lax.broadcasted_iota(jnp.int32, (rows, rows), 1) // CHUNK
        for g in range(SGU_GROUPS):
            cols = slice(g * LANES, (g + 1) * LANES)
            w = _bf16(jnp.where(ci <= ri, wsp_ref[g, :rows, :rows], 0.0))
            sp = _dot(w, vln_ref[r0:r0 + rows, cols]) + bsp_ref[g, :rows, :]
            a_s[r0:r0 + rows, cols] = _bf16(u_ref[r0:r0 + rows, cols].astype(jnp.float32) * sp)

    def attend_rows(r0, kwin_of, valid):
        for g in range(N_KV_HEADS):
            c0 = g * Q_PER_KV * HEAD_DIM
            kwin, vwin = kwin_of(g)
            oa, ob = _attend(q_ref[r0:r0 + CHUNK, c0:c0 + LANES],
                             q_ref[r0:r0 + CHUNK, c0 + LANES:c0 + 2 * LANES],
                             kwin, vwin, sink_ref[g], valid)
            o_s[r0:r0 + CHUNK, c0:c0 + LANES] = _bf16(oa)
            o_s[r0:r0 + CHUNK, c0 + LANES:c0 + 2 * LANES] = _bf16(ob)

    @pl.when(i < n_prompt_tiles)
    def _prompt():
        for c in range(ROW_TILE // SGU_CHUNK):
            sgu_rows(c * SGU_CHUNK, SGU_CHUNK)
        kwin_s[0:WINDOW] = kdp_ref[...]
        kwin_s[WINDOW:WINDOW + ROW_TILE] = kd_ref[...]
        vwin_s[0:WINDOW] = vdp_ref[...]
        vwin_s[WINDOW:WINDOW + ROW_TILE] = vd_ref[...]
        first = (i % tiles_per_seq) == 0
        col = lax.broadcasted_iota(jnp.int32, (1, KEY_SPAN), 1)
        for j in range(ROW_TILE // CHUNK):
            r0 = j * CHUNK
            valid = jnp.logical_or(jnp.logical_not(first), col + r0 >= WINDOW) if r0 < WINDOW else None

            def kwin_of(g, r0=r0):
                cols = slice(g * LANES, (g + 1) * LANES)
                return kwin_s[r0:r0 + KEY_SPAN, cols], vwin_s[r0:r0 + KEY_SPAN, cols]

            attend_rows(r0, kwin_of, valid)

    @pl.when(i >= n_prompt_tiles)
    def _sample():
        for s in range(n_streams):
            r0 = s * CHUNK
            sgu_rows(r0, CHUNK)
            kwin_s[0:WINDOW] = _dup_heads(ck_ref[s])
            kwin_s[WINDOW:KEY_SPAN] = kd_ref[r0:r0 + CHUNK]
            vwin_s[0:WINDOW] = _dup_heads(cv_ref[s])
            vwin_s[WINDOW:KEY_SPAN] = vd_ref[r0:r0 + CHUNK]

            def kwin_of(g):
                cols = slice(g * LANES, (g + 1) * LANES)
                return kwin_s[0:KEY_SPAN, cols], vwin_s[0:KEY_SPAN, cols]

            attend_rows(r0, kwin_of, None)

    hh, hl = hhi_s[...], hlo_s[...]
    logits_t = (_dot_nt(wrh_ref[...], hh) + _dot_nt(wrl_ref[...], hh) + _dot_nt(wrh_ref[...], hl)
                + jnp.concatenate([br_ref[...]] * (ROW_TILE // LANES), axis=1))
    m_a = ga_ref[...].astype(jnp.float32) * _dot(a_s[...], wpa_ref[...])
    idxs, gates, in_tile, count_col, count_row = _route_pick(logits_t)
    gate_ref[...] = gates
    m = m_a + gb_ref[...].astype(jnp.float32) * _dot(o_s[...], wpb_ref[...])
    slots, meta = _route_place(idxs, in_tile, count_col, count_row, carry_s, jnp.where(i > 0, 1.0, 0.0))
    slot_ref[...] = slots
    meta_ref[...] = meta
    count_ref[...] = carry_s[...].astype(jnp.int32)
    n_slots = ROW_TILE * TOP_K
    sid = lax.broadcasted_iota(jnp.int32, (n_slots, ROW_TILE), 0)
    place = jnp.zeros((n_slots, ROW_TILE), jnp.float32)
    for k in range(TOP_K):
        place = jnp.where(sid == slots[k:k + 1, :], 1.0, place)
    _store_row_tiled(hloc_ref, (), _dot(_bf16(place), hh))

    x = jnp.where(i < n_prompt_tiles, xp_ref[...], xs_ref[...])
    x1 = x + _dot(_bf16(m), wo_ref[...])
    x1_ref[...] = x1
    h2 = _rms(x1, gffn_ref[...])
    h2_hi = _bf16(h2)
    hhi_s[...] = h2_hi
    hlo_s[...] = _bf16(h2 - h2_hi.astype(jnp.float32))


def _mix(xp, xs, u, vln, q, kd, vd, cache_k, cache_v, ga, gb, w_sp, b_sp, sinks,
         w_pa, w_pb, w_o, g_ffn, w_router, b_router, seq):
    tp, ts = xp.shape[0], xs.shape[0]
    t = tp + ts
    npt = tp // ROW_TILE
    tiles_per_seq = seq // ROW_TILE
    f32, bf16 = jnp.float32, jnp.bfloat16
    n_streams = ROW_TILE // CHUNK
    win_per_tile = ROW_TILE // WINDOW

    nt = t // ROW_TILE
    cur = lambda i: jnp.minimum(i, nt - 1)
    smp = lambda i: jnp.maximum(cur(i) - npt, 0)
    row = lambda width: pl.BlockSpec((ROW_TILE, width), lambda i: (cur(i), 0))
    prev_spec = pl.BlockSpec(
        (WINDOW, KV_DUP_W), lambda i: (jnp.maximum(jnp.minimum(i, npt - 1) * win_per_tile - 1, 0), 0))
    cache_spec = pl.BlockSpec((n_streams, WINDOW, KV_W), lambda i: (smp(i), 0, 0))
    xs_spec = pl.BlockSpec((ROW_TILE, D_MODEL), lambda i: (smp(i), 0))
    sink_cols = jnp.broadcast_to(
        jnp.repeat(sinks.astype(f32).reshape(N_KV_HEADS, Q_PER_KV), CHUNK, axis=1)[:, :, None],
        (N_KV_HEADS, Q_PER_KV * CHUNK, LANES))
    wr_t = w_router.T
    wr_hi = wr_t.astype(bf16)
    wr_lo = (wr_t - wr_hi.astype(f32)).astype(bf16)
    routed8 = lambda width: pl.BlockSpec((8, width), lambda i: (jnp.maximum(i - 1, 0), 0))
    out_shape = (
        jax.ShapeDtypeStruct((t, D_MODEL), f32),
        jax.ShapeDtypeStruct((t * TOP_K * ROW_SUBTILES, LANES), f32),
        jax.ShapeDtypeStruct((nt * 8, ROW_TILE), jnp.int32),
        jax.ShapeDtypeStruct((nt * 8, ROW_TILE), f32),
        jax.ShapeDtypeStruct((nt * 8, LANES), jnp.int32),
        jax.ShapeDtypeStruct((8, LANES), jnp.int32),
    )
    return pl.pallas_call(
        functools.partial(_mix_kernel, tiles_per_seq, npt),
        out_shape=out_shape,
        grid=(nt + 1,),
        in_specs=[
            _prompt_spec(D_MODEL, npt), xs_spec,
            row(SGU_WIDTH), row(SGU_WIDTH), row(ATT_W),
            row(KV_DUP_W), row(KV_DUP_W), prev_spec, prev_spec,
            cache_spec, cache_spec, row(D_MODEL), row(D_MODEL),
            _const_spec((SGU_GROUPS, SGU_CHUNK, SGU_CHUNK)), _const_spec((SGU_GROUPS, SGU_CHUNK, LANES)),
            _const_spec((N_KV_HEADS, Q_PER_KV * CHUNK, LANES)),
            _const_spec((SGU_WIDTH, D_MODEL)), _const_spec((ATT_W, D_MODEL)),
            _const_spec((D_MODEL, D_MODEL)), _const_spec((1, D_MODEL)),
            _const_spec((N_EXPERTS, D_MODEL)), _const_spec((N_EXPERTS, D_MODEL)),
            _const_spec((N_EXPERTS, LANES)),
        ],
        out_specs=(row(D_MODEL),
                   pl.BlockSpec((ROW_TILE * TOP_K * ROW_SUBTILES, LANES),
                                lambda i: (jnp.maximum(i - 1, 0), 0)),
                   routed8(ROW_TILE), routed8(ROW_TILE), routed8(LANES), _const_spec((8, LANES))),
        scratch_shapes=[
            pltpu.VMEM((ROW_TILE, SGU_WIDTH), bf16), pltpu.VMEM((ROW_TILE, ATT_W), bf16),
            pltpu.VMEM((WINDOW + ROW_TILE, KV_DUP_W), bf16),
            pltpu.VMEM((WINDOW + ROW_TILE, KV_DUP_W), bf16),
            pltpu.VMEM((8, LANES), f32),
            pltpu.VMEM((ROW_TILE, D_MODEL), bf16), pltpu.VMEM((ROW_TILE, D_MODEL), bf16),
        ],
        compiler_params=_params(),
        name="mix",
    )(xp, xs, u, vln, q, kd, vd, kd, vd,
      cache_k.reshape(-1, WINDOW, KV_W), cache_v.reshape(-1, WINDOW, KV_W), ga, gb,
      w_sp, jnp.broadcast_to(b_sp[:, :, None], (SGU_GROUPS, SGU_CHUNK, LANES)), sink_cols,
      w_pa.astype(bf16), w_pb.astype(bf16), w_o.astype(bf16),
      g_ffn.reshape(1, -1), wr_hi, wr_lo,
      jnp.broadcast_to(b_router.astype(f32)[:, None], (N_EXPERTS, LANES)))


def _unrolled_rows(n_rows, fn):
    if isinstance(n_rows, int):
        groups, tail_start = n_rows // ROW_UNROLL, n_rows - n_rows % ROW_UNROLL
    else:
        groups = lax.shift_right_logical(n_rows, ROW_UNROLL.bit_length() - 1)
        tail_start = groups * ROW_UNROLL

    def group(gi, carry):
        for lane in range(ROW_UNROLL):
            fn(gi * ROW_UNROLL + lane, lane)
        return carry

    def tail(r, carry):
        fn(r, 0)
        return carry

    lax.fori_loop(0, groups, group, 0)
    lax.fori_loop(tail_start, n_rows, tail, 0)


def _row_span(first_row, n_rows):
    return pl.ds(pl.multiple_of(first_row * ROW_SUBTILES, ROW_SUBTILES),
                 pl.multiple_of(n_rows * ROW_SUBTILES, ROW_SUBTILES))


def _run_spec(index_of):
    return pl.BlockSpec((1, 1, LANES), lambda i, *_: (index_of(i), 0, 0), memory_space=pltpu.SMEM)


def _expert_kernel(n_token_tiles,
                   te_ref, nu_ref, nx_ref, par_ref, nv_ref, tf_ref, cnt_ref, loc_ref,
                   bgu_ref, bdn_ref, hloc_hbm, wgu_hbm, wdn_hbm,
                   ys_ref,
                   xbuf, wgu_f, wdn_f, wgu_s, wdn_s, walk, xsem, wsem):
    i = pl.program_id(0)
    n_used = nu_ref[0]
    expert = te_ref[i]
    buf = par_ref[i]
    slot = i % 2
    expert_changed = jnp.logical_or(i == 0, expert != te_ref[jnp.maximum(i - 1, 0)])

    def fetch_rows(j, b):
        e = te_ref[j]
        need = nv_ref[j]

        @pl.when(tf_ref[j] == 1)
        def _():
            walk[0] = 0
            walk[1] = 0

        @pl.when(need < MOE_TILE)
        def _():
            xbuf[b] = jnp.zeros(xbuf.shape[1:], xbuf.dtype)

        def unfinished(state):
            filled, tile, _ = state
            return jnp.logical_and(filled < need, tile < n_token_tiles)

        def take_run(state):
            filled, tile, off = state
            run = cnt_ref[tile * N_EXPERTS + e]
            take = jnp.minimum(run - off, need - filled)

            @pl.when(take > 0)
            def _():
                src = tile * (ROW_TILE * TOP_K) + loc_ref[tile * N_EXPERTS + e] + off
                pltpu.make_async_copy(hloc_hbm.at[_row_span(src, take)],
                                      xbuf.at[b, _row_span(filled, take)], xsem.at[b]).start()

            run_done = off + take == run
            return (filled + take, jnp.where(run_done, tile + 1, tile), jnp.where(run_done, 0, off + take))

        _, tile, off = lax.while_loop(unfinished, take_run, (jnp.int32(0), walk[0], walk[1]))
        walk[0] = tile
        walk[1] = off

    @pl.when(jnp.logical_and(i == 0, n_used > 0))
    def _():
        fetch_rows(0, 0)

    @pl.when(i + 1 < n_used)
    def _():
        fetch_rows(i + 1, 1 - slot)

    def weight_copies(e, b):
        return (pltpu.make_async_copy(wgu_hbm.at[e], wgu_f.at[b], wsem.at[0, b]),
                pltpu.make_async_copy(wdn_hbm.at[e], wdn_f.at[b], wsem.at[1, b]))

    @pl.when(jnp.logical_and(i < n_used, expert_changed))
    def _():
        @pl.when(i == 0)
        def _():
            for copy in weight_copies(expert, buf):
                copy.start()

        for copy in weight_copies(expert, buf):
            copy.wait()
        following = nx_ref[i]

        @pl.when(following != expert)
        def _():
            for copy in weight_copies(following, 1 - buf):
                copy.start()

        wgu_s[...] = _bf16(wgu_f[buf])
        wdn_s[...] = _bf16(wdn_f[buf])

    @pl.when(i < n_used)
    def _():
        rows = nv_ref[i]
        pltpu.make_async_copy(hloc_hbm.at[_row_span(0, rows)], xbuf.at[slot, _row_span(0, rows)],
                              xsem.at[slot]).wait()
        x = _bf16(_load_row_tiled(xbuf, (slot,), MOE_TILE))
        gu = _dot(x, wgu_s[...]) + bgu_ref[0]
        gate = jnp.minimum(gu[:, :D_FF], SWIGLU_LIMIT)
        lin = jnp.clip(gu[:, D_FF:], -SWIGLU_LIMIT, SWIGLU_LIMIT)
        act = gate * _sigmoid(SWIGLU_ALPHA * gate) * (lin + 1.0)
        _store_row_tiled(ys_ref, (), _dot(_bf16(act), wdn_s[...]) + bdn_ref[0])

    @pl.when(i >= n_used)
    def _():
        ys_ref[...] = jnp.zeros_like(ys_ref)


def _experts(tile_expert, n_used, next_expert, weight_buf, n_valid, tile_first, run_n, run_loc,
             h_local, w_gu, b_gu, w_dn, b_dn):
    n_tiles = tile_expert.shape[0]
    n_token_tiles = h_local.shape[0] // (ROW_TILE * TOP_K * ROW_SUBTILES)
    f32, bf16 = jnp.float32, jnp.bfloat16
    tile_rows = MOE_TILE * ROW_SUBTILES
    grid_spec = pltpu.PrefetchScalarGridSpec(
        num_scalar_prefetch=8,
        grid=(n_tiles,),
        in_specs=[
            pl.BlockSpec((1, 1, 2 * D_FF), lambda i, te, *_: (te[i], 0, 0)),
            pl.BlockSpec((1, 1, D_MODEL), lambda i, te, *_: (te[i], 0, 0)),
            pl.BlockSpec(memory_space=pl.ANY), pl.BlockSpec(memory_space=pl.ANY),
            pl.BlockSpec(memory_space=pl.ANY),
        ],
        out_specs=pl.BlockSpec((tile_rows, LANES), lambda i, *_: (i, 0)),
        scratch_shapes=[
            pltpu.VMEM((2, tile_rows, LANES), f32),
            pltpu.VMEM((2, D_MODEL, 2 * D_FF), f32), pltpu.VMEM((2, D_FF, D_MODEL), f32),
            pltpu.VMEM((D_MODEL, 2 * D_FF), bf16), pltpu.VMEM((D_FF, D_MODEL), bf16),
            pltpu.SMEM((2,), jnp.int32),
            pltpu.SemaphoreType.DMA((2,)), pltpu.SemaphoreType.DMA((2, 2)),
        ],
    )
    return pl.pallas_call(
        functools.partial(_expert_kernel, n_token_tiles),
        out_shape=jax.ShapeDtypeStruct((n_tiles * tile_rows, LANES), f32),
        grid_spec=grid_spec,
        compiler_params=_params(),
        name="experts",
    )(tile_expert, n_used, next_expert, weight_buf, n_valid, tile_first, run_n, run_loc,
      b_gu.reshape(N_EXPERTS, 1, -1), b_dn.reshape(N_EXPERTS, 1, -1), h_local, w_gu, w_dn)


def _combine_kernel(n_prompt_tiles,
                    src_ref, n_ref, dst_ref, src_nx_ref, n_nx_ref, dst_nx_ref, slot_ref, gate_ref,
                    x1_ref, gfin_ref, ys_hbm,
                    yp_ref, yo_ref,
                    local, mixed, run_sem):
    i = pl.program_id(0)
    last = pl.num_programs(0) - 1
    buf = i % 2

    def fetch_runs(s_ref, c_ref, d_ref, b):
        for e in range(N_EXPERTS):
            n = c_ref[0, 0, e]
            copy = pltpu.make_async_copy(ys_hbm.at[_row_span(d_ref[0, 0, e], n)],
                                         local.at[b, _row_span(s_ref[0, 0, e], n)], run_sem.at[b])
            pl.when(n > 0)(copy.start)

    @pl.when(i == 0)
    def _():
        fetch_runs(src_ref, n_ref, dst_ref, 0)

    @pl.when(i < last)
    def _():
        fetch_runs(src_nx_ref, n_nx_ref, dst_nx_ref, 1 - buf)

    pltpu.make_async_copy(ys_hbm.at[pl.ds(0, local.shape[1])], local.at[buf], run_sem.at[buf]).wait()

    def blend(t, lane):
        acc = None
        for k in range(TOP_K):
            at = pl.multiple_of(slot_ref[0, 0, k * ROW_TILE + t] * ROW_SUBTILES, ROW_SUBTILES)
            term = gate_ref[0, 0, k * ROW_TILE + t] * local[buf, pl.ds(at, ROW_SUBTILES), :]
            acc = term if acc is None else acc + term
        mixed[pl.ds(pl.multiple_of(t * ROW_SUBTILES, ROW_SUBTILES), ROW_SUBTILES), :] = acc
    _unrolled_rows(ROW_TILE, blend)

    out = _rms(x1_ref[...] + _load_row_tiled(mixed, (), ROW_TILE), gfin_ref[...])

    @pl.when(i < n_prompt_tiles)
    def _():
        yp_ref[...] = out

    @pl.when(i >= n_prompt_tiles)
    def _():
        yo_ref[...] = out


def _combine(run_src, run_n, run_dst, slots, gates, x1, ys, g_final, tp):
    t = x1.shape[0]
    npt = tp // ROW_TILE
    nt = t // ROW_TILE
    f32 = jnp.float32
    picks = ROW_TILE * TOP_K
    nxt = lambda i: jnp.minimum(i + 1, nt - 1)
    pick_spec = pl.BlockSpec((1, 1, picks), lambda i: (i, 0, 0), memory_space=pltpu.SMEM)
    return pl.pallas_call(
        functools.partial(_combine_kernel, npt),
        out_shape=(jax.ShapeDtypeStruct((tp, D_MODEL), f32),
                   jax.ShapeDtypeStruct((t - tp, D_MODEL), f32)),
        grid=(nt,),
        in_specs=[_run_spec(lambda i: i), _run_spec(lambda i: i), _run_spec(lambda i: i),
                  _run_spec(nxt), _run_spec(nxt), _run_spec(nxt), pick_spec, pick_spec,
                  _row_spec(D_MODEL), _const_spec((1, D_MODEL)), pl.BlockSpec(memory_space=pl.ANY)],
        out_specs=(_prompt_spec(D_MODEL, npt), _sample_spec(D_MODEL, npt)),
        scratch_shapes=[pltpu.VMEM((2, picks * ROW_SUBTILES, LANES), f32),
                        pltpu.VMEM((ROW_TILE * ROW_SUBTILES, LANES), f32),
                        pltpu.SemaphoreType.DMA((2,))],
        compiler_params=_params(),
        name="combine",
    )(run_src, run_n, run_dst, run_src, run_n, run_dst, slots, gates, x1, g_final.reshape(1, -1), ys)


def _plan(meta, counts, t):
    nt = t // ROW_TILE
    n_tiles = (t * TOP_K + N_EXPERTS * (MOE_TILE - 1)) // MOE_TILE
    counts = counts[0, :N_EXPERTS]
    tiles_e = (counts + MOE_TILE - 1) // MOE_TILE
    tile_end = jnp.cumsum(tiles_e)
    tile_start = tile_end - tiles_e
    n_used = tile_end[-1]
    first_row = jnp.pad(tile_start * MOE_TILE, (0, LANES - N_EXPERTS))
    meta = meta.reshape(nt, 8, LANES)
    run_dst = meta[:, 0:1, :] + first_row[None, None, :]
    run_n = meta[:, 1:2, :]
    run_src = meta[:, 2:3, :]
    tile_ids = jnp.arange(n_tiles, dtype=jnp.int32)
    live = jnp.minimum(tile_ids, n_used - 1)
    tile_expert = jnp.sum(tile_end[None, :] <= live[:, None], axis=1).astype(jnp.int32)
    ids = jnp.arange(N_EXPERTS, dtype=jnp.int32)
    is_expert = tile_expert[:, None] == ids[None, :]
    of_tile = lambda per_expert: jnp.sum(jnp.where(is_expert, per_expert[None, :], 0), axis=1)
    in_expert = tile_ids - of_tile(tile_start)
    n_valid = jnp.clip(of_tile(counts) - in_expert * MOE_TILE, 0, MOE_TILE)
    n_valid = jnp.where(tile_ids < n_used, n_valid, 0).astype(jnp.int32)
    tile_first = jnp.logical_and(in_expert == 0, tile_ids < n_used).astype(jnp.int32)
    run_n_flat = run_n[:, 0, :N_EXPERTS].reshape(-1)
    run_loc_flat = run_src[:, 0, :N_EXPERTS].reshape(-1)
    used = tiles_e > 0
    later_used = jnp.where(jnp.logical_and(used[None, :], ids[None, :] > ids[:, None]), ids[None, :],
                           N_EXPERTS)
    following = jnp.min(later_used, axis=1)
    following = jnp.where(following < N_EXPERTS, following, ids)
    buf_of = (jnp.cumsum(used.astype(jnp.int32)) - 1) % 2
    return (tile_expert, n_used.reshape(1).astype(jnp.int32), of_tile(following).astype(jnp.int32),
            of_tile(buf_of).astype(jnp.int32), n_valid, tile_first, run_n_flat, run_loc_flat,
            run_src, run_n, run_dst)


def kernel(x_prompt, x_sample, cache_k, cache_v, g_mix, w_in, b_in, ln_v_g, ln_v_b, w_sp, b_sp,
           attn_sinks, w_pa, w_pb, w_o, g_ffn, w_router, b_router, w_gu, b_gu, w_dn, b_dn, g_final):
    nb, seq, d = x_prompt.shape
    nsb, nnew, _ = x_sample.shape
    tp, ts = nb * seq, nsb * nnew
    t = tp + ts
    xp = x_prompt.reshape(tp, d)
    xs = x_sample.reshape(ts, d)
    pos = np.concatenate([np.arange(seq), np.tile(PAST_LEN + np.arange(nnew), ROW_TILE // nnew)])
    tables = _rope_tables(pos)
    u, vln, v_sgu, q, k, v, kd, vd, ga, gb = _project(
        xp, xs, g_mix[0], w_in[0], b_in[0], ln_v_g[0], ln_v_b[0], tables, seq)
    x1, h_local, slot_t, gate_t, meta, counts = _mix(
        xp, xs, u, vln, q, kd, vd, cache_k[0], cache_v[0], ga, gb, w_sp[0], b_sp[0], attn_sinks[0],
        w_pa[0], w_pb[0], w_o[0], g_ffn[0], w_router[0], b_router[0], seq)
    nt = t // ROW_TILE
    picks = lambda a: a.reshape(nt, 8, ROW_TILE)[:, :TOP_K, :].reshape(nt, 1, TOP_K * ROW_TILE)
    slots, gates = picks(slot_t), picks(gate_t)
    (tile_expert, n_used, next_expert, weight_buf, n_valid, tile_first, run_n_flat, run_loc_flat,
     run_src, run_n, run_dst) = _plan(meta, counts, t)
    y_sorted = _experts(tile_expert, n_used, next_expert, weight_buf, n_valid, tile_first, run_n_flat,
                        run_loc_flat, h_local, w_gu[0], b_gu[0], w_dn[0], b_dn[0])
    y_p, y_s = _combine(run_src, run_n, run_dst, slots, gates, x1, y_sorted, g_final, tp)

    keep = min(WINDOW, seq)
    tails = lambda a: jnp.stack([a[(b + 1) * seq - keep:(b + 1) * seq] for b in range(nb)]).reshape(
        nb, keep, N_KV_HEADS, HEAD_DIM)
    kp, vp = tails(k), tails(v)
    ks = k[tp:].reshape(nsb, nnew, N_KV_HEADS, HEAD_DIM)
    vs = v[tp:].reshape(nsb, nnew, N_KV_HEADS, HEAD_DIM)
    return (y_p.reshape(nb, seq, d), y_s.reshape(nsb, nnew, d), kp[None], vp[None], ks[None],
            vs[None], v_sgu.reshape(1, nsb, nnew, SGU_WIDTH))
```

```python
import functools

import numpy as np
import jax
import jax.numpy as jnp
from jax import lax
from jax.experimental import pallas as pl
from jax.experimental.pallas import tpu as pltpu

D_MODEL = 1024
PAST_LEN = 2048
CHUNK = 64
SGU_CHUNK = 128
SGU_GROUPS = 8
SGU_WIDTH = 1024
N_HEADS = 16
N_KV_HEADS = 4
HEAD_DIM = 64
Q_PER_KV = N_HEADS // N_KV_HEADS
WINDOW = 128
ROT_DIM = HEAD_DIM // 4
ROPE_THETA = 500000.0
ATT_W = N_HEADS * HEAD_DIM
KV_W = N_KV_HEADS * HEAD_DIM
N_EXPERTS = 32
TOP_K = 4
D_FF = 1024
SWIGLU_ALPHA = 1.702
SWIGLU_LIMIT = 7.0
NORM_EPS = 1e-5
NEG_INF = -1e30
N_IN = SGU_WIDTH * 2 + ATT_W + KV_W * 2 + D_MODEL * 2

LANES = 128
ROW_TILE = 256
MOE_TILE = 512
ROW_UNROLL = 8
COL_BLOCK = 256
KV_DUP_W = N_KV_HEADS * LANES
KEY_SPAN = WINDOW + CHUNK
VMEM_LIMIT = 56 * 1024 * 1024

_SQRT_HALF = 0.7071067811865476
_LOG2_E = 1.4426950408889634


def _gelu(x):
    t = 1.0 / (1.0 + (0.3275911 * _SQRT_HALF) * jnp.abs(x))
    half_poly = t * (0.127414796 + t * (-0.142248368 + t * (0.7107068705
                     + t * (-0.7265760135 + t * 0.5307027145))))
    half_tail = x * (half_poly * jnp.exp2(x * x * (-0.5 * _LOG2_E)))
    return jnp.where(x >= 0.0, x - half_tail, half_tail)


def _sigmoid(x):
    return 1.0 / (1.0 + jnp.exp(-x))


def _bf16(x):
    return x.astype(jnp.bfloat16)


def _dot(a, b):
    return jnp.dot(a, b, preferred_element_type=jnp.float32)


ROW_SUBTILES = D_MODEL // LANES


def _store_row_tiled(ref, lead, x):
    rows = x.shape[0]
    for s in range(ROW_SUBTILES):
        ref[(*lead, pl.ds(s, rows, stride=ROW_SUBTILES), slice(None))] = x[:, s * LANES:(s + 1) * LANES]


def _load_row_tiled(ref, lead, rows):
    return jnp.concatenate(
        [ref[(*lead, pl.ds(s, rows, stride=ROW_SUBTILES), slice(None))] for s in range(ROW_SUBTILES)],
        axis=1)


def _dot_nt(a, b):
    return lax.dot_general(a, b, (((1,), (1,)), ((), ())), preferred_element_type=jnp.float32)


def _rms(x, g):
    return x * lax.rsqrt(jnp.mean(x * x, axis=-1, keepdims=True) + NORM_EPS) * g


def _lane_lo(rows):
    return lax.broadcasted_iota(jnp.int32, (rows, LANES), 1) < HEAD_DIM


def _dup_heads(kv):
    rows = kv.shape[0]
    lo = _lane_lo(rows)
    out = []
    for j in range(KV_W // LANES):
        blk = kv[:, j * LANES:(j + 1) * LANES]
        swp = pltpu.roll(blk, HEAD_DIM, axis=1)
        out.append(jnp.where(lo, blk, swp))
        out.append(jnp.where(lo, swp, blk))
    return _bf16(jnp.concatenate(out, axis=1))


def _rope_block(zb, cos_b, sin_lo, sin_hi):
    up = pltpu.roll(zb, LANES - ROT_DIM // 2, axis=1)
    dn = pltpu.roll(zb, ROT_DIM // 2, axis=1)
    return zb * cos_b + up * sin_lo + dn * sin_hi


def _proj_kernel(n_prompt_tiles, xp_ref, xs_ref, gmix_ref, w_ref, b_ref, lng_ref, lnb_ref,
                 cos_ref, slo_ref, shi_ref,
                 u_ref, vln_ref, vs_ref, q_ref, k_ref, v_ref, kd_ref, vd_ref, ga_ref, gb_ref):
    i = pl.program_id(0)
    x = jnp.where(i < n_prompt_tiles, xp_ref[...], xs_ref[...])
    h = _bf16(_rms(x, gmix_ref[...]))

    cos_b, sin_lo, sin_hi = cos_ref[...], slo_ref[...], shi_ref[...]
    off_u, off_v, off_q, off_k, off_vv, off_ga, off_gb = (
        int(o) for o in np.cumsum((0, SGU_WIDTH, SGU_WIDTH, ATT_W, KV_W, KV_W, D_MODEL)))

    def z(lo):
        return _dot(h, w_ref[:, lo:lo + COL_BLOCK]) + b_ref[:, lo:lo + COL_BLOCK]

    def cols(j):
        return slice(j * COL_BLOCK, (j + 1) * COL_BLOCK)

    def rope(zb):
        return jnp.concatenate(
            [_rope_block(zb[:, t * LANES:(t + 1) * LANES], cos_b, sin_lo, sin_hi)
             for t in range(COL_BLOCK // LANES)], axis=1)

    gelu_v = []

    def do_u(j):
        u_ref[:, cols(j)] = _bf16(z(off_u + j * COL_BLOCK))

    def do_v(j):
        gelu_v.append(_gelu(z(off_v + j * COL_BLOCK)))

    def do_q(j):
        q_ref[:, cols(j)] = _bf16(rope(z(off_q + j * COL_BLOCK)) * (HEAD_DIM ** -0.5))

    def do_k(j):
        kr = rope(z(off_k))
        k_ref[...] = kr
        kd_ref[...] = _dup_heads(kr)

    def do_vv(j):
        zv = z(off_vv)
        v_ref[...] = zv
        vd_ref[...] = _dup_heads(zv)

    def do_ga(j):
        ga_ref[:, cols(j)] = _bf16(_sigmoid(z(off_ga + j * COL_BLOCK)))

    def do_gb(j):
        gb_ref[:, cols(j)] = _bf16(_sigmoid(z(off_gb + j * COL_BLOCK)))

    order = ((do_v, 0), (do_q, 0), (do_q, 1), (do_u, 0), (do_v, 1), (do_q, 2), (do_q, 3), (do_u, 1),
             (do_v, 2), (do_k, 0), (do_vv, 0), (do_u, 2), (do_v, 3), (do_ga, 0), (do_ga, 1), (do_u, 3),
             (do_ga, 2), (do_ga, 3), (do_gb, 0), (do_gb, 1), (do_gb, 2), (do_gb, 3))
    for fn, j in order:
        fn(j)
    gv = jnp.concatenate(gelu_v, axis=1)
    gc = gv - jnp.mean(gv, axis=-1, keepdims=True)
    var = jnp.mean(gc * gc, axis=-1, keepdims=True)
    vln = gc * lax.rsqrt(var + NORM_EPS) * lng_ref[...] + lnb_ref[...]
    vln_ref[...] = _bf16(vln)
    vs_ref[...] = vln


def _rope_tables(pos):
    half = ROT_DIM // 2
    inv = np.float32(ROPE_THETA) ** (-np.arange(half, dtype=np.float32) * np.float32(2.0) / ROT_DIM)
    ang = pos.astype(np.float32)[:, None] * inv.astype(np.float32)[None, :]
    cos = np.cos(ang.astype(np.float64)).astype(np.float32)
    sin = np.sin(ang.astype(np.float64)).astype(np.float32)
    n = pos.shape[0]
    ones = np.ones((n, HEAD_DIM - ROT_DIM), np.float32)
    zeros = np.zeros((n, HEAD_DIM - ROT_DIM), np.float32)
    zh = np.zeros((n, half), np.float32)
    cos_h = np.concatenate([cos, cos, ones], axis=1)
    slo_h = np.concatenate([-sin, zh, zeros], axis=1)
    shi_h = np.concatenate([zh, sin, zeros], axis=1)
    rep = LANES // HEAD_DIM
    return tuple(jnp.asarray(np.tile(a, (1, rep))) for a in (cos_h, slo_h, shi_h))


def _row_spec(width):
    return pl.BlockSpec((ROW_TILE, width), lambda i: (i, 0))


def _const_spec(shape):
    return pl.BlockSpec(shape, lambda i: (0,) * len(shape))


def _prompt_spec(width, n_prompt_tiles):
    return pl.BlockSpec((ROW_TILE, width), lambda i: (jnp.minimum(i, n_prompt_tiles - 1), 0))


def _sample_spec(width, n_prompt_tiles):
    return pl.BlockSpec((ROW_TILE, width), lambda i: (jnp.maximum(i - n_prompt_tiles, 0), 0))


def _params():
    return pltpu.CompilerParams(dimension_semantics=("arbitrary",), vmem_limit_bytes=VMEM_LIMIT)


def _project(xp, xs, g_mix, w_in, b_in, ln_g, ln_b, tables, seq):
    tp, ts = xp.shape[0], xs.shape[0]
    t = tp + ts
    npt = tp // ROW_TILE
    tiles_per_seq = seq // ROW_TILE
    f32, bf16 = jnp.float32, jnp.bfloat16
    table_spec = pl.BlockSpec(
        (ROW_TILE, LANES), lambda i: (jnp.where(i < npt, i % tiles_per_seq, tiles_per_seq), 0))
    out_shape = (
        jax.ShapeDtypeStruct((t, SGU_WIDTH), bf16),
        jax.ShapeDtypeStruct((t, SGU_WIDTH), bf16),
        jax.ShapeDtypeStruct((ts, SGU_WIDTH), f32),
        jax.ShapeDtypeStruct((t, ATT_W), bf16),
        jax.ShapeDtypeStruct((t, KV_W), f32),
        jax.ShapeDtypeStruct((t, KV_W), f32),
        jax.ShapeDtypeStruct((t, KV_DUP_W), bf16),
        jax.ShapeDtypeStruct((t, KV_DUP_W), bf16),
        jax.ShapeDtypeStruct((t, D_MODEL), bf16),
        jax.ShapeDtypeStruct((t, D_MODEL), bf16),
    )
    return pl.pallas_call(
        functools.partial(_proj_kernel, npt),
        out_shape=out_shape,
        grid=(t // ROW_TILE,),
        in_specs=[
            _prompt_spec(D_MODEL, npt), _sample_spec(D_MODEL, npt),
            _const_spec((1, D_MODEL)), _const_spec((D_MODEL, N_IN)),
            _const_spec((1, N_IN)), _const_spec((1, SGU_WIDTH)), _const_spec((1, SGU_WIDTH)),
            table_spec, table_spec, table_spec,
        ],
        out_specs=(
            _row_spec(SGU_WIDTH), _row_spec(SGU_WIDTH), _sample_spec(SGU_WIDTH, npt),
            _row_spec(ATT_W), _row_spec(KV_W), _row_spec(KV_W), _row_spec(KV_DUP_W),
            _row_spec(KV_DUP_W), _row_spec(D_MODEL), _row_spec(D_MODEL),
        ),
        compiler_params=_params(),
        name="proj",
    )(xp, xs, g_mix.reshape(1, -1), w_in.astype(bf16), b_in.reshape(1, -1),
      ln_g.reshape(1, -1), ln_b.reshape(1, -1), *tables)


def _attend(qa, qb, kwin, vwin, sink, valid):
    lo = _lane_lo(CHUNK)
    zero = jnp.zeros_like(qa)
    lhs = jnp.concatenate([jnp.where(lo, qa, zero), jnp.where(lo, zero, qa),
                           jnp.where(lo, qb, zero), jnp.where(lo, zero, qb)], axis=0)
    s = _dot_nt(lhs, kwin)
    if valid is not None:
        s = jnp.where(valid, s, NEG_INF)
    s_a, s_b = s[:, :LANES], s[:, LANES:]
    tail = s_b.shape[1]
    m = jnp.maximum(jnp.max(s, axis=-1, keepdims=True), sink)
    p_a = jnp.exp(s_a - m)
    p_b = jnp.exp(s_b - m[:, :tail])
    denom = (jnp.sum(jnp.concatenate([p_a, p_b], axis=1), axis=-1, keepdims=True)
             + jnp.exp(sink - m))
    inv = 1.0 / denom
    pn = jnp.concatenate([p_a * inv, p_b * inv[:, :tail]], axis=1)
    r = _dot(_bf16(pn), vwin)
    oa = jnp.where(lo, r[0:CHUNK], r[CHUNK:2 * CHUNK])
    ob = jnp.where(lo, r[2 * CHUNK:3 * CHUNK], r[3 * CHUNK:4 * CHUNK])
    return oa, ob


def _stack_rows(rows):
    ri = lax.broadcasted_iota(jnp.int32, (8, rows[0].shape[1]), 0)
    out = jnp.zeros((8, rows[0].shape[1]), rows[0].dtype)
    for k, row in enumerate(rows):
        out = jnp.where(ri == k, row, out)
    return out


def _route_pick(logits_t):
    rows = logits_t.shape[1]
    eid = lax.broadcasted_iota(jnp.int32, (N_EXPERTS, rows), 0)
    work = logits_t
    vals, idxs = [], []
    for _ in range(TOP_K):
        m = jnp.max(work, axis=0, keepdims=True)
        idx = jnp.min(jnp.where(work == m, eid, N_EXPERTS), axis=0, keepdims=True)
        vals.append(m)
        idxs.append(idx)
        work = jnp.where(eid == idx, -jnp.inf, work)
    exps = [jnp.exp(v - vals[0]) for v in vals]
    inv = 1.0 / (exps[0] + exps[1] + exps[2] + exps[3])
    gates = _stack_rows([e * inv for e in exps])

    picked = jnp.zeros((N_EXPERTS, rows), jnp.float32)
    for idx in idxs:
        picked = jnp.where(eid == idx, 1.0, picked)
    tr = lax.broadcasted_iota(jnp.int32, (rows, rows), 0)
    tc = lax.broadcasted_iota(jnp.int32, (rows, rows), 1)
    earlier = _bf16(jnp.where(tr < tc, 1.0, 0.0))
    in_tile = _dot(_bf16(picked), earlier)
    count_col = jnp.broadcast_to(jnp.sum(picked, axis=1, keepdims=True), (N_EXPERTS, LANES))
    eid_wide = lax.broadcasted_iota(jnp.int32, (LANES, rows), 0)
    picked_wide = jnp.zeros((LANES, rows), jnp.float32)
    for idx in idxs:
        picked_wide = jnp.where(eid_wide == idx, 1.0, picked_wide)
    count_row = _dot_nt(jnp.ones((8, rows), jnp.bfloat16), _bf16(picked_wide))
    return idxs, gates, in_tile, count_col, count_row


def _route_place(idxs, in_tile, count_col, count_row, carry_s, live):
    rows = in_tile.shape[1]
    eid = lax.broadcasted_iota(jnp.int32, (N_EXPERTS, rows), 0)
    er = lax.broadcasted_iota(jnp.int32, (N_EXPERTS, N_EXPERTS), 0)
    ec = lax.broadcasted_iota(jnp.int32, (N_EXPERTS, N_EXPERTS), 1)
    start_col = _dot(_bf16(jnp.where(ec < er, 1.0, 0.0)), _bf16(count_col))
    local = in_tile + jnp.concatenate([start_col] * (rows // LANES), axis=1)
    slots = _stack_rows([jnp.sum(jnp.where(eid == idx, local, 0.0), axis=0, keepdims=True)
                         for idx in idxs]).astype(jnp.int32)
    lr = lax.broadcasted_iota(jnp.int32, (LANES, LANES), 0)
    lc = lax.broadcasted_iota(jnp.int32, (LANES, LANES), 1)
    start_row = _dot(_bf16(count_row), _bf16(jnp.where(lr < lc, 1.0, 0.0)))
    ri = lax.broadcasted_iota(jnp.int32, (8, LANES), 0)
    meta = jnp.where(ri == 0, carry_s[...], jnp.where(ri == 1, count_row, jnp.where(ri == 2, start_row, 0.0)))
    carry_s[...] = carry_s[...] + count_row * live
    return slots, meta.astype(jnp.int32)


def _mix_kernel(tiles_per_seq, n_prompt_tiles,
                xp_ref, xs_ref, u_ref, vln_ref, q_ref, kd_ref, vd_ref, kdp_ref, vdp_ref,
                ck_ref, cv_ref, ga_ref, gb_ref, wsp_ref, bsp_ref, sink_ref,
                wpa_ref, wpb_ref, wo_ref, gffn_ref, wrh_ref, wrl_ref, br_ref,
                x1_ref, hloc_ref, slot_ref, gate_ref, meta_ref, count_ref,
                a_s, o_s, kwin_s, vwin_s, carry_s, hhi_s, hlo_s):
    i = pl.program_id(0)
    n_streams = ROW_TILE // CHUNK

    @pl.when(i == 0)
    def _():
        carry_s[...] = jnp.zeros_like(carry_s)
        hhi_s[...] = jnp.zeros_like(hhi_s)
        hlo_s[...] = jnp.zeros_like(hlo_s)

    def sgu_rows(r0, rows):
        ri = lax.broadcasted_iota(jnp.int32, (rows, rows), 0) // CHUNK
        ci = lax.broadcasted_iota(jnp.int32, (rows, rows), 1) // CHUNK
        for g in range(SGU_GROUPS):
            cols = slice(g * LANES, (g + 1) * LANES)
            w = _bf16(jnp.where(ci <= ri, wsp_ref[g, :rows, :rows], 0.0))
            sp = _dot(w, vln_ref[r0:r0 + rows, cols]) + bsp_ref[g, :rows, :]
            a_s[r0:r0 + rows, cols] = _bf16(_gelu(u_ref[r0:r0 + rows, cols].astype(jnp.float32)) * sp)

    def attend_rows(r0, kwin_of, valid):
        for g in range(N_KV_HEADS):
            c0 = g * Q_PER_KV * HEAD_DIM
            kwin, vwin = kwin_of(g)
            oa, ob = _attend(q_ref[r0:r0 + CHUNK, c0:c0 + LANES],
                             q_ref[r0:r0 + CHUNK, c0 + LANES:c0 + 2 * LANES],
                             kwin, vwin, sink_ref[g], valid)
            o_s[r0:r0 + CHUNK, c0:c0 + LANES] = _bf16(oa)
            o_s[r0:r0 + CHUNK, c0 + LANES:c0 + 2 * LANES] = _bf16(ob)

    @pl.when(i < n_prompt_tiles)
    def _prompt():
        for c in range(ROW_TILE // SGU_CHUNK):
            sgu_rows(c * SGU_CHUNK, SGU_CHUNK)
        kwin_s[0:WINDOW] = kdp_ref[...]
        kwin_s[WINDOW:WINDOW + ROW_TILE] = kd_ref[...]
        vwin_s[0:WINDOW] = vdp_ref[...]
        vwin_s[WINDOW:WINDOW + ROW_TILE] = vd_ref[...]
        first = (i % tiles_per_seq) == 0
        col = lax.broadcasted_iota(jnp.int32, (1, KEY_SPAN), 1)
        for j in range(ROW_TILE // CHUNK):
            r0 = j * CHUNK
            valid = jnp.logical_or(jnp.logical_not(first), col + r0 >= WINDOW) if r0 < WINDOW else None

            def kwin_of(g, r0=r0):
                cols = slice(g * LANES, (g + 1) * LANES)
                return kwin_s[r0:r0 + KEY_SPAN, cols], vwin_s[r0:r0 + KEY_SPAN, cols]

            attend_rows(r0, kwin_of, valid)

    @pl.when(i >= n_prompt_tiles)
    def _sample():
        for s in range(n_streams):
            r0 = s * CHUNK
            sgu_rows(r0, CHUNK)
            kwin_s[0:WINDOW] = _dup_heads(ck_ref[s])
            kwin_s[WINDOW:KEY_SPAN] = kd_ref[r0:r0 + CHUNK]
            vwin_s[0:WINDOW] = _dup_heads(cv_ref[s])
            vwin_s[WINDOW:KEY_SPAN] = vd_ref[r0:r0 + CHUNK]

            def kwin_of(g):
                cols = slice(g * LANES, (g + 1) * LANES)
                return kwin_s[0:KEY_SPAN, cols], vwin_s[0:KEY_SPAN, cols]

            attend_rows(r0, kwin_of, None)

    hh, hl = hhi_s[...], hlo_s[...]
    logits_t = (_dot_nt(wrh_ref[...], hh) + _dot_nt(wrl_ref[...], hh) + _dot_nt(wrh_ref[...], hl)
                + jnp.concatenate([br_ref[...]] * (ROW_TILE // LANES), axis=1))
    m_a = ga_ref[...].astype(jnp.float32) * _dot(a_s[...], wpa_ref[...])
    idxs, gates, in_tile, count_col, count_row = _route_pick(logits_t)
    gate_ref[...] = gates
    m = m_a + gb_ref[...].astype(jnp.float32) * _dot(o_s[...], wpb_ref[...])
    slots, meta = _route_place(idxs, in_tile, count_col, count_row, carry_s, jnp.where(i > 0, 1.0, 0.0))
    slot_ref[...] = slots * ROW_SUBTILES
    meta_ref[...] = meta
    count_ref[...] = carry_s[...].astype(jnp.int32)
    n_slots = ROW_TILE * TOP_K
    sid = lax.broadcasted_iota(jnp.int32, (n_slots, ROW_TILE), 0)
    place = jnp.zeros((n_slots, ROW_TILE), jnp.float32)
    for k in range(TOP_K):
        place = jnp.where(sid == slots[k:k + 1, :], 1.0, place)
    _store_row_tiled(hloc_ref, (), _dot(_bf16(place), hh))

    x = jnp.where(i < n_prompt_tiles, xp_ref[...], xs_ref[...])
    x1 = x + _dot(_bf16(m), wo_ref[...])
    x1_ref[...] = x1
    h2 = _rms(x1, gffn_ref[...])
    h2_hi = _bf16(h2)
    hhi_s[...] = h2_hi
    hlo_s[...] = _bf16(h2 - h2_hi.astype(jnp.float32))


def _mix(xp, xs, u, vln, q, kd, vd, cache_k, cache_v, ga, gb, w_sp, b_sp, sinks,
         w_pa, w_pb, w_o, g_ffn, w_router, b_router, seq):
    tp, ts = xp.shape[0], xs.shape[0]
    t = tp + ts
    npt = tp // ROW_TILE
    tiles_per_seq = seq // ROW_TILE
    f32, bf16 = jnp.float32, jnp.bfloat16
    n_streams = ROW_TILE // CHUNK
    win_per_tile = ROW_TILE // WINDOW

    nt = t // ROW_TILE
    cur = lambda i: jnp.minimum(i, nt - 1)
    smp = lambda i: jnp.maximum(cur(i) - npt, 0)
    row = lambda width: pl.BlockSpec((ROW_TILE, width), lambda i: (cur(i), 0))
    prev_spec = pl.BlockSpec(
        (WINDOW, KV_DUP_W), lambda i: (jnp.maximum(jnp.minimum(i, npt - 1) * win_per_tile - 1, 0), 0))
    cache_spec = pl.BlockSpec((n_streams, WINDOW, KV_W), lambda i: (smp(i), 0, 0))
    xs_spec = pl.BlockSpec((ROW_TILE, D_MODEL), lambda i: (smp(i), 0))
    sink_cols = jnp.broadcast_to(
        jnp.repeat(sinks.astype(f32).reshape(N_KV_HEADS, Q_PER_KV), CHUNK, axis=1)[:, :, None],
        (N_KV_HEADS, Q_PER_KV * CHUNK, LANES))
    wr_t = w_router.T
    wr_hi = wr_t.astype(bf16)
    wr_lo = (wr_t - wr_hi.astype(f32)).astype(bf16)
    routed8 = lambda width: pl.BlockSpec((8, width), lambda i: (jnp.maximum(i - 1, 0), 0))
    out_shape = (
        jax.ShapeDtypeStruct((t, D_MODEL), f32),
        jax.ShapeDtypeStruct((t * TOP_K * ROW_SUBTILES, LANES), f32),
        jax.ShapeDtypeStruct((nt * 8, ROW_TILE), jnp.int32),
        jax.ShapeDtypeStruct((nt * 8, ROW_TILE), f32),
        jax.ShapeDtypeStruct((nt * 8, LANES), jnp.int32),
        jax.ShapeDtypeStruct((8, LANES), jnp.int32),
    )
    return pl.pallas_call(
        functools.partial(_mix_kernel, tiles_per_seq, npt),
        out_shape=out_shape,
        grid=(nt + 1,),
        in_specs=[
            _prompt_spec(D_MODEL, npt), xs_spec,
            row(SGU_WIDTH), row(SGU_WIDTH), row(ATT_W),
            row(KV_DUP_W), row(KV_DUP_W), prev_spec, prev_spec,
            cache_spec, cache_spec, row(D_MODEL), row(D_MODEL),
            _const_spec((SGU_GROUPS, SGU_CHUNK, SGU_CHUNK)), _const_spec((SGU_GROUPS, SGU_CHUNK, LANES)),
            _const_spec((N_KV_HEADS, Q_PER_KV * CHUNK, LANES)),
            _const_spec((SGU_WIDTH, D_MODEL)), _const_spec((ATT_W, D_MODEL)),
            _const_spec((D_MODEL, D_MODEL)), _const_spec((1, D_MODEL)),
            _const_spec((N_EXPERTS, D_MODEL)), _const_spec((N_EXPERTS, D_MODEL)),
            _const_spec((N_EXPERTS, LANES)),
        ],
        out_specs=(row(D_MODEL),
                   pl.BlockSpec((ROW_TILE * TOP_K * ROW_SUBTILES, LANES),
                                lambda i: (jnp.maximum(i - 1, 0), 0)),
                   routed8(ROW_TILE), routed8(ROW_TILE), routed8(LANES), _const_spec((8, LANES))),
        scratch_shapes=[
            pltpu.VMEM((ROW_TILE, SGU_WIDTH), bf16), pltpu.VMEM((ROW_TILE, ATT_W), bf16),
            pltpu.VMEM((WINDOW + ROW_TILE, KV_DUP_W), bf16),
            pltpu.VMEM((WINDOW + ROW_TILE, KV_DUP_W), bf16),
            pltpu.VMEM((8, LANES), f32),
            pltpu.VMEM((ROW_TILE, D_MODEL), bf16), pltpu.VMEM((ROW_TILE, D_MODEL), bf16),
        ],
        compiler_params=_params(),
        name="mix",
    )(xp, xs, u, vln, q, kd, vd, kd, vd,
      cache_k.reshape(-1, WINDOW, KV_W), cache_v.reshape(-1, WINDOW, KV_W), ga, gb,
      w_sp, jnp.broadcast_to(b_sp[:, :, None], (SGU_GROUPS, SGU_CHUNK, LANES)), sink_cols,
      w_pa.astype(bf16), w_pb.astype(bf16), w_o.astype(bf16),
      g_ffn.reshape(1, -1), wr_hi, wr_lo,
      jnp.broadcast_to(b_router.astype(f32)[:, None], (N_EXPERTS, LANES)))


def _unrolled_rows(n_rows, fn):
    if isinstance(n_rows, int):
        groups, tail_start = n_rows // ROW_UNROLL, n_rows - n_rows % ROW_UNROLL
    else:
        groups = lax.shift_right_logical(n_rows, ROW_UNROLL.bit_length() - 1)
        tail_start = groups * ROW_UNROLL

    def group(gi, carry):
        for lane in range(ROW_UNROLL):
            fn(gi * ROW_UNROLL + lane, lane)
        return carry

    def tail(r, carry):
        fn(r, 0)
        return carry

    lax.fori_loop(0, groups, group, 0)
    lax.fori_loop(tail_start, n_rows, tail, 0)


def _row_span(first_row, n_rows):
    return pl.ds(pl.multiple_of(first_row * ROW_SUBTILES, ROW_SUBTILES),
                 pl.multiple_of(n_rows * ROW_SUBTILES, ROW_SUBTILES))


def _run_spec(index_of):
    return pl.BlockSpec((1, 1, LANES), lambda i, *_: (index_of(i), 0, 0), memory_space=pltpu.SMEM)


def _expert_kernel(n_token_tiles,
                   te_ref, nu_ref, nx_ref, par_ref, nv_ref, tf_ref, cnt_ref, loc_ref,
                   bgu_ref, bdn_ref, hloc_hbm, wgu_hbm, wdn_hbm,
                   ys_ref,
                   xbuf, wgu_f, wdn_f, wgu_s, wdn_s, walk, xsem, wsem):
    i = pl.program_id(0)
    n_used = nu_ref[0]
    expert = te_ref[i]
    buf = par_ref[i]
    slot = i % 2
    expert_changed = jnp.logical_or(i == 0, expert != te_ref[jnp.maximum(i - 1, 0)])

    def fetch_rows(j, b):
        e = te_ref[j]
        need = nv_ref[j]

        @pl.when(tf_ref[j] == 1)
        def _():
            walk[0] = 0
            walk[1] = 0

        @pl.when(need < MOE_TILE)
        def _():
            xbuf[b] = jnp.zeros(xbuf.shape[1:], xbuf.dtype)

        def unfinished(state):
            filled, tile, _ = state
            return jnp.logical_and(filled < need, tile < n_token_tiles)

        def take_run(state):
            filled, tile, off = state
            run = cnt_ref[tile * N_EXPERTS + e]
            take = jnp.minimum(run - off, need - filled)

            @pl.when(take > 0)
            def _():
                src = tile * (ROW_TILE * TOP_K) + loc_ref[tile * N_EXPERTS + e] + off
                pltpu.make_async_copy(hloc_hbm.at[_row_span(src, take)],
                                      xbuf.at[b, _row_span(filled, take)], xsem.at[b]).start()

            run_done = off + take == run
            return (filled + take, jnp.where(run_done, tile + 1, tile), jnp.where(run_done, 0, off + take))

        _, tile, off = lax.while_loop(unfinished, take_run, (jnp.int32(0), walk[0], walk[1]))
        walk[0] = tile
        walk[1] = off

    @pl.when(jnp.logical_and(i == 0, n_used > 0))
    def _():
        fetch_rows(0, 0)

    @pl.when(i + 1 < n_used)
    def _():
        fetch_rows(i + 1, 1 - slot)

    def weight_copies(e, b):
        return (pltpu.make_async_copy(wgu_hbm.at[e], wgu_f.at[b], wsem.at[0, b]),
                pltpu.make_async_copy(wdn_hbm.at[e], wdn_f.at[b], wsem.at[1, b]))

    @pl.when(jnp.logical_and(i < n_used, expert_changed))
    def _():
        @pl.when(i == 0)
        def _():
            for copy in weight_copies(expert, buf):
                copy.start()

        for copy in weight_copies(expert, buf):
            copy.wait()
        following = nx_ref[i]

        @pl.when(following != expert)
        def _():
            for copy in weight_copies(following, 1 - buf):
                copy.start()

        wgu_s[...] = _bf16(wgu_f[buf])
        wdn_s[...] = _bf16(wdn_f[buf])

    @pl.when(i < n_used)
    def _():
        rows = nv_ref[i]
        pltpu.make_async_copy(hloc_hbm.at[_row_span(0, rows)], xbuf.at[slot, _row_span(0, rows)],
                              xsem.at[slot]).wait()
        x = _bf16(_load_row_tiled(xbuf, (slot,), MOE_TILE))
        gu = _dot(x, wgu_s[...]) + bgu_ref[0]
        gate = jnp.minimum(gu[:, :D_FF], SWIGLU_LIMIT)
        lin = jnp.clip(gu[:, D_FF:], -SWIGLU_LIMIT, SWIGLU_LIMIT)
        act = gate * _sigmoid(SWIGLU_ALPHA * gate) * (lin + 1.0)
        _store_row_tiled(ys_ref, (), _dot(_bf16(act), wdn_s[...]) + bdn_ref[0])

    @pl.when(i >= n_used)
    def _():
        ys_ref[...] = jnp.zeros_like(ys_ref)


def _experts(tile_expert, n_used, next_expert, weight_buf, n_valid, tile_first, run_n, run_loc,
             h_local, w_gu, b_gu, w_dn, b_dn):
    n_tiles = tile_expert.shape[0]
    n_token_tiles = h_local.shape[0] // (ROW_TILE * TOP_K * ROW_SUBTILES)
    f32, bf16 = jnp.float32, jnp.bfloat16
    tile_rows = MOE_TILE * ROW_SUBTILES
    grid_spec = pltpu.PrefetchScalarGridSpec(
        num_scalar_prefetch=8,
        grid=(n_tiles,),
        in_specs=[
            pl.BlockSpec((1, 1, 2 * D_FF), lambda i, te, *_: (te[i], 0, 0)),
            pl.BlockSpec((1, 1, D_MODEL), lambda i, te, *_: (te[i], 0, 0)),
            pl.BlockSpec(memory_space=pl.ANY), pl.BlockSpec(memory_space=pl.ANY),
            pl.BlockSpec(memory_space=pl.ANY),
        ],
        out_specs=pl.BlockSpec((tile_rows, LANES), lambda i, *_: (i, 0)),
        scratch_shapes=[
            pltpu.VMEM((2, tile_rows, LANES), f32),
            pltpu.VMEM((2, D_MODEL, 2 * D_FF), f32), pltpu.VMEM((2, D_FF, D_MODEL), f32),
            pltpu.VMEM((D_MODEL, 2 * D_FF), bf16), pltpu.VMEM((D_FF, D_MODEL), bf16),
            pltpu.SMEM((2,), jnp.int32),
            pltpu.SemaphoreType.DMA((2,)), pltpu.SemaphoreType.DMA((2, 2)),
        ],
    )
    return pl.pallas_call(
        functools.partial(_expert_kernel, n_token_tiles),
        out_shape=jax.ShapeDtypeStruct((n_tiles * tile_rows, LANES), f32),
        grid_spec=grid_spec,
        compiler_params=_params(),
        name="experts",
    )(tile_expert, n_used, next_expert, weight_buf, n_valid, tile_first, run_n, run_loc,
      b_gu.reshape(N_EXPERTS, 1, -1), b_dn.reshape(N_EXPERTS, 1, -1), h_local, w_gu, w_dn)


def _combine_kernel(n_prompt_tiles,
                    src_ref, n_ref, dst_ref, src_nx_ref, n_nx_ref, dst_nx_ref, slot_ref, gate_ref,
                    x1_ref, gfin_ref, ys_hbm,
                    yp_ref, yo_ref,
                    local, mixed, run_sem):
    i = pl.program_id(0)
    last = pl.num_programs(0) - 1
    buf = i % 2

    def fetch_runs(s_ref, c_ref, d_ref, b):
        for e in range(N_EXPERTS):
            n = c_ref[0, 0, e]
            copy = pltpu.make_async_copy(ys_hbm.at[_row_span(d_ref[0, 0, e], n)],
                                         local.at[b, _row_span(s_ref[0, 0, e], n)], run_sem.at[b])
            pl.when(n > 0)(copy.start)

    @pl.when(i == 0)
    def _():
        fetch_runs(src_ref, n_ref, dst_ref, 0)

    @pl.when(i < last)
    def _():
        fetch_runs(src_nx_ref, n_nx_ref, dst_nx_ref, 1 - buf)

    pltpu.make_async_copy(ys_hbm.at[pl.ds(0, local.shape[1])], local.at[buf], run_sem.at[buf]).wait()

    def blend(t, lane):
        acc = None
        for k in range(TOP_K):
            at = pl.multiple_of(slot_ref[0, 0, k * ROW_TILE + t], ROW_SUBTILES)
            term = gate_ref[0, 0, k * ROW_TILE + t] * local[buf, pl.ds(at, ROW_SUBTILES), :]
            acc = term if acc is None else acc + term
        mixed[pl.ds(pl.multiple_of(t * ROW_SUBTILES, ROW_SUBTILES), ROW_SUBTILES), :] = acc
    _unrolled_rows(ROW_TILE, blend)

    out = _rms(x1_ref[...] + _load_row_tiled(mixed, (), ROW_TILE), gfin_ref[...])

    @pl.when(i < n_prompt_tiles)
    def _():
        yp_ref[...] = out

    @pl.when(i >= n_prompt_tiles)
    def _():
        yo_ref[...] = out


def _combine(run_src, run_n, run_dst, slots, gates, x1, ys, g_final, tp):
    t = x1.shape[0]
    npt = tp // ROW_TILE
    nt = t // ROW_TILE
    f32 = jnp.float32
    picks = ROW_TILE * TOP_K
    nxt = lambda i: jnp.minimum(i + 1, nt - 1)
    pick_spec = pl.BlockSpec((1, 1, picks), lambda i: (i, 0, 0), memory_space=pltpu.SMEM)
    return pl.pallas_call(
        functools.partial(_combine_kernel, npt),
        out_shape=(jax.ShapeDtypeStruct((tp, D_MODEL), f32),
                   jax.ShapeDtypeStruct((t - tp, D_MODEL), f32)),
        grid=(nt,),
        in_specs=[_run_spec(lambda i: i), _run_spec(lambda i: i), _run_spec(lambda i: i),
                  _run_spec(nxt), _run_spec(nxt), _run_spec(nxt), pick_spec, pick_spec,
                  _row_spec(D_MODEL), _const_spec((1, D_MODEL)), pl.BlockSpec(memory_space=pl.ANY)],
        out_specs=(_prompt_spec(D_MODEL, npt), _sample_spec(D_MODEL, npt)),
        scratch_shapes=[pltpu.VMEM((2, picks * ROW_SUBTILES, LANES), f32),
                        pltpu.VMEM((ROW_TILE * ROW_SUBTILES, LANES), f32),
                        pltpu.SemaphoreType.DMA((2,))],
        compiler_params=_params(),
        name="combine",
    )(run_src, run_n, run_dst, run_src, run_n, run_dst, slots, gates, x1, g_final.reshape(1, -1), ys)


def _plan(meta, counts, t):
    nt = t // ROW_TILE
    n_tiles = (t * TOP_K + N_EXPERTS * (MOE_TILE - 1)) // MOE_TILE
    counts = counts[0, :N_EXPERTS]
    tiles_e = (counts + MOE_TILE - 1) // MOE_TILE
    tile_end = jnp.cumsum(tiles_e)
    tile_start = tile_end - tiles_e
    n_used = tile_end[-1]
    first_row = jnp.pad(tile_start * MOE_TILE, (0, LANES - N_EXPERTS))
    meta = meta.reshape(nt, 8, LANES)
    run_dst = meta[:, 0:1, :] + first_row[None, None, :]
    run_n = meta[:, 1:2, :]
    run_src = meta[:, 2:3, :]
    tile_ids = jnp.arange(n_tiles, dtype=jnp.int32)
    live = jnp.minimum(tile_ids, n_used - 1)
    tile_expert = jnp.sum(tile_end[None, :] <= live[:, None], axis=1).astype(jnp.int32)
    ids = jnp.arange(N_EXPERTS, dtype=jnp.int32)
    is_expert = tile_expert[:, None] == ids[None, :]
    of_tile = lambda per_expert: jnp.sum(jnp.where(is_expert, per_expert[None, :], 0), axis=1)
    in_expert = tile_ids - of_tile(tile_start)
    n_valid = jnp.clip(of_tile(counts) - in_expert * MOE_TILE, 0, MOE_TILE)
    n_valid = jnp.where(tile_ids < n_used, n_valid, 0).astype(jnp.int32)
    tile_first = jnp.logical_and(in_expert == 0, tile_ids < n_used).astype(jnp.int32)
    run_n_flat = run_n[:, 0, :N_EXPERTS].reshape(-1)
    run_loc_flat = run_src[:, 0, :N_EXPERTS].reshape(-1)
    used = tiles_e > 0
    later_used = jnp.where(jnp.logical_and(used[None, :], ids[None, :] > ids[:, None]), ids[None, :],
                           N_EXPERTS)
    following = jnp.min(later_used, axis=1)
    following = jnp.where(following < N_EXPERTS, following, ids)
    buf_of = (jnp.cumsum(used.astype(jnp.int32)) - 1) % 2
    return (tile_expert, n_used.reshape(1).astype(jnp.int32), of_tile(following).astype(jnp.int32),
            of_tile(buf_of).astype(jnp.int32), n_valid, tile_first, run_n_flat, run_loc_flat,
            run_src, run_n, run_dst)


def kernel(x_prompt, x_sample, cache_k, cache_v, g_mix, w_in, b_in, ln_v_g, ln_v_b, w_sp, b_sp,
           attn_sinks, w_pa, w_pb, w_o, g_ffn, w_router, b_router, w_gu, b_gu, w_dn, b_dn, g_final):
    nb, seq, d = x_prompt.shape
    nsb, nnew, _ = x_sample.shape
    tp, ts = nb * seq, nsb * nnew
    t = tp + ts
    xp = x_prompt.reshape(tp, d)
    xs = x_sample.reshape(ts, d)
    pos = np.concatenate([np.arange(seq), np.tile(PAST_LEN + np.arange(nnew), ROW_TILE // nnew)])
    tables = _rope_tables(pos)
    u, vln, v_sgu, q, k, v, kd, vd, ga, gb = _project(
        xp, xs, g_mix[0], w_in[0], b_in[0], ln_v_g[0], ln_v_b[0], tables, seq)
    x1, h_local, slot_t, gate_t, meta, counts = _mix(
        xp, xs, u, vln, q, kd, vd, cache_k[0], cache_v[0], ga, gb, w_sp[0], b_sp[0], attn_sinks[0],
        w_pa[0], w_pb[0], w_o[0], g_ffn[0], w_router[0], b_router[0], seq)
    nt = t // ROW_TILE
    picks = lambda a: a.reshape(nt, 8, ROW_TILE)[:, :TOP_K, :].reshape(nt, 1, TOP_K * ROW_TILE)
    slots, gates = picks(slot_t), picks(gate_t)
    (tile_expert, n_used, next_expert, weight_buf, n_valid, tile_first, run_n_flat, run_loc_flat,
     run_src, run_n, run_dst) = _plan(meta, counts, t)
    y_sorted = _experts(tile_expert, n_used, next_expert, weight_buf, n_valid, tile_first, run_n_flat,
                        run_loc_flat, h_local, w_gu[0], b_gu[0], w_dn[0], b_dn[0])
    y_p, y_s = _combine(run_src, run_n, run_dst, slots, gates, x1, y_sorted, g_final, tp)

    keep = min(WINDOW, seq)
    tails = lambda a: jnp.stack([a[(b + 1) * seq - keep:(b + 1) * seq] for b in range(nb)]).reshape(
        nb, keep, N_KV_HEADS, HEAD_DIM)
    kp, vp = tails(k), tails(v)
    ks = k[tp:].reshape(nsb, nnew, N_KV_HEADS, HEAD_DIM)
    vs = v[tp:].reshape(nsb, nnew, N_KV_HEADS, HEAD_DIM)
    return (y_p.reshape(nb, seq, d), y_s.reshape(nsb, nnew, d), kp[None], vp[None], ks[None],
            vs[None], v_sgu.reshape(1, nsb, nnew, SGU_WIDTH))
```

```python
import functools

import numpy as np
import jax
import jax.numpy as jnp
from jax import lax
from jax.experimental import pallas as pl
from jax.experimental.pallas import tpu as pltpu

D_MODEL = 1024
PAST_LEN = 2048
CHUNK = 64
SGU_CHUNK = 128
SGU_GROUPS = 8
SGU_WIDTH = 1024
N_HEADS = 16
N_KV_HEADS = 4
HEAD_DIM = 64
Q_PER_KV = N_HEADS // N_KV_HEADS
WINDOW = 128
ROT_DIM = HEAD_DIM // 4
ROPE_THETA = 500000.0
ATT_W = N_HEADS * HEAD_DIM
KV_W = N_KV_HEADS * HEAD_DIM
N_EXPERTS = 32
TOP_K = 4
D_FF = 1024
SWIGLU_ALPHA = 1.702
SWIGLU_LIMIT = 7.0
NORM_EPS = 1e-5
NEG_INF = -1e30
N_IN = SGU_WIDTH * 2 + ATT_W + KV_W * 2 + D_MODEL * 2

LANES = 128
ROW_TILE = 256
MOE_TILE = 512
ROW_UNROLL = 8
COL_BLOCK = 256
KV_DUP_W = N_KV_HEADS * LANES
KEY_SPAN = WINDOW + CHUNK
VMEM_LIMIT = 56 * 1024 * 1024

_SQRT_HALF = 0.7071067811865476
_LOG2_E = 1.4426950408889634


def _gelu(x):
    t = 1.0 / (1.0 + (0.3275911 * _SQRT_HALF) * jnp.abs(x))
    half_poly = t * (0.127414796 + t * (-0.142248368 + t * (0.7107068705
                     + t * (-0.7265760135 + t * 0.5307027145))))
    half_tail = x * (half_poly * jnp.exp2(x * x * (-0.5 * _LOG2_E)))
    return jnp.where(x >= 0.0, x - half_tail, half_tail)


def _sigmoid(x):
    return 1.0 / (1.0 + jnp.exp(-x))


def _bf16(x):
    return x.astype(jnp.bfloat16)


def _dot(a, b):
    return jnp.dot(a, b, preferred_element_type=jnp.float32)


ROW_SUBTILES = D_MODEL // LANES


def _store_row_tiled(ref, lead, x):
    rows = x.shape[0]
    for s in range(ROW_SUBTILES):
        ref[(*lead, pl.ds(s, rows, stride=ROW_SUBTILES), slice(None))] = x[:, s * LANES:(s + 1) * LANES]


def _load_row_tiled(ref, lead, rows):
    return jnp.concatenate(
        [ref[(*lead, pl.ds(s, rows, stride=ROW_SUBTILES), slice(None))] for s in range(ROW_SUBTILES)],
        axis=1)


def _dot_nt(a, b):
    return lax.dot_general(a, b, (((1,), (1,)), ((), ())), preferred_element_type=jnp.float32)


def _rms(x, g):
    return x * lax.rsqrt(jnp.mean(x * x, axis=-1, keepdims=True) + NORM_EPS) * g


def _lane_lo(rows):
    return lax.broadcasted_iota(jnp.int32, (rows, LANES), 1) < HEAD_DIM


def _dup_heads(kv):
    rows = kv.shape[0]
    lo = _lane_lo(rows)
    out = []
    for j in range(KV_W // LANES):
        blk = kv[:, j * LANES:(j + 1) * LANES]
        swp = pltpu.roll(blk, HEAD_DIM, axis=1)
        out.append(jnp.where(lo, blk, swp))
        out.append(jnp.where(lo, swp, blk))
    return _bf16(jnp.concatenate(out, axis=1))


def _rope_block(zb, cos_b, sin_lo, sin_hi):
    up = pltpu.roll(zb, LANES - ROT_DIM // 2, axis=1)
    dn = pltpu.roll(zb, ROT_DIM // 2, axis=1)
    return zb * cos_b + up * sin_lo + dn * sin_hi


def _proj_kernel(n_prompt_tiles, xp_ref, xs_ref, gmix_ref, w_ref, b_ref, lng_ref, lnb_ref,
                 cos_ref, slo_ref, shi_ref,
                 u_ref, vln_ref, vs_ref, q_ref, k_ref, v_ref, kd_ref, vd_ref, ga_ref, gb_ref):
    i = pl.program_id(0)
    x = jnp.where(i < n_prompt_tiles, xp_ref[...], xs_ref[...])
    h = _bf16(_rms(x, gmix_ref[...]))

    cos_b, sin_lo, sin_hi = cos_ref[...], slo_ref[...], shi_ref[...]
    off_u, off_v, off_q, off_k, off_vv, off_ga, off_gb = (
        int(o) for o in np.cumsum((0, SGU_WIDTH, SGU_WIDTH, ATT_W, KV_W, KV_W, D_MODEL)))

    def z(lo):
        return _dot(h, w_ref[:, lo:lo + COL_BLOCK]) + b_ref[:, lo:lo + COL_BLOCK]

    def cols(j):
        return slice(j * COL_BLOCK, (j + 1) * COL_BLOCK)

    def rope(zb):
        return jnp.concatenate(
            [_rope_block(zb[:, t * LANES:(t + 1) * LANES], cos_b, sin_lo, sin_hi)
             for t in range(COL_BLOCK // LANES)], axis=1)

    gelu_v = []

    def do_u(j):
        u_ref[:, cols(j)] = _bf16(z(off_u + j * COL_BLOCK))

    def do_v(j):
        gelu_v.append(_gelu(z(off_v + j * COL_BLOCK)))

    def do_q(j):
        q_ref[:, cols(j)] = _bf16(rope(z(off_q + j * COL_BLOCK)) * (HEAD_DIM ** -0.5))

    def do_k(j):
        kr = rope(z(off_k))
        k_ref[...] = kr
        kd_ref[...] = _dup_heads(kr)

    def do_vv(j):
        zv = z(off_vv)
        v_ref[...] = zv
        vd_ref[...] = _dup_heads(zv)

    def do_ga(j):
        ga_ref[:, cols(j)] = _bf16(_sigmoid(z(off_ga + j * COL_BLOCK)))

    def do_gb(j):
        gb_ref[:, cols(j)] = _bf16(_sigmoid(z(off_gb + j * COL_BLOCK)))

    order = ((do_v, 0), (do_q, 0), (do_q, 1), (do_u, 0), (do_v, 1), (do_q, 2), (do_q, 3), (do_u, 1),
             (do_v, 2), (do_k, 0), (do_vv, 0), (do_u, 2), (do_v, 3), (do_ga, 0), (do_ga, 1), (do_u, 3),
             (do_ga, 2), (do_ga, 3), (do_gb, 0), (do_gb, 1), (do_gb, 2), (do_gb, 3))
    for fn, j in order:
        fn(j)
    gv = jnp.concatenate(gelu_v, axis=1)
    gc = gv - jnp.mean(gv, axis=-1, keepdims=True)
    var = jnp.mean(gc * gc, axis=-1, keepdims=True)
    vln = gc * lax.rsqrt(var + NORM_EPS) * lng_ref[...] + lnb_ref[...]
    vln_ref[...] = _bf16(vln)
    vs_ref[...] = vln


def _rope_tables(pos):
    half = ROT_DIM // 2
    inv = np.float32(ROPE_THETA) ** (-np.arange(half, dtype=np.float32) * np.float32(2.0) / ROT_DIM)
    ang = pos.astype(np.float32)[:, None] * inv.astype(np.float32)[None, :]
    cos = np.cos(ang.astype(np.float64)).astype(np.float32)
    sin = np.sin(ang.astype(np.float64)).astype(np.float32)
    n = pos.shape[0]
    ones = np.ones((n, HEAD_DIM - ROT_DIM), np.float32)
    zeros = np.zeros((n, HEAD_DIM - ROT_DIM), np.float32)
    zh = np.zeros((n, half), np.float32)
    cos_h = np.concatenate([cos, cos, ones], axis=1)
    slo_h = np.concatenate([-sin, zh, zeros], axis=1)
    shi_h = np.concatenate([zh, sin, zeros], axis=1)
    rep = LANES // HEAD_DIM
    return tuple(jnp.asarray(np.tile(a, (1, rep))) for a in (cos_h, slo_h, shi_h))


def _row_spec(width):
    return pl.BlockSpec((ROW_TILE, width), lambda i: (i, 0))


def _const_spec(shape):
    return pl.BlockSpec(shape, lambda i: (0,) * len(shape))


def _prompt_spec(width, n_prompt_tiles):
    return pl.BlockSpec((ROW_TILE, width), lambda i: (jnp.minimum(i, n_prompt_tiles - 1), 0))


def _sample_spec(width, n_prompt_tiles):
    return pl.BlockSpec((ROW_TILE, width), lambda i: (jnp.maximum(i - n_prompt_tiles, 0), 0))


def _params():
    return pltpu.CompilerParams(dimension_semantics=("arbitrary",), vmem_limit_bytes=VMEM_LIMIT)


def _project(xp, xs, g_mix, w_in, b_in, ln_g, ln_b, tables, seq):
    tp, ts = xp.shape[0], xs.shape[0]
    t = tp + ts
    npt = tp // ROW_TILE
    tiles_per_seq = seq // ROW_TILE
    f32, bf16 = jnp.float32, jnp.bfloat16
    table_spec = pl.BlockSpec(
        (ROW_TILE, LANES), lambda i: (jnp.where(i < npt, i % tiles_per_seq, tiles_per_seq), 0))
    out_shape = (
        jax.ShapeDtypeStruct((t, SGU_WIDTH), bf16),
        jax.ShapeDtypeStruct((t, SGU_WIDTH), bf16),
        jax.ShapeDtypeStruct((ts, SGU_WIDTH), f32),
        jax.ShapeDtypeStruct((t, ATT_W), bf16),
        jax.ShapeDtypeStruct((t, KV_W), f32),
        jax.ShapeDtypeStruct((t, KV_W), f32),
        jax.ShapeDtypeStruct((t, KV_DUP_W), bf16),
        jax.ShapeDtypeStruct((t, KV_DUP_W), bf16),
        jax.ShapeDtypeStruct((t, D_MODEL), bf16),
        jax.ShapeDtypeStruct((t, D_MODEL), bf16),
    )
    return pl.pallas_call(
        functools.partial(_proj_kernel, npt),
        out_shape=out_shape,
        grid=(t // ROW_TILE,),
        in_specs=[
            _prompt_spec(D_MODEL, npt), _sample_spec(D_MODEL, npt),
            _const_spec((1, D_MODEL)), _const_spec((D_MODEL, N_IN)),
            _const_spec((1, N_IN)), _const_spec((1, SGU_WIDTH)), _const_spec((1, SGU_WIDTH)),
            table_spec, table_spec, table_spec,
        ],
        out_specs=(
            _row_spec(SGU_WIDTH), _row_spec(SGU_WIDTH), _sample_spec(SGU_WIDTH, npt),
            _row_spec(ATT_W), _row_spec(KV_W), _row_spec(KV_W), _row_spec(KV_DUP_W),
            _row_spec(KV_DUP_W), _row_spec(D_MODEL), _row_spec(D_MODEL),
        ),
        compiler_params=_params(),
        name="proj",
    )(xp, xs, g_mix.reshape(1, -1), w_in.astype(bf16), b_in.reshape(1, -1),
      ln_g.reshape(1, -1), ln_b.reshape(1, -1), *tables)


def _attend(qa, qb, kwin, vwin, sink, valid):
    lo = _lane_lo(CHUNK)
    zero = jnp.zeros_like(qa)
    lhs = jnp.concatenate([jnp.where(lo, qa, zero), jnp.where(lo, zero, qa),
                           jnp.where(lo, qb, zero), jnp.where(lo, zero, qb)], axis=0)
    s = _dot_nt(lhs, kwin)
    if valid is not None:
        s = jnp.where(valid, s, NEG_INF)
    s_a, s_b = s[:, :LANES], s[:, LANES:]
    tail = s_b.shape[1]
    m = jnp.maximum(jnp.max(s, axis=-1, keepdims=True), sink)
    p_a = jnp.exp(s_a - m)
    p_b = jnp.exp(s_b - m[:, :tail])
    denom = (jnp.sum(jnp.concatenate([p_a, p_b], axis=1), axis=-1, keepdims=True)
             + jnp.exp(sink - m))
    inv = 1.0 / denom
    pn = jnp.concatenate([p_a * inv, p_b * inv[:, :tail]], axis=1)
    r = _dot(_bf16(pn), vwin)
    oa = jnp.where(lo, r[0:CHUNK], r[CHUNK:2 * CHUNK])
    ob = jnp.where(lo, r[2 * CHUNK:3 * CHUNK], r[3 * CHUNK:4 * CHUNK])
    return oa, ob


def _stack_rows(rows):
    ri = lax.broadcasted_iota(jnp.int32, (8, rows[0].shape[1]), 0)
    out = jnp.zeros((8, rows[0].shape[1]), rows[0].dtype)
    for k, row in enumerate(rows):
        out = jnp.where(ri == k, row, out)
    return out


def _route_pick(logits_t):
    rows = logits_t.shape[1]
    eid = lax.broadcasted_iota(jnp.int32, (N_EXPERTS, rows), 0)
    work = logits_t
    vals, idxs = [], []
    for _ in range(TOP_K):
        m = jnp.max(work, axis=0, keepdims=True)
        idx = jnp.min(jnp.where(work == m, eid, N_EXPERTS), axis=0, keepdims=True)
        vals.append(m)
        idxs.append(idx)
        work = jnp.where(eid == idx, -jnp.inf, work)
    exps = [jnp.exp(v - vals[0]) for v in vals]
    inv = 1.0 / (exps[0] + exps[1] + exps[2] + exps[3])
    gates = _stack_rows([e * inv for e in exps])

    picked = jnp.zeros((N_EXPERTS, rows), jnp.float32)
    for idx in idxs:
        picked = jnp.where(eid == idx, 1.0, picked)
    tr = lax.broadcasted_iota(jnp.int32, (rows, rows), 0)
    tc = lax.broadcasted_iota(jnp.int32, (rows, rows), 1)
    earlier = _bf16(jnp.where(tr < tc, 1.0, 0.0))
    in_tile = _dot(_bf16(picked), earlier)
    count_col = jnp.broadcast_to(jnp.sum(picked, axis=1, keepdims=True), (N_EXPERTS, LANES))
    eid_wide = lax.broadcasted_iota(jnp.int32, (LANES, rows), 0)
    picked_wide = jnp.zeros((LANES, rows), jnp.float32)
    for idx in idxs:
        picked_wide = jnp.where(eid_wide == idx, 1.0, picked_wide)
    count_row = _dot_nt(jnp.ones((8, rows), jnp.bfloat16), _bf16(picked_wide))
    return idxs, gates, in_tile, count_col, count_row


def _route_place(idxs, in_tile, count_col, count_row, carry_s, live):
    rows = in_tile.shape[1]
    eid = lax.broadcasted_iota(jnp.int32, (N_EXPERTS, rows), 0)
    er = lax.broadcasted_iota(jnp.int32, (N_EXPERTS, N_EXPERTS), 0)
    ec = lax.broadcasted_iota(jnp.int32, (N_EXPERTS, N_EXPERTS), 1)
    start_col = _dot(_bf16(jnp.where(ec < er, 1.0, 0.0)), _bf16(count_col))
    local = in_tile + jnp.concatenate([start_col] * (rows // LANES), axis=1)
    slots = _stack_rows([jnp.sum(jnp.where(eid == idx, local, 0.0), axis=0, keepdims=True)
                         for idx in idxs]).astype(jnp.int32)
    lr = lax.broadcasted_iota(jnp.int32, (LANES, LANES), 0)
    lc = lax.broadcasted_iota(jnp.int32, (LANES, LANES), 1)
    start_row = _dot(_bf16(count_row), _bf16(jnp.where(lr < lc, 1.0, 0.0)))
    ri = lax.broadcasted_iota(jnp.int32, (8, LANES), 0)
    meta = jnp.where(ri == 0, carry_s[...], jnp.where(ri == 1, count_row, jnp.where(ri == 2, start_row, 0.0)))
    carry_s[...] = carry_s[...] + count_row * live
    return slots, meta.astype(jnp.int32)


def _mix_kernel(tiles_per_seq, n_prompt_tiles,
                xp_ref, xs_ref, u_ref, vln_ref, q_ref, kd_ref, vd_ref, kdp_ref, vdp_ref,
                ck_ref, cv_ref, ga_ref, gb_ref, wsp_ref, bsp_ref, sink_ref,
                wpa_ref, wpb_ref, wo_ref, gffn_ref, wrh_ref, wrl_ref, br_ref,
                x1_ref, hloc_ref, slot_ref, gate_ref, meta_ref, count_ref,
                a_s, o_s, kwin_s, vwin_s, carry_s, hhi_s, hlo_s):
    i = pl.program_id(0)
    n_streams = ROW_TILE // CHUNK

    @pl.when(i == 0)
    def _():
        carry_s[...] = jnp.zeros_like(carry_s)
        hhi_s[...] = jnp.zeros_like(hhi_s)
        hlo_s[...] = jnp.zeros_like(hlo_s)

    def sgu_rows(r0, rows):
        ri = lax.broadcasted_iota(jnp.int32, (rows, rows), 0) // CHUNK
        ci = lax.broadcasted_iota(jnp.int32, (rows, rows), 1) // CHUNK
        for g in range(SGU_GROUPS):
            cols = slice(g * LANES, (g + 1) * LANES)
            w = _bf16(jnp.where(ci <= ri, wsp_ref[g, :rows, :rows], 0.0))
            sp = _dot(w, vln_ref[r0:r0 + rows, cols]) + bsp_ref[g, :rows, :]
            a_s[r0:r0 + rows, cols] = _bf16(_gelu(u_ref[r0:r0 + rows, cols].astype(jnp.float32)) * sp)

    def attend_rows(r0, kwin_of, valid):
        for g in range(N_KV_HEADS):
            c0 = g * Q_PER_KV * HEAD_DIM
            kwin, vwin = kwin_of(g)
            oa, ob = _attend(q_ref[r0:r0 + CHUNK, c0:c0 + LANES],
                             q_ref[r0:r0 + CHUNK, c0 + LANES:c0 + 2 * LANES],
                             kwin, vwin, sink_ref[g], valid)
            o_s[r0:r0 + CHUNK, c0:c0 + LANES] = _bf16(oa)
            o_s[r0:r0 + CHUNK, c0 + LANES:c0 + 2 * LANES] = _bf16(ob)

    @pl.when(i < n_prompt_tiles)
    def _prompt():
        for c in range(ROW_TILE // SGU_CHUNK):
            sgu_rows(c * SGU_CHUNK, SGU_CHUNK)
        kwin_s[0:WINDOW] = kdp_ref[...]
        kwin_s[WINDOW:WINDOW + ROW_TILE] = kd_ref[...]
        vwin_s[0:WINDOW] = vdp_ref[...]
        vwin_s[WINDOW:WINDOW + ROW_TILE] = vd_ref[...]
        first = (i % tiles_per_seq) == 0
        col = lax.broadcasted_iota(jnp.int32, (1, KEY_SPAN), 1)
        for j in range(ROW_TILE // CHUNK):
            r0 = j * CHUNK
            valid = jnp.logical_or(jnp.logical_not(first), col + r0 >= WINDOW) if r0 < WINDOW else None

            def kwin_of(g, r0=r0):
                cols = slice(g * LANES, (g + 1) * LANES)
                return kwin_s[r0:r0 + KEY_SPAN, cols], vwin_s[r0:r0 + KEY_SPAN, cols]

            attend_rows(r0, kwin_of, valid)

    @pl.when(i >= n_prompt_tiles)
    def _sample():
        for s in range(n_streams):
            r0 = s * CHUNK
            sgu_rows(r0, CHUNK)
            kwin_s[0:WINDOW] = _dup_heads(ck_ref[s])
            kwin_s[WINDOW:KEY_SPAN] = kd_ref[r0:r0 + CHUNK]
            vwin_s[0:WINDOW] = _dup_heads(cv_ref[s])
            vwin_s[WINDOW:KEY_SPAN] = vd_ref[r0:r0 + CHUNK]

            def kwin_of(g):
                cols = slice(g * LANES, (g + 1) * LANES)
                return kwin_s[0:KEY_SPAN, cols], vwin_s[0:KEY_SPAN, cols]

            attend_rows(r0, kwin_of, None)

    hh, hl = hhi_s[...], hlo_s[...]
    logits_t = (_dot_nt(wrh_ref[...], hh) + _dot_nt(wrl_ref[...], hh) + _dot_nt(wrh_ref[...], hl)
                + jnp.concatenate([br_ref[...]] * (ROW_TILE // LANES), axis=1))
    m_a = ga_ref[...].astype(jnp.float32) * _dot(a_s[...], wpa_ref[...])
    idxs, gates, in_tile, count_col, count_row = _route_pick(logits_t)
    gate_ref[...] = gates
    m = m_a + gb_ref[...].astype(jnp.float32) * _dot(o_s[...], wpb_ref[...])
    slots, meta = _route_place(idxs, in_tile, count_col, count_row, carry_s, jnp.where(i > 0, 1.0, 0.0))
    slot_ref[...] = slots * ROW_SUBTILES
    meta_ref[...] = meta
    count_ref[...] = carry_s[...].astype(jnp.int32)
    n_slots = ROW_TILE * TOP_K
    sid = lax.broadcasted_iota(jnp.int32, (n_slots, ROW_TILE), 0)
    place = jnp.zeros((n_slots, ROW_TILE), jnp.float32)
    for k in range(TOP_K):
        place = jnp.where(sid == slots[k:k + 1, :], 1.0, place)
    _store_row_tiled(hloc_ref, (), _dot(_bf16(place), hh))

    x = jnp.where(i < n_prompt_tiles, xp_ref[...], xs_ref[...])
    x1 = x + _dot(_bf16(m), wo_ref[...])
    x1_ref[...] = x1
    h2 = _rms(x1, gffn_ref[...])
    h2_hi = _bf16(h2)
    hhi_s[...] = h2_hi
    hlo_s[...] = _bf16(h2 - h2_hi.astype(jnp.float32))


def _mix(xp, xs, u, vln, q, kd, vd, cache_k, cache_v, ga, gb, w_sp, b_sp, sinks,
         w_pa, w_pb, w_o, g_ffn, w_router, b_router, seq):
    tp, ts = xp.shape[0], xs.shape[0]
    t = tp + ts
    npt = tp // ROW_TILE
    tiles_per_seq = seq // ROW_TILE
    f32, bf16 = jnp.float32, jnp.bfloat16
    n_streams = ROW_TILE // CHUNK
    win_per_tile = ROW_TILE // WINDOW

    nt = t // ROW_TILE
    cur = lambda i: jnp.minimum(i, nt - 1)
    smp = lambda i: jnp.maximum(cur(i) - npt, 0)
    row = lambda width: pl.BlockSpec((ROW_TILE, width), lambda i: (cur(i), 0))
    prev_spec = pl.BlockSpec(
        (WINDOW, KV_DUP_W), lambda i: (jnp.maximum(jnp.minimum(i, npt - 1) * win_per_tile - 1, 0), 0))
    cache_spec = pl.BlockSpec((n_streams, WINDOW, KV_W), lambda i: (smp(i), 0, 0))
    xs_spec = pl.BlockSpec((ROW_TILE, D_MODEL), lambda i: (smp(i), 0))
    sink_cols = jnp.broadcast_to(
        jnp.repeat(sinks.astype(f32).reshape(N_KV_HEADS, Q_PER_KV), CHUNK, axis=1)[:, :, None],
        (N_KV_HEADS, Q_PER_KV * CHUNK, LANES))
    wr_t = w_router.T
    wr_hi = wr_t.astype(bf16)
    wr_lo = (wr_t - wr_hi.astype(f32)).astype(bf16)
    routed8 = lambda width: pl.BlockSpec((8, width), lambda i: (jnp.maximum(i - 1, 0), 0))
    out_shape = (
        jax.ShapeDtypeStruct((t, D_MODEL), f32),
        jax.ShapeDtypeStruct((t * TOP_K * ROW_SUBTILES, LANES), f32),
        jax.ShapeDtypeStruct((nt * 8, ROW_TILE), jnp.int32),
        jax.ShapeDtypeStruct((nt * 8, ROW_TILE), f32),
        jax.ShapeDtypeStruct((nt * 8, LANES), jnp.int32),
        jax.ShapeDtypeStruct((8, LANES), jnp.int32),
    )
    return pl.pallas_call(
        functools.partial(_mix_kernel, tiles_per_seq, npt),
        out_shape=out_shape,
        grid=(nt + 1,),
        in_specs=[
            _prompt_spec(D_MODEL, npt), xs_spec,
            row(SGU_WIDTH), row(SGU_WIDTH), row(ATT_W),
            row(KV_DUP_W), row(KV_DUP_W), prev_spec, prev_spec,
            cache_spec, cache_spec, row(D_MODEL), row(D_MODEL),
            _const_spec((SGU_GROUPS, SGU_CHUNK, SGU_CHUNK)), _const_spec((SGU_GROUPS, SGU_CHUNK, LANES)),
            _const_spec((N_KV_HEADS, Q_PER_KV * CHUNK, LANES)),
            _const_spec((SGU_WIDTH, D_MODEL)), _const_spec((ATT_W, D_MODEL)),
            _const_spec((D_MODEL, D_MODEL)), _const_spec((1, D_MODEL)),
            _const_spec((N_EXPERTS, D_MODEL)), _const_spec((N_EXPERTS, D_MODEL)),
            _const_spec((N_EXPERTS, LANES)),
        ],
        out_specs=(row(D_MODEL),
                   pl.BlockSpec((ROW_TILE * TOP_K * ROW_SUBTILES, LANES),
                                lambda i: (jnp.maximum(i - 1, 0), 0)),
                   routed8(ROW_TILE), routed8(ROW_TILE), routed8(LANES), _const_spec((8, LANES))),
        scratch_shapes=[
            pltpu.VMEM((ROW_TILE, SGU_WIDTH), bf16), pltpu.VMEM((ROW_TILE, ATT_W), bf16),
            pltpu.VMEM((WINDOW + ROW_TILE, KV_DUP_W), bf16),
            pltpu.VMEM((WINDOW + ROW_TILE, KV_DUP_W), bf16),
            pltpu.VMEM((8, LANES), f32),
            pltpu.VMEM((ROW_TILE, D_MODEL), bf16), pltpu.VMEM((ROW_TILE, D_MODEL), bf16),
        ],
        compiler_params=_params(),
        name="mix",
    )(xp, xs, u, vln, q, kd, vd, kd, vd,
      cache_k.reshape(-1, WINDOW, KV_W), cache_v.reshape(-1, WINDOW, KV_W), ga, gb,
      w_sp, jnp.broadcast_to(b_sp[:, :, None], (SGU_GROUPS, SGU_CHUNK, LANES)), sink_cols,
      w_pa.astype(bf16), w_pb.astype(bf16), w_o.astype(bf16),
      g_ffn.reshape(1, -1), wr_hi, wr_lo,
      jnp.broadcast_to(b_router.astype(f32)[:, None], (N_EXPERTS, LANES)))


def _unrolled_rows(n_rows, fn):
    if isinstance(n_rows, int):
        groups, tail_start = n_rows // ROW_UNROLL, n_rows - n_rows % ROW_UNROLL
    else:
        groups = lax.shift_right_logical(n_rows, ROW_UNROLL.bit_length() - 1)
        tail_start = groups * ROW_UNROLL

    def group(gi, carry):
        for lane in range(ROW_UNROLL):
            fn(gi * ROW_UNROLL + lane, lane)
        return carry

    def tail(r, carry):
        fn(r, 0)
        return carry

    lax.fori_loop(0, groups, group, 0)
    lax.fori_loop(tail_start, n_rows, tail, 0)


def _row_span(first_row, n_rows):
    return pl.ds(pl.multiple_of(first_row * ROW_SUBTILES, ROW_SUBTILES),
                 pl.multiple_of(n_rows * ROW_SUBTILES, ROW_SUBTILES))


def _run_spec(index_of):
    return pl.BlockSpec((1, 1, LANES), lambda i, *_: (index_of(i), 0, 0), memory_space=pltpu.SMEM)


def _expert_kernel(n_token_tiles,
                   te_ref, nu_ref, nx_ref, par_ref, nv_ref, tf_ref, cnt_ref, loc_ref,
                   bgu_ref, bdn_ref, hloc_hbm, wgu_hbm, wdn_hbm,
                   ys_ref,
                   xbuf, wgu_f, wdn_f, wgu_s, wdn_s, walk, xsem, wsem):
    i = pl.program_id(0)
    n_used = nu_ref[0]
    expert = te_ref[i]
    buf = par_ref[i]
    slot = i % 2
    expert_changed = jnp.logical_or(i == 0, expert != te_ref[jnp.maximum(i - 1, 0)])

    def fetch_rows(j, b):
        e = te_ref[j]
        need = nv_ref[j]

        @pl.when(tf_ref[j] == 1)
        def _():
            walk[0] = 0
            walk[1] = 0

        @pl.when(need < MOE_TILE)
        def _():
            xbuf[b] = jnp.zeros(xbuf.shape[1:], xbuf.dtype)

        def unfinished(state):
            filled, tile, _ = state
            return jnp.logical_and(filled < need, tile < n_token_tiles)

        def take_run(state):
            filled, tile, off = state
            run = cnt_ref[tile * N_EXPERTS + e]
            take = jnp.minimum(run - off, need - filled)

            @pl.when(take > 0)
            def _():
                src = tile * (ROW_TILE * TOP_K) + loc_ref[tile * N_EXPERTS + e] + off
                pltpu.make_async_copy(hloc_hbm.at[_row_span(src, take)],
                                      xbuf.at[b, _row_span(filled, take)], xsem.at[b]).start()

            run_done = off + take == run
            return (filled + take, jnp.where(run_done, tile + 1, tile), jnp.where(run_done, 0, off + take))

        _, tile, off = lax.while_loop(unfinished, take_run, (jnp.int32(0), walk[0], walk[1]))
        walk[0] = tile
        walk[1] = off

    @pl.when(jnp.logical_and(i == 0, n_used > 0))
    def _():
        fetch_rows(0, 0)

    @pl.when(i + 1 < n_used)
    def _():
        fetch_rows(i + 1, 1 - slot)

    def weight_copies(e, b):
        return (pltpu.make_async_copy(wgu_hbm.at[e], wgu_f.at[b], wsem.at[0, b]),
                pltpu.make_async_copy(wdn_hbm.at[e], wdn_f.at[b], wsem.at[1, b]))

    @pl.when(jnp.logical_and(i < n_used, expert_changed))
    def _():
        @pl.when(i == 0)
        def _():
            for copy in weight_copies(expert, buf):
                copy.start()

        for copy in weight_copies(expert, buf):
            copy.wait()
        following = nx_ref[i]

        @pl.when(following != expert)
        def _():
            for copy in weight_copies(following, 1 - buf):
                copy.start()

        wgu_s[...] = _bf16(wgu_f[buf])
        wdn_s[...] = _bf16(wdn_f[buf])

    rows = nv_ref[i]

    @pl.when(i < n_used)
    def _():
        pltpu.make_async_copy(hloc_hbm.at[_row_span(0, rows)], xbuf.at[slot, _row_span(0, rows)],
                              xsem.at[slot]).wait()

    def mlp(n):
        x = _bf16(_load_row_tiled(xbuf, (slot,), n))
        gu = _dot(x, wgu_s[...]) + bgu_ref[0]
        gate = jnp.minimum(gu[:, :D_FF], SWIGLU_LIMIT)
        lin = jnp.clip(gu[:, D_FF:], -SWIGLU_LIMIT, SWIGLU_LIMIT)
        act = gate * _sigmoid(SWIGLU_ALPHA * gate) * (lin + 1.0)
        _store_row_tiled(ys_ref, (), _dot(_bf16(act), wdn_s[...]) + bdn_ref[0])

    @pl.when(jnp.logical_and(i < n_used, rows > MOE_TILE // 2))
    def _():
        mlp(MOE_TILE)

    @pl.when(jnp.logical_and(i < n_used, rows <= MOE_TILE // 2))
    def _():
        mlp(MOE_TILE // 2)
        ys_ref[pl.ds(MOE_TILE // 2 * ROW_SUBTILES, MOE_TILE // 2 * ROW_SUBTILES), :] = jnp.zeros(
            (MOE_TILE // 2 * ROW_SUBTILES, LANES), ys_ref.dtype)

    @pl.when(i >= n_used)
    def _():
        ys_ref[...] = jnp.zeros_like(ys_ref)


def _experts(tile_expert, n_used, next_expert, weight_buf, n_valid, tile_first, run_n, run_loc,
             h_local, w_gu, b_gu, w_dn, b_dn):
    n_tiles = tile_expert.shape[0]
    n_token_tiles = h_local.shape[0] // (ROW_TILE * TOP_K * ROW_SUBTILES)
    f32, bf16 = jnp.float32, jnp.bfloat16
    tile_rows = MOE_TILE * ROW_SUBTILES
    grid_spec = pltpu.PrefetchScalarGridSpec(
        num_scalar_prefetch=8,
        grid=(n_tiles,),
        in_specs=[
            pl.BlockSpec((1, 1, 2 * D_FF), lambda i, te, *_: (te[i], 0, 0)),
            pl.BlockSpec((1, 1, D_MODEL), lambda i, te, *_: (te[i], 0, 0)),
            pl.BlockSpec(memory_space=pl.ANY), pl.BlockSpec(memory_space=pl.ANY),
            pl.BlockSpec(memory_space=pl.ANY),
        ],
        out_specs=pl.BlockSpec((tile_rows, LANES), lambda i, *_: (i, 0)),
        scratch_shapes=[
            pltpu.VMEM((2, tile_rows, LANES), f32),
            pltpu.VMEM((2, D_MODEL, 2 * D_FF), f32), pltpu.VMEM((2, D_FF, D_MODEL), f32),
            pltpu.VMEM((D_MODEL, 2 * D_FF), bf16), pltpu.VMEM((D_FF, D_MODEL), bf16),
            pltpu.SMEM((2,), jnp.int32),
            pltpu.SemaphoreType.DMA((2,)), pltpu.SemaphoreType.DMA((2, 2)),
        ],
    )
    return pl.pallas_call(
        functools.partial(_expert_kernel, n_token_tiles),
        out_shape=jax.ShapeDtypeStruct((n_tiles * tile_rows, LANES), f32),
        grid_spec=grid_spec,
        compiler_params=_params(),
        name="experts",
    )(tile_expert, n_used, next_expert, weight_buf, n_valid, tile_first, run_n, run_loc,
      b_gu.reshape(N_EXPERTS, 1, -1), b_dn.reshape(N_EXPERTS, 1, -1), h_local, w_gu, w_dn)


def _combine_kernel(n_prompt_tiles,
                    src_ref, n_ref, dst_ref, src_nx_ref, n_nx_ref, dst_nx_ref, slot_ref, gate_ref,
                    x1_ref, gfin_ref, ys_hbm,
                    yp_ref, yo_ref,
                    local, mixed, run_sem):
    i = pl.program_id(0)
    last = pl.num_programs(0) - 1
    buf = i % 2

    def fetch_runs(s_ref, c_ref, d_ref, b):
        for e in range(N_EXPERTS):
            n = c_ref[0, 0, e]
            copy = pltpu.make_async_copy(ys_hbm.at[_row_span(d_ref[0, 0, e], n)],
                                         local.at[b, _row_span(s_ref[0, 0, e], n)], run_sem.at[b])
            pl.when(n > 0)(copy.start)

    @pl.when(i == 0)
    def _():
        fetch_runs(src_ref, n_ref, dst_ref, 0)

    @pl.when(i < last)
    def _():
        fetch_runs(src_nx_ref, n_nx_ref, dst_nx_ref, 1 - buf)

    pltpu.make_async_copy(ys_hbm.at[pl.ds(0, local.shape[1])], local.at[buf], run_sem.at[buf]).wait()

    def blend(t, lane):
        acc = None
        for k in range(TOP_K):
            at = pl.multiple_of(slot_ref[0, 0, k * ROW_TILE + t], ROW_SUBTILES)
            term = gate_ref[0, 0, k * ROW_TILE + t] * local[buf, pl.ds(at, ROW_SUBTILES), :]
            acc = term if acc is None else acc + term
        mixed[pl.ds(pl.multiple_of(t * ROW_SUBTILES, ROW_SUBTILES), ROW_SUBTILES), :] = acc
    _unrolled_rows(ROW_TILE, blend)

    out = _rms(x1_ref[...] + _load_row_tiled(mixed, (), ROW_TILE), gfin_ref[...])

    @pl.when(i < n_prompt_tiles)
    def _():
        yp_ref[...] = out

    @pl.when(i >= n_prompt_tiles)
    def _():
        yo_ref[...] = out


def _combine(run_src, run_n, run_dst, slots, gates, x1, ys, g_final, tp):
    t = x1.shape[0]
    npt = tp // ROW_TILE
    nt = t // ROW_TILE
    f32 = jnp.float32
    picks = ROW_TILE * TOP_K
    nxt = lambda i: jnp.minimum(i + 1, nt - 1)
    pick_spec = pl.BlockSpec((1, 1, picks), lambda i: (i, 0, 0), memory_space=pltpu.SMEM)
    return pl.pallas_call(
        functools.partial(_combine_kernel, npt),
        out_shape=(jax.ShapeDtypeStruct((tp, D_MODEL), f32),
                   jax.ShapeDtypeStruct((t - tp, D_MODEL), f32)),
        grid=(nt,),
        in_specs=[_run_spec(lambda i: i), _run_spec(lambda i: i), _run_spec(lambda i: i),
                  _run_spec(nxt), _run_spec(nxt), _run_spec(nxt), pick_spec, pick_spec,
                  _row_spec(D_MODEL), _const_spec((1, D_MODEL)), pl.BlockSpec(memory_space=pl.ANY)],
        out_specs=(_prompt_spec(D_MODEL, npt), _sample_spec(D_MODEL, npt)),
        scratch_shapes=[pltpu.VMEM((2, picks * ROW_SUBTILES, LANES), f32),
                        pltpu.VMEM((ROW_TILE * ROW_SUBTILES, LANES), f32),
                        pltpu.SemaphoreType.DMA((2,))],
        compiler_params=_params(),
        name="combine",
    )(run_src, run_n, run_dst, run_src, run_n, run_dst, slots, gates, x1, g_final.reshape(1, -1), ys)


def _plan(meta, counts, t):
    nt = t // ROW_TILE
    n_tiles = (t * TOP_K + N_EXPERTS * (MOE_TILE - 1)) // MOE_TILE
    counts = counts[0, :N_EXPERTS]
    tiles_e = (counts + MOE_TILE - 1) // MOE_TILE
    tile_end = jnp.cumsum(tiles_e)
    tile_start = tile_end - tiles_e
    n_used = tile_end[-1]
    first_row = jnp.pad(tile_start * MOE_TILE, (0, LANES - N_EXPERTS))
    meta = meta.reshape(nt, 8, LANES)
    run_dst = meta[:, 0:1, :] + first_row[None, None, :]
    run_n = meta[:, 1:2, :]
    run_src = meta[:, 2:3, :]
    tile_ids = jnp.arange(n_tiles, dtype=jnp.int32)
    live = jnp.minimum(tile_ids, n_used - 1)
    tile_expert = jnp.sum(tile_end[None, :] <= live[:, None], axis=1).astype(jnp.int32)
    ids = jnp.arange(N_EXPERTS, dtype=jnp.int32)
    is_expert = tile_expert[:, None] == ids[None, :]
    of_tile = lambda per_expert: jnp.sum(jnp.where(is_expert, per_expert[None, :], 0), axis=1)
    in_expert = tile_ids - of_tile(tile_start)
    n_valid = jnp.clip(of_tile(counts) - in_expert * MOE_TILE, 0, MOE_TILE)
    n_valid = jnp.where(tile_ids < n_used, n_valid, 0).astype(jnp.int32)
    tile_first = jnp.logical_and(in_expert == 0, tile_ids < n_used).astype(jnp.int32)
    run_n_flat = run_n[:, 0, :N_EXPERTS].reshape(-1)
    run_loc_flat = run_src[:, 0, :N_EXPERTS].reshape(-1)
    used = tiles_e > 0
    later_used = jnp.where(jnp.logical_and(used[None, :], ids[None, :] > ids[:, None]), ids[None, :],
                           N_EXPERTS)
    following = jnp.min(later_used, axis=1)
    following = jnp.where(following < N_EXPERTS, following, ids)
    buf_of = (jnp.cumsum(used.astype(jnp.int32)) - 1) % 2
    return (tile_expert, n_used.reshape(1).astype(jnp.int32), of_tile(following).astype(jnp.int32),
            of_tile(buf_of).astype(jnp.int32), n_valid, tile_first, run_n_flat, run_loc_flat,
            run_src, run_n, run_dst)


def kernel(x_prompt, x_sample, cache_k, cache_v, g_mix, w_in, b_in, ln_v_g, ln_v_b, w_sp, b_sp,
           attn_sinks, w_pa, w_pb, w_o, g_ffn, w_router, b_router, w_gu, b_gu, w_dn, b_dn, g_final):
    nb, seq, d = x_prompt.shape
    nsb, nnew, _ = x_sample.shape
    tp, ts = nb * seq, nsb * nnew
    t = tp + ts
    xp = x_prompt.reshape(tp, d)
    xs = x_sample.reshape(ts, d)
    pos = np.concatenate([np.arange(seq), np.tile(PAST_LEN + np.arange(nnew), ROW_TILE // nnew)])
    tables = _rope_tables(pos)
    u, vln, v_sgu, q, k, v, kd, vd, ga, gb = _project(
        xp, xs, g_mix[0], w_in[0], b_in[0], ln_v_g[0], ln_v_b[0], tables, seq)
    x1, h_local, slot_t, gate_t, meta, counts = _mix(
        xp, xs, u, vln, q, kd, vd, cache_k[0], cache_v[0], ga, gb, w_sp[0], b_sp[0], attn_sinks[0],
        w_pa[0], w_pb[0], w_o[0], g_ffn[0], w_router[0], b_router[0], seq)
    nt = t // ROW_TILE
    picks = lambda a: a.reshape(nt, 8, ROW_TILE)[:, :TOP_K, :].reshape(nt, 1, TOP_K * ROW_TILE)
    slots, gates = picks(slot_t), picks(gate_t)
    (tile_expert, n_used, next_expert, weight_buf, n_valid, tile_first, run_n_flat, run_loc_flat,
     run_src, run_n, run_dst) = _plan(meta, counts, t)
    y_sorted = _experts(tile_expert, n_used, next_expert, weight_buf, n_valid, tile_first, run_n_flat,
                        run_loc_flat, h_local, w_gu[0], b_gu[0], w_dn[0], b_dn[0])
    y_p, y_s = _combine(run_src, run_n, run_dst, slots, gates, x1, y_sorted, g_final, tp)

    keep = min(WINDOW, seq)
    tails = lambda a: jnp.stack([a[(b + 1) * seq - keep:(b + 1) * seq] for b in range(nb)]).reshape(
        nb, keep, N_KV_HEADS, HEAD_DIM)
    kp, vp = tails(k), tails(v)
    ks = k[tp:].reshape(nsb, nnew, N_KV_HEADS, HEAD_DIM)
    vs = v[tp:].reshape(nsb, nnew, N_KV_HEADS, HEAD_DIM)
    return (y_p.reshape(nb, seq, d), y_s.reshape(nsb, nnew, d), kp[None], vp[None], ks[None],
            vs[None], v_sgu.reshape(1, nsb, nnew, SGU_WIDTH))
```

```python
import functools

import numpy as np
import jax
import jax.numpy as jnp
from jax import lax
from jax.experimental import pallas as pl
from jax.experimental.pallas import tpu as pltpu

D_MODEL = 1024
PAST_LEN = 2048
CHUNK = 64
SGU_CHUNK = 128
SGU_GROUPS = 8
SGU_WIDTH = 1024
N_HEADS = 16
N_KV_HEADS = 4
HEAD_DIM = 64
Q_PER_KV = N_HEADS // N_KV_HEADS
WINDOW = 128
ROT_DIM = HEAD_DIM // 4
ROPE_THETA = 500000.0
ATT_W = N_HEADS * HEAD_DIM
KV_W = N_KV_HEADS * HEAD_DIM
N_EXPERTS = 32
TOP_K = 4
D_FF = 1024
SWIGLU_ALPHA = 1.702
SWIGLU_LIMIT = 7.0
NORM_EPS = 1e-5
NEG_INF = -1e30
N_IN = SGU_WIDTH * 2 + ATT_W + KV_W * 2 + D_MODEL * 2

LANES = 128
ROW_TILE = 256
MOE_TILE = 512
ROW_UNROLL = 8
COL_BLOCK = 256
KV_DUP_W = N_KV_HEADS * LANES
KEY_SPAN = WINDOW + CHUNK
VMEM_LIMIT = 56 * 1024 * 1024

_SQRT_HALF = 0.7071067811865476
_LOG2_E = 1.4426950408889634


def _gelu(x):
    t = 1.0 / (1.0 + (0.3275911 * _SQRT_HALF) * jnp.abs(x))
    half_poly = t * (0.127414796 + t * (-0.142248368 + t * (0.7107068705
                     + t * (-0.7265760135 + t * 0.5307027145))))
    half_tail = x * (half_poly * jnp.exp2(x * x * (-0.5 * _LOG2_E)))
    return jnp.where(x >= 0.0, x - half_tail, half_tail)


def _sigmoid(x):
    return 1.0 / (1.0 + jnp.exp(-x))


def _bf16(x):
    return x.astype(jnp.bfloat16)


def _dot(a, b):
    return jnp.dot(a, b, preferred_element_type=jnp.float32)


ROW_SUBTILES = D_MODEL // LANES


def _store_row_tiled(ref, lead, x):
    rows = x.shape[0]
    for s in range(ROW_SUBTILES):
        ref[(*lead, pl.ds(s, rows, stride=ROW_SUBTILES), slice(None))] = x[:, s * LANES:(s + 1) * LANES]


def _load_row_tiled(ref, lead, rows):
    return jnp.concatenate(
        [ref[(*lead, pl.ds(s, rows, stride=ROW_SUBTILES), slice(None))] for s in range(ROW_SUBTILES)],
        axis=1)


def _dot_nt(a, b):
    return lax.dot_general(a, b, (((1,), (1,)), ((), ())), preferred_element_type=jnp.float32)


def _rms(x, g):
    return x * lax.rsqrt(jnp.mean(x * x, axis=-1, keepdims=True) + NORM_EPS) * g


def _lane_lo(rows):
    return lax.broadcasted_iota(jnp.int32, (rows, LANES), 1) < HEAD_DIM


def _dup_heads(kv):
    rows = kv.shape[0]
    lo = _lane_lo(rows)
    out = []
    for j in range(KV_W // LANES):
        blk = kv[:, j * LANES:(j + 1) * LANES]
        swp = pltpu.roll(blk, HEAD_DIM, axis=1)
        out.append(jnp.where(lo, blk, swp))
        out.append(jnp.where(lo, swp, blk))
    return _bf16(jnp.concatenate(out, axis=1))


def _rope_block(zb, cos_b, sin_lo, sin_hi):
    up = pltpu.roll(zb, LANES - ROT_DIM // 2, axis=1)
    dn = pltpu.roll(zb, ROT_DIM // 2, axis=1)
    return zb * cos_b + up * sin_lo + dn * sin_hi


def _proj_kernel(n_prompt_tiles, xp_ref, xs_ref, gmix_ref, w_ref, b_ref, lng_ref, lnb_ref,
                 cos_ref, slo_ref, shi_ref,
                 u_ref, vln_ref, vs_ref, q_ref, k_ref, v_ref, kd_ref, vd_ref, ga_ref, gb_ref):
    i = pl.program_id(0)
    x = jnp.where(i < n_prompt_tiles, xp_ref[...], xs_ref[...])
    h = _bf16(_rms(x, gmix_ref[...]))

    cos_b, sin_lo, sin_hi = cos_ref[...], slo_ref[...], shi_ref[...]
    off_u, off_v, off_q, off_k, off_vv, off_ga, off_gb = (
        int(o) for o in np.cumsum((0, SGU_WIDTH, SGU_WIDTH, ATT_W, KV_W, KV_W, D_MODEL)))

    def z(lo):
        return _dot(h, w_ref[:, lo:lo + COL_BLOCK]) + b_ref[:, lo:lo + COL_BLOCK]

    def cols(j):
        return slice(j * COL_BLOCK, (j + 1) * COL_BLOCK)

    def rope(zb):
        return jnp.concatenate(
            [_rope_block(zb[:, t * LANES:(t + 1) * LANES], cos_b, sin_lo, sin_hi)
             for t in range(COL_BLOCK // LANES)], axis=1)

    gelu_v = []

    def do_u(j):
        u_ref[:, cols(j)] = _bf16(z(off_u + j * COL_BLOCK))

    def do_v(j):
        gelu_v.append(_gelu(z(off_v + j * COL_BLOCK)))

    def do_q(j):
        q_ref[:, cols(j)] = _bf16(rope(z(off_q + j * COL_BLOCK)) * (HEAD_DIM ** -0.5))

    def do_k(j):
        kr = rope(z(off_k))
        k_ref[...] = kr
        kd_ref[...] = _dup_heads(kr)

    def do_vv(j):
        zv = z(off_vv)
        v_ref[...] = zv
        vd_ref[...] = _dup_heads(zv)

    def do_ga(j):
        ga_ref[:, cols(j)] = _bf16(_sigmoid(z(off_ga + j * COL_BLOCK)))

    def do_gb(j):
        gb_ref[:, cols(j)] = _bf16(_sigmoid(z(off_gb + j * COL_BLOCK)))

    order = ((do_v, 0), (do_q, 0), (do_q, 1), (do_u, 0), (do_v, 1), (do_q, 2), (do_q, 3), (do_u, 1),
             (do_v, 2), (do_k, 0), (do_vv, 0), (do_u, 2), (do_v, 3), (do_ga, 0), (do_ga, 1), (do_u, 3),
             (do_ga, 2), (do_ga, 3), (do_gb, 0), (do_gb, 1), (do_gb, 2), (do_gb, 3))
    for fn, j in order:
        fn(j)
    gv = jnp.concatenate(gelu_v, axis=1)
    gc = gv - jnp.mean(gv, axis=-1, keepdims=True)
    var = jnp.mean(gc * gc, axis=-1, keepdims=True)
    vln = gc * lax.rsqrt(var + NORM_EPS) * lng_ref[...] + lnb_ref[...]
    vln_ref[...] = _bf16(vln)
    vs_ref[...] = vln


def _rope_tables(pos):
    half = ROT_DIM // 2
    inv = np.float32(ROPE_THETA) ** (-np.arange(half, dtype=np.float32) * np.float32(2.0) / ROT_DIM)
    ang = pos.astype(np.float32)[:, None] * inv.astype(np.float32)[None, :]
    cos = np.cos(ang.astype(np.float64)).astype(np.float32)
    sin = np.sin(ang.astype(np.float64)).astype(np.float32)
    n = pos.shape[0]
    ones = np.ones((n, HEAD_DIM - ROT_DIM), np.float32)
    zeros = np.zeros((n, HEAD_DIM - ROT_DIM), np.float32)
    zh = np.zeros((n, half), np.float32)
    cos_h = np.concatenate([cos, cos, ones], axis=1)
    slo_h = np.concatenate([-sin, zh, zeros], axis=1)
    shi_h = np.concatenate([zh, sin, zeros], axis=1)
    rep = LANES // HEAD_DIM
    return tuple(jnp.asarray(np.tile(a, (1, rep))) for a in (cos_h, slo_h, shi_h))


def _row_spec(width):
    return pl.BlockSpec((ROW_TILE, width), lambda i: (i, 0))


def _const_spec(shape):
    return pl.BlockSpec(shape, lambda i: (0,) * len(shape))


def _prompt_spec(width, n_prompt_tiles):
    return pl.BlockSpec((ROW_TILE, width), lambda i: (jnp.minimum(i, n_prompt_tiles - 1), 0))


def _sample_spec(width, n_prompt_tiles):
    return pl.BlockSpec((ROW_TILE, width), lambda i: (jnp.maximum(i - n_prompt_tiles, 0), 0))


def _params():
    return pltpu.CompilerParams(dimension_semantics=("arbitrary",), vmem_limit_bytes=VMEM_LIMIT)


def _project(xp, xs, g_mix, w_in, b_in, ln_g, ln_b, tables, seq):
    tp, ts = xp.shape[0], xs.shape[0]
    t = tp + ts
    npt = tp // ROW_TILE
    tiles_per_seq = seq // ROW_TILE
    f32, bf16 = jnp.float32, jnp.bfloat16
    table_spec = pl.BlockSpec(
        (ROW_TILE, LANES), lambda i: (jnp.where(i < npt, i % tiles_per_seq, tiles_per_seq), 0))
    out_shape = (
        jax.ShapeDtypeStruct((t, SGU_WIDTH), bf16),
        jax.ShapeDtypeStruct((t, SGU_WIDTH), bf16),
        jax.ShapeDtypeStruct((ts, SGU_WIDTH), f32),
        jax.ShapeDtypeStruct((t, ATT_W), bf16),
        jax.ShapeDtypeStruct((t, KV_W), f32),
        jax.ShapeDtypeStruct((t, KV_W), f32),
        jax.ShapeDtypeStruct((t, KV_DUP_W), bf16),
        jax.ShapeDtypeStruct((t, KV_DUP_W), bf16),
        jax.ShapeDtypeStruct((t, D_MODEL), bf16),
        jax.ShapeDtypeStruct((t, D_MODEL), bf16),
    )
    return pl.pallas_call(
        functools.partial(_proj_kernel, npt),
        out_shape=out_shape,
        grid=(t // ROW_TILE,),
        in_specs=[
            _prompt_spec(D_MODEL, npt), _sample_spec(D_MODEL, npt),
            _const_spec((1, D_MODEL)), _const_spec((D_MODEL, N_IN)),
            _const_spec((1, N_IN)), _const_spec((1, SGU_WIDTH)), _const_spec((1, SGU_WIDTH)),
            table_spec, table_spec, table_spec,
        ],
        out_specs=(
            _row_spec(SGU_WIDTH), _row_spec(SGU_WIDTH), _sample_spec(SGU_WIDTH, npt),
            _row_spec(ATT_W), _row_spec(KV_W), _row_spec(KV_W), _row_spec(KV_DUP_W),
            _row_spec(KV_DUP_W), _row_spec(D_MODEL), _row_spec(D_MODEL),
        ),
        compiler_params=_params(),
        name="proj",
    )(xp, xs, g_mix.reshape(1, -1), w_in.astype(bf16), b_in.reshape(1, -1),
      ln_g.reshape(1, -1), ln_b.reshape(1, -1), *tables)


def _attend(qa, qb, kwin, vwin, sink, valid):
    lo = _lane_lo(CHUNK)
    zero = jnp.zeros_like(qa)
    lhs = jnp.concatenate([jnp.where(lo, qa, zero), jnp.where(lo, zero, qa),
                           jnp.where(lo, qb, zero), jnp.where(lo, zero, qb)], axis=0)
    s = _dot_nt(lhs, kwin)
    if valid is not None:
        s = jnp.where(valid, s, NEG_INF)
    s_a, s_b = s[:, :LANES], s[:, LANES:]
    tail = s_b.shape[1]
    m = jnp.maximum(jnp.max(s, axis=-1, keepdims=True), sink)
    p_a = jnp.exp(s_a - m)
    p_b = jnp.exp(s_b - m[:, :tail])
    denom = (jnp.sum(jnp.concatenate([p_a, p_b], axis=1), axis=-1, keepdims=True)
             + jnp.exp(sink - m))
    inv = 1.0 / denom
    pn = jnp.concatenate([p_a * inv, p_b * inv[:, :tail]], axis=1)
    r = _dot(_bf16(pn), vwin)
    oa = jnp.where(lo, r[0:CHUNK], r[CHUNK:2 * CHUNK])
    ob = jnp.where(lo, r[2 * CHUNK:3 * CHUNK], r[3 * CHUNK:4 * CHUNK])
    return oa, ob


def _stack_rows(rows):
    ri = lax.broadcasted_iota(jnp.int32, (8, rows[0].shape[1]), 0)
    out = jnp.zeros((8, rows[0].shape[1]), rows[0].dtype)
    for k, row in enumerate(rows):
        out = jnp.where(ri == k, row, out)
    return out


def _route_pick(logits_t):
    rows = logits_t.shape[1]
    eid = lax.broadcasted_iota(jnp.int32, (N_EXPERTS, rows), 0)
    work = logits_t
    vals, idxs = [], []
    for _ in range(TOP_K):
        m = jnp.max(work, axis=0, keepdims=True)
        idx = jnp.min(jnp.where(work == m, eid, N_EXPERTS), axis=0, keepdims=True)
        vals.append(m)
        idxs.append(idx)
        work = jnp.where(eid == idx, -jnp.inf, work)
    exps = [jnp.exp(v - vals[0]) for v in vals]
    inv = 1.0 / (exps[0] + exps[1] + exps[2] + exps[3])
    gates = _stack_rows([e * inv for e in exps])

    picked = jnp.zeros((N_EXPERTS, rows), jnp.float32)
    for idx in idxs:
        picked = jnp.where(eid == idx, 1.0, picked)
    tr = lax.broadcasted_iota(jnp.int32, (rows, rows), 0)
    tc = lax.broadcasted_iota(jnp.int32, (rows, rows), 1)
    earlier = _bf16(jnp.where(tr < tc, 1.0, 0.0))
    in_tile = _dot(_bf16(picked), earlier)
    count_col = jnp.broadcast_to(jnp.sum(picked, axis=1, keepdims=True), (N_EXPERTS, LANES))
    eid_wide = lax.broadcasted_iota(jnp.int32, (LANES, rows), 0)
    picked_wide = jnp.zeros((LANES, rows), jnp.float32)
    for idx in idxs:
        picked_wide = jnp.where(eid_wide == idx, 1.0, picked_wide)
    count_row = _dot_nt(jnp.ones((8, rows), jnp.bfloat16), _bf16(picked_wide))
    return idxs, gates, in_tile, count_col, count_row


def _route_place(idxs, in_tile, count_col, count_row, carry_s, live):
    rows = in_tile.shape[1]
    eid = lax.broadcasted_iota(jnp.int32, (N_EXPERTS, rows), 0)
    er = lax.broadcasted_iota(jnp.int32, (N_EXPERTS, N_EXPERTS), 0)
    ec = lax.broadcasted_iota(jnp.int32, (N_EXPERTS, N_EXPERTS), 1)
    start_col = _dot(_bf16(jnp.where(ec < er, 1.0, 0.0)), _bf16(count_col))
    local = in_tile + jnp.concatenate([start_col] * (rows // LANES), axis=1)
    slots = _stack_rows([jnp.sum(jnp.where(eid == idx, local, 0.0), axis=0, keepdims=True)
                         for idx in idxs]).astype(jnp.int32)
    lr = lax.broadcasted_iota(jnp.int32, (LANES, LANES), 0)
    lc = lax.broadcasted_iota(jnp.int32, (LANES, LANES), 1)
    start_row = _dot(_bf16(count_row), _bf16(jnp.where(lr < lc, 1.0, 0.0)))
    ri = lax.broadcasted_iota(jnp.int32, (8, LANES), 0)
    meta = jnp.where(ri == 0, carry_s[...], jnp.where(ri == 1, count_row, jnp.where(ri == 2, start_row, 0.0)))
    carry_s[...] = carry_s[...] + count_row * live
    return slots, meta.astype(jnp.int32)


def _mix_kernel(tiles_per_seq, n_prompt_tiles,
                xp_ref, xs_ref, u_ref, vln_ref, q_ref, kd_ref, vd_ref, kdp_ref, vdp_ref,
                ck_ref, cv_ref, ga_ref, gb_ref, wsp_ref, bsp_ref, sink_ref,
                wpa_ref, wpb_ref, wo_ref, gffn_ref, wrh_ref, wrl_ref, br_ref,
                x1_ref, hloc_ref, slot_ref, gate_ref, meta_ref, count_ref,
                a_s, o_s, kwin_s, vwin_s, carry_s, hhi_s, hlo_s):
    i = pl.program_id(0)
    n_streams = ROW_TILE // CHUNK

    @pl.when(i == 0)
    def _():
        carry_s[...] = jnp.zeros_like(carry_s)
        hhi_s[...] = jnp.zeros_like(hhi_s)
        hlo_s[...] = jnp.zeros_like(hlo_s)

    def sgu_rows(r0, rows):
        ri = lax.broadcasted_iota(jnp.int32, (rows, rows), 0) // CHUNK
        ci = lax.broadcasted_iota(jnp.int32, (rows, rows), 1) // CHUNK
        for g in range(SGU_GROUPS):
            cols = slice(g * LANES, (g + 1) * LANES)
            w = _bf16(jnp.where(ci <= ri, wsp_ref[g, :rows, :rows], 0.0))
            sp = _dot(w, vln_ref[r0:r0 + rows, cols]) + bsp_ref[g, :rows, :]
            a_s[r0:r0 + rows, cols] = _bf16(_gelu(u_ref[r0:r0 + rows, cols].astype(jnp.float32)) * sp)

    def attend_rows(r0, kwin_of, valid):
        for g in range(N_KV_HEADS):
            c0 = g * Q_PER_KV * HEAD_DIM
            kwin, vwin = kwin_of(g)
            oa, ob = _attend(q_ref[r0:r0 + CHUNK, c0:c0 + LANES],
                             q_ref[r0:r0 + CHUNK, c0 + LANES:c0 + 2 * LANES],
                             kwin, vwin, sink_ref[g], valid)
            o_s[r0:r0 + CHUNK, c0:c0 + LANES] = _bf16(oa)
            o_s[r0:r0 + CHUNK, c0 + LANES:c0 + 2 * LANES] = _bf16(ob)

    @pl.when(i < n_prompt_tiles)
    def _prompt():
        for c in range(ROW_TILE // SGU_CHUNK):
            sgu_rows(c * SGU_CHUNK, SGU_CHUNK)
        kwin_s[0:WINDOW] = kdp_ref[...]
        kwin_s[WINDOW:WINDOW + ROW_TILE] = kd_ref[...]
        vwin_s[0:WINDOW] = vdp_ref[...]
        vwin_s[WINDOW:WINDOW + ROW_TILE] = vd_ref[...]
        first = (i % tiles_per_seq) == 0
        col = lax.broadcasted_iota(jnp.int32, (1, KEY_SPAN), 1)
        for j in range(ROW_TILE // CHUNK):
            r0 = j * CHUNK
            valid = jnp.logical_or(jnp.logical_not(first), col + r0 >= WINDOW) if r0 < WINDOW else None

            def kwin_of(g, r0=r0):
                cols = slice(g * LANES, (g + 1) * LANES)
                return kwin_s[r0:r0 + KEY_SPAN, cols], vwin_s[r0:r0 + KEY_SPAN, cols]

            attend_rows(r0, kwin_of, valid)

    @pl.when(i >= n_prompt_tiles)
    def _sample():
        for s in range(n_streams):
            r0 = s * CHUNK
            sgu_rows(r0, CHUNK)
            kwin_s[0:WINDOW] = _dup_heads(ck_ref[s])
            kwin_s[WINDOW:KEY_SPAN] = kd_ref[r0:r0 + CHUNK]
            vwin_s[0:WINDOW] = _dup_heads(cv_ref[s])
            vwin_s[WINDOW:KEY_SPAN] = vd_ref[r0:r0 + CHUNK]

            def kwin_of(g):
                cols = slice(g * LANES, (g + 1) * LANES)
                return kwin_s[0:KEY_SPAN, cols], vwin_s[0:KEY_SPAN, cols]

            attend_rows(r0, kwin_of, None)

    hh, hl = hhi_s[...], hlo_s[...]
    logits_t = (_dot_nt(wrh_ref[...], hh) + _dot_nt(wrl_ref[...], hh) + _dot_nt(wrh_ref[...], hl)
                + jnp.concatenate([br_ref[...]] * (ROW_TILE // LANES), axis=1))
    m_a = ga_ref[...].astype(jnp.float32) * _dot(a_s[...], wpa_ref[...])
    idxs, gates, in_tile, count_col, count_row = _route_pick(logits_t)
    gate_ref[...] = gates
    m = m_a + gb_ref[...].astype(jnp.float32) * _dot(o_s[...], wpb_ref[...])
    slots, meta = _route_place(idxs, in_tile, count_col, count_row, carry_s, jnp.where(i > 0, 1.0, 0.0))
    slot_ref[...] = slots * ROW_SUBTILES
    meta_ref[...] = meta
    count_ref[...] = carry_s[...].astype(jnp.int32)
    n_slots = ROW_TILE * TOP_K
    sid = lax.broadcasted_iota(jnp.int32, (n_slots, ROW_TILE), 0)
    place = jnp.zeros((n_slots, ROW_TILE), jnp.float32)
    for k in range(TOP_K):
        place = jnp.where(sid == slots[k:k + 1, :], 1.0, place)
    _store_row_tiled(hloc_ref, (), _dot(_bf16(place), hh))

    x = jnp.where(i < n_prompt_tiles, xp_ref[...], xs_ref[...])
    x1 = x + _dot(_bf16(m), wo_ref[...])
    x1_ref[...] = x1
    h2 = _rms(x1, gffn_ref[...])
    h2_hi = _bf16(h2)
    hhi_s[...] = h2_hi
    hlo_s[...] = _bf16(h2 - h2_hi.astype(jnp.float32))


def _mix(xp, xs, u, vln, q, kd, vd, cache_k, cache_v, ga, gb, w_sp, b_sp, sinks,
         w_pa, w_pb, w_o, g_ffn, w_router, b_router, seq):
    tp, ts = xp.shape[0], xs.shape[0]
    t = tp + ts
    npt = tp // ROW_TILE
    tiles_per_seq = seq // ROW_TILE
    f32, bf16 = jnp.float32, jnp.bfloat16
    n_streams = ROW_TILE // CHUNK
    win_per_tile = ROW_TILE // WINDOW

    nt = t // ROW_TILE
    cur = lambda i: jnp.minimum(i, nt - 1)
    smp = lambda i: jnp.maximum(cur(i) - npt, 0)
    row = lambda width: pl.BlockSpec((ROW_TILE, width), lambda i: (cur(i), 0))
    prev_spec = pl.BlockSpec(
        (WINDOW, KV_DUP_W), lambda i: (jnp.maximum(jnp.minimum(i, npt - 1) * win_per_tile - 1, 0), 0))
    cache_spec = pl.BlockSpec((n_streams, WINDOW, KV_W), lambda i: (smp(i), 0, 0))
    xs_spec = pl.BlockSpec((ROW_TILE, D_MODEL), lambda i: (smp(i), 0))
    sink_cols = jnp.broadcast_to(
        jnp.repeat(sinks.astype(f32).reshape(N_KV_HEADS, Q_PER_KV), CHUNK, axis=1)[:, :, None],
        (N_KV_HEADS, Q_PER_KV * CHUNK, LANES))
    wr_t = w_router.T
    wr_hi = wr_t.astype(bf16)
    wr_lo = (wr_t - wr_hi.astype(f32)).astype(bf16)
    routed8 = lambda width: pl.BlockSpec((8, width), lambda i: (jnp.maximum(i - 1, 0), 0))
    out_shape = (
        jax.ShapeDtypeStruct((t, D_MODEL), f32),
        jax.ShapeDtypeStruct((t * TOP_K * ROW_SUBTILES, LANES), f32),
        jax.ShapeDtypeStruct((nt * 8, ROW_TILE), jnp.int32),
        jax.ShapeDtypeStruct((nt * 8, ROW_TILE), f32),
        jax.ShapeDtypeStruct((nt * 8, LANES), jnp.int32),
        jax.ShapeDtypeStruct((8, LANES), jnp.int32),
    )
    return pl.pallas_call(
        functools.partial(_mix_kernel, tiles_per_seq, npt),
        out_shape=out_shape,
        grid=(nt + 1,),
        in_specs=[
            _prompt_spec(D_MODEL, npt), xs_spec,
            row(SGU_WIDTH), row(SGU_WIDTH), row(ATT_W),
            row(KV_DUP_W), row(KV_DUP_W), prev_spec, prev_spec,
            cache_spec, cache_spec, row(D_MODEL), row(D_MODEL),
            _const_spec((SGU_GROUPS, SGU_CHUNK, SGU_CHUNK)), _const_spec((SGU_GROUPS, SGU_CHUNK, LANES)),
            _const_spec((N_KV_HEADS, Q_PER_KV * CHUNK, LANES)),
            _const_spec((SGU_WIDTH, D_MODEL)), _const_spec((ATT_W, D_MODEL)),
            _const_spec((D_MODEL, D_MODEL)), _const_spec((1, D_MODEL)),
            _const_spec((N_EXPERTS, D_MODEL)), _const_spec((N_EXPERTS, D_MODEL)),
            _const_spec((N_EXPERTS, LANES)),
        ],
        out_specs=(row(D_MODEL),
                   pl.BlockSpec((ROW_TILE * TOP_K * ROW_SUBTILES, LANES),
                                lambda i: (jnp.maximum(i - 1, 0), 0)),
                   routed8(ROW_TILE), routed8(ROW_TILE), routed8(LANES), _const_spec((8, LANES))),
        scratch_shapes=[
            pltpu.VMEM((ROW_TILE, SGU_WIDTH), bf16), pltpu.VMEM((ROW_TILE, ATT_W), bf16),
            pltpu.VMEM((WINDOW + ROW_TILE, KV_DUP_W), bf16),
            pltpu.VMEM((WINDOW + ROW_TILE, KV_DUP_W), bf16),
            pltpu.VMEM((8, LANES), f32),
            pltpu.VMEM((ROW_TILE, D_MODEL), bf16), pltpu.VMEM((ROW_TILE, D_MODEL), bf16),
        ],
        compiler_params=_params(),
        name="mix",
    )(xp, xs, u, vln, q, kd, vd, kd, vd,
      cache_k.reshape(-1, WINDOW, KV_W), cache_v.reshape(-1, WINDOW, KV_W), ga, gb,
      w_sp, jnp.broadcast_to(b_sp[:, :, None], (SGU_GROUPS, SGU_CHUNK, LANES)), sink_cols,
      w_pa.astype(bf16), w_pb.astype(bf16), w_o.astype(bf16),
      g_ffn.reshape(1, -1), wr_hi, wr_lo,
      jnp.broadcast_to(b_router.astype(f32)[:, None], (N_EXPERTS, LANES)))


def _unrolled_rows(n_rows, fn):
    if isinstance(n_rows, int):
        groups, tail_start = n_rows // ROW_UNROLL, n_rows - n_rows % ROW_UNROLL
    else:
        groups = lax.shift_right_logical(n_rows, ROW_UNROLL.bit_length() - 1)
        tail_start = groups * ROW_UNROLL

    def group(gi, carry):
        for lane in range(ROW_UNROLL):
            fn(gi * ROW_UNROLL + lane, lane)
        return carry

    def tail(r, carry):
        fn(r, 0)
        return carry

    lax.fori_loop(0, groups, group, 0)
    lax.fori_loop(tail_start, n_rows, tail, 0)


def _row_span(first_row, n_rows):
    return pl.ds(pl.multiple_of(first_row * ROW_SUBTILES, ROW_SUBTILES),
                 pl.multiple_of(n_rows * ROW_SUBTILES, ROW_SUBTILES))


def _run_spec(index_of):
    return pl.BlockSpec((1, 1, LANES), lambda i, *_: (index_of(i), 0, 0), memory_space=pltpu.SMEM)


def _expert_kernel(n_token_tiles,
                   te_ref, nu_ref, nx_ref, par_ref, nv_ref, tf_ref, cnt_ref, loc_ref,
                   bgu_ref, bdn_ref, hloc_hbm, wgu_hbm, wdn_hbm,
                   ys_ref,
                   xbuf, wgu_f, wdn_f, wgu_s, wdn_s, walk, xsem, wsem):
    i = pl.program_id(0)
    n_used = nu_ref[0]
    expert = te_ref[i]
    buf = par_ref[i]
    slot = i % 2
    expert_changed = jnp.logical_or(i == 0, expert != te_ref[jnp.maximum(i - 1, 0)])

    def fetch_rows(j, b):
        e = te_ref[j]
        need = nv_ref[j]

        @pl.when(tf_ref[j] == 1)
        def _():
            walk[0] = 0
            walk[1] = 0

        @pl.when(need < MOE_TILE)
        def _():
            xbuf[b] = jnp.zeros(xbuf.shape[1:], xbuf.dtype)

        def unfinished(state):
            filled, tile, _ = state
            return jnp.logical_and(filled < need, tile < n_token_tiles)

        def take_run(state):
            filled, tile, off = state
            run = cnt_ref[tile * N_EXPERTS + e]
            take = jnp.minimum(run - off, need - filled)

            @pl.when(take > 0)
            def _():
                src = tile * (ROW_TILE * TOP_K) + loc_ref[tile * N_EXPERTS + e] + off
                pltpu.make_async_copy(hloc_hbm.at[_row_span(src, take)],
                                      xbuf.at[b, _row_span(filled, take)], xsem.at[b]).start()

            run_done = off + take == run
            return (filled + take, jnp.where(run_done, tile + 1, tile), jnp.where(run_done, 0, off + take))

        _, tile, off = lax.while_loop(unfinished, take_run, (jnp.int32(0), walk[0], walk[1]))
        walk[0] = tile
        walk[1] = off

    @pl.when(jnp.logical_and(i == 0, n_used > 0))
    def _():
        fetch_rows(0, 0)

    @pl.when(i + 1 < n_used)
    def _():
        fetch_rows(i + 1, 1 - slot)

    def weight_copies(e, b):
        return (pltpu.make_async_copy(wgu_hbm.at[e], wgu_f.at[b], wsem.at[0, b]),
                pltpu.make_async_copy(wdn_hbm.at[e], wdn_f.at[b], wsem.at[1, b]))

    @pl.when(jnp.logical_and(i < n_used, expert_changed))
    def _():
        @pl.when(i == 0)
        def _():
            for copy in weight_copies(expert, buf):
                copy.start()

        for copy in weight_copies(expert, buf):
            copy.wait()
        following = nx_ref[i]

        @pl.when(following != expert)
        def _():
            for copy in weight_copies(following, 1 - buf):
                copy.start()

        wgu_s[...] = _bf16(wgu_f[buf])
        wdn_s[...] = _bf16(wdn_f[buf])

    rows = nv_ref[i]

    @pl.when(i < n_used)
    def _():
        pltpu.make_async_copy(hloc_hbm.at[_row_span(0, rows)], xbuf.at[slot, _row_span(0, rows)],
                              xsem.at[slot]).wait()

    def mlp(n):
        x = _bf16(_load_row_tiled(xbuf, (slot,), n))
        gu = _dot(x, wgu_s[...]) + bgu_ref[0]
        gate = jnp.minimum(gu[:, :D_FF], SWIGLU_LIMIT)
        lin = jnp.clip(gu[:, D_FF:], -SWIGLU_LIMIT, SWIGLU_LIMIT)
        act = gate * _sigmoid(SWIGLU_ALPHA * gate) * (lin + 1.0)
        _store_row_tiled(ys_ref, (), _dot(_bf16(act), wdn_s[...]) + bdn_ref[0])

    @pl.when(jnp.logical_and(i < n_used, rows > MOE_TILE // 2))
    def _():
        mlp(MOE_TILE)

    def partial_tile(n):
        mlp(n)
        ys_ref[pl.ds(n * ROW_SUBTILES, (MOE_TILE - n) * ROW_SUBTILES), :] = jnp.zeros(
            ((MOE_TILE - n) * ROW_SUBTILES, LANES), ys_ref.dtype)

    @pl.when(jnp.logical_and(i < n_used, jnp.logical_and(rows > MOE_TILE // 4, rows <= MOE_TILE // 2)))
    def _():
        partial_tile(MOE_TILE // 2)

    @pl.when(jnp.logical_and(i < n_used, rows <= MOE_TILE // 4))
    def _():
        partial_tile(MOE_TILE // 4)

    @pl.when(i >= n_used)
    def _():
        ys_ref[...] = jnp.zeros_like(ys_ref)


def _experts(tile_expert, n_used, next_expert, weight_buf, n_valid, tile_first, run_n, run_loc,
             h_local, w_gu, b_gu, w_dn, b_dn):
    n_tiles = tile_expert.shape[0]
    n_token_tiles = h_local.shape[0] // (ROW_TILE * TOP_K * ROW_SUBTILES)
    f32, bf16 = jnp.float32, jnp.bfloat16
    tile_rows = MOE_TILE * ROW_SUBTILES
    grid_spec = pltpu.PrefetchScalarGridSpec(
        num_scalar_prefetch=8,
        grid=(n_tiles,),
        in_specs=[
            pl.BlockSpec((1, 1, 2 * D_FF), lambda i, te, *_: (te[i], 0, 0)),
            pl.BlockSpec((1, 1, D_MODEL), lambda i, te, *_: (te[i], 0, 0)),
            pl.BlockSpec(memory_space=pl.ANY), pl.BlockSpec(memory_space=pl.ANY),
            pl.BlockSpec(memory_space=pl.ANY),
        ],
        out_specs=pl.BlockSpec((tile_rows, LANES), lambda i, *_: (i, 0)),
        scratch_shapes=[
            pltpu.VMEM((2, tile_rows, LANES), f32),
            pltpu.VMEM((2, D_MODEL, 2 * D_FF), f32), pltpu.VMEM((2, D_FF, D_MODEL), f32),
            pltpu.VMEM((D_MODEL, 2 * D_FF), bf16), pltpu.VMEM((D_FF, D_MODEL), bf16),
            pltpu.SMEM((2,), jnp.int32),
            pltpu.SemaphoreType.DMA((2,)), pltpu.SemaphoreType.DMA((2, 2)),
        ],
    )
    return pl.pallas_call(
        functools.partial(_expert_kernel, n_token_tiles),
        out_shape=jax.ShapeDtypeStruct((n_tiles * tile_rows, LANES), f32),
        grid_spec=grid_spec,
        compiler_params=_params(),
        name="experts",
    )(tile_expert, n_used, next_expert, weight_buf, n_valid, tile_first, run_n, run_loc,
      b_gu.reshape(N_EXPERTS, 1, -1), b_dn.reshape(N_EXPERTS, 1, -1), h_local, w_gu, w_dn)


def _combine_kernel(n_prompt_tiles,
                    src_ref, n_ref, dst_ref, src_nx_ref, n_nx_ref, dst_nx_ref, slot_ref, gate_ref,
                    x1_ref, gfin_ref, ys_hbm,
                    yp_ref, yo_ref,
                    local, mixed, run_sem):
    i = pl.program_id(0)
    last = pl.num_programs(0) - 1
    buf = i % 2

    def fetch_runs(s_ref, c_ref, d_ref, b):
        for e in range(N_EXPERTS):
            n = c_ref[0, 0, e]
            copy = pltpu.make_async_copy(ys_hbm.at[_row_span(d_ref[0, 0, e], n)],
                                         local.at[b, _row_span(s_ref[0, 0, e], n)], run_sem.at[b])
            pl.when(n > 0)(copy.start)

    @pl.when(i == 0)
    def _():
        fetch_runs(src_ref, n_ref, dst_ref, 0)

    @pl.when(i < last)
    def _():
        fetch_runs(src_nx_ref, n_nx_ref, dst_nx_ref, 1 - buf)

    pltpu.make_async_copy(ys_hbm.at[pl.ds(0, local.shape[1])], local.at[buf], run_sem.at[buf]).wait()

    def blend(t, lane):
        acc = None
        for k in range(TOP_K):
            at = pl.multiple_of(slot_ref[0, 0, k * ROW_TILE + t], ROW_SUBTILES)
            term = gate_ref[0, 0, k * ROW_TILE + t] * local[buf, pl.ds(at, ROW_SUBTILES), :]
            acc = term if acc is None else acc + term
        mixed[pl.ds(pl.multiple_of(t * ROW_SUBTILES, ROW_SUBTILES), ROW_SUBTILES), :] = acc
    _unrolled_rows(ROW_TILE, blend)

    out = _rms(x1_ref[...] + _load_row_tiled(mixed, (), ROW_TILE), gfin_ref[...])

    @pl.when(i < n_prompt_tiles)
    def _():
        yp_ref[...] = out

    @pl.when(i >= n_prompt_tiles)
    def _():
        yo_ref[...] = out


def _combine(run_src, run_n, run_dst, slots, gates, x1, ys, g_final, tp):
    t = x1.shape[0]
    npt = tp // ROW_TILE
    nt = t // ROW_TILE
    f32 = jnp.float32
    picks = ROW_TILE * TOP_K
    nxt = lambda i: jnp.minimum(i + 1, nt - 1)
    pick_spec = pl.BlockSpec((1, 1, picks), lambda i: (i, 0, 0), memory_space=pltpu.SMEM)
    return pl.pallas_call(
        functools.partial(_combine_kernel, npt),
        out_shape=(jax.ShapeDtypeStruct((tp, D_MODEL), f32),
                   jax.ShapeDtypeStruct((t - tp, D_MODEL), f32)),
        grid=(nt,),
        in_specs=[_run_spec(lambda i: i), _run_spec(lambda i: i), _run_spec(lambda i: i),
                  _run_spec(nxt), _run_spec(nxt), _run_spec(nxt), pick_spec, pick_spec,
                  _row_spec(D_MODEL), _const_spec((1, D_MODEL)), pl.BlockSpec(memory_space=pl.ANY)],
        out_specs=(_prompt_spec(D_MODEL, npt), _sample_spec(D_MODEL, npt)),
        scratch_shapes=[pltpu.VMEM((2, picks * ROW_SUBTILES, LANES), f32),
                        pltpu.VMEM((ROW_TILE * ROW_SUBTILES, LANES), f32),
                        pltpu.SemaphoreType.DMA((2,))],
        compiler_params=_params(),
        name="combine",
    )(run_src, run_n, run_dst, run_src, run_n, run_dst, slots, gates, x1, g_final.reshape(1, -1), ys)


def _plan(meta, counts, t):
    nt = t // ROW_TILE
    n_tiles = (t * TOP_K + N_EXPERTS * (MOE_TILE - 1)) // MOE_TILE
    counts = counts[0, :N_EXPERTS]
    tiles_e = (counts + MOE_TILE - 1) // MOE_TILE
    tile_end = jnp.cumsum(tiles_e)
    tile_start = tile_end - tiles_e
    n_used = tile_end[-1]
    first_row = jnp.pad(tile_start * MOE_TILE, (0, LANES - N_EXPERTS))
    meta = meta.reshape(nt, 8, LANES)
    run_dst = meta[:, 0:1, :] + first_row[None, None, :]
    run_n = meta[:, 1:2, :]
    run_src = meta[:, 2:3, :]
    tile_ids = jnp.arange(n_tiles, dtype=jnp.int32)
    live = jnp.minimum(tile_ids, n_used - 1)
    tile_expert = jnp.sum(tile_end[None, :] <= live[:, None], axis=1).astype(jnp.int32)
    ids = jnp.arange(N_EXPERTS, dtype=jnp.int32)
    is_expert = tile_expert[:, None] == ids[None, :]
    of_tile = lambda per_expert: jnp.sum(jnp.where(is_expert, per_expert[None, :], 0), axis=1)
    in_expert = tile_ids - of_tile(tile_start)
    n_valid = jnp.clip(of_tile(counts) - in_expert * MOE_TILE, 0, MOE_TILE)
    n_valid = jnp.where(tile_ids < n_used, n_valid, 0).astype(jnp.int32)
    tile_first = jnp.logical_and(in_expert == 0, tile_ids < n_used).astype(jnp.int32)
    run_n_flat = run_n[:, 0, :N_EXPERTS].reshape(-1)
    run_loc_flat = run_src[:, 0, :N_EXPERTS].reshape(-1)
    used = tiles_e > 0
    later_used = jnp.where(jnp.logical_and(used[None, :], ids[None, :] > ids[:, None]), ids[None, :],
                           N_EXPERTS)
    following = jnp.min(later_used, axis=1)
    following = jnp.where(following < N_EXPERTS, following, ids)
    buf_of = (jnp.cumsum(used.astype(jnp.int32)) - 1) % 2
    return (tile_expert, n_used.reshape(1).astype(jnp.int32), of_tile(following).astype(jnp.int32),
            of_tile(buf_of).astype(jnp.int32), n_valid, tile_first, run_n_flat, run_loc_flat,
            run_src, run_n, run_dst)


def kernel(x_prompt, x_sample, cache_k, cache_v, g_mix, w_in, b_in, ln_v_g, ln_v_b, w_sp, b_sp,
           attn_sinks, w_pa, w_pb, w_o, g_ffn, w_router, b_router, w_gu, b_gu, w_dn, b_dn, g_final):
    nb, seq, d = x_prompt.shape
    nsb, nnew, _ = x_sample.shape
    tp, ts = nb * seq, nsb * nnew
    t = tp + ts
    xp = x_prompt.reshape(tp, d)
    xs = x_sample.reshape(ts, d)
    pos = np.concatenate([np.arange(seq), np.tile(PAST_LEN + np.arange(nnew), ROW_TILE // nnew)])
    tables = _rope_tables(pos)
    u, vln, v_sgu, q, k, v, kd, vd, ga, gb = _project(
        xp, xs, g_mix[0], w_in[0], b_in[0], ln_v_g[0], ln_v_b[0], tables, seq)
    x1, h_local, slot_t, gate_t, meta, counts = _mix(
        xp, xs, u, vln, q, kd, vd, cache_k[0], cache_v[0], ga, gb, w_sp[0], b_sp[0], attn_sinks[0],
        w_pa[0], w_pb[0], w_o[0], g_ffn[0], w_router[0], b_router[0], seq)
    nt = t // ROW_TILE
    picks = lambda a: a.reshape(nt, 8, ROW_TILE)[:, :TOP_K, :].reshape(nt, 1, TOP_K * ROW_TILE)
    slots, gates = picks(slot_t), picks(gate_t)
    (tile_expert, n_used, next_expert, weight_buf, n_valid, tile_first, run_n_flat, run_loc_flat,
     run_src, run_n, run_dst) = _plan(meta, counts, t)
    y_sorted = _experts(tile_expert, n_used, next_expert, weight_buf, n_valid, tile_first, run_n_flat,
                        run_loc_flat, h_local, w_gu[0], b_gu[0], w_dn[0], b_dn[0])
    y_p, y_s = _combine(run_src, run_n, run_dst, slots, gates, x1, y_sorted, g_final, tp)

    keep = min(WINDOW, seq)
    tails = lambda a: jnp.stack([a[(b + 1) * seq - keep:(b + 1) * seq] for b in range(nb)]).reshape(
        nb, keep, N_KV_HEADS, HEAD_DIM)
    kp, vp = tails(k), tails(v)
    ks = k[tp:].reshape(nsb, nnew, N_KV_HEADS, HEAD_DIM)
    vs = v[tp:].reshape(nsb, nnew, N_KV_HEADS, HEAD_DIM)
    return (y_p.reshape(nb, seq, d), y_s.reshape(nsb, nnew, d), kp[None], vp[None], ks[None],
            vs[None], v_sgu.reshape(1, nsb, nnew, SGU_WIDTH))
```

```python
import functools

import numpy as np
import jax
import jax.numpy as jnp
from jax import lax
from jax.experimental import pallas as pl
from jax.experimental.pallas import tpu as pltpu

D_MODEL = 1024
PAST_LEN = 2048
CHUNK = 64
SGU_CHUNK = 128
SGU_GROUPS = 8
SGU_WIDTH = 1024
N_HEADS = 16
N_KV_HEADS = 4
HEAD_DIM = 64
Q_PER_KV = N_HEADS // N_KV_HEADS
WINDOW = 128
ROT_DIM = HEAD_DIM // 4
ROPE_THETA = 500000.0
ATT_W = N_HEADS * HEAD_DIM
KV_W = N_KV_HEADS * HEAD_DIM
N_EXPERTS = 32
TOP_K = 4
D_FF = 1024
SWIGLU_ALPHA = 1.702
SWIGLU_LIMIT = 7.0
NORM_EPS = 1e-5
NEG_INF = -1e30
N_IN = SGU_WIDTH * 2 + ATT_W + KV_W * 2 + D_MODEL * 2

LANES = 128
ROW_TILE = 256
MOE_TILE = 512
ROW_UNROLL = 8
COL_BLOCK = 256
KV_DUP_W = N_KV_HEADS * LANES
KEY_SPAN = WINDOW + CHUNK
VMEM_LIMIT = 56 * 1024 * 1024

_SQRT_HALF = 0.7071067811865476
_LOG2_E = 1.4426950408889634


def _gelu(x):
    t = 1.0 / (1.0 + (0.3275911 * _SQRT_HALF) * jnp.abs(x))
    half_poly = t * (0.127414796 + t * (-0.142248368 + t * (0.7107068705
                     + t * (-0.7265760135 + t * 0.5307027145))))
    half_tail = x * (half_poly * jnp.exp2(x * x * (-0.5 * _LOG2_E)))
    return jnp.where(x >= 0.0, x - half_tail, half_tail)


def _sigmoid(x):
    return 1.0 / (1.0 + jnp.exp(-x))


def _bf16(x):
    return x.astype(jnp.bfloat16)


def _dot(a, b):
    return jnp.dot(a, b, preferred_element_type=jnp.float32)


ROW_SUBTILES = D_MODEL // LANES


def _store_row_tiled(ref, lead, x):
    rows = x.shape[0]
    for s in range(ROW_SUBTILES):
        ref[(*lead, pl.ds(s, rows, stride=ROW_SUBTILES), slice(None))] = x[:, s * LANES:(s + 1) * LANES]


def _load_row_tiled(ref, lead, rows):
    return jnp.concatenate(
        [ref[(*lead, pl.ds(s, rows, stride=ROW_SUBTILES), slice(None))] for s in range(ROW_SUBTILES)],
        axis=1)


def _dot_nt(a, b):
    return lax.dot_general(a, b, (((1,), (1,)), ((), ())), preferred_element_type=jnp.float32)


def _rms(x, g):
    return x * lax.rsqrt(jnp.mean(x * x, axis=-1, keepdims=True) + NORM_EPS) * g


def _lane_lo(rows):
    return lax.broadcasted_iota(jnp.int32, (rows, LANES), 1) < HEAD_DIM


def _dup_heads(kv):
    rows = kv.shape[0]
    lo = _lane_lo(rows)
    out = []
    for j in range(KV_W // LANES):
        blk = kv[:, j * LANES:(j + 1) * LANES]
        swp = pltpu.roll(blk, HEAD_DIM, axis=1)
        out.append(jnp.where(lo, blk, swp))
        out.append(jnp.where(lo, swp, blk))
    return _bf16(jnp.concatenate(out, axis=1))


def _rope_block(zb, cos_b, sin_lo, sin_hi):
    up = pltpu.roll(zb, LANES - ROT_DIM // 2, axis=1)
    dn = pltpu.roll(zb, ROT_DIM // 2, axis=1)
    return zb * cos_b + up * sin_lo + dn * sin_hi


def _proj_kernel(n_prompt_tiles, xp_ref, xs_ref, gmix_ref, w_ref, b_ref, lng_ref, lnb_ref,
                 cos_ref, slo_ref, shi_ref,
                 u_ref, vln_ref, vs_ref, q_ref, k_ref, v_ref, kd_ref, vd_ref, ga_ref, gb_ref):
    i = pl.program_id(0)
    x = jnp.where(i < n_prompt_tiles, xp_ref[...], xs_ref[...])
    h = _bf16(_rms(x, gmix_ref[...]))

    cos_b, sin_lo, sin_hi = cos_ref[...], slo_ref[...], shi_ref[...]
    off_u, off_v, off_q, off_k, off_vv, off_ga, off_gb = (
        int(o) for o in np.cumsum((0, SGU_WIDTH, SGU_WIDTH, ATT_W, KV_W, KV_W, D_MODEL)))

    def z(lo):
        return _dot(h, w_ref[:, lo:lo + COL_BLOCK]) + b_ref[:, lo:lo + COL_BLOCK]

    def cols(j):
        return slice(j * COL_BLOCK, (j + 1) * COL_BLOCK)

    def rope(zb):
        return jnp.concatenate(
            [_rope_block(zb[:, t * LANES:(t + 1) * LANES], cos_b, sin_lo, sin_hi)
             for t in range(COL_BLOCK // LANES)], axis=1)

    gelu_v = []

    def do_u(j):
        u_ref[:, cols(j)] = _bf16(z(off_u + j * COL_BLOCK))

    def do_v(j):
        gelu_v.append(_gelu(z(off_v + j * COL_BLOCK)))

    def do_q(j):
        q_ref[:, cols(j)] = _bf16(rope(z(off_q + j * COL_BLOCK)) * (HEAD_DIM ** -0.5))

    def do_k(j):
        kr = rope(z(off_k))
        k_ref[...] = kr
        kd_ref[...] = _dup_heads(kr)

    def do_vv(j):
        zv = z(off_vv)
        v_ref[...] = zv
        vd_ref[...] = _dup_heads(zv)

    def do_ga(j):
        ga_ref[:, cols(j)] = _bf16(_sigmoid(z(off_ga + j * COL_BLOCK)))

    def do_gb(j):
        gb_ref[:, cols(j)] = _bf16(_sigmoid(z(off_gb + j * COL_BLOCK)))

    order = ((do_v, 0), (do_q, 0), (do_q, 1), (do_u, 0), (do_v, 1), (do_q, 2), (do_q, 3), (do_u, 1),
             (do_v, 2), (do_k, 0), (do_vv, 0), (do_u, 2), (do_v, 3), (do_ga, 0), (do_ga, 1), (do_u, 3),
             (do_ga, 2), (do_ga, 3), (do_gb, 0), (do_gb, 1), (do_gb, 2), (do_gb, 3))
    for fn, j in order:
        fn(j)
    gv = jnp.concatenate(gelu_v, axis=1)
    gc = gv - jnp.mean(gv, axis=-1, keepdims=True)
    var = jnp.mean(gc * gc, axis=-1, keepdims=True)
    vln = gc * lax.rsqrt(var + NORM_EPS) * lng_ref[...] + lnb_ref[...]
    vln_ref[...] = _bf16(vln)
    vs_ref[...] = vln


def _rope_tables(pos):
    half = ROT_DIM // 2
    inv = np.float32(ROPE_THETA) ** (-np.arange(half, dtype=np.float32) * np.float32(2.0) / ROT_DIM)
    ang = pos.astype(np.float32)[:, None] * inv.astype(np.float32)[None, :]
    cos = np.cos(ang.astype(np.float64)).astype(np.float32)
    sin = np.sin(ang.astype(np.float64)).astype(np.float32)
    n = pos.shape[0]
    ones = np.ones((n, HEAD_DIM - ROT_DIM), np.float32)
    zeros = np.zeros((n, HEAD_DIM - ROT_DIM), np.float32)
    zh = np.zeros((n, half), np.float32)
    cos_h = np.concatenate([cos, cos, ones], axis=1)
    slo_h = np.concatenate([-sin, zh, zeros], axis=1)
    shi_h = np.concatenate([zh, sin, zeros], axis=1)
    rep = LANES // HEAD_DIM
    return tuple(jnp.asarray(np.tile(a, (1, rep))) for a in (cos_h, slo_h, shi_h))


def _row_spec(width):
    return pl.BlockSpec((ROW_TILE, width), lambda i: (i, 0))


def _const_spec(shape):
    return pl.BlockSpec(shape, lambda i: (0,) * len(shape))


def _prompt_spec(width, n_prompt_tiles):
    return pl.BlockSpec((ROW_TILE, width), lambda i: (jnp.minimum(i, n_prompt_tiles - 1), 0))


def _sample_spec(width, n_prompt_tiles):
    return pl.BlockSpec((ROW_TILE, width), lambda i: (jnp.maximum(i - n_prompt_tiles, 0), 0))


def _params():
    return pltpu.CompilerParams(dimension_semantics=("arbitrary",), vmem_limit_bytes=VMEM_LIMIT)


def _project(xp, xs, g_mix, w_in, b_in, ln_g, ln_b, tables, seq):
    tp, ts = xp.shape[0], xs.shape[0]
    t = tp + ts
    npt = tp // ROW_TILE
    tiles_per_seq = seq // ROW_TILE
    f32, bf16 = jnp.float32, jnp.bfloat16
    table_spec = pl.BlockSpec(
        (ROW_TILE, LANES), lambda i: (jnp.where(i < npt, i % tiles_per_seq, tiles_per_seq), 0))
    out_shape = (
        jax.ShapeDtypeStruct((t, SGU_WIDTH), bf16),
        jax.ShapeDtypeStruct((t, SGU_WIDTH), bf16),
        jax.ShapeDtypeStruct((ts, SGU_WIDTH), f32),
        jax.ShapeDtypeStruct((t, ATT_W), bf16),
        jax.ShapeDtypeStruct((t, KV_W), f32),
        jax.ShapeDtypeStruct((t, KV_W), f32),
        jax.ShapeDtypeStruct((t, KV_DUP_W), bf16),
        jax.ShapeDtypeStruct((t, KV_DUP_W), bf16),
        jax.ShapeDtypeStruct((t, D_MODEL), bf16),
        jax.ShapeDtypeStruct((t, D_MODEL), bf16),
    )
    return pl.pallas_call(
        functools.partial(_proj_kernel, npt),
        out_shape=out_shape,
        grid=(t // ROW_TILE,),
        in_specs=[
            _prompt_spec(D_MODEL, npt), _sample_spec(D_MODEL, npt),
            _const_spec((1, D_MODEL)), _const_spec((D_MODEL, N_IN)),
            _const_spec((1, N_IN)), _const_spec((1, SGU_WIDTH)), _const_spec((1, SGU_WIDTH)),
            table_spec, table_spec, table_spec,
        ],
        out_specs=(
            _row_spec(SGU_WIDTH), _row_spec(SGU_WIDTH), _sample_spec(SGU_WIDTH, npt),
            _row_spec(ATT_W), _row_spec(KV_W), _row_spec(KV_W), _row_spec(KV_DUP_W),
            _row_spec(KV_DUP_W), _row_spec(D_MODEL), _row_spec(D_MODEL),
        ),
        compiler_params=_params(),
        name="proj",
    )(xp, xs, g_mix.reshape(1, -1), w_in.astype(bf16), b_in.reshape(1, -1),
      ln_g.reshape(1, -1), ln_b.reshape(1, -1), *tables)


def _attend(qa, qb, kwin, vwin, sink, valid):
    lo = _lane_lo(CHUNK)
    zero = jnp.zeros_like(qa)
    lhs = jnp.concatenate([jnp.where(lo, qa, zero), jnp.where(lo, zero, qa),
                           jnp.where(lo, qb, zero), jnp.where(lo, zero, qb)], axis=0)
    s = _dot_nt(lhs, kwin)
    if valid is not None:
        s = jnp.where(valid, s, NEG_INF)
    s_a, s_b = s[:, :LANES], s[:, LANES:]
    tail = s_b.shape[1]
    m = jnp.maximum(jnp.max(s, axis=-1, keepdims=True), sink)
    p_a = jnp.exp(s_a - m)
    p_b = jnp.exp(s_b - m[:, :tail])
    denom = (jnp.sum(jnp.concatenate([p_a, p_b], axis=1), axis=-1, keepdims=True)
             + jnp.exp(sink - m))
    inv = 1.0 / denom
    pn = jnp.concatenate([p_a * inv, p_b * inv[:, :tail]], axis=1)
    r = _dot(_bf16(pn), vwin)
    oa = jnp.where(lo, r[0:CHUNK], r[CHUNK:2 * CHUNK])
    ob = jnp.where(lo, r[2 * CHUNK:3 * CHUNK], r[3 * CHUNK:4 * CHUNK])
    return oa, ob


def _stack_rows(rows):
    ri = lax.broadcasted_iota(jnp.int32, (8, rows[0].shape[1]), 0)
    out = jnp.zeros((8, rows[0].shape[1]), rows[0].dtype)
    for k, row in enumerate(rows):
        out = jnp.where(ri == k, row, out)
    return out


def _route_pick(logits_t):
    rows = logits_t.shape[1]
    eid = lax.broadcasted_iota(jnp.int32, (N_EXPERTS, rows), 0)
    work = logits_t
    vals, idxs = [], []
    for _ in range(TOP_K):
        m = jnp.max(work, axis=0, keepdims=True)
        idx = jnp.min(jnp.where(work == m, eid, N_EXPERTS), axis=0, keepdims=True)
        vals.append(m)
        idxs.append(idx)
        work = jnp.where(eid == idx, -jnp.inf, work)
    exps = [jnp.exp(v - vals[0]) for v in vals]
    inv = 1.0 / (exps[0] + exps[1] + exps[2] + exps[3])
    gates = _stack_rows([e * inv for e in exps])

    picked = jnp.zeros((N_EXPERTS, rows), jnp.float32)
    for idx in idxs:
        picked = jnp.where(eid == idx, 1.0, picked)
    tr = lax.broadcasted_iota(jnp.int32, (rows, rows), 0)
    tc = lax.broadcasted_iota(jnp.int32, (rows, rows), 1)
    earlier = _bf16(jnp.where(tr < tc, 1.0, 0.0))
    in_tile = _dot(_bf16(picked), earlier)
    count_col = jnp.broadcast_to(jnp.sum(picked, axis=1, keepdims=True), (N_EXPERTS, LANES))
    eid_wide = lax.broadcasted_iota(jnp.int32, (LANES, rows), 0)
    picked_wide = jnp.zeros((LANES, rows), jnp.float32)
    for idx in idxs:
        picked_wide = jnp.where(eid_wide == idx, 1.0, picked_wide)
    count_row = _dot_nt(jnp.ones((8, rows), jnp.bfloat16), _bf16(picked_wide))
    return idxs, gates, in_tile, count_col, count_row


def _route_place(idxs, in_tile, count_col, count_row, carry_s, live):
    rows = in_tile.shape[1]
    eid = lax.broadcasted_iota(jnp.int32, (N_EXPERTS, rows), 0)
    er = lax.broadcasted_iota(jnp.int32, (N_EXPERTS, N_EXPERTS), 0)
    ec = lax.broadcasted_iota(jnp.int32, (N_EXPERTS, N_EXPERTS), 1)
    start_col = _dot(_bf16(jnp.where(ec < er, 1.0, 0.0)), _bf16(count_col))
    local = in_tile + jnp.concatenate([start_col] * (rows // LANES), axis=1)
    slots = _stack_rows([jnp.sum(jnp.where(eid == idx, local, 0.0), axis=0, keepdims=True)
                         for idx in idxs]).astype(jnp.int32)
    lr = lax.broadcasted_iota(jnp.int32, (LANES, LANES), 0)
    lc = lax.broadcasted_iota(jnp.int32, (LANES, LANES), 1)
    start_row = _dot(_bf16(count_row), _bf16(jnp.where(lr < lc, 1.0, 0.0)))
    ri = lax.broadcasted_iota(jnp.int32, (8, LANES), 0)
    meta = jnp.where(ri == 0, carry_s[...], jnp.where(ri == 1, count_row, jnp.where(ri == 2, start_row, 0.0)))
    carry_s[...] = carry_s[...] + count_row * live
    return slots, meta.astype(jnp.int32)


def _mix_kernel(tiles_per_seq, n_prompt_tiles,
                xp_ref, xs_ref, u_ref, vln_ref, q_ref, kd_ref, vd_ref, kdp_ref, vdp_ref,
                ck_ref, cv_ref, ga_ref, gb_ref, wsp_ref, bsp_ref, sink_ref,
                wpa_ref, wpb_ref, wo_ref, gffn_ref, wrh_ref, wrl_ref, br_ref,
                x1_ref, hloc_ref, slot_ref, gate_ref, meta_ref, count_ref,
                a_s, o_s, kwin_s, vwin_s, carry_s, hhi_s, hlo_s):
    i = pl.program_id(0)
    n_streams = ROW_TILE // CHUNK

    @pl.when(i == 0)
    def _():
        carry_s[...] = jnp.zeros_like(carry_s)
        hhi_s[...] = jnp.zeros_like(hhi_s)
        hlo_s[...] = jnp.zeros_like(hlo_s)

    def sgu_rows(r0, rows):
        ri = lax.broadcasted_iota(jnp.int32, (rows, rows), 0) // CHUNK
        ci = lax.broadcasted_iota(jnp.int32, (rows, rows), 1) // CHUNK
        for g in range(SGU_GROUPS):
            cols = slice(g * LANES, (g + 1) * LANES)
            w = _bf16(jnp.where(ci <= ri, wsp_ref[g, :rows, :rows], 0.0))
            sp = _dot(w, vln_ref[r0:r0 + rows, cols]) + bsp_ref[g, :rows, :]
            a_s[r0:r0 + rows, cols] = _bf16(_gelu(u_ref[r0:r0 + rows, cols].astype(jnp.float32)) * sp)

    def attend_rows(r0, kwin_of, valid):
        for g in range(N_KV_HEADS):
            c0 = g * Q_PER_KV * HEAD_DIM
            kwin, vwin = kwin_of(g)
            oa, ob = _attend(q_ref[r0:r0 + CHUNK, c0:c0 + LANES],
                             q_ref[r0:r0 + CHUNK, c0 + LANES:c0 + 2 * LANES],
                             kwin, vwin, sink_ref[g], valid)
            o_s[r0:r0 + CHUNK, c0:c0 + LANES] = _bf16(oa)
            o_s[r0:r0 + CHUNK, c0 + LANES:c0 + 2 * LANES] = _bf16(ob)

    @pl.when(i < n_prompt_tiles)
    def _prompt():
        for c in range(ROW_TILE // SGU_CHUNK):
            sgu_rows(c * SGU_CHUNK, SGU_CHUNK)
        kwin_s[0:WINDOW] = kdp_ref[...]
        kwin_s[WINDOW:WINDOW + ROW_TILE] = kd_ref[...]
        vwin_s[0:WINDOW] = vdp_ref[...]
        vwin_s[WINDOW:WINDOW + ROW_TILE] = vd_ref[...]
        first = (i % tiles_per_seq) == 0
        col = lax.broadcasted_iota(jnp.int32, (1, KEY_SPAN), 1)
        for j in range(ROW_TILE // CHUNK):
            r0 = j * CHUNK
            valid = jnp.logical_or(jnp.logical_not(first), col + r0 >= WINDOW) if r0 < WINDOW else None

            def kwin_of(g, r0=r0):
                cols = slice(g * LANES, (g + 1) * LANES)
                return kwin_s[r0:r0 + KEY_SPAN, cols], vwin_s[r0:r0 + KEY_SPAN, cols]

            attend_rows(r0, kwin_of, valid)

    @pl.when(i >= n_prompt_tiles)
    def _sample():
        for s in range(n_streams):
            r0 = s * CHUNK
            sgu_rows(r0, CHUNK)
            kwin_s[0:WINDOW] = _dup_heads(ck_ref[s])
            kwin_s[WINDOW:KEY_SPAN] = kd_ref[r0:r0 + CHUNK]
            vwin_s[0:WINDOW] = _dup_heads(cv_ref[s])
            vwin_s[WINDOW:KEY_SPAN] = vd_ref[r0:r0 + CHUNK]

            def kwin_of(g):
                cols = slice(g * LANES, (g + 1) * LANES)
                return kwin_s[0:KEY_SPAN, cols], vwin_s[0:KEY_SPAN, cols]

            attend_rows(r0, kwin_of, None)

    hh, hl = hhi_s[...], hlo_s[...]
    logits_t = (_dot_nt(wrh_ref[...], hh) + _dot_nt(wrl_ref[...], hh) + _dot_nt(wrh_ref[...], hl)
                + jnp.concatenate([br_ref[...]] * (ROW_TILE // LANES), axis=1))
    m_a = ga_ref[...].astype(jnp.float32) * _dot(a_s[...], wpa_ref[...])
    idxs, gates, in_tile, count_col, count_row = _route_pick(logits_t)
    gate_ref[...] = gates
    m = m_a + gb_ref[...].astype(jnp.float32) * _dot(o_s[...], wpb_ref[...])
    slots, meta = _route_place(idxs, in_tile, count_col, count_row, carry_s, jnp.where(i > 0, 1.0, 0.0))
    half = ((i + 1) % 2) * (ROW_TILE * TOP_K * ROW_SUBTILES)
    slot_ref[...] = slots * ROW_SUBTILES + half
    meta_ref[...] = meta
    count_ref[...] = carry_s[...].astype(jnp.int32)
    n_slots = ROW_TILE * TOP_K
    sid = lax.broadcasted_iota(jnp.int32, (n_slots, ROW_TILE), 0)
    place = jnp.zeros((n_slots, ROW_TILE), jnp.float32)
    for k in range(TOP_K):
        place = jnp.where(sid == slots[k:k + 1, :], 1.0, place)
    _store_row_tiled(hloc_ref, (), _dot(_bf16(place), hh))

    x = jnp.where(i < n_prompt_tiles, xp_ref[...], xs_ref[...])
    x1 = x + _dot(_bf16(m), wo_ref[...])
    x1_ref[...] = x1
    h2 = _rms(x1, gffn_ref[...])
    h2_hi = _bf16(h2)
    hhi_s[...] = h2_hi
    hlo_s[...] = _bf16(h2 - h2_hi.astype(jnp.float32))


def _mix(xp, xs, u, vln, q, kd, vd, cache_k, cache_v, ga, gb, w_sp, b_sp, sinks,
         w_pa, w_pb, w_o, g_ffn, w_router, b_router, seq):
    tp, ts = xp.shape[0], xs.shape[0]
    t = tp + ts
    npt = tp // ROW_TILE
    tiles_per_seq = seq // ROW_TILE
    f32, bf16 = jnp.float32, jnp.bfloat16
    n_streams = ROW_TILE // CHUNK
    win_per_tile = ROW_TILE // WINDOW

    nt = t // ROW_TILE
    cur = lambda i: jnp.minimum(i, nt - 1)
    smp = lambda i: jnp.maximum(cur(i) - npt, 0)
    row = lambda width: pl.BlockSpec((ROW_TILE, width), lambda i: (cur(i), 0))
    prev_spec = pl.BlockSpec(
        (WINDOW, KV_DUP_W), lambda i: (jnp.maximum(jnp.minimum(i, npt - 1) * win_per_tile - 1, 0), 0))
    cache_spec = pl.BlockSpec((n_streams, WINDOW, KV_W), lambda i: (smp(i), 0, 0))
    xs_spec = pl.BlockSpec((ROW_TILE, D_MODEL), lambda i: (smp(i), 0))
    sink_cols = jnp.broadcast_to(
        jnp.repeat(sinks.astype(f32).reshape(N_KV_HEADS, Q_PER_KV), CHUNK, axis=1)[:, :, None],
        (N_KV_HEADS, Q_PER_KV * CHUNK, LANES))
    wr_t = w_router.T
    wr_hi = wr_t.astype(bf16)
    wr_lo = (wr_t - wr_hi.astype(f32)).astype(bf16)
    routed8 = lambda width: pl.BlockSpec((8, width), lambda i: (jnp.maximum(i - 1, 0), 0))
    out_shape = (
        jax.ShapeDtypeStruct((t, D_MODEL), f32),
        jax.ShapeDtypeStruct((t * TOP_K * ROW_SUBTILES, LANES), f32),
        jax.ShapeDtypeStruct((nt * 8, ROW_TILE), jnp.int32),
        jax.ShapeDtypeStruct((nt * 8, ROW_TILE), f32),
        jax.ShapeDtypeStruct((nt * 8, LANES), jnp.int32),
        jax.ShapeDtypeStruct((8, LANES), jnp.int32),
    )
    return pl.pallas_call(
        functools.partial(_mix_kernel, tiles_per_seq, npt),
        out_shape=out_shape,
        grid=(nt + 1,),
        in_specs=[
            _prompt_spec(D_MODEL, npt), xs_spec,
            row(SGU_WIDTH), row(SGU_WIDTH), row(ATT_W),
            row(KV_DUP_W), row(KV_DUP_W), prev_spec, prev_spec,
            cache_spec, cache_spec, row(D_MODEL), row(D_MODEL),
            _const_spec((SGU_GROUPS, SGU_CHUNK, SGU_CHUNK)), _const_spec((SGU_GROUPS, SGU_CHUNK, LANES)),
            _const_spec((N_KV_HEADS, Q_PER_KV * CHUNK, LANES)),
            _const_spec((SGU_WIDTH, D_MODEL)), _const_spec((ATT_W, D_MODEL)),
            _const_spec((D_MODEL, D_MODEL)), _const_spec((1, D_MODEL)),
            _const_spec((N_EXPERTS, D_MODEL)), _const_spec((N_EXPERTS, D_MODEL)),
            _const_spec((N_EXPERTS, LANES)),
        ],
        out_specs=(row(D_MODEL),
                   pl.BlockSpec((ROW_TILE * TOP_K * ROW_SUBTILES, LANES),
                                lambda i: (jnp.maximum(i - 1, 0), 0)),
                   routed8(ROW_TILE), routed8(ROW_TILE), routed8(LANES), _const_spec((8, LANES))),
        scratch_shapes=[
            pltpu.VMEM((ROW_TILE, SGU_WIDTH), bf16), pltpu.VMEM((ROW_TILE, ATT_W), bf16),
            pltpu.VMEM((WINDOW + ROW_TILE, KV_DUP_W), bf16),
            pltpu.VMEM((WINDOW + ROW_TILE, KV_DUP_W), bf16),
            pltpu.VMEM((8, LANES), f32),
            pltpu.VMEM((ROW_TILE, D_MODEL), bf16), pltpu.VMEM((ROW_TILE, D_MODEL), bf16),
        ],
        compiler_params=_params(),
        name="mix",
    )(xp, xs, u, vln, q, kd, vd, kd, vd,
      cache_k.reshape(-1, WINDOW, KV_W), cache_v.reshape(-1, WINDOW, KV_W), ga, gb,
      w_sp, jnp.broadcast_to(b_sp[:, :, None], (SGU_GROUPS, SGU_CHUNK, LANES)), sink_cols,
      w_pa.astype(bf16), w_pb.astype(bf16), w_o.astype(bf16),
      g_ffn.reshape(1, -1), wr_hi, wr_lo,
      jnp.broadcast_to(b_router.astype(f32)[:, None], (N_EXPERTS, LANES)))


def _unrolled_rows(n_rows, fn):
    if isinstance(n_rows, int):
        groups, tail_start = n_rows // ROW_UNROLL, n_rows - n_rows % ROW_UNROLL
    else:
        groups = lax.shift_right_logical(n_rows, ROW_UNROLL.bit_length() - 1)
        tail_start = groups * ROW_UNROLL

    def group(gi, carry):
        for lane in range(ROW_UNROLL):
            fn(gi * ROW_UNROLL + lane, lane)
        return carry

    def tail(r, carry):
        fn(r, 0)
        return carry

    lax.fori_loop(0, groups, group, 0)
    lax.fori_loop(tail_start, n_rows, tail, 0)


def _row_span(first_row, n_rows):
    return pl.ds(pl.multiple_of(first_row * ROW_SUBTILES, ROW_SUBTILES),
                 pl.multiple_of(n_rows * ROW_SUBTILES, ROW_SUBTILES))


def _run_spec(index_of):
    return pl.BlockSpec((1, 1, LANES), lambda i, *_: (index_of(i), 0, 0), memory_space=pltpu.SMEM)


def _expert_kernel(n_token_tiles,
                   te_ref, nu_ref, nx_ref, par_ref, nv_ref, tf_ref, cnt_ref, loc_ref,
                   bgu_ref, bdn_ref, hloc_hbm, wgu_hbm, wdn_hbm,
                   ys_ref,
                   xbuf, wgu_f, wdn_f, wgu_s, wdn_s, walk, xsem, wsem):
    i = pl.program_id(0)
    n_used = nu_ref[0]
    expert = te_ref[i]
    buf = par_ref[i]
    slot = i % 2
    expert_changed = jnp.logical_or(i == 0, expert != te_ref[jnp.maximum(i - 1, 0)])

    def fetch_rows(j, b):
        e = te_ref[j]
        need = nv_ref[j]

        @pl.when(tf_ref[j] == 1)
        def _():
            walk[0] = 0
            walk[1] = 0

        @pl.when(need < MOE_TILE)
        def _():
            xbuf[b] = jnp.zeros(xbuf.shape[1:], xbuf.dtype)

        def unfinished(state):
            filled, tile, _ = state
            return jnp.logical_and(filled < need, tile < n_token_tiles)

        def take_run(state):
            filled, tile, off = state
            run = cnt_ref[tile * N_EXPERTS + e]
            take = jnp.minimum(run - off, need - filled)

            @pl.when(take > 0)
            def _():
                src = tile * (ROW_TILE * TOP_K) + loc_ref[tile * N_EXPERTS + e] + off
                pltpu.make_async_copy(hloc_hbm.at[_row_span(src, take)],
                                      xbuf.at[b, _row_span(filled, take)], xsem.at[b]).start()

            run_done = off + take == run
            return (filled + take, jnp.where(run_done, tile + 1, tile), jnp.where(run_done, 0, off + take))

        _, tile, off = lax.while_loop(unfinished, take_run, (jnp.int32(0), walk[0], walk[1]))
        walk[0] = tile
        walk[1] = off

    @pl.when(jnp.logical_and(i == 0, n_used > 0))
    def _():
        fetch_rows(0, 0)

    @pl.when(i + 1 < n_used)
    def _():
        fetch_rows(i + 1, 1 - slot)

    def weight_copies(e, b):
        return (pltpu.make_async_copy(wgu_hbm.at[e], wgu_f.at[b], wsem.at[0, b]),
                pltpu.make_async_copy(wdn_hbm.at[e], wdn_f.at[b], wsem.at[1, b]))

    @pl.when(jnp.logical_and(i < n_used, expert_changed))
    def _():
        @pl.when(i == 0)
        def _():
            for copy in weight_copies(expert, buf):
                copy.start()

        for copy in weight_copies(expert, buf):
            copy.wait()
        following = nx_ref[i]

        @pl.when(following != expert)
        def _():
            for copy in weight_copies(following, 1 - buf):
                copy.start()

        wgu_s[...] = _bf16(wgu_f[buf])
        wdn_s[...] = _bf16(wdn_f[buf])

    rows = nv_ref[i]

    @pl.when(i < n_used)
    def _():
        pltpu.make_async_copy(hloc_hbm.at[_row_span(0, rows)], xbuf.at[slot, _row_span(0, rows)],
                              xsem.at[slot]).wait()

    def mlp(n):
        x = _bf16(_load_row_tiled(xbuf, (slot,), n))
        gu = _dot(x, wgu_s[...]) + bgu_ref[0]
        gate = jnp.minimum(gu[:, :D_FF], SWIGLU_LIMIT)
        lin = jnp.clip(gu[:, D_FF:], -SWIGLU_LIMIT, SWIGLU_LIMIT)
        act = gate * _sigmoid(SWIGLU_ALPHA * gate) * (lin + 1.0)
        _store_row_tiled(ys_ref, (), _dot(_bf16(act), wdn_s[...]) + bdn_ref[0])

    @pl.when(jnp.logical_and(i < n_used, rows > MOE_TILE // 2))
    def _():
        mlp(MOE_TILE)

    @pl.when(jnp.logical_and(i < n_used, rows <= MOE_TILE // 2))
    def _():
        mlp(MOE_TILE // 2)
        ys_ref[pl.ds(MOE_TILE // 2 * ROW_SUBTILES, MOE_TILE // 2 * ROW_SUBTILES), :] = jnp.zeros(
            (MOE_TILE // 2 * ROW_SUBTILES, LANES), ys_ref.dtype)

    @pl.when(i >= n_used)
    def _():
        ys_ref[...] = jnp.zeros_like(ys_ref)


def _experts(tile_expert, n_used, next_expert, weight_buf, n_valid, tile_first, run_n, run_loc,
             h_local, w_gu, b_gu, w_dn, b_dn):
    n_tiles = tile_expert.shape[0]
    n_token_tiles = h_local.shape[0] // (ROW_TILE * TOP_K * ROW_SUBTILES)
    f32, bf16 = jnp.float32, jnp.bfloat16
    tile_rows = MOE_TILE * ROW_SUBTILES
    grid_spec = pltpu.PrefetchScalarGridSpec(
        num_scalar_prefetch=8,
        grid=(n_tiles,),
        in_specs=[
            pl.BlockSpec((1, 1, 2 * D_FF), lambda i, te, *_: (te[i], 0, 0)),
            pl.BlockSpec((1, 1, D_MODEL), lambda i, te, *_: (te[i], 0, 0)),
            pl.BlockSpec(memory_space=pl.ANY), pl.BlockSpec(memory_space=pl.ANY),
            pl.BlockSpec(memory_space=pl.ANY),
        ],
        out_specs=pl.BlockSpec((tile_rows, LANES), lambda i, *_: (i, 0)),
        scratch_shapes=[
            pltpu.VMEM((2, tile_rows, LANES), f32),
            pltpu.VMEM((2, D_MODEL, 2 * D_FF), f32), pltpu.VMEM((2, D_FF, D_MODEL), f32),
            pltpu.VMEM((D_MODEL, 2 * D_FF), bf16), pltpu.VMEM((D_FF, D_MODEL), bf16),
            pltpu.SMEM((2,), jnp.int32),
            pltpu.SemaphoreType.DMA((2,)), pltpu.SemaphoreType.DMA((2, 2)),
        ],
    )
    return pl.pallas_call(
        functools.partial(_expert_kernel, n_token_tiles),
        out_shape=jax.ShapeDtypeStruct((n_tiles * tile_rows, LANES), f32),
        grid_spec=grid_spec,
        compiler_params=_params(),
        name="experts",
    )(tile_expert, n_used, next_expert, weight_buf, n_valid, tile_first, run_n, run_loc,
      b_gu.reshape(N_EXPERTS, 1, -1), b_dn.reshape(N_EXPERTS, 1, -1), h_local, w_gu, w_dn)


def _combine_kernel(n_prompt_tiles,
                    src_ref, n_ref, dst_ref, src_nx_ref, n_nx_ref, dst_nx_ref, slot_ref, gate_ref,
                    x1_ref, gfin_ref, ys_hbm,
                    yp_ref, yo_ref,
                    local, mixed, run_sem):
    i = pl.program_id(0)
    last = pl.num_programs(0) - 1
    buf = i % 2
    half_slots = ROW_TILE * TOP_K
    half_rows = half_slots * ROW_SUBTILES

    def fetch_runs(s_ref, c_ref, d_ref, b):
        for e in range(N_EXPERTS):
            n = c_ref[0, 0, e]
            copy = pltpu.make_async_copy(ys_hbm.at[_row_span(d_ref[0, 0, e], n)],
                                         local.at[_row_span(b * half_slots + s_ref[0, 0, e], n)],
                                         run_sem.at[b])
            pl.when(n > 0)(copy.start)

    @pl.when(i == 0)
    def _():
        fetch_runs(src_ref, n_ref, dst_ref, 0)

    @pl.when(i < last)
    def _():
        fetch_runs(src_nx_ref, n_nx_ref, dst_nx_ref, 1 - buf)

    pltpu.make_async_copy(ys_hbm.at[pl.ds(0, half_rows)],
                          local.at[pl.ds(pl.multiple_of(buf * half_rows, half_rows), half_rows)],
                          run_sem.at[buf]).wait()

    def blend(t, lane):
        acc = None
        for k in range(TOP_K):
            at = pl.multiple_of(slot_ref[0, 0, k * ROW_TILE + t], ROW_SUBTILES)
            term = gate_ref[0, 0, k * ROW_TILE + t] * local[pl.ds(at, ROW_SUBTILES), :]
            acc = term if acc is None else acc + term
        mixed[pl.ds(pl.multiple_of(t * ROW_SUBTILES, ROW_SUBTILES), ROW_SUBTILES), :] = acc
    _unrolled_rows(ROW_TILE, blend)

    out = _rms(x1_ref[...] + _load_row_tiled(mixed, (), ROW_TILE), gfin_ref[...])

    @pl.when(i < n_prompt_tiles)
    def _():
        yp_ref[...] = out

    @pl.when(i >= n_prompt_tiles)
    def _():
        yo_ref[...] = out


def _combine(run_src, run_n, run_dst, slots, gates, x1, ys, g_final, tp):
    t = x1.shape[0]
    npt = tp // ROW_TILE
    nt = t // ROW_TILE
    f32 = jnp.float32
    picks = ROW_TILE * TOP_K
    nxt = lambda i: jnp.minimum(i + 1, nt - 1)
    pick_spec = pl.BlockSpec((1, 1, picks), lambda i: (i, 0, 0), memory_space=pltpu.SMEM)
    return pl.pallas_call(
        functools.partial(_combine_kernel, npt),
        out_shape=(jax.ShapeDtypeStruct((tp, D_MODEL), f32),
                   jax.ShapeDtypeStruct((t - tp, D_MODEL), f32)),
        grid=(nt,),
        in_specs=[_run_spec(lambda i: i), _run_spec(lambda i: i), _run_spec(lambda i: i),
                  _run_spec(nxt), _run_spec(nxt), _run_spec(nxt), pick_spec, pick_spec,
                  _row_spec(D_MODEL), _const_spec((1, D_MODEL)), pl.BlockSpec(memory_space=pl.ANY)],
        out_specs=(_prompt_spec(D_MODEL, npt), _sample_spec(D_MODEL, npt)),
        scratch_shapes=[pltpu.VMEM((2 * picks * ROW_SUBTILES, LANES), f32),
                        pltpu.VMEM((ROW_TILE * ROW_SUBTILES, LANES), f32),
                        pltpu.SemaphoreType.DMA((2,))],
        compiler_params=_params(),
        name="combine",
    )(run_src, run_n, run_dst, run_src, run_n, run_dst, slots, gates, x1, g_final.reshape(1, -1), ys)


def _plan(meta, counts, t):
    nt = t // ROW_TILE
    n_tiles = (t * TOP_K + N_EXPERTS * (MOE_TILE - 1)) // MOE_TILE
    counts = counts[0, :N_EXPERTS]
    tiles_e = (counts + MOE_TILE - 1) // MOE_TILE
    tile_end = jnp.cumsum(tiles_e)
    tile_start = tile_end - tiles_e
    n_used = tile_end[-1]
    first_row = jnp.pad(tile_start * MOE_TILE, (0, LANES - N_EXPERTS))
    meta = meta.reshape(nt, 8, LANES)
    run_dst = meta[:, 0:1, :] + first_row[None, None, :]
    run_n = meta[:, 1:2, :]
    run_src = meta[:, 2:3, :]
    tile_ids = jnp.arange(n_tiles, dtype=jnp.int32)
    live = jnp.minimum(tile_ids, n_used - 1)
    tile_expert = jnp.sum(tile_end[None, :] <= live[:, None], axis=1).astype(jnp.int32)
    ids = jnp.arange(N_EXPERTS, dtype=jnp.int32)
    is_expert = tile_expert[:, None] == ids[None, :]
    of_tile = lambda per_expert: jnp.sum(jnp.where(is_expert, per_expert[None, :], 0), axis=1)
    in_expert = tile_ids - of_tile(tile_start)
    n_valid = jnp.clip(of_tile(counts) - in_expert * MOE_TILE, 0, MOE_TILE)
    n_valid = jnp.where(tile_ids < n_used, n_valid, 0).astype(jnp.int32)
    tile_first = jnp.logical_and(in_expert == 0, tile_ids < n_used).astype(jnp.int32)
    run_n_flat = run_n[:, 0, :N_EXPERTS].reshape(-1)
    run_loc_flat = run_src[:, 0, :N_EXPERTS].reshape(-1)
    used = tiles_e > 0
    later_used = jnp.where(jnp.logical_and(used[None, :], ids[None, :] > ids[:, None]), ids[None, :],
                           N_EXPERTS)
    following = jnp.min(later_used, axis=1)
    following = jnp.where(following < N_EXPERTS, following, ids)
    buf_of = (jnp.cumsum(used.astype(jnp.int32)) - 1) % 2
    return (tile_expert, n_used.reshape(1).astype(jnp.int32), of_tile(following).astype(jnp.int32),
            of_tile(buf_of).astype(jnp.int32), n_valid, tile_first, run_n_flat, run_loc_flat,
            run_src, run_n, run_dst)


def kernel(x_prompt, x_sample, cache_k, cache_v, g_mix, w_in, b_in, ln_v_g, ln_v_b, w_sp, b_sp,
           attn_sinks, w_pa, w_pb, w_o, g_ffn, w_router, b_router, w_gu, b_gu, w_dn, b_dn, g_final):
    nb, seq, d = x_prompt.shape
    nsb, nnew, _ = x_sample.shape
    tp, ts = nb * seq, nsb * nnew
    t = tp + ts
    xp = x_prompt.reshape(tp, d)
    xs = x_sample.reshape(ts, d)
    pos = np.concatenate([np.arange(seq), np.tile(PAST_LEN + np.arange(nnew), ROW_TILE // nnew)])
    tables = _rope_tables(pos)
    u, vln, v_sgu, q, k, v, kd, vd, ga, gb = _project(
        xp, xs, g_mix[0], w_in[0], b_in[0], ln_v_g[0], ln_v_b[0], tables, seq)
    x1, h_local, slot_t, gate_t, meta, counts = _mix(
        xp, xs, u, vln, q, kd, vd, cache_k[0], cache_v[0], ga, gb, w_sp[0], b_sp[0], attn_sinks[0],
        w_pa[0], w_pb[0], w_o[0], g_ffn[0], w_router[0], b_router[0], seq)
    nt = t // ROW_TILE
    picks = lambda a: a.reshape(nt, 8, ROW_TILE)[:, :TOP_K, :].reshape(nt, 1, TOP_K * ROW_TILE)
    slots, gates = picks(slot_t), picks(gate_t)
    (tile_expert, n_used, next_expert, weight_buf, n_valid, tile_first, run_n_flat, run_loc_flat,
     run_src, run_n, run_dst) = _plan(meta, counts, t)
    y_sorted = _experts(tile_expert, n_used, next_expert, weight_buf, n_valid, tile_first, run_n_flat,
                        run_loc_flat, h_local, w_gu[0], b_gu[0], w_dn[0], b_dn[0])
    y_p, y_s = _combine(run_src, run_n, run_dst, slots, gates, x1, y_sorted, g_final, tp)

    keep = min(WINDOW, seq)
    tails = lambda a: jnp.stack([a[(b + 1) * seq - keep:(b + 1) * seq] for b in range(nb)]).reshape(
        nb, keep, N_KV_HEADS, HEAD_DIM)
    kp, vp = tails(k), tails(v)
    ks = k[tp:].reshape(nsb, nnew, N_KV_HEADS, HEAD_DIM)
    vs = v[tp:].reshape(nsb, nnew, N_KV_HEADS, HEAD_DIM)
    return (y_p.reshape(nb, seq, d), y_s.reshape(nsb, nnew, d), kp[None], vp[None], ks[None],
            vs[None], v_sgu.reshape(1, nsb, nnew, SGU_WIDTH))
```

```python
import functools

import numpy as np
import jax
import jax.numpy as jnp
from jax import lax
from jax.experimental import pallas as pl
from jax.experimental.pallas import tpu as pltpu

D_MODEL = 1024
PAST_LEN = 2048
CHUNK = 64
SGU_CHUNK = 128
SGU_GROUPS = 8
SGU_WIDTH = 1024
N_HEADS = 16
N_KV_HEADS = 4
HEAD_DIM = 64
Q_PER_KV = N_HEADS // N_KV_HEADS
WINDOW = 128
ROT_DIM = HEAD_DIM // 4
ROPE_THETA = 500000.0
ATT_W = N_HEADS * HEAD_DIM
KV_W = N_KV_HEADS * HEAD_DIM
N_EXPERTS = 32
TOP_K = 4
D_FF = 1024
SWIGLU_ALPHA = 1.702
SWIGLU_LIMIT = 7.0
NORM_EPS = 1e-5
NEG_INF = -1e30
N_IN = SGU_WIDTH * 2 + ATT_W + KV_W * 2 + D_MODEL * 2

LANES = 128
ROW_TILE = 256
MOE_TILE = 512
ROW_UNROLL = 8
COL_BLOCK = 256
KV_DUP_W = N_KV_HEADS * LANES
KEY_SPAN = WINDOW + CHUNK
VMEM_LIMIT = 56 * 1024 * 1024

_SQRT_HALF = 0.7071067811865476
_LOG2_E = 1.4426950408889634


def _gelu(x):
    t = 1.0 / (1.0 + (0.3275911 * _SQRT_HALF) * jnp.abs(x))
    half_poly = t * (0.127414796 + t * (-0.142248368 + t * (0.7107068705
                     + t * (-0.7265760135 + t * 0.5307027145))))
    half_tail = x * (half_poly * jnp.exp2(x * x * (-0.5 * _LOG2_E)))
    return jnp.where(x >= 0.0, x - half_tail, half_tail)


def _sigmoid(x):
    return 1.0 / (1.0 + jnp.exp(-x))


def _bf16(x):
    return x.astype(jnp.bfloat16)


def _dot(a, b):
    return jnp.dot(a, b, preferred_element_type=jnp.float32)


ROW_SUBTILES = D_MODEL // LANES


def _store_row_tiled(ref, lead, x):
    rows = x.shape[0]
    for s in range(ROW_SUBTILES):
        ref[(*lead, pl.ds(s, rows, stride=ROW_SUBTILES), slice(None))] = x[:, s * LANES:(s + 1) * LANES]


def _load_row_tiled(ref, lead, rows):
    return jnp.concatenate(
        [ref[(*lead, pl.ds(s, rows, stride=ROW_SUBTILES), slice(None))] for s in range(ROW_SUBTILES)],
        axis=1)


def _dot_nt(a, b):
    return lax.dot_general(a, b, (((1,), (1,)), ((), ())), preferred_element_type=jnp.float32)


def _rms(x, g):
    return x * lax.rsqrt(jnp.mean(x * x, axis=-1, keepdims=True) + NORM_EPS) * g


def _lane_lo(rows):
    return lax.broadcasted_iota(jnp.int32, (rows, LANES), 1) < HEAD_DIM


def _dup_heads(kv):
    rows = kv.shape[0]
    lo = _lane_lo(rows)
    out = []
    for j in range(KV_W // LANES):
        blk = kv[:, j * LANES:(j + 1) * LANES]
        swp = pltpu.roll(blk, HEAD_DIM, axis=1)
        out.append(jnp.where(lo, blk, swp))
        out.append(jnp.where(lo, swp, blk))
    return _bf16(jnp.concatenate(out, axis=1))


def _rope_block(zb, cos_b, sin_lo, sin_hi):
    up = pltpu.roll(zb, LANES - ROT_DIM // 2, axis=1)
    dn = pltpu.roll(zb, ROT_DIM // 2, axis=1)
    return zb * cos_b + up * sin_lo + dn * sin_hi


def _proj_kernel(n_prompt_tiles, xp_ref, xs_ref, gmix_ref, w_ref, b_ref, lng_ref, lnb_ref,
                 cos_ref, slo_ref, shi_ref,
                 u_ref, vln_ref, vs_ref, q_ref, k_ref, v_ref, kd_ref, vd_ref, ga_ref, gb_ref):
    i = pl.program_id(0)
    x = jnp.where(i < n_prompt_tiles, xp_ref[...], xs_ref[...])
    h = _bf16(_rms(x, gmix_ref[...]))

    cos_b, sin_lo, sin_hi = cos_ref[...], slo_ref[...], shi_ref[...]
    off_u, off_v, off_q, off_k, off_vv, off_ga, off_gb = (
        int(o) for o in np.cumsum((0, SGU_WIDTH, SGU_WIDTH, ATT_W, KV_W, KV_W, D_MODEL)))

    def z(lo):
        return _dot(h, w_ref[:, lo:lo + COL_BLOCK]) + b_ref[:, lo:lo + COL_BLOCK]

    def cols(j):
        return slice(j * COL_BLOCK, (j + 1) * COL_BLOCK)

    def rope(zb):
        return jnp.concatenate(
            [_rope_block(zb[:, t * LANES:(t + 1) * LANES], cos_b, sin_lo, sin_hi)
             for t in range(COL_BLOCK // LANES)], axis=1)

    gelu_v = []

    def do_u(j):
        u_ref[:, cols(j)] = _bf16(z(off_u + j * COL_BLOCK))

    def do_v(j):
        gelu_v.append(_gelu(z(off_v + j * COL_BLOCK)))

    def do_q(j):
        q_ref[:, cols(j)] = _bf16(rope(z(off_q + j * COL_BLOCK)) * (HEAD_DIM ** -0.5 * _LOG2_E))

    def do_k(j):
        kr = rope(z(off_k))
        k_ref[...] = kr
        kd_ref[...] = _dup_heads(kr)

    def do_vv(j):
        zv = z(off_vv)
        v_ref[...] = zv
        vd_ref[...] = _dup_heads(zv)

    def do_ga(j):
        ga_ref[:, cols(j)] = _bf16(_sigmoid(z(off_ga + j * COL_BLOCK)))

    def do_gb(j):
        gb_ref[:, cols(j)] = _bf16(_sigmoid(z(off_gb + j * COL_BLOCK)))

    order = ((do_v, 0), (do_q, 0), (do_q, 1), (do_u, 0), (do_v, 1), (do_q, 2), (do_q, 3), (do_u, 1),
             (do_v, 2), (do_k, 0), (do_vv, 0), (do_u, 2), (do_v, 3), (do_ga, 0), (do_ga, 1), (do_u, 3),
             (do_ga, 2), (do_ga, 3), (do_gb, 0), (do_gb, 1), (do_gb, 2), (do_gb, 3))
    for fn, j in order:
        fn(j)
    gv = jnp.concatenate(gelu_v, axis=1)
    gc = gv - jnp.mean(gv, axis=-1, keepdims=True)
    var = jnp.mean(gc * gc, axis=-1, keepdims=True)
    vln = gc * lax.rsqrt(var + NORM_EPS) * lng_ref[...] + lnb_ref[...]
    vln_ref[...] = _bf16(vln)
    vs_ref[...] = vln


def _rope_tables(pos):
    half = ROT_DIM // 2
    inv = np.float32(ROPE_THETA) ** (-np.arange(half, dtype=np.float32) * np.float32(2.0) / ROT_DIM)
    ang = pos.astype(np.float32)[:, None] * inv.astype(np.float32)[None, :]
    cos = np.cos(ang.astype(np.float64)).astype(np.float32)
    sin = np.sin(ang.astype(np.float64)).astype(np.float32)
    n = pos.shape[0]
    ones = np.ones((n, HEAD_DIM - ROT_DIM), np.float32)
    zeros = np.zeros((n, HEAD_DIM - ROT_DIM), np.float32)
    zh = np.zeros((n, half), np.float32)
    cos_h = np.concatenate([cos, cos, ones], axis=1)
    slo_h = np.concatenate([-sin, zh, zeros], axis=1)
    shi_h = np.concatenate([zh, sin, zeros], axis=1)
    rep = LANES // HEAD_DIM
    return tuple(jnp.asarray(np.tile(a, (1, rep))) for a in (cos_h, slo_h, shi_h))


def _row_spec(width):
    return pl.BlockSpec((ROW_TILE, width), lambda i: (i, 0))


def _const_spec(shape):
    return pl.BlockSpec(shape, lambda i: (0,) * len(shape))


def _prompt_spec(width, n_prompt_tiles):
    return pl.BlockSpec((ROW_TILE, width), lambda i: (jnp.minimum(i, n_prompt_tiles - 1), 0))


def _sample_spec(width, n_prompt_tiles):
    return pl.BlockSpec((ROW_TILE, width), lambda i: (jnp.maximum(i - n_prompt_tiles, 0), 0))


def _params():
    return pltpu.CompilerParams(dimension_semantics=("arbitrary",), vmem_limit_bytes=VMEM_LIMIT)


def _project(xp, xs, g_mix, w_in, b_in, ln_g, ln_b, tables, seq):
    tp, ts = xp.shape[0], xs.shape[0]
    t = tp + ts
    npt = tp // ROW_TILE
    tiles_per_seq = seq // ROW_TILE
    f32, bf16 = jnp.float32, jnp.bfloat16
    table_spec = pl.BlockSpec(
        (ROW_TILE, LANES), lambda i: (jnp.where(i < npt, i % tiles_per_seq, tiles_per_seq), 0))
    out_shape = (
        jax.ShapeDtypeStruct((t, SGU_WIDTH), bf16),
        jax.ShapeDtypeStruct((t, SGU_WIDTH), bf16),
        jax.ShapeDtypeStruct((ts, SGU_WIDTH), f32),
        jax.ShapeDtypeStruct((t, ATT_W), bf16),
        jax.ShapeDtypeStruct((t, KV_W), f32),
        jax.ShapeDtypeStruct((t, KV_W), f32),
        jax.ShapeDtypeStruct((t, KV_DUP_W), bf16),
        jax.ShapeDtypeStruct((t, KV_DUP_W), bf16),
        jax.ShapeDtypeStruct((t, D_MODEL), bf16),
        jax.ShapeDtypeStruct((t, D_MODEL), bf16),
    )
    return pl.pallas_call(
        functools.partial(_proj_kernel, npt),
        out_shape=out_shape,
        grid=(t // ROW_TILE,),
        in_specs=[
            _prompt_spec(D_MODEL, npt), _sample_spec(D_MODEL, npt),
            _const_spec((1, D_MODEL)), _const_spec((D_MODEL, N_IN)),
            _const_spec((1, N_IN)), _const_spec((1, SGU_WIDTH)), _const_spec((1, SGU_WIDTH)),
            table_spec, table_spec, table_spec,
        ],
        out_specs=(
            _row_spec(SGU_WIDTH), _row_spec(SGU_WIDTH), _sample_spec(SGU_WIDTH, npt),
            _row_spec(ATT_W), _row_spec(KV_W), _row_spec(KV_W), _row_spec(KV_DUP_W),
            _row_spec(KV_DUP_W), _row_spec(D_MODEL), _row_spec(D_MODEL),
        ),
        compiler_params=_params(),
        name="proj",
    )(xp, xs, g_mix.reshape(1, -1), w_in.astype(bf16), b_in.reshape(1, -1),
      ln_g.reshape(1, -1), ln_b.reshape(1, -1), *tables)


def _attend(qa, qb, kwin, vwin, sink, valid):
    lo = _lane_lo(CHUNK)
    zero = jnp.zeros_like(qa)
    lhs = jnp.concatenate([jnp.where(lo, qa, zero), jnp.where(lo, zero, qa),
                           jnp.where(lo, qb, zero), jnp.where(lo, zero, qb)], axis=0)
    s = _dot_nt(lhs, kwin)
    if valid is not None:
        s = jnp.where(valid, s, NEG_INF)
    s_a, s_b = s[:, :LANES], s[:, LANES:]
    tail = s_b.shape[1]
    m = jnp.maximum(jnp.max(s, axis=-1, keepdims=True), sink)
    p_a = jnp.exp2(s_a - m)
    p_b = jnp.exp2(s_b - m[:, :tail])
    denom = (jnp.sum(jnp.concatenate([p_a, p_b], axis=1), axis=-1, keepdims=True)
             + jnp.exp2(sink - m))
    inv = 1.0 / denom
    pn = jnp.concatenate([p_a * inv, p_b * inv[:, :tail]], axis=1)
    r = _dot(_bf16(pn), vwin)
    oa = jnp.where(lo, r[0:CHUNK], r[CHUNK:2 * CHUNK])
    ob = jnp.where(lo, r[2 * CHUNK:3 * CHUNK], r[3 * CHUNK:4 * CHUNK])
    return oa, ob


def _stack_rows(rows):
    ri = lax.broadcasted_iota(jnp.int32, (8, rows[0].shape[1]), 0)
    out = jnp.zeros((8, rows[0].shape[1]), rows[0].dtype)
    for k, row in enumerate(rows):
        out = jnp.where(ri == k, row, out)
    return out


def _route_pick(logits_t):
    rows = logits_t.shape[1]
    eid = lax.broadcasted_iota(jnp.int32, (N_EXPERTS, rows), 0)
    work = logits_t
    vals, idxs = [], []
    for _ in range(TOP_K):
        m = jnp.max(work, axis=0, keepdims=True)
        idx = jnp.min(jnp.where(work == m, eid, N_EXPERTS), axis=0, keepdims=True)
        vals.append(m)
        idxs.append(idx)
        work = jnp.where(eid == idx, -jnp.inf, work)
    exps = [jnp.exp(v - vals[0]) for v in vals]
    inv = 1.0 / (exps[0] + exps[1] + exps[2] + exps[3])
    gates = _stack_rows([e * inv for e in exps])

    picked = jnp.zeros((N_EXPERTS, rows), jnp.float32)
    for idx in idxs:
        picked = jnp.where(eid == idx, 1.0, picked)
    tr = lax.broadcasted_iota(jnp.int32, (rows, rows), 0)
    tc = lax.broadcasted_iota(jnp.int32, (rows, rows), 1)
    earlier = _bf16(jnp.where(tr < tc, 1.0, 0.0))
    in_tile = _dot(_bf16(picked), earlier)
    count_col = jnp.broadcast_to(jnp.sum(picked, axis=1, keepdims=True), (N_EXPERTS, LANES))
    eid_wide = lax.broadcasted_iota(jnp.int32, (LANES, rows), 0)
    picked_wide = jnp.zeros((LANES, rows), jnp.float32)
    for idx in idxs:
        picked_wide = jnp.where(eid_wide == idx, 1.0, picked_wide)
    count_row = _dot_nt(jnp.ones((8, rows), jnp.bfloat16), _bf16(picked_wide))
    return idxs, gates, in_tile, count_col, count_row


def _route_place(idxs, in_tile, count_col, count_row, carry_s, live):
    rows = in_tile.shape[1]
    eid = lax.broadcasted_iota(jnp.int32, (N_EXPERTS, rows), 0)
    er = lax.broadcasted_iota(jnp.int32, (N_EXPERTS, N_EXPERTS), 0)
    ec = lax.broadcasted_iota(jnp.int32, (N_EXPERTS, N_EXPERTS), 1)
    start_col = _dot(_bf16(jnp.where(ec < er, 1.0, 0.0)), _bf16(count_col))
    local = in_tile + jnp.concatenate([start_col] * (rows // LANES), axis=1)
    slots = _stack_rows([jnp.sum(jnp.where(eid == idx, local, 0.0), axis=0, keepdims=True)
                         for idx in idxs]).astype(jnp.int32)
    lr = lax.broadcasted_iota(jnp.int32, (LANES, LANES), 0)
    lc = lax.broadcasted_iota(jnp.int32, (LANES, LANES), 1)
    start_row = _dot(_bf16(count_row), _bf16(jnp.where(lr < lc, 1.0, 0.0)))
    ri = lax.broadcasted_iota(jnp.int32, (8, LANES), 0)
    meta = jnp.where(ri == 0, carry_s[...], jnp.where(ri == 1, count_row, jnp.where(ri == 2, start_row, 0.0)))
    carry_s[...] = carry_s[...] + count_row * live
    return slots, meta.astype(jnp.int32)


def _mix_kernel(tiles_per_seq, n_prompt_tiles,
                xp_ref, xs_ref, u_ref, vln_ref, q_ref, kd_ref, vd_ref, kdp_ref, vdp_ref,
                ck_ref, cv_ref, ga_ref, gb_ref, wsp_ref, bsp_ref, sink_ref,
                wpa_ref, wpb_ref, wo_ref, gffn_ref, wrh_ref, wrl_ref, br_ref,
                x1_ref, hloc_ref, slot_ref, gate_ref, meta_ref, count_ref,
                a_s, o_s, kwin_s, vwin_s, carry_s, hhi_s, hlo_s):
    i = pl.program_id(0)
    n_streams = ROW_TILE // CHUNK

    @pl.when(i == 0)
    def _():
        carry_s[...] = jnp.zeros_like(carry_s)
        hhi_s[...] = jnp.zeros_like(hhi_s)
        hlo_s[...] = jnp.zeros_like(hlo_s)

    def sgu_rows(r0, rows):
        ri = lax.broadcasted_iota(jnp.int32, (rows, rows), 0) // CHUNK
        ci = lax.broadcasted_iota(jnp.int32, (rows, rows), 1) // CHUNK
        for g in range(SGU_GROUPS):
            cols = slice(g * LANES, (g + 1) * LANES)
            w = _bf16(jnp.where(ci <= ri, wsp_ref[g, :rows, :rows], 0.0))
            sp = _dot(w, vln_ref[r0:r0 + rows, cols]) + bsp_ref[g, :rows, :]
            a_s[r0:r0 + rows, cols] = _bf16(_gelu(u_ref[r0:r0 + rows, cols].astype(jnp.float32)) * sp)

    def attend_rows(r0, kwin_of, valid):
        for g in range(N_KV_HEADS):
            c0 = g * Q_PER_KV * HEAD_DIM
            kwin, vwin = kwin_of(g)
            oa, ob = _attend(q_ref[r0:r0 + CHUNK, c0:c0 + LANES],
                             q_ref[r0:r0 + CHUNK, c0 + LANES:c0 + 2 * LANES],
                             kwin, vwin, sink_ref[g], valid)
            o_s[r0:r0 + CHUNK, c0:c0 + LANES] = _bf16(oa)
            o_s[r0:r0 + CHUNK, c0 + LANES:c0 + 2 * LANES] = _bf16(ob)

    @pl.when(i < n_prompt_tiles)
    def _prompt():
        for c in range(ROW_TILE // SGU_CHUNK):
            sgu_rows(c * SGU_CHUNK, SGU_CHUNK)
        kwin_s[0:WINDOW] = kdp_ref[...]
        kwin_s[WINDOW:WINDOW + ROW_TILE] = kd_ref[...]
        vwin_s[0:WINDOW] = vdp_ref[...]
        vwin_s[WINDOW:WINDOW + ROW_TILE] = vd_ref[...]
        first = (i % tiles_per_seq) == 0
        col = lax.broadcasted_iota(jnp.int32, (1, KEY_SPAN), 1)
        for j in range(ROW_TILE // CHUNK):
            r0 = j * CHUNK
            valid = jnp.logical_or(jnp.logical_not(first), col + r0 >= WINDOW) if r0 < WINDOW else None

            def kwin_of(g, r0=r0):
                cols = slice(g * LANES, (g + 1) * LANES)
                return kwin_s[r0:r0 + KEY_SPAN, cols], vwin_s[r0:r0 + KEY_SPAN, cols]

            attend_rows(r0, kwin_of, valid)

    @pl.when(jnp.logical_and(i >= n_prompt_tiles, i < pl.num_programs(0) - 1))
    def _sample():
        for s in range(n_streams):
            r0 = s * CHUNK
            sgu_rows(r0, CHUNK)
            kwin_s[0:WINDOW] = _dup_heads(ck_ref[s])
            kwin_s[WINDOW:KEY_SPAN] = kd_ref[r0:r0 + CHUNK]
            vwin_s[0:WINDOW] = _dup_heads(cv_ref[s])
            vwin_s[WINDOW:KEY_SPAN] = vd_ref[r0:r0 + CHUNK]

            def kwin_of(g):
                cols = slice(g * LANES, (g + 1) * LANES)
                return kwin_s[0:KEY_SPAN, cols], vwin_s[0:KEY_SPAN, cols]

            attend_rows(r0, kwin_of, None)

    hh, hl = hhi_s[...], hlo_s[...]
    logits_t = (_dot_nt(wrh_ref[...], hh) + _dot_nt(wrl_ref[...], hh) + _dot_nt(wrh_ref[...], hl)
                + jnp.concatenate([br_ref[...]] * (ROW_TILE // LANES), axis=1))
    m_a = ga_ref[...].astype(jnp.float32) * _dot(a_s[...], wpa_ref[...])
    idxs, gates, in_tile, count_col, count_row = _route_pick(logits_t)
    gate_ref[...] = gates
    m = m_a + gb_ref[...].astype(jnp.float32) * _dot(o_s[...], wpb_ref[...])
    slots, meta = _route_place(idxs, in_tile, count_col, count_row, carry_s, jnp.where(i > 0, 1.0, 0.0))
    half = ((i + 1) % 2) * (ROW_TILE * TOP_K * ROW_SUBTILES)
    slot_ref[...] = slots * ROW_SUBTILES + half
    meta_ref[...] = meta
    count_ref[...] = carry_s[...].astype(jnp.int32)
    n_slots = ROW_TILE * TOP_K
    sid = lax.broadcasted_iota(jnp.int32, (n_slots, ROW_TILE), 0)
    place = jnp.zeros((n_slots, ROW_TILE), jnp.float32)
    for k in range(TOP_K):
        place = jnp.where(sid == slots[k:k + 1, :], 1.0, place)
    _store_row_tiled(hloc_ref, (), _dot(_bf16(place), hh))

    x = jnp.where(i < n_prompt_tiles, xp_ref[...], xs_ref[...])
    x1 = x + _dot(_bf16(m), wo_ref[...])
    x1_ref[...] = x1
    h2 = _rms(x1, gffn_ref[...])
    h2_hi = _bf16(h2)
    hhi_s[...] = h2_hi
    hlo_s[...] = _bf16(h2 - h2_hi.astype(jnp.float32))


def _mix(xp, xs, u, vln, q, kd, vd, cache_k, cache_v, ga, gb, w_sp, b_sp, sinks,
         w_pa, w_pb, w_o, g_ffn, w_router, b_router, seq):
    tp, ts = xp.shape[0], xs.shape[0]
    t = tp + ts
    npt = tp // ROW_TILE
    tiles_per_seq = seq // ROW_TILE
    f32, bf16 = jnp.float32, jnp.bfloat16
    n_streams = ROW_TILE // CHUNK
    win_per_tile = ROW_TILE // WINDOW

    nt = t // ROW_TILE
    cur = lambda i: jnp.minimum(i, nt - 1)
    smp = lambda i: jnp.maximum(cur(i) - npt, 0)
    row = lambda width: pl.BlockSpec((ROW_TILE, width), lambda i: (cur(i), 0))
    prev_spec = pl.BlockSpec(
        (WINDOW, KV_DUP_W), lambda i: (jnp.maximum(jnp.minimum(i, npt - 1) * win_per_tile - 1, 0), 0))
    cache_spec = pl.BlockSpec((n_streams, WINDOW, KV_W), lambda i: (smp(i), 0, 0))
    xs_spec = pl.BlockSpec((ROW_TILE, D_MODEL), lambda i: (smp(i), 0))
    sink_cols = jnp.broadcast_to(
        jnp.repeat(sinks.astype(f32).reshape(N_KV_HEADS, Q_PER_KV) * _LOG2_E, CHUNK, axis=1)[:, :, None],
        (N_KV_HEADS, Q_PER_KV * CHUNK, LANES))
    wr_t = w_router.T
    wr_hi = wr_t.astype(bf16)
    wr_lo = (wr_t - wr_hi.astype(f32)).astype(bf16)
    routed8 = lambda width: pl.BlockSpec((8, width), lambda i: (jnp.maximum(i - 1, 0), 0))
    out_shape = (
        jax.ShapeDtypeStruct((t, D_MODEL), f32),
        jax.ShapeDtypeStruct((t * TOP_K * ROW_SUBTILES, LANES), f32),
        jax.ShapeDtypeStruct((nt * 8, ROW_TILE), jnp.int32),
        jax.ShapeDtypeStruct((nt * 8, ROW_TILE), f32),
        jax.ShapeDtypeStruct((nt * 8, LANES), jnp.int32),
        jax.ShapeDtypeStruct((8, LANES), jnp.int32),
    )
    return pl.pallas_call(
        functools.partial(_mix_kernel, tiles_per_seq, npt),
        out_shape=out_shape,
        grid=(nt + 1,),
        in_specs=[
            _prompt_spec(D_MODEL, npt), xs_spec,
            row(SGU_WIDTH), row(SGU_WIDTH), row(ATT_W),
            row(KV_DUP_W), row(KV_DUP_W), prev_spec, prev_spec,
            cache_spec, cache_spec, row(D_MODEL), row(D_MODEL),
            _const_spec((SGU_GROUPS, SGU_CHUNK, SGU_CHUNK)), _const_spec((SGU_GROUPS, SGU_CHUNK, LANES)),
            _const_spec((N_KV_HEADS, Q_PER_KV * CHUNK, LANES)),
            _const_spec((SGU_WIDTH, D_MODEL)), _const_spec((ATT_W, D_MODEL)),
            _const_spec((D_MODEL, D_MODEL)), _const_spec((1, D_MODEL)),
            _const_spec((N_EXPERTS, D_MODEL)), _const_spec((N_EXPERTS, D_MODEL)),
            _const_spec((N_EXPERTS, LANES)),
        ],
        out_specs=(row(D_MODEL),
                   pl.BlockSpec((ROW_TILE * TOP_K * ROW_SUBTILES, LANES),
                                lambda i: (jnp.maximum(i - 1, 0), 0)),
                   routed8(ROW_TILE), routed8(ROW_TILE), routed8(LANES), _const_spec((8, LANES))),
        scratch_shapes=[
            pltpu.VMEM((ROW_TILE, SGU_WIDTH), bf16), pltpu.VMEM((ROW_TILE, ATT_W), bf16),
            pltpu.VMEM((WINDOW + ROW_TILE, KV_DUP_W), bf16),
            pltpu.VMEM((WINDOW + ROW_TILE, KV_DUP_W), bf16),
            pltpu.VMEM((8, LANES), f32),
            pltpu.VMEM((ROW_TILE, D_MODEL), bf16), pltpu.VMEM((ROW_TILE, D_MODEL), bf16),
        ],
        compiler_params=_params(),
        name="mix",
    )(xp, xs, u, vln, q, kd, vd, kd, vd,
      cache_k.reshape(-1, WINDOW, KV_W), cache_v.reshape(-1, WINDOW, KV_W), ga, gb,
      w_sp, jnp.broadcast_to(b_sp[:, :, None], (SGU_GROUPS, SGU_CHUNK, LANES)), sink_cols,
      w_pa.astype(bf16), w_pb.astype(bf16), w_o.astype(bf16),
      g_ffn.reshape(1, -1), wr_hi, wr_lo,
      jnp.broadcast_to(b_router.astype(f32)[:, None], (N_EXPERTS, LANES)))


def _unrolled_rows(n_rows, fn):
    if isinstance(n_rows, int):
        groups, tail_start = n_rows // ROW_UNROLL, n_rows - n_rows % ROW_UNROLL
    else:
        groups = lax.shift_right_logical(n_rows, ROW_UNROLL.bit_length() - 1)
        tail_start = groups * ROW_UNROLL

    def group(gi, carry):
        for lane in range(ROW_UNROLL):
            fn(gi * ROW_UNROLL + lane, lane)
        return carry

    def tail(r, carry):
        fn(r, 0)
        return carry

    lax.fori_loop(0, groups, group, 0)
    lax.fori_loop(tail_start, n_rows, tail, 0)


def _row_span(first_row, n_rows):
    return pl.ds(pl.multiple_of(first_row * ROW_SUBTILES, ROW_SUBTILES),
                 pl.multiple_of(n_rows * ROW_SUBTILES, ROW_SUBTILES))


def _run_spec(index_of):
    return pl.BlockSpec((1, 1, LANES), lambda i, *_: (index_of(i), 0, 0), memory_space=pltpu.SMEM)


def _expert_kernel(n_token_tiles,
                   te_ref, nu_ref, nx_ref, par_ref, nv_ref, tf_ref, cnt_ref, loc_ref,
                   bgu_ref, bdn_ref, hloc_hbm, wgu_hbm, wdn_hbm,
                   ys_ref,
                   xbuf, wgu_f, wdn_f, wgu_s, wdn_s, walk, xsem, wsem):
    i = pl.program_id(0)
    n_used = nu_ref[0]
    expert = te_ref[i]
    buf = par_ref[i]
    slot = i % 2
    expert_changed = jnp.logical_or(i == 0, expert != te_ref[jnp.maximum(i - 1, 0)])

    def fetch_rows(j, b):
        e = te_ref[j]
        need = nv_ref[j]

        @pl.when(tf_ref[j] == 1)
        def _():
            walk[0] = 0
            walk[1] = 0

        @pl.when(need < MOE_TILE)
        def _():
            xbuf[b] = jnp.zeros(xbuf.shape[1:], xbuf.dtype)

        def unfinished(state):
            filled, tile, _ = state
            return jnp.logical_and(filled < need, tile < n_token_tiles)

        def take_run(state):
            filled, tile, off = state
            run = cnt_ref[tile * N_EXPERTS + e]
            take = jnp.minimum(run - off, need - filled)

            @pl.when(take > 0)
            def _():
                src = tile * (ROW_TILE * TOP_K) + loc_ref[tile * N_EXPERTS + e] + off
                pltpu.make_async_copy(hloc_hbm.at[_row_span(src, take)],
                                      xbuf.at[b, _row_span(filled, take)], xsem.at[b]).start()

            run_done = off + take == run
            return (filled + take, jnp.where(run_done, tile + 1, tile), jnp.where(run_done, 0, off + take))

        _, tile, off = lax.while_loop(unfinished, take_run, (jnp.int32(0), walk[0], walk[1]))
        walk[0] = tile
        walk[1] = off

    @pl.when(jnp.logical_and(i == 0, n_used > 0))
    def _():
        fetch_rows(0, 0)

    @pl.when(i + 1 < n_used)
    def _():
        fetch_rows(i + 1, 1 - slot)

    def weight_copies(e, b):
        return (pltpu.make_async_copy(wgu_hbm.at[e], wgu_f.at[b], wsem.at[0, b]),
                pltpu.make_async_copy(wdn_hbm.at[e], wdn_f.at[b], wsem.at[1, b]))

    @pl.when(jnp.logical_and(i < n_used, expert_changed))
    def _():
        @pl.when(i == 0)
        def _():
            for copy in weight_copies(expert, buf):
                copy.start()

        for copy in weight_copies(expert, buf):
            copy.wait()
        following = nx_ref[i]

        @pl.when(following != expert)
        def _():
            for copy in weight_copies(following, 1 - buf):
                copy.start()

        wgu_s[...] = _bf16(wgu_f[buf])
        wdn_s[...] = _bf16(wdn_f[buf])

    rows = nv_ref[i]

    @pl.when(i < n_used)
    def _():
        pltpu.make_async_copy(hloc_hbm.at[_row_span(0, rows)], xbuf.at[slot, _row_span(0, rows)],
                              xsem.at[slot]).wait()

    def mlp(n):
        x = _bf16(_load_row_tiled(xbuf, (slot,), n))
        gu = _dot(x, wgu_s[...]) + bgu_ref[0]
        gate = jnp.minimum(gu[:, :D_FF], SWIGLU_LIMIT)
        lin = jnp.clip(gu[:, D_FF:], -SWIGLU_LIMIT, SWIGLU_LIMIT)
        act = gate * _sigmoid(SWIGLU_ALPHA * gate) * (lin + 1.0)
        _store_row_tiled(ys_ref, (), _dot(_bf16(act), wdn_s[...]) + bdn_ref[0])

    @pl.when(jnp.logical_and(i < n_used, rows > MOE_TILE // 2))
    def _():
        mlp(MOE_TILE)

    @pl.when(jnp.logical_and(i < n_used, rows <= MOE_TILE // 2))
    def _():
        mlp(MOE_TILE // 2)
        ys_ref[pl.ds(MOE_TILE // 2 * ROW_SUBTILES, MOE_TILE // 2 * ROW_SUBTILES), :] = jnp.zeros(
            (MOE_TILE // 2 * ROW_SUBTILES, LANES), ys_ref.dtype)

    @pl.when(i >= n_used)
    def _():
        ys_ref[...] = jnp.zeros_like(ys_ref)


def _experts(tile_expert, n_used, next_expert, weight_buf, n_valid, tile_first, run_n, run_loc,
             h_local, w_gu, b_gu, w_dn, b_dn):
    n_tiles = tile_expert.shape[0]
    n_token_tiles = h_local.shape[0] // (ROW_TILE * TOP_K * ROW_SUBTILES)
    f32, bf16 = jnp.float32, jnp.bfloat16
    tile_rows = MOE_TILE * ROW_SUBTILES
    grid_spec = pltpu.PrefetchScalarGridSpec(
        num_scalar_prefetch=8,
        grid=(n_tiles,),
        in_specs=[
            pl.BlockSpec((1, 1, 2 * D_FF), lambda i, te, *_: (te[i], 0, 0)),
            pl.BlockSpec((1, 1, D_MODEL), lambda i, te, *_: (te[i], 0, 0)),
            pl.BlockSpec(memory_space=pl.ANY), pl.BlockSpec(memory_space=pl.ANY),
            pl.BlockSpec(memory_space=pl.ANY),
        ],
        out_specs=pl.BlockSpec((tile_rows, LANES), lambda i, *_: (i, 0)),
        scratch_shapes=[
            pltpu.VMEM((2, tile_rows, LANES), f32),
            pltpu.VMEM((2, D_MODEL, 2 * D_FF), f32), pltpu.VMEM((2, D_FF, D_MODEL), f32),
            pltpu.VMEM((D_MODEL, 2 * D_FF), bf16), pltpu.VMEM((D_FF, D_MODEL), bf16),
            pltpu.SMEM((2,), jnp.int32),
            pltpu.SemaphoreType.DMA((2,)), pltpu.SemaphoreType.DMA((2, 2)),
        ],
    )
    return pl.pallas_call(
        functools.partial(_expert_kernel, n_token_tiles),
        out_shape=jax.ShapeDtypeStruct((n_tiles * tile_rows, LANES), f32),
        grid_spec=grid_spec,
        compiler_params=_params(),
        name="experts",
    )(tile_expert, n_used, next_expert, weight_buf, n_valid, tile_first, run_n, run_loc,
      b_gu.reshape(N_EXPERTS, 1, -1), b_dn.reshape(N_EXPERTS, 1, -1), h_local, w_gu, w_dn)


def _combine_kernel(n_prompt_tiles,
                    src_ref, n_ref, dst_ref, src_nx_ref, n_nx_ref, dst_nx_ref, slot_ref, gate_ref,
                    x1_ref, gfin_ref, ys_hbm,
                    yp_ref, yo_ref,
                    local, mixed, run_sem):
    i = pl.program_id(0)
    last = pl.num_programs(0) - 1
    buf = i % 2
    half_slots = ROW_TILE * TOP_K
    half_rows = half_slots * ROW_SUBTILES

    def fetch_runs(s_ref, c_ref, d_ref, b):
        for e in range(N_EXPERTS):
            n = c_ref[0, 0, e]
            copy = pltpu.make_async_copy(ys_hbm.at[_row_span(d_ref[0, 0, e], n)],
                                         local.at[_row_span(b * half_slots + s_ref[0, 0, e], n)],
                                         run_sem.at[b])
            pl.when(n > 0)(copy.start)

    @pl.when(i == 0)
    def _():
        fetch_runs(src_ref, n_ref, dst_ref, 0)

    @pl.when(i < last)
    def _():
        fetch_runs(src_nx_ref, n_nx_ref, dst_nx_ref, 1 - buf)

    pltpu.make_async_copy(ys_hbm.at[pl.ds(0, half_rows)],
                          local.at[pl.ds(pl.multiple_of(buf * half_rows, half_rows), half_rows)],
                          run_sem.at[buf]).wait()

    def blend(t, lane):
        acc = None
        for k in range(TOP_K):
            at = pl.multiple_of(slot_ref[0, 0, k * ROW_TILE + t], ROW_SUBTILES)
            term = gate_ref[0, 0, k * ROW_TILE + t] * local[pl.ds(at, ROW_SUBTILES), :]
            acc = term if acc is None else acc + term
        mixed[pl.ds(pl.multiple_of(t * ROW_SUBTILES, ROW_SUBTILES), ROW_SUBTILES), :] = acc
    _unrolled_rows(ROW_TILE, blend)

    out = _rms(x1_ref[...] + _load_row_tiled(mixed, (), ROW_TILE), gfin_ref[...])

    @pl.when(i < n_prompt_tiles)
    def _():
        yp_ref[...] = out

    @pl.when(i >= n_prompt_tiles)
    def _():
        yo_ref[...] = out


def _combine(run_src, run_n, run_dst, slots, gates, x1, ys, g_final, tp):
    t = x1.shape[0]
    npt = tp // ROW_TILE
    nt = t // ROW_TILE
    f32 = jnp.float32
    picks = ROW_TILE * TOP_K
    nxt = lambda i: jnp.minimum(i + 1, nt - 1)
    pick_spec = pl.BlockSpec((1, 1, picks), lambda i: (i, 0, 0), memory_space=pltpu.SMEM)
    return pl.pallas_call(
        functools.partial(_combine_kernel, npt),
        out_shape=(jax.ShapeDtypeStruct((tp, D_MODEL), f32),
                   jax.ShapeDtypeStruct((t - tp, D_MODEL), f32)),
        grid=(nt,),
        in_specs=[_run_spec(lambda i: i), _run_spec(lambda i: i), _run_spec(lambda i: i),
                  _run_spec(nxt), _run_spec(nxt), _run_spec(nxt), pick_spec, pick_spec,
                  _row_spec(D_MODEL), _const_spec((1, D_MODEL)), pl.BlockSpec(memory_space=pl.ANY)],
        out_specs=(_prompt_spec(D_MODEL, npt), _sample_spec(D_MODEL, npt)),
        scratch_shapes=[pltpu.VMEM((2 * picks * ROW_SUBTILES, LANES), f32),
                        pltpu.VMEM((ROW_TILE * ROW_SUBTILES, LANES), f32),
                        pltpu.SemaphoreType.DMA((2,))],
        compiler_params=_params(),
        name="combine",
    )(run_src, run_n, run_dst, run_src, run_n, run_dst, slots, gates, x1, g_final.reshape(1, -1), ys)


def _plan(meta, counts, t):
    nt = t // ROW_TILE
    n_tiles = (t * TOP_K + N_EXPERTS * (MOE_TILE - 1)) // MOE_TILE
    counts = counts[0, :N_EXPERTS]
    tiles_e = (counts + MOE_TILE - 1) // MOE_TILE
    tile_end = jnp.cumsum(tiles_e)
    tile_start = tile_end - tiles_e
    n_used = tile_end[-1]
    first_row = jnp.pad(tile_start * MOE_TILE, (0, LANES - N_EXPERTS))
    meta = meta.reshape(nt, 8, LANES)
    run_dst = meta[:, 0:1, :] + first_row[None, None, :]
    run_n = meta[:, 1:2, :]
    run_src = meta[:, 2:3, :]
    tile_ids = jnp.arange(n_tiles, dtype=jnp.int32)
    live = jnp.minimum(tile_ids, n_used - 1)
    tile_expert = jnp.sum(tile_end[None, :] <= live[:, None], axis=1).astype(jnp.int32)
    ids = jnp.arange(N_EXPERTS, dtype=jnp.int32)
    is_expert = tile_expert[:, None] == ids[None, :]
    of_tile = lambda per_expert: jnp.sum(jnp.where(is_expert, per_expert[None, :], 0), axis=1)
    in_expert = tile_ids - of_tile(tile_start)
    n_valid = jnp.clip(of_tile(counts) - in_expert * MOE_TILE, 0, MOE_TILE)
    n_valid = jnp.where(tile_ids < n_used, n_valid, 0).astype(jnp.int32)
    tile_first = jnp.logical_and(in_expert == 0, tile_ids < n_used).astype(jnp.int32)
    run_n_flat = run_n[:, 0, :N_EXPERTS].reshape(-1)
    run_loc_flat = run_src[:, 0, :N_EXPERTS].reshape(-1)
    used = tiles_e > 0
    later_used = jnp.where(jnp.logical_and(used[None, :], ids[None, :] > ids[:, None]), ids[None, :],
                           N_EXPERTS)
    following = jnp.min(later_used, axis=1)
    following = jnp.where(following < N_EXPERTS, following, ids)
    buf_of = (jnp.cumsum(used.astype(jnp.int32)) - 1) % 2
    return (tile_expert, n_used.reshape(1).astype(jnp.int32), of_tile(following).astype(jnp.int32),
            of_tile(buf_of).astype(jnp.int32), n_valid, tile_first, run_n_flat, run_loc_flat,
            run_src, run_n, run_dst)


def kernel(x_prompt, x_sample, cache_k, cache_v, g_mix, w_in, b_in, ln_v_g, ln_v_b, w_sp, b_sp,
           attn_sinks, w_pa, w_pb, w_o, g_ffn, w_router, b_router, w_gu, b_gu, w_dn, b_dn, g_final):
    nb, seq, d = x_prompt.shape
    nsb, nnew, _ = x_sample.shape
    tp, ts = nb * seq, nsb * nnew
    t = tp + ts
    xp = x_prompt.reshape(tp, d)
    xs = x_sample.reshape(ts, d)
    pos = np.concatenate([np.arange(seq), np.tile(PAST_LEN + np.arange(nnew), ROW_TILE // nnew)])
    tables = _rope_tables(pos)
    u, vln, v_sgu, q, k, v, kd, vd, ga, gb = _project(
        xp, xs, g_mix[0], w_in[0], b_in[0], ln_v_g[0], ln_v_b[0], tables, seq)
    x1, h_local, slot_t, gate_t, meta, counts = _mix(
        xp, xs, u, vln, q, kd, vd, cache_k[0], cache_v[0], ga, gb, w_sp[0], b_sp[0], attn_sinks[0],
        w_pa[0], w_pb[0], w_o[0], g_ffn[0], w_router[0], b_router[0], seq)
    nt = t // ROW_TILE
    picks = lambda a: a.reshape(nt, 8, ROW_TILE)[:, :TOP_K, :].reshape(nt, 1, TOP_K * ROW_TILE)
    slots, gates = picks(slot_t), picks(gate_t)
    (tile_expert, n_used, next_expert, weight_buf, n_valid, tile_first, run_n_flat, run_loc_flat,
     run_src, run_n, run_dst) = _plan(meta, counts, t)
    y_sorted = _experts(tile_expert, n_used, next_expert, weight_buf, n_valid, tile_first, run_n_flat,
                        run_loc_flat, h_local, w_gu[0], b_gu[0], w_dn[0], b_dn[0])
    y_p, y_s = _combine(run_src, run_n, run_dst, slots, gates, x1, y_sorted, g_final, tp)

    keep = min(WINDOW, seq)
    tails = lambda a: jnp.stack([a[(b + 1) * seq - keep:(b + 1) * seq] for b in range(nb)]).reshape(
        nb, keep, N_KV_HEADS, HEAD_DIM)
    kp, vp = tails(k), tails(v)
    ks = k[tp:].reshape(nsb, nnew, N_KV_HEADS, HEAD_DIM)
    vs = v[tp:].reshape(nsb, nnew, N_KV_HEADS, HEAD_DIM)
    return (y_p.reshape(nb, seq, d), y_s.reshape(nsb, nnew, d), kp[None], vp[None], ks[None],
            vs[None], v_sgu.reshape(1, nsb, nnew, SGU_WIDTH))
```

```python
import functools

import numpy as np
import jax
import jax.numpy as jnp
from jax import lax
from jax.experimental import pallas as pl
from jax.experimental.pallas import tpu as pltpu

D_MODEL = 1024
PAST_LEN = 2048
CHUNK = 64
SGU_CHUNK = 128
SGU_GROUPS = 8
SGU_WIDTH = 1024
N_HEADS = 16
N_KV_HEADS = 4
HEAD_DIM = 64
Q_PER_KV = N_HEADS // N_KV_HEADS
WINDOW = 128
ROT_DIM = HEAD_DIM // 4
ROPE_THETA = 500000.0
ATT_W = N_HEADS * HEAD_DIM
KV_W = N_KV_HEADS * HEAD_DIM
N_EXPERTS = 32
TOP_K = 4
D_FF = 1024
SWIGLU_ALPHA = 1.702
SWIGLU_LIMIT = 7.0
NORM_EPS = 1e-5
NEG_INF = -1e30
N_IN = SGU_WIDTH * 2 + ATT_W + KV_W * 2 + D_MODEL * 2

LANES = 128
ROW_TILE = 256
MOE_TILE = 512
ROW_UNROLL = 8
COL_BLOCK = 256
KV_DUP_W = N_KV_HEADS * LANES
KEY_SPAN = WINDOW + CHUNK
VMEM_LIMIT = 56 * 1024 * 1024

_SQRT_HALF = 0.7071067811865476
_LOG2_E = 1.4426950408889634


def _gelu(x):
    t = 1.0 / (1.0 + (0.3275911 * _SQRT_HALF) * jnp.abs(x))
    half_poly = t * (0.127414796 + t * (-0.142248368 + t * (0.7107068705
                     + t * (-0.7265760135 + t * 0.5307027145))))
    half_tail = x * (half_poly * jnp.exp2(x * x * (-0.5 * _LOG2_E)))
    return jnp.where(x >= 0.0, x - half_tail, half_tail)


def _sigmoid(x):
    return 1.0 / (1.0 + jnp.exp(-x))


def _bf16(x):
    return x.astype(jnp.bfloat16)


def _dot(a, b):
    return jnp.dot(a, b, preferred_element_type=jnp.float32)


ROW_SUBTILES = D_MODEL // LANES


def _store_row_tiled(ref, lead, x):
    rows = x.shape[0]
    for s in range(ROW_SUBTILES):
        ref[(*lead, pl.ds(s, rows, stride=ROW_SUBTILES), slice(None))] = x[:, s * LANES:(s + 1) * LANES]


def _load_row_tiled(ref, lead, rows):
    return jnp.concatenate(
        [ref[(*lead, pl.ds(s, rows, stride=ROW_SUBTILES), slice(None))] for s in range(ROW_SUBTILES)],
        axis=1)


def _dot_nt(a, b):
    return lax.dot_general(a, b, (((1,), (1,)), ((), ())), preferred_element_type=jnp.float32)


def _rms(x, g):
    return x * lax.rsqrt(jnp.mean(x * x, axis=-1, keepdims=True) + NORM_EPS) * g


def _lane_lo(rows):
    return lax.broadcasted_iota(jnp.int32, (rows, LANES), 1) < HEAD_DIM


def _dup_heads(kv):
    rows = kv.shape[0]
    lo = _lane_lo(rows)
    out = []
    for j in range(KV_W // LANES):
        blk = kv[:, j * LANES:(j + 1) * LANES]
        swp = pltpu.roll(blk, HEAD_DIM, axis=1)
        out.append(jnp.where(lo, blk, swp))
        out.append(jnp.where(lo, swp, blk))
    return _bf16(jnp.concatenate(out, axis=1))


def _rope_block(zb, cos_b, sin_lo, sin_hi):
    up = pltpu.roll(zb, LANES - ROT_DIM // 2, axis=1)
    dn = pltpu.roll(zb, ROT_DIM // 2, axis=1)
    return zb * cos_b + up * sin_lo + dn * sin_hi


def _proj_kernel(n_prompt_tiles, xp_ref, xs_ref, gmix_ref, w_ref, b_ref, lng_ref, lnb_ref,
                 cos_ref, slo_ref, shi_ref,
                 u_ref, vln_ref, vs_ref, q_ref, k_ref, v_ref, kd_ref, vd_ref, ga_ref, gb_ref):
    i = pl.program_id(0)
    x = jnp.where(i < n_prompt_tiles, xp_ref[...], xs_ref[...])
    h = _bf16(_rms(x, gmix_ref[...]))

    cos_b, sin_lo, sin_hi = cos_ref[...], slo_ref[...], shi_ref[...]
    off_u, off_v, off_q, off_k, off_vv, off_ga, off_gb = (
        int(o) for o in np.cumsum((0, SGU_WIDTH, SGU_WIDTH, ATT_W, KV_W, KV_W, D_MODEL)))

    def z(lo):
        return _dot(h, w_ref[:, lo:lo + COL_BLOCK]) + b_ref[:, lo:lo + COL_BLOCK]

    def cols(j):
        return slice(j * COL_BLOCK, (j + 1) * COL_BLOCK)

    def rope(zb):
        return jnp.concatenate(
            [_rope_block(zb[:, t * LANES:(t + 1) * LANES], cos_b, sin_lo, sin_hi)
             for t in range(COL_BLOCK // LANES)], axis=1)

    gelu_v = []

    def do_u(j):
        u_ref[:, cols(j)] = _bf16(z(off_u + j * COL_BLOCK))

    def do_v(j):
        gelu_v.append(_gelu(z(off_v + j * COL_BLOCK)))

    def do_q(j):
        q_ref[:, cols(j)] = _bf16(rope(z(off_q + j * COL_BLOCK)) * (HEAD_DIM ** -0.5))

    def do_k(j):
        kr = rope(z(off_k))
        k_ref[...] = kr
        kd_ref[...] = _dup_heads(kr)

    def do_vv(j):
        zv = z(off_vv)
        v_ref[...] = zv
        vd_ref[...] = _dup_heads(zv)

    def do_ga(j):
        ga_ref[:, cols(j)] = _bf16(_sigmoid(z(off_ga + j * COL_BLOCK)))

    def do_gb(j):
        gb_ref[:, cols(j)] = _bf16(_sigmoid(z(off_gb + j * COL_BLOCK)))

    order = ((do_v, 0), (do_q, 0), (do_q, 1), (do_u, 0), (do_v, 1), (do_q, 2), (do_q, 3), (do_u, 1),
             (do_v, 2), (do_k, 0), (do_vv, 0), (do_u, 2), (do_v, 3), (do_ga, 0), (do_ga, 1), (do_u, 3),
             (do_ga, 2), (do_ga, 3), (do_gb, 0), (do_gb, 1), (do_gb, 2), (do_gb, 3))
    for fn, j in order:
        fn(j)
    gv = jnp.concatenate(gelu_v, axis=1)
    gc = gv - jnp.mean(gv, axis=-1, keepdims=True)
    var = jnp.mean(gc * gc, axis=-1, keepdims=True)
    vln = gc * lax.rsqrt(var + NORM_EPS) * lng_ref[...] + lnb_ref[...]
    vln_ref[...] = _bf16(vln)
    vs_ref[...] = vln


def _rope_tables(pos):
    half = ROT_DIM // 2
    inv = np.float32(ROPE_THETA) ** (-np.arange(half, dtype=np.float32) * np.float32(2.0) / ROT_DIM)
    ang = pos.astype(np.float32)[:, None] * inv.astype(np.float32)[None, :]
    cos = np.cos(ang.astype(np.float64)).astype(np.float32)
    sin = np.sin(ang.astype(np.float64)).astype(np.float32)
    n = pos.shape[0]
    ones = np.ones((n, HEAD_DIM - ROT_DIM), np.float32)
    zeros = np.zeros((n, HEAD_DIM - ROT_DIM), np.float32)
    zh = np.zeros((n, half), np.float32)
    cos_h = np.concatenate([cos, cos, ones], axis=1)
    slo_h = np.concatenate([-sin, zh, zeros], axis=1)
    shi_h = np.concatenate([zh, sin, zeros], axis=1)
    rep = LANES // HEAD_DIM
    return tuple(jnp.asarray(np.tile(a, (1, rep))) for a in (cos_h, slo_h, shi_h))


def _row_spec(width):
    return pl.BlockSpec((ROW_TILE, width), lambda i: (i, 0))


def _const_spec(shape):
    return pl.BlockSpec(shape, lambda i: (0,) * len(shape))


def _prompt_spec(width, n_prompt_tiles):
    return pl.BlockSpec((ROW_TILE, width), lambda i: (jnp.minimum(i, n_prompt_tiles - 1), 0))


def _sample_spec(width, n_prompt_tiles):
    return pl.BlockSpec((ROW_TILE, width), lambda i: (jnp.maximum(i - n_prompt_tiles, 0), 0))


def _params():
    return pltpu.CompilerParams(dimension_semantics=("arbitrary",), vmem_limit_bytes=VMEM_LIMIT)


def _project(xp, xs, g_mix, w_in, b_in, ln_g, ln_b, tables, seq):
    tp, ts = xp.shape[0], xs.shape[0]
    t = tp + ts
    npt = tp // ROW_TILE
    tiles_per_seq = seq // ROW_TILE
    f32, bf16 = jnp.float32, jnp.bfloat16
    table_spec = pl.BlockSpec(
        (ROW_TILE, LANES), lambda i: (jnp.where(i < npt, i % tiles_per_seq, tiles_per_seq), 0))
    out_shape = (
        jax.ShapeDtypeStruct((t, SGU_WIDTH), bf16),
        jax.ShapeDtypeStruct((t, SGU_WIDTH), bf16),
        jax.ShapeDtypeStruct((ts, SGU_WIDTH), f32),
        jax.ShapeDtypeStruct((t, ATT_W), bf16),
        jax.ShapeDtypeStruct((t, KV_W), f32),
        jax.ShapeDtypeStruct((t, KV_W), f32),
        jax.ShapeDtypeStruct((t, KV_DUP_W), bf16),
        jax.ShapeDtypeStruct((t, KV_DUP_W), bf16),
        jax.ShapeDtypeStruct((t, D_MODEL), bf16),
        jax.ShapeDtypeStruct((t, D_MODEL), bf16),
    )
    return pl.pallas_call(
        functools.partial(_proj_kernel, npt),
        out_shape=out_shape,
        grid=(t // ROW_TILE,),
        in_specs=[
            _prompt_spec(D_MODEL, npt), _sample_spec(D_MODEL, npt),
            _const_spec((1, D_MODEL)), _const_spec((D_MODEL, N_IN)),
            _const_spec((1, N_IN)), _const_spec((1, SGU_WIDTH)), _const_spec((1, SGU_WIDTH)),
            table_spec, table_spec, table_spec,
        ],
        out_specs=(
            _row_spec(SGU_WIDTH), _row_spec(SGU_WIDTH), _sample_spec(SGU_WIDTH, npt),
            _row_spec(ATT_W), _row_spec(KV_W), _row_spec(KV_W), _row_spec(KV_DUP_W),
            _row_spec(KV_DUP_W), _row_spec(D_MODEL), _row_spec(D_MODEL),
        ),
        compiler_params=_params(),
        name="proj",
    )(xp, xs, g_mix.reshape(1, -1), w_in.astype(bf16), b_in.reshape(1, -1),
      ln_g.reshape(1, -1), ln_b.reshape(1, -1), *tables)


def _attend(qa, qb, kwin, vwin, sink, valid):
    lo = _lane_lo(CHUNK)
    zero = jnp.zeros_like(qa)
    lhs = jnp.concatenate([jnp.where(lo, qa, zero), jnp.where(lo, zero, qa),
                           jnp.where(lo, qb, zero), jnp.where(lo, zero, qb)], axis=0)
    s = _dot_nt(lhs, kwin)
    if valid is not None:
        s = jnp.where(valid, s, NEG_INF)
    s_a, s_b = s[:, :LANES], s[:, LANES:]
    tail = s_b.shape[1]
    m = jnp.maximum(jnp.max(s, axis=-1, keepdims=True), sink)
    p_a = jnp.exp(s_a - m)
    p_b = jnp.exp(s_b - m[:, :tail])
    denom = (jnp.sum(jnp.concatenate([p_a, p_b], axis=1), axis=-1, keepdims=True)
             + jnp.exp(sink - m))
    inv = 1.0 / denom
    pn = jnp.concatenate([p_a * inv, p_b * inv[:, :tail]], axis=1)
    r = _dot(_bf16(pn), vwin)
    oa = jnp.where(lo, r[0:CHUNK], r[CHUNK:2 * CHUNK])
    ob = jnp.where(lo, r[2 * CHUNK:3 * CHUNK], r[3 * CHUNK:4 * CHUNK])
    return oa, ob


def _stack_rows(rows):
    ri = lax.broadcasted_iota(jnp.int32, (8, rows[0].shape[1]), 0)
    out = jnp.zeros((8, rows[0].shape[1]), rows[0].dtype)
    for k, row in enumerate(rows):
        out = jnp.where(ri == k, row, out)
    return out


def _route_pick(logits_t):
    rows = logits_t.shape[1]
    eid = lax.broadcasted_iota(jnp.int32, (N_EXPERTS, rows), 0)
    work = logits_t
    vals, idxs = [], []
    for _ in range(TOP_K):
        m = jnp.max(work, axis=0, keepdims=True)
        idx = jnp.min(jnp.where(work == m, eid, N_EXPERTS), axis=0, keepdims=True)
        vals.append(m)
        idxs.append(idx)
        work = jnp.where(eid == idx, -jnp.inf, work)
    exps = [jnp.exp(v - vals[0]) for v in vals]
    inv = 1.0 / (exps[0] + exps[1] + exps[2] + exps[3])
    gates = _stack_rows([e * inv for e in exps])

    picked = jnp.zeros((N_EXPERTS, rows), jnp.float32)
    for idx in idxs:
        picked = jnp.where(eid == idx, 1.0, picked)
    tr = lax.broadcasted_iota(jnp.int32, (rows, rows), 0)
    tc = lax.broadcasted_iota(jnp.int32, (rows, rows), 1)
    earlier = _bf16(jnp.where(tr < tc, 1.0, 0.0))
    in_tile = _dot(_bf16(picked), earlier)
    count_col = jnp.broadcast_to(jnp.sum(picked, axis=1, keepdims=True), (N_EXPERTS, LANES))
    eid_wide = lax.broadcasted_iota(jnp.int32, (LANES, rows), 0)
    picked_wide = jnp.zeros((LANES, rows), jnp.float32)
    for idx in idxs:
        picked_wide = jnp.where(eid_wide == idx, 1.0, picked_wide)
    count_row = _dot_nt(jnp.ones((8, rows), jnp.bfloat16), _bf16(picked_wide))
    return idxs, gates, in_tile, count_col, count_row


def _route_place(idxs, in_tile, count_col, count_row, carry_s, live):
    rows = in_tile.shape[1]
    eid = lax.broadcasted_iota(jnp.int32, (N_EXPERTS, rows), 0)
    er = lax.broadcasted_iota(jnp.int32, (N_EXPERTS, N_EXPERTS), 0)
    ec = lax.broadcasted_iota(jnp.int32, (N_EXPERTS, N_EXPERTS), 1)
    start_col = _dot(_bf16(jnp.where(ec < er, 1.0, 0.0)), _bf16(count_col))
    local = in_tile + jnp.concatenate([start_col] * (rows // LANES), axis=1)
    slots = _stack_rows([jnp.sum(jnp.where(eid == idx, local, 0.0), axis=0, keepdims=True)
                         for idx in idxs]).astype(jnp.int32)
    lr = lax.broadcasted_iota(jnp.int32, (LANES, LANES), 0)
    lc = lax.broadcasted_iota(jnp.int32, (LANES, LANES), 1)
    start_row = _dot(_bf16(count_row), _bf16(jnp.where(lr < lc, 1.0, 0.0)))
    ri = lax.broadcasted_iota(jnp.int32, (8, LANES), 0)
    meta = jnp.where(ri == 0, carry_s[...], jnp.where(ri == 1, count_row, jnp.where(ri == 2, start_row, 0.0)))
    carry_s[...] = carry_s[...] + count_row * live
    return slots, meta.astype(jnp.int32)


def _mix_kernel(tiles_per_seq, n_prompt_tiles,
                xp_ref, xs_ref, u_ref, vln_ref, q_ref, kd_ref, vd_ref, kdp_ref, vdp_ref,
                ck_ref, cv_ref, ga_ref, gb_ref, wsp_ref, bsp_ref, sink_ref,
                wpa_ref, wpb_ref, wo_ref, gffn_ref, wrh_ref, wrl_ref, br_ref,
                x1_ref, hloc_ref, slot_ref, gate_ref, meta_ref, count_ref,
                a_s, o_s, kwin_s, vwin_s, carry_s, hhi_s, hlo_s):
    i = pl.program_id(0)
    n_streams = ROW_TILE // CHUNK

    @pl.when(i == 0)
    def _():
        carry_s[...] = jnp.zeros_like(carry_s)
        hhi_s[...] = jnp.zeros_like(hhi_s)
        hlo_s[...] = jnp.zeros_like(hlo_s)

    def sgu_rows(r0, rows):
        ri = lax.broadcasted_iota(jnp.int32, (rows, rows), 0) // CHUNK
        ci = lax.broadcasted_iota(jnp.int32, (rows, rows), 1) // CHUNK
        for g in range(SGU_GROUPS):
            cols = slice(g * LANES, (g + 1) * LANES)
            w = _bf16(jnp.where(ci <= ri, wsp_ref[g, :rows, :rows], 0.0))
            sp = _dot(w, vln_ref[r0:r0 + rows, cols]) + bsp_ref[g, :rows, :]
            a_s[r0:r0 + rows, cols] = _bf16(_gelu(u_ref[r0:r0 + rows, cols].astype(jnp.float32)) * sp)

    def attend_rows(r0, kwin_of, valid):
        for g in range(N_KV_HEADS):
            c0 = g * Q_PER_KV * HEAD_DIM
            kwin, vwin = kwin_of(g)
            oa, ob = _attend(q_ref[r0:r0 + CHUNK, c0:c0 + LANES],
                             q_ref[r0:r0 + CHUNK, c0 + LANES:c0 + 2 * LANES],
                             kwin, vwin, sink_ref[g], valid)
            o_s[r0:r0 + CHUNK, c0:c0 + LANES] = _bf16(oa)
            o_s[r0:r0 + CHUNK, c0 + LANES:c0 + 2 * LANES] = _bf16(ob)

    @pl.when(i < n_prompt_tiles)
    def _prompt():
        for c in range(ROW_TILE // SGU_CHUNK):
            sgu_rows(c * SGU_CHUNK, SGU_CHUNK)
        kwin_s[0:WINDOW] = kdp_ref[...]
        kwin_s[WINDOW:WINDOW + ROW_TILE] = kd_ref[...]
        vwin_s[0:WINDOW] = vdp_ref[...]
        vwin_s[WINDOW:WINDOW + ROW_TILE] = vd_ref[...]
        first = (i % tiles_per_seq) == 0
        col = lax.broadcasted_iota(jnp.int32, (1, KEY_SPAN), 1)
        for j in range(ROW_TILE // CHUNK):
            r0 = j * CHUNK
            valid = jnp.logical_or(jnp.logical_not(first), col + r0 >= WINDOW) if r0 < WINDOW else None

            def kwin_of(g, r0=r0):
                cols = slice(g * LANES, (g + 1) * LANES)
                return kwin_s[r0:r0 + KEY_SPAN, cols], vwin_s[r0:r0 + KEY_SPAN, cols]

            attend_rows(r0, kwin_of, valid)

    @pl.when(i >= n_prompt_tiles)
    def _sample():
        for s in range(n_streams):
            r0 = s * CHUNK
            sgu_rows(r0, CHUNK)
            kwin_s[0:WINDOW] = _dup_heads(ck_ref[s])
            kwin_s[WINDOW:KEY_SPAN] = kd_ref[r0:r0 + CHUNK]
            vwin_s[0:WINDOW] = _dup_heads(cv_ref[s])
            vwin_s[WINDOW:KEY_SPAN] = vd_ref[r0:r0 + CHUNK]

            def kwin_of(g):
                cols = slice(g * LANES, (g + 1) * LANES)
                return kwin_s[0:KEY_SPAN, cols], vwin_s[0:KEY_SPAN, cols]

            attend_rows(r0, kwin_of, None)

    hh, hl = hhi_s[...], hlo_s[...]
    logits_t = (_dot_nt(wrh_ref[...], hh) + _dot_nt(wrl_ref[...], hh) + _dot_nt(wrh_ref[...], hl)
                + jnp.concatenate([br_ref[...]] * (ROW_TILE // LANES), axis=1))
    m_a = ga_ref[...].astype(jnp.float32) * _dot(a_s[...], wpa_ref[...])
    idxs, gates, in_tile, count_col, count_row = _route_pick(logits_t)
    gate_ref[...] = gates
    m = m_a + gb_ref[...].astype(jnp.float32) * _dot(o_s[...], wpb_ref[...])
    slots, meta = _route_place(idxs, in_tile, count_col, count_row, carry_s, jnp.where(i > 0, 1.0, 0.0))
    half = ((i + 1) % 2) * (ROW_TILE * TOP_K * ROW_SUBTILES)
    slot_ref[...] = slots * ROW_SUBTILES + half
    meta_ref[...] = meta
    count_ref[...] = carry_s[...].astype(jnp.int32)
    n_slots = ROW_TILE * TOP_K
    sid = lax.broadcasted_iota(jnp.int32, (n_slots, ROW_TILE), 0)
    place = jnp.zeros((n_slots, ROW_TILE), jnp.float32)
    for k in range(TOP_K):
        place = jnp.where(sid == slots[k:k + 1, :], 1.0, place)
    _store_row_tiled(hloc_ref, (), _dot(_bf16(place), hh))

    x = jnp.where(i < n_prompt_tiles, xp_ref[...], xs_ref[...])
    x1 = x + _dot(_bf16(m), wo_ref[...])
    x1_ref[...] = x1
    h2 = _rms(x1, gffn_ref[...])
    h2_hi = _bf16(h2)
    hhi_s[...] = h2_hi
    hlo_s[...] = _bf16(h2 - h2_hi.astype(jnp.float32))


def _mix(xp, xs, u, vln, q, kd, vd, cache_k, cache_v, ga, gb, w_sp, b_sp, sinks,
         w_pa, w_pb, w_o, g_ffn, w_router, b_router, seq):
    tp, ts = xp.shape[0], xs.shape[0]
    t = tp + ts
    npt = tp // ROW_TILE
    tiles_per_seq = seq // ROW_TILE
    f32, bf16 = jnp.float32, jnp.bfloat16
    n_streams = ROW_TILE // CHUNK
    win_per_tile = ROW_TILE // WINDOW

    nt = t // ROW_TILE
    cur = lambda i: jnp.minimum(i, nt - 1)
    smp = lambda i: jnp.maximum(cur(i) - npt, 0)
    row = lambda width: pl.BlockSpec((ROW_TILE, width), lambda i: (cur(i), 0))
    prev_spec = pl.BlockSpec(
        (WINDOW, KV_DUP_W), lambda i: (jnp.maximum(jnp.minimum(i, npt - 1) * win_per_tile - 1, 0), 0))
    cache_spec = pl.BlockSpec((n_streams, WINDOW, KV_W), lambda i: (smp(i), 0, 0))
    xs_spec = pl.BlockSpec((ROW_TILE, D_MODEL), lambda i: (smp(i), 0))
    sink_cols = jnp.broadcast_to(
        jnp.repeat(sinks.astype(f32).reshape(N_KV_HEADS, Q_PER_KV), CHUNK, axis=1)[:, :, None],
        (N_KV_HEADS, Q_PER_KV * CHUNK, LANES))
    wr_t = w_router.T
    wr_hi = wr_t.astype(bf16)
    wr_lo = (wr_t - wr_hi.astype(f32)).astype(bf16)
    routed8 = lambda width: pl.BlockSpec((8, width), lambda i: (jnp.maximum(i - 1, 0), 0))
    out_shape = (
        jax.ShapeDtypeStruct((t, D_MODEL), f32),
        jax.ShapeDtypeStruct((t * TOP_K * ROW_SUBTILES, LANES), f32),
        jax.ShapeDtypeStruct((nt * 8, ROW_TILE), jnp.int32),
        jax.ShapeDtypeStruct((nt * 8, ROW_TILE), f32),
        jax.ShapeDtypeStruct((nt * 8, LANES), jnp.int32),
        jax.ShapeDtypeStruct((8, LANES), jnp.int32),
    )
    return pl.pallas_call(
        functools.partial(_mix_kernel, tiles_per_seq, npt),
        out_shape=out_shape,
        grid=(nt + 1,),
        in_specs=[
            _prompt_spec(D_MODEL, npt), xs_spec,
            row(SGU_WIDTH), row(SGU_WIDTH), row(ATT_W),
            row(KV_DUP_W), row(KV_DUP_W), prev_spec, prev_spec,
            cache_spec, cache_spec, row(D_MODEL), row(D_MODEL),
            _const_spec((SGU_GROUPS, SGU_CHUNK, SGU_CHUNK)), _const_spec((SGU_GROUPS, SGU_CHUNK, LANES)),
            _const_spec((N_KV_HEADS, Q_PER_KV * CHUNK, LANES)),
            _const_spec((SGU_WIDTH, D_MODEL)), _const_spec((ATT_W, D_MODEL)),
            _const_spec((D_MODEL, D_MODEL)), _const_spec((1, D_MODEL)),
            _const_spec((N_EXPERTS, D_MODEL)), _const_spec((N_EXPERTS, D_MODEL)),
            _const_spec((N_EXPERTS, LANES)),
        ],
        out_specs=(row(D_MODEL),
                   pl.BlockSpec((ROW_TILE * TOP_K * ROW_SUBTILES, LANES),
                                lambda i: (jnp.maximum(i - 1, 0), 0)),
                   routed8(ROW_TILE), routed8(ROW_TILE), routed8(LANES), _const_spec((8, LANES))),
        scratch_shapes=[
            pltpu.VMEM((ROW_TILE, SGU_WIDTH), bf16), pltpu.VMEM((ROW_TILE, ATT_W), bf16),
            pltpu.VMEM((WINDOW + ROW_TILE, KV_DUP_W), bf16),
            pltpu.VMEM((WINDOW + ROW_TILE, KV_DUP_W), bf16),
            pltpu.VMEM((8, LANES), f32),
            pltpu.VMEM((ROW_TILE, D_MODEL), bf16), pltpu.VMEM((ROW_TILE, D_MODEL), bf16),
        ],
        compiler_params=_params(),
        name="mix",
    )(xp, xs, u, vln, q, kd, vd, kd, vd,
      cache_k.reshape(-1, WINDOW, KV_W), cache_v.reshape(-1, WINDOW, KV_W), ga, gb,
      w_sp, jnp.broadcast_to(b_sp[:, :, None], (SGU_GROUPS, SGU_CHUNK, LANES)), sink_cols,
      w_pa.astype(bf16), w_pb.astype(bf16), w_o.astype(bf16),
      g_ffn.reshape(1, -1), wr_hi, wr_lo,
      jnp.broadcast_to(b_router.astype(f32)[:, None], (N_EXPERTS, LANES)))


def _unrolled_rows(n_rows, fn):
    if isinstance(n_rows, int):
        groups, tail_start = n_rows // ROW_UNROLL, n_rows - n_rows % ROW_UNROLL
    else:
        groups = lax.shift_right_logical(n_rows, ROW_UNROLL.bit_length() - 1)
        tail_start = groups * ROW_UNROLL

    def group(gi, carry):
        for lane in range(ROW_UNROLL):
            fn(gi * ROW_UNROLL + lane, lane)
        return carry

    def tail(r, carry):
        fn(r, 0)
        return carry

    lax.fori_loop(0, groups, group, 0)
    lax.fori_loop(tail_start, n_rows, tail, 0)


def _row_span(first_row, n_rows):
    return pl.ds(pl.multiple_of(first_row * ROW_SUBTILES, ROW_SUBTILES),
                 pl.multiple_of(n_rows * ROW_SUBTILES, ROW_SUBTILES))


def _run_spec(index_of):
    return pl.BlockSpec((1, 1, LANES), lambda i, *_: (index_of(i), 0, 0), memory_space=pltpu.SMEM)


def _expert_kernel(n_token_tiles,
                   te_ref, nu_ref, nx_ref, par_ref, nv_ref, tf_ref, cnt_ref, loc_ref,
                   bgu_ref, bdn_ref, hloc_hbm, wgu_hbm, wdn_hbm,
                   ys_ref,
                   xbuf, wgu_f, wdn_f, walk, xsem, wsem):
    i = pl.program_id(0)
    n_used = nu_ref[0]
    expert = te_ref[i]
    buf = par_ref[i]
    slot = i % 2
    expert_changed = jnp.logical_or(i == 0, expert != te_ref[jnp.maximum(i - 1, 0)])

    def fetch_rows(j, b):
        e = te_ref[j]
        need = nv_ref[j]

        @pl.when(tf_ref[j] == 1)
        def _():
            walk[0] = 0
            walk[1] = 0

        @pl.when(need < MOE_TILE)
        def _():
            xbuf[b] = jnp.zeros(xbuf.shape[1:], xbuf.dtype)

        def unfinished(state):
            filled, tile, _ = state
            return jnp.logical_and(filled < need, tile < n_token_tiles)

        def take_run(state):
            filled, tile, off = state
            run = cnt_ref[tile * N_EXPERTS + e]
            take = jnp.minimum(run - off, need - filled)

            @pl.when(take > 0)
            def _():
                src = tile * (ROW_TILE * TOP_K) + loc_ref[tile * N_EXPERTS + e] + off
                pltpu.make_async_copy(hloc_hbm.at[_row_span(src, take)],
                                      xbuf.at[b, _row_span(filled, take)], xsem.at[b]).start()

            run_done = off + take == run
            return (filled + take, jnp.where(run_done, tile + 1, tile), jnp.where(run_done, 0, off + take))

        _, tile, off = lax.while_loop(unfinished, take_run, (jnp.int32(0), walk[0], walk[1]))
        walk[0] = tile
        walk[1] = off

    @pl.when(jnp.logical_and(i == 0, n_used > 0))
    def _():
        fetch_rows(0, 0)

    @pl.when(i + 1 < n_used)
    def _():
        fetch_rows(i + 1, 1 - slot)

    def weight_copies(e, b):
        return (pltpu.make_async_copy(wgu_hbm.at[e], wgu_f.at[b], wsem.at[0, b]),
                pltpu.make_async_copy(wdn_hbm.at[e], wdn_f.at[b], wsem.at[1, b]))

    @pl.when(jnp.logical_and(i < n_used, expert_changed))
    def _():
        @pl.when(i == 0)
        def _():
            for copy in weight_copies(expert, buf):
                copy.start()

        for copy in weight_copies(expert, buf):
            copy.wait()
        following = nx_ref[i]

        @pl.when(following != expert)
        def _():
            for copy in weight_copies(following, 1 - buf):
                copy.start()


    rows = nv_ref[i]

    @pl.when(i < n_used)
    def _():
        pltpu.make_async_copy(hloc_hbm.at[_row_span(0, rows)], xbuf.at[slot, _row_span(0, rows)],
                              xsem.at[slot]).wait()

    def mlp(n):
        x = _bf16(_load_row_tiled(xbuf, (slot,), n))
        gu = _dot(x, _bf16(wgu_f[buf])) + bgu_ref[0]
        gate = jnp.minimum(gu[:, :D_FF], SWIGLU_LIMIT)
        lin = jnp.clip(gu[:, D_FF:], -SWIGLU_LIMIT, SWIGLU_LIMIT)
        act = gate * _sigmoid(SWIGLU_ALPHA * gate) * (lin + 1.0)
        _store_row_tiled(ys_ref, (), _dot(_bf16(act), _bf16(wdn_f[buf])) + bdn_ref[0])

    @pl.when(jnp.logical_and(i < n_used, rows > MOE_TILE // 2))
    def _():
        mlp(MOE_TILE)

    @pl.when(jnp.logical_and(i < n_used, rows <= MOE_TILE // 2))
    def _():
        mlp(MOE_TILE // 2)
        ys_ref[pl.ds(MOE_TILE // 2 * ROW_SUBTILES, MOE_TILE // 2 * ROW_SUBTILES), :] = jnp.zeros(
            (MOE_TILE // 2 * ROW_SUBTILES, LANES), ys_ref.dtype)

    @pl.when(i >= n_used)
    def _():
        ys_ref[...] = jnp.zeros_like(ys_ref)


def _experts(tile_expert, n_used, next_expert, weight_buf, n_valid, tile_first, run_n, run_loc,
             h_local, w_gu, b_gu, w_dn, b_dn):
    n_tiles = tile_expert.shape[0]
    n_token_tiles = h_local.shape[0] // (ROW_TILE * TOP_K * ROW_SUBTILES)
    f32, bf16 = jnp.float32, jnp.bfloat16
    tile_rows = MOE_TILE * ROW_SUBTILES
    grid_spec = pltpu.PrefetchScalarGridSpec(
        num_scalar_prefetch=8,
        grid=(n_tiles,),
        in_specs=[
            pl.BlockSpec((1, 1, 2 * D_FF), lambda i, te, *_: (te[i], 0, 0)),
            pl.BlockSpec((1, 1, D_MODEL), lambda i, te, *_: (te[i], 0, 0)),
            pl.BlockSpec(memory_space=pl.ANY), pl.BlockSpec(memory_space=pl.ANY),
            pl.BlockSpec(memory_space=pl.ANY),
        ],
        out_specs=pl.BlockSpec((tile_rows, LANES), lambda i, *_: (i, 0)),
        scratch_shapes=[
            pltpu.VMEM((2, tile_rows, LANES), f32),
            pltpu.VMEM((2, D_MODEL, 2 * D_FF), f32), pltpu.VMEM((2, D_FF, D_MODEL), f32),
            pltpu.SMEM((2,), jnp.int32),
            pltpu.SemaphoreType.DMA((2,)), pltpu.SemaphoreType.DMA((2, 2)),
        ],
    )
    return pl.pallas_call(
        functools.partial(_expert_kernel, n_token_tiles),
        out_shape=jax.ShapeDtypeStruct((n_tiles * tile_rows, LANES), f32),
        grid_spec=grid_spec,
        compiler_params=_params(),
        name="experts",
    )(tile_expert, n_used, next_expert, weight_buf, n_valid, tile_first, run_n, run_loc,
      b_gu.reshape(N_EXPERTS, 1, -1), b_dn.reshape(N_EXPERTS, 1, -1), h_local, w_gu, w_dn)


def _combine_kernel(n_prompt_tiles,
                    src_ref, n_ref, dst_ref, src_nx_ref, n_nx_ref, dst_nx_ref, slot_ref, gate_ref,
                    x1_ref, gfin_ref, ys_hbm,
                    yp_ref, yo_ref,
                    local, mixed, run_sem):
    i = pl.program_id(0)
    last = pl.num_programs(0) - 1
    buf = i % 2
    half_slots = ROW_TILE * TOP_K
    half_rows = half_slots * ROW_SUBTILES

    def fetch_runs(s_ref, c_ref, d_ref, b):
        for e in range(N_EXPERTS):
            n = c_ref[0, 0, e]
            copy = pltpu.make_async_copy(ys_hbm.at[_row_span(d_ref[0, 0, e], n)],
                                         local.at[_row_span(b * half_slots + s_ref[0, 0, e], n)],
                                         run_sem.at[b])
            pl.when(n > 0)(copy.start)

    @pl.when(i == 0)
    def _():
        fetch_runs(src_ref, n_ref, dst_ref, 0)

    @pl.when(i < last)
    def _():
        fetch_runs(src_nx_ref, n_nx_ref, dst_nx_ref, 1 - buf)

    pltpu.make_async_copy(ys_hbm.at[pl.ds(0, half_rows)],
                          local.at[pl.ds(pl.multiple_of(buf * half_rows, half_rows), half_rows)],
                          run_sem.at[buf]).wait()

    def blend(t, lane):
        acc = None
        for k in range(TOP_K):
            at = pl.multiple_of(slot_ref[0, 0, k * ROW_TILE + t], ROW_SUBTILES)
            term = gate_ref[0, 0, k * ROW_TILE + t] * local[pl.ds(at, ROW_SUBTILES), :]
            acc = term if acc is None else acc + term
        mixed[pl.ds(pl.multiple_of(t * ROW_SUBTILES, ROW_SUBTILES), ROW_SUBTILES), :] = acc
    _unrolled_rows(ROW_TILE, blend)

    out = _rms(x1_ref[...] + _load_row_tiled(mixed, (), ROW_TILE), gfin_ref[...])

    @pl.when(i < n_prompt_tiles)
    def _():
        yp_ref[...] = out

    @pl.when(i >= n_prompt_tiles)
    def _():
        yo_ref[...] = out


def _combine(run_src, run_n, run_dst, slots, gates, x1, ys, g_final, tp):
    t = x1.shape[0]
    npt = tp // ROW_TILE
    nt = t // ROW_TILE
    f32 = jnp.float32
    picks = ROW_TILE * TOP_K
    nxt = lambda i: jnp.minimum(i + 1, nt - 1)
    pick_spec = pl.BlockSpec((1, 1, picks), lambda i: (i, 0, 0), memory_space=pltpu.SMEM)
    return pl.pallas_call(
        functools.partial(_combine_kernel, npt),
        out_shape=(jax.ShapeDtypeStruct((tp, D_MODEL), f32),
                   jax.ShapeDtypeStruct((t - tp, D_MODEL), f32)),
        grid=(nt,),
        in_specs=[_run_spec(lambda i: i), _run_spec(lambda i: i), _run_spec(lambda i: i),
                  _run_spec(nxt), _run_spec(nxt), _run_spec(nxt), pick_spec, pick_spec,
                  _row_spec(D_MODEL), _const_spec((1, D_MODEL)), pl.BlockSpec(memory_space=pl.ANY)],
        out_specs=(_prompt_spec(D_MODEL, npt), _sample_spec(D_MODEL, npt)),
        scratch_shapes=[pltpu.VMEM((2 * picks * ROW_SUBTILES, LANES), f32),
                        pltpu.VMEM((ROW_TILE * ROW_SUBTILES, LANES), f32),
                        pltpu.SemaphoreType.DMA((2,))],
        compiler_params=_params(),
        name="combine",
    )(run_src, run_n, run_dst, run_src, run_n, run_dst, slots, gates, x1, g_final.reshape(1, -1), ys)


def _plan(meta, counts, t):
    nt = t // ROW_TILE
    n_tiles = (t * TOP_K + N_EXPERTS * (MOE_TILE - 1)) // MOE_TILE
    counts = counts[0, :N_EXPERTS]
    tiles_e = (counts + MOE_TILE - 1) // MOE_TILE
    tile_end = jnp.cumsum(tiles_e)
    tile_start = tile_end - tiles_e
    n_used = tile_end[-1]
    first_row = jnp.pad(tile_start * MOE_TILE, (0, LANES - N_EXPERTS))
    meta = meta.reshape(nt, 8, LANES)
    run_dst = meta[:, 0:1, :] + first_row[None, None, :]
    run_n = meta[:, 1:2, :]
    run_src = meta[:, 2:3, :]
    tile_ids = jnp.arange(n_tiles, dtype=jnp.int32)
    live = jnp.minimum(tile_ids, n_used - 1)
    tile_expert = jnp.sum(tile_end[None, :] <= live[:, None], axis=1).astype(jnp.int32)
    ids = jnp.arange(N_EXPERTS, dtype=jnp.int32)
    is_expert = tile_expert[:, None] == ids[None, :]
    of_tile = lambda per_expert: jnp.sum(jnp.where(is_expert, per_expert[None, :], 0), axis=1)
    in_expert = tile_ids - of_tile(tile_start)
    n_valid = jnp.clip(of_tile(counts) - in_expert * MOE_TILE, 0, MOE_TILE)
    n_valid = jnp.where(tile_ids < n_used, n_valid, 0).astype(jnp.int32)
    tile_first = jnp.logical_and(in_expert == 0, tile_ids < n_used).astype(jnp.int32)
    run_n_flat = run_n[:, 0, :N_EXPERTS].reshape(-1)
    run_loc_flat = run_src[:, 0, :N_EXPERTS].reshape(-1)
    used = tiles_e > 0
    later_used = jnp.where(jnp.logical_and(used[None, :], ids[None, :] > ids[:, None]), ids[None, :],
                           N_EXPERTS)
    following = jnp.min(later_used, axis=1)
    following = jnp.where(following < N_EXPERTS, following, ids)
    buf_of = (jnp.cumsum(used.astype(jnp.int32)) - 1) % 2
    return (tile_expert, n_used.reshape(1).astype(jnp.int32), of_tile(following).astype(jnp.int32),
            of_tile(buf_of).astype(jnp.int32), n_valid, tile_first, run_n_flat, run_loc_flat,
            run_src, run_n, run_dst)


def kernel(x_prompt, x_sample, cache_k, cache_v, g_mix, w_in, b_in, ln_v_g, ln_v_b, w_sp, b_sp,
           attn_sinks, w_pa, w_pb, w_o, g_ffn, w_router, b_router, w_gu, b_gu, w_dn, b_dn, g_final):
    nb, seq, d = x_prompt.shape
    nsb, nnew, _ = x_sample.shape
    tp, ts = nb * seq, nsb * nnew
    t = tp + ts
    xp = x_prompt.reshape(tp, d)
    xs = x_sample.reshape(ts, d)
    pos = np.concatenate([np.arange(seq), np.tile(PAST_LEN + np.arange(nnew), ROW_TILE // nnew)])
    tables = _rope_tables(pos)
    u, vln, v_sgu, q, k, v, kd, vd, ga, gb = _project(
        xp, xs, g_mix[0], w_in[0], b_in[0], ln_v_g[0], ln_v_b[0], tables, seq)
    x1, h_local, slot_t, gate_t, meta, counts = _mix(
        xp, xs, u, vln, q, kd, vd, cache_k[0], cache_v[0], ga, gb, w_sp[0], b_sp[0], attn_sinks[0],
        w_pa[0], w_pb[0], w_o[0], g_ffn[0], w_router[0], b_router[0], seq)
    nt = t // ROW_TILE
    picks = lambda a: a.reshape(nt, 8, ROW_TILE)[:, :TOP_K, :].reshape(nt, 1, TOP_K * ROW_TILE)
    slots, gates = picks(slot_t), picks(gate_t)
    (tile_expert, n_used, next_expert, weight_buf, n_valid, tile_first, run_n_flat, run_loc_flat,
     run_src, run_n, run_dst) = _plan(meta, counts, t)
    y_sorted = _experts(tile_expert, n_used, next_expert, weight_buf, n_valid, tile_first, run_n_flat,
                        run_loc_flat, h_local, w_gu[0], b_gu[0], w_dn[0], b_dn[0])
    y_p, y_s = _combine(run_src, run_n, run_dst, slots, gates, x1, y_sorted, g_final, tp)

    keep = min(WINDOW, seq)
    tails = lambda a: jnp.stack([a[(b + 1) * seq - keep:(b + 1) * seq] for b in range(nb)]).reshape(
        nb, keep, N_KV_HEADS, HEAD_DIM)
    kp, vp = tails(k), tails(v)
    ks = k[tp:].reshape(nsb, nnew, N_KV_HEADS, HEAD_DIM)
    vs = v[tp:].reshape(nsb, nnew, N_KV_HEADS, HEAD_DIM)
    return (y_p.reshape(nb, seq, d), y_s.reshape(nsb, nnew, d), kp[None], vp[None], ks[None],
            vs[None], v_sgu.reshape(1, nsb, nnew, SGU_WIDTH))
```

```python
import functools

import numpy as np
import jax
import jax.numpy as jnp
from jax import lax
from jax.experimental import pallas as pl
from jax.experimental.pallas import tpu as pltpu

D_MODEL = 1024
PAST_LEN = 2048
CHUNK = 64
SGU_CHUNK = 128
SGU_GROUPS = 8
SGU_WIDTH = 1024
N_HEADS = 16
N_KV_HEADS = 4
HEAD_DIM = 64
Q_PER_KV = N_HEADS // N_KV_HEADS
WINDOW = 128
ROT_DIM = HEAD_DIM // 4
ROPE_THETA = 500000.0
ATT_W = N_HEADS * HEAD_DIM
KV_W = N_KV_HEADS * HEAD_DIM
N_EXPERTS = 32
TOP_K = 4
D_FF = 1024
SWIGLU_ALPHA = 1.702
SWIGLU_LIMIT = 7.0
NORM_EPS = 1e-5
NEG_INF = -1e30
N_IN = SGU_WIDTH * 2 + ATT_W + KV_W * 2 + D_MODEL * 2

LANES = 128
ROW_TILE = 256
MOE_TILE = 512
ROW_UNROLL = 8
COL_BLOCK = 256
KV_DUP_W = N_KV_HEADS * LANES
KEY_SPAN = WINDOW + CHUNK
VMEM_LIMIT = 56 * 1024 * 1024

_SQRT_HALF = 0.7071067811865476
_LOG2_E = 1.4426950408889634


def _gelu(x):
    t = 1.0 / (1.0 + (0.3275911 * _SQRT_HALF) * jnp.abs(x))
    half_poly = t * (0.127414796 + t * (-0.142248368 + t * (0.7107068705
                     + t * (-0.7265760135 + t * 0.5307027145))))
    half_tail = x * (half_poly * jnp.exp2(x * x * (-0.5 * _LOG2_E)))
    return jnp.where(x >= 0.0, x - half_tail, half_tail)


def _sigmoid(x):
    return 1.0 / (1.0 + jnp.exp(-x))


def _bf16(x):
    return x.astype(jnp.bfloat16)


def _dot(a, b):
    return jnp.dot(a, b, preferred_element_type=jnp.float32)


ROW_SUBTILES = D_MODEL // LANES


def _store_row_tiled(ref, lead, x):
    rows = x.shape[0]
    for s in range(ROW_SUBTILES):
        ref[(*lead, pl.ds(s, rows, stride=ROW_SUBTILES), slice(None))] = x[:, s * LANES:(s + 1) * LANES]


def _load_row_tiled(ref, lead, rows):
    return jnp.concatenate(
        [ref[(*lead, pl.ds(s, rows, stride=ROW_SUBTILES), slice(None))] for s in range(ROW_SUBTILES)],
        axis=1)


def _dot_nt(a, b):
    return lax.dot_general(a, b, (((1,), (1,)), ((), ())), preferred_element_type=jnp.float32)


def _rms(x, g):
    return x * lax.rsqrt(jnp.mean(x * x, axis=-1, keepdims=True) + NORM_EPS) * g


def _lane_lo(rows):
    return lax.broadcasted_iota(jnp.int32, (rows, LANES), 1) < HEAD_DIM


def _dup_heads(kv):
    rows = kv.shape[0]
    lo = _lane_lo(rows)
    out = []
    for j in range(KV_W // LANES):
        blk = kv[:, j * LANES:(j + 1) * LANES]
        swp = pltpu.roll(blk, HEAD_DIM, axis=1)
        out.append(jnp.where(lo, blk, swp))
        out.append(jnp.where(lo, swp, blk))
    return _bf16(jnp.concatenate(out, axis=1))


def _rope_block(zb, cos_b, sin_lo, sin_hi):
    up = pltpu.roll(zb, LANES - ROT_DIM // 2, axis=1)
    dn = pltpu.roll(zb, ROT_DIM // 2, axis=1)
    return zb * cos_b + up * sin_lo + dn * sin_hi


def _proj_kernel(n_prompt_tiles, xp_ref, xs_ref, gmix_ref, w_ref, b_ref, lng_ref, lnb_ref,
                 cos_ref, slo_ref, shi_ref,
                 u_ref, vln_ref, vs_ref, q_ref, k_ref, v_ref, kd_ref, vd_ref, ga_ref, gb_ref):
    i = pl.program_id(0)
    x = jnp.where(i < n_prompt_tiles, xp_ref[...], xs_ref[...])
    h = _bf16(_rms(x, gmix_ref[...]))

    cos_b, sin_lo, sin_hi = cos_ref[...], slo_ref[...], shi_ref[...]
    off_u, off_v, off_q, off_k, off_vv, off_ga, off_gb = (
        int(o) for o in np.cumsum((0, SGU_WIDTH, SGU_WIDTH, ATT_W, KV_W, KV_W, D_MODEL)))

    def z(lo):
        return _dot(h, w_ref[:, lo:lo + COL_BLOCK]) + b_ref[:, lo:lo + COL_BLOCK]

    def cols(j):
        return slice(j * COL_BLOCK, (j + 1) * COL_BLOCK)

    def rope(zb):
        return jnp.concatenate(
            [_rope_block(zb[:, t * LANES:(t + 1) * LANES], cos_b, sin_lo, sin_hi)
             for t in range(COL_BLOCK // LANES)], axis=1)

    gelu_v = []

    def do_u(j):
        u_ref[:, cols(j)] = _bf16(z(off_u + j * COL_BLOCK))

    def do_v(j):
        gelu_v.append(_gelu(z(off_v + j * COL_BLOCK)))

    def do_q(j):
        q_ref[:, cols(j)] = _bf16(rope(z(off_q + j * COL_BLOCK)) * (HEAD_DIM ** -0.5))

    def do_k(j):
        kr = rope(z(off_k))
        k_ref[...] = kr
        kd_ref[...] = _dup_heads(kr)

    def do_vv(j):
        zv = z(off_vv)
        v_ref[...] = zv
        vd_ref[...] = _dup_heads(zv)

    def do_ga(j):
        ga_ref[:, cols(j)] = _bf16(_sigmoid(z(off_ga + j * COL_BLOCK)))

    def do_gb(j):
        gb_ref[:, cols(j)] = _bf16(_sigmoid(z(off_gb + j * COL_BLOCK)))

    order = ((do_v, 0), (do_q, 0), (do_q, 1), (do_u, 0), (do_v, 1), (do_q, 2), (do_q, 3), (do_u, 1),
             (do_v, 2), (do_k, 0), (do_vv, 0), (do_u, 2), (do_v, 3), (do_ga, 0), (do_ga, 1), (do_u, 3),
             (do_ga, 2), (do_ga, 3), (do_gb, 0), (do_gb, 1), (do_gb, 2), (do_gb, 3))
    for fn, j in order:
        fn(j)
    gv = jnp.concatenate(gelu_v, axis=1)
    gc = gv - jnp.mean(gv, axis=-1, keepdims=True)
    var = jnp.mean(gc * gc, axis=-1, keepdims=True)
    vln = gc * lax.rsqrt(var + NORM_EPS) * lng_ref[...] + lnb_ref[...]
    vln_ref[...] = _bf16(vln)
    vs_ref[...] = vln


def _rope_tables(pos):
    half = ROT_DIM // 2
    inv = np.float32(ROPE_THETA) ** (-np.arange(half, dtype=np.float32) * np.float32(2.0) / ROT_DIM)
    ang = pos.astype(np.float32)[:, None] * inv.astype(np.float32)[None, :]
    cos = np.cos(ang.astype(np.float64)).astype(np.float32)
    sin = np.sin(ang.astype(np.float64)).astype(np.float32)
    n = pos.shape[0]
    ones = np.ones((n, HEAD_DIM - ROT_DIM), np.float32)
    zeros = np.zeros((n, HEAD_DIM - ROT_DIM), np.float32)
    zh = np.zeros((n, half), np.float32)
    cos_h = np.concatenate([cos, cos, ones], axis=1)
    slo_h = np.concatenate([-sin, zh, zeros], axis=1)
    shi_h = np.concatenate([zh, sin, zeros], axis=1)
    rep = LANES // HEAD_DIM
    return tuple(jnp.asarray(np.tile(a, (1, rep))) for a in (cos_h, slo_h, shi_h))


def _row_spec(width):
    return pl.BlockSpec((ROW_TILE, width), lambda i: (i, 0))


def _const_spec(shape):
    return pl.BlockSpec(shape, lambda i: (0,) * len(shape))


def _prompt_spec(width, n_prompt_tiles):
    return pl.BlockSpec((ROW_TILE, width), lambda i: (jnp.minimum(i, n_prompt_tiles - 1), 0))


def _sample_spec(width, n_prompt_tiles):
    return pl.BlockSpec((ROW_TILE, width), lambda i: (jnp.maximum(i - n_prompt_tiles, 0), 0))


def _params():
    return pltpu.CompilerParams(dimension_semantics=("arbitrary",), vmem_limit_bytes=VMEM_LIMIT)


def _project(xp, xs, g_mix, w_in, b_in, ln_g, ln_b, tables, seq):
    tp, ts = xp.shape[0], xs.shape[0]
    t = tp + ts
    npt = tp // ROW_TILE
    tiles_per_seq = seq // ROW_TILE
    f32, bf16 = jnp.float32, jnp.bfloat16
    table_spec = pl.BlockSpec(
        (ROW_TILE, LANES), lambda i: (jnp.where(i < npt, i % tiles_per_seq, tiles_per_seq), 0))
    out_shape = (
        jax.ShapeDtypeStruct((t, SGU_WIDTH), bf16),
        jax.ShapeDtypeStruct((t, SGU_WIDTH), bf16),
        jax.ShapeDtypeStruct((ts, SGU_WIDTH), f32),
        jax.ShapeDtypeStruct((t, ATT_W), bf16),
        jax.ShapeDtypeStruct((t, KV_W), f32),
        jax.ShapeDtypeStruct((t, KV_W), f32),
        jax.ShapeDtypeStruct((t, KV_DUP_W), bf16),
        jax.ShapeDtypeStruct((t, KV_DUP_W), bf16),
        jax.ShapeDtypeStruct((t, D_MODEL), bf16),
        jax.ShapeDtypeStruct((t, D_MODEL), bf16),
    )
    return pl.pallas_call(
        functools.partial(_proj_kernel, npt),
        out_shape=out_shape,
        grid=(t // ROW_TILE,),
        in_specs=[
            _prompt_spec(D_MODEL, npt), _sample_spec(D_MODEL, npt),
            _const_spec((1, D_MODEL)), _const_spec((D_MODEL, N_IN)),
            _const_spec((1, N_IN)), _const_spec((1, SGU_WIDTH)), _const_spec((1, SGU_WIDTH)),
            table_spec, table_spec, table_spec,
        ],
        out_specs=(
            _row_spec(SGU_WIDTH), _row_spec(SGU_WIDTH), _sample_spec(SGU_WIDTH, npt),
            _row_spec(ATT_W), _row_spec(KV_W), _row_spec(KV_W), _row_spec(KV_DUP_W),
            _row_spec(KV_DUP_W), _row_spec(D_MODEL), _row_spec(D_MODEL),
        ),
        compiler_params=_params(),
        name="proj",
    )(xp, xs, g_mix.reshape(1, -1), w_in.astype(bf16), b_in.reshape(1, -1),
      ln_g.reshape(1, -1), ln_b.reshape(1, -1), *tables)


def _attend(qa, qb, kwin, vwin, sink, valid):
    lo = _lane_lo(CHUNK)
    zero = jnp.zeros_like(qa)
    lhs = jnp.concatenate([jnp.where(lo, qa, zero), jnp.where(lo, zero, qa),
                           jnp.where(lo, qb, zero), jnp.where(lo, zero, qb)], axis=0)
    s = _dot_nt(lhs, kwin)
    if valid is not None:
        s = jnp.where(valid, s, NEG_INF)
    s_a, s_b = s[:, :LANES], s[:, LANES:]
    tail = s_b.shape[1]
    m = jnp.maximum(jnp.max(s, axis=-1, keepdims=True), sink)
    p_a = jnp.exp(s_a - m)
    p_b = jnp.exp(s_b - m[:, :tail])
    denom = (jnp.sum(jnp.concatenate([p_a, p_b], axis=1), axis=-1, keepdims=True)
             + jnp.exp(sink - m))
    inv = 1.0 / denom
    pn = jnp.concatenate([p_a * inv, p_b * inv[:, :tail]], axis=1)
    r = _dot(_bf16(pn), vwin)
    oa = jnp.where(lo, r[0:CHUNK], r[CHUNK:2 * CHUNK])
    ob = jnp.where(lo, r[2 * CHUNK:3 * CHUNK], r[3 * CHUNK:4 * CHUNK])
    return oa, ob


def _stack_rows(rows):
    ri = lax.broadcasted_iota(jnp.int32, (8, rows[0].shape[1]), 0)
    out = jnp.zeros((8, rows[0].shape[1]), rows[0].dtype)
    for k, row in enumerate(rows):
        out = jnp.where(ri == k, row, out)
    return out


def _route_pick(logits_t):
    rows = logits_t.shape[1]
    eid = lax.broadcasted_iota(jnp.int32, (N_EXPERTS, rows), 0)
    work = logits_t
    vals, idxs = [], []
    for _ in range(TOP_K):
        m = jnp.max(work, axis=0, keepdims=True)
        idx = jnp.min(jnp.where(work == m, eid, N_EXPERTS), axis=0, keepdims=True)
        vals.append(m)
        idxs.append(idx)
        work = jnp.where(eid == idx, -jnp.inf, work)
    exps = [jnp.exp(v - vals[0]) for v in vals]
    inv = 1.0 / (exps[0] + exps[1] + exps[2] + exps[3])
    gates = _stack_rows([e * inv for e in exps])

    picked = jnp.zeros((N_EXPERTS, rows), jnp.float32)
    for idx in idxs:
        picked = jnp.where(eid == idx, 1.0, picked)
    tr = lax.broadcasted_iota(jnp.int32, (rows, rows), 0)
    tc = lax.broadcasted_iota(jnp.int32, (rows, rows), 1)
    earlier = _bf16(jnp.where(tr < tc, 1.0, 0.0))
    in_tile = _dot(_bf16(picked), earlier)
    count_col = jnp.broadcast_to(jnp.sum(picked, axis=1, keepdims=True), (N_EXPERTS, LANES))
    eid_wide = lax.broadcasted_iota(jnp.int32, (LANES, rows), 0)
    picked_wide = jnp.zeros((LANES, rows), jnp.float32)
    for idx in idxs:
        picked_wide = jnp.where(eid_wide == idx, 1.0, picked_wide)
    count_row = _dot_nt(jnp.ones((8, rows), jnp.bfloat16), _bf16(picked_wide))
    return idxs, gates, in_tile, count_col, count_row


def _route_place(idxs, in_tile, count_col, count_row, carry_s, live):
    rows = in_tile.shape[1]
    eid = lax.broadcasted_iota(jnp.int32, (N_EXPERTS, rows), 0)
    er = lax.broadcasted_iota(jnp.int32, (N_EXPERTS, N_EXPERTS), 0)
    ec = lax.broadcasted_iota(jnp.int32, (N_EXPERTS, N_EXPERTS), 1)
    start_col = _dot(_bf16(jnp.where(ec < er, 1.0, 0.0)), _bf16(count_col))
    local = in_tile + jnp.concatenate([start_col] * (rows // LANES), axis=1)
    slots = _stack_rows([jnp.sum(jnp.where(eid == idx, local, 0.0), axis=0, keepdims=True)
                         for idx in idxs]).astype(jnp.int32)
    lr = lax.broadcasted_iota(jnp.int32, (LANES, LANES), 0)
    lc = lax.broadcasted_iota(jnp.int32, (LANES, LANES), 1)
    start_row = _dot(_bf16(count_row), _bf16(jnp.where(lr < lc, 1.0, 0.0)))
    ri = lax.broadcasted_iota(jnp.int32, (8, LANES), 0)
    meta = jnp.where(ri == 0, carry_s[...], jnp.where(ri == 1, count_row, jnp.where(ri == 2, start_row, 0.0)))
    carry_s[...] = carry_s[...] + count_row * live
    return slots, meta.astype(jnp.int32)


def _mix_kernel(tiles_per_seq, n_prompt_tiles,
                xp_ref, xs_ref, u_ref, vln_ref, q_ref, kd_ref, vd_ref, kdp_ref, vdp_ref,
                ck_ref, cv_ref, ga_ref, gb_ref, wsp_ref, bsp_ref, sink_ref,
                wpa_ref, wpb_ref, wo_ref, gffn_ref, wrh_ref, wrl_ref, br_ref,
                x1_ref, hloc_ref, slot_ref, gate_ref, meta_ref, count_ref,
                a_s, o_s, kwin_s, vwin_s, carry_s, hhi_s, hlo_s):
    i = pl.program_id(0)
    n_streams = ROW_TILE // CHUNK

    @pl.when(i == 0)
    def _():
        carry_s[...] = jnp.zeros_like(carry_s)
        hhi_s[...] = jnp.zeros_like(hhi_s)
        hlo_s[...] = jnp.zeros_like(hlo_s)

    def sgu_rows(r0, rows):
        ri = lax.broadcasted_iota(jnp.int32, (rows, rows), 0) // CHUNK
        ci = lax.broadcasted_iota(jnp.int32, (rows, rows), 1) // CHUNK
        for g in range(SGU_GROUPS):
            cols = slice(g * LANES, (g + 1) * LANES)
            w = _bf16(jnp.where(ci <= ri, wsp_ref[g, :rows, :rows], 0.0))
            sp = _dot(w, vln_ref[r0:r0 + rows, cols]) + bsp_ref[g, :rows, :]
            a_s[r0:r0 + rows, cols] = _bf16(_gelu(u_ref[r0:r0 + rows, cols].astype(jnp.float32)) * sp)

    def attend_rows(r0, kwin_of, valid):
        for g in range(N_KV_HEADS):
            c0 = g * Q_PER_KV * HEAD_DIM
            kwin, vwin = kwin_of(g)
            oa, ob = _attend(q_ref[r0:r0 + CHUNK, c0:c0 + LANES],
                             q_ref[r0:r0 + CHUNK, c0 + LANES:c0 + 2 * LANES],
                             kwin, vwin, sink_ref[g], valid)
            o_s[r0:r0 + CHUNK, c0:c0 + LANES] = _bf16(oa)
            o_s[r0:r0 + CHUNK, c0 + LANES:c0 + 2 * LANES] = _bf16(ob)

    @pl.when(i < n_prompt_tiles)
    def _prompt():
        for c in range(ROW_TILE // SGU_CHUNK):
            sgu_rows(c * SGU_CHUNK, SGU_CHUNK)
        kwin_s[0:WINDOW] = kdp_ref[...]
        kwin_s[WINDOW:WINDOW + ROW_TILE] = kd_ref[...]
        vwin_s[0:WINDOW] = vdp_ref[...]
        vwin_s[WINDOW:WINDOW + ROW_TILE] = vd_ref[...]
        first = (i % tiles_per_seq) == 0
        col = lax.broadcasted_iota(jnp.int32, (1, KEY_SPAN), 1)
        for j in range(ROW_TILE // CHUNK):
            r0 = j * CHUNK
            valid = jnp.logical_or(jnp.logical_not(first), col + r0 >= WINDOW) if r0 < WINDOW else None

            def kwin_of(g, r0=r0):
                cols = slice(g * LANES, (g + 1) * LANES)
                return kwin_s[r0:r0 + KEY_SPAN, cols], vwin_s[r0:r0 + KEY_SPAN, cols]

            attend_rows(r0, kwin_of, valid)

    @pl.when(i >= n_prompt_tiles)
    def _sample():
        for s in range(n_streams):
            r0 = s * CHUNK
            sgu_rows(r0, CHUNK)
            kwin_s[0:WINDOW] = _dup_heads(ck_ref[s])
            kwin_s[WINDOW:KEY_SPAN] = kd_ref[r0:r0 + CHUNK]
            vwin_s[0:WINDOW] = _dup_heads(cv_ref[s])
            vwin_s[WINDOW:KEY_SPAN] = vd_ref[r0:r0 + CHUNK]

            def kwin_of(g):
                cols = slice(g * LANES, (g + 1) * LANES)
                return kwin_s[0:KEY_SPAN, cols], vwin_s[0:KEY_SPAN, cols]

            attend_rows(r0, kwin_of, None)

    hh, hl = hhi_s[...], hlo_s[...]
    logits_t = (_dot_nt(wrh_ref[...], hh) + _dot_nt(wrl_ref[...], hh) + _dot_nt(wrh_ref[...], hl)
                + jnp.concatenate([br_ref[...]] * (ROW_TILE // LANES), axis=1))
    m_a = ga_ref[...].astype(jnp.float32) * _dot(a_s[...], _bf16(wpa_ref[...]))
    idxs, gates, in_tile, count_col, count_row = _route_pick(logits_t)
    gate_ref[...] = gates
    m = m_a + gb_ref[...].astype(jnp.float32) * _dot(o_s[...], _bf16(wpb_ref[...]))
    slots, meta = _route_place(idxs, in_tile, count_col, count_row, carry_s, jnp.where(i > 0, 1.0, 0.0))
    half = ((i + 1) % 2) * (ROW_TILE * TOP_K * ROW_SUBTILES)
    slot_ref[...] = slots * ROW_SUBTILES + half
    meta_ref[...] = meta
    count_ref[...] = carry_s[...].astype(jnp.int32)
    n_slots = ROW_TILE * TOP_K
    sid = lax.broadcasted_iota(jnp.int32, (n_slots, ROW_TILE), 0)
    place = jnp.zeros((n_slots, ROW_TILE), jnp.float32)
    for k in range(TOP_K):
        place = jnp.where(sid == slots[k:k + 1, :], 1.0, place)
    _store_row_tiled(hloc_ref, (), _dot(_bf16(place), hh))

    x = jnp.where(i < n_prompt_tiles, xp_ref[...], xs_ref[...])
    x1 = x + _dot(_bf16(m), _bf16(wo_ref[...]))
    x1_ref[...] = x1
    h2 = _rms(x1, gffn_ref[...])
    h2_hi = _bf16(h2)
    hhi_s[...] = h2_hi
    hlo_s[...] = _bf16(h2 - h2_hi.astype(jnp.float32))


def _mix(xp, xs, u, vln, q, kd, vd, cache_k, cache_v, ga, gb, w_sp, b_sp, sinks,
         w_pa, w_pb, w_o, g_ffn, w_router, b_router, seq):
    tp, ts = xp.shape[0], xs.shape[0]
    t = tp + ts
    npt = tp // ROW_TILE
    tiles_per_seq = seq // ROW_TILE
    f32, bf16 = jnp.float32, jnp.bfloat16
    n_streams = ROW_TILE // CHUNK
    win_per_tile = ROW_TILE // WINDOW

    nt = t // ROW_TILE
    cur = lambda i: jnp.minimum(i, nt - 1)
    smp = lambda i: jnp.maximum(cur(i) - npt, 0)
    row = lambda width: pl.BlockSpec((ROW_TILE, width), lambda i: (cur(i), 0))
    prev_spec = pl.BlockSpec(
        (WINDOW, KV_DUP_W), lambda i: (jnp.maximum(jnp.minimum(i, npt - 1) * win_per_tile - 1, 0), 0))
    cache_spec = pl.BlockSpec((n_streams, WINDOW, KV_W), lambda i: (smp(i), 0, 0))
    xs_spec = pl.BlockSpec((ROW_TILE, D_MODEL), lambda i: (smp(i), 0))
    sink_cols = jnp.broadcast_to(
        jnp.repeat(sinks.astype(f32).reshape(N_KV_HEADS, Q_PER_KV), CHUNK, axis=1)[:, :, None],
        (N_KV_HEADS, Q_PER_KV * CHUNK, LANES))
    wr_t = w_router.T
    wr_hi = wr_t.astype(bf16)
    wr_lo = (wr_t - wr_hi.astype(f32)).astype(bf16)
    routed8 = lambda width: pl.BlockSpec((8, width), lambda i: (jnp.maximum(i - 1, 0), 0))
    out_shape = (
        jax.ShapeDtypeStruct((t, D_MODEL), f32),
        jax.ShapeDtypeStruct((t * TOP_K * ROW_SUBTILES, LANES), f32),
        jax.ShapeDtypeStruct((nt * 8, ROW_TILE), jnp.int32),
        jax.ShapeDtypeStruct((nt * 8, ROW_TILE), f32),
        jax.ShapeDtypeStruct((nt * 8, LANES), jnp.int32),
        jax.ShapeDtypeStruct((8, LANES), jnp.int32),
    )
    return pl.pallas_call(
        functools.partial(_mix_kernel, tiles_per_seq, npt),
        out_shape=out_shape,
        grid=(nt + 1,),
        in_specs=[
            _prompt_spec(D_MODEL, npt), xs_spec,
            row(SGU_WIDTH), row(SGU_WIDTH), row(ATT_W),
            row(KV_DUP_W), row(KV_DUP_W), prev_spec, prev_spec,
            cache_spec, cache_spec, row(D_MODEL), row(D_MODEL),
            _const_spec((SGU_GROUPS, SGU_CHUNK, SGU_CHUNK)), _const_spec((SGU_GROUPS, SGU_CHUNK, LANES)),
            _const_spec((N_KV_HEADS, Q_PER_KV * CHUNK, LANES)),
            _const_spec((SGU_WIDTH, D_MODEL)), _const_spec((ATT_W, D_MODEL)),
            _const_spec((D_MODEL, D_MODEL)), _const_spec((1, D_MODEL)),
            _const_spec((N_EXPERTS, D_MODEL)), _const_spec((N_EXPERTS, D_MODEL)),
            _const_spec((N_EXPERTS, LANES)),
        ],
        out_specs=(row(D_MODEL),
                   pl.BlockSpec((ROW_TILE * TOP_K * ROW_SUBTILES, LANES),
                                lambda i: (jnp.maximum(i - 1, 0), 0)),
                   routed8(ROW_TILE), routed8(ROW_TILE), routed8(LANES), _const_spec((8, LANES))),
        scratch_shapes=[
            pltpu.VMEM((ROW_TILE, SGU_WIDTH), bf16), pltpu.VMEM((ROW_TILE, ATT_W), bf16),
            pltpu.VMEM((WINDOW + ROW_TILE, KV_DUP_W), bf16),
            pltpu.VMEM((WINDOW + ROW_TILE, KV_DUP_W), bf16),
            pltpu.VMEM((8, LANES), f32),
            pltpu.VMEM((ROW_TILE, D_MODEL), bf16), pltpu.VMEM((ROW_TILE, D_MODEL), bf16),
        ],
        compiler_params=_params(),
        name="mix",
    )(xp, xs, u, vln, q, kd, vd, kd, vd,
      cache_k.reshape(-1, WINDOW, KV_W), cache_v.reshape(-1, WINDOW, KV_W), ga, gb,
      w_sp, jnp.broadcast_to(b_sp[:, :, None], (SGU_GROUPS, SGU_CHUNK, LANES)), sink_cols,
      w_pa, w_pb, w_o,
      g_ffn.reshape(1, -1), wr_hi, wr_lo,
      jnp.broadcast_to(b_router.astype(f32)[:, None], (N_EXPERTS, LANES)))


def _unrolled_rows(n_rows, fn):
    if isinstance(n_rows, int):
        groups, tail_start = n_rows // ROW_UNROLL, n_rows - n_rows % ROW_UNROLL
    else:
        groups = lax.shift_right_logical(n_rows, ROW_UNROLL.bit_length() - 1)
        tail_start = groups * ROW_UNROLL

    def group(gi, carry):
        for lane in range(ROW_UNROLL):
            fn(gi * ROW_UNROLL + lane, lane)
        return carry

    def tail(r, carry):
        fn(r, 0)
        return carry

    lax.fori_loop(0, groups, group, 0)
    lax.fori_loop(tail_start, n_rows, tail, 0)


def _row_span(first_row, n_rows):
    return pl.ds(pl.multiple_of(first_row * ROW_SUBTILES, ROW_SUBTILES),
                 pl.multiple_of(n_rows * ROW_SUBTILES, ROW_SUBTILES))


def _run_spec(index_of):
    return pl.BlockSpec((1, 1, LANES), lambda i, *_: (index_of(i), 0, 0), memory_space=pltpu.SMEM)


def _expert_kernel(n_token_tiles,
                   te_ref, nu_ref, nx_ref, par_ref, nv_ref, tf_ref, cnt_ref, loc_ref,
                   bgu_ref, bdn_ref, hloc_hbm, wgu_hbm, wdn_hbm,
                   ys_ref,
                   xbuf, wgu_f, wdn_f, walk, xsem, wsem):
    i = pl.program_id(0)
    n_used = nu_ref[0]
    expert = te_ref[i]
    buf = par_ref[i]
    slot = i % 2
    expert_changed = jnp.logical_or(i == 0, expert != te_ref[jnp.maximum(i - 1, 0)])

    def fetch_rows(j, b):
        e = te_ref[j]
        need = nv_ref[j]

        @pl.when(tf_ref[j] == 1)
        def _():
            walk[0] = 0
            walk[1] = 0

        @pl.when(need < MOE_TILE)
        def _():
            xbuf[b] = jnp.zeros(xbuf.shape[1:], xbuf.dtype)

        def unfinished(state):
            filled, tile, _ = state
            return jnp.logical_and(filled < need, tile < n_token_tiles)

        def take_run(state):
            filled, tile, off = state
            run = cnt_ref[tile * N_EXPERTS + e]
            take = jnp.minimum(run - off, need - filled)

            @pl.when(take > 0)
            def _():
                src = tile * (ROW_TILE * TOP_K) + loc_ref[tile * N_EXPERTS + e] + off
                pltpu.make_async_copy(hloc_hbm.at[_row_span(src, take)],
                                      xbuf.at[b, _row_span(filled, take)], xsem.at[b]).start()

            run_done = off + take == run
            return (filled + take, jnp.where(run_done, tile + 1, tile), jnp.where(run_done, 0, off + take))

        _, tile, off = lax.while_loop(unfinished, take_run, (jnp.int32(0), walk[0], walk[1]))
        walk[0] = tile
        walk[1] = off

    @pl.when(jnp.logical_and(i == 0, n_used > 0))
    def _():
        fetch_rows(0, 0)

    @pl.when(i + 1 < n_used)
    def _():
        fetch_rows(i + 1, 1 - slot)

    def weight_copies(e, b):
        return (pltpu.make_async_copy(wgu_hbm.at[e], wgu_f.at[b], wsem.at[0, b]),
                pltpu.make_async_copy(wdn_hbm.at[e], wdn_f.at[b], wsem.at[1, b]))

    @pl.when(jnp.logical_and(i < n_used, expert_changed))
    def _():
        @pl.when(i == 0)
        def _():
            for copy in weight_copies(expert, buf):
                copy.start()

        for copy in weight_copies(expert, buf):
            copy.wait()
        following = nx_ref[i]

        @pl.when(following != expert)
        def _():
            for copy in weight_copies(following, 1 - buf):
                copy.start()


    rows = nv_ref[i]

    @pl.when(i < n_used)
    def _():
        pltpu.make_async_copy(hloc_hbm.at[_row_span(0, rows)], xbuf.at[slot, _row_span(0, rows)],
                              xsem.at[slot]).wait()

    def mlp(n):
        x = _bf16(_load_row_tiled(xbuf, (slot,), n))
        gu = _dot(x, _bf16(wgu_f[buf])) + bgu_ref[0]
        gate = jnp.minimum(gu[:, :D_FF], SWIGLU_LIMIT)
        lin = jnp.clip(gu[:, D_FF:], -SWIGLU_LIMIT, SWIGLU_LIMIT)
        act = gate * _sigmoid(SWIGLU_ALPHA * gate) * (lin + 1.0)
        _store_row_tiled(ys_ref, (), _dot(_bf16(act), _bf16(wdn_f[buf])) + bdn_ref[0])

    @pl.when(jnp.logical_and(i < n_used, rows > MOE_TILE // 2))
    def _():
        mlp(MOE_TILE)

    @pl.when(jnp.logical_and(i < n_used, rows <= MOE_TILE // 2))
    def _():
        mlp(MOE_TILE // 2)
        ys_ref[pl.ds(MOE_TILE // 2 * ROW_SUBTILES, MOE_TILE // 2 * ROW_SUBTILES), :] = jnp.zeros(
            (MOE_TILE // 2 * ROW_SUBTILES, LANES), ys_ref.dtype)

    @pl.when(i >= n_used)
    def _():
        ys_ref[...] = jnp.zeros_like(ys_ref)


def _experts(tile_expert, n_used, next_expert, weight_buf, n_valid, tile_first, run_n, run_loc,
             h_local, w_gu, b_gu, w_dn, b_dn):
    n_tiles = tile_expert.shape[0]
    n_token_tiles = h_local.shape[0] // (ROW_TILE * TOP_K * ROW_SUBTILES)
    f32, bf16 = jnp.float32, jnp.bfloat16
    tile_rows = MOE_TILE * ROW_SUBTILES
    grid_spec = pltpu.PrefetchScalarGridSpec(
        num_scalar_prefetch=8,
        grid=(n_tiles,),
        in_specs=[
            pl.BlockSpec((1, 1, 2 * D_FF), lambda i, te, *_: (te[i], 0, 0)),
            pl.BlockSpec((1, 1, D_MODEL), lambda i, te, *_: (te[i], 0, 0)),
            pl.BlockSpec(memory_space=pl.ANY), pl.BlockSpec(memory_space=pl.ANY),
            pl.BlockSpec(memory_space=pl.ANY),
        ],
        out_specs=pl.BlockSpec((tile_rows, LANES), lambda i, *_: (i, 0)),
        scratch_shapes=[
            pltpu.VMEM((2, tile_rows, LANES), f32),
            pltpu.VMEM((2, D_MODEL, 2 * D_FF), f32), pltpu.VMEM((2, D_FF, D_MODEL), f32),
            pltpu.SMEM((2,), jnp.int32),
            pltpu.SemaphoreType.DMA((2,)), pltpu.SemaphoreType.DMA((2, 2)),
        ],
    )
    return pl.pallas_call(
        functools.partial(_expert_kernel, n_token_tiles),
        out_shape=jax.ShapeDtypeStruct((n_tiles * tile_rows, LANES), f32),
        grid_spec=grid_spec,
        compiler_params=_params(),
        name="experts",
    )(tile_expert, n_used, next_expert, weight_buf, n_valid, tile_first, run_n, run_loc,
      b_gu.reshape(N_EXPERTS, 1, -1), b_dn.reshape(N_EXPERTS, 1, -1), h_local, w_gu, w_dn)


def _combine_kernel(n_prompt_tiles,
                    src_ref, n_ref, dst_ref, src_nx_ref, n_nx_ref, dst_nx_ref, slot_ref, gate_ref,
                    x1_ref, gfin_ref, ys_hbm,
                    yp_ref, yo_ref,
                    local, mixed, run_sem):
    i = pl.program_id(0)
    last = pl.num_programs(0) - 1
    buf = i % 2
    half_slots = ROW_TILE * TOP_K
    half_rows = half_slots * ROW_SUBTILES

    def fetch_runs(s_ref, c_ref, d_ref, b):
        for e in range(N_EXPERTS):
            n = c_ref[0, 0, e]
            copy = pltpu.make_async_copy(ys_hbm.at[_row_span(d_ref[0, 0, e], n)],
                                         local.at[_row_span(b * half_slots + s_ref[0, 0, e], n)],
                                         run_sem.at[b])
            pl.when(n > 0)(copy.start)

    @pl.when(i == 0)
    def _():
        fetch_runs(src_ref, n_ref, dst_ref, 0)

    @pl.when(i < last)
    def _():
        fetch_runs(src_nx_ref, n_nx_ref, dst_nx_ref, 1 - buf)

    pltpu.make_async_copy(ys_hbm.at[pl.ds(0, half_rows)],
                          local.at[pl.ds(pl.multiple_of(buf * half_rows, half_rows), half_rows)],
                          run_sem.at[buf]).wait()

    def blend(t, lane):
        acc = None
        for k in range(TOP_K):
            at = pl.multiple_of(slot_ref[0, 0, k * ROW_TILE + t], ROW_SUBTILES)
            term = gate_ref[0, 0, k * ROW_TILE + t] * local[pl.ds(at, ROW_SUBTILES), :]
            acc = term if acc is None else acc + term
        mixed[pl.ds(pl.multiple_of(t * ROW_SUBTILES, ROW_SUBTILES), ROW_SUBTILES), :] = acc
    _unrolled_rows(ROW_TILE, blend)

    out = _rms(x1_ref[...] + _load_row_tiled(mixed, (), ROW_TILE), gfin_ref[...])

    @pl.when(i < n_prompt_tiles)
    def _():
        yp_ref[...] = out

    @pl.when(i >= n_prompt_tiles)
    def _():
        yo_ref[...] = out


def _combine(run_src, run_n, run_dst, slots, gates, x1, ys, g_final, tp):
    t = x1.shape[0]
    npt = tp // ROW_TILE
    nt = t // ROW_TILE
    f32 = jnp.float32
    picks = ROW_TILE * TOP_K
    nxt = lambda i: jnp.minimum(i + 1, nt - 1)
    pick_spec = pl.BlockSpec((1, 1, picks), lambda i: (i, 0, 0), memory_space=pltpu.SMEM)
    return pl.pallas_call(
        functools.partial(_combine_kernel, npt),
        out_shape=(jax.ShapeDtypeStruct((tp, D_MODEL), f32),
                   jax.ShapeDtypeStruct((t - tp, D_MODEL), f32)),
        grid=(nt,),
        in_specs=[_run_spec(lambda i: i), _run_spec(lambda i: i), _run_spec(lambda i: i),
                  _run_spec(nxt), _run_spec(nxt), _run_spec(nxt), pick_spec, pick_spec,
                  _row_spec(D_MODEL), _const_spec((1, D_MODEL)), pl.BlockSpec(memory_space=pl.ANY)],
        out_specs=(_prompt_spec(D_MODEL, npt), _sample_spec(D_MODEL, npt)),
        scratch_shapes=[pltpu.VMEM((2 * picks * ROW_SUBTILES, LANES), f32),
                        pltpu.VMEM((ROW_TILE * ROW_SUBTILES, LANES), f32),
                        pltpu.SemaphoreType.DMA((2,))],
        compiler_params=_params(),
        name="combine",
    )(run_src, run_n, run_dst, run_src, run_n, run_dst, slots, gates, x1, g_final.reshape(1, -1), ys)


def _plan(meta, counts, t):
    nt = t // ROW_TILE
    n_tiles = (t * TOP_K + N_EXPERTS * (MOE_TILE - 1)) // MOE_TILE
    counts = counts[0, :N_EXPERTS]
    tiles_e = (counts + MOE_TILE - 1) // MOE_TILE
    tile_end = jnp.cumsum(tiles_e)
    tile_start = tile_end - tiles_e
    n_used = tile_end[-1]
    first_row = jnp.pad(tile_start * MOE_TILE, (0, LANES - N_EXPERTS))
    meta = meta.reshape(nt, 8, LANES)
    run_dst = meta[:, 0:1, :] + first_row[None, None, :]
    run_n = meta[:, 1:2, :]
    run_src = meta[:, 2:3, :]
    tile_ids = jnp.arange(n_tiles, dtype=jnp.int32)
    live = jnp.minimum(tile_ids, n_used - 1)
    tile_expert = jnp.sum(tile_end[None, :] <= live[:, None], axis=1).astype(jnp.int32)
    ids = jnp.arange(N_EXPERTS, dtype=jnp.int32)
    is_expert = tile_expert[:, None] == ids[None, :]
    of_tile = lambda per_expert: jnp.sum(jnp.where(is_expert, per_expert[None, :], 0), axis=1)
    in_expert = tile_ids - of_tile(tile_start)
    n_valid = jnp.clip(of_tile(counts) - in_expert * MOE_TILE, 0, MOE_TILE)
    n_valid = jnp.where(tile_ids < n_used, n_valid, 0).astype(jnp.int32)
    tile_first = jnp.logical_and(in_expert == 0, tile_ids < n_used).astype(jnp.int32)
    run_n_flat = run_n[:, 0, :N_EXPERTS].reshape(-1)
    run_loc_flat = run_src[:, 0, :N_EXPERTS].reshape(-1)
    used = tiles_e > 0
    later_used = jnp.where(jnp.logical_and(used[None, :], ids[None, :] > ids[:, None]), ids[None, :],
                           N_EXPERTS)
    following = jnp.min(later_used, axis=1)
    following = jnp.where(following < N_EXPERTS, following, ids)
    buf_of = (jnp.cumsum(used.astype(jnp.int32)) - 1) % 2
    return (tile_expert, n_used.reshape(1).astype(jnp.int32), of_tile(following).astype(jnp.int32),
            of_tile(buf_of).astype(jnp.int32), n_valid, tile_first, run_n_flat, run_loc_flat,
            run_src, run_n, run_dst)


def kernel(x_prompt, x_sample, cache_k, cache_v, g_mix, w_in, b_in, ln_v_g, ln_v_b, w_sp, b_sp,
           attn_sinks, w_pa, w_pb, w_o, g_ffn, w_router, b_router, w_gu, b_gu, w_dn, b_dn, g_final):
    nb, seq, d = x_prompt.shape
    nsb, nnew, _ = x_sample.shape
    tp, ts = nb * seq, nsb * nnew
    t = tp + ts
    xp = x_prompt.reshape(tp, d)
    xs = x_sample.reshape(ts, d)
    pos = np.concatenate([np.arange(seq), np.tile(PAST_LEN + np.arange(nnew), ROW_TILE // nnew)])
    tables = _rope_tables(pos)
    u, vln, v_sgu, q, k, v, kd, vd, ga, gb = _project(
        xp, xs, g_mix[0], w_in[0], b_in[0], ln_v_g[0], ln_v_b[0], tables, seq)
    x1, h_local, slot_t, gate_t, meta, counts = _mix(
        xp, xs, u, vln, q, kd, vd, cache_k[0], cache_v[0], ga, gb, w_sp[0], b_sp[0], attn_sinks[0],
        w_pa[0], w_pb[0], w_o[0], g_ffn[0], w_router[0], b_router[0], seq)
    nt = t // ROW_TILE
    picks = lambda a: a.reshape(nt, 8, ROW_TILE)[:, :TOP_K, :].reshape(nt, 1, TOP_K * ROW_TILE)
    slots, gates = picks(slot_t), picks(gate_t)
    (tile_expert, n_used, next_expert, weight_buf, n_valid, tile_first, run_n_flat, run_loc_flat,
     run_src, run_n, run_dst) = _plan(meta, counts, t)
    y_sorted = _experts(tile_expert, n_used, next_expert, weight_buf, n_valid, tile_first, run_n_flat,
                        run_loc_flat, h_local, w_gu[0], b_gu[0], w_dn[0], b_dn[0])
    y_p, y_s = _combine(run_src, run_n, run_dst, slots, gates, x1, y_sorted, g_final, tp)

    keep = min(WINDOW, seq)
    tails = lambda a: jnp.stack([a[(b + 1) * seq - keep:(b + 1) * seq] for b in range(nb)]).reshape(
        nb, keep, N_KV_HEADS, HEAD_DIM)
    kp, vp = tails(k), tails(v)
    ks = k[tp:].reshape(nsb, nnew, N_KV_HEADS, HEAD_DIM)
    vs = v[tp:].reshape(nsb, nnew, N_KV_HEADS, HEAD_DIM)
    return (y_p.reshape(nb, seq, d), y_s.reshape(nsb, nnew, d), kp[None], vp[None], ks[None],
            vs[None], v_sgu.reshape(1, nsb, nnew, SGU_WIDTH))
```

```python
import functools

import numpy as np
import jax
import jax.numpy as jnp
from jax import lax
from jax.experimental import pallas as pl
from jax.experimental.pallas import tpu as pltpu

D_MODEL = 1024
PAST_LEN = 2048
CHUNK = 64
SGU_CHUNK = 128
SGU_GROUPS = 8
SGU_WIDTH = 1024
N_HEADS = 16
N_KV_HEADS = 4
HEAD_DIM = 64
Q_PER_KV = N_HEADS // N_KV_HEADS
WINDOW = 128
ROT_DIM = HEAD_DIM // 4
ROPE_THETA = 500000.0
ATT_W = N_HEADS * HEAD_DIM
KV_W = N_KV_HEADS * HEAD_DIM
N_EXPERTS = 32
TOP_K = 4
D_FF = 1024
SWIGLU_ALPHA = 1.702
SWIGLU_LIMIT = 7.0
NORM_EPS = 1e-5
NEG_INF = -1e30
N_IN = SGU_WIDTH * 2 + ATT_W + KV_W * 2 + D_MODEL * 2

LANES = 128
ROW_TILE = 256
MOE_TILE = 512
ROW_UNROLL = 8
COL_BLOCK = 256
KV_DUP_W = N_KV_HEADS * LANES
KEY_SPAN = WINDOW + CHUNK
VMEM_LIMIT = 56 * 1024 * 1024

_SQRT_HALF = 0.7071067811865476
_LOG2_E = 1.4426950408889634


def _gelu(x):
    t = 1.0 / (1.0 + (0.3275911 * _SQRT_HALF) * jnp.abs(x))
    half_poly = t * (0.127414796 + t * (-0.142248368 + t * (0.7107068705
                     + t * (-0.7265760135 + t * 0.5307027145))))
    half_tail = x * (half_poly * jnp.exp2(x * x * (-0.5 * _LOG2_E)))
    return jnp.where(x >= 0.0, x - half_tail, half_tail)


def _sigmoid(x):
    return 1.0 / (1.0 + jnp.exp(-x))


def _bf16(x):
    return x.astype(jnp.bfloat16)


def _dot(a, b):
    return jnp.dot(a, b, preferred_element_type=jnp.float32)


ROW_SUBTILES = D_MODEL // LANES


def _store_row_tiled(ref, lead, x):
    rows = x.shape[0]
    for s in range(ROW_SUBTILES):
        ref[(*lead, pl.ds(s, rows, stride=ROW_SUBTILES), slice(None))] = x[:, s * LANES:(s + 1) * LANES]


def _load_row_tiled(ref, lead, rows):
    return jnp.concatenate(
        [ref[(*lead, pl.ds(s, rows, stride=ROW_SUBTILES), slice(None))] for s in range(ROW_SUBTILES)],
        axis=1)


def _dot_nt(a, b):
    return lax.dot_general(a, b, (((1,), (1,)), ((), ())), preferred_element_type=jnp.float32)


def _rms(x, g):
    return x * lax.rsqrt(jnp.mean(x * x, axis=-1, keepdims=True) + NORM_EPS) * g


def _lane_lo(rows):
    return lax.broadcasted_iota(jnp.int32, (rows, LANES), 1) < HEAD_DIM


def _dup_heads(kv):
    rows = kv.shape[0]
    lo = _lane_lo(rows)
    out = []
    for j in range(KV_W // LANES):
        blk = kv[:, j * LANES:(j + 1) * LANES]
        swp = pltpu.roll(blk, HEAD_DIM, axis=1)
        out.append(jnp.where(lo, blk, swp))
        out.append(jnp.where(lo, swp, blk))
    return _bf16(jnp.concatenate(out, axis=1))


def _rope_block(zb, cos_b, sin_lo, sin_hi):
    up = pltpu.roll(zb, LANES - ROT_DIM // 2, axis=1)
    dn = pltpu.roll(zb, ROT_DIM // 2, axis=1)
    return zb * cos_b + up * sin_lo + dn * sin_hi


def _proj_kernel(n_prompt_tiles, xp_ref, xs_ref, gmix_ref, w_ref, b_ref, lng_ref, lnb_ref,
                 cos_ref, slo_ref, shi_ref,
                 u_ref, vln_ref, vs_ref, q_ref, k_ref, v_ref, kd_ref, vd_ref, ga_ref, gb_ref):
    i = pl.program_id(0)
    x = jnp.where(i < n_prompt_tiles, xp_ref[...], xs_ref[...])
    h = _bf16(_rms(x, gmix_ref[...]))

    cos_b, sin_lo, sin_hi = cos_ref[...], slo_ref[...], shi_ref[...]
    off_u, off_v, off_q, off_k, off_vv, off_ga, off_gb = (
        int(o) for o in np.cumsum((0, SGU_WIDTH, SGU_WIDTH, ATT_W, KV_W, KV_W, D_MODEL)))

    def z(lo):
        return _dot(h, w_ref[:, lo:lo + COL_BLOCK]) + b_ref[:, lo:lo + COL_BLOCK]

    def cols(j):
        return slice(j * COL_BLOCK, (j + 1) * COL_BLOCK)

    def rope(zb):
        return jnp.concatenate(
            [_rope_block(zb[:, t * LANES:(t + 1) * LANES], cos_b, sin_lo, sin_hi)
             for t in range(COL_BLOCK // LANES)], axis=1)

    gelu_v = []

    def do_u(j):
        u_ref[:, cols(j)] = _bf16(z(off_u + j * COL_BLOCK))

    def do_v(j):
        gelu_v.append(_gelu(z(off_v + j * COL_BLOCK)))

    def do_q(j):
        q_ref[:, cols(j)] = _bf16(rope(z(off_q + j * COL_BLOCK)) * (HEAD_DIM ** -0.5))

    def do_k(j):
        kr = rope(z(off_k))
        k_ref[...] = kr
        kd_ref[...] = _dup_heads(kr)

    def do_vv(j):
        zv = z(off_vv)
        v_ref[...] = zv
        vd_ref[...] = _dup_heads(zv)

    def do_ga(j):
        ga_ref[:, cols(j)] = _bf16(z(off_ga + j * COL_BLOCK))

    def do_gb(j):
        gb_ref[:, cols(j)] = _bf16(z(off_gb + j * COL_BLOCK))

    order = ((do_v, 0), (do_q, 0), (do_q, 1), (do_u, 0), (do_v, 1), (do_q, 2), (do_q, 3), (do_u, 1),
             (do_v, 2), (do_k, 0), (do_vv, 0), (do_u, 2), (do_v, 3), (do_ga, 0), (do_ga, 1), (do_u, 3),
             (do_ga, 2), (do_ga, 3), (do_gb, 0), (do_gb, 1), (do_gb, 2), (do_gb, 3))
    for fn, j in order:
        fn(j)
    gv = jnp.concatenate(gelu_v, axis=1)
    gc = gv - jnp.mean(gv, axis=-1, keepdims=True)
    var = jnp.mean(gc * gc, axis=-1, keepdims=True)
    vln = gc * lax.rsqrt(var + NORM_EPS) * lng_ref[...] + lnb_ref[...]
    vln_ref[...] = _bf16(vln)
    vs_ref[...] = vln


def _rope_tables(pos):
    half = ROT_DIM // 2
    inv = np.float32(ROPE_THETA) ** (-np.arange(half, dtype=np.float32) * np.float32(2.0) / ROT_DIM)
    ang = pos.astype(np.float32)[:, None] * inv.astype(np.float32)[None, :]
    cos = np.cos(ang.astype(np.float64)).astype(np.float32)
    sin = np.sin(ang.astype(np.float64)).astype(np.float32)
    n = pos.shape[0]
    ones = np.ones((n, HEAD_DIM - ROT_DIM), np.float32)
    zeros = np.zeros((n, HEAD_DIM - ROT_DIM), np.float32)
    zh = np.zeros((n, half), np.float32)
    cos_h = np.concatenate([cos, cos, ones], axis=1)
    slo_h = np.concatenate([-sin, zh, zeros], axis=1)
    shi_h = np.concatenate([zh, sin, zeros], axis=1)
    rep = LANES // HEAD_DIM
    return tuple(jnp.asarray(np.tile(a, (1, rep))) for a in (cos_h, slo_h, shi_h))


def _row_spec(width):
    return pl.BlockSpec((ROW_TILE, width), lambda i: (i, 0))


def _const_spec(shape):
    return pl.BlockSpec(shape, lambda i: (0,) * len(shape))


def _prompt_spec(width, n_prompt_tiles):
    return pl.BlockSpec((ROW_TILE, width), lambda i: (jnp.minimum(i, n_prompt_tiles - 1), 0))


def _sample_spec(width, n_prompt_tiles):
    return pl.BlockSpec((ROW_TILE, width), lambda i: (jnp.maximum(i - n_prompt_tiles, 0), 0))


def _params():
    return pltpu.CompilerParams(dimension_semantics=("arbitrary",), vmem_limit_bytes=VMEM_LIMIT)


def _project(xp, xs, g_mix, w_in, b_in, ln_g, ln_b, tables, seq):
    tp, ts = xp.shape[0], xs.shape[0]
    t = tp + ts
    npt = tp // ROW_TILE
    tiles_per_seq = seq // ROW_TILE
    f32, bf16 = jnp.float32, jnp.bfloat16
    table_spec = pl.BlockSpec(
        (ROW_TILE, LANES), lambda i: (jnp.where(i < npt, i % tiles_per_seq, tiles_per_seq), 0))
    out_shape = (
        jax.ShapeDtypeStruct((t, SGU_WIDTH), bf16),
        jax.ShapeDtypeStruct((t, SGU_WIDTH), bf16),
        jax.ShapeDtypeStruct((ts, SGU_WIDTH), f32),
        jax.ShapeDtypeStruct((t, ATT_W), bf16),
        jax.ShapeDtypeStruct((t, KV_W), f32),
        jax.ShapeDtypeStruct((t, KV_W), f32),
        jax.ShapeDtypeStruct((t, KV_DUP_W), bf16),
        jax.ShapeDtypeStruct((t, KV_DUP_W), bf16),
        jax.ShapeDtypeStruct((t, D_MODEL), bf16),
        jax.ShapeDtypeStruct((t, D_MODEL), bf16),
    )
    return pl.pallas_call(
        functools.partial(_proj_kernel, npt),
        out_shape=out_shape,
        grid=(t // ROW_TILE,),
        in_specs=[
            _prompt_spec(D_MODEL, npt), _sample_spec(D_MODEL, npt),
            _const_spec((1, D_MODEL)), _const_spec((D_MODEL, N_IN)),
            _const_spec((1, N_IN)), _const_spec((1, SGU_WIDTH)), _const_spec((1, SGU_WIDTH)),
            table_spec, table_spec, table_spec,
        ],
        out_specs=(
            _row_spec(SGU_WIDTH), _row_spec(SGU_WIDTH), _sample_spec(SGU_WIDTH, npt),
            _row_spec(ATT_W), _row_spec(KV_W), _row_spec(KV_W), _row_spec(KV_DUP_W),
            _row_spec(KV_DUP_W), _row_spec(D_MODEL), _row_spec(D_MODEL),
        ),
        compiler_params=_params(),
        name="proj",
    )(xp, xs, g_mix.reshape(1, -1), w_in.astype(bf16), b_in.reshape(1, -1),
      ln_g.reshape(1, -1), ln_b.reshape(1, -1), *tables)


def _attend(qa, qb, kwin, vwin, sink, valid):
    lo = _lane_lo(CHUNK)
    zero = jnp.zeros_like(qa)
    lhs = jnp.concatenate([jnp.where(lo, qa, zero), jnp.where(lo, zero, qa),
                           jnp.where(lo, qb, zero), jnp.where(lo, zero, qb)], axis=0)
    s = _dot_nt(lhs, kwin)
    if valid is not None:
        s = jnp.where(valid, s, NEG_INF)
    s_a, s_b = s[:, :LANES], s[:, LANES:]
    tail = s_b.shape[1]
    m = jnp.maximum(jnp.max(s, axis=-1, keepdims=True), sink)
    p_a = jnp.exp(s_a - m)
    p_b = jnp.exp(s_b - m[:, :tail])
    denom = (jnp.sum(jnp.concatenate([p_a, p_b], axis=1), axis=-1, keepdims=True)
             + jnp.exp(sink - m))
    inv = 1.0 / denom
    pn = jnp.concatenate([p_a * inv, p_b * inv[:, :tail]], axis=1)
    r = _dot(_bf16(pn), vwin)
    oa = jnp.where(lo, r[0:CHUNK], r[CHUNK:2 * CHUNK])
    ob = jnp.where(lo, r[2 * CHUNK:3 * CHUNK], r[3 * CHUNK:4 * CHUNK])
    return oa, ob


def _stack_rows(rows):
    ri = lax.broadcasted_iota(jnp.int32, (8, rows[0].shape[1]), 0)
    out = jnp.zeros((8, rows[0].shape[1]), rows[0].dtype)
    for k, row in enumerate(rows):
        out = jnp.where(ri == k, row, out)
    return out


def _route_pick(logits_t):
    rows = logits_t.shape[1]
    eid = lax.broadcasted_iota(jnp.int32, (N_EXPERTS, rows), 0)
    work = logits_t
    vals, idxs = [], []
    for _ in range(TOP_K):
        m = jnp.max(work, axis=0, keepdims=True)
        idx = jnp.min(jnp.where(work == m, eid, N_EXPERTS), axis=0, keepdims=True)
        vals.append(m)
        idxs.append(idx)
        work = jnp.where(eid == idx, -jnp.inf, work)
    exps = [jnp.exp(v - vals[0]) for v in vals]
    inv = 1.0 / (exps[0] + exps[1] + exps[2] + exps[3])
    gates = _stack_rows([e * inv for e in exps])

    picked = jnp.zeros((N_EXPERTS, rows), jnp.float32)
    for idx in idxs:
        picked = jnp.where(eid == idx, 1.0, picked)
    tr = lax.broadcasted_iota(jnp.int32, (rows, rows), 0)
    tc = lax.broadcasted_iota(jnp.int32, (rows, rows), 1)
    earlier = _bf16(jnp.where(tr < tc, 1.0, 0.0))
    in_tile = _dot(_bf16(picked), earlier)
    count_col = jnp.broadcast_to(jnp.sum(picked, axis=1, keepdims=True), (N_EXPERTS, LANES))
    eid_wide = lax.broadcasted_iota(jnp.int32, (LANES, rows), 0)
    picked_wide = jnp.zeros((LANES, rows), jnp.float32)
    for idx in idxs:
        picked_wide = jnp.where(eid_wide == idx, 1.0, picked_wide)
    count_row = _dot_nt(jnp.ones((8, rows), jnp.bfloat16), _bf16(picked_wide))
    return idxs, gates, in_tile, count_col, count_row


def _route_place(idxs, in_tile, count_col, count_row, carry_s, live):
    rows = in_tile.shape[1]
    eid = lax.broadcasted_iota(jnp.int32, (N_EXPERTS, rows), 0)
    er = lax.broadcasted_iota(jnp.int32, (N_EXPERTS, N_EXPERTS), 0)
    ec = lax.broadcasted_iota(jnp.int32, (N_EXPERTS, N_EXPERTS), 1)
    start_col = _dot(_bf16(jnp.where(ec < er, 1.0, 0.0)), _bf16(count_col))
    local = in_tile + jnp.concatenate([start_col] * (rows // LANES), axis=1)
    slots = _stack_rows([jnp.sum(jnp.where(eid == idx, local, 0.0), axis=0, keepdims=True)
                         for idx in idxs]).astype(jnp.int32)
    lr = lax.broadcasted_iota(jnp.int32, (LANES, LANES), 0)
    lc = lax.broadcasted_iota(jnp.int32, (LANES, LANES), 1)
    start_row = _dot(_bf16(count_row), _bf16(jnp.where(lr < lc, 1.0, 0.0)))
    ri = lax.broadcasted_iota(jnp.int32, (8, LANES), 0)
    meta = jnp.where(ri == 0, carry_s[...], jnp.where(ri == 1, count_row, jnp.where(ri == 2, start_row, 0.0)))
    carry_s[...] = carry_s[...] + count_row * live
    return slots, meta.astype(jnp.int32)


def _mix_kernel(tiles_per_seq, n_prompt_tiles,
                xp_ref, xs_ref, u_ref, vln_ref, q_ref, kd_ref, vd_ref, kdp_ref, vdp_ref,
                ck_ref, cv_ref, ga_ref, gb_ref, wsp_ref, bsp_ref, sink_ref,
                wpa_ref, wpb_ref, wo_ref, gffn_ref, wrh_ref, wrl_ref, br_ref,
                x1_ref, hloc_ref, slot_ref, gate_ref, meta_ref, count_ref,
                a_s, o_s, kwin_s, vwin_s, carry_s, hhi_s, hlo_s):
    i = pl.program_id(0)
    n_streams = ROW_TILE // CHUNK

    @pl.when(i == 0)
    def _():
        carry_s[...] = jnp.zeros_like(carry_s)
        hhi_s[...] = jnp.zeros_like(hhi_s)
        hlo_s[...] = jnp.zeros_like(hlo_s)

    def sgu_rows(r0, rows):
        ri = lax.broadcasted_iota(jnp.int32, (rows, rows), 0) // CHUNK
        ci = lax.broadcasted_iota(jnp.int32, (rows, rows), 1) // CHUNK
        for g in range(SGU_GROUPS):
            cols = slice(g * LANES, (g + 1) * LANES)
            w = _bf16(jnp.where(ci <= ri, wsp_ref[g, :rows, :rows], 0.0))
            sp = _dot(w, vln_ref[r0:r0 + rows, cols]) + bsp_ref[g, :rows, :]
            a_s[r0:r0 + rows, cols] = _bf16(_gelu(u_ref[r0:r0 + rows, cols].astype(jnp.float32)) * sp)

    def attend_rows(r0, kwin_of, valid):
        for g in range(N_KV_HEADS):
            c0 = g * Q_PER_KV * HEAD_DIM
            kwin, vwin = kwin_of(g)
            oa, ob = _attend(q_ref[r0:r0 + CHUNK, c0:c0 + LANES],
                             q_ref[r0:r0 + CHUNK, c0 + LANES:c0 + 2 * LANES],
                             kwin, vwin, sink_ref[g], valid)
            o_s[r0:r0 + CHUNK, c0:c0 + LANES] = _bf16(oa)
            o_s[r0:r0 + CHUNK, c0 + LANES:c0 + 2 * LANES] = _bf16(ob)

    @pl.when(i < n_prompt_tiles)
    def _prompt():
        for c in range(ROW_TILE // SGU_CHUNK):
            sgu_rows(c * SGU_CHUNK, SGU_CHUNK)
        kwin_s[0:WINDOW] = kdp_ref[...]
        kwin_s[WINDOW:WINDOW + ROW_TILE] = kd_ref[...]
        vwin_s[0:WINDOW] = vdp_ref[...]
        vwin_s[WINDOW:WINDOW + ROW_TILE] = vd_ref[...]
        first = (i % tiles_per_seq) == 0
        col = lax.broadcasted_iota(jnp.int32, (1, KEY_SPAN), 1)
        for j in range(ROW_TILE // CHUNK):
            r0 = j * CHUNK
            valid = jnp.logical_or(jnp.logical_not(first), col + r0 >= WINDOW) if r0 < WINDOW else None

            def kwin_of(g, r0=r0):
                cols = slice(g * LANES, (g + 1) * LANES)
                return kwin_s[r0:r0 + KEY_SPAN, cols], vwin_s[r0:r0 + KEY_SPAN, cols]

            attend_rows(r0, kwin_of, valid)

    @pl.when(i >= n_prompt_tiles)
    def _sample():
        for s in range(n_streams):
            r0 = s * CHUNK
            sgu_rows(r0, CHUNK)
            kwin_s[0:WINDOW] = _dup_heads(ck_ref[s])
            kwin_s[WINDOW:KEY_SPAN] = kd_ref[r0:r0 + CHUNK]
            vwin_s[0:WINDOW] = _dup_heads(cv_ref[s])
            vwin_s[WINDOW:KEY_SPAN] = vd_ref[r0:r0 + CHUNK]

            def kwin_of(g):
                cols = slice(g * LANES, (g + 1) * LANES)
                return kwin_s[0:KEY_SPAN, cols], vwin_s[0:KEY_SPAN, cols]

            attend_rows(r0, kwin_of, None)

    hh, hl = hhi_s[...], hlo_s[...]
    logits_t = (_dot_nt(wrh_ref[...], hh) + _dot_nt(wrl_ref[...], hh) + _dot_nt(wrh_ref[...], hl)
                + jnp.concatenate([br_ref[...]] * (ROW_TILE // LANES), axis=1))
    m_a = _sigmoid(ga_ref[...].astype(jnp.float32)) * _dot(a_s[...], _bf16(wpa_ref[...]))
    idxs, gates, in_tile, count_col, count_row = _route_pick(logits_t)
    gate_ref[...] = gates
    m = m_a + _sigmoid(gb_ref[...].astype(jnp.float32)) * _dot(o_s[...], _bf16(wpb_ref[...]))
    slots, meta = _route_place(idxs, in_tile, count_col, count_row, carry_s, jnp.where(i > 0, 1.0, 0.0))
    half = ((i + 1) % 2) * (ROW_TILE * TOP_K * ROW_SUBTILES)
    slot_ref[...] = slots * ROW_SUBTILES + half
    meta_ref[...] = meta
    count_ref[...] = carry_s[...].astype(jnp.int32)
    n_slots = ROW_TILE * TOP_K
    sid = lax.broadcasted_iota(jnp.int32, (n_slots, ROW_TILE), 0)
    place = jnp.zeros((n_slots, ROW_TILE), jnp.float32)
    for k in range(TOP_K):
        place = jnp.where(sid == slots[k:k + 1, :], 1.0, place)
    _store_row_tiled(hloc_ref, (), _dot(_bf16(place), hh))

    x = jnp.where(i < n_prompt_tiles, xp_ref[...], xs_ref[...])
    x1 = x + _dot(_bf16(m), _bf16(wo_ref[...]))
    x1_ref[...] = x1
    h2 = _rms(x1, gffn_ref[...])
    h2_hi = _bf16(h2)
    hhi_s[...] = h2_hi
    hlo_s[...] = _bf16(h2 - h2_hi.astype(jnp.float32))


def _mix(xp, xs, u, vln, q, kd, vd, cache_k, cache_v, ga, gb, w_sp, b_sp, sinks,
         w_pa, w_pb, w_o, g_ffn, w_router, b_router, seq):
    tp, ts = xp.shape[0], xs.shape[0]
    t = tp + ts
    npt = tp // ROW_TILE
    tiles_per_seq = seq // ROW_TILE
    f32, bf16 = jnp.float32, jnp.bfloat16
    n_streams = ROW_TILE // CHUNK
    win_per_tile = ROW_TILE // WINDOW

    nt = t // ROW_TILE
    cur = lambda i: jnp.minimum(i, nt - 1)
    smp = lambda i: jnp.maximum(cur(i) - npt, 0)
    row = lambda width: pl.BlockSpec((ROW_TILE, width), lambda i: (cur(i), 0))
    prev_spec = pl.BlockSpec(
        (WINDOW, KV_DUP_W), lambda i: (jnp.maximum(jnp.minimum(i, npt - 1) * win_per_tile - 1, 0), 0))
    cache_spec = pl.BlockSpec((n_streams, WINDOW, KV_W), lambda i: (smp(i), 0, 0))
    xs_spec = pl.BlockSpec((ROW_TILE, D_MODEL), lambda i: (smp(i), 0))
    sink_cols = jnp.broadcast_to(
        jnp.repeat(sinks.astype(f32).reshape(N_KV_HEADS, Q_PER_KV), CHUNK, axis=1)[:, :, None],
        (N_KV_HEADS, Q_PER_KV * CHUNK, LANES))
    wr_t = w_router.T
    wr_hi = wr_t.astype(bf16)
    wr_lo = (wr_t - wr_hi.astype(f32)).astype(bf16)
    routed8 = lambda width: pl.BlockSpec((8, width), lambda i: (jnp.maximum(i - 1, 0), 0))
    out_shape = (
        jax.ShapeDtypeStruct((t, D_MODEL), f32),
        jax.ShapeDtypeStruct((t * TOP_K * ROW_SUBTILES, LANES), f32),
        jax.ShapeDtypeStruct((nt * 8, ROW_TILE), jnp.int32),
        jax.ShapeDtypeStruct((nt * 8, ROW_TILE), f32),
        jax.ShapeDtypeStruct((nt * 8, LANES), jnp.int32),
        jax.ShapeDtypeStruct((8, LANES), jnp.int32),
    )
    return pl.pallas_call(
        functools.partial(_mix_kernel, tiles_per_seq, npt),
        out_shape=out_shape,
        grid=(nt + 1,),
        in_specs=[
            _prompt_spec(D_MODEL, npt), xs_spec,
            row(SGU_WIDTH), row(SGU_WIDTH), row(ATT_W),
            row(KV_DUP_W), row(KV_DUP_W), prev_spec, prev_spec,
            cache_spec, cache_spec, row(D_MODEL), row(D_MODEL),
            _const_spec((SGU_GROUPS, SGU_CHUNK, SGU_CHUNK)), _const_spec((SGU_GROUPS, SGU_CHUNK, LANES)),
            _const_spec((N_KV_HEADS, Q_PER_KV * CHUNK, LANES)),
            _const_spec((SGU_WIDTH, D_MODEL)), _const_spec((ATT_W, D_MODEL)),
            _const_spec((D_MODEL, D_MODEL)), _const_spec((1, D_MODEL)),
            _const_spec((N_EXPERTS, D_MODEL)), _const_spec((N_EXPERTS, D_MODEL)),
            _const_spec((N_EXPERTS, LANES)),
        ],
        out_specs=(row(D_MODEL),
                   pl.BlockSpec((ROW_TILE * TOP_K * ROW_SUBTILES, LANES),
                                lambda i: (jnp.maximum(i - 1, 0), 0)),
                   routed8(ROW_TILE), routed8(ROW_TILE), routed8(LANES), _const_spec((8, LANES))),
        scratch_shapes=[
            pltpu.VMEM((ROW_TILE, SGU_WIDTH), bf16), pltpu.VMEM((ROW_TILE, ATT_W), bf16),
            pltpu.VMEM((WINDOW + ROW_TILE, KV_DUP_W), bf16),
            pltpu.VMEM((WINDOW + ROW_TILE, KV_DUP_W), bf16),
            pltpu.VMEM((8, LANES), f32),
            pltpu.VMEM((ROW_TILE, D_MODEL), bf16), pltpu.VMEM((ROW_TILE, D_MODEL), bf16),
        ],
        compiler_params=_params(),
        name="mix",
    )(xp, xs, u, vln, q, kd, vd, kd, vd,
      cache_k.reshape(-1, WINDOW, KV_W), cache_v.reshape(-1, WINDOW, KV_W), ga, gb,
      w_sp, jnp.broadcast_to(b_sp[:, :, None], (SGU_GROUPS, SGU_CHUNK, LANES)), sink_cols,
      w_pa, w_pb, w_o,
      g_ffn.reshape(1, -1), wr_hi, wr_lo,
      jnp.broadcast_to(b_router.astype(f32)[:, None], (N_EXPERTS, LANES)))


def _unrolled_rows(n_rows, fn):
    if isinstance(n_rows, int):
        groups, tail_start = n_rows // ROW_UNROLL, n_rows - n_rows % ROW_UNROLL
    else:
        groups = lax.shift_right_logical(n_rows, ROW_UNROLL.bit_length() - 1)
        tail_start = groups * ROW_UNROLL

    def group(gi, carry):
        for lane in range(ROW_UNROLL):
            fn(gi * ROW_UNROLL + lane, lane)
        return carry

    def tail(r, carry):
        fn(r, 0)
        return carry

    lax.fori_loop(0, groups, group, 0)
    lax.fori_loop(tail_start, n_rows, tail, 0)


def _row_span(first_row, n_rows):
    return pl.ds(pl.multiple_of(first_row * ROW_SUBTILES, ROW_SUBTILES),
                 pl.multiple_of(n_rows * ROW_SUBTILES, ROW_SUBTILES))


def _run_spec(index_of):
    return pl.BlockSpec((1, 1, LANES), lambda i, *_: (index_of(i), 0, 0), memory_space=pltpu.SMEM)


def _expert_kernel(n_token_tiles,
                   te_ref, nu_ref, nx_ref, par_ref, nv_ref, tf_ref, cnt_ref, loc_ref,
                   bgu_ref, bdn_ref, hloc_hbm, wgu_hbm, wdn_hbm,
                   ys_ref,
                   xbuf, wgu_f, wdn_f, walk, xsem, wsem):
    i = pl.program_id(0)
    n_used = nu_ref[0]
    expert = te_ref[i]
    buf = par_ref[i]
    slot = i % 2
    expert_changed = jnp.logical_or(i == 0, expert != te_ref[jnp.maximum(i - 1, 0)])

    def fetch_rows(j, b):
        e = te_ref[j]
        need = nv_ref[j]

        @pl.when(tf_ref[j] == 1)
        def _():
            walk[0] = 0
            walk[1] = 0

        @pl.when(need < MOE_TILE)
        def _():
            xbuf[b] = jnp.zeros(xbuf.shape[1:], xbuf.dtype)

        def unfinished(state):
            filled, tile, _ = state
            return jnp.logical_and(filled < need, tile < n_token_tiles)

        def take_run(state):
            filled, tile, off = state
            run = cnt_ref[tile * N_EXPERTS + e]
            take = jnp.minimum(run - off, need - filled)

            @pl.when(take > 0)
            def _():
                src = tile * (ROW_TILE * TOP_K) + loc_ref[tile * N_EXPERTS + e] + off
                pltpu.make_async_copy(hloc_hbm.at[_row_span(src, take)],
                                      xbuf.at[b, _row_span(filled, take)], xsem.at[b]).start()

            run_done = off + take == run
            return (filled + take, jnp.where(run_done, tile + 1, tile), jnp.where(run_done, 0, off + take))

        _, tile, off = lax.while_loop(unfinished, take_run, (jnp.int32(0), walk[0], walk[1]))
        walk[0] = tile
        walk[1] = off

    @pl.when(jnp.logical_and(i == 0, n_used > 0))
    def _():
        fetch_rows(0, 0)

    @pl.when(i + 1 < n_used)
    def _():
        fetch_rows(i + 1, 1 - slot)

    def weight_copies(e, b):
        return (pltpu.make_async_copy(wgu_hbm.at[e], wgu_f.at[b], wsem.at[0, b]),
                pltpu.make_async_copy(wdn_hbm.at[e], wdn_f.at[b], wsem.at[1, b]))

    @pl.when(jnp.logical_and(i < n_used, expert_changed))
    def _():
        @pl.when(i == 0)
        def _():
            for copy in weight_copies(expert, buf):
                copy.start()

        for copy in weight_copies(expert, buf):
            copy.wait()
        following = nx_ref[i]

        @pl.when(following != expert)
        def _():
            for copy in weight_copies(following, 1 - buf):
                copy.start()


    rows = nv_ref[i]

    @pl.when(i < n_used)
    def _():
        pltpu.make_async_copy(hloc_hbm.at[_row_span(0, rows)], xbuf.at[slot, _row_span(0, rows)],
                              xsem.at[slot]).wait()

    def mlp(n):
        x = _bf16(_load_row_tiled(xbuf, (slot,), n))
        gu = _dot(x, _bf16(wgu_f[buf])) + bgu_ref[0]
        gate = jnp.minimum(gu[:, :D_FF], SWIGLU_LIMIT)
        lin = jnp.clip(gu[:, D_FF:], -SWIGLU_LIMIT, SWIGLU_LIMIT)
        act = gate * _sigmoid(SWIGLU_ALPHA * gate) * (lin + 1.0)
        _store_row_tiled(ys_ref, (), _dot(_bf16(act), _bf16(wdn_f[buf])) + bdn_ref[0])

    @pl.when(jnp.logical_and(i < n_used, rows > MOE_TILE // 2))
    def _():
        mlp(MOE_TILE)

    @pl.when(jnp.logical_and(i < n_used, rows <= MOE_TILE // 2))
    def _():
        mlp(MOE_TILE // 2)
        ys_ref[pl.ds(MOE_TILE // 2 * ROW_SUBTILES, MOE_TILE // 2 * ROW_SUBTILES), :] = jnp.zeros(
            (MOE_TILE // 2 * ROW_SUBTILES, LANES), ys_ref.dtype)

    @pl.when(i >= n_used)
    def _():
        ys_ref[...] = jnp.zeros_like(ys_ref)


def _experts(tile_expert, n_used, next_expert, weight_buf, n_valid, tile_first, run_n, run_loc,
             h_local, w_gu, b_gu, w_dn, b_dn):
    n_tiles = tile_expert.shape[0]
    n_token_tiles = h_local.shape[0] // (ROW_TILE * TOP_K * ROW_SUBTILES)
    f32, bf16 = jnp.float32, jnp.bfloat16
    tile_rows = MOE_TILE * ROW_SUBTILES
    grid_spec = pltpu.PrefetchScalarGridSpec(
        num_scalar_prefetch=8,
        grid=(n_tiles,),
        in_specs=[
            pl.BlockSpec((1, 1, 2 * D_FF), lambda i, te, *_: (te[i], 0, 0)),
            pl.BlockSpec((1, 1, D_MODEL), lambda i, te, *_: (te[i], 0, 0)),
            pl.BlockSpec(memory_space=pl.ANY), pl.BlockSpec(memory_space=pl.ANY),
            pl.BlockSpec(memory_space=pl.ANY),
        ],
        out_specs=pl.BlockSpec((tile_rows, LANES), lambda i, *_: (i, 0)),
        scratch_shapes=[
            pltpu.VMEM((2, tile_rows, LANES), f32),
            pltpu.VMEM((2, D_MODEL, 2 * D_FF), f32), pltpu.VMEM((2, D_FF, D_MODEL), f32),
            pltpu.SMEM((2,), jnp.int32),
            pltpu.SemaphoreType.DMA((2,)), pltpu.SemaphoreType.DMA((2, 2)),
        ],
    )
    return pl.pallas_call(
        functools.partial(_expert_kernel, n_token_tiles),
        out_shape=jax.ShapeDtypeStruct((n_tiles * tile_rows, LANES), f32),
        grid_spec=grid_spec,
        compiler_params=_params(),
        name="experts",
    )(tile_expert, n_used, next_expert, weight_buf, n_valid, tile_first, run_n, run_loc,
      b_gu.reshape(N_EXPERTS, 1, -1), b_dn.reshape(N_EXPERTS, 1, -1), h_local, w_gu, w_dn)


def _combine_kernel(n_prompt_tiles,
                    src_ref, n_ref, dst_ref, src_nx_ref, n_nx_ref, dst_nx_ref, slot_ref, gate_ref,
                    x1_ref, gfin_ref, ys_hbm,
                    yp_ref, yo_ref,
                    local, mixed, run_sem):
    i = pl.program_id(0)
    last = pl.num_programs(0) - 1
    buf = i % 2
    half_slots = ROW_TILE * TOP_K
    half_rows = half_slots * ROW_SUBTILES

    def fetch_runs(s_ref, c_ref, d_ref, b):
        for e in range(N_EXPERTS):
            n = c_ref[0, 0, e]
            copy = pltpu.make_async_copy(ys_hbm.at[_row_span(d_ref[0, 0, e], n)],
                                         local.at[_row_span(b * half_slots + s_ref[0, 0, e], n)],
                                         run_sem.at[b])
            pl.when(n > 0)(copy.start)

    @pl.when(i == 0)
    def _():
        fetch_runs(src_ref, n_ref, dst_ref, 0)

    @pl.when(i < last)
    def _():
        fetch_runs(src_nx_ref, n_nx_ref, dst_nx_ref, 1 - buf)

    pltpu.make_async_copy(ys_hbm.at[pl.ds(0, half_rows)],
                          local.at[pl.ds(pl.multiple_of(buf * half_rows, half_rows), half_rows)],
                          run_sem.at[buf]).wait()

    def blend(t, lane):
        acc = None
        for k in range(TOP_K):
            at = pl.multiple_of(slot_ref[0, 0, k * ROW_TILE + t], ROW_SUBTILES)
            term = gate_ref[0, 0, k * ROW_TILE + t] * local[pl.ds(at, ROW_SUBTILES), :]
            acc = term if acc is None else acc + term
        mixed[pl.ds(pl.multiple_of(t * ROW_SUBTILES, ROW_SUBTILES), ROW_SUBTILES), :] = acc
    _unrolled_rows(ROW_TILE, blend)

    out = _rms(x1_ref[...] + _load_row_tiled(mixed, (), ROW_TILE), gfin_ref[...])

    @pl.when(i < n_prompt_tiles)
    def _():
        yp_ref[...] = out

    @pl.when(i >= n_prompt_tiles)
    def _():
        yo_ref[...] = out


def _combine(run_src, run_n, run_dst, slots, gates, x1, ys, g_final, tp):
    t = x1.shape[0]
    npt = tp // ROW_TILE
    nt = t // ROW_TILE
    f32 = jnp.float32
    picks = ROW_TILE * TOP_K
    nxt = lambda i: jnp.minimum(i + 1, nt - 1)
    pick_spec = pl.BlockSpec((1, 1, picks), lambda i: (i, 0, 0), memory_space=pltpu.SMEM)
    return pl.pallas_call(
        functools.partial(_combine_kernel, npt),
        out_shape=(jax.ShapeDtypeStruct((tp, D_MODEL), f32),
                   jax.ShapeDtypeStruct((t - tp, D_MODEL), f32)),
        grid=(nt,),
        in_specs=[_run_spec(lambda i: i), _run_spec(lambda i: i), _run_spec(lambda i: i),
                  _run_spec(nxt), _run_spec(nxt), _run_spec(nxt), pick_spec, pick_spec,
                  _row_spec(D_MODEL), _const_spec((1, D_MODEL)), pl.BlockSpec(memory_space=pl.ANY)],
        out_specs=(_prompt_spec(D_MODEL, npt), _sample_spec(D_MODEL, npt)),
        scratch_shapes=[pltpu.VMEM((2 * picks * ROW_SUBTILES, LANES), f32),
                        pltpu.VMEM((ROW_TILE * ROW_SUBTILES, LANES), f32),
                        pltpu.SemaphoreType.DMA((2,))],
        compiler_params=_params(),
        name="combine",
    )(run_src, run_n, run_dst, run_src, run_n, run_dst, slots, gates, x1, g_final.reshape(1, -1), ys)


def _plan(meta, counts, t):
    nt = t // ROW_TILE
    n_tiles = (t * TOP_K + N_EXPERTS * (MOE_TILE - 1)) // MOE_TILE
    counts = counts[0, :N_EXPERTS]
    tiles_e = (counts + MOE_TILE - 1) // MOE_TILE
    tile_end = jnp.cumsum(tiles_e)
    tile_start = tile_end - tiles_e
    n_used = tile_end[-1]
    first_row = jnp.pad(tile_start * MOE_TILE, (0, LANES - N_EXPERTS))
    meta = meta.reshape(nt, 8, LANES)
    run_dst = meta[:, 0:1, :] + first_row[None, None, :]
    run_n = meta[:, 1:2, :]
    run_src = meta[:, 2:3, :]
    tile_ids = jnp.arange(n_tiles, dtype=jnp.int32)
    live = jnp.minimum(tile_ids, n_used - 1)
    tile_expert = jnp.sum(tile_end[None, :] <= live[:, None], axis=1).astype(jnp.int32)
    ids = jnp.arange(N_EXPERTS, dtype=jnp.int32)
    is_expert = tile_expert[:, None] == ids[None, :]
    of_tile = lambda per_expert: jnp.sum(jnp.where(is_expert, per_expert[None, :], 0), axis=1)
    in_expert = tile_ids - of_tile(tile_start)
    n_valid = jnp.clip(of_tile(counts) - in_expert * MOE_TILE, 0, MOE_TILE)
    n_valid = jnp.where(tile_ids < n_used, n_valid, 0).astype(jnp.int32)
    tile_first = jnp.logical_and(in_expert == 0, tile_ids < n_used).astype(jnp.int32)
    run_n_flat = run_n[:, 0, :N_EXPERTS].reshape(-1)
    run_loc_flat = run_src[:, 0, :N_EXPERTS].reshape(-1)
    used = tiles_e > 0
    later_used = jnp.where(jnp.logical_and(used[None, :], ids[None, :] > ids[:, None]), ids[None, :],
                           N_EXPERTS)
    following = jnp.min(later_used, axis=1)
    following = jnp.where(following < N_EXPERTS, following, ids)
    buf_of = (jnp.cumsum(used.astype(jnp.int32)) - 1) % 2
    return (tile_expert, n_used.reshape(1).astype(jnp.int32), of_tile(following).astype(jnp.int32),
            of_tile(buf_of).astype(jnp.int32), n_valid, tile_first, run_n_flat, run_loc_flat,
            run_src, run_n, run_dst)


def kernel(x_prompt, x_sample, cache_k, cache_v, g_mix, w_in, b_in, ln_v_g, ln_v_b, w_sp, b_sp,
           attn_sinks, w_pa, w_pb, w_o, g_ffn, w_router, b_router, w_gu, b_gu, w_dn, b_dn, g_final):
    nb, seq, d = x_prompt.shape
    nsb, nnew, _ = x_sample.shape
    tp, ts = nb * seq, nsb * nnew
    t = tp + ts
    xp = x_prompt.reshape(tp, d)
    xs = x_sample.reshape(ts, d)
    pos = np.concatenate([np.arange(seq), np.tile(PAST_LEN + np.arange(nnew), ROW_TILE // nnew)])
    tables = _rope_tables(pos)
    u, vln, v_sgu, q, k, v, kd, vd, ga, gb = _project(
        xp, xs, g_mix[0], w_in[0], b_in[0], ln_v_g[0], ln_v_b[0], tables, seq)
    x1, h_local, slot_t, gate_t, meta, counts = _mix(
        xp, xs, u, vln, q, kd, vd, cache_k[0], cache_v[0], ga, gb, w_sp[0], b_sp[0], attn_sinks[0],
        w_pa[0], w_pb[0], w_o[0], g_ffn[0], w_router[0], b_router[0], seq)
    nt = t // ROW_TILE
    picks = lambda a: a.reshape(nt, 8, ROW_TILE)[:, :TOP_K, :].reshape(nt, 1, TOP_K * ROW_TILE)
    slots, gates = picks(slot_t), picks(gate_t)
    (tile_expert, n_used, next_expert, weight_buf, n_valid, tile_first, run_n_flat, run_loc_flat,
     run_src, run_n, run_dst) = _plan(meta, counts, t)
    y_sorted = _experts(tile_expert, n_used, next_expert, weight_buf, n_valid, tile_first, run_n_flat,
                        run_loc_flat, h_local, w_gu[0], b_gu[0], w_dn[0], b_dn[0])
    y_p, y_s = _combine(run_src, run_n, run_dst, slots, gates, x1, y_sorted, g_final, tp)

    keep = min(WINDOW, seq)
    tails = lambda a: jnp.stack([a[(b + 1) * seq - keep:(b + 1) * seq] for b in range(nb)]).reshape(
        nb, keep, N_KV_HEADS, HEAD_DIM)
    kp, vp = tails(k), tails(v)
    ks = k[tp:].reshape(nsb, nnew, N_KV_HEADS, HEAD_DIM)
    vs = v[tp:].reshape(nsb, nnew, N_KV_HEADS, HEAD_DIM)
    return (y_p.reshape(nb, seq, d), y_s.reshape(nsb, nnew, d), kp[None], vp[None], ks[None],
            vs[None], v_sgu.reshape(1, nsb, nnew, SGU_WIDTH))
```

```python
import functools

import numpy as np
import jax
import jax.numpy as jnp
from jax import lax
from jax.experimental import pallas as pl
from jax.experimental.pallas import tpu as pltpu

D_MODEL = 1024
PAST_LEN = 2048
CHUNK = 64
SGU_CHUNK = 128
SGU_GROUPS = 8
SGU_WIDTH = 1024
N_HEADS = 16
N_KV_HEADS = 4
HEAD_DIM = 64
Q_PER_KV = N_HEADS // N_KV_HEADS
WINDOW = 128
ROT_DIM = HEAD_DIM // 4
ROPE_THETA = 500000.0
ATT_W = N_HEADS * HEAD_DIM
KV_W = N_KV_HEADS * HEAD_DIM
N_EXPERTS = 32
TOP_K = 4
D_FF = 1024
SWIGLU_ALPHA = 1.702
SWIGLU_LIMIT = 7.0
NORM_EPS = 1e-5
NEG_INF = -1e30
N_IN = SGU_WIDTH * 2 + ATT_W + KV_W * 2 + D_MODEL * 2

LANES = 128
ROW_TILE = 256
MOE_TILE = 512
ROW_UNROLL = 16
COL_BLOCK = 256
KV_DUP_W = N_KV_HEADS * LANES
KEY_SPAN = WINDOW + CHUNK
VMEM_LIMIT = 56 * 1024 * 1024

_SQRT_HALF = 0.7071067811865476
_LOG2_E = 1.4426950408889634


def _gelu(x):
    t = 1.0 / (1.0 + (0.3275911 * _SQRT_HALF) * jnp.abs(x))
    half_poly = t * (0.127414796 + t * (-0.142248368 + t * (0.7107068705
                     + t * (-0.7265760135 + t * 0.5307027145))))
    half_tail = x * (half_poly * jnp.exp2(x * x * (-0.5 * _LOG2_E)))
    return jnp.where(x >= 0.0, x - half_tail, half_tail)


def _sigmoid(x):
    return 1.0 / (1.0 + jnp.exp(-x))


def _bf16(x):
    return x.astype(jnp.bfloat16)


def _dot(a, b):
    return jnp.dot(a, b, preferred_element_type=jnp.float32)


ROW_SUBTILES = D_MODEL // LANES


def _store_row_tiled(ref, lead, x):
    rows = x.shape[0]
    for s in range(ROW_SUBTILES):
        ref[(*lead, pl.ds(s, rows, stride=ROW_SUBTILES), slice(None))] = x[:, s * LANES:(s + 1) * LANES]


def _load_row_tiled(ref, lead, rows):
    return jnp.concatenate(
        [ref[(*lead, pl.ds(s, rows, stride=ROW_SUBTILES), slice(None))] for s in range(ROW_SUBTILES)],
        axis=1)


def _dot_nt(a, b):
    return lax.dot_general(a, b, (((1,), (1,)), ((), ())), preferred_element_type=jnp.float32)


def _rms(x, g):
    return x * lax.rsqrt(jnp.mean(x * x, axis=-1, keepdims=True) + NORM_EPS) * g


def _lane_lo(rows):
    return lax.broadcasted_iota(jnp.int32, (rows, LANES), 1) < HEAD_DIM


def _dup_heads(kv):
    rows = kv.shape[0]
    lo = _lane_lo(rows)
    out = []
    for j in range(KV_W // LANES):
        blk = kv[:, j * LANES:(j + 1) * LANES]
        swp = pltpu.roll(blk, HEAD_DIM, axis=1)
        out.append(jnp.where(lo, blk, swp))
        out.append(jnp.where(lo, swp, blk))
    return _bf16(jnp.concatenate(out, axis=1))


def _rope_block(zb, cos_b, sin_lo, sin_hi):
    up = pltpu.roll(zb, LANES - ROT_DIM // 2, axis=1)
    dn = pltpu.roll(zb, ROT_DIM // 2, axis=1)
    return zb * cos_b + up * sin_lo + dn * sin_hi


def _proj_kernel(n_prompt_tiles, xp_ref, xs_ref, gmix_ref, w_ref, b_ref, lng_ref, lnb_ref,
                 cos_ref, slo_ref, shi_ref,
                 u_ref, vln_ref, vs_ref, q_ref, k_ref, v_ref, kd_ref, vd_ref, ga_ref, gb_ref):
    i = pl.program_id(0)
    x = jnp.where(i < n_prompt_tiles, xp_ref[...], xs_ref[...])
    h = _bf16(_rms(x, gmix_ref[...]))

    cos_b, sin_lo, sin_hi = cos_ref[...], slo_ref[...], shi_ref[...]
    off_u, off_v, off_q, off_k, off_vv, off_ga, off_gb = (
        int(o) for o in np.cumsum((0, SGU_WIDTH, SGU_WIDTH, ATT_W, KV_W, KV_W, D_MODEL)))

    def z(lo):
        return _dot(h, w_ref[:, lo:lo + COL_BLOCK]) + b_ref[:, lo:lo + COL_BLOCK]

    def cols(j):
        return slice(j * COL_BLOCK, (j + 1) * COL_BLOCK)

    def rope(zb):
        return jnp.concatenate(
            [_rope_block(zb[:, t * LANES:(t + 1) * LANES], cos_b, sin_lo, sin_hi)
             for t in range(COL_BLOCK // LANES)], axis=1)

    gelu_v = []

    def do_u(j):
        u_ref[:, cols(j)] = _bf16(z(off_u + j * COL_BLOCK))

    def do_v(j):
        gelu_v.append(_gelu(z(off_v + j * COL_BLOCK)))

    def do_q(j):
        q_ref[:, cols(j)] = _bf16(rope(z(off_q + j * COL_BLOCK)) * (HEAD_DIM ** -0.5))

    def do_k(j):
        kr = rope(z(off_k))
        k_ref[...] = kr
        kd_ref[...] = _dup_heads(kr)

    def do_vv(j):
        zv = z(off_vv)
        v_ref[...] = zv
        vd_ref[...] = _dup_heads(zv)

    def do_ga(j):
        ga_ref[:, cols(j)] = _bf16(z(off_ga + j * COL_BLOCK))

    def do_gb(j):
        gb_ref[:, cols(j)] = _bf16(z(off_gb + j * COL_BLOCK))

    order = ((do_v, 0), (do_q, 0), (do_q, 1), (do_u, 0), (do_v, 1), (do_q, 2), (do_q, 3), (do_u, 1),
             (do_v, 2), (do_k, 0), (do_vv, 0), (do_u, 2), (do_v, 3), (do_ga, 0), (do_ga, 1), (do_u, 3),
             (do_ga, 2), (do_ga, 3), (do_gb, 0), (do_gb, 1), (do_gb, 2), (do_gb, 3))
    for fn, j in order:
        fn(j)
    gv = jnp.concatenate(gelu_v, axis=1)
    gc = gv - jnp.mean(gv, axis=-1, keepdims=True)
    var = jnp.mean(gc * gc, axis=-1, keepdims=True)
    vln = gc * lax.rsqrt(var + NORM_EPS) * lng_ref[...] + lnb_ref[...]
    vln_ref[...] = _bf16(vln)
    vs_ref[...] = vln


def _rope_tables(pos):
    half = ROT_DIM // 2
    inv = np.float32(ROPE_THETA) ** (-np.arange(half, dtype=np.float32) * np.float32(2.0) / ROT_DIM)
    ang = pos.astype(np.float32)[:, None] * inv.astype(np.float32)[None, :]
    cos = np.cos(ang.astype(np.float64)).astype(np.float32)
    sin = np.sin(ang.astype(np.float64)).astype(np.float32)
    n = pos.shape[0]
    ones = np.ones((n, HEAD_DIM - ROT_DIM), np.float32)
    zeros = np.zeros((n, HEAD_DIM - ROT_DIM), np.float32)
    zh = np.zeros((n, half), np.float32)
    cos_h = np.concatenate([cos, cos, ones], axis=1)
    slo_h = np.concatenate([-sin, zh, zeros], axis=1)
    shi_h = np.concatenate([zh, sin, zeros], axis=1)
    rep = LANES // HEAD_DIM
    return tuple(jnp.asarray(np.tile(a, (1, rep))) for a in (cos_h, slo_h, shi_h))


def _row_spec(width):
    return pl.BlockSpec((ROW_TILE, width), lambda i: (i, 0))


def _const_spec(shape):
    return pl.BlockSpec(shape, lambda i: (0,) * len(shape))


def _prompt_spec(width, n_prompt_tiles):
    return pl.BlockSpec((ROW_TILE, width), lambda i: (jnp.minimum(i, n_prompt_tiles - 1), 0))


def _sample_spec(width, n_prompt_tiles):
    return pl.BlockSpec((ROW_TILE, width), lambda i: (jnp.maximum(i - n_prompt_tiles, 0), 0))


def _params():
    return pltpu.CompilerParams(dimension_semantics=("arbitrary",), vmem_limit_bytes=VMEM_LIMIT)


def _project(xp, xs, g_mix, w_in, b_in, ln_g, ln_b, tables, seq):
    tp, ts = xp.shape[0], xs.shape[0]
    t = tp + ts
    npt = tp // ROW_TILE
    tiles_per_seq = seq // ROW_TILE
    f32, bf16 = jnp.float32, jnp.bfloat16
    table_spec = pl.BlockSpec(
        (ROW_TILE, LANES), lambda i: (jnp.where(i < npt, i % tiles_per_seq, tiles_per_seq), 0))
    out_shape = (
        jax.ShapeDtypeStruct((t, SGU_WIDTH), bf16),
        jax.ShapeDtypeStruct((t, SGU_WIDTH), bf16),
        jax.ShapeDtypeStruct((ts, SGU_WIDTH), f32),
        jax.ShapeDtypeStruct((t, ATT_W), bf16),
        jax.ShapeDtypeStruct((t, KV_W), f32),
        jax.ShapeDtypeStruct((t, KV_W), f32),
        jax.ShapeDtypeStruct((t, KV_DUP_W), bf16),
        jax.ShapeDtypeStruct((t, KV_DUP_W), bf16),
        jax.ShapeDtypeStruct((t, D_MODEL), bf16),
        jax.ShapeDtypeStruct((t, D_MODEL), bf16),
    )
    return pl.pallas_call(
        functools.partial(_proj_kernel, npt),
        out_shape=out_shape,
        grid=(t // ROW_TILE,),
        in_specs=[
            _prompt_spec(D_MODEL, npt), _sample_spec(D_MODEL, npt),
            _const_spec((1, D_MODEL)), _const_spec((D_MODEL, N_IN)),
            _const_spec((1, N_IN)), _const_spec((1, SGU_WIDTH)), _const_spec((1, SGU_WIDTH)),
            table_spec, table_spec, table_spec,
        ],
        out_specs=(
            _row_spec(SGU_WIDTH), _row_spec(SGU_WIDTH), _sample_spec(SGU_WIDTH, npt),
            _row_spec(ATT_W), _row_spec(KV_W), _row_spec(KV_W), _row_spec(KV_DUP_W),
            _row_spec(KV_DUP_W), _row_spec(D_MODEL), _row_spec(D_MODEL),
        ),
        compiler_params=_params(),
        name="proj",
    )(xp, xs, g_mix.reshape(1, -1), w_in.astype(bf16), b_in.reshape(1, -1),
      ln_g.reshape(1, -1), ln_b.reshape(1, -1), *tables)


def _attend(qa, qb, kwin, vwin, sink, valid):
    lo = _lane_lo(CHUNK)
    zero = jnp.zeros_like(qa)
    lhs = jnp.concatenate([jnp.where(lo, qa, zero), jnp.where(lo, zero, qa),
                           jnp.where(lo, qb, zero), jnp.where(lo, zero, qb)], axis=0)
    s = _dot_nt(lhs, kwin)
    if valid is not None:
        s = jnp.where(valid, s, NEG_INF)
    s_a, s_b = s[:, :LANES], s[:, LANES:]
    tail = s_b.shape[1]
    m = jnp.maximum(jnp.max(s, axis=-1, keepdims=True), sink)
    p_a = jnp.exp(s_a - m)
    p_b = jnp.exp(s_b - m[:, :tail])
    denom = (jnp.sum(jnp.concatenate([p_a, p_b], axis=1), axis=-1, keepdims=True)
             + jnp.exp(sink - m))
    inv = 1.0 / denom
    pn = jnp.concatenate([p_a * inv, p_b * inv[:, :tail]], axis=1)
    r = _dot(_bf16(pn), vwin)
    oa = jnp.where(lo, r[0:CHUNK], r[CHUNK:2 * CHUNK])
    ob = jnp.where(lo, r[2 * CHUNK:3 * CHUNK], r[3 * CHUNK:4 * CHUNK])
    return oa, ob


def _stack_rows(rows):
    ri = lax.broadcasted_iota(jnp.int32, (8, rows[0].shape[1]), 0)
    out = jnp.zeros((8, rows[0].shape[1]), rows[0].dtype)
    for k, row in enumerate(rows):
        out = jnp.where(ri == k, row, out)
    return out


def _route_pick(logits_t):
    rows = logits_t.shape[1]
    eid = lax.broadcasted_iota(jnp.int32, (N_EXPERTS, rows), 0)
    work = logits_t
    vals, idxs = [], []
    for _ in range(TOP_K):
        m = jnp.max(work, axis=0, keepdims=True)
        idx = jnp.min(jnp.where(work == m, eid, N_EXPERTS), axis=0, keepdims=True)
        vals.append(m)
        idxs.append(idx)
        work = jnp.where(eid == idx, -jnp.inf, work)
    exps = [jnp.exp(v - vals[0]) for v in vals]
    inv = 1.0 / (exps[0] + exps[1] + exps[2] + exps[3])
    gates = _stack_rows([e * inv for e in exps])

    picked = jnp.zeros((N_EXPERTS, rows), jnp.float32)
    for idx in idxs:
        picked = jnp.where(eid == idx, 1.0, picked)
    tr = lax.broadcasted_iota(jnp.int32, (rows, rows), 0)
    tc = lax.broadcasted_iota(jnp.int32, (rows, rows), 1)
    earlier = _bf16(jnp.where(tr < tc, 1.0, 0.0))
    in_tile = _dot(_bf16(picked), earlier)
    count_col = jnp.broadcast_to(jnp.sum(picked, axis=1, keepdims=True), (N_EXPERTS, LANES))
    eid_wide = lax.broadcasted_iota(jnp.int32, (LANES, rows), 0)
    picked_wide = jnp.zeros((LANES, rows), jnp.float32)
    for idx in idxs:
        picked_wide = jnp.where(eid_wide == idx, 1.0, picked_wide)
    count_row = _dot_nt(jnp.ones((8, rows), jnp.bfloat16), _bf16(picked_wide))
    return idxs, gates, in_tile, count_col, count_row


def _route_place(idxs, in_tile, count_col, count_row, carry_s, live):
    rows = in_tile.shape[1]
    eid = lax.broadcasted_iota(jnp.int32, (N_EXPERTS, rows), 0)
    er = lax.broadcasted_iota(jnp.int32, (N_EXPERTS, N_EXPERTS), 0)
    ec = lax.broadcasted_iota(jnp.int32, (N_EXPERTS, N_EXPERTS), 1)
    start_col = _dot(_bf16(jnp.where(ec < er, 1.0, 0.0)), _bf16(count_col))
    local = in_tile + jnp.concatenate([start_col] * (rows // LANES), axis=1)
    slots = _stack_rows([jnp.sum(jnp.where(eid == idx, local, 0.0), axis=0, keepdims=True)
                         for idx in idxs]).astype(jnp.int32)
    lr = lax.broadcasted_iota(jnp.int32, (LANES, LANES), 0)
    lc = lax.broadcasted_iota(jnp.int32, (LANES, LANES), 1)
    start_row = _dot(_bf16(count_row), _bf16(jnp.where(lr < lc, 1.0, 0.0)))
    ri = lax.broadcasted_iota(jnp.int32, (8, LANES), 0)
    meta = jnp.where(ri == 0, carry_s[...], jnp.where(ri == 1, count_row, jnp.where(ri == 2, start_row, 0.0)))
    carry_s[...] = carry_s[...] + count_row * live
    return slots, meta.astype(jnp.int32)


def _mix_kernel(tiles_per_seq, n_prompt_tiles,
                xp_ref, xs_ref, u_ref, vln_ref, q_ref, kd_ref, vd_ref, kdp_ref, vdp_ref,
                ck_ref, cv_ref, ga_ref, gb_ref, wsp_ref, bsp_ref, sink_ref,
                wpa_ref, wpb_ref, wo_ref, gffn_ref, wrh_ref, wrl_ref, br_ref,
                x1_ref, hloc_ref, slot_ref, gate_ref, meta_ref, count_ref,
                a_s, o_s, kwin_s, vwin_s, carry_s, hhi_s, hlo_s):
    i = pl.program_id(0)
    n_streams = ROW_TILE // CHUNK

    @pl.when(i == 0)
    def _():
        carry_s[...] = jnp.zeros_like(carry_s)
        hhi_s[...] = jnp.zeros_like(hhi_s)
        hlo_s[...] = jnp.zeros_like(hlo_s)

    def sgu_rows(r0, rows):
        ri = lax.broadcasted_iota(jnp.int32, (rows, rows), 0) // CHUNK
        ci = lax.broadcasted_iota(jnp.int32, (rows, rows), 1) // CHUNK
        for g in range(SGU_GROUPS):
            cols = slice(g * LANES, (g + 1) * LANES)
            w = _bf16(jnp.where(ci <= ri, wsp_ref[g, :rows, :rows], 0.0))
            sp = _dot(w, vln_ref[r0:r0 + rows, cols]) + bsp_ref[g, :rows, :]
            a_s[r0:r0 + rows, cols] = _bf16(_gelu(u_ref[r0:r0 + rows, cols].astype(jnp.float32)) * sp)

    def attend_rows(r0, kwin_of, valid):
        for g in range(N_KV_HEADS):
            c0 = g * Q_PER_KV * HEAD_DIM
            kwin, vwin = kwin_of(g)
            oa, ob = _attend(q_ref[r0:r0 + CHUNK, c0:c0 + LANES],
                             q_ref[r0:r0 + CHUNK, c0 + LANES:c0 + 2 * LANES],
                             kwin, vwin, sink_ref[g], valid)
            o_s[r0:r0 + CHUNK, c0:c0 + LANES] = _bf16(oa)
            o_s[r0:r0 + CHUNK, c0 + LANES:c0 + 2 * LANES] = _bf16(ob)

    @pl.when(i < n_prompt_tiles)
    def _prompt():
        for c in range(ROW_TILE // SGU_CHUNK):
            sgu_rows(c * SGU_CHUNK, SGU_CHUNK)
        kwin_s[0:WINDOW] = kdp_ref[...]
        kwin_s[WINDOW:WINDOW + ROW_TILE] = kd_ref[...]
        vwin_s[0:WINDOW] = vdp_ref[...]
        vwin_s[WINDOW:WINDOW + ROW_TILE] = vd_ref[...]
        first = (i % tiles_per_seq) == 0
        col = lax.broadcasted_iota(jnp.int32, (1, KEY_SPAN), 1)
        for j in range(ROW_TILE // CHUNK):
            r0 = j * CHUNK
            valid = jnp.logical_or(jnp.logical_not(first), col + r0 >= WINDOW) if r0 < WINDOW else None

            def kwin_of(g, r0=r0):
                cols = slice(g * LANES, (g + 1) * LANES)
                return kwin_s[r0:r0 + KEY_SPAN, cols], vwin_s[r0:r0 + KEY_SPAN, cols]

            attend_rows(r0, kwin_of, valid)

    @pl.when(i >= n_prompt_tiles)
    def _sample():
        for s in range(n_streams):
            r0 = s * CHUNK
            sgu_rows(r0, CHUNK)
            kwin_s[0:WINDOW] = _dup_heads(ck_ref[s])
            kwin_s[WINDOW:KEY_SPAN] = kd_ref[r0:r0 + CHUNK]
            vwin_s[0:WINDOW] = _dup_heads(cv_ref[s])
            vwin_s[WINDOW:KEY_SPAN] = vd_ref[r0:r0 + CHUNK]

            def kwin_of(g):
                cols = slice(g * LANES, (g + 1) * LANES)
                return kwin_s[0:KEY_SPAN, cols], vwin_s[0:KEY_SPAN, cols]

            attend_rows(r0, kwin_of, None)

    hh, hl = hhi_s[...], hlo_s[...]
    logits_t = (_dot_nt(wrh_ref[...], hh) + _dot_nt(wrl_ref[...], hh) + _dot_nt(wrh_ref[...], hl)
                + jnp.concatenate([br_ref[...]] * (ROW_TILE // LANES), axis=1))
    m_a = _sigmoid(ga_ref[...].astype(jnp.float32)) * _dot(a_s[...], _bf16(wpa_ref[...]))
    idxs, gates, in_tile, count_col, count_row = _route_pick(logits_t)
    gate_ref[...] = gates
    m = m_a + _sigmoid(gb_ref[...].astype(jnp.float32)) * _dot(o_s[...], _bf16(wpb_ref[...]))
    slots, meta = _route_place(idxs, in_tile, count_col, count_row, carry_s, jnp.where(i > 0, 1.0, 0.0))
    half = ((i + 1) % 2) * (ROW_TILE * TOP_K * ROW_SUBTILES)
    slot_ref[...] = slots * ROW_SUBTILES + half
    meta_ref[...] = meta
    count_ref[...] = carry_s[...].astype(jnp.int32)
    n_slots = ROW_TILE * TOP_K
    sid = lax.broadcasted_iota(jnp.int32, (n_slots, ROW_TILE), 0)
    place = jnp.zeros((n_slots, ROW_TILE), jnp.float32)
    for k in range(TOP_K):
        place = jnp.where(sid == slots[k:k + 1, :], 1.0, place)
    _store_row_tiled(hloc_ref, (), _dot(_bf16(place), hh))

    x = jnp.where(i < n_prompt_tiles, xp_ref[...], xs_ref[...])
    x1 = x + _dot(_bf16(m), _bf16(wo_ref[...]))
    x1_ref[...] = x1
    h2 = _rms(x1, gffn_ref[...])
    h2_hi = _bf16(h2)
    hhi_s[...] = h2_hi
    hlo_s[...] = _bf16(h2 - h2_hi.astype(jnp.float32))


def _mix(xp, xs, u, vln, q, kd, vd, cache_k, cache_v, ga, gb, w_sp, b_sp, sinks,
         w_pa, w_pb, w_o, g_ffn, w_router, b_router, seq):
    tp, ts = xp.shape[0], xs.shape[0]
    t = tp + ts
    npt = tp // ROW_TILE
    tiles_per_seq = seq // ROW_TILE
    f32, bf16 = jnp.float32, jnp.bfloat16
    n_streams = ROW_TILE // CHUNK
    win_per_tile = ROW_TILE // WINDOW

    nt = t // ROW_TILE
    cur = lambda i: jnp.minimum(i, nt - 1)
    smp = lambda i: jnp.maximum(cur(i) - npt, 0)
    row = lambda width: pl.BlockSpec((ROW_TILE, width), lambda i: (cur(i), 0))
    prev_spec = pl.BlockSpec(
        (WINDOW, KV_DUP_W), lambda i: (jnp.maximum(jnp.minimum(i, npt - 1) * win_per_tile - 1, 0), 0))
    cache_spec = pl.BlockSpec((n_streams, WINDOW, KV_W), lambda i: (smp(i), 0, 0))
    xs_spec = pl.BlockSpec((ROW_TILE, D_MODEL), lambda i: (smp(i), 0))
    sink_cols = jnp.broadcast_to(
        jnp.repeat(sinks.astype(f32).reshape(N_KV_HEADS, Q_PER_KV), CHUNK, axis=1)[:, :, None],
        (N_KV_HEADS, Q_PER_KV * CHUNK, LANES))
    wr_t = w_router.T
    wr_hi = wr_t.astype(bf16)
    wr_lo = (wr_t - wr_hi.astype(f32)).astype(bf16)
    routed8 = lambda width: pl.BlockSpec((8, width), lambda i: (jnp.maximum(i - 1, 0), 0))
    out_shape = (
        jax.ShapeDtypeStruct((t, D_MODEL), f32),
        jax.ShapeDtypeStruct((t * TOP_K * ROW_SUBTILES, LANES), f32),
        jax.ShapeDtypeStruct((nt * 8, ROW_TILE), jnp.int32),
        jax.ShapeDtypeStruct((nt * 8, ROW_TILE), f32),
        jax.ShapeDtypeStruct((nt * 8, LANES), jnp.int32),
        jax.ShapeDtypeStruct((8, LANES), jnp.int32),
    )
    return pl.pallas_call(
        functools.partial(_mix_kernel, tiles_per_seq, npt),
        out_shape=out_shape,
        grid=(nt + 1,),
        in_specs=[
            _prompt_spec(D_MODEL, npt), xs_spec,
            row(SGU_WIDTH), row(SGU_WIDTH), row(ATT_W),
            row(KV_DUP_W), row(KV_DUP_W), prev_spec, prev_spec,
            cache_spec, cache_spec, row(D_MODEL), row(D_MODEL),
            _const_spec((SGU_GROUPS, SGU_CHUNK, SGU_CHUNK)), _const_spec((SGU_GROUPS, SGU_CHUNK, LANES)),
            _const_spec((N_KV_HEADS, Q_PER_KV * CHUNK, LANES)),
            _const_spec((SGU_WIDTH, D_MODEL)), _const_spec((ATT_W, D_MODEL)),
            _const_spec((D_MODEL, D_MODEL)), _const_spec((1, D_MODEL)),
            _const_spec((N_EXPERTS, D_MODEL)), _const_spec((N_EXPERTS, D_MODEL)),
            _const_spec((N_EXPERTS, LANES)),
        ],
        out_specs=(row(D_MODEL),
                   pl.BlockSpec((ROW_TILE * TOP_K * ROW_SUBTILES, LANES),
                                lambda i: (jnp.maximum(i - 1, 0), 0)),
                   routed8(ROW_TILE), routed8(ROW_TILE), routed8(LANES), _const_spec((8, LANES))),
        scratch_shapes=[
            pltpu.VMEM((ROW_TILE, SGU_WIDTH), bf16), pltpu.VMEM((ROW_TILE, ATT_W), bf16),
            pltpu.VMEM((WINDOW + ROW_TILE, KV_DUP_W), bf16),
            pltpu.VMEM((WINDOW + ROW_TILE, KV_DUP_W), bf16),
            pltpu.VMEM((8, LANES), f32),
            pltpu.VMEM((ROW_TILE, D_MODEL), bf16), pltpu.VMEM((ROW_TILE, D_MODEL), bf16),
        ],
        compiler_params=_params(),
        name="mix",
    )(xp, xs, u, vln, q, kd, vd, kd, vd,
      cache_k.reshape(-1, WINDOW, KV_W), cache_v.reshape(-1, WINDOW, KV_W), ga, gb,
      w_sp, jnp.broadcast_to(b_sp[:, :, None], (SGU_GROUPS, SGU_CHUNK, LANES)), sink_cols,
      w_pa, w_pb, w_o,
      g_ffn.reshape(1, -1), wr_hi, wr_lo,
      jnp.broadcast_to(b_router.astype(f32)[:, None], (N_EXPERTS, LANES)))


def _unrolled_rows(n_rows, fn):
    if isinstance(n_rows, int):
        groups, tail_start = n_rows // ROW_UNROLL, n_rows - n_rows % ROW_UNROLL
    else:
        groups = lax.shift_right_logical(n_rows, ROW_UNROLL.bit_length() - 1)
        tail_start = groups * ROW_UNROLL

    def group(gi, carry):
        for lane in range(ROW_UNROLL):
            fn(gi * ROW_UNROLL + lane, lane)
        return carry

    def tail(r, carry):
        fn(r, 0)
        return carry

    lax.fori_loop(0, groups, group, 0)
    lax.fori_loop(tail_start, n_rows, tail, 0)


def _row_span(first_row, n_rows):
    return pl.ds(pl.multiple_of(first_row * ROW_SUBTILES, ROW_SUBTILES),
                 pl.multiple_of(n_rows * ROW_SUBTILES, ROW_SUBTILES))


def _run_spec(index_of):
    return pl.BlockSpec((1, 1, LANES), lambda i, *_: (index_of(i), 0, 0), memory_space=pltpu.SMEM)


def _expert_kernel(n_token_tiles,
                   te_ref, nu_ref, nx_ref, par_ref, nv_ref, tf_ref, cnt_ref, loc_ref,
                   bgu_ref, bdn_ref, hloc_hbm, wgu_hbm, wdn_hbm,
                   ys_ref,
                   xbuf, wgu_f, wdn_f, walk, xsem, wsem):
    i = pl.program_id(0)
    n_used = nu_ref[0]
    expert = te_ref[i]
    buf = par_ref[i]
    slot = i % 2
    expert_changed = jnp.logical_or(i == 0, expert != te_ref[jnp.maximum(i - 1, 0)])

    def fetch_rows(j, b):
        e = te_ref[j]
        need = nv_ref[j]

        @pl.when(tf_ref[j] == 1)
        def _():
            walk[0] = 0
            walk[1] = 0

        @pl.when(need < MOE_TILE)
        def _():
            xbuf[b] = jnp.zeros(xbuf.shape[1:], xbuf.dtype)

        def unfinished(state):
            filled, tile, _ = state
            return jnp.logical_and(filled < need, tile < n_token_tiles)

        def take_run(state):
            filled, tile, off = state
            run = cnt_ref[tile * N_EXPERTS + e]
            take = jnp.minimum(run - off, need - filled)

            @pl.when(take > 0)
            def _():
                src = tile * (ROW_TILE * TOP_K) + loc_ref[tile * N_EXPERTS + e] + off
                pltpu.make_async_copy(hloc_hbm.at[_row_span(src, take)],
                                      xbuf.at[b, _row_span(filled, take)], xsem.at[b]).start()

            run_done = off + take == run
            return (filled + take, jnp.where(run_done, tile + 1, tile), jnp.where(run_done, 0, off + take))

        _, tile, off = lax.while_loop(unfinished, take_run, (jnp.int32(0), walk[0], walk[1]))
        walk[0] = tile
        walk[1] = off

    @pl.when(jnp.logical_and(i == 0, n_used > 0))
    def _():
        fetch_rows(0, 0)

    @pl.when(i + 1 < n_used)
    def _():
        fetch_rows(i + 1, 1 - slot)

    def weight_copies(e, b):
        return (pltpu.make_async_copy(wgu_hbm.at[e], wgu_f.at[b], wsem.at[0, b]),
                pltpu.make_async_copy(wdn_hbm.at[e], wdn_f.at[b], wsem.at[1, b]))

    @pl.when(jnp.logical_and(i < n_used, expert_changed))
    def _():
        @pl.when(i == 0)
        def _():
            for copy in weight_copies(expert, buf):
                copy.start()

        for copy in weight_copies(expert, buf):
            copy.wait()
        following = nx_ref[i]

        @pl.when(following != expert)
        def _():
            for copy in weight_copies(following, 1 - buf):
                copy.start()


    rows = nv_ref[i]

    @pl.when(i < n_used)
    def _():
        pltpu.make_async_copy(hloc_hbm.at[_row_span(0, rows)], xbuf.at[slot, _row_span(0, rows)],
                              xsem.at[slot]).wait()

    def mlp(n):
        x = _bf16(_load_row_tiled(xbuf, (slot,), n))
        gu = _dot(x, _bf16(wgu_f[buf])) + bgu_ref[0]
        gate = jnp.minimum(gu[:, :D_FF], SWIGLU_LIMIT)
        lin = jnp.clip(gu[:, D_FF:], -SWIGLU_LIMIT, SWIGLU_LIMIT)
        act = gate * _sigmoid(SWIGLU_ALPHA * gate) * (lin + 1.0)
        _store_row_tiled(ys_ref, (), _dot(_bf16(act), _bf16(wdn_f[buf])) + bdn_ref[0])

    @pl.when(jnp.logical_and(i < n_used, rows > MOE_TILE // 2))
    def _():
        mlp(MOE_TILE)

    @pl.when(jnp.logical_and(i < n_used, rows <= MOE_TILE // 2))
    def _():
        mlp(MOE_TILE // 2)
        ys_ref[pl.ds(MOE_TILE // 2 * ROW_SUBTILES, MOE_TILE // 2 * ROW_SUBTILES), :] = jnp.zeros(
            (MOE_TILE // 2 * ROW_SUBTILES, LANES), ys_ref.dtype)

    @pl.when(i >= n_used)
    def _():
        ys_ref[...] = jnp.zeros_like(ys_ref)


def _experts(tile_expert, n_used, next_expert, weight_buf, n_valid, tile_first, run_n, run_loc,
             h_local, w_gu, b_gu, w_dn, b_dn):
    n_tiles = tile_expert.shape[0]
    n_token_tiles = h_local.shape[0] // (ROW_TILE * TOP_K * ROW_SUBTILES)
    f32, bf16 = jnp.float32, jnp.bfloat16
    tile_rows = MOE_TILE * ROW_SUBTILES
    grid_spec = pltpu.PrefetchScalarGridSpec(
        num_scalar_prefetch=8,
        grid=(n_tiles,),
        in_specs=[
            pl.BlockSpec((1, 1, 2 * D_FF), lambda i, te, *_: (te[i], 0, 0)),
            pl.BlockSpec((1, 1, D_MODEL), lambda i, te, *_: (te[i], 0, 0)),
            pl.BlockSpec(memory_space=pl.ANY), pl.BlockSpec(memory_space=pl.ANY),
            pl.BlockSpec(memory_space=pl.ANY),
        ],
        out_specs=pl.BlockSpec((tile_rows, LANES), lambda i, *_: (i, 0)),
        scratch_shapes=[
            pltpu.VMEM((2, tile_rows, LANES), f32),
            pltpu.VMEM((2, D_MODEL, 2 * D_FF), f32), pltpu.VMEM((2, D_FF, D_MODEL), f32),
            pltpu.SMEM((2,), jnp.int32),
            pltpu.SemaphoreType.DMA((2,)), pltpu.SemaphoreType.DMA((2, 2)),
        ],
    )
    return pl.pallas_call(
        functools.partial(_expert_kernel, n_token_tiles),
        out_shape=jax.ShapeDtypeStruct((n_tiles * tile_rows, LANES), f32),
        grid_spec=grid_spec,
        compiler_params=_params(),
        name="experts",
    )(tile_expert, n_used, next_expert, weight_buf, n_valid, tile_first, run_n, run_loc,
      b_gu.reshape(N_EXPERTS, 1, -1), b_dn.reshape(N_EXPERTS, 1, -1), h_local, w_gu, w_dn)


def _combine_kernel(n_prompt_tiles,
                    src_ref, n_ref, dst_ref, src_nx_ref, n_nx_ref, dst_nx_ref, slot_ref, gate_ref,
                    x1_ref, gfin_ref, ys_hbm,
                    yp_ref, yo_ref,
                    local, mixed, run_sem):
    i = pl.program_id(0)
    last = pl.num_programs(0) - 1
    buf = i % 2
    half_slots = ROW_TILE * TOP_K
    half_rows = half_slots * ROW_SUBTILES

    def fetch_runs(s_ref, c_ref, d_ref, b):
        for e in range(N_EXPERTS):
            n = c_ref[0, 0, e]
            copy = pltpu.make_async_copy(ys_hbm.at[_row_span(d_ref[0, 0, e], n)],
                                         local.at[_row_span(b * half_slots + s_ref[0, 0, e], n)],
                                         run_sem.at[b])
            pl.when(n > 0)(copy.start)

    @pl.when(i == 0)
    def _():
        fetch_runs(src_ref, n_ref, dst_ref, 0)

    @pl.when(i < last)
    def _():
        fetch_runs(src_nx_ref, n_nx_ref, dst_nx_ref, 1 - buf)

    pltpu.make_async_copy(ys_hbm.at[pl.ds(0, half_rows)],
                          local.at[pl.ds(pl.multiple_of(buf * half_rows, half_rows), half_rows)],
                          run_sem.at[buf]).wait()

    def blend(t, lane):
        acc = None
        for k in range(TOP_K):
            at = pl.multiple_of(slot_ref[0, 0, k * ROW_TILE + t], ROW_SUBTILES)
            term = gate_ref[0, 0, k * ROW_TILE + t] * local[pl.ds(at, ROW_SUBTILES), :]
            acc = term if acc is None else acc + term
        mixed[pl.ds(pl.multiple_of(t * ROW_SUBTILES, ROW_SUBTILES), ROW_SUBTILES), :] = acc
    _unrolled_rows(ROW_TILE, blend)

    out = _rms(x1_ref[...] + _load_row_tiled(mixed, (), ROW_TILE), gfin_ref[...])

    @pl.when(i < n_prompt_tiles)
    def _():
        yp_ref[...] = out

    @pl.when(i >= n_prompt_tiles)
    def _():
        yo_ref[...] = out


def _combine(run_src, run_n, run_dst, slots, gates, x1, ys, g_final, tp):
    t = x1.shape[0]
    npt = tp // ROW_TILE
    nt = t // ROW_TILE
    f32 = jnp.float32
    picks = ROW_TILE * TOP_K
    nxt = lambda i: jnp.minimum(i + 1, nt - 1)
    pick_spec = pl.BlockSpec((1, 1, picks), lambda i: (i, 0, 0), memory_space=pltpu.SMEM)
    return pl.pallas_call(
        functools.partial(_combine_kernel, npt),
        out_shape=(jax.ShapeDtypeStruct((tp, D_MODEL), f32),
                   jax.ShapeDtypeStruct((t - tp, D_MODEL), f32)),
        grid=(nt,),
        in_specs=[_run_spec(lambda i: i), _run_spec(lambda i: i), _run_spec(lambda i: i),
                  _run_spec(nxt), _run_spec(nxt), _run_spec(nxt), pick_spec, pick_spec,
                  _row_spec(D_MODEL), _const_spec((1, D_MODEL)), pl.BlockSpec(memory_space=pl.ANY)],
        out_specs=(_prompt_spec(D_MODEL, npt), _sample_spec(D_MODEL, npt)),
        scratch_shapes=[pltpu.VMEM((2 * picks * ROW_SUBTILES, LANES), f32),
                        pltpu.VMEM((ROW_TILE * ROW_SUBTILES, LANES), f32),
                        pltpu.SemaphoreType.DMA((2,))],
        compiler_params=_params(),
        name="combine",
    )(run_src, run_n, run_dst, run_src, run_n, run_dst, slots, gates, x1, g_final.reshape(1, -1), ys)


def _plan(meta, counts, t):
    nt = t // ROW_TILE
    n_tiles = (t * TOP_K + N_EXPERTS * (MOE_TILE - 1)) // MOE_TILE
    counts = counts[0, :N_EXPERTS]
    tiles_e = (counts + MOE_TILE - 1) // MOE_TILE
    tile_end = jnp.cumsum(tiles_e)
    tile_start = tile_end - tiles_e
    n_used = tile_end[-1]
    first_row = jnp.pad(tile_start * MOE_TILE, (0, LANES - N_EXPERTS))
    meta = meta.reshape(nt, 8, LANES)
    run_dst = meta[:, 0:1, :] + first_row[None, None, :]
    run_n = meta[:, 1:2, :]
    run_src = meta[:, 2:3, :]
    tile_ids = jnp.arange(n_tiles, dtype=jnp.int32)
    live = jnp.minimum(tile_ids, n_used - 1)
    tile_expert = jnp.sum(tile_end[None, :] <= live[:, None], axis=1).astype(jnp.int32)
    ids = jnp.arange(N_EXPERTS, dtype=jnp.int32)
    is_expert = tile_expert[:, None] == ids[None, :]
    of_tile = lambda per_expert: jnp.sum(jnp.where(is_expert, per_expert[None, :], 0), axis=1)
    in_expert = tile_ids - of_tile(tile_start)
    n_valid = jnp.clip(of_tile(counts) - in_expert * MOE_TILE, 0, MOE_TILE)
    n_valid = jnp.where(tile_ids < n_used, n_valid, 0).astype(jnp.int32)
    tile_first = jnp.logical_and(in_expert == 0, tile_ids < n_used).astype(jnp.int32)
    run_n_flat = run_n[:, 0, :N_EXPERTS].reshape(-1)
    run_loc_flat = run_src[:, 0, :N_EXPERTS].reshape(-1)
    used = tiles_e > 0
    later_used = jnp.where(jnp.logical_and(used[None, :], ids[None, :] > ids[:, None]), ids[None, :],
                           N_EXPERTS)
    following = jnp.min(later_used, axis=1)
    following = jnp.where(following < N_EXPERTS, following, ids)
    buf_of = (jnp.cumsum(used.astype(jnp.int32)) - 1) % 2
    return (tile_expert, n_used.reshape(1).astype(jnp.int32), of_tile(following).astype(jnp.int32),
            of_tile(buf_of).astype(jnp.int32), n_valid, tile_first, run_n_flat, run_loc_flat,
            run_src, run_n, run_dst)


def kernel(x_prompt, x_sample, cache_k, cache_v, g_mix, w_in, b_in, ln_v_g, ln_v_b, w_sp, b_sp,
           attn_sinks, w_pa, w_pb, w_o, g_ffn, w_router, b_router, w_gu, b_gu, w_dn, b_dn, g_final):
    nb, seq, d = x_prompt.shape
    nsb, nnew, _ = x_sample.shape
    tp, ts = nb * seq, nsb * nnew
    t = tp + ts
    xp = x_prompt.reshape(tp, d)
    xs = x_sample.reshape(ts, d)
    pos = np.concatenate([np.arange(seq), np.tile(PAST_LEN + np.arange(nnew), ROW_TILE // nnew)])
    tables = _rope_tables(pos)
    u, vln, v_sgu, q, k, v, kd, vd, ga, gb = _project(
        xp, xs, g_mix[0], w_in[0], b_in[0], ln_v_g[0], ln_v_b[0], tables, seq)
    x1, h_local, slot_t, gate_t, meta, counts = _mix(
        xp, xs, u, vln, q, kd, vd, cache_k[0], cache_v[0], ga, gb, w_sp[0], b_sp[0], attn_sinks[0],
        w_pa[0], w_pb[0], w_o[0], g_ffn[0], w_router[0], b_router[0], seq)
    nt = t // ROW_TILE
    picks = lambda a: a.reshape(nt, 8, ROW_TILE)[:, :TOP_K, :].reshape(nt, 1, TOP_K * ROW_TILE)
    slots, gates = picks(slot_t), picks(gate_t)
    (tile_expert, n_used, next_expert, weight_buf, n_valid, tile_first, run_n_flat, run_loc_flat,
     run_src, run_n, run_dst) = _plan(meta, counts, t)
    y_sorted = _experts(tile_expert, n_used, next_expert, weight_buf, n_valid, tile_first, run_n_flat,
                        run_loc_flat, h_local, w_gu[0], b_gu[0], w_dn[0], b_dn[0])
    y_p, y_s = _combine(run_src, run_n, run_dst, slots, gates, x1, y_sorted, g_final, tp)

    keep = min(WINDOW, seq)
    tails = lambda a: jnp.stack([a[(b + 1) * seq - keep:(b + 1) * seq] for b in range(nb)]).reshape(
        nb, keep, N_KV_HEADS, HEAD_DIM)
    kp, vp = tails(k), tails(v)
    ks = k[tp:].reshape(nsb, nnew, N_KV_HEADS, HEAD_DIM)
    vs = v[tp:].reshape(nsb, nnew, N_KV_HEADS, HEAD_DIM)
    return (y_p.reshape(nb, seq, d), y_s.reshape(nsb, nnew, d), kp[None], vp[None], ks[None],
            vs[None], v_sgu.reshape(1, nsb, nnew, SGU_WIDTH))
```

```python
import functools

import numpy as np
import jax
import jax.numpy as jnp
from jax import lax
from jax.experimental import pallas as pl
from jax.experimental.pallas import tpu as pltpu

D_MODEL = 1024
PAST_LEN = 2048
CHUNK = 64
SGU_CHUNK = 128
SGU_GROUPS = 8
SGU_WIDTH = 1024
N_HEADS = 16
N_KV_HEADS = 4
HEAD_DIM = 64
Q_PER_KV = N_HEADS // N_KV_HEADS
WINDOW = 128
ROT_DIM = HEAD_DIM // 4
ROPE_THETA = 500000.0
ATT_W = N_HEADS * HEAD_DIM
KV_W = N_KV_HEADS * HEAD_DIM
N_EXPERTS = 32
TOP_K = 4
D_FF = 1024
SWIGLU_ALPHA = 1.702
SWIGLU_LIMIT = 7.0
NORM_EPS = 1e-5
NEG_INF = -1e30
N_IN = SGU_WIDTH * 2 + ATT_W + KV_W * 2 + D_MODEL * 2

LANES = 128
ROW_TILE = 256
MOE_TILE = 512
ROW_UNROLL = 32
COL_BLOCK = 256
KV_DUP_W = N_KV_HEADS * LANES
KEY_SPAN = WINDOW + CHUNK
VMEM_LIMIT = 56 * 1024 * 1024

_SQRT_HALF = 0.7071067811865476
_LOG2_E = 1.4426950408889634


def _gelu(x):
    t = 1.0 / (1.0 + (0.3275911 * _SQRT_HALF) * jnp.abs(x))
    half_poly = t * (0.127414796 + t * (-0.142248368 + t * (0.7107068705
                     + t * (-0.7265760135 + t * 0.5307027145))))
    half_tail = x * (half_poly * jnp.exp2(x * x * (-0.5 * _LOG2_E)))
    return jnp.where(x >= 0.0, x - half_tail, half_tail)


def _sigmoid(x):
    return 1.0 / (1.0 + jnp.exp(-x))


def _bf16(x):
    return x.astype(jnp.bfloat16)


def _dot(a, b):
    return jnp.dot(a, b, preferred_element_type=jnp.float32)


ROW_SUBTILES = D_MODEL // LANES


def _store_row_tiled(ref, lead, x):
    rows = x.shape[0]
    for s in range(ROW_SUBTILES):
        ref[(*lead, pl.ds(s, rows, stride=ROW_SUBTILES), slice(None))] = x[:, s * LANES:(s + 1) * LANES]


def _load_row_tiled(ref, lead, rows):
    return jnp.concatenate(
        [ref[(*lead, pl.ds(s, rows, stride=ROW_SUBTILES), slice(None))] for s in range(ROW_SUBTILES)],
        axis=1)


def _dot_nt(a, b):
    return lax.dot_general(a, b, (((1,), (1,)), ((), ())), preferred_element_type=jnp.float32)


def _rms(x, g):
    return x * lax.rsqrt(jnp.mean(x * x, axis=-1, keepdims=True) + NORM_EPS) * g


def _lane_lo(rows):
    return lax.broadcasted_iota(jnp.int32, (rows, LANES), 1) < HEAD_DIM


def _dup_heads(kv):
    rows = kv.shape[0]
    lo = _lane_lo(rows)
    out = []
    for j in range(KV_W // LANES):
        blk = kv[:, j * LANES:(j + 1) * LANES]
        swp = pltpu.roll(blk, HEAD_DIM, axis=1)
        out.append(jnp.where(lo, blk, swp))
        out.append(jnp.where(lo, swp, blk))
    return _bf16(jnp.concatenate(out, axis=1))


def _rope_block(zb, cos_b, sin_lo, sin_hi):
    up = pltpu.roll(zb, LANES - ROT_DIM // 2, axis=1)
    dn = pltpu.roll(zb, ROT_DIM // 2, axis=1)
    return zb * cos_b + up * sin_lo + dn * sin_hi


def _proj_kernel(n_prompt_tiles, xp_ref, xs_ref, gmix_ref, w_ref, b_ref, lng_ref, lnb_ref,
                 cos_ref, slo_ref, shi_ref,
                 u_ref, vln_ref, vs_ref, q_ref, k_ref, v_ref, kd_ref, vd_ref, ga_ref, gb_ref):
    i = pl.program_id(0)
    x = jnp.where(i < n_prompt_tiles, xp_ref[...], xs_ref[...])
    h = _bf16(_rms(x, gmix_ref[...]))

    cos_b, sin_lo, sin_hi = cos_ref[...], slo_ref[...], shi_ref[...]
    off_u, off_v, off_q, off_k, off_vv, off_ga, off_gb = (
        int(o) for o in np.cumsum((0, SGU_WIDTH, SGU_WIDTH, ATT_W, KV_W, KV_W, D_MODEL)))

    def z(lo):
        return _dot(h, w_ref[:, lo:lo + COL_BLOCK]) + b_ref[:, lo:lo + COL_BLOCK]

    def cols(j):
        return slice(j * COL_BLOCK, (j + 1) * COL_BLOCK)

    def rope(zb):
        return jnp.concatenate(
            [_rope_block(zb[:, t * LANES:(t + 1) * LANES], cos_b, sin_lo, sin_hi)
             for t in range(COL_BLOCK // LANES)], axis=1)

    gelu_v = []

    def do_u(j):
        u_ref[:, cols(j)] = _bf16(z(off_u + j * COL_BLOCK))

    def do_v(j):
        gelu_v.append(_gelu(z(off_v + j * COL_BLOCK)))

    def do_q(j):
        q_ref[:, cols(j)] = _bf16(rope(z(off_q + j * COL_BLOCK)) * (HEAD_DIM ** -0.5))

    def do_k(j):
        kr = rope(z(off_k))
        k_ref[...] = kr
        kd_ref[...] = _dup_heads(kr)

    def do_vv(j):
        zv = z(off_vv)
        v_ref[...] = zv
        vd_ref[...] = _dup_heads(zv)

    def do_ga(j):
        ga_ref[:, cols(j)] = _bf16(z(off_ga + j * COL_BLOCK))

    def do_gb(j):
        gb_ref[:, cols(j)] = _bf16(z(off_gb + j * COL_BLOCK))

    order = ((do_v, 0), (do_q, 0), (do_q, 1), (do_u, 0), (do_v, 1), (do_q, 2), (do_q, 3), (do_u, 1),
             (do_v, 2), (do_k, 0), (do_vv, 0), (do_u, 2), (do_v, 3), (do_ga, 0), (do_ga, 1), (do_u, 3),
             (do_ga, 2), (do_ga, 3), (do_gb, 0), (do_gb, 1), (do_gb, 2), (do_gb, 3))
    for fn, j in order:
        fn(j)
    gv = jnp.concatenate(gelu_v, axis=1)
    gc = gv - jnp.mean(gv, axis=-1, keepdims=True)
    var = jnp.mean(gc * gc, axis=-1, keepdims=True)
    vln = gc * lax.rsqrt(var + NORM_EPS) * lng_ref[...] + lnb_ref[...]
    vln_ref[...] = _bf16(vln)
    vs_ref[...] = vln


def _rope_tables(pos):
    half = ROT_DIM // 2
    inv = np.float32(ROPE_THETA) ** (-np.arange(half, dtype=np.float32) * np.float32(2.0) / ROT_DIM)
    ang = pos.astype(np.float32)[:, None] * inv.astype(np.float32)[None, :]
    cos = np.cos(ang.astype(np.float64)).astype(np.float32)
    sin = np.sin(ang.astype(np.float64)).astype(np.float32)
    n = pos.shape[0]
    ones = np.ones((n, HEAD_DIM - ROT_DIM), np.float32)
    zeros = np.zeros((n, HEAD_DIM - ROT_DIM), np.float32)
    zh = np.zeros((n, half), np.float32)
    cos_h = np.concatenate([cos, cos, ones], axis=1)
    slo_h = np.concatenate([-sin, zh, zeros], axis=1)
    shi_h = np.concatenate([zh, sin, zeros], axis=1)
    rep = LANES // HEAD_DIM
    return tuple(jnp.asarray(np.tile(a, (1, rep))) for a in (cos_h, slo_h, shi_h))


def _row_spec(width):
    return pl.BlockSpec((ROW_TILE, width), lambda i: (i, 0))


def _const_spec(shape):
    return pl.BlockSpec(shape, lambda i: (0,) * len(shape))


def _prompt_spec(width, n_prompt_tiles):
    return pl.BlockSpec((ROW_TILE, width), lambda i: (jnp.minimum(i, n_prompt_tiles - 1), 0))


def _sample_spec(width, n_prompt_tiles):
    return pl.BlockSpec((ROW_TILE, width), lambda i: (jnp.maximum(i - n_prompt_tiles, 0), 0))


def _params():
    return pltpu.CompilerParams(dimension_semantics=("arbitrary",), vmem_limit_bytes=VMEM_LIMIT)


def _project(xp, xs, g_mix, w_in, b_in, ln_g, ln_b, tables, seq):
    tp, ts = xp.shape[0], xs.shape[0]
    t = tp + ts
    npt = tp // ROW_TILE
    tiles_per_seq = seq // ROW_TILE
    f32, bf16 = jnp.float32, jnp.bfloat16
    table_spec = pl.BlockSpec(
        (ROW_TILE, LANES), lambda i: (jnp.where(i < npt, i % tiles_per_seq, tiles_per_seq), 0))
    out_shape = (
        jax.ShapeDtypeStruct((t, SGU_WIDTH), bf16),
        jax.ShapeDtypeStruct((t, SGU_WIDTH), bf16),
        jax.ShapeDtypeStruct((ts, SGU_WIDTH), f32),
        jax.ShapeDtypeStruct((t, ATT_W), bf16),
        jax.ShapeDtypeStruct((t, KV_W), f32),
        jax.ShapeDtypeStruct((t, KV_W), f32),
        jax.ShapeDtypeStruct((t, KV_DUP_W), bf16),
        jax.ShapeDtypeStruct((t, KV_DUP_W), bf16),
        jax.ShapeDtypeStruct((t, D_MODEL), bf16),
        jax.ShapeDtypeStruct((t, D_MODEL), bf16),
    )
    return pl.pallas_call(
        functools.partial(_proj_kernel, npt),
        out_shape=out_shape,
        grid=(t // ROW_TILE,),
        in_specs=[
            _prompt_spec(D_MODEL, npt), _sample_spec(D_MODEL, npt),
            _const_spec((1, D_MODEL)), _const_spec((D_MODEL, N_IN)),
            _const_spec((1, N_IN)), _const_spec((1, SGU_WIDTH)), _const_spec((1, SGU_WIDTH)),
            table_spec, table_spec, table_spec,
        ],
        out_specs=(
            _row_spec(SGU_WIDTH), _row_spec(SGU_WIDTH), _sample_spec(SGU_WIDTH, npt),
            _row_spec(ATT_W), _row_spec(KV_W), _row_spec(KV_W), _row_spec(KV_DUP_W),
            _row_spec(KV_DUP_W), _row_spec(D_MODEL), _row_spec(D_MODEL),
        ),
        compiler_params=_params(),
        name="proj",
    )(xp, xs, g_mix.reshape(1, -1), w_in.astype(bf16), b_in.reshape(1, -1),
      ln_g.reshape(1, -1), ln_b.reshape(1, -1), *tables)


def _attend(qa, qb, kwin, vwin, sink, valid):
    lo = _lane_lo(CHUNK)
    zero = jnp.zeros_like(qa)
    lhs = jnp.concatenate([jnp.where(lo, qa, zero), jnp.where(lo, zero, qa),
                           jnp.where(lo, qb, zero), jnp.where(lo, zero, qb)], axis=0)
    s = _dot_nt(lhs, kwin)
    if valid is not None:
        s = jnp.where(valid, s, NEG_INF)
    s_a, s_b = s[:, :LANES], s[:, LANES:]
    tail = s_b.shape[1]
    m = jnp.maximum(jnp.max(s, axis=-1, keepdims=True), sink)
    p_a = jnp.exp(s_a - m)
    p_b = jnp.exp(s_b - m[:, :tail])
    denom = (jnp.sum(jnp.concatenate([p_a, p_b], axis=1), axis=-1, keepdims=True)
             + jnp.exp(sink - m))
    inv = 1.0 / denom
    pn = jnp.concatenate([p_a * inv, p_b * inv[:, :tail]], axis=1)
    r = _dot(_bf16(pn), vwin)
    oa = jnp.where(lo, r[0:CHUNK], r[CHUNK:2 * CHUNK])
    ob = jnp.where(lo, r[2 * CHUNK:3 * CHUNK], r[3 * CHUNK:4 * CHUNK])
    return oa, ob


def _stack_rows(rows):
    ri = lax.broadcasted_iota(jnp.int32, (8, rows[0].shape[1]), 0)
    out = jnp.zeros((8, rows[0].shape[1]), rows[0].dtype)
    for k, row in enumerate(rows):
        out = jnp.where(ri == k, row, out)
    return out


def _route_pick(logits_t):
    rows = logits_t.shape[1]
    eid = lax.broadcasted_iota(jnp.int32, (N_EXPERTS, rows), 0)
    work = logits_t
    vals, idxs = [], []
    for _ in range(TOP_K):
        m = jnp.max(work, axis=0, keepdims=True)
        idx = jnp.min(jnp.where(work == m, eid, N_EXPERTS), axis=0, keepdims=True)
        vals.append(m)
        idxs.append(idx)
        work = jnp.where(eid == idx, -jnp.inf, work)
    exps = [jnp.exp(v - vals[0]) for v in vals]
    inv = 1.0 / (exps[0] + exps[1] + exps[2] + exps[3])
    gates = _stack_rows([e * inv for e in exps])

    picked = jnp.zeros((N_EXPERTS, rows), jnp.float32)
    for idx in idxs:
        picked = jnp.where(eid == idx, 1.0, picked)
    tr = lax.broadcasted_iota(jnp.int32, (rows, rows), 0)
    tc = lax.broadcasted_iota(jnp.int32, (rows, rows), 1)
    earlier = _bf16(jnp.where(tr < tc, 1.0, 0.0))
    in_tile = _dot(_bf16(picked), earlier)
    count_col = jnp.broadcast_to(jnp.sum(picked, axis=1, keepdims=True), (N_EXPERTS, LANES))
    eid_wide = lax.broadcasted_iota(jnp.int32, (LANES, rows), 0)
    picked_wide = jnp.zeros((LANES, rows), jnp.float32)
    for idx in idxs:
        picked_wide = jnp.where(eid_wide == idx, 1.0, picked_wide)
    count_row = _dot_nt(jnp.ones((8, rows), jnp.bfloat16), _bf16(picked_wide))
    return idxs, gates, in_tile, count_col, count_row


def _route_place(idxs, in_tile, count_col, count_row, carry_s, live):
    rows = in_tile.shape[1]
    eid = lax.broadcasted_iota(jnp.int32, (N_EXPERTS, rows), 0)
    er = lax.broadcasted_iota(jnp.int32, (N_EXPERTS, N_EXPERTS), 0)
    ec = lax.broadcasted_iota(jnp.int32, (N_EXPERTS, N_EXPERTS), 1)
    start_col = _dot(_bf16(jnp.where(ec < er, 1.0, 0.0)), _bf16(count_col))
    local = in_tile + jnp.concatenate([start_col] * (rows // LANES), axis=1)
    slots = _stack_rows([jnp.sum(jnp.where(eid == idx, local, 0.0), axis=0, keepdims=True)
                         for idx in idxs]).astype(jnp.int32)
    lr = lax.broadcasted_iota(jnp.int32, (LANES, LANES), 0)
    lc = lax.broadcasted_iota(jnp.int32, (LANES, LANES), 1)
    start_row = _dot(_bf16(count_row), _bf16(jnp.where(lr < lc, 1.0, 0.0)))
    ri = lax.broadcasted_iota(jnp.int32, (8, LANES), 0)
    meta = jnp.where(ri == 0, carry_s[...], jnp.where(ri == 1, count_row, jnp.where(ri == 2, start_row, 0.0)))
    carry_s[...] = carry_s[...] + count_row * live
    return slots, meta.astype(jnp.int32)


def _mix_kernel(tiles_per_seq, n_prompt_tiles,
                xp_ref, xs_ref, u_ref, vln_ref, q_ref, kd_ref, vd_ref, kdp_ref, vdp_ref,
                ck_ref, cv_ref, ga_ref, gb_ref, wsp_ref, bsp_ref, sink_ref,
                wpa_ref, wpb_ref, wo_ref, gffn_ref, wrh_ref, wrl_ref, br_ref,
                x1_ref, hloc_ref, slot_ref, gate_ref, meta_ref, count_ref,
                a_s, o_s, kwin_s, vwin_s, carry_s, hhi_s, hlo_s):
    i = pl.program_id(0)
    n_streams = ROW_TILE // CHUNK

    @pl.when(i == 0)
    def _():
        carry_s[...] = jnp.zeros_like(carry_s)
        hhi_s[...] = jnp.zeros_like(hhi_s)
        hlo_s[...] = jnp.zeros_like(hlo_s)

    def sgu_rows(r0, rows):
        ri = lax.broadcasted_iota(jnp.int32, (rows, rows), 0) // CHUNK
        ci = lax.broadcasted_iota(jnp.int32, (rows, rows), 1) // CHUNK
        for g in range(SGU_GROUPS):
            cols = slice(g * LANES, (g + 1) * LANES)
            w = _bf16(jnp.where(ci <= ri, wsp_ref[g, :rows, :rows], 0.0))
            sp = _dot(w, vln_ref[r0:r0 + rows, cols]) + bsp_ref[g, :rows, :]
            a_s[r0:r0 + rows, cols] = _bf16(_gelu(u_ref[r0:r0 + rows, cols].astype(jnp.float32)) * sp)

    def attend_rows(r0, kwin_of, valid):
        for g in range(N_KV_HEADS):
            c0 = g * Q_PER_KV * HEAD_DIM
            kwin, vwin = kwin_of(g)
            oa, ob = _attend(q_ref[r0:r0 + CHUNK, c0:c0 + LANES],
                             q_ref[r0:r0 + CHUNK, c0 + LANES:c0 + 2 * LANES],
                             kwin, vwin, sink_ref[g], valid)
            o_s[r0:r0 + CHUNK, c0:c0 + LANES] = _bf16(oa)
            o_s[r0:r0 + CHUNK, c0 + LANES:c0 + 2 * LANES] = _bf16(ob)

    @pl.when(i < n_prompt_tiles)
    def _prompt():
        for c in range(ROW_TILE // SGU_CHUNK):
            sgu_rows(c * SGU_CHUNK, SGU_CHUNK)
        kwin_s[0:WINDOW] = kdp_ref[...]
        kwin_s[WINDOW:WINDOW + ROW_TILE] = kd_ref[...]
        vwin_s[0:WINDOW] = vdp_ref[...]
        vwin_s[WINDOW:WINDOW + ROW_TILE] = vd_ref[...]
        first = (i % tiles_per_seq) == 0
        col = lax.broadcasted_iota(jnp.int32, (1, KEY_SPAN), 1)
        for j in range(ROW_TILE // CHUNK):
            r0 = j * CHUNK
            valid = jnp.logical_or(jnp.logical_not(first), col + r0 >= WINDOW) if r0 < WINDOW else None

            def kwin_of(g, r0=r0):
                cols = slice(g * LANES, (g + 1) * LANES)
                return kwin_s[r0:r0 + KEY_SPAN, cols], vwin_s[r0:r0 + KEY_SPAN, cols]

            attend_rows(r0, kwin_of, valid)

    @pl.when(i >= n_prompt_tiles)
    def _sample():
        for s in range(n_streams):
            r0 = s * CHUNK
            sgu_rows(r0, CHUNK)
            kwin_s[0:WINDOW] = _dup_heads(ck_ref[s])
            kwin_s[WINDOW:KEY_SPAN] = kd_ref[r0:r0 + CHUNK]
            vwin_s[0:WINDOW] = _dup_heads(cv_ref[s])
            vwin_s[WINDOW:KEY_SPAN] = vd_ref[r0:r0 + CHUNK]

            def kwin_of(g):
                cols = slice(g * LANES, (g + 1) * LANES)
                return kwin_s[0:KEY_SPAN, cols], vwin_s[0:KEY_SPAN, cols]

            attend_rows(r0, kwin_of, None)

    hh, hl = hhi_s[...], hlo_s[...]
    logits_t = (_dot_nt(wrh_ref[...], hh) + _dot_nt(wrl_ref[...], hh) + _dot_nt(wrh_ref[...], hl)
                + jnp.concatenate([br_ref[...]] * (ROW_TILE // LANES), axis=1))
    m_a = _sigmoid(ga_ref[...].astype(jnp.float32)) * _dot(a_s[...], _bf16(wpa_ref[...]))
    idxs, gates, in_tile, count_col, count_row = _route_pick(logits_t)
    gate_ref[...] = gates
    m = m_a + _sigmoid(gb_ref[...].astype(jnp.float32)) * _dot(o_s[...], _bf16(wpb_ref[...]))
    slots, meta = _route_place(idxs, in_tile, count_col, count_row, carry_s, jnp.where(i > 0, 1.0, 0.0))
    half = ((i + 1) % 2) * (ROW_TILE * TOP_K * ROW_SUBTILES)
    slot_ref[...] = slots * ROW_SUBTILES + half
    meta_ref[...] = meta
    count_ref[...] = carry_s[...].astype(jnp.int32)
    n_slots = ROW_TILE * TOP_K
    sid = lax.broadcasted_iota(jnp.int32, (n_slots, ROW_TILE), 0)
    place = jnp.zeros((n_slots, ROW_TILE), jnp.float32)
    for k in range(TOP_K):
        place = jnp.where(sid == slots[k:k + 1, :], 1.0, place)
    _store_row_tiled(hloc_ref, (), _dot(_bf16(place), hh))

    x = jnp.where(i < n_prompt_tiles, xp_ref[...], xs_ref[...])
    x1 = x + _dot(_bf16(m), _bf16(wo_ref[...]))
    x1_ref[...] = x1
    h2 = _rms(x1, gffn_ref[...])
    h2_hi = _bf16(h2)
    hhi_s[...] = h2_hi
    hlo_s[...] = _bf16(h2 - h2_hi.astype(jnp.float32))


def _mix(xp, xs, u, vln, q, kd, vd, cache_k, cache_v, ga, gb, w_sp, b_sp, sinks,
         w_pa, w_pb, w_o, g_ffn, w_router, b_router, seq):
    tp, ts = xp.shape[0], xs.shape[0]
    t = tp + ts
    npt = tp // ROW_TILE
    tiles_per_seq = seq // ROW_TILE
    f32, bf16 = jnp.float32, jnp.bfloat16
    n_streams = ROW_TILE // CHUNK
    win_per_tile = ROW_TILE // WINDOW

    nt = t // ROW_TILE
    cur = lambda i: jnp.minimum(i, nt - 1)
    smp = lambda i: jnp.maximum(cur(i) - npt, 0)
    row = lambda width: pl.BlockSpec((ROW_TILE, width), lambda i: (cur(i), 0))
    prev_spec = pl.BlockSpec(
        (WINDOW, KV_DUP_W), lambda i: (jnp.maximum(jnp.minimum(i, npt - 1) * win_per_tile - 1, 0), 0))
    cache_spec = pl.BlockSpec((n_streams, WINDOW, KV_W), lambda i: (smp(i), 0, 0))
    xs_spec = pl.BlockSpec((ROW_TILE, D_MODEL), lambda i: (smp(i), 0))
    sink_cols = jnp.broadcast_to(
        jnp.repeat(sinks.astype(f32).reshape(N_KV_HEADS, Q_PER_KV), CHUNK, axis=1)[:, :, None],
        (N_KV_HEADS, Q_PER_KV * CHUNK, LANES))
    wr_t = w_router.T
    wr_hi = wr_t.astype(bf16)
    wr_lo = (wr_t - wr_hi.astype(f32)).astype(bf16)
    routed8 = lambda width: pl.BlockSpec((8, width), lambda i: (jnp.maximum(i - 1, 0), 0))
    out_shape = (
        jax.ShapeDtypeStruct((t, D_MODEL), f32),
        jax.ShapeDtypeStruct((t * TOP_K * ROW_SUBTILES, LANES), f32),
        jax.ShapeDtypeStruct((nt * 8, ROW_TILE), jnp.int32),
        jax.ShapeDtypeStruct((nt * 8, ROW_TILE), f32),
        jax.ShapeDtypeStruct((nt * 8, LANES), jnp.int32),
        jax.ShapeDtypeStruct((8, LANES), jnp.int32),
    )
    return pl.pallas_call(
        functools.partial(_mix_kernel, tiles_per_seq, npt),
        out_shape=out_shape,
        grid=(nt + 1,),
        in_specs=[
            _prompt_spec(D_MODEL, npt), xs_spec,
            row(SGU_WIDTH), row(SGU_WIDTH), row(ATT_W),
            row(KV_DUP_W), row(KV_DUP_W), prev_spec, prev_spec,
            cache_spec, cache_spec, row(D_MODEL), row(D_MODEL),
            _const_spec((SGU_GROUPS, SGU_CHUNK, SGU_CHUNK)), _const_spec((SGU_GROUPS, SGU_CHUNK, LANES)),
            _const_spec((N_KV_HEADS, Q_PER_KV * CHUNK, LANES)),
            _const_spec((SGU_WIDTH, D_MODEL)), _const_spec((ATT_W, D_MODEL)),
            _const_spec((D_MODEL, D_MODEL)), _const_spec((1, D_MODEL)),
            _const_spec((N_EXPERTS, D_MODEL)), _const_spec((N_EXPERTS, D_MODEL)),
            _const_spec((N_EXPERTS, LANES)),
        ],
        out_specs=(row(D_MODEL),
                   pl.BlockSpec((ROW_TILE * TOP_K * ROW_SUBTILES, LANES),
                                lambda i: (jnp.maximum(i - 1, 0), 0)),
                   routed8(ROW_TILE), routed8(ROW_TILE), routed8(LANES), _const_spec((8, LANES))),
        scratch_shapes=[
            pltpu.VMEM((ROW_TILE, SGU_WIDTH), bf16), pltpu.VMEM((ROW_TILE, ATT_W), bf16),
            pltpu.VMEM((WINDOW + ROW_TILE, KV_DUP_W), bf16),
            pltpu.VMEM((WINDOW + ROW_TILE, KV_DUP_W), bf16),
            pltpu.VMEM((8, LANES), f32),
            pltpu.VMEM((ROW_TILE, D_MODEL), bf16), pltpu.VMEM((ROW_TILE, D_MODEL), bf16),
        ],
        compiler_params=_params(),
        name="mix",
    )(xp, xs, u, vln, q, kd, vd, kd, vd,
      cache_k.reshape(-1, WINDOW, KV_W), cache_v.reshape(-1, WINDOW, KV_W), ga, gb,
      w_sp, jnp.broadcast_to(b_sp[:, :, None], (SGU_GROUPS, SGU_CHUNK, LANES)), sink_cols,
      w_pa, w_pb, w_o,
      g_ffn.reshape(1, -1), wr_hi, wr_lo,
      jnp.broadcast_to(b_router.astype(f32)[:, None], (N_EXPERTS, LANES)))


def _unrolled_rows(n_rows, fn):
    if isinstance(n_rows, int):
        groups, tail_start = n_rows // ROW_UNROLL, n_rows - n_rows % ROW_UNROLL
    else:
        groups = lax.shift_right_logical(n_rows, ROW_UNROLL.bit_length() - 1)
        tail_start = groups * ROW_UNROLL

    def group(gi, carry):
        for lane in range(ROW_UNROLL):
            fn(gi * ROW_UNROLL + lane, lane)
        return carry

    def tail(r, carry):
        fn(r, 0)
        return carry

    lax.fori_loop(0, groups, group, 0)
    lax.fori_loop(tail_start, n_rows, tail, 0)


def _row_span(first_row, n_rows):
    return pl.ds(pl.multiple_of(first_row * ROW_SUBTILES, ROW_SUBTILES),
                 pl.multiple_of(n_rows * ROW_SUBTILES, ROW_SUBTILES))


def _run_spec(index_of):
    return pl.BlockSpec((1, 1, LANES), lambda i, *_: (index_of(i), 0, 0), memory_space=pltpu.SMEM)


def _expert_kernel(n_token_tiles,
                   te_ref, nu_ref, nx_ref, par_ref, nv_ref, tf_ref, cnt_ref, loc_ref,
                   bgu_ref, bdn_ref, hloc_hbm, wgu_hbm, wdn_hbm,
                   ys_ref,
                   xbuf, wgu_f, wdn_f, walk, xsem, wsem):
    i = pl.program_id(0)
    n_used = nu_ref[0]
    expert = te_ref[i]
    buf = par_ref[i]
    slot = i % 2
    expert_changed = jnp.logical_or(i == 0, expert != te_ref[jnp.maximum(i - 1, 0)])

    def fetch_rows(j, b):
        e = te_ref[j]
        need = nv_ref[j]

        @pl.when(tf_ref[j] == 1)
        def _():
            walk[0] = 0
            walk[1] = 0

        @pl.when(need < MOE_TILE)
        def _():
            xbuf[b] = jnp.zeros(xbuf.shape[1:], xbuf.dtype)

        def unfinished(state):
            filled, tile, _ = state
            return jnp.logical_and(filled < need, tile < n_token_tiles)

        def take_run(state):
            filled, tile, off = state
            run = cnt_ref[tile * N_EXPERTS + e]
            take = jnp.minimum(run - off, need - filled)

            @pl.when(take > 0)
            def _():
                src = tile * (ROW_TILE * TOP_K) + loc_ref[tile * N_EXPERTS + e] + off
                pltpu.make_async_copy(hloc_hbm.at[_row_span(src, take)],
                                      xbuf.at[b, _row_span(filled, take)], xsem.at[b]).start()

            run_done = off + take == run
            return (filled + take, jnp.where(run_done, tile + 1, tile), jnp.where(run_done, 0, off + take))

        _, tile, off = lax.while_loop(unfinished, take_run, (jnp.int32(0), walk[0], walk[1]))
        walk[0] = tile
        walk[1] = off

    @pl.when(jnp.logical_and(i == 0, n_used > 0))
    def _():
        fetch_rows(0, 0)

    @pl.when(i + 1 < n_used)
    def _():
        fetch_rows(i + 1, 1 - slot)

    def weight_copies(e, b):
        return (pltpu.make_async_copy(wgu_hbm.at[e], wgu_f.at[b], wsem.at[0, b]),
                pltpu.make_async_copy(wdn_hbm.at[e], wdn_f.at[b], wsem.at[1, b]))

    @pl.when(jnp.logical_and(i < n_used, expert_changed))
    def _():
        @pl.when(i == 0)
        def _():
            for copy in weight_copies(expert, buf):
                copy.start()

        for copy in weight_copies(expert, buf):
            copy.wait()
        following = nx_ref[i]

        @pl.when(following != expert)
        def _():
            for copy in weight_copies(following, 1 - buf):
                copy.start()


    rows = nv_ref[i]

    @pl.when(i < n_used)
    def _():
        pltpu.make_async_copy(hloc_hbm.at[_row_span(0, rows)], xbuf.at[slot, _row_span(0, rows)],
                              xsem.at[slot]).wait()

    def mlp(n):
        x = _bf16(_load_row_tiled(xbuf, (slot,), n))
        gu = _dot(x, _bf16(wgu_f[buf])) + bgu_ref[0]
        gate = jnp.minimum(gu[:, :D_FF], SWIGLU_LIMIT)
        lin = jnp.clip(gu[:, D_FF:], -SWIGLU_LIMIT, SWIGLU_LIMIT)
        act = gate * _sigmoid(SWIGLU_ALPHA * gate) * (lin + 1.0)
        _store_row_tiled(ys_ref, (), _dot(_bf16(act), _bf16(wdn_f[buf])) + bdn_ref[0])

    @pl.when(jnp.logical_and(i < n_used, rows > MOE_TILE // 2))
    def _():
        mlp(MOE_TILE)

    @pl.when(jnp.logical_and(i < n_used, rows <= MOE_TILE // 2))
    def _():
        mlp(MOE_TILE // 2)
        ys_ref[pl.ds(MOE_TILE // 2 * ROW_SUBTILES, MOE_TILE // 2 * ROW_SUBTILES), :] = jnp.zeros(
            (MOE_TILE // 2 * ROW_SUBTILES, LANES), ys_ref.dtype)

    @pl.when(i >= n_used)
    def _():
        ys_ref[...] = jnp.zeros_like(ys_ref)


def _experts(tile_expert, n_used, next_expert, weight_buf, n_valid, tile_first, run_n, run_loc,
             h_local, w_gu, b_gu, w_dn, b_dn):
    n_tiles = tile_expert.shape[0]
    n_token_tiles = h_local.shape[0] // (ROW_TILE * TOP_K * ROW_SUBTILES)
    f32, bf16 = jnp.float32, jnp.bfloat16
    tile_rows = MOE_TILE * ROW_SUBTILES
    grid_spec = pltpu.PrefetchScalarGridSpec(
        num_scalar_prefetch=8,
        grid=(n_tiles,),
        in_specs=[
            pl.BlockSpec((1, 1, 2 * D_FF), lambda i, te, *_: (te[i], 0, 0)),
            pl.BlockSpec((1, 1, D_MODEL), lambda i, te, *_: (te[i], 0, 0)),
            pl.BlockSpec(memory_space=pl.ANY), pl.BlockSpec(memory_space=pl.ANY),
            pl.BlockSpec(memory_space=pl.ANY),
        ],
        out_specs=pl.BlockSpec((tile_rows, LANES), lambda i, *_: (i, 0)),
        scratch_shapes=[
            pltpu.VMEM((2, tile_rows, LANES), f32),
            pltpu.VMEM((2, D_MODEL, 2 * D_FF), f32), pltpu.VMEM((2, D_FF, D_MODEL), f32),
            pltpu.SMEM((2,), jnp.int32),
            pltpu.SemaphoreType.DMA((2,)), pltpu.SemaphoreType.DMA((2, 2)),
        ],
    )
    return pl.pallas_call(
        functools.partial(_expert_kernel, n_token_tiles),
        out_shape=jax.ShapeDtypeStruct((n_tiles * tile_rows, LANES), f32),
        grid_spec=grid_spec,
        compiler_params=_params(),
        name="experts",
    )(tile_expert, n_used, next_expert, weight_buf, n_valid, tile_first, run_n, run_loc,
      b_gu.reshape(N_EXPERTS, 1, -1), b_dn.reshape(N_EXPERTS, 1, -1), h_local, w_gu, w_dn)


def _combine_kernel(n_prompt_tiles,
                    src_ref, n_ref, dst_ref, src_nx_ref, n_nx_ref, dst_nx_ref, slot_ref, gate_ref,
                    x1_ref, gfin_ref, ys_hbm,
                    yp_ref, yo_ref,
                    local, mixed, run_sem):
    i = pl.program_id(0)
    last = pl.num_programs(0) - 1
    buf = i % 2
    half_slots = ROW_TILE * TOP_K
    half_rows = half_slots * ROW_SUBTILES

    def fetch_runs(s_ref, c_ref, d_ref, b):
        for e in range(N_EXPERTS):
            n = c_ref[0, 0, e]
            copy = pltpu.make_async_copy(ys_hbm.at[_row_span(d_ref[0, 0, e], n)],
                                         local.at[_row_span(b * half_slots + s_ref[0, 0, e], n)],
                                         run_sem.at[b])
            pl.when(n > 0)(copy.start)

    @pl.when(i == 0)
    def _():
        fetch_runs(src_ref, n_ref, dst_ref, 0)

    @pl.when(i < last)
    def _():
        fetch_runs(src_nx_ref, n_nx_ref, dst_nx_ref, 1 - buf)

    pltpu.make_async_copy(ys_hbm.at[pl.ds(0, half_rows)],
                          local.at[pl.ds(pl.multiple_of(buf * half_rows, half_rows), half_rows)],
                          run_sem.at[buf]).wait()

    def blend(t, lane):
        acc = None
        for k in range(TOP_K):
            at = pl.multiple_of(slot_ref[0, 0, k * ROW_TILE + t], ROW_SUBTILES)
            term = gate_ref[0, 0, k * ROW_TILE + t] * local[pl.ds(at, ROW_SUBTILES), :]
            acc = term if acc is None else acc + term
        mixed[pl.ds(pl.multiple_of(t * ROW_SUBTILES, ROW_SUBTILES), ROW_SUBTILES), :] = acc
    _unrolled_rows(ROW_TILE, blend)

    out = _rms(x1_ref[...] + _load_row_tiled(mixed, (), ROW_TILE), gfin_ref[...])

    @pl.when(i < n_prompt_tiles)
    def _():
        yp_ref[...] = out

    @pl.when(i >= n_prompt_tiles)
    def _():
        yo_ref[...] = out


def _combine(run_src, run_n, run_dst, slots, gates, x1, ys, g_final, tp):
    t = x1.shape[0]
    npt = tp // ROW_TILE
    nt = t // ROW_TILE
    f32 = jnp.float32
    picks = ROW_TILE * TOP_K
    nxt = lambda i: jnp.minimum(i + 1, nt - 1)
    pick_spec = pl.BlockSpec((1, 1, picks), lambda i: (i, 0, 0), memory_space=pltpu.SMEM)
    return pl.pallas_call(
        functools.partial(_combine_kernel, npt),
        out_shape=(jax.ShapeDtypeStruct((tp, D_MODEL), f32),
                   jax.ShapeDtypeStruct((t - tp, D_MODEL), f32)),
        grid=(nt,),
        in_specs=[_run_spec(lambda i: i), _run_spec(lambda i: i), _run_spec(lambda i: i),
                  _run_spec(nxt), _run_spec(nxt), _run_spec(nxt), pick_spec, pick_spec,
                  _row_spec(D_MODEL), _const_spec((1, D_MODEL)), pl.BlockSpec(memory_space=pl.ANY)],
        out_specs=(_prompt_spec(D_MODEL, npt), _sample_spec(D_MODEL, npt)),
        scratch_shapes=[pltpu.VMEM((2 * picks * ROW_SUBTILES, LANES), f32),
                        pltpu.VMEM((ROW_TILE * ROW_SUBTILES, LANES), f32),
                        pltpu.SemaphoreType.DMA((2,))],
        compiler_params=_params(),
        name="combine",
    )(run_src, run_n, run_dst, run_src, run_n, run_dst, slots, gates, x1, g_final.reshape(1, -1), ys)


def _plan(meta, counts, t):
    nt = t // ROW_TILE
    n_tiles = (t * TOP_K + N_EXPERTS * (MOE_TILE - 1)) // MOE_TILE
    counts = counts[0, :N_EXPERTS]
    tiles_e = (counts + MOE_TILE - 1) // MOE_TILE
    tile_end = jnp.cumsum(tiles_e)
    tile_start = tile_end - tiles_e
    n_used = tile_end[-1]
    first_row = jnp.pad(tile_start * MOE_TILE, (0, LANES - N_EXPERTS))
    meta = meta.reshape(nt, 8, LANES)
    run_dst = meta[:, 0:1, :] + first_row[None, None, :]
    run_n = meta[:, 1:2, :]
    run_src = meta[:, 2:3, :]
    tile_ids = jnp.arange(n_tiles, dtype=jnp.int32)
    live = jnp.minimum(tile_ids, n_used - 1)
    tile_expert = jnp.sum(tile_end[None, :] <= live[:, None], axis=1).astype(jnp.int32)
    ids = jnp.arange(N_EXPERTS, dtype=jnp.int32)
    is_expert = tile_expert[:, None] == ids[None, :]
    of_tile = lambda per_expert: jnp.sum(jnp.where(is_expert, per_expert[None, :], 0), axis=1)
    in_expert = tile_ids - of_tile(tile_start)
    n_valid = jnp.clip(of_tile(counts) - in_expert * MOE_TILE, 0, MOE_TILE)
    n_valid = jnp.where(tile_ids < n_used, n_valid, 0).astype(jnp.int32)
    tile_first = jnp.logical_and(in_expert == 0, tile_ids < n_used).astype(jnp.int32)
    run_n_flat = run_n[:, 0, :N_EXPERTS].reshape(-1)
    run_loc_flat = run_src[:, 0, :N_EXPERTS].reshape(-1)
    used = tiles_e > 0
    later_used = jnp.where(jnp.logical_and(used[None, :], ids[None, :] > ids[:, None]), ids[None, :],
                           N_EXPERTS)
    following = jnp.min(later_used, axis=1)
    following = jnp.where(following < N_EXPERTS, following, ids)
    buf_of = (jnp.cumsum(used.astype(jnp.int32)) - 1) % 2
    return (tile_expert, n_used.reshape(1).astype(jnp.int32), of_tile(following).astype(jnp.int32),
            of_tile(buf_of).astype(jnp.int32), n_valid, tile_first, run_n_flat, run_loc_flat,
            run_src, run_n, run_dst)


def kernel(x_prompt, x_sample, cache_k, cache_v, g_mix, w_in, b_in, ln_v_g, ln_v_b, w_sp, b_sp,
           attn_sinks, w_pa, w_pb, w_o, g_ffn, w_router, b_router, w_gu, b_gu, w_dn, b_dn, g_final):
    nb, seq, d = x_prompt.shape
    nsb, nnew, _ = x_sample.shape
    tp, ts = nb * seq, nsb * nnew
    t = tp + ts
    xp = x_prompt.reshape(tp, d)
    xs = x_sample.reshape(ts, d)
    pos = np.concatenate([np.arange(seq), np.tile(PAST_LEN + np.arange(nnew), ROW_TILE // nnew)])
    tables = _rope_tables(pos)
    u, vln, v_sgu, q, k, v, kd, vd, ga, gb = _project(
        xp, xs, g_mix[0], w_in[0], b_in[0], ln_v_g[0], ln_v_b[0], tables, seq)
    x1, h_local, slot_t, gate_t, meta, counts = _mix(
        xp, xs, u, vln, q, kd, vd, cache_k[0], cache_v[0], ga, gb, w_sp[0], b_sp[0], attn_sinks[0],
        w_pa[0], w_pb[0], w_o[0], g_ffn[0], w_router[0], b_router[0], seq)
    nt = t // ROW_TILE
    picks = lambda a: a.reshape(nt, 8, ROW_TILE)[:, :TOP_K, :].reshape(nt, 1, TOP_K * ROW_TILE)
    slots, gates = picks(slot_t), picks(gate_t)
    (tile_expert, n_used, next_expert, weight_buf, n_valid, tile_first, run_n_flat, run_loc_flat,
     run_src, run_n, run_dst) = _plan(meta, counts, t)
    y_sorted = _experts(tile_expert, n_used, next_expert, weight_buf, n_valid, tile_first, run_n_flat,
                        run_loc_flat, h_local, w_gu[0], b_gu[0], w_dn[0], b_dn[0])
    y_p, y_s = _combine(run_src, run_n, run_dst, slots, gates, x1, y_sorted, g_final, tp)

    keep = min(WINDOW, seq)
    tails = lambda a: jnp.stack([a[(b + 1) * seq - keep:(b + 1) * seq] for b in range(nb)]).reshape(
        nb, keep, N_KV_HEADS, HEAD_DIM)
    kp, vp = tails(k), tails(v)
    ks = k[tp:].reshape(nsb, nnew, N_KV_HEADS, HEAD_DIM)
    vs = v[tp:].reshape(nsb, nnew, N_KV_HEADS, HEAD_DIM)
    return (y_p.reshape(nb, seq, d), y_s.reshape(nsb, nnew, d), kp[None], vp[None], ks[None],
            vs[None], v_sgu.reshape(1, nsb, nnew, SGU_WIDTH))
```

```python
import functools

import numpy as np
import jax
import jax.numpy as jnp
from jax import lax
from jax.experimental import pallas as pl
from jax.experimental.pallas import tpu as pltpu

D_MODEL = 1024
PAST_LEN = 2048
CHUNK = 64
SGU_CHUNK = 128
SGU_GROUPS = 8
SGU_WIDTH = 1024
N_HEADS = 16
N_KV_HEADS = 4
HEAD_DIM = 64
Q_PER_KV = N_HEADS // N_KV_HEADS
WINDOW = 128
ROT_DIM = HEAD_DIM // 4
ROPE_THETA = 500000.0
ATT_W = N_HEADS * HEAD_DIM
KV_W = N_KV_HEADS * HEAD_DIM
N_EXPERTS = 32
TOP_K = 4
D_FF = 1024
SWIGLU_ALPHA = 1.702
SWIGLU_LIMIT = 7.0
NORM_EPS = 1e-5
NEG_INF = -1e30
N_IN = SGU_WIDTH * 2 + ATT_W + KV_W * 2 + D_MODEL * 2

LANES = 128
ROW_TILE = 256
MOE_TILE = 512
ROW_UNROLL = 16
COL_BLOCK = 256
KV_DUP_W = N_KV_HEADS * LANES
KEY_SPAN = WINDOW + CHUNK
VMEM_LIMIT = 56 * 1024 * 1024

_SQRT_HALF = 0.7071067811865476
_LOG2_E = 1.4426950408889634


def _gelu(x):
    t = 1.0 / (1.0 + (0.3275911 * _SQRT_HALF) * jnp.abs(x))
    half_poly = t * (0.127414796 + t * (-0.142248368 + t * (0.7107068705
                     + t * (-0.7265760135 + t * 0.5307027145))))
    half_tail = x * (half_poly * jnp.exp2(x * x * (-0.5 * _LOG2_E)))
    return jnp.where(x >= 0.0, x - half_tail, half_tail)


def _sigmoid(x):
    return 1.0 / (1.0 + jnp.exp(-x))


def _bf16(x):
    return x.astype(jnp.bfloat16)


def _dot(a, b):
    return jnp.dot(a, b, preferred_element_type=jnp.float32)


ROW_SUBTILES = D_MODEL // LANES


def _store_row_tiled(ref, lead, x):
    rows = x.shape[0]
    for s in range(ROW_SUBTILES):
        ref[(*lead, pl.ds(s, rows, stride=ROW_SUBTILES), slice(None))] = x[:, s * LANES:(s + 1) * LANES]


def _load_row_tiled(ref, lead, rows):
    return jnp.concatenate(
        [ref[(*lead, pl.ds(s, rows, stride=ROW_SUBTILES), slice(None))] for s in range(ROW_SUBTILES)],
        axis=1)


def _dot_nt(a, b):
    return lax.dot_general(a, b, (((1,), (1,)), ((), ())), preferred_element_type=jnp.float32)


def _rms(x, g):
    return x * lax.rsqrt(jnp.mean(x * x, axis=-1, keepdims=True) + NORM_EPS) * g


def _lane_lo(rows):
    return lax.broadcasted_iota(jnp.int32, (rows, LANES), 1) < HEAD_DIM


def _dup_heads(kv):
    rows = kv.shape[0]
    lo = _lane_lo(rows)
    out = []
    for j in range(KV_W // LANES):
        blk = kv[:, j * LANES:(j + 1) * LANES]
        swp = pltpu.roll(blk, HEAD_DIM, axis=1)
        out.append(jnp.where(lo, blk, swp))
        out.append(jnp.where(lo, swp, blk))
    return _bf16(jnp.concatenate(out, axis=1))


def _rope_block(zb, cos_b, sin_lo, sin_hi):
    up = pltpu.roll(zb, LANES - ROT_DIM // 2, axis=1)
    dn = pltpu.roll(zb, ROT_DIM // 2, axis=1)
    return zb * cos_b + up * sin_lo + dn * sin_hi


def _proj_kernel(n_prompt_tiles, xp_ref, xs_ref, gmix_ref, w_ref, b_ref, lng_ref, lnb_ref,
                 cos_ref, slo_ref, shi_ref,
                 u_ref, vln_ref, vs_ref, q_ref, k_ref, v_ref, kd_ref, vd_ref, ga_ref, gb_ref):
    i = pl.program_id(0)
    x = jnp.where(i < n_prompt_tiles, xp_ref[...], xs_ref[...])
    h = _bf16(_rms(x, gmix_ref[...]))

    cos_b, sin_lo, sin_hi = cos_ref[...], slo_ref[...], shi_ref[...]
    off_u, off_v, off_q, off_k, off_vv, off_ga, off_gb = (
        int(o) for o in np.cumsum((0, SGU_WIDTH, SGU_WIDTH, ATT_W, KV_W, KV_W, D_MODEL)))

    def z(lo):
        return _dot(h, w_ref[:, lo:lo + COL_BLOCK]) + b_ref[:, lo:lo + COL_BLOCK]

    def cols(j):
        return slice(j * COL_BLOCK, (j + 1) * COL_BLOCK)

    def rope(zb):
        return jnp.concatenate(
            [_rope_block(zb[:, t * LANES:(t + 1) * LANES], cos_b, sin_lo, sin_hi)
             for t in range(COL_BLOCK // LANES)], axis=1)

    gelu_v = []

    def do_u(j):
        u_ref[:, cols(j)] = _bf16(z(off_u + j * COL_BLOCK))

    def do_v(j):
        gelu_v.append(_gelu(z(off_v + j * COL_BLOCK)))

    def do_q(j):
        q_ref[:, cols(j)] = _bf16(rope(z(off_q + j * COL_BLOCK)) * (HEAD_DIM ** -0.5))

    def do_k(j):
        kr = rope(z(off_k))
        k_ref[...] = kr
        kd_ref[...] = _dup_heads(kr)

    def do_vv(j):
        zv = z(off_vv)
        v_ref[...] = zv
        vd_ref[...] = _dup_heads(zv)

    def do_ga(j):
        ga_ref[:, cols(j)] = _bf16(z(off_ga + j * COL_BLOCK))

    def do_gb(j):
        gb_ref[:, cols(j)] = _bf16(z(off_gb + j * COL_BLOCK))

    order = ((do_v, 0), (do_q, 0), (do_q, 1), (do_u, 0), (do_v, 1), (do_q, 2), (do_q, 3), (do_u, 1),
             (do_v, 2), (do_k, 0), (do_vv, 0), (do_u, 2), (do_v, 3), (do_ga, 0), (do_ga, 1), (do_u, 3),
             (do_ga, 2), (do_ga, 3), (do_gb, 0), (do_gb, 1), (do_gb, 2), (do_gb, 3))
    for fn, j in order:
        fn(j)
    gv = jnp.concatenate(gelu_v, axis=1)
    gc = gv - jnp.mean(gv, axis=-1, keepdims=True)
    var = jnp.mean(gc * gc, axis=-1, keepdims=True)
    vln = gc * lax.rsqrt(var + NORM_EPS) * lng_ref[...] + lnb_ref[...]
    vln_ref[...] = _bf16(vln)
    vs_ref[...] = vln


def _rope_tables(pos):
    half = ROT_DIM // 2
    inv = np.float32(ROPE_THETA) ** (-np.arange(half, dtype=np.float32) * np.float32(2.0) / ROT_DIM)
    ang = pos.astype(np.float32)[:, None] * inv.astype(np.float32)[None, :]
    cos = np.cos(ang.astype(np.float64)).astype(np.float32)
    sin = np.sin(ang.astype(np.float64)).astype(np.float32)
    n = pos.shape[0]
    ones = np.ones((n, HEAD_DIM - ROT_DIM), np.float32)
    zeros = np.zeros((n, HEAD_DIM - ROT_DIM), np.float32)
    zh = np.zeros((n, half), np.float32)
    cos_h = np.concatenate([cos, cos, ones], axis=1)
    slo_h = np.concatenate([-sin, zh, zeros], axis=1)
    shi_h = np.concatenate([zh, sin, zeros], axis=1)
    rep = LANES // HEAD_DIM
    return tuple(jnp.asarray(np.tile(a, (1, rep))) for a in (cos_h, slo_h, shi_h))


def _row_spec(width):
    return pl.BlockSpec((ROW_TILE, width), lambda i: (i, 0))


def _const_spec(shape):
    return pl.BlockSpec(shape, lambda i: (0,) * len(shape))


def _prompt_spec(width, n_prompt_tiles):
    return pl.BlockSpec((ROW_TILE, width), lambda i: (jnp.minimum(i, n_prompt_tiles - 1), 0))


def _sample_spec(width, n_prompt_tiles):
    return pl.BlockSpec((ROW_TILE, width), lambda i: (jnp.maximum(i - n_prompt_tiles, 0), 0))


def _params():
    return pltpu.CompilerParams(dimension_semantics=("arbitrary",), vmem_limit_bytes=VMEM_LIMIT)


def _project(xp, xs, g_mix, w_in, b_in, ln_g, ln_b, tables, seq):
    tp, ts = xp.shape[0], xs.shape[0]
    t = tp + ts
    npt = tp // ROW_TILE
    tiles_per_seq = seq // ROW_TILE
    f32, bf16 = jnp.float32, jnp.bfloat16
    table_spec = pl.BlockSpec(
        (ROW_TILE, LANES), lambda i: (jnp.where(i < npt, i % tiles_per_seq, tiles_per_seq), 0))
    out_shape = (
        jax.ShapeDtypeStruct((t, SGU_WIDTH), bf16),
        jax.ShapeDtypeStruct((t, SGU_WIDTH), bf16),
        jax.ShapeDtypeStruct((ts, SGU_WIDTH), f32),
        jax.ShapeDtypeStruct((t, ATT_W), bf16),
        jax.ShapeDtypeStruct((t, KV_W), f32),
        jax.ShapeDtypeStruct((t, KV_W), f32),
        jax.ShapeDtypeStruct((t, KV_DUP_W), bf16),
        jax.ShapeDtypeStruct((t, KV_DUP_W), bf16),
        jax.ShapeDtypeStruct((t, D_MODEL), bf16),
        jax.ShapeDtypeStruct((t, D_MODEL), bf16),
    )
    return pl.pallas_call(
        functools.partial(_proj_kernel, npt),
        out_shape=out_shape,
        grid=(t // ROW_TILE,),
        in_specs=[
            _prompt_spec(D_MODEL, npt), _sample_spec(D_MODEL, npt),
            _const_spec((1, D_MODEL)), _const_spec((D_MODEL, N_IN)),
            _const_spec((1, N_IN)), _const_spec((1, SGU_WIDTH)), _const_spec((1, SGU_WIDTH)),
            table_spec, table_spec, table_spec,
        ],
        out_specs=(
            _row_spec(SGU_WIDTH), _row_spec(SGU_WIDTH), _sample_spec(SGU_WIDTH, npt),
            _row_spec(ATT_W), _row_spec(KV_W), _row_spec(KV_W), _row_spec(KV_DUP_W),
            _row_spec(KV_DUP_W), _row_spec(D_MODEL), _row_spec(D_MODEL),
        ),
        compiler_params=_params(),
        name="proj",
    )(xp, xs, g_mix.reshape(1, -1), w_in.astype(bf16), b_in.reshape(1, -1),
      ln_g.reshape(1, -1), ln_b.reshape(1, -1), *tables)


def _attend(qa, qb, kwin, vwin, sink, valid):
    lo = _lane_lo(CHUNK)
    zero = jnp.zeros_like(qa)
    lhs = jnp.concatenate([jnp.where(lo, qa, zero), jnp.where(lo, zero, qa),
                           jnp.where(lo, qb, zero), jnp.where(lo, zero, qb)], axis=0)
    s = _dot_nt(lhs, kwin)
    if valid is not None:
        s = jnp.where(valid, s, NEG_INF)
    s_a, s_b = s[:, :LANES], s[:, LANES:]
    tail = s_b.shape[1]
    m = jnp.maximum(jnp.max(s, axis=-1, keepdims=True), sink)
    p_a = jnp.exp(s_a - m)
    p_b = jnp.exp(s_b - m[:, :tail])
    denom = (jnp.sum(jnp.concatenate([p_a, p_b], axis=1), axis=-1, keepdims=True)
             + jnp.exp(sink - m))
    inv = 1.0 / denom
    pn = jnp.concatenate([p_a * inv, p_b * inv[:, :tail]], axis=1)
    r = _dot(_bf16(pn), vwin)
    oa = jnp.where(lo, r[0:CHUNK], r[CHUNK:2 * CHUNK])
    ob = jnp.where(lo, r[2 * CHUNK:3 * CHUNK], r[3 * CHUNK:4 * CHUNK])
    return oa, ob


def _stack_rows(rows):
    ri = lax.broadcasted_iota(jnp.int32, (8, rows[0].shape[1]), 0)
    out = jnp.zeros((8, rows[0].shape[1]), rows[0].dtype)
    for k, row in enumerate(rows):
        out = jnp.where(ri == k, row, out)
    return out


def _route_pick(logits_t):
    rows = logits_t.shape[1]
    eid = lax.broadcasted_iota(jnp.int32, (N_EXPERTS, rows), 0)
    work = logits_t
    vals, idxs = [], []
    for _ in range(TOP_K):
        m = jnp.max(work, axis=0, keepdims=True)
        idx = jnp.min(jnp.where(work == m, eid, N_EXPERTS), axis=0, keepdims=True)
        vals.append(m)
        idxs.append(idx)
        work = jnp.where(eid == idx, -jnp.inf, work)
    exps = [jnp.exp(v - vals[0]) for v in vals]
    inv = 1.0 / (exps[0] + exps[1] + exps[2] + exps[3])
    gates = _stack_rows([e * inv for e in exps])

    picked = jnp.zeros((N_EXPERTS, rows), jnp.float32)
    for idx in idxs:
        picked = jnp.where(eid == idx, 1.0, picked)
    tr = lax.broadcasted_iota(jnp.int32, (rows, rows), 0)
    tc = lax.broadcasted_iota(jnp.int32, (rows, rows), 1)
    earlier = _bf16(jnp.where(tr < tc, 1.0, 0.0))
    in_tile = _dot(_bf16(picked), earlier)
    count_col = jnp.broadcast_to(jnp.sum(picked, axis=1, keepdims=True), (N_EXPERTS, LANES))
    eid_wide = lax.broadcasted_iota(jnp.int32, (LANES, rows), 0)
    picked_wide = jnp.zeros((LANES, rows), jnp.float32)
    for idx in idxs:
        picked_wide = jnp.where(eid_wide == idx, 1.0, picked_wide)
    count_row = _dot_nt(jnp.ones((8, rows), jnp.bfloat16), _bf16(picked_wide))
    return idxs, gates, in_tile, count_col, count_row


def _route_place(idxs, in_tile, count_col, count_row, carry_s, live):
    rows = in_tile.shape[1]
    eid = lax.broadcasted_iota(jnp.int32, (N_EXPERTS, rows), 0)
    er = lax.broadcasted_iota(jnp.int32, (N_EXPERTS, N_EXPERTS), 0)
    ec = lax.broadcasted_iota(jnp.int32, (N_EXPERTS, N_EXPERTS), 1)
    start_col = _dot(_bf16(jnp.where(ec < er, 1.0, 0.0)), _bf16(count_col))
    local = in_tile + jnp.concatenate([start_col] * (rows // LANES), axis=1)
    slots = _stack_rows([jnp.sum(jnp.where(eid == idx, local, 0.0), axis=0, keepdims=True)
                         for idx in idxs]).astype(jnp.int32)
    lr = lax.broadcasted_iota(jnp.int32, (LANES, LANES), 0)
    lc = lax.broadcasted_iota(jnp.int32, (LANES, LANES), 1)
    start_row = _dot(_bf16(count_row), _bf16(jnp.where(lr < lc, 1.0, 0.0)))
    ri = lax.broadcasted_iota(jnp.int32, (8, LANES), 0)
    meta = jnp.where(ri == 0, carry_s[...], jnp.where(ri == 1, count_row, jnp.where(ri == 2, start_row, 0.0)))
    carry_s[...] = carry_s[...] + count_row * live
    return slots, meta.astype(jnp.int32)


def _mix_kernel(tiles_per_seq, n_prompt_tiles,
                xp_ref, xs_ref, u_ref, vln_ref, q_ref, kd_ref, vd_ref, kdp_ref, vdp_ref,
                ck_ref, cv_ref, ga_ref, gb_ref, wsp_ref, bsp_ref, sink_ref,
                wpa_ref, wpb_ref, wo_ref, gffn_ref, wrh_ref, wrl_ref, br_ref,
                x1_ref, hloc_ref, slot_ref, gate_ref, meta_ref, count_ref,
                a_s, o_s, kwin_s, vwin_s, carry_s, hhi_s, hlo_s):
    i = pl.program_id(0)
    n_streams = ROW_TILE // CHUNK

    @pl.when(i == 0)
    def _():
        carry_s[...] = jnp.zeros_like(carry_s)
        hhi_s[...] = jnp.zeros_like(hhi_s)
        hlo_s[...] = jnp.zeros_like(hlo_s)

    def sgu_rows(r0, rows):
        ri = lax.broadcasted_iota(jnp.int32, (rows, rows), 0) // CHUNK
        ci = lax.broadcasted_iota(jnp.int32, (rows, rows), 1) // CHUNK
        for g in range(SGU_GROUPS):
            cols = slice(g * LANES, (g + 1) * LANES)
            w = _bf16(jnp.where(ci <= ri, wsp_ref[g, :rows, :rows], 0.0))
            sp = _dot(w, vln_ref[r0:r0 + rows, cols]) + bsp_ref[g, :rows, :]
            a_s[r0:r0 + rows, cols] = _bf16(_gelu(u_ref[r0:r0 + rows, cols].astype(jnp.float32)) * sp)

    def attend_rows(r0, kwin_of, valid):
        for g in range(N_KV_HEADS):
            c0 = g * Q_PER_KV * HEAD_DIM
            kwin, vwin = kwin_of(g)
            oa, ob = _attend(q_ref[r0:r0 + CHUNK, c0:c0 + LANES],
                             q_ref[r0:r0 + CHUNK, c0 + LANES:c0 + 2 * LANES],
                             kwin, vwin, sink_ref[g], valid)
            o_s[r0:r0 + CHUNK, c0:c0 + LANES] = _bf16(oa)
            o_s[r0:r0 + CHUNK, c0 + LANES:c0 + 2 * LANES] = _bf16(ob)

    @pl.when(i < n_prompt_tiles)
    def _prompt():
        for c in range(ROW_TILE // SGU_CHUNK):
            sgu_rows(c * SGU_CHUNK, SGU_CHUNK)
        kwin_s[0:WINDOW] = kdp_ref[...]
        kwin_s[WINDOW:WINDOW + ROW_TILE] = kd_ref[...]
        vwin_s[0:WINDOW] = vdp_ref[...]
        vwin_s[WINDOW:WINDOW + ROW_TILE] = vd_ref[...]
        first = (i % tiles_per_seq) == 0
        col = lax.broadcasted_iota(jnp.int32, (1, KEY_SPAN), 1)
        for j in range(ROW_TILE // CHUNK):
            r0 = j * CHUNK
            valid = jnp.logical_or(jnp.logical_not(first), col + r0 >= WINDOW) if r0 < WINDOW else None

            def kwin_of(g, r0=r0):
                cols = slice(g * LANES, (g + 1) * LANES)
                return kwin_s[r0:r0 + KEY_SPAN, cols], vwin_s[r0:r0 + KEY_SPAN, cols]

            attend_rows(r0, kwin_of, valid)

    @pl.when(i >= n_prompt_tiles)
    def _sample():
        for s in range(n_streams):
            r0 = s * CHUNK
            sgu_rows(r0, CHUNK)
            kwin_s[0:WINDOW] = _dup_heads(ck_ref[s])
            kwin_s[WINDOW:KEY_SPAN] = kd_ref[r0:r0 + CHUNK]
            vwin_s[0:WINDOW] = _dup_heads(cv_ref[s])
            vwin_s[WINDOW:KEY_SPAN] = vd_ref[r0:r0 + CHUNK]

            def kwin_of(g):
                cols = slice(g * LANES, (g + 1) * LANES)
                return kwin_s[0:KEY_SPAN, cols], vwin_s[0:KEY_SPAN, cols]

            attend_rows(r0, kwin_of, None)

    hh, hl = hhi_s[...], hlo_s[...]
    logits_t = (_dot_nt(wrh_ref[...], hh) + _dot_nt(wrl_ref[...], hh) + _dot_nt(wrh_ref[...], hl)
                + jnp.concatenate([br_ref[...]] * (ROW_TILE // LANES), axis=1))
    m_a = _sigmoid(ga_ref[...].astype(jnp.float32)) * _dot(a_s[...], _bf16(wpa_ref[...]))
    idxs, gates, in_tile, count_col, count_row = _route_pick(logits_t)
    gate_ref[...] = gates
    m = m_a + _sigmoid(gb_ref[...].astype(jnp.float32)) * _dot(o_s[...], _bf16(wpb_ref[...]))
    slots, meta = _route_place(idxs, in_tile, count_col, count_row, carry_s, jnp.where(i > 0, 1.0, 0.0))
    half = ((i + 1) % 2) * (ROW_TILE * TOP_K * ROW_SUBTILES)
    slot_ref[...] = slots * ROW_SUBTILES + half
    meta_ref[...] = meta
    count_ref[...] = carry_s[...].astype(jnp.int32)
    n_slots = ROW_TILE * TOP_K
    sid = lax.broadcasted_iota(jnp.int32, (n_slots, ROW_TILE), 0)
    place = jnp.zeros((n_slots, ROW_TILE), jnp.float32)
    for k in range(TOP_K):
        place = jnp.where(sid == slots[k:k + 1, :], 1.0, place)
    _store_row_tiled(hloc_ref, (), _dot(_bf16(place), hh))

    x = jnp.where(i < n_prompt_tiles, xp_ref[...], xs_ref[...])
    x1 = x + _dot(_bf16(m), _bf16(wo_ref[...]))
    x1_ref[...] = x1
    h2 = _rms(x1, gffn_ref[...])
    h2_hi = _bf16(h2)
    hhi_s[...] = h2_hi
    hlo_s[...] = _bf16(h2 - h2_hi.astype(jnp.float32))


def _mix(xp, xs, u, vln, q, kd, vd, cache_k, cache_v, ga, gb, w_sp, b_sp, sinks,
         w_pa, w_pb, w_o, g_ffn, w_router, b_router, seq):
    tp, ts = xp.shape[0], xs.shape[0]
    t = tp + ts
    npt = tp // ROW_TILE
    tiles_per_seq = seq // ROW_TILE
    f32, bf16 = jnp.float32, jnp.bfloat16
    n_streams = ROW_TILE // CHUNK
    win_per_tile = ROW_TILE // WINDOW

    nt = t // ROW_TILE
    cur = lambda i: jnp.minimum(i, nt - 1)
    smp = lambda i: jnp.maximum(cur(i) - npt, 0)
    row = lambda width: pl.BlockSpec((ROW_TILE, width), lambda i: (cur(i), 0))
    prev_spec = pl.BlockSpec(
        (WINDOW, KV_DUP_W), lambda i: (jnp.maximum(jnp.minimum(i, npt - 1) * win_per_tile - 1, 0), 0))
    cache_spec = pl.BlockSpec((n_streams, WINDOW, KV_W), lambda i: (smp(i), 0, 0))
    xs_spec = pl.BlockSpec((ROW_TILE, D_MODEL), lambda i: (smp(i), 0))
    sink_cols = jnp.broadcast_to(
        jnp.repeat(sinks.astype(f32).reshape(N_KV_HEADS, Q_PER_KV), CHUNK, axis=1)[:, :, None],
        (N_KV_HEADS, Q_PER_KV * CHUNK, LANES))
    wr_t = w_router.T
    wr_hi = wr_t.astype(bf16)
    wr_lo = (wr_t - wr_hi.astype(f32)).astype(bf16)
    routed8 = lambda width: pl.BlockSpec((8, width), lambda i: (jnp.maximum(i - 1, 0), 0))
    out_shape = (
        jax.ShapeDtypeStruct((t, D_MODEL), f32),
        jax.ShapeDtypeStruct((t * TOP_K * ROW_SUBTILES, LANES), f32),
        jax.ShapeDtypeStruct((nt * 8, ROW_TILE), jnp.int32),
        jax.ShapeDtypeStruct((nt * 8, ROW_TILE), f32),
        jax.ShapeDtypeStruct((nt * 8, LANES), jnp.int32),
        jax.ShapeDtypeStruct((8, LANES), jnp.int32),
    )
    return pl.pallas_call(
        functools.partial(_mix_kernel, tiles_per_seq, npt),
        out_shape=out_shape,
        grid=(nt + 1,),
        in_specs=[
            _prompt_spec(D_MODEL, npt), xs_spec,
            row(SGU_WIDTH), row(SGU_WIDTH), row(ATT_W),
            row(KV_DUP_W), row(KV_DUP_W), prev_spec, prev_spec,
            cache_spec, cache_spec, row(D_MODEL), row(D_MODEL),
            _const_spec((SGU_GROUPS, SGU_CHUNK, SGU_CHUNK)), _const_spec((SGU_GROUPS, SGU_CHUNK, LANES)),
            _const_spec((N_KV_HEADS, Q_PER_KV * CHUNK, LANES)),
            _const_spec((SGU_WIDTH, D_MODEL)), _const_spec((ATT_W, D_MODEL)),
            _const_spec((D_MODEL, D_MODEL)), _const_spec((1, D_MODEL)),
            _const_spec((N_EXPERTS, D_MODEL)), _const_spec((N_EXPERTS, D_MODEL)),
            _const_spec((N_EXPERTS, LANES)),
        ],
        out_specs=(row(D_MODEL),
                   pl.BlockSpec((ROW_TILE * TOP_K * ROW_SUBTILES, LANES),
                                lambda i: (jnp.maximum(i - 1, 0), 0)),
                   routed8(ROW_TILE), routed8(ROW_TILE), routed8(LANES), _const_spec((8, LANES))),
        scratch_shapes=[
            pltpu.VMEM((ROW_TILE, SGU_WIDTH), bf16), pltpu.VMEM((ROW_TILE, ATT_W), bf16),
            pltpu.VMEM((WINDOW + ROW_TILE, KV_DUP_W), bf16),
            pltpu.VMEM((WINDOW + ROW_TILE, KV_DUP_W), bf16),
            pltpu.VMEM((8, LANES), f32),
            pltpu.VMEM((ROW_TILE, D_MODEL), bf16), pltpu.VMEM((ROW_TILE, D_MODEL), bf16),
        ],
        compiler_params=_params(),
        name="mix",
    )(xp, xs, u, vln, q, kd, vd, kd, vd,
      cache_k.reshape(-1, WINDOW, KV_W), cache_v.reshape(-1, WINDOW, KV_W), ga, gb,
      w_sp, jnp.broadcast_to(b_sp[:, :, None], (SGU_GROUPS, SGU_CHUNK, LANES)), sink_cols,
      w_pa, w_pb, w_o,
      g_ffn.reshape(1, -1), wr_hi, wr_lo,
      jnp.broadcast_to(b_router.astype(f32)[:, None], (N_EXPERTS, LANES)))


def _unrolled_rows(n_rows, fn):
    if isinstance(n_rows, int):
        groups, tail_start = n_rows // ROW_UNROLL, n_rows - n_rows % ROW_UNROLL
    else:
        groups = lax.shift_right_logical(n_rows, ROW_UNROLL.bit_length() - 1)
        tail_start = groups * ROW_UNROLL

    def group(gi, carry):
        for lane in range(ROW_UNROLL):
            fn(gi * ROW_UNROLL + lane, lane)
        return carry

    def tail(r, carry):
        fn(r, 0)
        return carry

    lax.fori_loop(0, groups, group, 0)
    lax.fori_loop(tail_start, n_rows, tail, 0)


def _row_span(first_row, n_rows):
    return pl.ds(pl.multiple_of(first_row * ROW_SUBTILES, ROW_SUBTILES),
                 pl.multiple_of(n_rows * ROW_SUBTILES, ROW_SUBTILES))


def _run_spec(index_of):
    return pl.BlockSpec((1, 1, LANES), lambda i, *_: (index_of(i), 0, 0), memory_space=pltpu.SMEM)


def _expert_kernel(n_token_tiles,
                   te_ref, nu_ref, nx_ref, par_ref, nv_ref, tf_ref, cnt_ref, loc_ref,
                   bgu_ref, bdn_ref, hloc_hbm, wgu_hbm, wdn_hbm,
                   ys_hbm,
                   xbuf, ybuf, zbuf, wgu_f, wdn_f, walk, xsem, ysem, zsem, wsem):
    i = pl.program_id(0)
    n_used = nu_ref[0]
    expert = te_ref[i]
    buf = par_ref[i]
    slot = i % 2
    expert_changed = jnp.logical_or(i == 0, expert != te_ref[jnp.maximum(i - 1, 0)])

    def fetch_rows(j, b):
        e = te_ref[j]
        need = nv_ref[j]

        @pl.when(tf_ref[j] == 1)
        def _():
            walk[0] = 0
            walk[1] = 0

        @pl.when(need < MOE_TILE)
        def _():
            xbuf[b] = jnp.zeros(xbuf.shape[1:], xbuf.dtype)

        def unfinished(state):
            filled, tile, _ = state
            return jnp.logical_and(filled < need, tile < n_token_tiles)

        def take_run(state):
            filled, tile, off = state
            run = cnt_ref[tile * N_EXPERTS + e]
            take = jnp.minimum(run - off, need - filled)

            @pl.when(take > 0)
            def _():
                src = tile * (ROW_TILE * TOP_K) + loc_ref[tile * N_EXPERTS + e] + off
                pltpu.make_async_copy(hloc_hbm.at[_row_span(src, take)],
                                      xbuf.at[b, _row_span(filled, take)], xsem.at[b]).start()

            run_done = off + take == run
            return (filled + take, jnp.where(run_done, tile + 1, tile), jnp.where(run_done, 0, off + take))

        _, tile, off = lax.while_loop(unfinished, take_run, (jnp.int32(0), walk[0], walk[1]))
        walk[0] = tile
        walk[1] = off

    @pl.when(jnp.logical_and(i == 0, n_used > 0))
    def _():
        fetch_rows(0, 0)

    @pl.when(i + 1 < n_used)
    def _():
        fetch_rows(i + 1, 1 - slot)

    def weight_copies(e, b):
        return (pltpu.make_async_copy(wgu_hbm.at[e], wgu_f.at[b], wsem.at[0, b]),
                pltpu.make_async_copy(wdn_hbm.at[e], wdn_f.at[b], wsem.at[1, b]))

    @pl.when(jnp.logical_and(i < n_used, expert_changed))
    def _():
        @pl.when(i == 0)
        def _():
            for copy in weight_copies(expert, buf):
                copy.start()

        for copy in weight_copies(expert, buf):
            copy.wait()
        following = nx_ref[i]

        @pl.when(following != expert)
        def _():
            for copy in weight_copies(following, 1 - buf):
                copy.start()


    rows = nv_ref[i]

    @pl.when(i < n_used)
    def _():
        pltpu.make_async_copy(hloc_hbm.at[_row_span(0, rows)], xbuf.at[slot, _row_span(0, rows)],
                              xsem.at[slot]).wait()

    def mlp(n):
        x = _bf16(_load_row_tiled(xbuf, (slot,), n))
        gu = _dot(x, _bf16(wgu_f[buf])) + bgu_ref[0]
        gate = jnp.minimum(gu[:, :D_FF], SWIGLU_LIMIT)
        lin = jnp.clip(gu[:, D_FF:], -SWIGLU_LIMIT, SWIGLU_LIMIT)
        act = gate * _sigmoid(SWIGLU_ALPHA * gate) * (lin + 1.0)
        _store_row_tiled(ybuf, (slot,), _dot(_bf16(act), _bf16(wdn_f[buf])) + bdn_ref[0])

    last = pl.num_programs(0) - 1
    tile_rows = ybuf.shape[1]

    def out_copy(j, b):
        return pltpu.make_async_copy(
            ybuf.at[b], ys_hbm.at[pl.ds(pl.multiple_of(j * tile_rows, tile_rows), tile_rows)], ysem.at[b])

    def unused_tiles(act_on):
        def body(j, carry):
            act_on(pltpu.make_async_copy(
                zbuf, ys_hbm.at[pl.ds(pl.multiple_of(j * tile_rows, tile_rows), tile_rows)], zsem))
            return carry
        lax.fori_loop(n_used, last + 1, body, 0)

    @pl.when(i == 0)
    def _():
        zbuf[...] = jnp.zeros_like(zbuf)
        unused_tiles(lambda copy: copy.start())

    @pl.when(jnp.logical_and(i >= 2, i - 2 < n_used))
    def _():
        out_copy(i - 2, slot).wait()

    @pl.when(jnp.logical_and(i < n_used, rows > MOE_TILE // 2))
    def _():
        mlp(MOE_TILE)

    @pl.when(jnp.logical_and(i < n_used, rows <= MOE_TILE // 2))
    def _():
        mlp(MOE_TILE // 2)
        ybuf[slot, pl.ds(MOE_TILE // 2 * ROW_SUBTILES, MOE_TILE // 2 * ROW_SUBTILES), :] = jnp.zeros(
            (MOE_TILE // 2 * ROW_SUBTILES, LANES), ybuf.dtype)

    @pl.when(i < n_used)
    def _():
        out_copy(i, slot).start()

    @pl.when(i == last)
    def _():
        @pl.when(jnp.logical_and(i >= 1, i - 1 < n_used))
        def _():
            out_copy(i - 1, 1 - slot).wait()

        @pl.when(i < n_used)
        def _():
            out_copy(i, slot).wait()

        unused_tiles(lambda copy: copy.wait())


def _experts(tile_expert, n_used, next_expert, weight_buf, n_valid, tile_first, run_n, run_loc,
             h_local, w_gu, b_gu, w_dn, b_dn):
    n_tiles = tile_expert.shape[0]
    n_token_tiles = h_local.shape[0] // (ROW_TILE * TOP_K * ROW_SUBTILES)
    f32, bf16 = jnp.float32, jnp.bfloat16
    tile_rows = MOE_TILE * ROW_SUBTILES
    grid_spec = pltpu.PrefetchScalarGridSpec(
        num_scalar_prefetch=8,
        grid=(n_tiles,),
        in_specs=[
            pl.BlockSpec((1, 1, 2 * D_FF), lambda i, te, *_: (te[i], 0, 0)),
            pl.BlockSpec((1, 1, D_MODEL), lambda i, te, *_: (te[i], 0, 0)),
            pl.BlockSpec(memory_space=pl.ANY), pl.BlockSpec(memory_space=pl.ANY),
            pl.BlockSpec(memory_space=pl.ANY),
        ],
        out_specs=pl.BlockSpec(memory_space=pl.ANY),
        scratch_shapes=[
            pltpu.VMEM((2, tile_rows, LANES), f32), pltpu.VMEM((2, tile_rows, LANES), f32),
            pltpu.VMEM((tile_rows, LANES), f32),
            pltpu.VMEM((2, D_MODEL, 2 * D_FF), f32), pltpu.VMEM((2, D_FF, D_MODEL), f32),
            pltpu.SMEM((2,), jnp.int32),
            pltpu.SemaphoreType.DMA((2,)), pltpu.SemaphoreType.DMA((2,)), pltpu.SemaphoreType.DMA,
            pltpu.SemaphoreType.DMA((2, 2)),
        ],
    )
    return pl.pallas_call(
        functools.partial(_expert_kernel, n_token_tiles),
        out_shape=jax.ShapeDtypeStruct((n_tiles * tile_rows, LANES), f32),
        grid_spec=grid_spec,
        compiler_params=_params(),
        name="experts",
    )(tile_expert, n_used, next_expert, weight_buf, n_valid, tile_first, run_n, run_loc,
      b_gu.reshape(N_EXPERTS, 1, -1), b_dn.reshape(N_EXPERTS, 1, -1), h_local, w_gu, w_dn)


def _combine_kernel(n_prompt_tiles,
                    src_ref, n_ref, dst_ref, src_nx_ref, n_nx_ref, dst_nx_ref, slot_ref, gate_ref,
                    x1_ref, gfin_ref, ys_hbm,
                    yp_ref, yo_ref,
                    local, mixed, run_sem):
    i = pl.program_id(0)
    last = pl.num_programs(0) - 1
    buf = i % 2
    half_slots = ROW_TILE * TOP_K
    half_rows = half_slots * ROW_SUBTILES

    def fetch_runs(s_ref, c_ref, d_ref, b):
        for e in range(N_EXPERTS):
            n = c_ref[0, 0, e]
            copy = pltpu.make_async_copy(ys_hbm.at[_row_span(d_ref[0, 0, e], n)],
                                         local.at[_row_span(b * half_slots + s_ref[0, 0, e], n)],
                                         run_sem.at[b])
            pl.when(n > 0)(copy.start)

    @pl.when(i == 0)
    def _():
        fetch_runs(src_ref, n_ref, dst_ref, 0)

    @pl.when(i < last)
    def _():
        fetch_runs(src_nx_ref, n_nx_ref, dst_nx_ref, 1 - buf)

    pltpu.make_async_copy(ys_hbm.at[pl.ds(0, half_rows)],
                          local.at[pl.ds(pl.multiple_of(buf * half_rows, half_rows), half_rows)],
                          run_sem.at[buf]).wait()

    def blend(t, lane):
        acc = None
        for k in range(TOP_K):
            at = pl.multiple_of(slot_ref[0, 0, k * ROW_TILE + t], ROW_SUBTILES)
            term = gate_ref[0, 0, k * ROW_TILE + t] * local[pl.ds(at, ROW_SUBTILES), :]
            acc = term if acc is None else acc + term
        mixed[pl.ds(pl.multiple_of(t * ROW_SUBTILES, ROW_SUBTILES), ROW_SUBTILES), :] = acc
    _unrolled_rows(ROW_TILE, blend)

    out = _rms(x1_ref[...] + _load_row_tiled(mixed, (), ROW_TILE), gfin_ref[...])

    @pl.when(i < n_prompt_tiles)
    def _():
        yp_ref[...] = out

    @pl.when(i >= n_prompt_tiles)
    def _():
        yo_ref[...] = out


def _combine(run_src, run_n, run_dst, slots, gates, x1, ys, g_final, tp):
    t = x1.shape[0]
    npt = tp // ROW_TILE
    nt = t // ROW_TILE
    f32 = jnp.float32
    picks = ROW_TILE * TOP_K
    nxt = lambda i: jnp.minimum(i + 1, nt - 1)
    pick_spec = pl.BlockSpec((1, 1, picks), lambda i: (i, 0, 0), memory_space=pltpu.SMEM)
    return pl.pallas_call(
        functools.partial(_combine_kernel, npt),
        out_shape=(jax.ShapeDtypeStruct((tp, D_MODEL), f32),
                   jax.ShapeDtypeStruct((t - tp, D_MODEL), f32)),
        grid=(nt,),
        in_specs=[_run_spec(lambda i: i), _run_spec(lambda i: i), _run_spec(lambda i: i),
                  _run_spec(nxt), _run_spec(nxt), _run_spec(nxt), pick_spec, pick_spec,
                  _row_spec(D_MODEL), _const_spec((1, D_MODEL)), pl.BlockSpec(memory_space=pl.ANY)],
        out_specs=(_prompt_spec(D_MODEL, npt), _sample_spec(D_MODEL, npt)),
        scratch_shapes=[pltpu.VMEM((2 * picks * ROW_SUBTILES, LANES), f32),
                        pltpu.VMEM((ROW_TILE * ROW_SUBTILES, LANES), f32),
                        pltpu.SemaphoreType.DMA((2,))],
        compiler_params=_params(),
        name="combine",
    )(run_src, run_n, run_dst, run_src, run_n, run_dst, slots, gates, x1, g_final.reshape(1, -1), ys)


def _plan(meta, counts, t):
    nt = t // ROW_TILE
    n_tiles = (t * TOP_K + N_EXPERTS * (MOE_TILE - 1)) // MOE_TILE
    counts = counts[0, :N_EXPERTS]
    tiles_e = (counts + MOE_TILE - 1) // MOE_TILE
    tile_end = jnp.cumsum(tiles_e)
    tile_start = tile_end - tiles_e
    n_used = tile_end[-1]
    first_row = jnp.pad(tile_start * MOE_TILE, (0, LANES - N_EXPERTS))
    meta = meta.reshape(nt, 8, LANES)
    run_dst = meta[:, 0:1, :] + first_row[None, None, :]
    run_n = meta[:, 1:2, :]
    run_src = meta[:, 2:3, :]
    tile_ids = jnp.arange(n_tiles, dtype=jnp.int32)
    live = jnp.minimum(tile_ids, n_used - 1)
    tile_expert = jnp.sum(tile_end[None, :] <= live[:, None], axis=1).astype(jnp.int32)
    ids = jnp.arange(N_EXPERTS, dtype=jnp.int32)
    is_expert = tile_expert[:, None] == ids[None, :]
    of_tile = lambda per_expert: jnp.sum(jnp.where(is_expert, per_expert[None, :], 0), axis=1)
    in_expert = tile_ids - of_tile(tile_start)
    n_valid = jnp.clip(of_tile(counts) - in_expert * MOE_TILE, 0, MOE_TILE)
    n_valid = jnp.where(tile_ids < n_used, n_valid, 0).astype(jnp.int32)
    tile_first = jnp.logical_and(in_expert == 0, tile_ids < n_used).astype(jnp.int32)
    run_n_flat = run_n[:, 0, :N_EXPERTS].reshape(-1)
    run_loc_flat = run_src[:, 0, :N_EXPERTS].reshape(-1)
    used = tiles_e > 0
    later_used = jnp.where(jnp.logical_and(used[None, :], ids[None, :] > ids[:, None]), ids[None, :],
                           N_EXPERTS)
    following = jnp.min(later_used, axis=1)
    following = jnp.where(following < N_EXPERTS, following, ids)
    buf_of = (jnp.cumsum(used.astype(jnp.int32)) - 1) % 2
    return (tile_expert, n_used.reshape(1).astype(jnp.int32), of_tile(following).astype(jnp.int32),
            of_tile(buf_of).astype(jnp.int32), n_valid, tile_first, run_n_flat, run_loc_flat,
            run_src, run_n, run_dst)


def kernel(x_prompt, x_sample, cache_k, cache_v, g_mix, w_in, b_in, ln_v_g, ln_v_b, w_sp, b_sp,
           attn_sinks, w_pa, w_pb, w_o, g_ffn, w_router, b_router, w_gu, b_gu, w_dn, b_dn, g_final):
    nb, seq, d = x_prompt.shape
    nsb, nnew, _ = x_sample.shape
    tp, ts = nb * seq, nsb * nnew
    t = tp + ts
    xp = x_prompt.reshape(tp, d)
    xs = x_sample.reshape(ts, d)
    pos = np.concatenate([np.arange(seq), np.tile(PAST_LEN + np.arange(nnew), ROW_TILE // nnew)])
    tables = _rope_tables(pos)
    u, vln, v_sgu, q, k, v, kd, vd, ga, gb = _project(
        xp, xs, g_mix[0], w_in[0], b_in[0], ln_v_g[0], ln_v_b[0], tables, seq)
    x1, h_local, slot_t, gate_t, meta, counts = _mix(
        xp, xs, u, vln, q, kd, vd, cache_k[0], cache_v[0], ga, gb, w_sp[0], b_sp[0], attn_sinks[0],
        w_pa[0], w_pb[0], w_o[0], g_ffn[0], w_router[0], b_router[0], seq)
    nt = t // ROW_TILE
    picks = lambda a: a.reshape(nt, 8, ROW_TILE)[:, :TOP_K, :].reshape(nt, 1, TOP_K * ROW_TILE)
    slots, gates = picks(slot_t), picks(gate_t)
    (tile_expert, n_used, next_expert, weight_buf, n_valid, tile_first, run_n_flat, run_loc_flat,
     run_src, run_n, run_dst) = _plan(meta, counts, t)
    y_sorted = _experts(tile_expert, n_used, next_expert, weight_buf, n_valid, tile_first, run_n_flat,
                        run_loc_flat, h_local, w_gu[0], b_gu[0], w_dn[0], b_dn[0])
    y_p, y_s = _combine(run_src, run_n, run_dst, slots, gates, x1, y_sorted, g_final, tp)

    keep = min(WINDOW, seq)
    tails = lambda a: jnp.stack([a[(b + 1) * seq - keep:(b + 1) * seq] for b in range(nb)]).reshape(
        nb, keep, N_KV_HEADS, HEAD_DIM)
    kp, vp = tails(k), tails(v)
    ks = k[tp:].reshape(nsb, nnew, N_KV_HEADS, HEAD_DIM)
    vs = v[tp:].reshape(nsb, nnew, N_KV_HEADS, HEAD_DIM)
    return (y_p.reshape(nb, seq, d), y_s.reshape(nsb, nnew, d), kp[None], vp[None], ks[None],
            vs[None], v_sgu.reshape(1, nsb, nnew, SGU_WIDTH))
```

```python
import functools

import numpy as np
import jax
import jax.numpy as jnp
from jax import lax
from jax.experimental import pallas as pl
from jax.experimental.pallas import tpu as pltpu

D_MODEL = 1024
PAST_LEN = 2048
CHUNK = 64
SGU_CHUNK = 128
SGU_GROUPS = 8
SGU_WIDTH = 1024
N_HEADS = 16
N_KV_HEADS = 4
HEAD_DIM = 64
Q_PER_KV = N_HEADS // N_KV_HEADS
WINDOW = 128
ROT_DIM = HEAD_DIM // 4
ROPE_THETA = 500000.0
ATT_W = N_HEADS * HEAD_DIM
KV_W = N_KV_HEADS * HEAD_DIM
N_EXPERTS = 32
TOP_K = 4
D_FF = 1024
SWIGLU_ALPHA = 1.702
SWIGLU_LIMIT = 7.0
NORM_EPS = 1e-5
NEG_INF = -1e30
N_IN = SGU_WIDTH * 2 + ATT_W + KV_W * 2 + D_MODEL * 2

LANES = 128
ROW_TILE = 256
MOE_TILE = 512
ROW_UNROLL = 16
COL_BLOCK = 256
KV_DUP_W = N_KV_HEADS * LANES
KEY_SPAN = WINDOW + CHUNK
VMEM_LIMIT = 56 * 1024 * 1024

_SQRT_HALF = 0.7071067811865476
_LOG2_E = 1.4426950408889634


def _gelu(x):
    t = 1.0 / (1.0 + (0.3275911 * _SQRT_HALF) * jnp.abs(x))
    half_poly = t * (0.127414796 + t * (-0.142248368 + t * (0.7107068705
                     + t * (-0.7265760135 + t * 0.5307027145))))
    half_tail = x * (half_poly * jnp.exp2(x * x * (-0.5 * _LOG2_E)))
    return jnp.where(x >= 0.0, x - half_tail, half_tail)


def _sigmoid(x):
    return 1.0 / (1.0 + jnp.exp(-x))


def _bf16(x):
    return x.astype(jnp.bfloat16)


def _dot(a, b):
    return jnp.dot(a, b, preferred_element_type=jnp.float32)


ROW_SUBTILES = D_MODEL // LANES


def _store_row_tiled(ref, lead, x):
    rows = x.shape[0]
    for s in range(ROW_SUBTILES):
        ref[(*lead, pl.ds(s, rows, stride=ROW_SUBTILES), slice(None))] = x[:, s * LANES:(s + 1) * LANES]


def _load_row_tiled(ref, lead, rows):
    return jnp.concatenate(
        [ref[(*lead, pl.ds(s, rows, stride=ROW_SUBTILES), slice(None))] for s in range(ROW_SUBTILES)],
        axis=1)


def _dot_nt(a, b):
    return lax.dot_general(a, b, (((1,), (1,)), ((), ())), preferred_element_type=jnp.float32)


def _rms(x, g):
    return x * lax.rsqrt(jnp.mean(x * x, axis=-1, keepdims=True) + NORM_EPS) * g


def _lane_lo(rows):
    return lax.broadcasted_iota(jnp.int32, (rows, LANES), 1) < HEAD_DIM


def _dup_heads(kv):
    rows = kv.shape[0]
    lo = _lane_lo(rows)
    out = []
    for j in range(KV_W // LANES):
        blk = kv[:, j * LANES:(j + 1) * LANES]
        swp = pltpu.roll(blk, HEAD_DIM, axis=1)
        out.append(jnp.where(lo, blk, swp))
        out.append(jnp.where(lo, swp, blk))
    return _bf16(jnp.concatenate(out, axis=1))


def _rope_block(zb, cos_b, sin_lo, sin_hi):
    up = pltpu.roll(zb, LANES - ROT_DIM // 2, axis=1)
    dn = pltpu.roll(zb, ROT_DIM // 2, axis=1)
    return zb * cos_b + up * sin_lo + dn * sin_hi


def _proj_kernel(n_prompt_tiles, xp_ref, xs_ref, gmix_ref, w_ref, b_ref, lng_ref, lnb_ref,
                 cos_ref, slo_ref, shi_ref,
                 u_ref, vln_ref, vs_ref, q_ref, k_ref, v_ref, kd_ref, vd_ref, ga_ref, gb_ref):
    i = pl.program_id(0)
    x = jnp.where(i < n_prompt_tiles, xp_ref[...], xs_ref[...])
    h = _bf16(_rms(x, gmix_ref[...]))

    cos_b, sin_lo, sin_hi = cos_ref[...], slo_ref[...], shi_ref[...]
    off_u, off_v, off_q, off_k, off_vv, off_ga, off_gb = (
        int(o) for o in np.cumsum((0, SGU_WIDTH, SGU_WIDTH, ATT_W, KV_W, KV_W, D_MODEL)))

    def z(lo):
        return _dot(h, w_ref[:, lo:lo + COL_BLOCK]) + b_ref[:, lo:lo + COL_BLOCK]

    def cols(j):
        return slice(j * COL_BLOCK, (j + 1) * COL_BLOCK)

    def rope(zb):
        return jnp.concatenate(
            [_rope_block(zb[:, t * LANES:(t + 1) * LANES], cos_b, sin_lo, sin_hi)
             for t in range(COL_BLOCK // LANES)], axis=1)

    gelu_v = []

    def do_u(j):
        u_ref[:, cols(j)] = _bf16(z(off_u + j * COL_BLOCK))

    def do_v(j):
        gelu_v.append(_gelu(z(off_v + j * COL_BLOCK)))

    def do_q(j):
        q_ref[:, cols(j)] = _bf16(rope(z(off_q + j * COL_BLOCK)) * (HEAD_DIM ** -0.5))

    def do_k(j):
        kr = rope(z(off_k))
        k_ref[...] = kr
        kd_ref[...] = _dup_heads(kr)

    def do_vv(j):
        zv = z(off_vv)
        v_ref[...] = zv
        vd_ref[...] = _dup_heads(zv)

    def do_ga(j):
        ga_ref[:, cols(j)] = _bf16(z(off_ga + j * COL_BLOCK))

    def do_gb(j):
        gb_ref[:, cols(j)] = _bf16(z(off_gb + j * COL_BLOCK))

    def do_ln(j):
        gv = jnp.concatenate(gelu_v, axis=1)
        gc = gv - jnp.mean(gv, axis=-1, keepdims=True)
        var = jnp.mean(gc * gc, axis=-1, keepdims=True)
        vln = gc * lax.rsqrt(var + NORM_EPS) * lng_ref[...] + lnb_ref[...]
        vln_ref[...] = _bf16(vln)
        vs_ref[...] = vln

    order = ((do_v, 0), (do_q, 0), (do_v, 1), (do_q, 1), (do_v, 2), (do_q, 2), (do_v, 3), (do_q, 3),
             (do_u, 0), (do_ln, 0), (do_k, 0), (do_u, 1), (do_vv, 0), (do_u, 2), (do_ga, 0), (do_u, 3),
             (do_ga, 1), (do_ga, 2), (do_ga, 3), (do_gb, 0), (do_gb, 1), (do_gb, 2), (do_gb, 3))
    for fn, j in order:
        fn(j)


def _rope_tables(pos):
    half = ROT_DIM // 2
    inv = np.float32(ROPE_THETA) ** (-np.arange(half, dtype=np.float32) * np.float32(2.0) / ROT_DIM)
    ang = pos.astype(np.float32)[:, None] * inv.astype(np.float32)[None, :]
    cos = np.cos(ang.astype(np.float64)).astype(np.float32)
    sin = np.sin(ang.astype(np.float64)).astype(np.float32)
    n = pos.shape[0]
    ones = np.ones((n, HEAD_DIM - ROT_DIM), np.float32)
    zeros = np.zeros((n, HEAD_DIM - ROT_DIM), np.float32)
    zh = np.zeros((n, half), np.float32)
    cos_h = np.concatenate([cos, cos, ones], axis=1)
    slo_h = np.concatenate([-sin, zh, zeros], axis=1)
    shi_h = np.concatenate([zh, sin, zeros], axis=1)
    rep = LANES // HEAD_DIM
    return tuple(jnp.asarray(np.tile(a, (1, rep))) for a in (cos_h, slo_h, shi_h))


def _row_spec(width):
    return pl.BlockSpec((ROW_TILE, width), lambda i: (i, 0))


def _const_spec(shape):
    return pl.BlockSpec(shape, lambda i: (0,) * len(shape))


def _prompt_spec(width, n_prompt_tiles):
    return pl.BlockSpec((ROW_TILE, width), lambda i: (jnp.minimum(i, n_prompt_tiles - 1), 0))


def _sample_spec(width, n_prompt_tiles):
    return pl.BlockSpec((ROW_TILE, width), lambda i: (jnp.maximum(i - n_prompt_tiles, 0), 0))


def _params():
    return pltpu.CompilerParams(dimension_semantics=("arbitrary",), vmem_limit_bytes=VMEM_LIMIT)


def _project(xp, xs, g_mix, w_in, b_in, ln_g, ln_b, tables, seq):
    tp, ts = xp.shape[0], xs.shape[0]
    t = tp + ts
    npt = tp // ROW_TILE
    tiles_per_seq = seq // ROW_TILE
    f32, bf16 = jnp.float32, jnp.bfloat16
    table_spec = pl.BlockSpec(
        (ROW_TILE, LANES), lambda i: (jnp.where(i < npt, i % tiles_per_seq, tiles_per_seq), 0))
    out_shape = (
        jax.ShapeDtypeStruct((t, SGU_WIDTH), bf16),
        jax.ShapeDtypeStruct((t, SGU_WIDTH), bf16),
        jax.ShapeDtypeStruct((ts, SGU_WIDTH), f32),
        jax.ShapeDtypeStruct((t, ATT_W), bf16),
        jax.ShapeDtypeStruct((t, KV_W), f32),
        jax.ShapeDtypeStruct((t, KV_W), f32),
        jax.ShapeDtypeStruct((t, KV_DUP_W), bf16),
        jax.ShapeDtypeStruct((t, KV_DUP_W), bf16),
        jax.ShapeDtypeStruct((t, D_MODEL), bf16),
        jax.ShapeDtypeStruct((t, D_MODEL), bf16),
    )
    return pl.pallas_call(
        functools.partial(_proj_kernel, npt),
        out_shape=out_shape,
        grid=(t // ROW_TILE,),
        in_specs=[
            _prompt_spec(D_MODEL, npt), _sample_spec(D_MODEL, npt),
            _const_spec((1, D_MODEL)), _const_spec((D_MODEL, N_IN)),
            _const_spec((1, N_IN)), _const_spec((1, SGU_WIDTH)), _const_spec((1, SGU_WIDTH)),
            table_spec, table_spec, table_spec,
        ],
        out_specs=(
            _row_spec(SGU_WIDTH), _row_spec(SGU_WIDTH), _sample_spec(SGU_WIDTH, npt),
            _row_spec(ATT_W), _row_spec(KV_W), _row_spec(KV_W), _row_spec(KV_DUP_W),
            _row_spec(KV_DUP_W), _row_spec(D_MODEL), _row_spec(D_MODEL),
        ),
        compiler_params=_params(),
        name="proj",
    )(xp, xs, g_mix.reshape(1, -1), w_in.astype(bf16), b_in.reshape(1, -1),
      ln_g.reshape(1, -1), ln_b.reshape(1, -1), *tables)


def _attend(qa, qb, kwin, vwin, sink, valid):
    lo = _lane_lo(CHUNK)
    zero = jnp.zeros_like(qa)
    lhs = jnp.concatenate([jnp.where(lo, qa, zero), jnp.where(lo, zero, qa),
                           jnp.where(lo, qb, zero), jnp.where(lo, zero, qb)], axis=0)
    s = _dot_nt(lhs, kwin)
    if valid is not None:
        s = jnp.where(valid, s, NEG_INF)
    s_a, s_b = s[:, :LANES], s[:, LANES:]
    tail = s_b.shape[1]
    m = jnp.maximum(jnp.max(s, axis=-1, keepdims=True), sink)
    p_a = jnp.exp(s_a - m)
    p_b = jnp.exp(s_b - m[:, :tail])
    denom = (jnp.sum(jnp.concatenate([p_a, p_b], axis=1), axis=-1, keepdims=True)
             + jnp.exp(sink - m))
    inv = 1.0 / denom
    pn = jnp.concatenate([p_a * inv, p_b * inv[:, :tail]], axis=1)
    r = _dot(_bf16(pn), vwin)
    oa = jnp.where(lo, r[0:CHUNK], r[CHUNK:2 * CHUNK])
    ob = jnp.where(lo, r[2 * CHUNK:3 * CHUNK], r[3 * CHUNK:4 * CHUNK])
    return oa, ob


def _stack_rows(rows):
    ri = lax.broadcasted_iota(jnp.int32, (8, rows[0].shape[1]), 0)
    out = jnp.zeros((8, rows[0].shape[1]), rows[0].dtype)
    for k, row in enumerate(rows):
        out = jnp.where(ri == k, row, out)
    return out


def _route_pick(logits_t):
    rows = logits_t.shape[1]
    eid = lax.broadcasted_iota(jnp.int32, (N_EXPERTS, rows), 0)
    work = logits_t
    vals, idxs = [], []
    for _ in range(TOP_K):
        m = jnp.max(work, axis=0, keepdims=True)
        idx = jnp.min(jnp.where(work == m, eid, N_EXPERTS), axis=0, keepdims=True)
        vals.append(m)
        idxs.append(idx)
        work = jnp.where(eid == idx, -jnp.inf, work)
    exps = [jnp.exp(v - vals[0]) for v in vals]
    inv = 1.0 / (exps[0] + exps[1] + exps[2] + exps[3])
    gates = _stack_rows([e * inv for e in exps])

    picked = jnp.zeros((N_EXPERTS, rows), jnp.float32)
    for idx in idxs:
        picked = jnp.where(eid == idx, 1.0, picked)
    tr = lax.broadcasted_iota(jnp.int32, (rows, rows), 0)
    tc = lax.broadcasted_iota(jnp.int32, (rows, rows), 1)
    earlier = _bf16(jnp.where(tr < tc, 1.0, 0.0))
    in_tile = _dot(_bf16(picked), earlier)
    count_col = jnp.broadcast_to(jnp.sum(picked, axis=1, keepdims=True), (N_EXPERTS, LANES))
    eid_wide = lax.broadcasted_iota(jnp.int32, (LANES, rows), 0)
    picked_wide = jnp.zeros((LANES, rows), jnp.float32)
    for idx in idxs:
        picked_wide = jnp.where(eid_wide == idx, 1.0, picked_wide)
    count_row = _dot_nt(jnp.ones((8, rows), jnp.bfloat16), _bf16(picked_wide))
    return idxs, gates, in_tile, count_col, count_row


def _route_place(idxs, in_tile, count_col, count_row, carry_s, live):
    rows = in_tile.shape[1]
    eid = lax.broadcasted_iota(jnp.int32, (N_EXPERTS, rows), 0)
    er = lax.broadcasted_iota(jnp.int32, (N_EXPERTS, N_EXPERTS), 0)
    ec = lax.broadcasted_iota(jnp.int32, (N_EXPERTS, N_EXPERTS), 1)
    start_col = _dot(_bf16(jnp.where(ec < er, 1.0, 0.0)), _bf16(count_col))
    local = in_tile + jnp.concatenate([start_col] * (rows // LANES), axis=1)
    slots = _stack_rows([jnp.sum(jnp.where(eid == idx, local, 0.0), axis=0, keepdims=True)
                         for idx in idxs]).astype(jnp.int32)
    lr = lax.broadcasted_iota(jnp.int32, (LANES, LANES), 0)
    lc = lax.broadcasted_iota(jnp.int32, (LANES, LANES), 1)
    start_row = _dot(_bf16(count_row), _bf16(jnp.where(lr < lc, 1.0, 0.0)))
    ri = lax.broadcasted_iota(jnp.int32, (8, LANES), 0)
    meta = jnp.where(ri == 0, carry_s[...], jnp.where(ri == 1, count_row, jnp.where(ri == 2, start_row, 0.0)))
    carry_s[...] = carry_s[...] + count_row * live
    return slots, meta.astype(jnp.int32)


def _mix_kernel(tiles_per_seq, n_prompt_tiles,
                xp_ref, xs_ref, u_ref, vln_ref, q_ref, kd_ref, vd_ref, kdp_ref, vdp_ref,
                ck_ref, cv_ref, ga_ref, gb_ref, wsp_ref, bsp_ref, sink_ref,
                wpa_ref, wpb_ref, wo_ref, gffn_ref, wrh_ref, wrl_ref, br_ref,
                x1_ref, hloc_ref, slot_ref, gate_ref, meta_ref, count_ref,
                a_s, o_s, kwin_s, vwin_s, carry_s, hhi_s, hlo_s):
    i = pl.program_id(0)
    n_streams = ROW_TILE // CHUNK

    @pl.when(i == 0)
    def _():
        carry_s[...] = jnp.zeros_like(carry_s)
        hhi_s[...] = jnp.zeros_like(hhi_s)
        hlo_s[...] = jnp.zeros_like(hlo_s)

    def sgu_rows(r0, rows):
        ri = lax.broadcasted_iota(jnp.int32, (rows, rows), 0) // CHUNK
        ci = lax.broadcasted_iota(jnp.int32, (rows, rows), 1) // CHUNK
        for g in range(SGU_GROUPS):
            cols = slice(g * LANES, (g + 1) * LANES)
            w = _bf16(jnp.where(ci <= ri, wsp_ref[g, :rows, :rows], 0.0))
            sp = _dot(w, vln_ref[r0:r0 + rows, cols]) + bsp_ref[g, :rows, :]
            a_s[r0:r0 + rows, cols] = _bf16(_gelu(u_ref[r0:r0 + rows, cols].astype(jnp.float32)) * sp)

    def attend_rows(r0, kwin_of, valid):
        for g in range(N_KV_HEADS):
            c0 = g * Q_PER_KV * HEAD_DIM
            kwin, vwin = kwin_of(g)
            oa, ob = _attend(q_ref[r0:r0 + CHUNK, c0:c0 + LANES],
                             q_ref[r0:r0 + CHUNK, c0 + LANES:c0 + 2 * LANES],
                             kwin, vwin, sink_ref[g], valid)
            o_s[r0:r0 + CHUNK, c0:c0 + LANES] = _bf16(oa)
            o_s[r0:r0 + CHUNK, c0 + LANES:c0 + 2 * LANES] = _bf16(ob)

    @pl.when(i < n_prompt_tiles)
    def _prompt():
        for c in range(ROW_TILE // SGU_CHUNK):
            sgu_rows(c * SGU_CHUNK, SGU_CHUNK)
        kwin_s[0:WINDOW] = kdp_ref[...]
        kwin_s[WINDOW:WINDOW + ROW_TILE] = kd_ref[...]
        vwin_s[0:WINDOW] = vdp_ref[...]
        vwin_s[WINDOW:WINDOW + ROW_TILE] = vd_ref[...]
        first = (i % tiles_per_seq) == 0
        col = lax.broadcasted_iota(jnp.int32, (1, KEY_SPAN), 1)
        for j in range(ROW_TILE // CHUNK):
            r0 = j * CHUNK
            valid = jnp.logical_or(jnp.logical_not(first), col + r0 >= WINDOW) if r0 < WINDOW else None

            def kwin_of(g, r0=r0):
                cols = slice(g * LANES, (g + 1) * LANES)
                return kwin_s[r0:r0 + KEY_SPAN, cols], vwin_s[r0:r0 + KEY_SPAN, cols]

            attend_rows(r0, kwin_of, valid)

    @pl.when(i >= n_prompt_tiles)
    def _sample():
        for s in range(n_streams):
            r0 = s * CHUNK
            sgu_rows(r0, CHUNK)
            kwin_s[0:WINDOW] = _dup_heads(ck_ref[s])
            kwin_s[WINDOW:KEY_SPAN] = kd_ref[r0:r0 + CHUNK]
            vwin_s[0:WINDOW] = _dup_heads(cv_ref[s])
            vwin_s[WINDOW:KEY_SPAN] = vd_ref[r0:r0 + CHUNK]

            def kwin_of(g):
                cols = slice(g * LANES, (g + 1) * LANES)
                return kwin_s[0:KEY_SPAN, cols], vwin_s[0:KEY_SPAN, cols]

            attend_rows(r0, kwin_of, None)

    hh, hl = hhi_s[...], hlo_s[...]
    logits_t = (_dot_nt(wrh_ref[...], hh) + _dot_nt(wrl_ref[...], hh) + _dot_nt(wrh_ref[...], hl)
                + jnp.concatenate([br_ref[...]] * (ROW_TILE // LANES), axis=1))
    m_a = _sigmoid(ga_ref[...].astype(jnp.float32)) * _dot(a_s[...], _bf16(wpa_ref[...]))
    idxs, gates, in_tile, count_col, count_row = _route_pick(logits_t)
    gate_ref[...] = gates
    m = m_a + _sigmoid(gb_ref[...].astype(jnp.float32)) * _dot(o_s[...], _bf16(wpb_ref[...]))
    slots, meta = _route_place(idxs, in_tile, count_col, count_row, carry_s, jnp.where(i > 0, 1.0, 0.0))
    half = ((i + 1) % 2) * (ROW_TILE * TOP_K * ROW_SUBTILES)
    slot_ref[...] = slots * ROW_SUBTILES + half
    meta_ref[...] = meta
    count_ref[...] = carry_s[...].astype(jnp.int32)
    n_slots = ROW_TILE * TOP_K
    sid = lax.broadcasted_iota(jnp.int32, (n_slots, ROW_TILE), 0)
    place = jnp.zeros((n_slots, ROW_TILE), jnp.float32)
    for k in range(TOP_K):
        place = jnp.where(sid == slots[k:k + 1, :], 1.0, place)
    _store_row_tiled(hloc_ref, (), _dot(_bf16(place), hh))

    x = jnp.where(i < n_prompt_tiles, xp_ref[...], xs_ref[...])
    x1 = x + _dot(_bf16(m), _bf16(wo_ref[...]))
    x1_ref[...] = x1
    h2 = _rms(x1, gffn_ref[...])
    h2_hi = _bf16(h2)
    hhi_s[...] = h2_hi
    hlo_s[...] = _bf16(h2 - h2_hi.astype(jnp.float32))


def _mix(xp, xs, u, vln, q, kd, vd, cache_k, cache_v, ga, gb, w_sp, b_sp, sinks,
         w_pa, w_pb, w_o, g_ffn, w_router, b_router, seq):
    tp, ts = xp.shape[0], xs.shape[0]
    t = tp + ts
    npt = tp // ROW_TILE
    tiles_per_seq = seq // ROW_TILE
    f32, bf16 = jnp.float32, jnp.bfloat16
    n_streams = ROW_TILE // CHUNK
    win_per_tile = ROW_TILE // WINDOW

    nt = t // ROW_TILE
    cur = lambda i: jnp.minimum(i, nt - 1)
    smp = lambda i: jnp.maximum(cur(i) - npt, 0)
    row = lambda width: pl.BlockSpec((ROW_TILE, width), lambda i: (cur(i), 0))
    prev_spec = pl.BlockSpec(
        (WINDOW, KV_DUP_W), lambda i: (jnp.maximum(jnp.minimum(i, npt - 1) * win_per_tile - 1, 0), 0))
    cache_spec = pl.BlockSpec((n_streams, WINDOW, KV_W), lambda i: (smp(i), 0, 0))
    xs_spec = pl.BlockSpec((ROW_TILE, D_MODEL), lambda i: (smp(i), 0))
    sink_cols = jnp.broadcast_to(
        jnp.repeat(sinks.astype(f32).reshape(N_KV_HEADS, Q_PER_KV), CHUNK, axis=1)[:, :, None],
        (N_KV_HEADS, Q_PER_KV * CHUNK, LANES))
    wr_t = w_router.T
    wr_hi = wr_t.astype(bf16)
    wr_lo = (wr_t - wr_hi.astype(f32)).astype(bf16)
    routed8 = lambda width: pl.BlockSpec((8, width), lambda i: (jnp.maximum(i - 1, 0), 0))
    out_shape = (
        jax.ShapeDtypeStruct((t, D_MODEL), f32),
        jax.ShapeDtypeStruct((t * TOP_K * ROW_SUBTILES, LANES), f32),
        jax.ShapeDtypeStruct((nt * 8, ROW_TILE), jnp.int32),
        jax.ShapeDtypeStruct((nt * 8, ROW_TILE), f32),
        jax.ShapeDtypeStruct((nt * 8, LANES), jnp.int32),
        jax.ShapeDtypeStruct((8, LANES), jnp.int32),
    )
    return pl.pallas_call(
        functools.partial(_mix_kernel, tiles_per_seq, npt),
        out_shape=out_shape,
        grid=(nt + 1,),
        in_specs=[
            _prompt_spec(D_MODEL, npt), xs_spec,
            row(SGU_WIDTH), row(SGU_WIDTH), row(ATT_W),
            row(KV_DUP_W), row(KV_DUP_W), prev_spec, prev_spec,
            cache_spec, cache_spec, row(D_MODEL), row(D_MODEL),
            _const_spec((SGU_GROUPS, SGU_CHUNK, SGU_CHUNK)), _const_spec((SGU_GROUPS, SGU_CHUNK, LANES)),
            _const_spec((N_KV_HEADS, Q_PER_KV * CHUNK, LANES)),
            _const_spec((SGU_WIDTH, D_MODEL)), _const_spec((ATT_W, D_MODEL)),
            _const_spec((D_MODEL, D_MODEL)), _const_spec((1, D_MODEL)),
            _const_spec((N_EXPERTS, D_MODEL)), _const_spec((N_EXPERTS, D_MODEL)),
            _const_spec((N_EXPERTS, LANES)),
        ],
        out_specs=(row(D_MODEL),
                   pl.BlockSpec((ROW_TILE * TOP_K * ROW_SUBTILES, LANES),
                                lambda i: (jnp.maximum(i - 1, 0), 0)),
                   routed8(ROW_TILE), routed8(ROW_TILE), routed8(LANES), _const_spec((8, LANES))),
        scratch_shapes=[
            pltpu.VMEM((ROW_TILE, SGU_WIDTH), bf16), pltpu.VMEM((ROW_TILE, ATT_W), bf16),
            pltpu.VMEM((WINDOW + ROW_TILE, KV_DUP_W), bf16),
            pltpu.VMEM((WINDOW + ROW_TILE, KV_DUP_W), bf16),
            pltpu.VMEM((8, LANES), f32),
            pltpu.VMEM((ROW_TILE, D_MODEL), bf16), pltpu.VMEM((ROW_TILE, D_MODEL), bf16),
        ],
        compiler_params=_params(),
        name="mix",
    )(xp, xs, u, vln, q, kd, vd, kd, vd,
      cache_k.reshape(-1, WINDOW, KV_W), cache_v.reshape(-1, WINDOW, KV_W), ga, gb,
      w_sp, jnp.broadcast_to(b_sp[:, :, None], (SGU_GROUPS, SGU_CHUNK, LANES)), sink_cols,
      w_pa, w_pb, w_o,
      g_ffn.reshape(1, -1), wr_hi, wr_lo,
      jnp.broadcast_to(b_router.astype(f32)[:, None], (N_EXPERTS, LANES)))


def _unrolled_rows(n_rows, fn):
    if isinstance(n_rows, int):
        groups, tail_start = n_rows // ROW_UNROLL, n_rows - n_rows % ROW_UNROLL
    else:
        groups = lax.shift_right_logical(n_rows, ROW_UNROLL.bit_length() - 1)
        tail_start = groups * ROW_UNROLL

    def group(gi, carry):
        for lane in range(ROW_UNROLL):
            fn(gi * ROW_UNROLL + lane, lane)
        return carry

    def tail(r, carry):
        fn(r, 0)
        return carry

    lax.fori_loop(0, groups, group, 0)
    lax.fori_loop(tail_start, n_rows, tail, 0)


def _row_span(first_row, n_rows):
    return pl.ds(pl.multiple_of(first_row * ROW_SUBTILES, ROW_SUBTILES),
                 pl.multiple_of(n_rows * ROW_SUBTILES, ROW_SUBTILES))


def _run_spec(index_of):
    return pl.BlockSpec((1, 1, LANES), lambda i, *_: (index_of(i), 0, 0), memory_space=pltpu.SMEM)


def _expert_kernel(n_token_tiles,
                   te_ref, nu_ref, nx_ref, par_ref, nv_ref, tf_ref, cnt_ref, loc_ref,
                   bgu_ref, bdn_ref, hloc_hbm, wgu_hbm, wdn_hbm,
                   ys_hbm,
                   xbuf, ybuf, zbuf, wgu_f, wdn_f, walk, xsem, ysem, zsem, wsem):
    i = pl.program_id(0)
    n_used = nu_ref[0]
    expert = te_ref[i]
    buf = par_ref[i]
    slot = i % 2
    expert_changed = jnp.logical_or(i == 0, expert != te_ref[jnp.maximum(i - 1, 0)])

    def fetch_rows(j, b):
        e = te_ref[j]
        need = nv_ref[j]

        @pl.when(tf_ref[j] == 1)
        def _():
            walk[0] = 0
            walk[1] = 0

        @pl.when(need < MOE_TILE)
        def _():
            xbuf[b] = jnp.zeros(xbuf.shape[1:], xbuf.dtype)

        def unfinished(state):
            filled, tile, _ = state
            return jnp.logical_and(filled < need, tile < n_token_tiles)

        def take_run(state):
            filled, tile, off = state
            run = cnt_ref[tile * N_EXPERTS + e]
            take = jnp.minimum(run - off, need - filled)

            @pl.when(take > 0)
            def _():
                src = tile * (ROW_TILE * TOP_K) + loc_ref[tile * N_EXPERTS + e] + off
                pltpu.make_async_copy(hloc_hbm.at[_row_span(src, take)],
                                      xbuf.at[b, _row_span(filled, take)], xsem.at[b]).start()

            run_done = off + take == run
            return (filled + take, jnp.where(run_done, tile + 1, tile), jnp.where(run_done, 0, off + take))

        _, tile, off = lax.while_loop(unfinished, take_run, (jnp.int32(0), walk[0], walk[1]))
        walk[0] = tile
        walk[1] = off

    @pl.when(jnp.logical_and(i == 0, n_used > 0))
    def _():
        fetch_rows(0, 0)

    @pl.when(i + 1 < n_used)
    def _():
        fetch_rows(i + 1, 1 - slot)

    def weight_copies(e, b):
        return (pltpu.make_async_copy(wgu_hbm.at[e], wgu_f.at[b], wsem.at[0, b]),
                pltpu.make_async_copy(wdn_hbm.at[e], wdn_f.at[b], wsem.at[1, b]))

    @pl.when(jnp.logical_and(i < n_used, expert_changed))
    def _():
        @pl.when(i == 0)
        def _():
            for copy in weight_copies(expert, buf):
                copy.start()

        for copy in weight_copies(expert, buf):
            copy.wait()
        following = nx_ref[i]

        @pl.when(following != expert)
        def _():
            for copy in weight_copies(following, 1 - buf):
                copy.start()


    rows = nv_ref[i]

    @pl.when(i < n_used)
    def _():
        pltpu.make_async_copy(hloc_hbm.at[_row_span(0, rows)], xbuf.at[slot, _row_span(0, rows)],
                              xsem.at[slot]).wait()

    def mlp(n):
        x = _bf16(_load_row_tiled(xbuf, (slot,), n))
        gu = _dot(x, _bf16(wgu_f[buf])) + bgu_ref[0]
        gate = jnp.minimum(gu[:, :D_FF], SWIGLU_LIMIT)
        lin = jnp.clip(gu[:, D_FF:], -SWIGLU_LIMIT, SWIGLU_LIMIT)
        act = gate * _sigmoid(SWIGLU_ALPHA * gate) * (lin + 1.0)
        _store_row_tiled(ybuf, (slot,), _dot(_bf16(act), _bf16(wdn_f[buf])) + bdn_ref[0])

    last = pl.num_programs(0) - 1
    tile_rows = ybuf.shape[1]

    def out_copy(j, b):
        return pltpu.make_async_copy(
            ybuf.at[b], ys_hbm.at[pl.ds(pl.multiple_of(j * tile_rows, tile_rows), tile_rows)], ysem.at[b])

    def unused_tiles(act_on):
        def body(j, carry):
            act_on(pltpu.make_async_copy(
                zbuf, ys_hbm.at[pl.ds(pl.multiple_of(j * tile_rows, tile_rows), tile_rows)], zsem))
            return carry
        lax.fori_loop(n_used, last + 1, body, 0)

    @pl.when(i == 0)
    def _():
        zbuf[...] = jnp.zeros_like(zbuf)
        unused_tiles(lambda copy: copy.start())

    @pl.when(jnp.logical_and(i >= 2, i - 2 < n_used))
    def _():
        out_copy(i - 2, slot).wait()

    @pl.when(jnp.logical_and(i < n_used, rows > MOE_TILE // 2))
    def _():
        mlp(MOE_TILE)

    @pl.when(jnp.logical_and(i < n_used, rows <= MOE_TILE // 2))
    def _():
        mlp(MOE_TILE // 2)
        ybuf[slot, pl.ds(MOE_TILE // 2 * ROW_SUBTILES, MOE_TILE // 2 * ROW_SUBTILES), :] = jnp.zeros(
            (MOE_TILE // 2 * ROW_SUBTILES, LANES), ybuf.dtype)

    @pl.when(i < n_used)
    def _():
        out_copy(i, slot).start()

    @pl.when(i == last)
    def _():
        @pl.when(jnp.logical_and(i >= 1, i - 1 < n_used))
        def _():
            out_copy(i - 1, 1 - slot).wait()

        @pl.when(i < n_used)
        def _():
            out_copy(i, slot).wait()

        unused_tiles(lambda copy: copy.wait())


def _experts(tile_expert, n_used, next_expert, weight_buf, n_valid, tile_first, run_n, run_loc,
             h_local, w_gu, b_gu, w_dn, b_dn):
    n_tiles = tile_expert.shape[0]
    n_token_tiles = h_local.shape[0] // (ROW_TILE * TOP_K * ROW_SUBTILES)
    f32, bf16 = jnp.float32, jnp.bfloat16
    tile_rows = MOE_TILE * ROW_SUBTILES
    grid_spec = pltpu.PrefetchScalarGridSpec(
        num_scalar_prefetch=8,
        grid=(n_tiles,),
        in_specs=[
            pl.BlockSpec((1, 1, 2 * D_FF), lambda i, te, *_: (te[i], 0, 0)),
            pl.BlockSpec((1, 1, D_MODEL), lambda i, te, *_: (te[i], 0, 0)),
            pl.BlockSpec(memory_space=pl.ANY), pl.BlockSpec(memory_space=pl.ANY),
            pl.BlockSpec(memory_space=pl.ANY),
        ],
        out_specs=pl.BlockSpec(memory_space=pl.ANY),
        scratch_shapes=[
            pltpu.VMEM((2, tile_rows, LANES), f32), pltpu.VMEM((2, tile_rows, LANES), f32),
            pltpu.VMEM((tile_rows, LANES), f32),
            pltpu.VMEM((2, D_MODEL, 2 * D_FF), f32), pltpu.VMEM((2, D_FF, D_MODEL), f32),
            pltpu.SMEM((2,), jnp.int32),
            pltpu.SemaphoreType.DMA((2,)), pltpu.SemaphoreType.DMA((2,)), pltpu.SemaphoreType.DMA,
            pltpu.SemaphoreType.DMA((2, 2)),
        ],
    )
    return pl.pallas_call(
        functools.partial(_expert_kernel, n_token_tiles),
        out_shape=jax.ShapeDtypeStruct((n_tiles * tile_rows, LANES), f32),
        grid_spec=grid_spec,
        compiler_params=_params(),
        name="experts",
    )(tile_expert, n_used, next_expert, weight_buf, n_valid, tile_first, run_n, run_loc,
      b_gu.reshape(N_EXPERTS, 1, -1), b_dn.reshape(N_EXPERTS, 1, -1), h_local, w_gu, w_dn)


def _combine_kernel(n_prompt_tiles,
                    src_ref, n_ref, dst_ref, src_nx_ref, n_nx_ref, dst_nx_ref, slot_ref, gate_ref,
                    x1_ref, gfin_ref, ys_hbm,
                    yp_ref, yo_ref,
                    local, mixed, run_sem):
    i = pl.program_id(0)
    last = pl.num_programs(0) - 1
    buf = i % 2
    half_slots = ROW_TILE * TOP_K
    half_rows = half_slots * ROW_SUBTILES

    def fetch_runs(s_ref, c_ref, d_ref, b):
        for e in range(N_EXPERTS):
            n = c_ref[0, 0, e]
            copy = pltpu.make_async_copy(ys_hbm.at[_row_span(d_ref[0, 0, e], n)],
                                         local.at[_row_span(b * half_slots + s_ref[0, 0, e], n)],
                                         run_sem.at[b])
            pl.when(n > 0)(functools.partial(copy.start, priority=e % 2))

    @pl.when(i == 0)
    def _():
        fetch_runs(src_ref, n_ref, dst_ref, 0)

    @pl.when(i < last)
    def _():
        fetch_runs(src_nx_ref, n_nx_ref, dst_nx_ref, 1 - buf)

    pltpu.make_async_copy(ys_hbm.at[pl.ds(0, half_rows)],
                          local.at[pl.ds(pl.multiple_of(buf * half_rows, half_rows), half_rows)],
                          run_sem.at[buf]).wait()

    def blend(t, lane):
        acc = None
        for k in range(TOP_K):
            at = pl.multiple_of(slot_ref[0, 0, k * ROW_TILE + t], ROW_SUBTILES)
            term = gate_ref[0, 0, k * ROW_TILE + t] * local[pl.ds(at, ROW_SUBTILES), :]
            acc = term if acc is None else acc + term
        mixed[pl.ds(pl.multiple_of(t * ROW_SUBTILES, ROW_SUBTILES), ROW_SUBTILES), :] = acc
    _unrolled_rows(ROW_TILE, blend)

    out = _rms(x1_ref[...] + _load_row_tiled(mixed, (), ROW_TILE), gfin_ref[...])

    @pl.when(i < n_prompt_tiles)
    def _():
        yp_ref[...] = out

    @pl.when(i >= n_prompt_tiles)
    def _():
        yo_ref[...] = out


def _combine(run_src, run_n, run_dst, slots, gates, x1, ys, g_final, tp):
    t = x1.shape[0]
    npt = tp // ROW_TILE
    nt = t // ROW_TILE
    f32 = jnp.float32
    picks = ROW_TILE * TOP_K
    nxt = lambda i: jnp.minimum(i + 1, nt - 1)
    pick_spec = pl.BlockSpec((1, 1, picks), lambda i: (i, 0, 0), memory_space=pltpu.SMEM)
    return pl.pallas_call(
        functools.partial(_combine_kernel, npt),
        out_shape=(jax.ShapeDtypeStruct((tp, D_MODEL), f32),
                   jax.ShapeDtypeStruct((t - tp, D_MODEL), f32)),
        grid=(nt,),
        in_specs=[_run_spec(lambda i: i), _run_spec(lambda i: i), _run_spec(lambda i: i),
                  _run_spec(nxt), _run_spec(nxt), _run_spec(nxt), pick_spec, pick_spec,
                  _row_spec(D_MODEL), _const_spec((1, D_MODEL)), pl.BlockSpec(memory_space=pl.ANY)],
        out_specs=(_prompt_spec(D_MODEL, npt), _sample_spec(D_MODEL, npt)),
        scratch_shapes=[pltpu.VMEM((2 * picks * ROW_SUBTILES, LANES), f32),
                        pltpu.VMEM((ROW_TILE * ROW_SUBTILES, LANES), f32),
                        pltpu.SemaphoreType.DMA((2,))],
        compiler_params=_params(),
        name="combine",
    )(run_src, run_n, run_dst, run_src, run_n, run_dst, slots, gates, x1, g_final.reshape(1, -1), ys)


def _plan(meta, counts, t):
    nt = t // ROW_TILE
    n_tiles = (t * TOP_K + N_EXPERTS * (MOE_TILE - 1)) // MOE_TILE
    counts = counts[0, :N_EXPERTS]
    tiles_e = (counts + MOE_TILE - 1) // MOE_TILE
    tile_end = jnp.cumsum(tiles_e)
    tile_start = tile_end - tiles_e
    n_used = tile_end[-1]
    first_row = jnp.pad(tile_start * MOE_TILE, (0, LANES - N_EXPERTS))
    meta = meta.reshape(nt, 8, LANES)
    run_dst = meta[:, 0:1, :] + first_row[None, None, :]
    run_n = meta[:, 1:2, :]
    run_src = meta[:, 2:3, :]
    tile_ids = jnp.arange(n_tiles, dtype=jnp.int32)
    live = jnp.minimum(tile_ids, n_used - 1)
    tile_expert = jnp.sum(tile_end[None, :] <= live[:, None], axis=1).astype(jnp.int32)
    ids = jnp.arange(N_EXPERTS, dtype=jnp.int32)
    is_expert = tile_expert[:, None] == ids[None, :]
    of_tile = lambda per_expert: jnp.sum(jnp.where(is_expert, per_expert[None, :], 0), axis=1)
    in_expert = tile_ids - of_tile(tile_start)
    n_valid = jnp.clip(of_tile(counts) - in_expert * MOE_TILE, 0, MOE_TILE)
    n_valid = jnp.where(tile_ids < n_used, n_valid, 0).astype(jnp.int32)
    tile_first = jnp.logical_and(in_expert == 0, tile_ids < n_used).astype(jnp.int32)
    run_n_flat = run_n[:, 0, :N_EXPERTS].reshape(-1)
    run_loc_flat = run_src[:, 0, :N_EXPERTS].reshape(-1)
    used = tiles_e > 0
    later_used = jnp.where(jnp.logical_and(used[None, :], ids[None, :] > ids[:, None]), ids[None, :],
                           N_EXPERTS)
    following = jnp.min(later_used, axis=1)
    following = jnp.where(following < N_EXPERTS, following, ids)
    buf_of = (jnp.cumsum(used.astype(jnp.int32)) - 1) % 2
    return (tile_expert, n_used.reshape(1).astype(jnp.int32), of_tile(following).astype(jnp.int32),
            of_tile(buf_of).astype(jnp.int32), n_valid, tile_first, run_n_flat, run_loc_flat,
            run_src, run_n, run_dst)


def kernel(x_prompt, x_sample, cache_k, cache_v, g_mix, w_in, b_in, ln_v_g, ln_v_b, w_sp, b_sp,
           attn_sinks, w_pa, w_pb, w_o, g_ffn, w_router, b_router, w_gu, b_gu, w_dn, b_dn, g_final):
    nb, seq, d = x_prompt.shape
    nsb, nnew, _ = x_sample.shape
    tp, ts = nb * seq, nsb * nnew
    t = tp + ts
    xp = x_prompt.reshape(tp, d)
    xs = x_sample.reshape(ts, d)
    pos = np.concatenate([np.arange(seq), np.tile(PAST_LEN + np.arange(nnew), ROW_TILE // nnew)])
    tables = _rope_tables(pos)
    u, vln, v_sgu, q, k, v, kd, vd, ga, gb = _project(
        xp, xs, g_mix[0], w_in[0], b_in[0], ln_v_g[0], ln_v_b[0], tables, seq)
    x1, h_local, slot_t, gate_t, meta, counts = _mix(
        xp, xs, u, vln, q, kd, vd, cache_k[0], cache_v[0], ga, gb, w_sp[0], b_sp[0], attn_sinks[0],
        w_pa[0], w_pb[0], w_o[0], g_ffn[0], w_router[0], b_router[0], seq)
    nt = t // ROW_TILE
    picks = lambda a: a.reshape(nt, 8, ROW_TILE)[:, :TOP_K, :].reshape(nt, 1, TOP_K * ROW_TILE)
    slots, gates = picks(slot_t), picks(gate_t)
    (tile_expert, n_used, next_expert, weight_buf, n_valid, tile_first, run_n_flat, run_loc_flat,
     run_src, run_n, run_dst) = _plan(meta, counts, t)
    y_sorted = _experts(tile_expert, n_used, next_expert, weight_buf, n_valid, tile_first, run_n_flat,
                        run_loc_flat, h_local, w_gu[0], b_gu[0], w_dn[0], b_dn[0])
    y_p, y_s = _combine(run_src, run_n, run_dst, slots, gates, x1, y_sorted, g_final, tp)

    keep = min(WINDOW, seq)
    tails = lambda a: jnp.stack([a[(b + 1) * seq - keep:(b + 1) * seq] for b in range(nb)]).reshape(
        nb, keep, N_KV_HEADS, HEAD_DIM)
    kp, vp = tails(k), tails(v)
    ks = k[tp:].reshape(nsb, nnew, N_KV_HEADS, HEAD_DIM)
    vs = v[tp:].reshape(nsb, nnew, N_KV_HEADS, HEAD_DIM)
    return (y_p.reshape(nb, seq, d), y_s.reshape(nsb, nnew, d), kp[None], vp[None], ks[None],
            vs[None], v_sgu.reshape(1, nsb, nnew, SGU_WIDTH))
```
